```python
import jax, jax.numpy as jnp
from jax import lax
import numpy as np

D_MODEL = 2048
BATCH = 8
SEQ = 2048
DEPTH = 1

CHUNK = 64
N_MEM = 256
MIX_WIDTH = D_MODEL
W_A = MIX_WIDTH // 2
G_A = 8
GA_DIM = W_A // G_A
SGU_BLOCK = 128
W_B = MIX_WIDTH - W_A
H_B = 8
DH_B = W_B // H_B
Q_BLOCK = 128
X_HEADS = 4
X_DH = D_MODEL // X_HEADS
D_FF = 5632
EPS = 1e-6

kernel_name = "hybrid_sgu_stickbreak_macaron_block"


def rmsnorm(x, g):
    xf = x.astype(jnp.float32)
    y = xf * lax.rsqrt(jnp.mean(xf * xf, axis=-1, keepdims=True) + EPS)
    return (y * g.astype(jnp.float32)).astype(x.dtype)


def layernorm(x, g, b):
    xf = x.astype(jnp.float32)
    mu = jnp.mean(xf, axis=-1, keepdims=True)
    var = jnp.mean(jnp.square(xf - mu), axis=-1, keepdims=True)
    y = (xf - mu) * lax.rsqrt(var + EPS)
    return (y * g.astype(jnp.float32) + b.astype(jnp.float32)).astype(x.dtype)


def swiglu(x, w_in, w_out):
    gate, up = jnp.split(x @ w_in, 2, axis=-1)
    return (jax.nn.silu(gate) * up) @ w_out


def spatial_gating_unit(za, ln_g, ln_b, w_s, b_s):
    bsz, seq, _ = za.shape
    u, v = jnp.split(za, 2, axis=-1)
    v = layernorm(v.reshape(bsz, seq, G_A, GA_DIM),
                  ln_g.reshape(G_A, GA_DIM), ln_b.reshape(G_A, GA_DIM))
    v = v.reshape(bsz, seq // SGU_BLOCK, SGU_BLOCK, G_A, GA_DIM)
    cidx = jnp.arange(SGU_BLOCK) // CHUNK
    mask = cidx[None, :] <= cidx[:, None]
    w = jnp.where(mask[None], w_s, jnp.zeros((), w_s.dtype))
    mixed = jnp.einsum('gts,bnsgc->bntgc', w, v) + b_s.T[:, :, None]
    return u * mixed.reshape(bsz, seq, W_A)


def stick_breaking_attention(q, k, v):
    seq = q.shape[1]
    q = q * (DH_B ** -0.5)
    outs = []
    for i in range(seq // Q_BLOCK):
        end = (i + 1) * Q_BLOCK
        q_blk = q[:, i * Q_BLOCK:end]
        k_pre, v_pre = k[:, :end], v[:, :end]
        z = jnp.einsum('bqhd,bkhd->bhqk', q_blk, k_pre).astype(jnp.float32)
        t_pos = i * Q_BLOCK + jnp.arange(Q_BLOCK)
        s_pos = jnp.arange(end)
        causal = s_pos[None, :] < t_pos[:, None]
        log_beta = jax.nn.log_sigmoid(z)
        log_1m = jnp.where(causal, jax.nn.log_sigmoid(-z), 0.0)
        rest = lax.cumsum(log_1m, axis=3, reverse=True) - log_1m
        a = jnp.where(causal, jnp.exp(log_beta + rest), 0.0)
        outs.append(jnp.einsum('bhqk,bkhd->bqhd', a.astype(v.dtype), v_pre))
    return jnp.concatenate(outs, axis=1)


def memory_cross_attention(xn, memn, w_cq, w_ckv, w_co):
    bsz, seq, _ = xn.shape
    q = (xn @ w_cq).reshape(bsz, seq, X_HEADS, X_DH) * (X_DH ** -0.5)
    k, v = jnp.split(memn @ w_ckv, 2, axis=-1)
    k = k.reshape(bsz, N_MEM, X_HEADS, X_DH)
    v = v.reshape(bsz, N_MEM, X_HEADS, X_DH)
    s = jnp.einsum('bshd,bmhd->bhsm', q, k).astype(jnp.float32)
    p = jax.nn.softmax(s, axis=-1).astype(v.dtype)
    o = jnp.einsum('bhsm,bmhd->bshd', p, v).reshape(bsz, seq, D_MODEL)
    return o @ w_co


def _fwd_setup_inputs(seed: int = 0) -> dict:
    key = jax.random.key(seed)
    ks = jax.random.split(key, 32)
    f32 = jnp.float32

    def w(k, shape, fan_in):
        return jax.random.normal(k, shape, f32) * (fan_in ** -0.5)

    def gain(k, shape):
        return 1.0 + 0.05 * jax.random.normal(k, shape, f32)

    L = DEPTH
    return {
        "x": jax.random.normal(ks[0], (BATCH, SEQ, D_MODEL), f32),
        "mem": jax.random.normal(ks[1], (BATCH, N_MEM, D_MODEL), f32),
        "ffn1_norm": gain(ks[2], (L, D_MODEL)),
        "ffn1_w_in": w(ks[3], (L, D_MODEL, 2 * D_FF), D_MODEL),
        "ffn1_w_out": w(ks[4], (L, D_FF, D_MODEL), D_FF),
        "mix_norm": gain(ks[5], (L, D_MODEL)),
        "w_mix_in": w(ks[6], (L, D_MODEL, 2 * W_A + 3 * W_B), D_MODEL),
        "ln_v_gain": gain(ks[7], (L, W_A)),
        "ln_v_bias": 0.02 * jax.random.normal(ks[8], (L, W_A), f32),
        "spatial_w": w(ks[9], (L, G_A, SGU_BLOCK, SGU_BLOCK), SGU_BLOCK),
        "spatial_b": 1.0 + 0.1 * jax.random.normal(ks[10], (L, G_A, SGU_BLOCK), f32),
        "gnorm_a": gain(ks[11], (L, W_A)),
        "gnorm_b": gain(ks[12], (L, W_B)),
        "w_mix_out": w(ks[13], (L, MIX_WIDTH, D_MODEL), MIX_WIDTH),
        "cross_norm": gain(ks[14], (L, D_MODEL)),
        "mem_norm": gain(ks[15], (L, D_MODEL)),
        "w_cq": w(ks[16], (L, D_MODEL, D_MODEL), D_MODEL),
        "w_ckv": w(ks[17], (L, D_MODEL, 2 * D_MODEL), D_MODEL),
        "w_co": w(ks[18], (L, D_MODEL, D_MODEL), D_MODEL),
        "ffn2_norm": gain(ks[19], (L, D_MODEL)),
        "ffn2_w_in": w(ks[20], (L, D_MODEL, 2 * D_FF), D_MODEL),
        "ffn2_w_out": w(ks[21], (L, D_FF, D_MODEL), D_FF),
        "final_norm": gain(ks[22], (D_MODEL,)),
    }


def _fwd_reference(x, mem, ffn1_norm, ffn1_w_in, ffn1_w_out, mix_norm, w_mix_in,
              ln_v_gain, ln_v_bias, spatial_w, spatial_b, gnorm_a, gnorm_b,
              w_mix_out, cross_norm, mem_norm, w_cq, w_ckv, w_co,
              ffn2_norm, ffn2_w_in, ffn2_w_out, final_norm):
    bsz, seq, _ = x.shape
    h = x
    for l in range(DEPTH):
        h = h + 0.5 * swiglu(rmsnorm(h, ffn1_norm[l]), ffn1_w_in[l], ffn1_w_out[l])

        z = rmsnorm(h, mix_norm[l]) @ w_mix_in[l]
        za = jax.nn.gelu(z[..., :2 * W_A])
        q, k, v = jnp.split(z[..., 2 * W_A:], 3, axis=-1)
        y_a = spatial_gating_unit(za, ln_v_gain[l], ln_v_bias[l], spatial_w[l], spatial_b[l])
        y_b = stick_breaking_attention(q.reshape(bsz, seq, H_B, DH_B),
                                       k.reshape(bsz, seq, H_B, DH_B),
                                       v.reshape(bsz, seq, H_B, DH_B)).reshape(bsz, seq, W_B)
        y = jnp.concatenate([rmsnorm(y_a, gnorm_a[l]), rmsnorm(y_b, gnorm_b[l])], axis=-1)
        h = h + y @ w_mix_out[l]

        h = h + memory_cross_attention(rmsnorm(h, cross_norm[l]), rmsnorm(mem, mem_norm[l]),
                                       w_cq[l], w_ckv[l], w_co[l])

        h = h + 0.5 * swiglu(rmsnorm(h, ffn2_norm[l]), ffn2_w_in[l], ffn2_w_out[l])
    return rmsnorm(h, final_norm)


import jax as _jax
import jax.numpy as _jnp

TWIN_FORMAT = 'train_step'
FWD_PARAMS = ['x', 'mem', 'ffn1_norm', 'ffn1_w_in', 'ffn1_w_out', 'mix_norm', 'w_mix_in', 'ln_v_gain', 'ln_v_bias', 'spatial_w', 'spatial_b', 'gnorm_a', 'gnorm_b', 'w_mix_out', 'cross_norm', 'mem_norm', 'w_cq', 'w_ckv', 'w_co', 'ffn2_norm', 'ffn2_w_in', 'ffn2_w_out', 'final_norm']
TWIN_WEIGHTS = ['ffn1_norm', 'ffn1_w_in', 'ffn1_w_out', 'mix_norm', 'w_mix_in', 'ln_v_gain', 'ln_v_bias', 'spatial_w', 'spatial_b', 'gnorm_a', 'gnorm_b', 'w_mix_out', 'cross_norm', 'mem_norm', 'w_cq', 'w_ckv', 'w_co', 'ffn2_norm', 'ffn2_w_in', 'ffn2_w_out', 'final_norm']
TWIN_DIFF_INPUT = 'x'
TWIN_INPUTS = ['x', 'mem', 'ffn1_norm', 'ffn1_w_in', 'ffn1_w_out', 'mix_norm', 'w_mix_in', 'ln_v_gain', 'ln_v_bias', 'spatial_w', 'spatial_b', 'gnorm_a', 'gnorm_b', 'w_mix_out', 'cross_norm', 'mem_norm', 'w_cq', 'w_ckv', 'w_co', 'ffn2_norm', 'ffn2_w_in', 'ffn2_w_out', 'final_norm', 'loss_target', 'm_ffn1_norm', 'm_ffn1_w_in', 'm_ffn1_w_out', 'm_mix_norm', 'm_w_mix_in', 'm_ln_v_gain', 'm_ln_v_bias', 'm_spatial_w', 'm_spatial_b', 'm_gnorm_a', 'm_gnorm_b', 'm_w_mix_out', 'm_cross_norm', 'm_mem_norm', 'm_w_cq', 'm_w_ckv', 'm_w_co', 'm_ffn2_norm', 'm_ffn2_w_in', 'm_ffn2_w_out', 'm_final_norm', 'v_ffn1_norm', 'v_ffn1_w_in', 'v_ffn1_w_out', 'v_mix_norm', 'v_w_mix_in', 'v_ln_v_gain', 'v_ln_v_bias', 'v_spatial_w', 'v_spatial_b', 'v_gnorm_a', 'v_gnorm_b', 'v_w_mix_out', 'v_cross_norm', 'v_mem_norm', 'v_w_cq', 'v_w_ckv', 'v_w_co', 'v_ffn2_norm', 'v_ffn2_w_in', 'v_ffn2_w_out', 'v_final_norm']
TWIN_OUTPUTS = ['loss', 'grad_x', 'grad_ffn1_norm', 'grad_ffn1_w_in', 'grad_ffn1_w_out', 'grad_mix_norm', 'grad_w_mix_in', 'grad_ln_v_gain', 'grad_ln_v_bias', 'grad_spatial_w', 'grad_spatial_b', 'grad_gnorm_a', 'grad_gnorm_b', 'grad_w_mix_out', 'grad_cross_norm', 'grad_mem_norm', 'grad_w_cq', 'grad_w_ckv', 'grad_w_co', 'grad_ffn2_norm', 'grad_ffn2_w_in', 'grad_ffn2_w_out', 'grad_final_norm', 'delta_ffn1_norm', 'delta_ffn1_w_in', 'delta_ffn1_w_out', 'delta_mix_norm', 'delta_w_mix_in', 'delta_ln_v_gain', 'delta_ln_v_bias', 'delta_spatial_w', 'delta_spatial_b', 'delta_gnorm_a', 'delta_gnorm_b', 'delta_w_mix_out', 'delta_cross_norm', 'delta_mem_norm', 'delta_w_cq', 'delta_w_ckv', 'delta_w_co', 'delta_ffn2_norm', 'delta_ffn2_w_in', 'delta_ffn2_w_out', 'delta_final_norm', 'new_m_ffn1_norm', 'new_m_ffn1_w_in', 'new_m_ffn1_w_out', 'new_m_mix_norm', 'new_m_w_mix_in', 'new_m_ln_v_gain', 'new_m_ln_v_bias', 'new_m_spatial_w', 'new_m_spatial_b', 'new_m_gnorm_a', 'new_m_gnorm_b', 'new_m_w_mix_out', 'new_m_cross_norm', 'new_m_mem_norm', 'new_m_w_cq', 'new_m_w_ckv', 'new_m_w_co', 'new_m_ffn2_norm', 'new_m_ffn2_w_in', 'new_m_ffn2_w_out', 'new_m_final_norm', 'new_v_ffn1_norm', 'new_v_ffn1_w_in', 'new_v_ffn1_w_out', 'new_v_mix_norm', 'new_v_w_mix_in', 'new_v_ln_v_gain', 'new_v_ln_v_bias', 'new_v_spatial_w', 'new_v_spatial_b', 'new_v_gnorm_a', 'new_v_gnorm_b', 'new_v_w_mix_out', 'new_v_cross_norm', 'new_v_mem_norm', 'new_v_w_cq', 'new_v_w_ckv', 'new_v_w_co', 'new_v_ffn2_norm', 'new_v_ffn2_w_in', 'new_v_ffn2_w_out', 'new_v_final_norm']
TWIN_LEAF_KINDS = {'loss': 'loss', 'grad_x': 'grad_x', 'grad_ffn1_norm': 'grad_w', 'grad_ffn1_w_in': 'grad_w', 'grad_ffn1_w_out': 'grad_w', 'grad_mix_norm': 'grad_w', 'grad_w_mix_in': 'grad_w', 'grad_ln_v_gain': 'grad_w', 'grad_ln_v_bias': 'grad_w', 'grad_spatial_w': 'grad_w', 'grad_spatial_b': 'grad_w', 'grad_gnorm_a': 'grad_w', 'grad_gnorm_b': 'grad_w', 'grad_w_mix_out': 'grad_w', 'grad_cross_norm': 'grad_w', 'grad_mem_norm': 'grad_w', 'grad_w_cq': 'grad_w', 'grad_w_ckv': 'grad_w', 'grad_w_co': 'grad_w', 'grad_ffn2_norm': 'grad_w', 'grad_ffn2_w_in': 'grad_w', 'grad_ffn2_w_out': 'grad_w', 'grad_final_norm': 'grad_w', 'delta_ffn1_norm': 'delta_w', 'delta_ffn1_w_in': 'delta_w', 'delta_ffn1_w_out': 'delta_w', 'delta_mix_norm': 'delta_w', 'delta_w_mix_in': 'delta_w', 'delta_ln_v_gain': 'delta_w', 'delta_ln_v_bias': 'delta_w', 'delta_spatial_w': 'delta_w', 'delta_spatial_b': 'delta_w', 'delta_gnorm_a': 'delta_w', 'delta_gnorm_b': 'delta_w', 'delta_w_mix_out': 'delta_w', 'delta_cross_norm': 'delta_w', 'delta_mem_norm': 'delta_w', 'delta_w_cq': 'delta_w', 'delta_w_ckv': 'delta_w', 'delta_w_co': 'delta_w', 'delta_ffn2_norm': 'delta_w', 'delta_ffn2_w_in': 'delta_w', 'delta_ffn2_w_out': 'delta_w', 'delta_final_norm': 'delta_w', 'new_m_ffn1_norm': 'new_m', 'new_m_ffn1_w_in': 'new_m', 'new_m_ffn1_w_out': 'new_m', 'new_m_mix_norm': 'new_m', 'new_m_w_mix_in': 'new_m', 'new_m_ln_v_gain': 'new_m', 'new_m_ln_v_bias': 'new_m', 'new_m_spatial_w': 'new_m', 'new_m_spatial_b': 'new_m', 'new_m_gnorm_a': 'new_m', 'new_m_gnorm_b': 'new_m', 'new_m_w_mix_out': 'new_m', 'new_m_cross_norm': 'new_m', 'new_m_mem_norm': 'new_m', 'new_m_w_cq': 'new_m', 'new_m_w_ckv': 'new_m', 'new_m_w_co': 'new_m', 'new_m_ffn2_norm': 'new_m', 'new_m_ffn2_w_in': 'new_m', 'new_m_ffn2_w_out': 'new_m', 'new_m_final_norm': 'new_m', 'new_v_ffn1_norm': 'new_v', 'new_v_ffn1_w_in': 'new_v', 'new_v_ffn1_w_out': 'new_v', 'new_v_mix_norm': 'new_v', 'new_v_w_mix_in': 'new_v', 'new_v_ln_v_gain': 'new_v', 'new_v_ln_v_bias': 'new_v', 'new_v_spatial_w': 'new_v', 'new_v_spatial_b': 'new_v', 'new_v_gnorm_a': 'new_v', 'new_v_gnorm_b': 'new_v', 'new_v_w_mix_out': 'new_v', 'new_v_cross_norm': 'new_v', 'new_v_mem_norm': 'new_v', 'new_v_w_cq': 'new_v', 'new_v_w_ckv': 'new_v', 'new_v_w_co': 'new_v', 'new_v_ffn2_norm': 'new_v', 'new_v_ffn2_w_in': 'new_v', 'new_v_ffn2_w_out': 'new_v', 'new_v_final_norm': 'new_v'}


def _forward(args):
    return _fwd_reference(*[args[k] for k in FWD_PARAMS])


def _output_shape():
    out = _jax.eval_shape(lambda: _forward(_fwd_setup_inputs(0)))
    return out.shape, out.dtype

N_MICROBATCH = 1
ADAM_LR = 0.001
ADAM_B1 = 0.9
ADAM_B2 = 0.999
ADAM_EPS = 1e-08
ADAM_WD = 0.01
ADAM_STEP = 10
PER_EXAMPLE_BATCH_AXIS = {'x': 0, 'mem': 0, 'loss_target': 0}
SHARED_INPUTS = []
_WEIGHT_DTYPES = {'ffn1_norm': _jnp.float32, 'ffn1_w_in': _jnp.float32, 'ffn1_w_out': _jnp.float32, 'mix_norm': _jnp.float32, 'w_mix_in': _jnp.float32, 'ln_v_gain': _jnp.float32, 'ln_v_bias': _jnp.float32, 'spatial_w': _jnp.float32, 'spatial_b': _jnp.float32, 'gnorm_a': _jnp.float32, 'gnorm_b': _jnp.float32, 'w_mix_out': _jnp.float32, 'cross_norm': _jnp.float32, 'mem_norm': _jnp.float32, 'w_cq': _jnp.float32, 'w_ckv': _jnp.float32, 'w_co': _jnp.float32, 'ffn2_norm': _jnp.float32, 'ffn2_w_in': _jnp.float32, 'ffn2_w_out': _jnp.float32, 'final_norm': _jnp.float32}
MOMENT_SCALE = {'ffn1_norm': 3.186004e-02, 'ffn1_w_in': 1.264202e-02, 'ffn1_w_out': 2.062212e-02, 'mix_norm': 5.656615e-02, 'w_mix_in': 3.485542e-02, 'ln_v_gain': 2.929891e-02, 'ln_v_bias': 2.831975e-02, 'spatial_w': 2.859675e-02, 'spatial_b': 3.225672e-02, 'gnorm_a': 4.957246e-02, 'gnorm_b': 4.421767e-02, 'w_mix_out': 4.705200e-02, 'cross_norm': 4.539220e-03, 'mem_norm': 6.800684e-03, 'w_cq': 4.519355e-03, 'w_ckv': 4.721448e-03, 'w_co': 4.967376e-03, 'ffn2_norm': 1.892714e-02, 'ffn2_w_in': 7.806716e-03, 'ffn2_w_out': 1.279465e-02, 'final_norm': 8.004039e+00}


def _to_microbatches(a, axis):
    t = _jnp.moveaxis(a, axis, 0)
    t = t.reshape((N_MICROBATCH, t.shape[0] // N_MICROBATCH) + t.shape[1:])
    return _jnp.moveaxis(t, 1, axis + 1)


def setup_inputs(seed: int = 0) -> dict:
    inp = _fwd_setup_inputs(seed)
    key = _jax.random.fold_in(_jax.random.key(seed), 7919)
    shape, _ = _output_shape()
    out = dict(inp)
    out["loss_target"] = _jax.random.normal(_jax.random.fold_in(key, 0), shape, _jnp.float32)
    for i, name in enumerate(TWIN_WEIGHTS):
        w = inp[name].astype(_jnp.float32)
        if MOMENT_SCALE is None:
            s = _jnp.sqrt(_jnp.mean(_jnp.square(w)) + 1e-30)
        else:
            s = MOMENT_SCALE[name]
        km, kv = _jax.random.split(_jax.random.fold_in(key, i + 1))
        out[name] = w
        out["m_" + name] = s * _jax.random.normal(km, w.shape, _jnp.float32)
        out["v_" + name] = (s * s) * _jax.random.uniform(kv, w.shape, _jnp.float32, 0.5, 1.5)
    if N_MICROBATCH > 1:
        for name, axis in PER_EXAMPLE_BATCH_AXIS.items():
            out[name] = _to_microbatches(out[name], axis)
    return {'x': out['x'], 'mem': out['mem'], 'ffn1_norm': out['ffn1_norm'], 'ffn1_w_in': out['ffn1_w_in'], 'ffn1_w_out': out['ffn1_w_out'], 'mix_norm': out['mix_norm'], 'w_mix_in': out['w_mix_in'], 'ln_v_gain': out['ln_v_gain'], 'ln_v_bias': out['ln_v_bias'], 'spatial_w': out['spatial_w'], 'spatial_b': out['spatial_b'], 'gnorm_a': out['gnorm_a'], 'gnorm_b': out['gnorm_b'], 'w_mix_out': out['w_mix_out'], 'cross_norm': out['cross_norm'], 'mem_norm': out['mem_norm'], 'w_cq': out['w_cq'], 'w_ckv': out['w_ckv'], 'w_co': out['w_co'], 'ffn2_norm': out['ffn2_norm'], 'ffn2_w_in': out['ffn2_w_in'], 'ffn2_w_out': out['ffn2_w_out'], 'final_norm': out['final_norm'], 'loss_target': out['loss_target'], 'm_ffn1_norm': out['m_ffn1_norm'], 'm_ffn1_w_in': out['m_ffn1_w_in'], 'm_ffn1_w_out': out['m_ffn1_w_out'], 'm_mix_norm': out['m_mix_norm'], 'm_w_mix_in': out['m_w_mix_in'], 'm_ln_v_gain': out['m_ln_v_gain'], 'm_ln_v_bias': out['m_ln_v_bias'], 'm_spatial_w': out['m_spatial_w'], 'm_spatial_b': out['m_spatial_b'], 'm_gnorm_a': out['m_gnorm_a'], 'm_gnorm_b': out['m_gnorm_b'], 'm_w_mix_out': out['m_w_mix_out'], 'm_cross_norm': out['m_cross_norm'], 'm_mem_norm': out['m_mem_norm'], 'm_w_cq': out['m_w_cq'], 'm_w_ckv': out['m_w_ckv'], 'm_w_co': out['m_w_co'], 'm_ffn2_norm': out['m_ffn2_norm'], 'm_ffn2_w_in': out['m_ffn2_w_in'], 'm_ffn2_w_out': out['m_ffn2_w_out'], 'm_final_norm': out['m_final_norm'], 'v_ffn1_norm': out['v_ffn1_norm'], 'v_ffn1_w_in': out['v_ffn1_w_in'], 'v_ffn1_w_out': out['v_ffn1_w_out'], 'v_mix_norm': out['v_mix_norm'], 'v_w_mix_in': out['v_w_mix_in'], 'v_ln_v_gain': out['v_ln_v_gain'], 'v_ln_v_bias': out['v_ln_v_bias'], 'v_spatial_w': out['v_spatial_w'], 'v_spatial_b': out['v_spatial_b'], 'v_gnorm_a': out['v_gnorm_a'], 'v_gnorm_b': out['v_gnorm_b'], 'v_w_mix_out': out['v_w_mix_out'], 'v_cross_norm': out['v_cross_norm'], 'v_mem_norm': out['v_mem_norm'], 'v_w_cq': out['v_w_cq'], 'v_w_ckv': out['v_w_ckv'], 'v_w_co': out['v_w_co'], 'v_ffn2_norm': out['v_ffn2_norm'], 'v_ffn2_w_in': out['v_ffn2_w_in'], 'v_ffn2_w_out': out['v_ffn2_w_out'], 'v_final_norm': out['v_final_norm']}


def _loss(weights, diff, rest, loss_target):
    with _jax.named_scope("forward"):
        args = {**rest, TWIN_DIFF_INPUT: diff, **{k: w.astype(_WEIGHT_DTYPES[k]) for k, w in weights.items()}}
        y = _forward(args)
    with _jax.named_scope("loss_head"):
        err = _jnp.square(y.astype(_jnp.float32) - loss_target)
        return 0.5 * _jnp.sum(_jnp.mean(err, axis=-1)) if err.ndim else 0.5 * err


def _adamw(w, g, m, v):
    m = ADAM_B1 * m + (1.0 - ADAM_B1) * g
    v = ADAM_B2 * v + (1.0 - ADAM_B2) * _jnp.square(g)
    m_hat = m / (1.0 - ADAM_B1 ** ADAM_STEP)
    v_hat = v / (1.0 - ADAM_B2 ** ADAM_STEP)
    delta = -ADAM_LR * (m_hat / (_jnp.sqrt(v_hat) + ADAM_EPS) + ADAM_WD * w)
    return delta, m, v


def reference(x, mem, ffn1_norm, ffn1_w_in, ffn1_w_out, mix_norm, w_mix_in, ln_v_gain, ln_v_bias, spatial_w, spatial_b, gnorm_a, gnorm_b, w_mix_out, cross_norm, mem_norm, w_cq, w_ckv, w_co, ffn2_norm, ffn2_w_in, ffn2_w_out, final_norm, loss_target, m_ffn1_norm, m_ffn1_w_in, m_ffn1_w_out, m_mix_norm, m_w_mix_in, m_ln_v_gain, m_ln_v_bias, m_spatial_w, m_spatial_b, m_gnorm_a, m_gnorm_b, m_w_mix_out, m_cross_norm, m_mem_norm, m_w_cq, m_w_ckv, m_w_co, m_ffn2_norm, m_ffn2_w_in, m_ffn2_w_out, m_final_norm, v_ffn1_norm, v_ffn1_w_in, v_ffn1_w_out, v_mix_norm, v_w_mix_in, v_ln_v_gain, v_ln_v_bias, v_spatial_w, v_spatial_b, v_gnorm_a, v_gnorm_b, v_w_mix_out, v_cross_norm, v_mem_norm, v_w_cq, v_w_ckv, v_w_co, v_ffn2_norm, v_ffn2_w_in, v_ffn2_w_out, v_final_norm):
    given = dict(x=x, mem=mem, ffn1_norm=ffn1_norm, ffn1_w_in=ffn1_w_in, ffn1_w_out=ffn1_w_out, mix_norm=mix_norm, w_mix_in=w_mix_in, ln_v_gain=ln_v_gain, ln_v_bias=ln_v_bias, spatial_w=spatial_w, spatial_b=spatial_b, gnorm_a=gnorm_a, gnorm_b=gnorm_b, w_mix_out=w_mix_out, cross_norm=cross_norm, mem_norm=mem_norm, w_cq=w_cq, w_ckv=w_ckv, w_co=w_co, ffn2_norm=ffn2_norm, ffn2_w_in=ffn2_w_in, ffn2_w_out=ffn2_w_out, final_norm=final_norm, loss_target=loss_target, m_ffn1_norm=m_ffn1_norm, m_ffn1_w_in=m_ffn1_w_in, m_ffn1_w_out=m_ffn1_w_out, m_mix_norm=m_mix_norm, m_w_mix_in=m_w_mix_in, m_ln_v_gain=m_ln_v_gain, m_ln_v_bias=m_ln_v_bias, m_spatial_w=m_spatial_w, m_spatial_b=m_spatial_b, m_gnorm_a=m_gnorm_a, m_gnorm_b=m_gnorm_b, m_w_mix_out=m_w_mix_out, m_cross_norm=m_cross_norm, m_mem_norm=m_mem_norm, m_w_cq=m_w_cq, m_w_ckv=m_w_ckv, m_w_co=m_w_co, m_ffn2_norm=m_ffn2_norm, m_ffn2_w_in=m_ffn2_w_in, m_ffn2_w_out=m_ffn2_w_out, m_final_norm=m_final_norm, v_ffn1_norm=v_ffn1_norm, v_ffn1_w_in=v_ffn1_w_in, v_ffn1_w_out=v_ffn1_w_out, v_mix_norm=v_mix_norm, v_w_mix_in=v_w_mix_in, v_ln_v_gain=v_ln_v_gain, v_ln_v_bias=v_ln_v_bias, v_spatial_w=v_spatial_w, v_spatial_b=v_spatial_b, v_gnorm_a=v_gnorm_a, v_gnorm_b=v_gnorm_b, v_w_mix_out=v_w_mix_out, v_cross_norm=v_cross_norm, v_mem_norm=v_mem_norm, v_w_cq=v_w_cq, v_w_ckv=v_w_ckv, v_w_co=v_w_co, v_ffn2_norm=v_ffn2_norm, v_ffn2_w_in=v_ffn2_w_in, v_ffn2_w_out=v_ffn2_w_out, v_final_norm=v_final_norm)
    weights = {n: given[n] for n in TWIN_WEIGHTS}
    shared = {n: given[n] for n in SHARED_INPUTS}
    per_example = {n: given[n] for n in ['x', 'mem']}
    grad_fn = _jax.value_and_grad(_loss, argnums=(0, 1))

    def one_microbatch(ex, loss_target):
        ex = dict(ex)
        diff = ex.pop(TWIN_DIFF_INPUT)
        return grad_fn(weights, diff, {**shared, **ex}, loss_target)

    if N_MICROBATCH == 1:
        loss, (grad_w, grad_x) = one_microbatch(per_example, given["loss_target"])
    else:
        def body(carry, xs):
            loss_sum, grad_sum = carry
            l_k, (gw_k, gx_k) = one_microbatch(xs[0], xs[1])
            with _jax.named_scope("update"):
                return (loss_sum + l_k, _jax.tree.map(_jnp.add, grad_sum, gw_k)), gx_k

        init = (_jnp.zeros((), _jnp.float32), _jax.tree.map(_jnp.zeros_like, weights))
        (loss, grad_w), grad_x = _jax.lax.scan(body, init, (per_example, given["loss_target"]))
    with _jax.named_scope("update"):
        delta_w, new_m, new_v = {}, {}, {}
        for n in TWIN_WEIGHTS:
            delta_w[n], new_m[n], new_v[n] = _adamw(weights[n], grad_w[n], given["m_" + n], given["v_" + n])
    return (loss, grad_x, *[grad_w[n] for n in TWIN_WEIGHTS], *[delta_w[n] for n in TWIN_WEIGHTS],
            *[new_m[n] for n in TWIN_WEIGHTS], *[new_v[n] for n in TWIN_WEIGHTS])
```

```python
import math

import jax
import jax.numpy as jnp
from jax import lax
from jax.experimental import pallas as pl
from jax.experimental.pallas import tpu as pltpu

F32 = jnp.float32
BF16 = jnp.bfloat16

N_DEV = 8
D_MODEL = 2048
D_FF = 5632
W_A = 1024
G_A = 8
GA_DIM = 128
SGU_BLOCK = 128
CHUNK = 64
H_B = 8
DH_B = 128
Q_BLOCK = 128
X_HEADS = 4
X_DH = 512
N_MEM = 256
EPS = 1e-6

ADAM_LR = 0.001
ADAM_B1 = 0.9
ADAM_B2 = 0.999
ADAM_EPS = 1e-08
ADAM_WD = 0.01
ADAM_STEP = 10

VMEM_LIMIT = 56 * 2**20
ROW_TILE = 256

MESH = pl.DeviceIdType.MESH
ANY = pl.BlockSpec(memory_space=pl.ANY)

_NT = (((1,), (1,)), ((), ()))
_TN = (((0,), (0,)), ((), ()))


def _params(*sem):
    return pltpu.CompilerParams(dimension_semantics=sem, vmem_limit_bytes=VMEM_LIMIT)


def _zeros(ref):
    return jnp.zeros(ref.shape, ref.dtype)


def _dot(a, b):
    return jnp.dot(a, b, preferred_element_type=F32)


def _dot_nt(a, b):
    return lax.dot_general(a, b, _NT, preferred_element_type=F32)


def _dot_tn(a, b):
    return lax.dot_general(a, b, _TN, preferred_element_type=F32)


def mm_nn_g(a, bg, out_dtype, name, tm=512):
    M, K = a.shape
    G, _, n = bg.shape
    tm = min(tm, M)

    def body(a_ref, b_ref, o_ref):
        o_ref[...] = _dot(a_ref[...], b_ref[...]).astype(o_ref.dtype)

    return pl.pallas_call(
        body, name=name, grid=(G, M // tm),
        in_specs=[pl.BlockSpec((tm, K), lambda g, m: (m, 0)),
                  pl.BlockSpec((None, K, n), lambda g, m: (g, 0, 0))],
        out_specs=pl.BlockSpec((tm, n), lambda g, m: (m, g)),
        out_shape=jax.ShapeDtypeStruct((M, G * n), out_dtype),
        compiler_params=_params("parallel", "parallel"),
    )(a, bg)


def mm_nn(a, b, out_dtype, name, tm=512, tn=1024, scale=1.0, res=None):
    M, K = a.shape
    _, N = b.shape
    tm, tn = min(tm, M), min(tn, N)

    def body(*refs):
        if res is None:
            a_ref, b_ref, o_ref = refs
            acc = _dot(a_ref[...], b_ref[...])
            o_ref[...] = (acc * scale if scale != 1.0 else acc).astype(o_ref.dtype)
        else:
            a_ref, b_ref, r_ref, o_ref = refs
            o_ref[...] = (r_ref[...] + scale * _dot(a_ref[...], b_ref[...])).astype(o_ref.dtype)

    in_specs = [pl.BlockSpec((tm, K), lambda n, m: (m, 0)),
                pl.BlockSpec((K, tn), lambda n, m: (0, n))]
    args = [a, b]
    if res is not None:
        in_specs.append(pl.BlockSpec((tm, tn), lambda n, m: (m, n)))
        args.append(res)
    return pl.pallas_call(
        body, name=name, grid=(N // tn, M // tm),
        in_specs=in_specs,
        out_specs=pl.BlockSpec((tm, tn), lambda n, m: (m, n)),
        out_shape=jax.ShapeDtypeStruct((M, N), out_dtype),
        compiler_params=_params("parallel", "parallel"),
    )(*args)


def mm_nt_g(dy, bg, name, tm=512):
    M, _ = dy.shape
    G, K, n = bg.shape
    tm = min(tm, M)

    def body(dy_ref, b_ref, o_ref):
        part = _dot_nt(dy_ref[...], b_ref[...])

        @pl.when(pl.program_id(1) == 0)
        def _():
            o_ref[...] = part

        @pl.when(pl.program_id(1) > 0)
        def _():
            o_ref[...] += part

    return pl.pallas_call(
        body, name=name, grid=(M // tm, G),
        in_specs=[pl.BlockSpec((tm, n), lambda m, g: (m, g)),
                  pl.BlockSpec((None, K, n), lambda m, g: (g, 0, 0))],
        out_specs=pl.BlockSpec((tm, K), lambda m, g: (m, 0)),
        out_shape=jax.ShapeDtypeStruct((M, K), F32),
        compiler_params=_params("parallel", "arbitrary"),
    )(dy, bg)


def mm_nt(dy, b, out_dtype, name, tk=512):
    M, N = dy.shape
    K, _ = b.shape

    def body(dy_ref, b_ref, o_ref):
        o_ref[...] = _dot_nt(dy_ref[...], b_ref[...]).astype(o_ref.dtype)

    return pl.pallas_call(
        body, name=name, grid=(K // tk,),
        in_specs=[pl.BlockSpec((M, N), lambda k: (0, 0)),
                  pl.BlockSpec((tk, N), lambda k: (k, 0))],
        out_specs=pl.BlockSpec((M, tk), lambda k: (0, k)),
        out_shape=jax.ShapeDtypeStruct((M, K), out_dtype),
        compiler_params=_params("parallel"),
    )(dy, b)


def mm_tn_g(x, dy, G, name, tk=512):
    M, K = x.shape
    n = dy.shape[1] // G

    def body(x_ref, dy_ref, o_ref):
        o_ref[...] = _dot_tn(x_ref[...], dy_ref[...]).astype(o_ref.dtype)

    return pl.pallas_call(
        body, name=name, grid=(G, K // tk),
        in_specs=[pl.BlockSpec((M, tk), lambda g, k: (0, k)),
                  pl.BlockSpec((M, n), lambda g, k: (0, g))],
        out_specs=pl.BlockSpec((None, tk, n), lambda g, k: (g, k, 0)),
        out_shape=jax.ShapeDtypeStruct((G, K, n), BF16),
        compiler_params=_params("parallel", "parallel"),
    )(x, dy)


def mm_tn(x, dy, name, tk=512):
    M, K = x.shape
    _, N = dy.shape

    def body(x_ref, dy_ref, o_ref):
        o_ref[...] = _dot_tn(x_ref[...], dy_ref[...]).astype(o_ref.dtype)

    return pl.pallas_call(
        body, name=name, grid=(K // tk,),
        in_specs=[pl.BlockSpec((M, tk), lambda k: (0, k)),
                  pl.BlockSpec((M, N), lambda k: (0, 0))],
        out_specs=pl.BlockSpec((tk, N), lambda k: (k, 0)),
        out_shape=jax.ShapeDtypeStruct((K, N), BF16),
        compiler_params=_params("parallel"),
    )(x, dy)


def _rstd(x):
    return lax.rsqrt(jnp.mean(x * x, axis=-1, keepdims=True) + EPS)


def _rms_bwd(dn, xhat, r, g):
    dxhat = dn * g
    return r * (dxhat - xhat * jnp.mean(dxhat * xhat, axis=-1, keepdims=True))


def _row_spec(tr, width, col=0):
    return pl.BlockSpec((tr, width), lambda i: (i, col))


def _vec_spec(width):
    return pl.BlockSpec((1, width), lambda i: (0, 0))


def rms_fwd(x, g, name):
    M, D = x.shape
    tr = min(ROW_TILE, M)

    def body(x_ref, g_ref, o_ref):
        xv = x_ref[...]
        o_ref[...] = (xv * _rstd(xv) * g_ref[...]).astype(o_ref.dtype)

    return pl.pallas_call(
        body, name=name, grid=(M // tr,),
        in_specs=[_row_spec(tr, D), _vec_spec(D)],
        out_specs=_row_spec(tr, D),
        out_shape=jax.ShapeDtypeStruct((M, D), BF16),
        compiler_params=_params("parallel"),
    )(x, g)


def rms_bwd(dn, h, g, dres, copy_scale, name):
    M, D = h.shape
    tr = min(ROW_TILE, M)
    has_res = dres is not None

    def body(*refs):
        if has_res:
            dn_ref, h_ref, g_ref, dres_ref, dh_ref, dhb_ref, dg_ref = refs
        else:
            dn_ref, h_ref, g_ref, dh_ref, dhb_ref, dg_ref = refs
        hv = h_ref[...]
        r = _rstd(hv)
        xhat = hv * r
        dn = dn_ref[...]
        part = jnp.sum(dn * xhat, axis=0, keepdims=True)

        @pl.when(pl.program_id(0) == 0)
        def _():
            dg_ref[...] = part

        @pl.when(pl.program_id(0) > 0)
        def _():
            dg_ref[...] += part

        dh = _rms_bwd(dn, xhat, r, g_ref[...])
        if has_res:
            dh = dh + dres_ref[...]
        dh_ref[...] = dh
        dhb_ref[...] = (dh * copy_scale if copy_scale != 1.0 else dh).astype(BF16)

    in_specs = [_row_spec(tr, D), _row_spec(tr, D), _vec_spec(D)]
    args = [dn, h, g]
    if has_res:
        in_specs.append(_row_spec(tr, D))
        args.append(dres)
    return pl.pallas_call(
        body, name=name, grid=(M // tr,),
        in_specs=in_specs,
        out_specs=[_row_spec(tr, D), _row_spec(tr, D), _vec_spec(D)],
        out_shape=[jax.ShapeDtypeStruct((M, D), F32), jax.ShapeDtypeStruct((M, D), BF16),
                   jax.ShapeDtypeStruct((1, D), F32)],
        compiler_params=_params("arbitrary"),
    )(*args)


def _sigmoid(x):
    return 1.0 / (1.0 + jnp.exp(-x))


def swiglu_fwd(a, name):
    M, F2 = a.shape
    F = F2 // 2
    tr = min(ROW_TILE, M)

    def body(g_ref, u_ref, o_ref):
        gt = g_ref[...].astype(F32)
        o_ref[...] = (gt * _sigmoid(gt) * u_ref[...].astype(F32)).astype(o_ref.dtype)

    return pl.pallas_call(
        body, name=name, grid=(M // tr,),
        in_specs=[_row_spec(tr, F, 0), _row_spec(tr, F, 1)],
        out_specs=_row_spec(tr, F),
        out_shape=jax.ShapeDtypeStruct((M, F), BF16),
        compiler_params=_params("parallel"),
    )(a, a)


def swiglu_bwd(a, dh, name):
    M, F2 = a.shape
    F = F2 // 2
    tr = min(ROW_TILE, M)

    def body(g_ref, u_ref, dh_ref, da_ref):
        gt = g_ref[...].astype(F32)
        up = u_ref[...].astype(F32)
        dh = dh_ref[...]
        sg = _sigmoid(gt)
        da_ref[:, :F] = (dh * up * (sg * (1.0 + gt * (1.0 - sg)))).astype(BF16)
        da_ref[:, F:] = (dh * (gt * sg)).astype(BF16)

    return pl.pallas_call(
        body, name=name, grid=(M // tr,),
        in_specs=[_row_spec(tr, F, 0), _row_spec(tr, F, 1), _row_spec(tr, F)],
        out_specs=_row_spec(tr, F2),
        out_shape=jax.ShapeDtypeStruct((M, F2), BF16),
        compiler_params=_params("parallel"),
    )(a, a, dh)


def rmscat_fwd(ya, yb, ga, gb, name):
    M, W = ya.shape
    tr = min(ROW_TILE, M)

    def body(ya_ref, yb_ref, ga_ref, gb_ref, o_ref):
        a = ya_ref[...]
        b = yb_ref[...]
        o_ref[:, :W] = (a * _rstd(a) * ga_ref[...]).astype(BF16)
        o_ref[:, W:] = (b * _rstd(b) * gb_ref[...]).astype(BF16)

    return pl.pallas_call(
        body, name=name, grid=(M // tr,),
        in_specs=[_row_spec(tr, W), _row_spec(tr, W), _vec_spec(W), _vec_spec(W)],
        out_specs=_row_spec(tr, 2 * W),
        out_shape=jax.ShapeDtypeStruct((M, 2 * W), BF16),
        compiler_params=_params("parallel"),
    )(ya, yb, ga, gb)


def rmscat_bwd(dycat, ya, yb, ga, gb, name):
    M, W = ya.shape
    tr = min(ROW_TILE, M)

    def body(dc_ref, ya_ref, yb_ref, ga_ref, gb_ref, dya_ref, dyb_ref, dga_ref, dgb_ref):
        first = pl.program_id(0) == 0
        for y_ref, g_ref, dy_ref, dg_ref, lo in ((ya_ref, ga_ref, dya_ref, dga_ref, 0),
                                                 (yb_ref, gb_ref, dyb_ref, dgb_ref, W)):
            yv = y_ref[...]
            r = _rstd(yv)
            xhat = yv * r
            dn = dc_ref[:, lo:lo + W]
            part = jnp.sum(dn * xhat, axis=0, keepdims=True)

            @pl.when(first)
            def _():
                dg_ref[...] = part

            @pl.when(jnp.logical_not(first))
            def _():
                dg_ref[...] += part

            dy_ref[...] = _rms_bwd(dn, xhat, r, g_ref[...])

    return pl.pallas_call(
        body, name=name, grid=(M // tr,),
        in_specs=[_row_spec(tr, 2 * W), _row_spec(tr, W), _row_spec(tr, W), _vec_spec(W), _vec_spec(W)],
        out_specs=[_row_spec(tr, W), _row_spec(tr, W), _vec_spec(W), _vec_spec(W)],
        out_shape=[jax.ShapeDtypeStruct((M, W), F32), jax.ShapeDtypeStruct((M, W), F32),
                   jax.ShapeDtypeStruct((1, W), F32), jax.ShapeDtypeStruct((1, W), F32)],
        compiler_params=_params("arbitrary"),
    )(dycat, ya, yb, ga, gb)


def loss_head(h, target, g, name):
    M, D = h.shape
    tr = min(ROW_TILE, M)

    def body(h_ref, t_ref, g_ref, loss_ref, dh_ref, dhb_ref, dg_ref):
        hv = h_ref[...]
        gv = g_ref[...]
        r = _rstd(hv)
        xhat = hv * r
        err = xhat * gv - t_ref[...]
        lsum = jnp.sum(jnp.sum(err * err, axis=1, keepdims=True), axis=0, keepdims=True) * (0.5 / D)
        dy = err * (1.0 / D)
        part = jnp.sum(dy * xhat, axis=0, keepdims=True)

        @pl.when(pl.program_id(0) == 0)
        def _():
            dg_ref[...] = part
            loss_ref[...] = _zeros(loss_ref) + lsum

        @pl.when(pl.program_id(0) > 0)
        def _():
            dg_ref[...] += part
            loss_ref[...] += lsum

        dh = _rms_bwd(dy, xhat, r, gv)
        dh_ref[...] = dh
        dhb_ref[...] = (0.5 * dh).astype(BF16)

    return pl.pallas_call(
        body, name=name, grid=(M // tr,),
        in_specs=[_row_spec(tr, D), _row_spec(tr, D), _vec_spec(D)],
        out_specs=[pl.BlockSpec((8, 128), lambda i: (0, 0)), _row_spec(tr, D), _row_spec(tr, D), _vec_spec(D)],
        out_shape=[jax.ShapeDtypeStruct((8, 128), F32), jax.ShapeDtypeStruct((M, D), F32),
                   jax.ShapeDtypeStruct((M, D), BF16), jax.ShapeDtypeStruct((1, D), F32)],
        compiler_params=_params("arbitrary"),
    )(h, target, g)


_GELU_C = math.sqrt(2.0 / math.pi)


def _gelu(x):
    return 0.5 * x * (1.0 + jnp.tanh(_GELU_C * (x + 0.044715 * (x * x * x))))


def _gelu_grad(x):
    t = jnp.tanh(_GELU_C * (x + 0.044715 * (x * x * x)))
    return 0.5 * (1.0 + t) + 0.5 * x * (1.0 - t * t) * (_GELU_C * (1.0 + 3.0 * 0.044715 * (x * x)))


def _sgu_mask():
    t = lax.broadcasted_iota(jnp.int32, (SGU_BLOCK, SGU_BLOCK), 0) // CHUNK
    s = lax.broadcasted_iota(jnp.int32, (SGU_BLOCK, SGU_BLOCK), 1) // CHUNK
    return s <= t


def _layernorm_stats(v):
    mu = jnp.mean(v, axis=-1, keepdims=True)
    cen = v - mu
    rstd = lax.rsqrt(jnp.mean(cen * cen, axis=-1, keepdims=True) + EPS)
    return cen * rstd, rstd


def sgu_fwd(z, ln_g, ln_b, w_s, b_t, name):
    S = z.shape[0]

    def body(zu_ref, zv_ref, lg_ref, lb_ref, w_ref, bt_ref, o_ref):
        mask = _sgu_mask()
        for g in range(G_A):
            cols = slice(g * GA_DIM, (g + 1) * GA_DIM)
            u = _gelu(zu_ref[:, cols])
            vhat, _ = _layernorm_stats(_gelu(zv_ref[:, cols]))
            vln = vhat * lg_ref[:, cols] + lb_ref[:, cols]
            w = jnp.where(mask, w_ref[g], 0.0).astype(BF16)
            mixed = _dot(w, vln.astype(BF16)) + bt_ref[:, g:g + 1]
            o_ref[:, cols] = u * mixed

    return pl.pallas_call(
        body, name=name, grid=(S // SGU_BLOCK,),
        in_specs=[_row_spec(SGU_BLOCK, W_A, 0), _row_spec(SGU_BLOCK, W_A, 1), _vec_spec(W_A), _vec_spec(W_A),
                  pl.BlockSpec((G_A, SGU_BLOCK, SGU_BLOCK), lambda i: (0, 0, 0)),
                  pl.BlockSpec((SGU_BLOCK, G_A), lambda i: (0, 0))],
        out_specs=_row_spec(SGU_BLOCK, W_A),
        out_shape=jax.ShapeDtypeStruct((S, W_A), F32),
        compiler_params=_params("parallel"),
    )(z, z, ln_g, ln_b, w_s, b_t)


def sgu_bwd(z, dya, ln_g, ln_b, w_s, b_t, name):
    S = z.shape[0]
    nblk = S // SGU_BLOCK

    def body(zu_ref, zv_ref, dy_ref, lg_ref, lb_ref, w_ref, bt_ref,
             dz_ref, dlg_ref, dlb_ref, dw_ref, db_ref, dmix_acc):
        step = pl.program_id(0)
        mask = _sgu_mask()

        @pl.when(step == 0)
        def _():
            dlg_ref[...] = _zeros(dlg_ref)
            dlb_ref[...] = _zeros(dlb_ref)
            dw_ref[...] = _zeros(dw_ref)
            dmix_acc[...] = _zeros(dmix_acc)

        for g in range(G_A):
            cols = slice(g * GA_DIM, (g + 1) * GA_DIM)
            zu = zu_ref[:, cols]
            zv = zv_ref[:, cols]
            u = _gelu(zu)
            vhat, rstd = _layernorm_stats(_gelu(zv))
            lg = lg_ref[:, cols]
            vln = (vhat * lg + lb_ref[:, cols]).astype(BF16)
            w = jnp.where(mask, w_ref[g], 0.0)
            mixed = _dot(w.astype(BF16), vln) + bt_ref[:, g:g + 1]
            dy = dy_ref[:, cols]
            du = dy * mixed
            dmixed = dy * u
            dmixed_b = dmixed.astype(BF16)
            dmix_acc[g] += dmixed
            dw_ref[g] += jnp.where(mask, _dot_nt(dmixed_b, vln), 0.0)
            dvln = _dot(w.T.astype(BF16), dmixed_b)
            dlb_ref[:, cols] += jnp.sum(dvln, axis=0, keepdims=True)
            dlg_ref[:, cols] += jnp.sum(dvln * vhat, axis=0, keepdims=True)
            dvhat = dvln * lg
            dv = rstd * (dvhat - jnp.mean(dvhat, axis=-1, keepdims=True)
                         - vhat * jnp.mean(dvhat * vhat, axis=-1, keepdims=True))
            dz_ref[:, cols] = (du * _gelu_grad(zu)).astype(BF16)
            dz_ref[:, W_A + g * GA_DIM:W_A + (g + 1) * GA_DIM] = (dv * _gelu_grad(zv)).astype(BF16)

        @pl.when(step == nblk - 1)
        def _():
            for g in range(G_A):
                db_ref[g] = jnp.sum(dmix_acc[g], axis=1, keepdims=True)

    whole3 = lambda shape: pl.BlockSpec(shape, lambda i: (0, 0, 0))
    return pl.pallas_call(
        body, name=name, grid=(nblk,),
        in_specs=[_row_spec(SGU_BLOCK, W_A, 0), _row_spec(SGU_BLOCK, W_A, 1), _row_spec(SGU_BLOCK, W_A),
                  _vec_spec(W_A), _vec_spec(W_A), whole3((G_A, SGU_BLOCK, SGU_BLOCK)),
                  pl.BlockSpec((SGU_BLOCK, G_A), lambda i: (0, 0))],
        out_specs=[_row_spec(SGU_BLOCK, 2 * W_A), _vec_spec(W_A), _vec_spec(W_A),
                   whole3((G_A, SGU_BLOCK, SGU_BLOCK)), whole3((G_A, SGU_BLOCK, 1))],
        out_shape=[jax.ShapeDtypeStruct((S, 2 * W_A), BF16), jax.ShapeDtypeStruct((1, W_A), F32),
                   jax.ShapeDtypeStruct((1, W_A), F32), jax.ShapeDtypeStruct((G_A, SGU_BLOCK, SGU_BLOCK), F32),
                   jax.ShapeDtypeStruct((G_A, SGU_BLOCK, 1), F32)],
        scratch_shapes=[pltpu.VMEM((G_A, SGU_BLOCK, SGU_BLOCK), F32)],
        compiler_params=_params("arbitrary"),
    )(z, z, dya, ln_g, ln_b, w_s, b_t)


def _log_sigmoid(z):
    return jnp.minimum(z, 0.0) - jnp.log(1.0 + jnp.exp(-jnp.abs(z)))


def _suffix_sum(x, upper):
    hi = x.astype(BF16)
    rem = x - hi.astype(F32)
    mid = rem.astype(BF16)
    lo = (rem - mid.astype(F32)).astype(BF16)
    return _dot(hi, upper) + _dot(mid, upper) + _dot(lo, upper)


def _sb_consts():
    row = lax.broadcasted_iota(jnp.int32, (Q_BLOCK, Q_BLOCK), 0)
    col = lax.broadcasted_iota(jnp.int32, (Q_BLOCK, Q_BLOCK), 1)
    upper = (row > col).astype(BF16)
    causal = col < row
    return upper, causal


def _blk(ref, j):
    return ref[pl.ds(pl.multiple_of(j * Q_BLOCK, Q_BLOCK), Q_BLOCK), :]


def _sb_head_spec(S, part, h_cols):
    base = (2 * W_A + part * (H_B * DH_B)) // DH_B
    return pl.BlockSpec((S, DH_B), lambda h: (0, base + h))


def sb_fwd(z, name):
    S = z.shape[0]
    nq = S // Q_BLOCK
    scale = DH_B ** -0.5

    def body(q_ref, k_ref, v_ref, o_ref, qs, kb, vb):
        qs[...] = (q_ref[...] * scale).astype(BF16)
        kb[...] = k_ref[...].astype(BF16)
        vb[...] = v_ref[...].astype(BF16)
        upper, causal = _sb_consts()

        def tile(q_i, j, carry, acc, diag):
            zz = _dot_nt(q_i, _blk(kb, j))
            lb = _log_sigmoid(zz)
            l1m = lb - zz
            if diag:
                l1m = jnp.where(causal, l1m, 0.0)
            a = jnp.exp(lb + _suffix_sum(l1m, upper) + carry)
            if diag:
                a = jnp.where(causal, a, 0.0)
            acc = acc + _dot(a.astype(BF16), _blk(vb, j))
            return carry + jnp.sum(l1m, axis=1, keepdims=True), acc

        def qblock(i, _):
            q_i = _blk(qs, i)
            st = tile(q_i, i, jnp.zeros((Q_BLOCK, 1), F32), jnp.zeros((Q_BLOCK, DH_B), F32), True)
            st = lax.fori_loop(0, i, lambda jj, s: tile(q_i, i - 1 - jj, s[0], s[1], False), st)
            o_ref[pl.ds(pl.multiple_of(i * Q_BLOCK, Q_BLOCK), Q_BLOCK), :] = st[1]
            return 0

        lax.fori_loop(0, nq, qblock, 0)

    return pl.pallas_call(
        body, name=name, grid=(H_B,),
        in_specs=[_sb_head_spec(S, 0, 0), _sb_head_spec(S, 1, 0), _sb_head_spec(S, 2, 0)],
        out_specs=pl.BlockSpec((S, DH_B), lambda h: (0, h)),
        out_shape=jax.ShapeDtypeStruct((S, H_B * DH_B), F32),
        scratch_shapes=[pltpu.VMEM((S, DH_B), BF16)] * 3,
        compiler_params=_params("parallel"),
    )(z, z, z)


def sb_bwd(z, out, dout, name):
    S = z.shape[0]
    nq = S // Q_BLOCK
    scale = DH_B ** -0.5

    def body(q_ref, k_ref, v_ref, o_ref, do_ref, dq_ref, dk_ref, dv_ref, qs, kb, vb, dob, dk_acc, dv_acc):
        qs[...] = (q_ref[...] * scale).astype(BF16)
        kb[...] = k_ref[...].astype(BF16)
        vb[...] = v_ref[...].astype(BF16)
        dob[...] = do_ref[...].astype(BF16)
        dk_acc[...] = _zeros(dk_acc)
        dv_acc[...] = _zeros(dv_acc)
        upper, causal = _sb_consts()

        def tile(q_i, do_i, delta, j, c_l1m, c_g, dq, diag):
            rows = pl.ds(pl.multiple_of(j * Q_BLOCK, Q_BLOCK), Q_BLOCK)
            k_j = kb[rows, :]
            zz = _dot_nt(q_i, k_j)
            lb = _log_sigmoid(zz)
            l1m = lb - zz
            if diag:
                l1m = jnp.where(causal, l1m, 0.0)
            a = jnp.exp(lb + _suffix_sum(l1m, upper) + c_l1m)
            if diag:
                a = jnp.where(causal, a, 0.0)
            a_b = a.astype(BF16)
            dv_acc[rows, :] += _dot_tn(a_b, do_i)
            gmat = a_b.astype(F32) * _dot_nt(do_i, vb[rows, :])
            before = delta - gmat - (_suffix_sum(gmat, upper) + c_g)
            sig = jnp.exp(lb)
            dz = gmat * (1.0 - sig) - sig * before
            if diag:
                dz = jnp.where(causal, dz, 0.0)
            dz_b = dz.astype(BF16)
            dk_acc[rows, :] += _dot_tn(dz_b, q_i)
            return (c_l1m + jnp.sum(l1m, axis=1, keepdims=True), c_g + jnp.sum(gmat, axis=1, keepdims=True),
                    dq + _dot(dz_b, k_j))

        def qblock(i, _):
            rows = pl.ds(pl.multiple_of(i * Q_BLOCK, Q_BLOCK), Q_BLOCK)
            q_i = qs[rows, :]
            do_i = dob[rows, :]
            delta = jnp.sum(do_i.astype(F32) * o_ref[rows, :], axis=1, keepdims=True)
            zero = jnp.zeros((Q_BLOCK, 1), F32)
            st = tile(q_i, do_i, delta, i, zero, zero, jnp.zeros((Q_BLOCK, DH_B), F32), True)
            st = lax.fori_loop(0, i, lambda jj, s: tile(q_i, do_i, delta, i - 1 - jj, s[0], s[1], s[2], False), st)
            dq_ref[rows, :] = (st[2] * scale).astype(BF16)
            return 0

        lax.fori_loop(0, nq, qblock, 0)
        dk_ref[...] = dk_acc[...].astype(BF16)
        dv_ref[...] = dv_acc[...].astype(BF16)

    head = pl.BlockSpec((S, DH_B), lambda h: (0, h))
    out_sds = jax.ShapeDtypeStruct((S, H_B * DH_B), BF16)
    return pl.pallas_call(
        body, name=name, grid=(H_B,),
        in_specs=[_sb_head_spec(S, 0, 0), _sb_head_spec(S, 1, 0), _sb_head_spec(S, 2, 0), head, head],
        out_specs=[head, head, head],
        out_shape=[out_sds, out_sds, out_sds],
        scratch_shapes=[pltpu.VMEM((S, DH_B), BF16)] * 4 + [pltpu.VMEM((S, DH_B), F32)] * 2,
        compiler_params=_params("parallel"),
    )(z, z, z, out, dout)


def _softmax(s):
    e = jnp.exp(s - jnp.max(s, axis=-1, keepdims=True))
    return e / jnp.sum(e, axis=-1, keepdims=True)


def xattn_fwd(qc, kv, name):
    S, D = qc.shape
    tr = min(ROW_TILE, S)

    def body(q_ref, kv_ref, o_ref):
        for h in range(X_HEADS):
            cols = slice(h * X_DH, (h + 1) * X_DH)
            p = _softmax(_dot_nt(q_ref[:, cols], kv_ref[:, cols]))
            o_ref[:, cols] = _dot(p.astype(BF16), kv_ref[:, D + h * X_DH:D + (h + 1) * X_DH]).astype(BF16)

    return pl.pallas_call(
        body, name=name, grid=(S // tr,),
        in_specs=[_row_spec(tr, D), pl.BlockSpec((N_MEM, 2 * D), lambda i: (0, 0))],
        out_specs=_row_spec(tr, D),
        out_shape=jax.ShapeDtypeStruct((S, D), BF16),
        compiler_params=_params("parallel"),
    )(qc, kv)


def xattn_bwd(qc, kv, do, name):
    S, D = qc.shape
    tr = min(ROW_TILE, S)
    nstep = S // tr
    scale = X_DH ** -0.5

    def body(q_ref, kv_ref, do_ref, dq_ref, dkv_ref, acc):
        step = pl.program_id(0)

        @pl.when(step == 0)
        def _():
            acc[...] = _zeros(acc)

        for h in range(X_HEADS):
            cols = slice(h * X_DH, (h + 1) * X_DH)
            vcols = slice(D + h * X_DH, D + (h + 1) * X_DH)
            q = q_ref[:, cols]
            k = kv_ref[:, cols]
            do_h = do_ref[:, cols]
            p = _softmax(_dot_nt(q, k))
            dp = _dot_nt(do_h, kv_ref[:, vcols])
            acc[:, vcols] += _dot_tn(p.astype(BF16), do_h)
            ds = (p * (dp - jnp.sum(p * dp, axis=-1, keepdims=True))).astype(BF16)
            dq_ref[:, cols] = (_dot(ds, k) * scale).astype(BF16)
            acc[:, cols] += _dot_tn(ds, q)

        @pl.when(step == nstep - 1)
        def _():
            dkv_ref[...] = acc[...].astype(BF16)

    whole = pl.BlockSpec((N_MEM, 2 * D), lambda i: (0, 0))
    return pl.pallas_call(
        body, name=name, grid=(nstep,),
        in_specs=[_row_spec(tr, D), whole, _row_spec(tr, D)],
        out_specs=[_row_spec(tr, D), whole],
        out_shape=[jax.ShapeDtypeStruct((S, D), BF16), jax.ShapeDtypeStruct((N_MEM, 2 * D), BF16)],
        scratch_shapes=[pltpu.VMEM((N_MEM, 2 * D), F32)],
        compiler_params=_params("arbitrary"),
    )(qc, kv, do)


def _row_tile(rows, cap=128):
    return max(t for t in range(16, cap + 1, 16) if rows % t == 0)


def cast_bf16(w, name):
    R, C = w.shape
    tr = _row_tile(R, 256)

    def body(w_ref, o_ref):
        o_ref[...] = w_ref[...].astype(BF16)

    return pl.pallas_call(
        body, name=name, grid=(R // tr,),
        in_specs=[_row_spec(tr, C)], out_specs=_row_spec(tr, C),
        out_shape=jax.ShapeDtypeStruct((R, C), BF16),
        compiler_params=_params("parallel"),
    )(w)


def adamw(parts, w, m, v, name):
    R, C = w.shape
    tr = _row_tile(R)
    c1 = 1.0 - ADAM_B1 ** ADAM_STEP
    c2 = 1.0 - ADAM_B2 ** ADAM_STEP

    def body(p_ref, w_ref, m_ref, v_ref, g_ref, d_ref, mo_ref, vo_ref):
        g = p_ref[0].astype(F32)
        for p in range(1, N_DEV):
            g = g + p_ref[p].astype(F32)
        m_new = ADAM_B1 * m_ref[...] + (1.0 - ADAM_B1) * g
        v_new = ADAM_B2 * v_ref[...] + (1.0 - ADAM_B2) * (g * g)
        g_ref[...] = g
        mo_ref[...] = m_new
        vo_ref[...] = v_new
        d_ref[...] = -ADAM_LR * ((m_new / c1) / (jnp.sqrt(v_new / c2) + ADAM_EPS) + ADAM_WD * w_ref[...])

    spec = _row_spec(tr, C)
    sds = jax.ShapeDtypeStruct((R, C), F32)
    return pl.pallas_call(
        body, name=name, grid=(R // tr,),
        in_specs=[pl.BlockSpec((N_DEV, tr, C), lambda i: (0, i, 0)), spec, spec, spec],
        out_specs=[spec, spec, spec, spec],
        out_shape=[sds, sds, sds, sds],
        compiler_params=_params("parallel"),
    )(parts, w, m, v)


def _place():
    return lax.axis_index("x"), lax.axis_index("y"), lax.axis_index("c")


def all_gather_shards(shards, name):
    n = len(shards)

    def body(*refs):
        ins, outs = refs[:n], refs[n:2 * n]
        send_sems, recv_sems, local_sems = refs[2 * n:]
        x, y, c = _place()
        me, sibling = (x, y, c), (x, y, 1 - c)
        chips = [(1 - x, y), (x, 1 - y), (1 - x, 1 - y)]

        def copy(w, k, block, to, src=None):
            px, py, pc = block
            dst = outs[w].at[4 * px + 2 * py + pc]
            return pltpu.make_async_remote_copy(
                src_ref=dst if src is None else src, dst_ref=dst,
                send_sem=send_sems.at[7 * w + k], recv_sem=recv_sems.at[7 * w + k],
                device_id=to, device_id_type=MESH)

        mine = [pltpu.make_async_copy(ins[w], outs[w].at[4 * x + 2 * y + c], local_sems.at[w]) for w in range(n)]
        for cp in mine:
            cp.start()
        first = []
        for w in range(n):
            first.append(copy(w, 0, me, sibling, src=ins[w]))
            first += [copy(w, 1 + j, me, (*chip, c), src=ins[w]) for j, chip in enumerate(chips)]
        for cp in first:
            cp.start()
        passed = []
        for w in range(n):
            for j, chip in enumerate(chips):
                copy(w, 1 + j, (*chip, c), me).wait_recv()
                cp = copy(w, 4 + j, (*chip, c), sibling)
                cp.start()
                passed.append(cp)
        for w in range(n):
            copy(w, 0, sibling, me).wait_recv()
            for j, chip in enumerate(chips):
                copy(w, 4 + j, (*chip, 1 - c), me).wait_recv()
        for cp in first + passed:
            cp.wait_send()
        for cp in mine:
            cp.wait()

    return pl.pallas_call(
        body, name=name,
        in_specs=[ANY] * n, out_specs=[ANY] * n,
        out_shape=[jax.ShapeDtypeStruct((N_DEV,) + s.shape, s.dtype) for s in shards],
        scratch_shapes=[pltpu.SemaphoreType.DMA((7 * n,)), pltpu.SemaphoreType.DMA((7 * n,)),
                        pltpu.SemaphoreType.DMA((n,))],
        compiler_params=pltpu.CompilerParams(has_side_effects=True),
    )(*shards)


def exchange_partials(parts, small, name):
    n = len(parts)
    arrays = list(parts) + [small]

    def body(*refs):
        ins, outs = refs[:n + 1], refs[n + 1:2 * n + 2]
        send_sems, recv_sems, local_sems = refs[2 * n + 2:]
        x, y, c = _place()
        me = 4 * x + 2 * y + c

        def flip(v, bit):
            return 1 - v if bit else v

        def copy(w, rel):
            bx, by, bc = (rel >> 2) & 1, (rel >> 1) & 1, rel & 1
            px, py, pc = flip(x, bx), flip(y, by), flip(c, bc)
            src = ins[w].at[4 * px + 2 * py + pc] if w < n else ins[w]
            k = 7 * w + rel - 1
            return pltpu.make_async_remote_copy(
                src_ref=src, dst_ref=outs[w].at[me],
                send_sem=send_sems.at[k], recv_sem=recv_sems.at[k],
                device_id=(px, py, pc), device_id_type=MESH)

        mine = [pltpu.make_async_copy(ins[w].at[me] if w < n else ins[w], outs[w].at[me], local_sems.at[w])
                for w in range(n + 1)]
        for cp in mine:
            cp.start()
        sent = [copy(w, rel) for w in range(n + 1) for rel in range(1, N_DEV)]
        for cp in sent:
            cp.start()
        for cp in sent:
            cp.wait()
        for cp in mine:
            cp.wait()

    return pl.pallas_call(
        body, name=name,
        in_specs=[ANY] * (n + 1), out_specs=[ANY] * (n + 1),
        out_shape=[jax.ShapeDtypeStruct(p.shape, p.dtype) for p in parts]
        + [jax.ShapeDtypeStruct((N_DEV,) + small.shape, small.dtype)],
        scratch_shapes=[pltpu.SemaphoreType.DMA((7 * (n + 1),)), pltpu.SemaphoreType.DMA((7 * (n + 1),)),
                        pltpu.SemaphoreType.DMA((n + 1,))],
        compiler_params=pltpu.CompilerParams(has_side_effects=True),
    )(*arrays)


_SMALL = ("ffn1_norm", "mix_norm", "ln_v_gain", "ln_v_bias", "spatial_w", "spatial_b", "gnorm_a", "gnorm_b",
          "cross_norm", "mem_norm", "ffn2_norm", "final_norm")
_BIG = ("ffn1_w_in", "ffn1_w_out", "w_mix_in", "w_mix_out", "w_cq", "w_ckv", "w_co", "ffn2_w_in", "ffn2_w_out")
_COL_SHARDED = ("ffn1_w_in", "w_mix_in", "w_ckv", "ffn2_w_in")
_ORDER = ("ffn1_norm", "ffn1_w_in", "ffn1_w_out", "mix_norm", "w_mix_in", "ln_v_gain", "ln_v_bias", "spatial_w",
          "spatial_b", "gnorm_a", "gnorm_b", "w_mix_out", "cross_norm", "mem_norm", "w_cq", "w_ckv", "w_co",
          "ffn2_norm", "ffn2_w_in", "ffn2_w_out", "final_norm")


_SMALL_PAD = 120


def _rows128(a):
    return a.reshape(-1, 128)


def _ffn_bwd(tag, n_in, a, hsw, dfb, w_in_g, w_out):
    d_w_out = mm_tn(hsw, dfb, f"{tag}_dwout")
    dhsw = mm_nt(dfb, w_out, F32, f"{tag}_dhsw")
    da = swiglu_bwd(a, dhsw, f"{tag}_swiglu_bwd")
    d_w_in = mm_tn_g(n_in, da, N_DEV, f"{tag}_dwin")
    dn = mm_nt_g(da, w_in_g, f"{tag}_dn")
    return d_w_in, d_w_out, dn


def kernel(x, mem, ffn1_norm, ffn1_w_in, ffn1_w_out, mix_norm, w_mix_in, ln_v_gain, ln_v_bias, spatial_w, spatial_b, gnorm_a, gnorm_b, w_mix_out, cross_norm, mem_norm, w_cq, w_ckv, w_co, ffn2_norm, ffn2_w_in, ffn2_w_out, final_norm, loss_target, m_ffn1_norm, m_ffn1_w_in, m_ffn1_w_out, m_mix_norm, m_w_mix_in, m_ln_v_gain, m_ln_v_bias, m_spatial_w, m_spatial_b, m_gnorm_a, m_gnorm_b, m_w_mix_out, m_cross_norm, m_mem_norm, m_w_cq, m_w_ckv, m_w_co, m_ffn2_norm, m_ffn2_w_in, m_ffn2_w_out, m_final_norm, v_ffn1_norm, v_ffn1_w_in, v_ffn1_w_out, v_mix_norm, v_w_mix_in, v_ln_v_gain, v_ln_v_bias, v_spatial_w, v_spatial_b, v_gnorm_a, v_gnorm_b, v_w_mix_out, v_cross_norm, v_mem_norm, v_w_cq, v_w_ckv, v_w_co, v_ffn2_norm, v_ffn2_w_in, v_ffn2_w_out, v_final_norm):
    given = dict(locals())
    wts = {k: given[k] for k in _ORDER}
    mom = {k: given["m_" + k] for k in _ORDER}
    var = {k: given["v_" + k] for k in _ORDER}

    D = D_MODEL
    xs = x.reshape(-1, D)
    mems = mem.reshape(-1, D)
    tgt = loss_target.reshape(-1, D)
    vec = lambda a: a.reshape(1, -1)
    g1, gmix, gcross, gmem, g2, gfin = (vec(wts[k]) for k in
                                        ("ffn1_norm", "mix_norm", "cross_norm", "mem_norm", "ffn2_norm", "final_norm"))
    ln_g, ln_b, ga, gb = (vec(wts[k]) for k in ("ln_v_gain", "ln_v_bias", "gnorm_a", "gnorm_b"))
    w_s = spatial_w.reshape(G_A, SGU_BLOCK, SGU_BLOCK)
    b_t = spatial_b.reshape(G_A, SGU_BLOCK).T

    shard2d = {k: wts[k].reshape(wts[k].shape[1:]) for k in _BIG}
    gathered = dict(zip(_BIG, all_gather_shards([cast_bf16(shard2d[k], f"cast_{k}") for k in _BIG], "ag_weights")))
    full = {k: (gathered[k] if k in _COL_SHARDED else gathered[k].reshape(-1, gathered[k].shape[2])) for k in _BIG}

    n1 = rms_fwd(xs, g1, "f_n1")
    a1 = mm_nn_g(n1, full["ffn1_w_in"], BF16, "f_a1")
    hsw1 = swiglu_fwd(a1, "f_hsw1")
    h1 = mm_nn(hsw1, full["ffn1_w_out"], F32, "f_h1", scale=0.5, res=xs)
    n2 = rms_fwd(h1, gmix, "f_n2")
    z = mm_nn_g(n2, full["w_mix_in"], F32, "f_z")
    ya = sgu_fwd(z, ln_g, ln_b, w_s, b_t, "f_sgu")
    yb = sb_fwd(z, "f_sb")
    ycat = rmscat_fwd(ya, yb, ga, gb, "f_ycat")
    h2 = mm_nn(ycat, full["w_mix_out"], F32, "f_h2", res=h1)
    n3 = rms_fwd(h2, gcross, "f_n3")
    memn = rms_fwd(mems, gmem, "f_memn")
    qc = mm_nn(n3, full["w_cq"], BF16, "f_qc", scale=X_DH ** -0.5)
    kv = mm_nn_g(memn, full["w_ckv"], BF16, "f_kv")
    o = xattn_fwd(qc, kv, "f_xattn")
    h3 = mm_nn(o, full["w_co"], F32, "f_h3", res=h2)
    n4 = rms_fwd(h3, g2, "f_n4")
    a2 = mm_nn_g(n4, full["ffn2_w_in"], BF16, "f_a2")
    hsw2 = swiglu_fwd(a2, "f_hsw2")
    h4 = mm_nn(hsw2, full["ffn2_w_out"], F32, "f_h4", scale=0.5, res=h3)

    grads = {}
    loss_part, dh4, df2, grads["final_norm"] = loss_head(h4, tgt, gfin, "loss_head")
    grads["ffn2_w_in"], grads["ffn2_w_out"], dn4 = _ffn_bwd("b_ffn2", n4, a2, hsw2, df2, full["ffn2_w_in"], full["ffn2_w_out"])
    dh3, dh3b, grads["ffn2_norm"] = rms_bwd(dn4, h3, g2, dh4, 1.0, "b_n4")

    grads["w_co"] = mm_tn(o, dh3b, "b_dwco")
    do = mm_nt(dh3b, full["w_co"], BF16, "b_do")
    dqp, dkv = xattn_bwd(qc, kv, do, "b_xattn")
    grads["w_cq"] = mm_tn(n3, dqp, "b_dwcq")
    dn3 = mm_nt(dqp, full["w_cq"], F32, "b_dn3")
    grads["w_ckv"] = mm_tn_g(memn, dkv, N_DEV, "b_dwckv")
    dmemn = mm_nt_g(dkv, full["w_ckv"], "b_dmemn")
    _, _, grads["mem_norm"] = rms_bwd(dmemn, mems, gmem, None, 1.0, "b_memn")
    dh2, dh2b, grads["cross_norm"] = rms_bwd(dn3, h2, gcross, dh3, 1.0, "b_n3")

    grads["w_mix_out"] = mm_tn(ycat, dh2b, "b_dwmixout")
    dycat = mm_nt(dh2b, full["w_mix_out"], F32, "b_dycat")
    dya, dyb, grads["gnorm_a"], grads["gnorm_b"] = rmscat_bwd(dycat, ya, yb, ga, gb, "b_ycat")
    dza, grads["ln_v_gain"], grads["ln_v_bias"], grads["spatial_w"], grads["spatial_b"] = sgu_bwd(
        z, dya, ln_g, ln_b, w_s, b_t, "b_sgu")
    dq, dk, dv = sb_bwd(z, yb, dyb, "b_sb")
    dz = jnp.concatenate([dza, dq, dk, dv], axis=1)
    grads["w_mix_in"] = mm_tn_g(n2, dz, N_DEV, "b_dwmixin")
    dn2 = mm_nt_g(dz, full["w_mix_in"], "b_dn2")
    dh1, dh1b, grads["mix_norm"] = rms_bwd(dn2, h1, gmix, dh2, 0.5, "b_n2")

    grads["ffn1_w_in"], grads["ffn1_w_out"], dn1 = _ffn_bwd("b_ffn1", n1, a1, hsw1, dh1b, full["ffn1_w_in"], full["ffn1_w_out"])
    dx, _, grads["ffn1_norm"] = rms_bwd(dn1, xs, g1, dh1, 1.0, "b_n1")

    pack = lambda d: jnp.concatenate([_rows128(d[k]) for k in _SMALL] + [jnp.zeros((_SMALL_PAD, 128), F32)], axis=0)
    small_part = pack(grads)
    parts = [grads[k] if k in _COL_SHARDED else grads[k].reshape(N_DEV, -1, grads[k].shape[1]) for k in _BIG]
    *recv, small_all = exchange_partials(parts, small_part, "exchange_grads")

    out_g, out_d, out_m, out_v = {}, {}, {}, {}
    for k, r in zip(_BIG, recv):
        res = adamw(r, shard2d[k], mom[k].reshape(shard2d[k].shape), var[k].reshape(shard2d[k].shape), f"adamw_{k}")
        out_g[k], out_d[k], out_m[k], out_v[k] = (t.reshape(wts[k].shape) for t in res)
    res = adamw(small_all, pack(wts), pack(mom), pack(var), "adamw_small")
    row = 0
    for k in _SMALL:
        nrow = wts[k].size // 128
        for dst, t in zip((out_g, out_d, out_m, out_v), res):
            dst[k] = t[row:row + nrow].reshape(wts[k].shape)
        row += nrow

    loss = lax.psum(loss_part[0, 0], ("x", "y", "c"))
    grad_x = dx.reshape(x.shape)
    return (loss, grad_x, *[out_g[k] for k in _ORDER], *[out_d[k] for k in _ORDER],
            *[out_m[k] for k in _ORDER], *[out_v[k] for k in _ORDER])
```

```python
import math

import jax
import jax.numpy as jnp
from jax import lax
from jax.experimental import pallas as pl
from jax.experimental.pallas import tpu as pltpu

F32 = jnp.float32
BF16 = jnp.bfloat16

N_DEV = 8
D_MODEL = 2048
D_FF = 5632
W_A = 1024
G_A = 8
GA_DIM = 128
SGU_BLOCK = 128
CHUNK = 64
H_B = 8
DH_B = 128
Q_BLOCK = 128
X_HEADS = 4
X_DH = 512
N_MEM = 256
EPS = 1e-6

ADAM_LR = 0.001
ADAM_B1 = 0.9
ADAM_B2 = 0.999
ADAM_EPS = 1e-08
ADAM_WD = 0.01
ADAM_STEP = 10

VMEM_LIMIT = 56 * 2**20
ROW_TILE = 256

MESH = pl.DeviceIdType.MESH
ANY = pl.BlockSpec(memory_space=pl.ANY)

_NT = (((1,), (1,)), ((), ()))
_TN = (((0,), (0,)), ((), ()))


def _params(*sem):
    return pltpu.CompilerParams(dimension_semantics=sem, vmem_limit_bytes=VMEM_LIMIT)


def _zeros(ref):
    return jnp.zeros(ref.shape, ref.dtype)


def _dot(a, b):
    return jnp.dot(a, b, preferred_element_type=F32)


def _dot_nt(a, b):
    return lax.dot_general(a, b, _NT, preferred_element_type=F32)


def _dot_tn(a, b):
    return lax.dot_general(a, b, _TN, preferred_element_type=F32)


def mm_nn_g(a, bg, out_dtype, name, tm=512):
    M, K = a.shape
    G, _, n = bg.shape
    tm = min(tm, M)

    def body(a_ref, b_ref, o_ref):
        o_ref[...] = _dot(a_ref[...], b_ref[...]).astype(o_ref.dtype)

    return pl.pallas_call(
        body, name=name, grid=(G, M // tm),
        in_specs=[pl.BlockSpec((tm, K), lambda g, m: (m, 0)),
                  pl.BlockSpec((None, K, n), lambda g, m: (g, 0, 0))],
        out_specs=pl.BlockSpec((tm, n), lambda g, m: (m, g)),
        out_shape=jax.ShapeDtypeStruct((M, G * n), out_dtype),
        compiler_params=_params("parallel", "parallel"),
    )(a, bg)


def mm_nn(a, b, out_dtype, name, tm=512, tn=1024, scale=1.0, res=None):
    M, K = a.shape
    _, N = b.shape
    tm, tn = min(tm, M), min(tn, N)

    def body(*refs):
        if res is None:
            a_ref, b_ref, o_ref = refs
            acc = _dot(a_ref[...], b_ref[...])
            o_ref[...] = (acc * scale if scale != 1.0 else acc).astype(o_ref.dtype)
        else:
            a_ref, b_ref, r_ref, o_ref = refs
            o_ref[...] = (r_ref[...] + scale * _dot(a_ref[...], b_ref[...])).astype(o_ref.dtype)

    in_specs = [pl.BlockSpec((tm, K), lambda n, m: (m, 0)),
                pl.BlockSpec((K, tn), lambda n, m: (0, n))]
    args = [a, b]
    if res is not None:
        in_specs.append(pl.BlockSpec((tm, tn), lambda n, m: (m, n)))
        args.append(res)
    return pl.pallas_call(
        body, name=name, grid=(N // tn, M // tm),
        in_specs=in_specs,
        out_specs=pl.BlockSpec((tm, tn), lambda n, m: (m, n)),
        out_shape=jax.ShapeDtypeStruct((M, N), out_dtype),
        compiler_params=_params("parallel", "parallel"),
    )(*args)


def mm_nt_g(dy, bg, name, tm=512):
    M, _ = dy.shape
    G, K, n = bg.shape
    tm = min(tm, M)

    def body(dy_ref, b_ref, o_ref):
        part = _dot_nt(dy_ref[...], b_ref[...])

        @pl.when(pl.program_id(1) == 0)
        def _():
            o_ref[...] = part

        @pl.when(pl.program_id(1) > 0)
        def _():
            o_ref[...] += part

    return pl.pallas_call(
        body, name=name, grid=(M // tm, G),
        in_specs=[pl.BlockSpec((tm, n), lambda m, g: (m, g)),
                  pl.BlockSpec((None, K, n), lambda m, g: (g, 0, 0))],
        out_specs=pl.BlockSpec((tm, K), lambda m, g: (m, 0)),
        out_shape=jax.ShapeDtypeStruct((M, K), F32),
        compiler_params=_params("parallel", "arbitrary"),
    )(dy, bg)


def mm_nt(dy, b, out_dtype, name, tk=512):
    M, N = dy.shape
    K, _ = b.shape

    def body(dy_ref, b_ref, o_ref):
        o_ref[...] = _dot_nt(dy_ref[...], b_ref[...]).astype(o_ref.dtype)

    return pl.pallas_call(
        body, name=name, grid=(K // tk,),
        in_specs=[pl.BlockSpec((M, N), lambda k: (0, 0)),
                  pl.BlockSpec((tk, N), lambda k: (k, 0))],
        out_specs=pl.BlockSpec((M, tk), lambda k: (0, k)),
        out_shape=jax.ShapeDtypeStruct((M, K), out_dtype),
        compiler_params=_params("parallel"),
    )(dy, b)


def mm_tn_g(x, dy, G, name, tk=512):
    M, K = x.shape
    n = dy.shape[1] // G

    def body(x_ref, dy_ref, o_ref):
        o_ref[...] = _dot_tn(x_ref[...], dy_ref[...]).astype(o_ref.dtype)

    return pl.pallas_call(
        body, name=name, grid=(G, K // tk),
        in_specs=[pl.BlockSpec((M, tk), lambda g, k: (0, k)),
                  pl.BlockSpec((M, n), lambda g, k: (0, g))],
        out_specs=pl.BlockSpec((None, tk, n), lambda g, k: (g, k, 0)),
        out_shape=jax.ShapeDtypeStruct((G, K, n), BF16),
        compiler_params=_params("parallel", "parallel"),
    )(x, dy)


def mm_tn(x, dy, name, tk=512):
    M, K = x.shape
    _, N = dy.shape

    def body(x_ref, dy_ref, o_ref):
        o_ref[...] = _dot_tn(x_ref[...], dy_ref[...]).astype(o_ref.dtype)

    return pl.pallas_call(
        body, name=name, grid=(K // tk,),
        in_specs=[pl.BlockSpec((M, tk), lambda k: (0, k)),
                  pl.BlockSpec((M, N), lambda k: (0, 0))],
        out_specs=pl.BlockSpec((tk, N), lambda k: (k, 0)),
        out_shape=jax.ShapeDtypeStruct((K, N), BF16),
        compiler_params=_params("parallel"),
    )(x, dy)


def _rstd(x):
    return lax.rsqrt(jnp.mean(x * x, axis=-1, keepdims=True) + EPS)


def _rms_bwd(dn, xhat, r, g):
    dxhat = dn * g
    return r * (dxhat - xhat * jnp.mean(dxhat * xhat, axis=-1, keepdims=True))


def _row_spec(tr, width, col=0):
    return pl.BlockSpec((tr, width), lambda i: (i, col))


def _vec_spec(width):
    return pl.BlockSpec((1, width), lambda i: (0, 0))


def rms_fwd(x, g, name):
    M, D = x.shape
    tr = min(ROW_TILE, M)

    def body(x_ref, g_ref, o_ref):
        xv = x_ref[...]
        o_ref[...] = (xv * _rstd(xv) * g_ref[...]).astype(o_ref.dtype)

    return pl.pallas_call(
        body, name=name, grid=(M // tr,),
        in_specs=[_row_spec(tr, D), _vec_spec(D)],
        out_specs=_row_spec(tr, D),
        out_shape=jax.ShapeDtypeStruct((M, D), BF16),
        compiler_params=_params("parallel"),
    )(x, g)


def rms_bwd(dn, h, g, dres, copy_scale, name):
    M, D = h.shape
    tr = min(ROW_TILE, M)
    has_res = dres is not None

    def body(*refs):
        if has_res:
            dn_ref, h_ref, g_ref, dres_ref, dh_ref, dhb_ref, dg_ref = refs
        else:
            dn_ref, h_ref, g_ref, dh_ref, dhb_ref, dg_ref = refs
        hv = h_ref[...]
        r = _rstd(hv)
        xhat = hv * r
        dn = dn_ref[...]
        part = jnp.sum(dn * xhat, axis=0, keepdims=True)

        @pl.when(pl.program_id(0) == 0)
        def _():
            dg_ref[...] = part

        @pl.when(pl.program_id(0) > 0)
        def _():
            dg_ref[...] += part

        dh = _rms_bwd(dn, xhat, r, g_ref[...])
        if has_res:
            dh = dh + dres_ref[...]
        dh_ref[...] = dh
        dhb_ref[...] = (dh * copy_scale if copy_scale != 1.0 else dh).astype(BF16)

    in_specs = [_row_spec(tr, D), _row_spec(tr, D), _vec_spec(D)]
    args = [dn, h, g]
    if has_res:
        in_specs.append(_row_spec(tr, D))
        args.append(dres)
    return pl.pallas_call(
        body, name=name, grid=(M // tr,),
        in_specs=in_specs,
        out_specs=[_row_spec(tr, D), _row_spec(tr, D), _vec_spec(D)],
        out_shape=[jax.ShapeDtypeStruct((M, D), F32), jax.ShapeDtypeStruct((M, D), BF16),
                   jax.ShapeDtypeStruct((1, D), F32)],
        compiler_params=_params("arbitrary"),
    )(*args)


def _sigmoid(x):
    return 1.0 / (1.0 + jnp.exp(-x))


def swiglu_fwd(a, name):
    M, F2 = a.shape
    F = F2 // 2
    tr = min(ROW_TILE, M)

    def body(g_ref, u_ref, o_ref):
        gt = g_ref[...].astype(F32)
        o_ref[...] = (gt * _sigmoid(gt) * u_ref[...].astype(F32)).astype(o_ref.dtype)

    return pl.pallas_call(
        body, name=name, grid=(M // tr,),
        in_specs=[_row_spec(tr, F, 0), _row_spec(tr, F, 1)],
        out_specs=_row_spec(tr, F),
        out_shape=jax.ShapeDtypeStruct((M, F), BF16),
        compiler_params=_params("parallel"),
    )(a, a)


def swiglu_bwd(a, dh, name):
    M, F2 = a.shape
    F = F2 // 2
    tr = min(ROW_TILE, M)

    def body(g_ref, u_ref, dh_ref, da_ref):
        gt = g_ref[...].astype(F32)
        up = u_ref[...].astype(F32)
        dh = dh_ref[...]
        sg = _sigmoid(gt)
        da_ref[:, :F] = (dh * up * (sg * (1.0 + gt * (1.0 - sg)))).astype(BF16)
        da_ref[:, F:] = (dh * (gt * sg)).astype(BF16)

    return pl.pallas_call(
        body, name=name, grid=(M // tr,),
        in_specs=[_row_spec(tr, F, 0), _row_spec(tr, F, 1), _row_spec(tr, F)],
        out_specs=_row_spec(tr, F2),
        out_shape=jax.ShapeDtypeStruct((M, F2), BF16),
        compiler_params=_params("parallel"),
    )(a, a, dh)


def rmscat_fwd(ya, yb, ga, gb, name):
    M, W = ya.shape
    tr = min(ROW_TILE, M)

    def body(ya_ref, yb_ref, ga_ref, gb_ref, o_ref):
        a = ya_ref[...]
        b = yb_ref[...]
        o_ref[:, :W] = (a * _rstd(a) * ga_ref[...]).astype(BF16)
        o_ref[:, W:] = (b * _rstd(b) * gb_ref[...]).astype(BF16)

    return pl.pallas_call(
        body, name=name, grid=(M // tr,),
        in_specs=[_row_spec(tr, W), _row_spec(tr, W), _vec_spec(W), _vec_spec(W)],
        out_specs=_row_spec(tr, 2 * W),
        out_shape=jax.ShapeDtypeStruct((M, 2 * W), BF16),
        compiler_params=_params("parallel"),
    )(ya, yb, ga, gb)


def rmscat_bwd(dycat, ya, yb, ga, gb, name):
    M, W = ya.shape
    tr = min(ROW_TILE, M)

    def body(dc_ref, ya_ref, yb_ref, ga_ref, gb_ref, dya_ref, dyb_ref, dga_ref, dgb_ref):
        first = pl.program_id(0) == 0
        for y_ref, g_ref, dy_ref, dg_ref, lo in ((ya_ref, ga_ref, dya_ref, dga_ref, 0),
                                                 (yb_ref, gb_ref, dyb_ref, dgb_ref, W)):
            yv = y_ref[...]
            r = _rstd(yv)
            xhat = yv * r
            dn = dc_ref[:, lo:lo + W]
            part = jnp.sum(dn * xhat, axis=0, keepdims=True)

            @pl.when(first)
            def _():
                dg_ref[...] = part

            @pl.when(jnp.logical_not(first))
            def _():
                dg_ref[...] += part

            dy_ref[...] = _rms_bwd(dn, xhat, r, g_ref[...])

    return pl.pallas_call(
        body, name=name, grid=(M // tr,),
        in_specs=[_row_spec(tr, 2 * W), _row_spec(tr, W), _row_spec(tr, W), _vec_spec(W), _vec_spec(W)],
        out_specs=[_row_spec(tr, W), _row_spec(tr, W), _vec_spec(W), _vec_spec(W)],
        out_shape=[jax.ShapeDtypeStruct((M, W), F32), jax.ShapeDtypeStruct((M, W), F32),
                   jax.ShapeDtypeStruct((1, W), F32), jax.ShapeDtypeStruct((1, W), F32)],
        compiler_params=_params("arbitrary"),
    )(dycat, ya, yb, ga, gb)


def loss_head(h, target, g, name):
    M, D = h.shape
    tr = min(ROW_TILE, M)

    def body(h_ref, t_ref, g_ref, loss_ref, dh_ref, dhb_ref, dg_ref):
        hv = h_ref[...]
        gv = g_ref[...]
        r = _rstd(hv)
        xhat = hv * r
        err = xhat * gv - t_ref[...]
        lsum = jnp.sum(jnp.sum(err * err, axis=1, keepdims=True), axis=0, keepdims=True) * (0.5 / D)
        dy = err * (1.0 / D)
        part = jnp.sum(dy * xhat, axis=0, keepdims=True)

        @pl.when(pl.program_id(0) == 0)
        def _():
            dg_ref[...] = part
            loss_ref[...] = _zeros(loss_ref) + lsum

        @pl.when(pl.program_id(0) > 0)
        def _():
            dg_ref[...] += part
            loss_ref[...] += lsum

        dh = _rms_bwd(dy, xhat, r, gv)
        dh_ref[...] = dh
        dhb_ref[...] = (0.5 * dh).astype(BF16)

    return pl.pallas_call(
        body, name=name, grid=(M // tr,),
        in_specs=[_row_spec(tr, D), _row_spec(tr, D), _vec_spec(D)],
        out_specs=[pl.BlockSpec((8, 128), lambda i: (0, 0)), _row_spec(tr, D), _row_spec(tr, D), _vec_spec(D)],
        out_shape=[jax.ShapeDtypeStruct((8, 128), F32), jax.ShapeDtypeStruct((M, D), F32),
                   jax.ShapeDtypeStruct((M, D), BF16), jax.ShapeDtypeStruct((1, D), F32)],
        compiler_params=_params("arbitrary"),
    )(h, target, g)


_GELU_C = math.sqrt(2.0 / math.pi)


def _gelu(x):
    return 0.5 * x * (1.0 + jnp.tanh(_GELU_C * (x + 0.044715 * (x * x * x))))


def _gelu_grad(x):
    t = jnp.tanh(_GELU_C * (x + 0.044715 * (x * x * x)))
    return 0.5 * (1.0 + t) + 0.5 * x * (1.0 - t * t) * (_GELU_C * (1.0 + 3.0 * 0.044715 * (x * x)))


def _sgu_mask():
    t = lax.broadcasted_iota(jnp.int32, (SGU_BLOCK, SGU_BLOCK), 0) // CHUNK
    s = lax.broadcasted_iota(jnp.int32, (SGU_BLOCK, SGU_BLOCK), 1) // CHUNK
    return s <= t


def _layernorm_stats(v):
    mu = jnp.mean(v, axis=-1, keepdims=True)
    cen = v - mu
    rstd = lax.rsqrt(jnp.mean(cen * cen, axis=-1, keepdims=True) + EPS)
    return cen * rstd, rstd


def sgu_fwd(z, ln_g, ln_b, w_s, b_t, name):
    S = z.shape[0]

    def body(zu_ref, zv_ref, lg_ref, lb_ref, w_ref, bt_ref, o_ref):
        mask = _sgu_mask()
        for g in range(G_A):
            cols = slice(g * GA_DIM, (g + 1) * GA_DIM)
            u = _gelu(zu_ref[:, cols])
            vhat, _ = _layernorm_stats(_gelu(zv_ref[:, cols]))
            vln = vhat * lg_ref[:, cols] + lb_ref[:, cols]
            w = jnp.where(mask, w_ref[g], 0.0).astype(BF16)
            mixed = _dot(w, vln.astype(BF16)) + bt_ref[:, g:g + 1]
            o_ref[:, cols] = u * mixed

    return pl.pallas_call(
        body, name=name, grid=(S // SGU_BLOCK,),
        in_specs=[_row_spec(SGU_BLOCK, W_A, 0), _row_spec(SGU_BLOCK, W_A, 1), _vec_spec(W_A), _vec_spec(W_A),
                  pl.BlockSpec((G_A, SGU_BLOCK, SGU_BLOCK), lambda i: (0, 0, 0)),
                  pl.BlockSpec((SGU_BLOCK, G_A), lambda i: (0, 0))],
        out_specs=_row_spec(SGU_BLOCK, W_A),
        out_shape=jax.ShapeDtypeStruct((S, W_A), F32),
        compiler_params=_params("parallel"),
    )(z, z, ln_g, ln_b, w_s, b_t)


def sgu_bwd(z, dya, ln_g, ln_b, w_s, b_t, name):
    S = z.shape[0]
    nblk = S // SGU_BLOCK

    def body(zu_ref, zv_ref, dy_ref, lg_ref, lb_ref, w_ref, bt_ref,
             dz_ref, dlg_ref, dlb_ref, dw_ref, db_ref, dmix_acc):
        step = pl.program_id(0)
        mask = _sgu_mask()

        @pl.when(step == 0)
        def _():
            dlg_ref[...] = _zeros(dlg_ref)
            dlb_ref[...] = _zeros(dlb_ref)
            dw_ref[...] = _zeros(dw_ref)
            dmix_acc[...] = _zeros(dmix_acc)

        for g in range(G_A):
            cols = slice(g * GA_DIM, (g + 1) * GA_DIM)
            zu = zu_ref[:, cols]
            zv = zv_ref[:, cols]
            u = _gelu(zu)
            vhat, rstd = _layernorm_stats(_gelu(zv))
            lg = lg_ref[:, cols]
            vln = (vhat * lg + lb_ref[:, cols]).astype(BF16)
            w = jnp.where(mask, w_ref[g], 0.0)
            mixed = _dot(w.astype(BF16), vln) + bt_ref[:, g:g + 1]
            dy = dy_ref[:, cols]
            du = dy * mixed
            dmixed = dy * u
            dmixed_b = dmixed.astype(BF16)
            dmix_acc[g] += dmixed
            dw_ref[g] += jnp.where(mask, _dot_nt(dmixed_b, vln), 0.0)
            dvln = _dot(w.T.astype(BF16), dmixed_b)
            dlb_ref[:, cols] += jnp.sum(dvln, axis=0, keepdims=True)
            dlg_ref[:, cols] += jnp.sum(dvln * vhat, axis=0, keepdims=True)
            dvhat = dvln * lg
            dv = rstd * (dvhat - jnp.mean(dvhat, axis=-1, keepdims=True)
                         - vhat * jnp.mean(dvhat * vhat, axis=-1, keepdims=True))
            dz_ref[:, cols] = (du * _gelu_grad(zu)).astype(BF16)
            dz_ref[:, W_A + g * GA_DIM:W_A + (g + 1) * GA_DIM] = (dv * _gelu_grad(zv)).astype(BF16)

        @pl.when(step == nblk - 1)
        def _():
            for g in range(G_A):
                db_ref[g] = jnp.sum(dmix_acc[g], axis=1, keepdims=True)

    whole3 = lambda shape: pl.BlockSpec(shape, lambda i: (0, 0, 0))
    return pl.pallas_call(
        body, name=name, grid=(nblk,),
        in_specs=[_row_spec(SGU_BLOCK, W_A, 0), _row_spec(SGU_BLOCK, W_A, 1), _row_spec(SGU_BLOCK, W_A),
                  _vec_spec(W_A), _vec_spec(W_A), whole3((G_A, SGU_BLOCK, SGU_BLOCK)),
                  pl.BlockSpec((SGU_BLOCK, G_A), lambda i: (0, 0))],
        out_specs=[_row_spec(SGU_BLOCK, 2 * W_A), _vec_spec(W_A), _vec_spec(W_A),
                   whole3((G_A, SGU_BLOCK, SGU_BLOCK)), whole3((G_A, SGU_BLOCK, 1))],
        out_shape=[jax.ShapeDtypeStruct((S, 2 * W_A), BF16), jax.ShapeDtypeStruct((1, W_A), F32),
                   jax.ShapeDtypeStruct((1, W_A), F32), jax.ShapeDtypeStruct((G_A, SGU_BLOCK, SGU_BLOCK), F32),
                   jax.ShapeDtypeStruct((G_A, SGU_BLOCK, 1), F32)],
        scratch_shapes=[pltpu.VMEM((G_A, SGU_BLOCK, SGU_BLOCK), F32)],
        compiler_params=_params("arbitrary"),
    )(z, z, dya, ln_g, ln_b, w_s, b_t)


def _log_sigmoid(z):
    return jnp.minimum(z, 0.0) - jnp.log(1.0 + jnp.exp(-jnp.abs(z)))


def _suffix_sum(x, upper):
    hi = x.astype(BF16)
    rem = x - hi.astype(F32)
    mid = rem.astype(BF16)
    lo = (rem - mid.astype(F32)).astype(BF16)
    return _dot(hi, upper) + _dot(mid, upper) + _dot(lo, upper)


SB_ROWS = 512
_SB_SUB = SB_ROWS // Q_BLOCK


def _sb_upper():
    row = lax.broadcasted_iota(jnp.int32, (Q_BLOCK, Q_BLOCK), 0)
    col = lax.broadcasted_iota(jnp.int32, (Q_BLOCK, Q_BLOCK), 1)
    return (row > col).astype(BF16)


def _sb_sweep(step, tile, state):
    row = lax.broadcasted_iota(jnp.int32, (SB_ROWS, Q_BLOCK), 0)
    col = lax.broadcasted_iota(jnp.int32, (SB_ROWS, Q_BLOCK), 1)
    for r in reversed(range(_SB_SUB)):
        state = tile(step * _SB_SUB + r, state, col + r * Q_BLOCK < row)

    def group(g, state):
        base = (step - 1 - g) * _SB_SUB
        for r in reversed(range(_SB_SUB)):
            state = tile(base + r, state, None)
        return state

    return lax.fori_loop(0, step, group, state)


def _sb_col(part):
    return (2 * W_A + part * (H_B * DH_B)) // DH_B


def _sb_q_spec():
    return pl.BlockSpec((SB_ROWS, DH_B), lambda h, i: (i, _sb_col(0) + h))


def _sb_kv_spec(S, part):
    return pl.BlockSpec((S, DH_B), lambda h, i: (0, _sb_col(part) + h))


def sb_fwd(z, name):
    S = z.shape[0]
    scale = DH_B ** -0.5

    def body(q_ref, k_ref, v_ref, o_ref):
        step = pl.program_id(1)
        q = (q_ref[...] * scale).astype(BF16)
        upper = _sb_upper()

        def tile(j, carry, causal):
            rows = pl.ds(pl.multiple_of(j * Q_BLOCK, Q_BLOCK), Q_BLOCK)
            zz = _dot_nt(q, k_ref[rows, :].astype(BF16))
            lb = _log_sigmoid(zz)
            l1m = lb - zz
            if causal is not None:
                l1m = jnp.where(causal, l1m, 0.0)
            a = jnp.exp(lb + _suffix_sum(l1m, upper) + carry)
            if causal is not None:
                a = jnp.where(causal, a, 0.0)
            o_ref[...] += _dot(a.astype(BF16), v_ref[rows, :].astype(BF16))
            return carry + jnp.sum(l1m, axis=1, keepdims=True)

        o_ref[...] = _zeros(o_ref)
        carry = _sb_sweep(step, tile, jnp.zeros((SB_ROWS, 1), F32))

    return pl.pallas_call(
        body, name=name, grid=(H_B, S // SB_ROWS),
        in_specs=[_sb_q_spec(), _sb_kv_spec(S, 1), _sb_kv_spec(S, 2)],
        out_specs=pl.BlockSpec((SB_ROWS, DH_B), lambda h, i: (i, h)),
        out_shape=jax.ShapeDtypeStruct((S, H_B * DH_B), F32),
        compiler_params=_params("parallel", "parallel"),
    )(z, z, z)


def sb_bwd(z, out, dout, name):
    S = z.shape[0]
    nstep = S // SB_ROWS
    scale = DH_B ** -0.5

    def body(q_ref, k_ref, v_ref, o_ref, do_ref, dq_ref, dk_ref, dv_ref, dq_acc, dk_acc, dv_acc):
        step = pl.program_id(1)

        @pl.when(step == 0)
        def _():
            dk_acc[...] = _zeros(dk_acc)
            dv_acc[...] = _zeros(dv_acc)

        q = (q_ref[...] * scale).astype(BF16)
        do_b = do_ref[...].astype(BF16)
        delta = jnp.sum(do_b.astype(F32) * o_ref[...], axis=1, keepdims=True)
        upper = _sb_upper()
        dq_acc[...] = _zeros(dq_acc)

        def tile(j, state, causal):
            c_l1m, c_g = state
            rows = pl.ds(pl.multiple_of(j * Q_BLOCK, Q_BLOCK), Q_BLOCK)
            k_j = k_ref[rows, :].astype(BF16)
            zz = _dot_nt(q, k_j)
            lb = _log_sigmoid(zz)
            l1m = lb - zz
            if causal is not None:
                l1m = jnp.where(causal, l1m, 0.0)
            a = jnp.exp(lb + _suffix_sum(l1m, upper) + c_l1m)
            if causal is not None:
                a = jnp.where(causal, a, 0.0)
            a_b = a.astype(BF16)
            dv_acc[rows, :] += _dot_tn(a_b, do_b)
            gmat = a_b.astype(F32) * _dot_nt(do_b, v_ref[rows, :].astype(BF16))
            before = delta - gmat - (_suffix_sum(gmat, upper) + c_g)
            sig = jnp.exp(lb)
            dz = gmat * (1.0 - sig) - sig * before
            if causal is not None:
                dz = jnp.where(causal, dz, 0.0)
            dz_b = dz.astype(BF16)
            dk_acc[rows, :] += _dot_tn(dz_b, q)
            dq_acc[...] += _dot(dz_b, k_j)
            return c_l1m + jnp.sum(l1m, axis=1, keepdims=True), c_g + jnp.sum(gmat, axis=1, keepdims=True)

        zero = jnp.zeros((SB_ROWS, 1), F32)
        _sb_sweep(step, tile, (zero, zero))
        dq_ref[...] = (dq_acc[...] * scale).astype(BF16)

        @pl.when(step == nstep - 1)
        def _():
            dk_ref[...] = dk_acc[...].astype(BF16)
            dv_ref[...] = dv_acc[...].astype(BF16)

    rows_spec = pl.BlockSpec((SB_ROWS, DH_B), lambda h, i: (i, h))
    head_spec = pl.BlockSpec((S, DH_B), lambda h, i: (0, h))
    out_sds = jax.ShapeDtypeStruct((S, H_B * DH_B), BF16)
    return pl.pallas_call(
        body, name=name, grid=(H_B, nstep),
        in_specs=[_sb_q_spec(), _sb_kv_spec(S, 1), _sb_kv_spec(S, 2), rows_spec, rows_spec],
        out_specs=[rows_spec, head_spec, head_spec],
        out_shape=[out_sds, out_sds, out_sds],
        scratch_shapes=[pltpu.VMEM((SB_ROWS, DH_B), F32)] + [pltpu.VMEM((S, DH_B), F32)] * 2,
        compiler_params=_params("parallel", "arbitrary"),
    )(z, z, z, out, dout)


def _softmax(s):
    e = jnp.exp(s - jnp.max(s, axis=-1, keepdims=True))
    return e / jnp.sum(e, axis=-1, keepdims=True)


def xattn_fwd(qc, kv, name):
    S, D = qc.shape
    tr = min(ROW_TILE, S)

    def body(q_ref, kv_ref, o_ref):
        for h in range(X_HEADS):
            cols = slice(h * X_DH, (h + 1) * X_DH)
            p = _softmax(_dot_nt(q_ref[:, cols], kv_ref[:, cols]))
            o_ref[:, cols] = _dot(p.astype(BF16), kv_ref[:, D + h * X_DH:D + (h + 1) * X_DH]).astype(BF16)

    return pl.pallas_call(
        body, name=name, grid=(S // tr,),
        in_specs=[_row_spec(tr, D), pl.BlockSpec((N_MEM, 2 * D), lambda i: (0, 0))],
        out_specs=_row_spec(tr, D),
        out_shape=jax.ShapeDtypeStruct((S, D), BF16),
        compiler_params=_params("parallel"),
    )(qc, kv)


def xattn_bwd(qc, kv, do, name):
    S, D = qc.shape
    tr = min(ROW_TILE, S)
    nstep = S // tr
    scale = X_DH ** -0.5

    def body(q_ref, kv_ref, do_ref, dq_ref, dkv_ref, acc):
        step = pl.program_id(0)

        @pl.when(step == 0)
        def _():
            acc[...] = _zeros(acc)

        for h in range(X_HEADS):
            cols = slice(h * X_DH, (h + 1) * X_DH)
            vcols = slice(D + h * X_DH, D + (h + 1) * X_DH)
            q = q_ref[:, cols]
            k = kv_ref[:, cols]
            do_h = do_ref[:, cols]
            p = _softmax(_dot_nt(q, k))
            dp = _dot_nt(do_h, kv_ref[:, vcols])
            acc[:, vcols] += _dot_tn(p.astype(BF16), do_h)
            ds = (p * (dp - jnp.sum(p * dp, axis=-1, keepdims=True))).astype(BF16)
            dq_ref[:, cols] = (_dot(ds, k) * scale).astype(BF16)
            acc[:, cols] += _dot_tn(ds, q)

        @pl.when(step == nstep - 1)
        def _():
            dkv_ref[...] = acc[...].astype(BF16)

    whole = pl.BlockSpec((N_MEM, 2 * D), lambda i: (0, 0))
    return pl.pallas_call(
        body, name=name, grid=(nstep,),
        in_specs=[_row_spec(tr, D), whole, _row_spec(tr, D)],
        out_specs=[_row_spec(tr, D), whole],
        out_shape=[jax.ShapeDtypeStruct((S, D), BF16), jax.ShapeDtypeStruct((N_MEM, 2 * D), BF16)],
        scratch_shapes=[pltpu.VMEM((N_MEM, 2 * D), F32)],
        compiler_params=_params("arbitrary"),
    )(qc, kv, do)


def _row_tile(rows, cap=128):
    return max(t for t in range(16, cap + 1, 16) if rows % t == 0)


def cast_bf16(w, name):
    R, C = w.shape
    tr = _row_tile(R, 256)

    def body(w_ref, o_ref):
        o_ref[...] = w_ref[...].astype(BF16)

    return pl.pallas_call(
        body, name=name, grid=(R // tr,),
        in_specs=[_row_spec(tr, C)], out_specs=_row_spec(tr, C),
        out_shape=jax.ShapeDtypeStruct((R, C), BF16),
        compiler_params=_params("parallel"),
    )(w)


def adamw(parts, w, m, v, name):
    R, C = w.shape
    n_parts = parts.shape[0]
    tr = _row_tile(R)
    c1 = 1.0 - ADAM_B1 ** ADAM_STEP
    c2 = 1.0 - ADAM_B2 ** ADAM_STEP

    def body(p_ref, w_ref, m_ref, v_ref, g_ref, d_ref, mo_ref, vo_ref):
        g = p_ref[0].astype(F32)
        for p in range(1, n_parts):
            g = g + p_ref[p].astype(F32)
        m_new = ADAM_B1 * m_ref[...] + (1.0 - ADAM_B1) * g
        v_new = ADAM_B2 * v_ref[...] + (1.0 - ADAM_B2) * (g * g)
        g_ref[...] = g
        mo_ref[...] = m_new
        vo_ref[...] = v_new
        d_ref[...] = -ADAM_LR * ((m_new / c1) / (jnp.sqrt(v_new / c2) + ADAM_EPS) + ADAM_WD * w_ref[...])

    spec = _row_spec(tr, C)
    sds = jax.ShapeDtypeStruct((R, C), F32)
    return pl.pallas_call(
        body, name=name, grid=(R // tr,),
        in_specs=[pl.BlockSpec((n_parts, tr, C), lambda i: (0, i, 0)), spec, spec, spec],
        out_specs=[spec, spec, spec, spec],
        out_shape=[sds, sds, sds, sds],
        compiler_params=_params("parallel"),
    )(parts, w, m, v)


def pair_sum(parts, from_sibling, core, name):
    _, R, C = parts.shape
    tr = _row_tile(R, 256)

    def body(core_ref, p_ref, s_ref, o_ref):
        o_ref[...] = (p_ref[...].astype(F32) + s_ref[...].astype(F32)).astype(o_ref.dtype)

    return pl.pallas_call(
        body, name=name,
        grid_spec=pltpu.PrefetchScalarGridSpec(
            num_scalar_prefetch=1, grid=(4, R // tr),
            in_specs=[pl.BlockSpec((None, tr, C), lambda q, i, core_ref: (2 * q + core_ref[0], i, 0)),
                      pl.BlockSpec((None, tr, C), lambda q, i, core_ref: (q, i, 0))],
            out_specs=pl.BlockSpec((None, tr, C), lambda q, i, core_ref: (q, i, 0))),
        out_shape=jax.ShapeDtypeStruct((4, R, C), BF16),
        compiler_params=_params("parallel", "parallel"),
    )(core, parts, from_sibling)


def add2(a, b, name):
    R, C = a.shape
    tr = _row_tile(R, 256)

    def body(a_ref, b_ref, o_ref):
        o_ref[...] = a_ref[...] + b_ref[...]

    spec = _row_spec(tr, C)
    return pl.pallas_call(
        body, name=name, grid=(R // tr,), in_specs=[spec, spec], out_specs=spec,
        out_shape=jax.ShapeDtypeStruct((R, C), F32), compiler_params=_params("parallel"),
    )(a, b)


def _place():
    return lax.axis_index("x"), lax.axis_index("y"), lax.axis_index("c")


class Comm:
    def __init__(self, arrays, out_shapes, n_remote, n_local, start, finish):
        self.arrays, self.out_shapes = list(arrays), list(out_shapes)
        self.n_remote, self.n_local = n_remote, max(n_local, 1)
        self.start, self.finish = start, finish

    def sem_shapes(self):
        return [pltpu.SemaphoreType.DMA((self.n_remote,)), pltpu.SemaphoreType.DMA((self.n_remote,)),
                pltpu.SemaphoreType.DMA((self.n_local,))]


def run_comm(comm, name):
    n_in, n_out = len(comm.arrays), len(comm.out_shapes)

    def body(*refs):
        ins, outs, sems = refs[:n_in], refs[n_in:n_in + n_out], refs[n_in + n_out:]
        comm.start(ins, outs, sems)
        comm.finish(ins, outs, sems)

    return pl.pallas_call(
        body, name=name, in_specs=[ANY] * n_in, out_specs=[ANY] * n_out, out_shape=comm.out_shapes,
        scratch_shapes=comm.sem_shapes(),
    )(*comm.arrays)


def _remote(src, dst, sems, k, to):
    return pltpu.make_async_remote_copy(src_ref=src, dst_ref=dst, send_sem=sems[0].at[k], recv_sem=sems[1].at[k],
                                        device_id=to, device_id_type=MESH)


def comm_all_gather(shards):
    n = len(shards)

    def parties():
        x, y, c = _place()
        return (x, y, c), (x, y, 1 - c), [(1 - x, y), (x, 1 - y), (1 - x, 1 - y)]

    def slab(outs, w, dev):
        return outs[w].at[4 * dev[0] + 2 * dev[1] + dev[2]]

    def own(ins, outs, sems):
        me, sibling, chips = parties()
        local = [pltpu.make_async_copy(ins[w], slab(outs, w, me), sems[2].at[w]) for w in range(n)]
        first = []
        for w in range(n):
            first.append(_remote(ins[w], slab(outs, w, me), sems, 7 * w, sibling))
            first += [_remote(ins[w], slab(outs, w, me), sems, 7 * w + 1 + j, (*chip, me[2]))
                      for j, chip in enumerate(chips)]
        return local, first

    def start(ins, outs, sems):
        local, first = own(ins, outs, sems)
        for cp in local + first:
            cp.start()

    def finish(ins, outs, sems):
        me, sibling, chips = parties()
        local, first = own(ins, outs, sems)
        passed = []
        for w in range(n):
            for j, chip in enumerate(chips):
                got = slab(outs, w, (*chip, me[2]))
                _remote(got, got, sems, 7 * w + 1 + j, me).wait_recv()
                cp = _remote(got, got, sems, 7 * w + 4 + j, sibling)
                cp.start()
                passed.append(cp)
        for w in range(n):
            got = slab(outs, w, sibling)
            _remote(got, got, sems, 7 * w, me).wait_recv()
            for j, chip in enumerate(chips):
                got = slab(outs, w, (*chip, sibling[2]))
                _remote(got, got, sems, 7 * w + 4 + j, me).wait_recv()
        for cp in first + passed:
            cp.wait_send()
        for cp in local:
            cp.wait()

    return Comm(shards, [jax.ShapeDtypeStruct((N_DEV,) + s.shape, s.dtype) for s in shards], 7 * n, n, start, finish)


def comm_pairs(parts, small):
    n = len(parts)

    def copies(ins, outs, sems):
        x, y, c = _place()
        sibling = (x, y, 1 - c)
        cps = [_remote(ins[w].at[2 * q + (1 - c)], outs[w].at[q], sems, 4 * w + q, sibling)
               for w in range(n) for q in range(4)]
        return cps + [_remote(ins[n], outs[n], sems, 4 * n, sibling)]

    def start(ins, outs, sems):
        for cp in copies(ins, outs, sems):
            cp.start()

    def finish(ins, outs, sems):
        for cp in copies(ins, outs, sems):
            cp.wait()

    out_shapes = [jax.ShapeDtypeStruct((4,) + p.shape[1:], p.dtype) for p in parts]
    out_shapes.append(jax.ShapeDtypeStruct(small.shape, small.dtype))
    return Comm(list(parts) + [small], out_shapes, 4 * n + 1, 0, start, finish)


def comm_chips(sums, small):
    n = len(sums)

    def copies(ins, outs, sems):
        x, y, c = _place()
        mine = 2 * x + y
        local = [pltpu.make_async_copy(ins[w].at[mine] if w < n else ins[w], outs[w].at[mine], sems[2].at[w])
                 for w in range(n + 1)]
        remote = []
        for w in range(n + 1):
            for j, (px, py) in enumerate([(1 - x, y), (x, 1 - y), (1 - x, 1 - y)]):
                src = ins[w].at[2 * px + py] if w < n else ins[w]
                remote.append(_remote(src, outs[w].at[mine], sems, 3 * w + j, (px, py, c)))
        return local, remote

    def start(ins, outs, sems):
        local, remote = copies(ins, outs, sems)
        for cp in local + remote:
            cp.start()

    def finish(ins, outs, sems):
        local, remote = copies(ins, outs, sems)
        for cp in remote + local:
            cp.wait()

    out_shapes = [jax.ShapeDtypeStruct(s.shape, s.dtype) for s in sums]
    out_shapes.append(jax.ShapeDtypeStruct((4,) + small.shape, small.dtype))
    return Comm(list(sums) + [small], out_shapes, 3 * (n + 1), n + 1, start, finish)


_SMALL = ("ffn1_norm", "mix_norm", "ln_v_gain", "ln_v_bias", "spatial_w", "spatial_b", "gnorm_a", "gnorm_b",
          "cross_norm", "mem_norm", "ffn2_norm", "final_norm")
_BIG = ("ffn1_w_in", "ffn1_w_out", "w_mix_in", "w_mix_out", "w_cq", "w_ckv", "w_co", "ffn2_w_in", "ffn2_w_out")
_COL_SHARDED = ("ffn1_w_in", "w_mix_in", "w_ckv", "ffn2_w_in")
_ORDER = ("ffn1_norm", "ffn1_w_in", "ffn1_w_out", "mix_norm", "w_mix_in", "ln_v_gain", "ln_v_bias", "spatial_w",
          "spatial_b", "gnorm_a", "gnorm_b", "w_mix_out", "cross_norm", "mem_norm", "w_cq", "w_ckv", "w_co",
          "ffn2_norm", "ffn2_w_in", "ffn2_w_out", "final_norm")


_SMALL_PAD = 120


def _rows128(a):
    return a.reshape(-1, 128)


def _ffn_bwd(tag, n_in, a, hsw, dfb, w_in_g, w_out):
    d_w_out = mm_tn(hsw, dfb, f"{tag}_dwout")
    dhsw = mm_nt(dfb, w_out, F32, f"{tag}_dhsw")
    da = swiglu_bwd(a, dhsw, f"{tag}_swiglu_bwd")
    d_w_in = mm_tn_g(n_in, da, N_DEV, f"{tag}_dwin")
    dn = mm_nt_g(da, w_in_g, f"{tag}_dn")
    return d_w_in, d_w_out, dn


def kernel(x, mem, ffn1_norm, ffn1_w_in, ffn1_w_out, mix_norm, w_mix_in, ln_v_gain, ln_v_bias, spatial_w, spatial_b, gnorm_a, gnorm_b, w_mix_out, cross_norm, mem_norm, w_cq, w_ckv, w_co, ffn2_norm, ffn2_w_in, ffn2_w_out, final_norm, loss_target, m_ffn1_norm, m_ffn1_w_in, m_ffn1_w_out, m_mix_norm, m_w_mix_in, m_ln_v_gain, m_ln_v_bias, m_spatial_w, m_spatial_b, m_gnorm_a, m_gnorm_b, m_w_mix_out, m_cross_norm, m_mem_norm, m_w_cq, m_w_ckv, m_w_co, m_ffn2_norm, m_ffn2_w_in, m_ffn2_w_out, m_final_norm, v_ffn1_norm, v_ffn1_w_in, v_ffn1_w_out, v_mix_norm, v_w_mix_in, v_ln_v_gain, v_ln_v_bias, v_spatial_w, v_spatial_b, v_gnorm_a, v_gnorm_b, v_w_mix_out, v_cross_norm, v_mem_norm, v_w_cq, v_w_ckv, v_w_co, v_ffn2_norm, v_ffn2_w_in, v_ffn2_w_out, v_final_norm):
    given = dict(locals())
    wts = {k: given[k] for k in _ORDER}
    mom = {k: given["m_" + k] for k in _ORDER}
    var = {k: given["v_" + k] for k in _ORDER}

    D = D_MODEL
    xs = x.reshape(-1, D)
    mems = mem.reshape(-1, D)
    tgt = loss_target.reshape(-1, D)
    vec = lambda a: a.reshape(1, -1)
    g1, gmix, gcross, gmem, g2, gfin = (vec(wts[k]) for k in
                                        ("ffn1_norm", "mix_norm", "cross_norm", "mem_norm", "ffn2_norm", "final_norm"))
    ln_g, ln_b, ga, gb = (vec(wts[k]) for k in ("ln_v_gain", "ln_v_bias", "gnorm_a", "gnorm_b"))
    w_s = spatial_w.reshape(G_A, SGU_BLOCK, SGU_BLOCK)
    b_t = spatial_b.reshape(G_A, SGU_BLOCK).T

    shard2d = {k: wts[k].reshape(wts[k].shape[1:]) for k in _BIG}
    shards_b = [cast_bf16(shard2d[k], f"cast_{k}") for k in _BIG]
    gathered = dict(zip(_BIG, run_comm(comm_all_gather(shards_b), "comm_ag_weights")))
    full = {k: (gathered[k] if k in _COL_SHARDED else gathered[k].reshape(-1, gathered[k].shape[2])) for k in _BIG}

    n1 = rms_fwd(xs, g1, "f_n1")
    a1 = mm_nn_g(n1, full["ffn1_w_in"], BF16, "f_a1")
    hsw1 = swiglu_fwd(a1, "f_hsw1")
    h1 = mm_nn(hsw1, full["ffn1_w_out"], F32, "f_h1", scale=0.5, res=xs)
    n2 = rms_fwd(h1, gmix, "f_n2")
    z = mm_nn_g(n2, full["w_mix_in"], F32, "f_z")
    ya = sgu_fwd(z, ln_g, ln_b, w_s, b_t, "f_sgu")
    yb = sb_fwd(z, "f_sb")
    ycat = rmscat_fwd(ya, yb, ga, gb, "f_ycat")
    h2 = mm_nn(ycat, full["w_mix_out"], F32, "f_h2", res=h1)
    n3 = rms_fwd(h2, gcross, "f_n3")
    memn = rms_fwd(mems, gmem, "f_memn")
    qc = mm_nn(n3, full["w_cq"], BF16, "f_qc", scale=X_DH ** -0.5)
    kv = mm_nn_g(memn, full["w_ckv"], BF16, "f_kv")
    o = xattn_fwd(qc, kv, "f_xattn")
    h3 = mm_nn(o, full["w_co"], F32, "f_h3", res=h2)
    n4 = rms_fwd(h3, g2, "f_n4")
    a2 = mm_nn_g(n4, full["ffn2_w_in"], BF16, "f_a2")
    hsw2 = swiglu_fwd(a2, "f_hsw2")
    h4 = mm_nn(hsw2, full["ffn2_w_out"], F32, "f_h4", scale=0.5, res=h3)

    grads = {}
    loss_part, dh4, df2, grads["final_norm"] = loss_head(h4, tgt, gfin, "loss_head")
    grads["ffn2_w_in"], grads["ffn2_w_out"], dn4 = _ffn_bwd("b_ffn2", n4, a2, hsw2, df2, full["ffn2_w_in"], full["ffn2_w_out"])
    dh3, dh3b, grads["ffn2_norm"] = rms_bwd(dn4, h3, g2, dh4, 1.0, "b_n4")

    grads["w_co"] = mm_tn(o, dh3b, "b_dwco")
    do = mm_nt(dh3b, full["w_co"], BF16, "b_do")
    dqp, dkv = xattn_bwd(qc, kv, do, "b_xattn")
    grads["w_cq"] = mm_tn(n3, dqp, "b_dwcq")
    dn3 = mm_nt(dqp, full["w_cq"], F32, "b_dn3")
    grads["w_ckv"] = mm_tn_g(memn, dkv, N_DEV, "b_dwckv")
    dmemn = mm_nt_g(dkv, full["w_ckv"], "b_dmemn")
    _, _, grads["mem_norm"] = rms_bwd(dmemn, mems, gmem, None, 1.0, "b_memn")
    dh2, dh2b, grads["cross_norm"] = rms_bwd(dn3, h2, gcross, dh3, 1.0, "b_n3")

    grads["w_mix_out"] = mm_tn(ycat, dh2b, "b_dwmixout")
    dycat = mm_nt(dh2b, full["w_mix_out"], F32, "b_dycat")
    dya, dyb, grads["gnorm_a"], grads["gnorm_b"] = rmscat_bwd(dycat, ya, yb, ga, gb, "b_ycat")
    dza, grads["ln_v_gain"], grads["ln_v_bias"], grads["spatial_w"], grads["spatial_b"] = sgu_bwd(
        z, dya, ln_g, ln_b, w_s, b_t, "b_sgu")
    dq, dk, dv = sb_bwd(z, yb, dyb, "b_sb")
    dz = jnp.concatenate([dza, dq, dk, dv], axis=1)
    grads["w_mix_in"] = mm_tn_g(n2, dz, N_DEV, "b_dwmixin")
    dn2 = mm_nt_g(dz, full["w_mix_in"], "b_dn2")
    dh1, dh1b, grads["mix_norm"] = rms_bwd(dn2, h1, gmix, dh2, 0.5, "b_n2")

    grads["ffn1_w_in"], grads["ffn1_w_out"], dn1 = _ffn_bwd("b_ffn1", n1, a1, hsw1, dh1b, full["ffn1_w_in"], full["ffn1_w_out"])
    dx, _, grads["ffn1_norm"] = rms_bwd(dn1, xs, g1, dh1, 1.0, "b_n1")

    pack = lambda d: jnp.concatenate([_rows128(d[k]) for k in _SMALL] + [jnp.zeros((_SMALL_PAD, 128), F32)], axis=0)
    small_part = pack(grads)
    parts = [grads[k] if k in _COL_SHARDED else grads[k].reshape(N_DEV, -1, grads[k].shape[1]) for k in _BIG]
    core = lax.axis_index("c").astype(jnp.int32).reshape(1)
    *from_sibling, small_sibling = run_comm(comm_pairs(parts, small_part), "comm_pairs")
    sums = [pair_sum(p, s, core, f"pair_sum_{k}") for k, p, s in zip(_BIG, parts, from_sibling)]
    small_pair = add2(small_part, small_sibling, "pair_sum_small")
    *recv, small_all = run_comm(comm_chips(sums, small_pair), "comm_chips")

    out_g, out_d, out_m, out_v = {}, {}, {}, {}
    for k, r in zip(_BIG, recv):
        res = adamw(r, shard2d[k], mom[k].reshape(shard2d[k].shape), var[k].reshape(shard2d[k].shape), f"adamw_{k}")
        out_g[k], out_d[k], out_m[k], out_v[k] = (t.reshape(wts[k].shape) for t in res)
    res = adamw(small_all, pack(wts), pack(mom), pack(var), "adamw_small")
    row = 0
    for k in _SMALL:
        nrow = wts[k].size // 128
        for dst, t in zip((out_g, out_d, out_m, out_v), res):
            dst[k] = t[row:row + nrow].reshape(wts[k].shape)
        row += nrow

    loss = lax.psum(loss_part[0, 0], ("x", "y", "c"))
    grad_x = dx.reshape(x.shape)
    return (loss, grad_x, *[out_g[k] for k in _ORDER], *[out_d[k] for k in _ORDER],
            *[out_m[k] for k in _ORDER], *[out_v[k] for k in _ORDER])
```

```python
import functools
import math

import jax
import jax.numpy as jnp
from jax import lax
from jax.experimental import pallas as pl
from jax.experimental.pallas import tpu as pltpu

F32 = jnp.float32
BF16 = jnp.bfloat16

N_DEV = 8
D_MODEL = 2048
D_FF = 5632
W_A = 1024
G_A = 8
GA_DIM = 128
SGU_BLOCK = 128
CHUNK = 64
H_B = 8
DH_B = 128
Q_BLOCK = 128
X_HEADS = 4
X_DH = 512
N_MEM = 256
EPS = 1e-6

ADAM_LR = 0.001
ADAM_B1 = 0.9
ADAM_B2 = 0.999
ADAM_EPS = 1e-08
ADAM_WD = 0.01
ADAM_STEP = 10

VMEM_LIMIT = 56 * 2**20
ROW_TILE = 256

MESH = pl.DeviceIdType.MESH
ANY = pl.BlockSpec(memory_space=pl.ANY)

_NT = (((1,), (1,)), ((), ()))
_TN = (((0,), (0,)), ((), ()))


def _params(*sem):
    return pltpu.CompilerParams(dimension_semantics=sem, vmem_limit_bytes=VMEM_LIMIT)


def _zeros(ref):
    return jnp.zeros(ref.shape, ref.dtype)


def _pcall(comm, body, *, name, grid, in_specs, out_specs, out_shape, compiler_params, scratch_shapes=()):
    if comm is None:
        return pl.pallas_call(body, name=name, grid=grid, in_specs=in_specs, out_specs=out_specs, out_shape=out_shape,
                              scratch_shapes=list(scratch_shapes), compiler_params=compiler_params)
    multi = isinstance(out_shape, (list, tuple))
    out_shapes = list(out_shape) if multi else [out_shape]
    out_specs_l = list(out_specs) if multi else [out_specs]
    n_in, n_out, n_scr = len(in_specs), len(out_shapes), len(scratch_shapes)
    n_cin, n_cout = len(comm.arrays), len(comm.out_shapes)

    def with_comm(*refs):
        ins, refs = refs[:n_in], refs[n_in:]
        cins, refs = refs[:n_cin], refs[n_cin:]
        outs, refs = refs[:n_out], refs[n_out:]
        couts, refs = refs[:n_cout], refs[n_cout:]
        scr, sems = refs[:n_scr], refs[n_scr:]
        first = functools.reduce(jnp.logical_and, [pl.program_id(a) == 0 for a in range(len(grid))])
        last = functools.reduce(jnp.logical_and, [pl.program_id(a) == grid[a] - 1 for a in range(len(grid))])
        pl.when(first)(lambda: comm.start(cins, couts, sems))
        body(*ins, *outs, *scr)
        pl.when(last)(lambda: comm.finish(cins, couts, sems))

    call = pl.pallas_call(
        with_comm, name=name, grid=grid, in_specs=list(in_specs) + [ANY] * n_cin,
        out_specs=out_specs_l + [ANY] * n_cout, out_shape=out_shapes + comm.out_shapes,
        scratch_shapes=list(scratch_shapes) + comm.sem_shapes(), compiler_params=_params(*(("arbitrary",) * len(grid))))

    def run(*args):
        res = call(*args, *comm.arrays)
        main = res[:n_out]
        return (list(main) if multi else main[0]), list(res[n_out:])

    return run


def _dot(a, b):
    return jnp.dot(a, b, preferred_element_type=F32)


def _dot_nt(a, b):
    return lax.dot_general(a, b, _NT, preferred_element_type=F32)


def _dot_tn(a, b):
    return lax.dot_general(a, b, _TN, preferred_element_type=F32)


def mm_nn_g(a, bg, out_dtype, name, tm=512, comm=None):
    M, K = a.shape
    G, _, n = bg.shape
    tm = min(tm, M)

    def body(a_ref, b_ref, o_ref):
        o_ref[...] = _dot(a_ref[...], b_ref[...]).astype(o_ref.dtype)

    return _pcall(
        comm, body, name=name, grid=(G, M // tm),
        in_specs=[pl.BlockSpec((tm, K), lambda g, m: (m, 0)),
                  pl.BlockSpec((None, K, n), lambda g, m: (g, 0, 0))],
        out_specs=pl.BlockSpec((tm, n), lambda g, m: (m, g)),
        out_shape=jax.ShapeDtypeStruct((M, G * n), out_dtype),
        compiler_params=_params("parallel", "parallel"),
    )(a, bg)


def mm_nn(a, b, out_dtype, name, tm=512, tn=1024, scale=1.0, res=None, comm=None):
    M, K = a.shape
    _, N = b.shape
    tm, tn = min(tm, M), min(tn, N)

    def body(*refs):
        if res is None:
            a_ref, b_ref, o_ref = refs
            acc = _dot(a_ref[...], b_ref[...])
            o_ref[...] = (acc * scale if scale != 1.0 else acc).astype(o_ref.dtype)
        else:
            a_ref, b_ref, r_ref, o_ref = refs
            o_ref[...] = (r_ref[...] + scale * _dot(a_ref[...], b_ref[...])).astype(o_ref.dtype)

    in_specs = [pl.BlockSpec((tm, K), lambda n, m: (m, 0)),
                pl.BlockSpec((K, tn), lambda n, m: (0, n))]
    args = [a, b]
    if res is not None:
        in_specs.append(pl.BlockSpec((tm, tn), lambda n, m: (m, n)))
        args.append(res)
    return _pcall(
        comm, body, name=name, grid=(N // tn, M // tm),
        in_specs=in_specs,
        out_specs=pl.BlockSpec((tm, tn), lambda n, m: (m, n)),
        out_shape=jax.ShapeDtypeStruct((M, N), out_dtype),
        compiler_params=_params("parallel", "parallel"),
    )(*args)


def mm_nt_g(dy, bg, name, tm=512, comm=None):
    M, _ = dy.shape
    G, K, n = bg.shape
    tm = min(tm, M)

    def body(dy_ref, b_ref, o_ref):
        part = _dot_nt(dy_ref[...], b_ref[...])

        @pl.when(pl.program_id(1) == 0)
        def _():
            o_ref[...] = part

        @pl.when(pl.program_id(1) > 0)
        def _():
            o_ref[...] += part

    return _pcall(
        comm, body, name=name, grid=(M // tm, G),
        in_specs=[pl.BlockSpec((tm, n), lambda m, g: (m, g)),
                  pl.BlockSpec((None, K, n), lambda m, g: (g, 0, 0))],
        out_specs=pl.BlockSpec((tm, K), lambda m, g: (m, 0)),
        out_shape=jax.ShapeDtypeStruct((M, K), F32),
        compiler_params=_params("parallel", "arbitrary"),
    )(dy, bg)


def mm_nt(dy, b, out_dtype, name, tk=512, comm=None):
    M, N = dy.shape
    K, _ = b.shape

    def body(dy_ref, b_ref, o_ref):
        o_ref[...] = _dot_nt(dy_ref[...], b_ref[...]).astype(o_ref.dtype)

    return _pcall(
        comm, body, name=name, grid=(K // tk,),
        in_specs=[pl.BlockSpec((M, N), lambda k: (0, 0)),
                  pl.BlockSpec((tk, N), lambda k: (k, 0))],
        out_specs=pl.BlockSpec((M, tk), lambda k: (0, k)),
        out_shape=jax.ShapeDtypeStruct((M, K), out_dtype),
        compiler_params=_params("parallel"),
    )(dy, b)


def mm_tn_g(x, dy, G, name, tk=512, comm=None):
    M, K = x.shape
    n = dy.shape[1] // G

    def body(x_ref, dy_ref, o_ref):
        o_ref[...] = _dot_tn(x_ref[...], dy_ref[...]).astype(o_ref.dtype)

    return _pcall(
        comm, body, name=name, grid=(G, K // tk),
        in_specs=[pl.BlockSpec((M, tk), lambda g, k: (0, k)),
                  pl.BlockSpec((M, n), lambda g, k: (0, g))],
        out_specs=pl.BlockSpec((None, tk, n), lambda g, k: (g, k, 0)),
        out_shape=jax.ShapeDtypeStruct((G, K, n), BF16),
        compiler_params=_params("parallel", "parallel"),
    )(x, dy)


def mm_tn(x, dy, name, tk=512, comm=None):
    M, K = x.shape
    _, N = dy.shape

    def body(x_ref, dy_ref, o_ref):
        o_ref[...] = _dot_tn(x_ref[...], dy_ref[...]).astype(o_ref.dtype)

    return _pcall(
        comm, body, name=name, grid=(K // tk,),
        in_specs=[pl.BlockSpec((M, tk), lambda k: (0, k)),
                  pl.BlockSpec((M, N), lambda k: (0, 0))],
        out_specs=pl.BlockSpec((tk, N), lambda k: (k, 0)),
        out_shape=jax.ShapeDtypeStruct((K, N), BF16),
        compiler_params=_params("parallel"),
    )(x, dy)


def _rstd(x):
    return lax.rsqrt(jnp.mean(x * x, axis=-1, keepdims=True) + EPS)


def _rms_bwd(dn, xhat, r, g):
    dxhat = dn * g
    return r * (dxhat - xhat * jnp.mean(dxhat * xhat, axis=-1, keepdims=True))


def _row_spec(tr, width, col=0):
    return pl.BlockSpec((tr, width), lambda i: (i, col))


def _vec_spec(width):
    return pl.BlockSpec((1, width), lambda i: (0, 0))


def rms_fwd(x, g, name, comm=None):
    M, D = x.shape
    tr = min(ROW_TILE, M)

    def body(x_ref, g_ref, o_ref):
        xv = x_ref[...]
        o_ref[...] = (xv * _rstd(xv) * g_ref[...]).astype(o_ref.dtype)

    return _pcall(
        comm, body, name=name, grid=(M // tr,),
        in_specs=[_row_spec(tr, D), _vec_spec(D)],
        out_specs=_row_spec(tr, D),
        out_shape=jax.ShapeDtypeStruct((M, D), BF16),
        compiler_params=_params("parallel"),
    )(x, g)


def rms_bwd(dn, h, g, dres, copy_scale, name, comm=None):
    M, D = h.shape
    tr = min(ROW_TILE, M)
    has_res = dres is not None

    def body(*refs):
        if has_res:
            dn_ref, h_ref, g_ref, dres_ref, dh_ref, dhb_ref, dg_ref = refs
        else:
            dn_ref, h_ref, g_ref, dh_ref, dhb_ref, dg_ref = refs
        hv = h_ref[...]
        r = _rstd(hv)
        xhat = hv * r
        dn = dn_ref[...]
        part = jnp.sum(dn * xhat, axis=0, keepdims=True)

        @pl.when(pl.program_id(0) == 0)
        def _():
            dg_ref[...] = part

        @pl.when(pl.program_id(0) > 0)
        def _():
            dg_ref[...] += part

        dh = _rms_bwd(dn, xhat, r, g_ref[...])
        if has_res:
            dh = dh + dres_ref[...]
        dh_ref[...] = dh
        dhb_ref[...] = (dh * copy_scale if copy_scale != 1.0 else dh).astype(BF16)

    in_specs = [_row_spec(tr, D), _row_spec(tr, D), _vec_spec(D)]
    args = [dn, h, g]
    if has_res:
        in_specs.append(_row_spec(tr, D))
        args.append(dres)
    return _pcall(
        comm, body, name=name, grid=(M // tr,),
        in_specs=in_specs,
        out_specs=[_row_spec(tr, D), _row_spec(tr, D), _vec_spec(D)],
        out_shape=[jax.ShapeDtypeStruct((M, D), F32), jax.ShapeDtypeStruct((M, D), BF16),
                   jax.ShapeDtypeStruct((1, D), F32)],
        compiler_params=_params("arbitrary"),
    )(*args)


def _sigmoid(x):
    return 1.0 / (1.0 + jnp.exp(-x))


def swiglu_fwd(a, name, comm=None):
    M, F2 = a.shape
    F = F2 // 2
    tr = min(ROW_TILE, M)

    def body(g_ref, u_ref, o_ref):
        gt = g_ref[...].astype(F32)
        o_ref[...] = (gt * _sigmoid(gt) * u_ref[...].astype(F32)).astype(o_ref.dtype)

    return _pcall(
        comm, body, name=name, grid=(M // tr,),
        in_specs=[_row_spec(tr, F, 0), _row_spec(tr, F, 1)],
        out_specs=_row_spec(tr, F),
        out_shape=jax.ShapeDtypeStruct((M, F), BF16),
        compiler_params=_params("parallel"),
    )(a, a)


def swiglu_bwd(a, dh, name, comm=None):
    M, F2 = a.shape
    F = F2 // 2
    tr = min(ROW_TILE, M)

    def body(g_ref, u_ref, dh_ref, da_ref):
        gt = g_ref[...].astype(F32)
        up = u_ref[...].astype(F32)
        dh = dh_ref[...]
        sg = _sigmoid(gt)
        da_ref[:, :F] = (dh * up * (sg * (1.0 + gt * (1.0 - sg)))).astype(BF16)
        da_ref[:, F:] = (dh * (gt * sg)).astype(BF16)

    return _pcall(
        comm, body, name=name, grid=(M // tr,),
        in_specs=[_row_spec(tr, F, 0), _row_spec(tr, F, 1), _row_spec(tr, F)],
        out_specs=_row_spec(tr, F2),
        out_shape=jax.ShapeDtypeStruct((M, F2), BF16),
        compiler_params=_params("parallel"),
    )(a, a, dh)


def rmscat_fwd(ya, yb, ga, gb, name, comm=None):
    M, W = ya.shape
    tr = min(ROW_TILE, M)

    def body(ya_ref, yb_ref, ga_ref, gb_ref, o_ref):
        a = ya_ref[...]
        b = yb_ref[...]
        o_ref[:, :W] = (a * _rstd(a) * ga_ref[...]).astype(BF16)
        o_ref[:, W:] = (b * _rstd(b) * gb_ref[...]).astype(BF16)

    return _pcall(
        comm, body, name=name, grid=(M // tr,),
        in_specs=[_row_spec(tr, W), _row_spec(tr, W), _vec_spec(W), _vec_spec(W)],
        out_specs=_row_spec(tr, 2 * W),
        out_shape=jax.ShapeDtypeStruct((M, 2 * W), BF16),
        compiler_params=_params("parallel"),
    )(ya, yb, ga, gb)


def rmscat_bwd(dycat, ya, yb, ga, gb, name, comm=None):
    M, W = ya.shape
    tr = min(ROW_TILE, M)

    def body(dc_ref, ya_ref, yb_ref, ga_ref, gb_ref, dya_ref, dyb_ref, dga_ref, dgb_ref):
        first = pl.program_id(0) == 0
        for y_ref, g_ref, dy_ref, dg_ref, lo in ((ya_ref, ga_ref, dya_ref, dga_ref, 0),
                                                 (yb_ref, gb_ref, dyb_ref, dgb_ref, W)):
            yv = y_ref[...]
            r = _rstd(yv)
            xhat = yv * r
            dn = dc_ref[:, lo:lo + W]
            part = jnp.sum(dn * xhat, axis=0, keepdims=True)

            @pl.when(first)
            def _():
                dg_ref[...] = part

            @pl.when(jnp.logical_not(first))
            def _():
                dg_ref[...] += part

            dy_ref[...] = _rms_bwd(dn, xhat, r, g_ref[...])

    return _pcall(
        comm, body, name=name, grid=(M // tr,),
        in_specs=[_row_spec(tr, 2 * W), _row_spec(tr, W), _row_spec(tr, W), _vec_spec(W), _vec_spec(W)],
        out_specs=[_row_spec(tr, W), _row_spec(tr, W), _vec_spec(W), _vec_spec(W)],
        out_shape=[jax.ShapeDtypeStruct((M, W), F32), jax.ShapeDtypeStruct((M, W), F32),
                   jax.ShapeDtypeStruct((1, W), F32), jax.ShapeDtypeStruct((1, W), F32)],
        compiler_params=_params("arbitrary"),
    )(dycat, ya, yb, ga, gb)


def loss_head(h, target, g, name, comm=None):
    M, D = h.shape
    tr = min(ROW_TILE, M)

    def body(h_ref, t_ref, g_ref, loss_ref, dh_ref, dhb_ref, dg_ref):
        hv = h_ref[...]
        gv = g_ref[...]
        r = _rstd(hv)
        xhat = hv * r
        err = xhat * gv - t_ref[...]
        lsum = jnp.sum(jnp.sum(err * err, axis=1, keepdims=True), axis=0, keepdims=True) * (0.5 / D)
        dy = err * (1.0 / D)
        part = jnp.sum(dy * xhat, axis=0, keepdims=True)

        @pl.when(pl.program_id(0) == 0)
        def _():
            dg_ref[...] = part
            loss_ref[...] = _zeros(loss_ref) + lsum

        @pl.when(pl.program_id(0) > 0)
        def _():
            dg_ref[...] += part
            loss_ref[...] += lsum

        dh = _rms_bwd(dy, xhat, r, gv)
        dh_ref[...] = dh
        dhb_ref[...] = (0.5 * dh).astype(BF16)

    return _pcall(
        comm, body, name=name, grid=(M // tr,),
        in_specs=[_row_spec(tr, D), _row_spec(tr, D), _vec_spec(D)],
        out_specs=[pl.BlockSpec((8, 128), lambda i: (0, 0)), _row_spec(tr, D), _row_spec(tr, D), _vec_spec(D)],
        out_shape=[jax.ShapeDtypeStruct((8, 128), F32), jax.ShapeDtypeStruct((M, D), F32),
                   jax.ShapeDtypeStruct((M, D), BF16), jax.ShapeDtypeStruct((1, D), F32)],
        compiler_params=_params("arbitrary"),
    )(h, target, g)


_GELU_C = math.sqrt(2.0 / math.pi)


def _gelu(x):
    return 0.5 * x * (1.0 + jnp.tanh(_GELU_C * (x + 0.044715 * (x * x * x))))


def _gelu_grad(x):
    t = jnp.tanh(_GELU_C * (x + 0.044715 * (x * x * x)))
    return 0.5 * (1.0 + t) + 0.5 * x * (1.0 - t * t) * (_GELU_C * (1.0 + 3.0 * 0.044715 * (x * x)))


def _sgu_mask():
    t = lax.broadcasted_iota(jnp.int32, (SGU_BLOCK, SGU_BLOCK), 0) // CHUNK
    s = lax.broadcasted_iota(jnp.int32, (SGU_BLOCK, SGU_BLOCK), 1) // CHUNK
    return s <= t


def _layernorm_stats(v):
    mu = jnp.mean(v, axis=-1, keepdims=True)
    cen = v - mu
    rstd = lax.rsqrt(jnp.mean(cen * cen, axis=-1, keepdims=True) + EPS)
    return cen * rstd, rstd


def sgu_fwd(z, ln_g, ln_b, w_s, b_t, name, comm=None):
    S = z.shape[0]

    def body(zu_ref, zv_ref, lg_ref, lb_ref, w_ref, bt_ref, o_ref):
        mask = _sgu_mask()
        for g in range(G_A):
            cols = slice(g * GA_DIM, (g + 1) * GA_DIM)
            u = _gelu(zu_ref[:, cols])
            vhat, _ = _layernorm_stats(_gelu(zv_ref[:, cols]))
            vln = vhat * lg_ref[:, cols] + lb_ref[:, cols]
            w = jnp.where(mask, w_ref[g], 0.0).astype(BF16)
            mixed = _dot(w, vln.astype(BF16)) + bt_ref[:, g:g + 1]
            o_ref[:, cols] = u * mixed

    return _pcall(
        comm, body, name=name, grid=(S // SGU_BLOCK,),
        in_specs=[_row_spec(SGU_BLOCK, W_A, 0), _row_spec(SGU_BLOCK, W_A, 1), _vec_spec(W_A), _vec_spec(W_A),
                  pl.BlockSpec((G_A, SGU_BLOCK, SGU_BLOCK), lambda i: (0, 0, 0)),
                  pl.BlockSpec((SGU_BLOCK, G_A), lambda i: (0, 0))],
        out_specs=_row_spec(SGU_BLOCK, W_A),
        out_shape=jax.ShapeDtypeStruct((S, W_A), F32),
        compiler_params=_params("parallel"),
    )(z, z, ln_g, ln_b, w_s, b_t)


def sgu_bwd(z, dya, ln_g, ln_b, w_s, b_t, name, comm=None):
    S = z.shape[0]
    nblk = S // SGU_BLOCK

    def body(zu_ref, zv_ref, dy_ref, lg_ref, lb_ref, w_ref, bt_ref,
             dz_ref, dlg_ref, dlb_ref, dw_ref, db_ref, dmix_acc):
        step = pl.program_id(0)
        mask = _sgu_mask()

        @pl.when(step == 0)
        def _():
            dlg_ref[...] = _zeros(dlg_ref)
            dlb_ref[...] = _zeros(dlb_ref)
            dw_ref[...] = _zeros(dw_ref)
            dmix_acc[...] = _zeros(dmix_acc)

        for g in range(G_A):
            cols = slice(g * GA_DIM, (g + 1) * GA_DIM)
            zu = zu_ref[:, cols]
            zv = zv_ref[:, cols]
            u = _gelu(zu)
            vhat, rstd = _layernorm_stats(_gelu(zv))
            lg = lg_ref[:, cols]
            vln = (vhat * lg + lb_ref[:, cols]).astype(BF16)
            w = jnp.where(mask, w_ref[g], 0.0)
            mixed = _dot(w.astype(BF16), vln) + bt_ref[:, g:g + 1]
            dy = dy_ref[:, cols]
            du = dy * mixed
            dmixed = dy * u
            dmixed_b = dmixed.astype(BF16)
            dmix_acc[g] += dmixed
            dw_ref[g] += jnp.where(mask, _dot_nt(dmixed_b, vln), 0.0)
            dvln = _dot(w.T.astype(BF16), dmixed_b)
            dlb_ref[:, cols] += jnp.sum(dvln, axis=0, keepdims=True)
            dlg_ref[:, cols] += jnp.sum(dvln * vhat, axis=0, keepdims=True)
            dvhat = dvln * lg
            dv = rstd * (dvhat - jnp.mean(dvhat, axis=-1, keepdims=True)
                         - vhat * jnp.mean(dvhat * vhat, axis=-1, keepdims=True))
            dz_ref[:, cols] = (du * _gelu_grad(zu)).astype(BF16)
            dz_ref[:, W_A + g * GA_DIM:W_A + (g + 1) * GA_DIM] = (dv * _gelu_grad(zv)).astype(BF16)

        @pl.when(step == nblk - 1)
        def _():
            for g in range(G_A):
                db_ref[g] = jnp.sum(dmix_acc[g], axis=1, keepdims=True)

    whole3 = lambda shape: pl.BlockSpec(shape, lambda i: (0, 0, 0))
    return _pcall(
        comm, body, name=name, grid=(nblk,),
        in_specs=[_row_spec(SGU_BLOCK, W_A, 0), _row_spec(SGU_BLOCK, W_A, 1), _row_spec(SGU_BLOCK, W_A),
                  _vec_spec(W_A), _vec_spec(W_A), whole3((G_A, SGU_BLOCK, SGU_BLOCK)),
                  pl.BlockSpec((SGU_BLOCK, G_A), lambda i: (0, 0))],
        out_specs=[_row_spec(SGU_BLOCK, 2 * W_A), _vec_spec(W_A), _vec_spec(W_A),
                   whole3((G_A, SGU_BLOCK, SGU_BLOCK)), whole3((G_A, SGU_BLOCK, 1))],
        out_shape=[jax.ShapeDtypeStruct((S, 2 * W_A), BF16), jax.ShapeDtypeStruct((1, W_A), F32),
                   jax.ShapeDtypeStruct((1, W_A), F32), jax.ShapeDtypeStruct((G_A, SGU_BLOCK, SGU_BLOCK), F32),
                   jax.ShapeDtypeStruct((G_A, SGU_BLOCK, 1), F32)],
        scratch_shapes=[pltpu.VMEM((G_A, SGU_BLOCK, SGU_BLOCK), F32)],
        compiler_params=_params("arbitrary"),
    )(z, z, dya, ln_g, ln_b, w_s, b_t)


def _log_sigmoid(z):
    return jnp.minimum(z, 0.0) - jnp.log(1.0 + jnp.exp(-jnp.abs(z)))


def _suffix_sum(x, upper):
    hi = x.astype(BF16)
    rem = x - hi.astype(F32)
    mid = rem.astype(BF16)
    lo = (rem - mid.astype(F32)).astype(BF16)
    return _dot(hi, upper) + _dot(mid, upper) + _dot(lo, upper)


SB_ROWS = 512
_SB_SUB = SB_ROWS // Q_BLOCK


def _sb_upper():
    row = lax.broadcasted_iota(jnp.int32, (Q_BLOCK, Q_BLOCK), 0)
    col = lax.broadcasted_iota(jnp.int32, (Q_BLOCK, Q_BLOCK), 1)
    return (row > col).astype(BF16)


def _sb_sweep(step, tile, state):
    row = lax.broadcasted_iota(jnp.int32, (SB_ROWS, Q_BLOCK), 0)
    col = lax.broadcasted_iota(jnp.int32, (SB_ROWS, Q_BLOCK), 1)
    for r in reversed(range(_SB_SUB)):
        state = tile(step * _SB_SUB + r, state, col + r * Q_BLOCK < row)

    def group(g, state):
        base = (step - 1 - g) * _SB_SUB
        for r in reversed(range(_SB_SUB)):
            state = tile(base + r, state, None)
        return state

    return lax.fori_loop(0, step, group, state)


def _sb_col(part):
    return (2 * W_A + part * (H_B * DH_B)) // DH_B


def _sb_q_spec():
    return pl.BlockSpec((SB_ROWS, DH_B), lambda h, i: (i, _sb_col(0) + h))


def _sb_kv_spec(S, part):
    return pl.BlockSpec((S, DH_B), lambda h, i: (0, _sb_col(part) + h))


def sb_fwd(z, name, comm=None):
    S = z.shape[0]
    scale = DH_B ** -0.5

    def body(q_ref, k_ref, v_ref, o_ref):
        step = pl.program_id(1)
        q = (q_ref[...] * scale).astype(BF16)
        upper = _sb_upper()

        def tile(j, carry, causal):
            rows = pl.ds(pl.multiple_of(j * Q_BLOCK, Q_BLOCK), Q_BLOCK)
            zz = _dot_nt(q, k_ref[rows, :].astype(BF16))
            lb = _log_sigmoid(zz)
            l1m = lb - zz
            if causal is not None:
                l1m = jnp.where(causal, l1m, 0.0)
            a = jnp.exp(lb + _suffix_sum(l1m, upper) + carry)
            if causal is not None:
                a = jnp.where(causal, a, 0.0)
            o_ref[...] += _dot(a.astype(BF16), v_ref[rows, :].astype(BF16))
            return carry + jnp.sum(l1m, axis=1, keepdims=True)

        o_ref[...] = _zeros(o_ref)
        carry = _sb_sweep(step, tile, jnp.zeros((SB_ROWS, 1), F32))

    return _pcall(
        comm, body, name=name, grid=(H_B, S // SB_ROWS),
        in_specs=[_sb_q_spec(), _sb_kv_spec(S, 1), _sb_kv_spec(S, 2)],
        out_specs=pl.BlockSpec((SB_ROWS, DH_B), lambda h, i: (i, h)),
        out_shape=jax.ShapeDtypeStruct((S, H_B * DH_B), F32),
        compiler_params=_params("parallel", "parallel"),
    )(z, z, z)


def sb_bwd(z, out, dout, name, comm=None):
    S = z.shape[0]
    nstep = S // SB_ROWS
    scale = DH_B ** -0.5

    def body(q_ref, k_ref, v_ref, o_ref, do_ref, dq_ref, dk_ref, dv_ref, dq_acc, dk_acc, dv_acc):
        step = pl.program_id(1)

        @pl.when(step == 0)
        def _():
            dk_acc[...] = _zeros(dk_acc)
            dv_acc[...] = _zeros(dv_acc)

        q = (q_ref[...] * scale).astype(BF16)
        do_b = do_ref[...].astype(BF16)
        delta = jnp.sum(do_b.astype(F32) * o_ref[...], axis=1, keepdims=True)
        upper = _sb_upper()
        dq_acc[...] = _zeros(dq_acc)

        def tile(j, state, causal):
            c_l1m, c_g = state
            rows = pl.ds(pl.multiple_of(j * Q_BLOCK, Q_BLOCK), Q_BLOCK)
            k_j = k_ref[rows, :].astype(BF16)
            zz = _dot_nt(q, k_j)
            lb = _log_sigmoid(zz)
            l1m = lb - zz
            if causal is not None:
                l1m = jnp.where(causal, l1m, 0.0)
            a = jnp.exp(lb + _suffix_sum(l1m, upper) + c_l1m)
            if causal is not None:
                a = jnp.where(causal, a, 0.0)
            a_b = a.astype(BF16)
            dv_acc[rows, :] += _dot_tn(a_b, do_b)
            gmat = a_b.astype(F32) * _dot_nt(do_b, v_ref[rows, :].astype(BF16))
            before = delta - gmat - (_suffix_sum(gmat, upper) + c_g)
            sig = jnp.exp(lb)
            dz = gmat * (1.0 - sig) - sig * before
            if causal is not None:
                dz = jnp.where(causal, dz, 0.0)
            dz_b = dz.astype(BF16)
            dk_acc[rows, :] += _dot_tn(dz_b, q)
            dq_acc[...] += _dot(dz_b, k_j)
            return c_l1m + jnp.sum(l1m, axis=1, keepdims=True), c_g + jnp.sum(gmat, axis=1, keepdims=True)

        zero = jnp.zeros((SB_ROWS, 1), F32)
        _sb_sweep(step, tile, (zero, zero))
        dq_ref[...] = (dq_acc[...] * scale).astype(BF16)

        @pl.when(step == nstep - 1)
        def _():
            dk_ref[...] = dk_acc[...].astype(BF16)
            dv_ref[...] = dv_acc[...].astype(BF16)

    rows_spec = pl.BlockSpec((SB_ROWS, DH_B), lambda h, i: (i, h))
    head_spec = pl.BlockSpec((S, DH_B), lambda h, i: (0, h))
    out_sds = jax.ShapeDtypeStruct((S, H_B * DH_B), BF16)
    return _pcall(
        comm, body, name=name, grid=(H_B, nstep),
        in_specs=[_sb_q_spec(), _sb_kv_spec(S, 1), _sb_kv_spec(S, 2), rows_spec, rows_spec],
        out_specs=[rows_spec, head_spec, head_spec],
        out_shape=[out_sds, out_sds, out_sds],
        scratch_shapes=[pltpu.VMEM((SB_ROWS, DH_B), F32)] + [pltpu.VMEM((S, DH_B), F32)] * 2,
        compiler_params=_params("parallel", "arbitrary"),
    )(z, z, z, out, dout)


def _softmax(s):
    e = jnp.exp(s - jnp.max(s, axis=-1, keepdims=True))
    return e / jnp.sum(e, axis=-1, keepdims=True)


def xattn_fwd(qc, kv, name, comm=None):
    S, D = qc.shape
    tr = min(ROW_TILE, S)

    def body(q_ref, kv_ref, o_ref):
        for h in range(X_HEADS):
            cols = slice(h * X_DH, (h + 1) * X_DH)
            p = _softmax(_dot_nt(q_ref[:, cols], kv_ref[:, cols]))
            o_ref[:, cols] = _dot(p.astype(BF16), kv_ref[:, D + h * X_DH:D + (h + 1) * X_DH]).astype(BF16)

    return _pcall(
        comm, body, name=name, grid=(S // tr,),
        in_specs=[_row_spec(tr, D), pl.BlockSpec((N_MEM, 2 * D), lambda i: (0, 0))],
        out_specs=_row_spec(tr, D),
        out_shape=jax.ShapeDtypeStruct((S, D), BF16),
        compiler_params=_params("parallel"),
    )(qc, kv)


def xattn_bwd(qc, kv, do, name, comm=None):
    S, D = qc.shape
    tr = min(ROW_TILE, S)
    nstep = S // tr
    scale = X_DH ** -0.5

    def body(q_ref, kv_ref, do_ref, dq_ref, dkv_ref, acc):
        step = pl.program_id(0)

        @pl.when(step == 0)
        def _():
            acc[...] = _zeros(acc)

        for h in range(X_HEADS):
            cols = slice(h * X_DH, (h + 1) * X_DH)
            vcols = slice(D + h * X_DH, D + (h + 1) * X_DH)
            q = q_ref[:, cols]
            k = kv_ref[:, cols]
            do_h = do_ref[:, cols]
            p = _softmax(_dot_nt(q, k))
            dp = _dot_nt(do_h, kv_ref[:, vcols])
            acc[:, vcols] += _dot_tn(p.astype(BF16), do_h)
            ds = (p * (dp - jnp.sum(p * dp, axis=-1, keepdims=True))).astype(BF16)
            dq_ref[:, cols] = (_dot(ds, k) * scale).astype(BF16)
            acc[:, cols] += _dot_tn(ds, q)

        @pl.when(step == nstep - 1)
        def _():
            dkv_ref[...] = acc[...].astype(BF16)

    whole = pl.BlockSpec((N_MEM, 2 * D), lambda i: (0, 0))
    return _pcall(
        comm, body, name=name, grid=(nstep,),
        in_specs=[_row_spec(tr, D), whole, _row_spec(tr, D)],
        out_specs=[_row_spec(tr, D), whole],
        out_shape=[jax.ShapeDtypeStruct((S, D), BF16), jax.ShapeDtypeStruct((N_MEM, 2 * D), BF16)],
        scratch_shapes=[pltpu.VMEM((N_MEM, 2 * D), F32)],
        compiler_params=_params("arbitrary"),
    )(qc, kv, do)


def _row_tile(rows, cap=128):
    return max(t for t in range(16, cap + 1, 16) if rows % t == 0)


def cast_bf16(w, name, comm=None):
    R, C = w.shape
    tr = _row_tile(R, 256)

    def body(w_ref, o_ref):
        o_ref[...] = w_ref[...].astype(BF16)

    return _pcall(
        comm, body, name=name, grid=(R // tr,),
        in_specs=[_row_spec(tr, C)], out_specs=_row_spec(tr, C),
        out_shape=jax.ShapeDtypeStruct((R, C), BF16),
        compiler_params=_params("parallel"),
    )(w)


def adamw(parts, w, m, v, name, comm=None):
    R, C = w.shape
    n_parts = parts.shape[0]
    tr = _row_tile(R)
    c1 = 1.0 - ADAM_B1 ** ADAM_STEP
    c2 = 1.0 - ADAM_B2 ** ADAM_STEP

    def body(p_ref, w_ref, m_ref, v_ref, g_ref, d_ref, mo_ref, vo_ref):
        g = p_ref[0].astype(F32)
        for p in range(1, n_parts):
            g = g + p_ref[p].astype(F32)
        m_new = ADAM_B1 * m_ref[...] + (1.0 - ADAM_B1) * g
        v_new = ADAM_B2 * v_ref[...] + (1.0 - ADAM_B2) * (g * g)
        g_ref[...] = g
        mo_ref[...] = m_new
        vo_ref[...] = v_new
        d_ref[...] = -ADAM_LR * ((m_new / c1) / (jnp.sqrt(v_new / c2) + ADAM_EPS) + ADAM_WD * w_ref[...])

    spec = _row_spec(tr, C)
    sds = jax.ShapeDtypeStruct((R, C), F32)
    return _pcall(
        comm, body, name=name, grid=(R // tr,),
        in_specs=[pl.BlockSpec((n_parts, tr, C), lambda i: (0, i, 0)), spec, spec, spec],
        out_specs=[spec, spec, spec, spec],
        out_shape=[sds, sds, sds, sds],
        compiler_params=_params("parallel"),
    )(parts, w, m, v)


def pair_sum(parts, from_sibling, core, name):
    _, R, C = parts.shape
    tr = _row_tile(R, 256)

    def body(core_ref, p_ref, s_ref, o_ref):
        o_ref[...] = (p_ref[...].astype(F32) + s_ref[...].astype(F32)).astype(o_ref.dtype)

    return pl.pallas_call(
        body, name=name,
        grid_spec=pltpu.PrefetchScalarGridSpec(
            num_scalar_prefetch=1, grid=(4, R // tr),
            in_specs=[pl.BlockSpec((None, tr, C), lambda q, i, core_ref: (2 * q + core_ref[0], i, 0)),
                      pl.BlockSpec((None, tr, C), lambda q, i, core_ref: (q, i, 0))],
            out_specs=pl.BlockSpec((None, tr, C), lambda q, i, core_ref: (q, i, 0))),
        out_shape=jax.ShapeDtypeStruct((4, R, C), BF16),
        compiler_params=_params("parallel", "parallel"),
    )(core, parts, from_sibling)


def add2(a, b, name, comm=None):
    R, C = a.shape
    tr = _row_tile(R, 256)

    def body(a_ref, b_ref, o_ref):
        o_ref[...] = a_ref[...] + b_ref[...]

    spec = _row_spec(tr, C)
    return _pcall(
        comm, body, name=name, grid=(R // tr,), in_specs=[spec, spec], out_specs=spec,
        out_shape=jax.ShapeDtypeStruct((R, C), F32), compiler_params=_params("parallel"),
    )(a, b)


def _place():
    return lax.axis_index("x"), lax.axis_index("y"), lax.axis_index("c")


class Comm:
    def __init__(self, arrays, out_shapes, n_remote, n_local, start, finish):
        self.arrays, self.out_shapes = list(arrays), list(out_shapes)
        self.n_remote, self.n_local = n_remote, max(n_local, 1)
        self.start, self.finish = start, finish

    def sem_shapes(self):
        return [pltpu.SemaphoreType.DMA((self.n_remote,)), pltpu.SemaphoreType.DMA((self.n_remote,)),
                pltpu.SemaphoreType.DMA((self.n_local,))]


class _Shifted:
    def __init__(self, ref, offset):
        self.ref, self.offset = ref, offset

    @property
    def at(self):
        return self

    def __getitem__(self, k):
        return self.ref.at[self.offset + k]


def merge_comms(comms):
    comms = [c for c in comms if c is not None]
    if not comms:
        return None

    def each(method):
        def run(ins, outs, sems):
            i = o = r = l = 0
            for c in comms:
                sub = (_Shifted(sems[0], r), _Shifted(sems[1], r), _Shifted(sems[2], l))
                getattr(c, method)(ins[i:i + len(c.arrays)], outs[o:o + len(c.out_shapes)], sub)
                i, o, r, l = i + len(c.arrays), o + len(c.out_shapes), r + c.n_remote, l + c.n_local
        return run

    merged = Comm([a for c in comms for a in c.arrays], [s for c in comms for s in c.out_shapes],
                  sum(c.n_remote for c in comms), sum(c.n_local for c in comms), each("start"), each("finish"))
    merged.sizes = [len(c.out_shapes) for c in comms]
    return merged


def split_results(comm, results):
    out, i = [], 0
    for n in comm.sizes:
        out.append(list(results[i:i + n]))
        i += n
    return out


def run_comm(comm, name):
    n_in, n_out = len(comm.arrays), len(comm.out_shapes)

    def body(*refs):
        ins, outs, sems = refs[:n_in], refs[n_in:n_in + n_out], refs[n_in + n_out:]
        comm.start(ins, outs, sems)
        comm.finish(ins, outs, sems)

    return pl.pallas_call(
        body, name=name, in_specs=[ANY] * n_in, out_specs=[ANY] * n_out, out_shape=comm.out_shapes,
        scratch_shapes=comm.sem_shapes(),
    )(*comm.arrays)


def _remote(src, dst, sems, k, to):
    return pltpu.make_async_remote_copy(src_ref=src, dst_ref=dst, send_sem=sems[0].at[k], recv_sem=sems[1].at[k],
                                        device_id=to, device_id_type=MESH)


def comm_all_gather(shards):
    n = len(shards)

    def parties():
        x, y, c = _place()
        return (x, y, c), (x, y, 1 - c), [(1 - x, y), (x, 1 - y), (1 - x, 1 - y)]

    def slab(outs, w, dev):
        return outs[w].at[4 * dev[0] + 2 * dev[1] + dev[2]]

    def own(ins, outs, sems):
        me, sibling, chips = parties()
        local = [pltpu.make_async_copy(ins[w], slab(outs, w, me), sems[2].at[w]) for w in range(n)]
        first = []
        for w in range(n):
            first.append(_remote(ins[w], slab(outs, w, me), sems, 7 * w, sibling))
            first += [_remote(ins[w], slab(outs, w, me), sems, 7 * w + 1 + j, (*chip, me[2]))
                      for j, chip in enumerate(chips)]
        return local, first

    def start(ins, outs, sems):
        local, first = own(ins, outs, sems)
        for cp in local + first:
            cp.start()

    def finish(ins, outs, sems):
        me, sibling, chips = parties()
        local, first = own(ins, outs, sems)
        passed = []
        for w in range(n):
            for j, chip in enumerate(chips):
                got = slab(outs, w, (*chip, me[2]))
                _remote(got, got, sems, 7 * w + 1 + j, me).wait_recv()
                cp = _remote(got, got, sems, 7 * w + 4 + j, sibling)
                cp.start()
                passed.append(cp)
        for w in range(n):
            got = slab(outs, w, sibling)
            _remote(got, got, sems, 7 * w, me).wait_recv()
            for j, chip in enumerate(chips):
                got = slab(outs, w, (*chip, sibling[2]))
                _remote(got, got, sems, 7 * w + 4 + j, me).wait_recv()
        for cp in first + passed:
            cp.wait_send()
        for cp in local:
            cp.wait()

    return Comm(shards, [jax.ShapeDtypeStruct((N_DEV,) + s.shape, s.dtype) for s in shards], 7 * n, n, start, finish)


def comm_pairs(items):
    slabbed = [a.ndim == 3 for a in items]
    first = [sum(4 if s else 1 for s in slabbed[:w]) for w in range(len(items))]

    def copies(ins, outs, sems):
        x, y, c = _place()
        sibling = (x, y, 1 - c)
        cps = []
        for w, s in enumerate(slabbed):
            if s:
                cps += [_remote(ins[w].at[2 * q + (1 - c)], outs[w].at[q], sems, first[w] + q, sibling) for q in range(4)]
            else:
                cps.append(_remote(ins[w], outs[w], sems, first[w], sibling))
        return cps

    def start(ins, outs, sems):
        for cp in copies(ins, outs, sems):
            cp.start()

    def finish(ins, outs, sems):
        for cp in copies(ins, outs, sems):
            cp.wait()

    out_shapes = [jax.ShapeDtypeStruct(((4,) + a.shape[1:]) if s else a.shape, a.dtype) for a, s in zip(items, slabbed)]
    return Comm(items, out_shapes, sum(4 if s else 1 for s in slabbed), 0, start, finish)


def comm_chips(items):
    n = len(items)
    slabbed = [a.ndim == 3 for a in items]

    def copies(ins, outs, sems):
        x, y, c = _place()
        mine = 2 * x + y
        local = [pltpu.make_async_copy(ins[w].at[mine] if slabbed[w] else ins[w], outs[w].at[mine], sems[2].at[w])
                 for w in range(n)]
        remote = []
        for w in range(n):
            for j, (px, py) in enumerate([(1 - x, y), (x, 1 - y), (1 - x, 1 - y)]):
                src = ins[w].at[2 * px + py] if slabbed[w] else ins[w]
                remote.append(_remote(src, outs[w].at[mine], sems, 3 * w + j, (px, py, c)))
        return local, remote

    def start(ins, outs, sems):
        local, remote = copies(ins, outs, sems)
        for cp in local + remote:
            cp.start()

    def finish(ins, outs, sems):
        local, remote = copies(ins, outs, sems)
        for cp in remote + local:
            cp.wait()

    out_shapes = [jax.ShapeDtypeStruct(a.shape if s else (4,) + a.shape, a.dtype) for a, s in zip(items, slabbed)]
    return Comm(items, out_shapes, 3 * n, n, start, finish)


_SMALL = ("ffn1_norm", "mix_norm", "ln_v_gain", "ln_v_bias", "spatial_w", "spatial_b", "gnorm_a", "gnorm_b",
          "cross_norm", "mem_norm", "ffn2_norm", "final_norm")
_BIG = ("ffn1_w_in", "ffn1_w_out", "w_mix_in", "w_mix_out", "w_cq", "w_ckv", "w_co", "ffn2_w_in", "ffn2_w_out")
_COL_SHARDED = ("ffn1_w_in", "w_mix_in", "w_ckv", "ffn2_w_in")
_ORDER = ("ffn1_norm", "ffn1_w_in", "ffn1_w_out", "mix_norm", "w_mix_in", "ln_v_gain", "ln_v_bias", "spatial_w",
          "spatial_b", "gnorm_a", "gnorm_b", "w_mix_out", "cross_norm", "mem_norm", "w_cq", "w_ckv", "w_co",
          "ffn2_norm", "ffn2_w_in", "ffn2_w_out", "final_norm")


_SMALL_PAD = 120


def _rows128(a):
    return a.reshape(-1, 128)


def kernel(x, mem, ffn1_norm, ffn1_w_in, ffn1_w_out, mix_norm, w_mix_in, ln_v_gain, ln_v_bias, spatial_w, spatial_b, gnorm_a, gnorm_b, w_mix_out, cross_norm, mem_norm, w_cq, w_ckv, w_co, ffn2_norm, ffn2_w_in, ffn2_w_out, final_norm, loss_target, m_ffn1_norm, m_ffn1_w_in, m_ffn1_w_out, m_mix_norm, m_w_mix_in, m_ln_v_gain, m_ln_v_bias, m_spatial_w, m_spatial_b, m_gnorm_a, m_gnorm_b, m_w_mix_out, m_cross_norm, m_mem_norm, m_w_cq, m_w_ckv, m_w_co, m_ffn2_norm, m_ffn2_w_in, m_ffn2_w_out, m_final_norm, v_ffn1_norm, v_ffn1_w_in, v_ffn1_w_out, v_mix_norm, v_w_mix_in, v_ln_v_gain, v_ln_v_bias, v_spatial_w, v_spatial_b, v_gnorm_a, v_gnorm_b, v_w_mix_out, v_cross_norm, v_mem_norm, v_w_cq, v_w_ckv, v_w_co, v_ffn2_norm, v_ffn2_w_in, v_ffn2_w_out, v_final_norm):
    given = dict(locals())
    wts = {k: given[k] for k in _ORDER}
    mom = {k: given["m_" + k] for k in _ORDER}
    var = {k: given["v_" + k] for k in _ORDER}

    D = D_MODEL
    xs = x.reshape(-1, D)
    mems = mem.reshape(-1, D)
    tgt = loss_target.reshape(-1, D)
    vec = lambda a: a.reshape(1, -1)
    g1, gmix, gcross, gmem, g2, gfin = (vec(wts[k]) for k in
                                        ("ffn1_norm", "mix_norm", "cross_norm", "mem_norm", "ffn2_norm", "final_norm"))
    ln_g, ln_b, ga, gb = (vec(wts[k]) for k in ("ln_v_gain", "ln_v_bias", "gnorm_a", "gnorm_b"))
    w_s = spatial_w.reshape(G_A, SGU_BLOCK, SGU_BLOCK)
    b_t = spatial_b.reshape(G_A, SGU_BLOCK).T

    shard2d = {k: wts[k].reshape(wts[k].shape[1:]) for k in _BIG}
    shard_b = {k: cast_bf16(shard2d[k], f"cast_{k}") for k in _BIG}
    full = {}

    def gathering(names, fn, *args, **kw):
        out, got = fn(*args, comm=comm_all_gather([shard_b[k] for k in names]), **kw)
        for k, g in zip(names, got):
            full[k] = g if k in _COL_SHARDED else g.reshape(-1, g.shape[2])
        return out

    n1 = gathering(("ffn1_w_in",), rms_fwd, xs, g1, "f_n1")
    a1 = gathering(("ffn1_w_out",), mm_nn_g, n1, full["ffn1_w_in"], BF16, "f_a1")
    hsw1 = swiglu_fwd(a1, "f_hsw1")
    h1 = gathering(("w_mix_in",), mm_nn, hsw1, full["ffn1_w_out"], F32, "f_h1", scale=0.5, res=xs)
    n2 = rms_fwd(h1, gmix, "f_n2")
    z = gathering(("w_mix_out", "w_cq"), mm_nn_g, n2, full["w_mix_in"], F32, "f_z")
    ya = sgu_fwd(z, ln_g, ln_b, w_s, b_t, "f_sgu")
    yb = gathering(("w_ckv", "w_co"), sb_fwd, z, "f_sb")
    ycat = rmscat_fwd(ya, yb, ga, gb, "f_ycat")
    h2 = gathering(("ffn2_w_in",), mm_nn, ycat, full["w_mix_out"], F32, "f_h2", res=h1)
    n3 = rms_fwd(h2, gcross, "f_n3")
    memn = rms_fwd(mems, gmem, "f_memn")
    qc = mm_nn(n3, full["w_cq"], BF16, "f_qc", scale=X_DH ** -0.5)
    kv = mm_nn_g(memn, full["w_ckv"], BF16, "f_kv")
    o = xattn_fwd(qc, kv, "f_xattn")
    h3 = mm_nn(o, full["w_co"], F32, "f_h3", res=h2)
    n4 = rms_fwd(h3, g2, "f_n4")
    a2 = gathering(("ffn2_w_out",), mm_nn_g, n4, full["ffn2_w_in"], BF16, "f_a2")
    hsw2 = swiglu_fwd(a2, "f_hsw2")
    h4 = mm_nn(hsw2, full["ffn2_w_out"], F32, "f_h4", scale=0.5, res=h3)

    grads, parts, sums, recv = {}, {}, {}, {}
    core = lax.axis_index("c").astype(jnp.int32).reshape(1)

    def partial_of(k, g):
        grads[k] = g
        parts[k] = g if g.ndim == 3 else g.reshape(N_DEV, -1, g.shape[1])

    def reducing(pairs, chips, fn, *args, **kw):
        comm = merge_comms([comm_pairs([parts[k] for k in pairs]) if pairs else None,
                            comm_chips([sums[k] for k in chips]) if chips else None])
        out, got = fn(*args, comm=comm, **kw)
        got = split_results(comm, got)
        if pairs:
            for k, r in zip(pairs, got[0]):
                sums[k] = pair_sum(parts[k], r, core, f"pair_sum_{k}")
        if chips:
            recv.update(zip(chips, got[-1]))
        return out

    loss_part, dh4, df2, grads["final_norm"] = loss_head(h4, tgt, gfin, "loss_head")
    partial_of("ffn2_w_out", mm_tn(hsw2, df2, "b_ffn2_dwout"))
    dhsw2 = reducing(("ffn2_w_out",), (), mm_nt, df2, full["ffn2_w_out"], F32, "b_ffn2_dhsw")
    da2 = swiglu_bwd(a2, dhsw2, "b_ffn2_swiglu_bwd")
    partial_of("ffn2_w_in", reducing((), ("ffn2_w_out",), mm_tn_g, n4, da2, N_DEV, "b_ffn2_dwin"))
    dn4 = reducing(("ffn2_w_in",), (), mm_nt_g, da2, full["ffn2_w_in"], "b_ffn2_dn")
    dh3, dh3b, grads["ffn2_norm"] = rms_bwd(dn4, h3, g2, dh4, 1.0, "b_n4")

    partial_of("w_co", mm_tn(o, dh3b, "b_dwco"))
    do = reducing(("w_co",), (), mm_nt, dh3b, full["w_co"], BF16, "b_do")
    dqp, dkv = xattn_bwd(qc, kv, do, "b_xattn")
    partial_of("w_cq", mm_tn(n3, dqp, "b_dwcq"))
    dn3 = reducing(("w_cq",), (), mm_nt, dqp, full["w_cq"], F32, "b_dn3")
    partial_of("w_ckv", mm_tn_g(memn, dkv, N_DEV, "b_dwckv"))
    dmemn = reducing(("w_ckv",), (), mm_nt_g, dkv, full["w_ckv"], "b_dmemn")
    _, _, grads["mem_norm"] = rms_bwd(dmemn, mems, gmem, None, 1.0, "b_memn")
    dh2, dh2b, grads["cross_norm"] = rms_bwd(dn3, h2, gcross, dh3, 1.0, "b_n3")

    partial_of("w_mix_out", mm_tn(ycat, dh2b, "b_dwmixout"))
    dycat = reducing(("w_mix_out",), (), mm_nt, dh2b, full["w_mix_out"], F32, "b_dycat")
    dya, dyb, grads["gnorm_a"], grads["gnorm_b"] = rmscat_bwd(dycat, ya, yb, ga, gb, "b_ycat")
    dza, grads["ln_v_gain"], grads["ln_v_bias"], grads["spatial_w"], grads["spatial_b"] = sgu_bwd(
        z, dya, ln_g, ln_b, w_s, b_t, "b_sgu")
    dq, dk, dv = reducing((), ("ffn2_w_in", "w_co", "w_cq"), sb_bwd, z, yb, dyb, "b_sb")
    dz = jnp.concatenate([dza, dq, dk, dv], axis=1)
    partial_of("w_mix_in", reducing((), ("w_ckv",), mm_tn_g, n2, dz, N_DEV, "b_dwmixin"))
    dn2 = reducing(("w_mix_in",), ("w_mix_out",), mm_nt_g, dz, full["w_mix_in"], "b_dn2")
    dh1, dh1b, grads["mix_norm"] = rms_bwd(dn2, h1, gmix, dh2, 0.5, "b_n2")

    partial_of("ffn1_w_out", reducing((), ("w_mix_in",), mm_tn, hsw1, dh1b, "b_ffn1_dwout"))
    dhsw1 = reducing(("ffn1_w_out",), (), mm_nt, dh1b, full["ffn1_w_out"], F32, "b_ffn1_dhsw")
    da1 = swiglu_bwd(a1, dhsw1, "b_ffn1_swiglu_bwd")
    partial_of("ffn1_w_in", reducing((), ("ffn1_w_out",), mm_tn_g, n1, da1, N_DEV, "b_ffn1_dwin"))
    dn1 = reducing(("ffn1_w_in",), (), mm_nt_g, da1, full["ffn1_w_in"], "b_ffn1_dn")
    (dx, _, grads["ffn1_norm"]) = reducing((), ("ffn1_w_in",), rms_bwd, dn1, xs, g1, dh1, 1.0, "b_n1")

    pack = lambda d: jnp.concatenate([_rows128(d[k]) for k in _SMALL] + [jnp.zeros((_SMALL_PAD, 128), F32)], axis=0)
    small_part = pack(grads)
    (small_sibling,) = run_comm(comm_pairs([small_part]), "comm_pairs_small")
    small_pair = add2(small_part, small_sibling, "pair_sum_small")
    (small_all,) = run_comm(comm_chips([small_pair]), "comm_chips_small")

    out_g, out_d, out_m, out_v = {}, {}, {}, {}
    for k in _BIG:
        res = adamw(recv[k], shard2d[k], mom[k].reshape(shard2d[k].shape), var[k].reshape(shard2d[k].shape), f"adamw_{k}")
        out_g[k], out_d[k], out_m[k], out_v[k] = (t.reshape(wts[k].shape) for t in res)
    res = adamw(small_all, pack(wts), pack(mom), pack(var), "adamw_small")
    row = 0
    for k in _SMALL:
        nrow = wts[k].size // 128
        for dst, t in zip((out_g, out_d, out_m, out_v), res):
            dst[k] = t[row:row + nrow].reshape(wts[k].shape)
        row += nrow

    loss = lax.psum(loss_part[0, 0], ("x", "y", "c"))
    grad_x = dx.reshape(x.shape)
    return (loss, grad_x, *[out_g[k] for k in _ORDER], *[out_d[k] for k in _ORDER],
            *[out_m[k] for k in _ORDER], *[out_v[k] for k in _ORDER])
```

```python
import functools
import math

import jax
import jax.numpy as jnp
from jax import lax
from jax.experimental import pallas as pl
from jax.experimental.pallas import tpu as pltpu

F32 = jnp.float32
BF16 = jnp.bfloat16

N_DEV = 8
D_MODEL = 2048
D_FF = 5632
W_A = 1024
G_A = 8
GA_DIM = 128
SGU_BLOCK = 128
CHUNK = 64
H_B = 8
DH_B = 128
Q_BLOCK = 128
X_HEADS = 4
X_DH = 512
N_MEM = 256
EPS = 1e-6

ADAM_LR = 0.001
ADAM_B1 = 0.9
ADAM_B2 = 0.999
ADAM_EPS = 1e-08
ADAM_WD = 0.01
ADAM_STEP = 10

VMEM_LIMIT = 56 * 2**20
ROW_TILE = 256

MESH = pl.DeviceIdType.MESH
ANY = pl.BlockSpec(memory_space=pl.ANY)

_NT = (((1,), (1,)), ((), ()))
_TN = (((0,), (0,)), ((), ()))


def _params(*sem):
    return pltpu.CompilerParams(dimension_semantics=sem, vmem_limit_bytes=VMEM_LIMIT)


def _zeros(ref):
    return jnp.zeros(ref.shape, ref.dtype)


def _pcall(comm, body, *, name, grid, in_specs, out_specs, out_shape, compiler_params, scratch_shapes=()):
    if comm is None:
        return pl.pallas_call(body, name=name, grid=grid, in_specs=in_specs, out_specs=out_specs, out_shape=out_shape,
                              scratch_shapes=list(scratch_shapes), compiler_params=compiler_params)
    multi = isinstance(out_shape, (list, tuple))
    out_shapes = list(out_shape) if multi else [out_shape]
    out_specs_l = list(out_specs) if multi else [out_specs]
    n_in, n_out, n_scr = len(in_specs), len(out_shapes), len(scratch_shapes)
    n_cin, n_cout = len(comm.arrays), len(comm.out_shapes)

    def with_comm(*refs):
        ins, refs = refs[:n_in], refs[n_in:]
        cins, refs = refs[:n_cin], refs[n_cin:]
        outs, refs = refs[:n_out], refs[n_out:]
        couts, refs = refs[:n_cout], refs[n_cout:]
        scr, sems = refs[:n_scr], refs[n_scr:]
        first = functools.reduce(jnp.logical_and, [pl.program_id(a) == 0 for a in range(len(grid))])
        last = functools.reduce(jnp.logical_and, [pl.program_id(a) == grid[a] - 1 for a in range(len(grid))])
        pl.when(first)(lambda: comm.start(cins, couts, sems))
        body(*ins, *outs, *scr)
        pl.when(last)(lambda: comm.finish(cins, couts, sems))

    call = pl.pallas_call(
        with_comm, name=name, grid=grid, in_specs=list(in_specs) + [ANY] * n_cin,
        out_specs=out_specs_l + [ANY] * n_cout, out_shape=out_shapes + comm.out_shapes,
        scratch_shapes=list(scratch_shapes) + comm.sem_shapes(), compiler_params=_params(*(("arbitrary",) * len(grid))))

    def run(*args):
        res = call(*args, *comm.arrays)
        main = res[:n_out]
        return (list(main) if multi else main[0]), list(res[n_out:])

    return run


def _dot(a, b):
    return jnp.dot(a, b, preferred_element_type=F32)


def _dot_nt(a, b):
    return lax.dot_general(a, b, _NT, preferred_element_type=F32)


def _dot_tn(a, b):
    return lax.dot_general(a, b, _TN, preferred_element_type=F32)


def mm_nn_g(a, bg, out_dtype, name, tm=512, comm=None):
    M, K = a.shape
    G, _, n = bg.shape
    tm = min(tm, M)

    def body(a_ref, b_ref, o_ref):
        o_ref[...] = _dot(a_ref[...], b_ref[...]).astype(o_ref.dtype)

    return _pcall(
        comm, body, name=name, grid=(G, M // tm),
        in_specs=[pl.BlockSpec((tm, K), lambda g, m: (m, 0)),
                  pl.BlockSpec((None, K, n), lambda g, m: (g, 0, 0))],
        out_specs=pl.BlockSpec((tm, n), lambda g, m: (m, g)),
        out_shape=jax.ShapeDtypeStruct((M, G * n), out_dtype),
        compiler_params=_params("parallel", "parallel"),
    )(a, bg)


def mm_swiglu_g(a, bg, name, tm=512, comm=None):
    M, K = a.shape
    G, _, n = bg.shape
    half = G // 2
    tm = min(tm, M)

    def body(a_ref, bgate_ref, bup_ref, gu_ref, h_ref):
        av = a_ref[...]
        gate = _dot(av, bgate_ref[...])
        up = _dot(av, bup_ref[...])
        gu_ref[0] = gate.astype(BF16)
        gu_ref[1] = up.astype(BF16)
        h_ref[...] = (gate * _sigmoid(gate) * up).astype(BF16)

    return _pcall(
        comm, body, name=name, grid=(half, M // tm),
        in_specs=[pl.BlockSpec((tm, K), lambda p, m: (m, 0)),
                  pl.BlockSpec((None, K, n), lambda p, m: (p, 0, 0)),
                  pl.BlockSpec((None, K, n), lambda p, m: (p + half, 0, 0))],
        out_specs=[pl.BlockSpec((2, tm, n), lambda p, m: (0, m, p)), pl.BlockSpec((tm, n), lambda p, m: (m, p))],
        out_shape=[jax.ShapeDtypeStruct((2, M, half * n), BF16), jax.ShapeDtypeStruct((M, half * n), BF16)],
        compiler_params=_params("parallel", "parallel"),
    )(a, bg, bg)


def mm_nn(a, b, out_dtype, name, tm=512, tn=1024, scale=1.0, res=None, comm=None):
    M, K = a.shape
    _, N = b.shape
    tm, tn = min(tm, M), min(tn, N)

    def body(*refs):
        if res is None:
            a_ref, b_ref, o_ref = refs
            acc = _dot(a_ref[...], b_ref[...])
            o_ref[...] = (acc * scale if scale != 1.0 else acc).astype(o_ref.dtype)
        else:
            a_ref, b_ref, r_ref, o_ref = refs
            o_ref[...] = (r_ref[...] + scale * _dot(a_ref[...], b_ref[...])).astype(o_ref.dtype)

    in_specs = [pl.BlockSpec((tm, K), lambda n, m: (m, 0)),
                pl.BlockSpec((K, tn), lambda n, m: (0, n))]
    args = [a, b]
    if res is not None:
        in_specs.append(pl.BlockSpec((tm, tn), lambda n, m: (m, n)))
        args.append(res)
    return _pcall(
        comm, body, name=name, grid=(N // tn, M // tm),
        in_specs=in_specs,
        out_specs=pl.BlockSpec((tm, tn), lambda n, m: (m, n)),
        out_shape=jax.ShapeDtypeStruct((M, N), out_dtype),
        compiler_params=_params("parallel", "parallel"),
    )(*args)


def mm_nt_g(dy, bg, name, tm=512, comm=None):
    M, _ = dy.shape
    G, K, n = bg.shape
    tm = min(tm, M)

    def body(dy_ref, b_ref, o_ref):
        part = _dot_nt(dy_ref[...], b_ref[...])

        @pl.when(pl.program_id(1) == 0)
        def _():
            o_ref[...] = part

        @pl.when(pl.program_id(1) > 0)
        def _():
            o_ref[...] += part

    return _pcall(
        comm, body, name=name, grid=(M // tm, G),
        in_specs=[pl.BlockSpec((tm, n), lambda m, g: (m, g)),
                  pl.BlockSpec((None, K, n), lambda m, g: (g, 0, 0))],
        out_specs=pl.BlockSpec((tm, K), lambda m, g: (m, 0)),
        out_shape=jax.ShapeDtypeStruct((M, K), F32),
        compiler_params=_params("parallel", "arbitrary"),
    )(dy, bg)


def mm_nt(dy, b, out_dtype, name, tk=512, comm=None):
    M, N = dy.shape
    K, _ = b.shape

    def body(dy_ref, b_ref, o_ref):
        o_ref[...] = _dot_nt(dy_ref[...], b_ref[...]).astype(o_ref.dtype)

    return _pcall(
        comm, body, name=name, grid=(K // tk,),
        in_specs=[pl.BlockSpec((M, N), lambda k: (0, 0)),
                  pl.BlockSpec((tk, N), lambda k: (k, 0))],
        out_specs=pl.BlockSpec((M, tk), lambda k: (0, k)),
        out_shape=jax.ShapeDtypeStruct((M, K), out_dtype),
        compiler_params=_params("parallel"),
    )(dy, b)


def mm_tn_g(x, dy, G, name, tk=512, comm=None):
    M, K = x.shape
    n = dy.shape[1] // G

    def body(x_ref, dy_ref, o_ref):
        o_ref[...] = _dot_tn(x_ref[...], dy_ref[...]).astype(o_ref.dtype)

    return _pcall(
        comm, body, name=name, grid=(G, K // tk),
        in_specs=[pl.BlockSpec((M, tk), lambda g, k: (0, k)),
                  pl.BlockSpec((M, n), lambda g, k: (0, g))],
        out_specs=pl.BlockSpec((None, tk, n), lambda g, k: (g, k, 0)),
        out_shape=jax.ShapeDtypeStruct((G, K, n), BF16),
        compiler_params=_params("parallel", "parallel"),
    )(x, dy)


def mm_tn(x, dy, name, tk=512, comm=None):
    M, K = x.shape
    _, N = dy.shape

    def body(x_ref, dy_ref, o_ref):
        o_ref[...] = _dot_tn(x_ref[...], dy_ref[...]).astype(o_ref.dtype)

    return _pcall(
        comm, body, name=name, grid=(K // tk,),
        in_specs=[pl.BlockSpec((M, tk), lambda k: (0, k)),
                  pl.BlockSpec((M, N), lambda k: (0, 0))],
        out_specs=pl.BlockSpec((tk, N), lambda k: (k, 0)),
        out_shape=jax.ShapeDtypeStruct((K, N), BF16),
        compiler_params=_params("parallel"),
    )(x, dy)


def _rstd(x):
    return lax.rsqrt(jnp.mean(x * x, axis=-1, keepdims=True) + EPS)


def _rms_bwd(dn, xhat, r, g):
    dxhat = dn * g
    return r * (dxhat - xhat * jnp.mean(dxhat * xhat, axis=-1, keepdims=True))


def _row_spec(tr, width, col=0):
    return pl.BlockSpec((tr, width), lambda i: (i, col))


def _vec_spec(width):
    return pl.BlockSpec((1, width), lambda i: (0, 0))


def rms_fwd(x, g, name, comm=None):
    M, D = x.shape
    tr = min(ROW_TILE, M)

    def body(x_ref, g_ref, o_ref):
        xv = x_ref[...]
        o_ref[...] = (xv * _rstd(xv) * g_ref[...]).astype(o_ref.dtype)

    return _pcall(
        comm, body, name=name, grid=(M // tr,),
        in_specs=[_row_spec(tr, D), _vec_spec(D)],
        out_specs=_row_spec(tr, D),
        out_shape=jax.ShapeDtypeStruct((M, D), BF16),
        compiler_params=_params("parallel"),
    )(x, g)


def rms_bwd(dn, h, g, dres, copy_scale, name, comm=None):
    M, D = h.shape
    tr = min(ROW_TILE, M)
    has_res = dres is not None

    def body(*refs):
        if has_res:
            dn_ref, h_ref, g_ref, dres_ref, dh_ref, dhb_ref, dg_ref = refs
        else:
            dn_ref, h_ref, g_ref, dh_ref, dhb_ref, dg_ref = refs
        hv = h_ref[...]
        r = _rstd(hv)
        xhat = hv * r
        dn = dn_ref[...]
        part = jnp.sum(dn * xhat, axis=0, keepdims=True)

        @pl.when(pl.program_id(0) == 0)
        def _():
            dg_ref[...] = part

        @pl.when(pl.program_id(0) > 0)
        def _():
            dg_ref[...] += part

        dh = _rms_bwd(dn, xhat, r, g_ref[...])
        if has_res:
            dh = dh + dres_ref[...]
        dh_ref[...] = dh
        dhb_ref[...] = (dh * copy_scale if copy_scale != 1.0 else dh).astype(BF16)

    in_specs = [_row_spec(tr, D), _row_spec(tr, D), _vec_spec(D)]
    args = [dn, h, g]
    if has_res:
        in_specs.append(_row_spec(tr, D))
        args.append(dres)
    return _pcall(
        comm, body, name=name, grid=(M // tr,),
        in_specs=in_specs,
        out_specs=[_row_spec(tr, D), _row_spec(tr, D), _vec_spec(D)],
        out_shape=[jax.ShapeDtypeStruct((M, D), F32), jax.ShapeDtypeStruct((M, D), BF16),
                   jax.ShapeDtypeStruct((1, D), F32)],
        compiler_params=_params("arbitrary"),
    )(*args)


def _sigmoid(x):
    return 1.0 / (1.0 + jnp.exp(-x))


def swiglu_bwd(a, dh, name, comm=None):
    _, M, F = a.shape
    F2 = 2 * F
    tr = min(ROW_TILE, M)

    def body(g_ref, u_ref, dh_ref, da_ref):
        gt = g_ref[...].astype(F32)
        up = u_ref[...].astype(F32)
        dh = dh_ref[...].astype(F32)
        sg = _sigmoid(gt)
        da_ref[:, :F] = (dh * up * (sg * (1.0 + gt * (1.0 - sg)))).astype(BF16)
        da_ref[:, F:] = (dh * (gt * sg)).astype(BF16)

    return _pcall(
        comm, body, name=name, grid=(M // tr,),
        in_specs=[pl.BlockSpec((None, tr, F), lambda i: (0, i, 0)), pl.BlockSpec((None, tr, F), lambda i: (1, i, 0)),
                  _row_spec(tr, F)],
        out_specs=_row_spec(tr, F2),
        out_shape=jax.ShapeDtypeStruct((M, F2), BF16),
        compiler_params=_params("parallel"),
    )(a, a, dh)


def rmscat_fwd(ya, yb, ga, gb, name, comm=None):
    M, W = ya.shape
    tr = min(ROW_TILE, M)

    def body(ya_ref, yb_ref, ga_ref, gb_ref, o_ref):
        a = ya_ref[...]
        b = yb_ref[...]
        o_ref[:, :W] = (a * _rstd(a) * ga_ref[...]).astype(BF16)
        o_ref[:, W:] = (b * _rstd(b) * gb_ref[...]).astype(BF16)

    return _pcall(
        comm, body, name=name, grid=(M // tr,),
        in_specs=[_row_spec(tr, W), _row_spec(tr, W), _vec_spec(W), _vec_spec(W)],
        out_specs=_row_spec(tr, 2 * W),
        out_shape=jax.ShapeDtypeStruct((M, 2 * W), BF16),
        compiler_params=_params("parallel"),
    )(ya, yb, ga, gb)


def rmscat_bwd(dycat, ya, yb, ga, gb, name, comm=None):
    M, W = ya.shape
    tr = min(ROW_TILE, M)

    def body(dc_ref, ya_ref, yb_ref, ga_ref, gb_ref, dya_ref, dyb_ref, dga_ref, dgb_ref):
        first = pl.program_id(0) == 0
        for y_ref, g_ref, dy_ref, dg_ref, lo in ((ya_ref, ga_ref, dya_ref, dga_ref, 0),
                                                 (yb_ref, gb_ref, dyb_ref, dgb_ref, W)):
            yv = y_ref[...]
            r = _rstd(yv)
            xhat = yv * r
            dn = dc_ref[:, lo:lo + W]
            part = jnp.sum(dn * xhat, axis=0, keepdims=True)

            @pl.when(first)
            def _():
                dg_ref[...] = part

            @pl.when(jnp.logical_not(first))
            def _():
                dg_ref[...] += part

            dy_ref[...] = _rms_bwd(dn, xhat, r, g_ref[...])

    return _pcall(
        comm, body, name=name, grid=(M // tr,),
        in_specs=[_row_spec(tr, 2 * W), _row_spec(tr, W), _row_spec(tr, W), _vec_spec(W), _vec_spec(W)],
        out_specs=[_row_spec(tr, W), _row_spec(tr, W), _vec_spec(W), _vec_spec(W)],
        out_shape=[jax.ShapeDtypeStruct((M, W), F32), jax.ShapeDtypeStruct((M, W), F32),
                   jax.ShapeDtypeStruct((1, W), F32), jax.ShapeDtypeStruct((1, W), F32)],
        compiler_params=_params("arbitrary"),
    )(dycat, ya, yb, ga, gb)


def loss_head(h, target, g, name, comm=None):
    M, D = h.shape
    tr = min(ROW_TILE, M)

    def body(h_ref, t_ref, g_ref, loss_ref, dh_ref, dhb_ref, dg_ref):
        hv = h_ref[...]
        gv = g_ref[...]
        r = _rstd(hv)
        xhat = hv * r
        err = xhat * gv - t_ref[...]
        lsum = jnp.sum(jnp.sum(err * err, axis=1, keepdims=True), axis=0, keepdims=True) * (0.5 / D)
        dy = err * (1.0 / D)
        part = jnp.sum(dy * xhat, axis=0, keepdims=True)

        @pl.when(pl.program_id(0) == 0)
        def _():
            dg_ref[...] = part
            loss_ref[...] = _zeros(loss_ref) + lsum

        @pl.when(pl.program_id(0) > 0)
        def _():
            dg_ref[...] += part
            loss_ref[...] += lsum

        dh = _rms_bwd(dy, xhat, r, gv)
        dh_ref[...] = dh
        dhb_ref[...] = (0.5 * dh).astype(BF16)

    return _pcall(
        comm, body, name=name, grid=(M // tr,),
        in_specs=[_row_spec(tr, D), _row_spec(tr, D), _vec_spec(D)],
        out_specs=[pl.BlockSpec((8, 128), lambda i: (0, 0)), _row_spec(tr, D), _row_spec(tr, D), _vec_spec(D)],
        out_shape=[jax.ShapeDtypeStruct((8, 128), F32), jax.ShapeDtypeStruct((M, D), F32),
                   jax.ShapeDtypeStruct((M, D), BF16), jax.ShapeDtypeStruct((1, D), F32)],
        compiler_params=_params("arbitrary"),
    )(h, target, g)


_GELU_C = math.sqrt(2.0 / math.pi)


def _gelu(x):
    return 0.5 * x * (1.0 + jnp.tanh(_GELU_C * (x + 0.044715 * (x * x * x))))


def _gelu_grad(x):
    t = jnp.tanh(_GELU_C * (x + 0.044715 * (x * x * x)))
    return 0.5 * (1.0 + t) + 0.5 * x * (1.0 - t * t) * (_GELU_C * (1.0 + 3.0 * 0.044715 * (x * x)))


def _sgu_mask():
    t = lax.broadcasted_iota(jnp.int32, (SGU_BLOCK, SGU_BLOCK), 0) // CHUNK
    s = lax.broadcasted_iota(jnp.int32, (SGU_BLOCK, SGU_BLOCK), 1) // CHUNK
    return s <= t


def _layernorm_stats(v):
    mu = jnp.mean(v, axis=-1, keepdims=True)
    cen = v - mu
    rstd = lax.rsqrt(jnp.mean(cen * cen, axis=-1, keepdims=True) + EPS)
    return cen * rstd, rstd


def sgu_fwd(z, ln_g, ln_b, w_s, b_t, name, comm=None):
    S = z.shape[0]

    def body(zu_ref, zv_ref, lg_ref, lb_ref, w_ref, bt_ref, o_ref):
        mask = _sgu_mask()
        for g in range(G_A):
            cols = slice(g * GA_DIM, (g + 1) * GA_DIM)
            u = _gelu(zu_ref[:, cols])
            vhat, _ = _layernorm_stats(_gelu(zv_ref[:, cols]))
            vln = vhat * lg_ref[:, cols] + lb_ref[:, cols]
            w = jnp.where(mask, w_ref[g], 0.0).astype(BF16)
            mixed = _dot(w, vln.astype(BF16)) + bt_ref[:, g:g + 1]
            o_ref[:, cols] = u * mixed

    return _pcall(
        comm, body, name=name, grid=(S // SGU_BLOCK,),
        in_specs=[_row_spec(SGU_BLOCK, W_A, 0), _row_spec(SGU_BLOCK, W_A, 1), _vec_spec(W_A), _vec_spec(W_A),
                  pl.BlockSpec((G_A, SGU_BLOCK, SGU_BLOCK), lambda i: (0, 0, 0)),
                  pl.BlockSpec((SGU_BLOCK, G_A), lambda i: (0, 0))],
        out_specs=_row_spec(SGU_BLOCK, W_A),
        out_shape=jax.ShapeDtypeStruct((S, W_A), F32),
        compiler_params=_params("parallel"),
    )(z, z, ln_g, ln_b, w_s, b_t)


def sgu_bwd(z, dya, ln_g, ln_b, w_s, b_t, name, comm=None):
    S = z.shape[0]
    nblk = S // SGU_BLOCK

    def body(zu_ref, zv_ref, dy_ref, lg_ref, lb_ref, w_ref, bt_ref,
             dz_ref, dlg_ref, dlb_ref, dw_ref, db_ref, dmix_acc):
        step = pl.program_id(0)
        mask = _sgu_mask()

        @pl.when(step == 0)
        def _():
            dlg_ref[...] = _zeros(dlg_ref)
            dlb_ref[...] = _zeros(dlb_ref)
            dw_ref[...] = _zeros(dw_ref)
            dmix_acc[...] = _zeros(dmix_acc)

        for g in range(G_A):
            cols = slice(g * GA_DIM, (g + 1) * GA_DIM)
            zu = zu_ref[:, cols]
            zv = zv_ref[:, cols]
            u = _gelu(zu)
            vhat, rstd = _layernorm_stats(_gelu(zv))
            lg = lg_ref[:, cols]
            vln = (vhat * lg + lb_ref[:, cols]).astype(BF16)
            w = jnp.where(mask, w_ref[g], 0.0)
            mixed = _dot(w.astype(BF16), vln) + bt_ref[:, g:g + 1]
            dy = dy_ref[:, cols]
            du = dy * mixed
            dmixed = dy * u
            dmixed_b = dmixed.astype(BF16)
            dmix_acc[g] += dmixed
            dw_ref[g] += jnp.where(mask, _dot_nt(dmixed_b, vln), 0.0)
            dvln = _dot(w.T.astype(BF16), dmixed_b)
            dlb_ref[:, cols] += jnp.sum(dvln, axis=0, keepdims=True)
            dlg_ref[:, cols] += jnp.sum(dvln * vhat, axis=0, keepdims=True)
            dvhat = dvln * lg
            dv = rstd * (dvhat - jnp.mean(dvhat, axis=-1, keepdims=True)
                         - vhat * jnp.mean(dvhat * vhat, axis=-1, keepdims=True))
            dz_ref[:, cols] = (du * _gelu_grad(zu)).astype(BF16)
            dz_ref[:, W_A + g * GA_DIM:W_A + (g + 1) * GA_DIM] = (dv * _gelu_grad(zv)).astype(BF16)

        @pl.when(step == nblk - 1)
        def _():
            for g in range(G_A):
                db_ref[g] = jnp.sum(dmix_acc[g], axis=1, keepdims=True)

    whole3 = lambda shape: pl.BlockSpec(shape, lambda i: (0, 0, 0))
    return _pcall(
        comm, body, name=name, grid=(nblk,),
        in_specs=[_row_spec(SGU_BLOCK, W_A, 0), _row_spec(SGU_BLOCK, W_A, 1), _row_spec(SGU_BLOCK, W_A),
                  _vec_spec(W_A), _vec_spec(W_A), whole3((G_A, SGU_BLOCK, SGU_BLOCK)),
                  pl.BlockSpec((SGU_BLOCK, G_A), lambda i: (0, 0))],
        out_specs=[_row_spec(SGU_BLOCK, 2 * W_A), _vec_spec(W_A), _vec_spec(W_A),
                   whole3((G_A, SGU_BLOCK, SGU_BLOCK)), whole3((G_A, SGU_BLOCK, 1))],
        out_shape=[jax.ShapeDtypeStruct((S, 2 * W_A), BF16), jax.ShapeDtypeStruct((1, W_A), F32),
                   jax.ShapeDtypeStruct((1, W_A), F32), jax.ShapeDtypeStruct((G_A, SGU_BLOCK, SGU_BLOCK), F32),
                   jax.ShapeDtypeStruct((G_A, SGU_BLOCK, 1), F32)],
        scratch_shapes=[pltpu.VMEM((G_A, SGU_BLOCK, SGU_BLOCK), F32)],
        compiler_params=_params("arbitrary"),
    )(z, z, dya, ln_g, ln_b, w_s, b_t)


def _log_sigmoid(z):
    return jnp.minimum(z, 0.0) - jnp.log(1.0 + jnp.exp(-jnp.abs(z)))


def _suffix_sum(x, upper):
    hi = x.astype(BF16)
    rem = x - hi.astype(F32)
    mid = rem.astype(BF16)
    lo = (rem - mid.astype(F32)).astype(BF16)
    return _dot(hi, upper) + _dot(mid, upper) + _dot(lo, upper)


SB_ROWS = 1024
_SB_SUB = SB_ROWS // Q_BLOCK


def _sb_upper():
    row = lax.broadcasted_iota(jnp.int32, (Q_BLOCK, Q_BLOCK), 0)
    col = lax.broadcasted_iota(jnp.int32, (Q_BLOCK, Q_BLOCK), 1)
    return (row > col).astype(BF16)


def _sb_sweep(step, tile):
    for r in reversed(range(_SB_SUB)):
        tile(step * _SB_SUB + r, r * Q_BLOCK)

    def group(g, _):
        base = (step - 1 - g) * _SB_SUB
        for r in reversed(range(_SB_SUB)):
            tile(base + r, None)
        return 0

    lax.fori_loop(0, step, group, 0)


def _sb_causal(n):
    return lax.broadcasted_iota(jnp.int32, (n, Q_BLOCK), 1) < lax.broadcasted_iota(jnp.int32, (n, Q_BLOCK), 0)


def _sb_col(part):
    return (2 * W_A + part * (H_B * DH_B)) // DH_B


def _sb_q_spec():
    return pl.BlockSpec((SB_ROWS, DH_B), lambda h, i: (i, _sb_col(0) + h))


def _sb_kv_spec(S, part):
    return pl.BlockSpec((S, DH_B), lambda h, i: (0, _sb_col(part) + h))


def sb_fwd(z, name, comm=None):
    S = z.shape[0]
    scale = DH_B ** -0.5

    def body(q_ref, k_ref, v_ref, o_ref, q_b, c_l1m):
        step = pl.program_id(1)
        q_b[...] = (q_ref[...] * scale).astype(BF16)
        o_ref[...] = _zeros(o_ref)
        c_l1m[...] = _zeros(c_l1m)
        upper = _sb_upper()

        def tile(j, row0):
            rq = slice(row0 or 0, SB_ROWS)
            causal = None if row0 is None else _sb_causal(SB_ROWS - row0)
            rows = pl.ds(pl.multiple_of(j * Q_BLOCK, Q_BLOCK), Q_BLOCK)
            zz = _dot_nt(q_b[rq, :], k_ref[rows, :].astype(BF16))
            lb = _log_sigmoid(zz)
            l1m = lb - zz
            if causal is not None:
                l1m = jnp.where(causal, l1m, 0.0)
            a = jnp.exp(lb + _suffix_sum(l1m, upper) + c_l1m[rq, :])
            if causal is not None:
                a = jnp.where(causal, a, 0.0)
            o_ref[rq, :] += _dot(a.astype(BF16), v_ref[rows, :].astype(BF16))
            c_l1m[rq, :] += jnp.sum(l1m, axis=1, keepdims=True)

        _sb_sweep(step, tile)

    return _pcall(
        comm, body, name=name, grid=(H_B, S // SB_ROWS),
        in_specs=[_sb_q_spec(), _sb_kv_spec(S, 1), _sb_kv_spec(S, 2)],
        out_specs=pl.BlockSpec((SB_ROWS, DH_B), lambda h, i: (i, h)),
        out_shape=jax.ShapeDtypeStruct((S, H_B * DH_B), F32),
        scratch_shapes=[pltpu.VMEM((SB_ROWS, DH_B), BF16), pltpu.VMEM((SB_ROWS, 1), F32)],
        compiler_params=_params("parallel", "parallel"),
    )(z, z, z)


def sb_bwd(z, out, dout, name, comm=None):
    S = z.shape[0]
    nstep = S // SB_ROWS
    scale = DH_B ** -0.5

    def body(q_ref, k_ref, v_ref, o_ref, do_ref, dq_ref, dk_ref, dv_ref,
             dq_acc, dk_acc, dv_acc, q_b, do_b, delta, c_l1m, c_g):
        step = pl.program_id(1)

        @pl.when(step == 0)
        def _():
            dk_acc[...] = _zeros(dk_acc)
            dv_acc[...] = _zeros(dv_acc)

        q_b[...] = (q_ref[...] * scale).astype(BF16)
        do_b[...] = do_ref[...].astype(BF16)
        delta[...] = jnp.sum(do_b[...].astype(F32) * o_ref[...], axis=1, keepdims=True)
        dq_acc[...] = _zeros(dq_acc)
        c_l1m[...] = _zeros(c_l1m)
        c_g[...] = _zeros(c_g)
        upper = _sb_upper()

        def tile(j, row0):
            rq = slice(row0 or 0, SB_ROWS)
            causal = None if row0 is None else _sb_causal(SB_ROWS - row0)
            rows = pl.ds(pl.multiple_of(j * Q_BLOCK, Q_BLOCK), Q_BLOCK)
            q, do_t = q_b[rq, :], do_b[rq, :]
            k_j = k_ref[rows, :].astype(BF16)
            zz = _dot_nt(q, k_j)
            lb = _log_sigmoid(zz)
            l1m = lb - zz
            if causal is not None:
                l1m = jnp.where(causal, l1m, 0.0)
            a = jnp.exp(lb + _suffix_sum(l1m, upper) + c_l1m[rq, :])
            if causal is not None:
                a = jnp.where(causal, a, 0.0)
            a_b = a.astype(BF16)
            dv_acc[rows, :] += _dot_tn(a_b, do_t)
            gmat = a_b.astype(F32) * _dot_nt(do_t, v_ref[rows, :].astype(BF16))
            before = delta[rq, :] - gmat - (_suffix_sum(gmat, upper) + c_g[rq, :])
            sig = jnp.exp(lb)
            dz = gmat * (1.0 - sig) - sig * before
            if causal is not None:
                dz = jnp.where(causal, dz, 0.0)
            dz_b = dz.astype(BF16)
            dk_acc[rows, :] += _dot_tn(dz_b, q)
            dq_acc[rq, :] += _dot(dz_b, k_j)
            c_l1m[rq, :] += jnp.sum(l1m, axis=1, keepdims=True)
            c_g[rq, :] += jnp.sum(gmat, axis=1, keepdims=True)

        _sb_sweep(step, tile)
        dq_ref[...] = (dq_acc[...] * scale).astype(BF16)

        @pl.when(step == nstep - 1)
        def _():
            dk_ref[...] = dk_acc[...].astype(BF16)
            dv_ref[...] = dv_acc[...].astype(BF16)

    rows_spec = pl.BlockSpec((SB_ROWS, DH_B), lambda h, i: (i, h))
    head_spec = pl.BlockSpec((S, DH_B), lambda h, i: (0, h))
    out_sds = jax.ShapeDtypeStruct((S, H_B * DH_B), BF16)
    return _pcall(
        comm, body, name=name, grid=(H_B, nstep),
        in_specs=[_sb_q_spec(), _sb_kv_spec(S, 1), _sb_kv_spec(S, 2), rows_spec, rows_spec],
        out_specs=[rows_spec, head_spec, head_spec],
        out_shape=[out_sds, out_sds, out_sds],
        scratch_shapes=[pltpu.VMEM((SB_ROWS, DH_B), F32)] + [pltpu.VMEM((S, DH_B), F32)] * 2
        + [pltpu.VMEM((SB_ROWS, DH_B), BF16)] * 2 + [pltpu.VMEM((SB_ROWS, 1), F32)] * 3,
        compiler_params=_params("parallel", "arbitrary"),
    )(z, z, z, out, dout)


def _softmax(s):
    e = jnp.exp(s - jnp.max(s, axis=-1, keepdims=True))
    return e / jnp.sum(e, axis=-1, keepdims=True)


def xattn_fwd(qc, kv, name, comm=None):
    S, D = qc.shape
    tr = min(ROW_TILE, S)

    def body(q_ref, kv_ref, o_ref):
        for h in range(X_HEADS):
            cols = slice(h * X_DH, (h + 1) * X_DH)
            p = _softmax(_dot_nt(q_ref[:, cols], kv_ref[:, cols]))
            o_ref[:, cols] = _dot(p.astype(BF16), kv_ref[:, D + h * X_DH:D + (h + 1) * X_DH]).astype(BF16)

    return _pcall(
        comm, body, name=name, grid=(S // tr,),
        in_specs=[_row_spec(tr, D), pl.BlockSpec((N_MEM, 2 * D), lambda i: (0, 0))],
        out_specs=_row_spec(tr, D),
        out_shape=jax.ShapeDtypeStruct((S, D), BF16),
        compiler_params=_params("parallel"),
    )(qc, kv)


def xattn_bwd(qc, kv, do, name, comm=None):
    S, D = qc.shape
    tr = min(ROW_TILE, S)
    nstep = S // tr
    scale = X_DH ** -0.5

    def body(q_ref, kv_ref, do_ref, dq_ref, dkv_ref, acc):
        step = pl.program_id(0)

        @pl.when(step == 0)
        def _():
            acc[...] = _zeros(acc)

        for h in range(X_HEADS):
            cols = slice(h * X_DH, (h + 1) * X_DH)
            vcols = slice(D + h * X_DH, D + (h + 1) * X_DH)
            q = q_ref[:, cols]
            k = kv_ref[:, cols]
            do_h = do_ref[:, cols]
            p = _softmax(_dot_nt(q, k))
            dp = _dot_nt(do_h, kv_ref[:, vcols])
            acc[:, vcols] += _dot_tn(p.astype(BF16), do_h)
            ds = (p * (dp - jnp.sum(p * dp, axis=-1, keepdims=True))).astype(BF16)
            dq_ref[:, cols] = (_dot(ds, k) * scale).astype(BF16)
            acc[:, cols] += _dot_tn(ds, q)

        @pl.when(step == nstep - 1)
        def _():
            dkv_ref[...] = acc[...].astype(BF16)

    whole = pl.BlockSpec((N_MEM, 2 * D), lambda i: (0, 0))
    return _pcall(
        comm, body, name=name, grid=(nstep,),
        in_specs=[_row_spec(tr, D), whole, _row_spec(tr, D)],
        out_specs=[_row_spec(tr, D), whole],
        out_shape=[jax.ShapeDtypeStruct((S, D), BF16), jax.ShapeDtypeStruct((N_MEM, 2 * D), BF16)],
        scratch_shapes=[pltpu.VMEM((N_MEM, 2 * D), F32)],
        compiler_params=_params("arbitrary"),
    )(qc, kv, do)


def _row_tile(rows, cap=128):
    return max(t for t in range(16, cap + 1, 16) if rows % t == 0)


def cast_bf16(w, name, comm=None):
    R, C = w.shape
    tr = _row_tile(R, 256)

    def body(w_ref, o_ref):
        o_ref[...] = w_ref[...].astype(BF16)

    return _pcall(
        comm, body, name=name, grid=(R // tr,),
        in_specs=[_row_spec(tr, C)], out_specs=_row_spec(tr, C),
        out_shape=jax.ShapeDtypeStruct((R, C), BF16),
        compiler_params=_params("parallel"),
    )(w)


def adamw(parts, w, m, v, name, comm=None):
    R, C = w.shape
    n_parts = parts.shape[0]
    tr = _row_tile(R, 256)
    c1 = 1.0 - ADAM_B1 ** ADAM_STEP
    c2 = 1.0 - ADAM_B2 ** ADAM_STEP

    def body(p_ref, w_ref, m_ref, v_ref, g_ref, d_ref, mo_ref, vo_ref):
        g = p_ref[0].astype(F32)
        for p in range(1, n_parts):
            g = g + p_ref[p].astype(F32)
        m_new = ADAM_B1 * m_ref[...] + (1.0 - ADAM_B1) * g
        v_new = ADAM_B2 * v_ref[...] + (1.0 - ADAM_B2) * (g * g)
        g_ref[...] = g
        mo_ref[...] = m_new
        vo_ref[...] = v_new
        d_ref[...] = -ADAM_LR * ((m_new / c1) / (jnp.sqrt(v_new / c2) + ADAM_EPS) + ADAM_WD * w_ref[...])

    spec = _row_spec(tr, C)
    sds = jax.ShapeDtypeStruct((R, C), F32)
    return _pcall(
        comm, body, name=name, grid=(R // tr,),
        in_specs=[pl.BlockSpec((n_parts, tr, C), lambda i: (0, i, 0)), spec, spec, spec],
        out_specs=[spec, spec, spec, spec],
        out_shape=[sds, sds, sds, sds],
        compiler_params=_params("parallel"),
    )(parts, w, m, v)


def pair_sum(parts, from_sibling, core, name):
    _, R, C = parts.shape
    tr = _row_tile(R, 1024)

    def body(core_ref, p_ref, s_ref, o_ref):
        o_ref[...] = (p_ref[...].astype(F32) + s_ref[...].astype(F32)).astype(o_ref.dtype)

    return pl.pallas_call(
        body, name=name,
        grid_spec=pltpu.PrefetchScalarGridSpec(
            num_scalar_prefetch=1, grid=(4, R // tr),
            in_specs=[pl.BlockSpec((None, tr, C), lambda q, i, core_ref: (2 * q + core_ref[0], i, 0)),
                      pl.BlockSpec((None, tr, C), lambda q, i, core_ref: (q, i, 0))],
            out_specs=pl.BlockSpec((None, tr, C), lambda q, i, core_ref: (q, i, 0))),
        out_shape=jax.ShapeDtypeStruct((4, R, C), BF16),
        compiler_params=_params("parallel", "parallel"),
    )(core, parts, from_sibling)


def add2(a, b, name, comm=None):
    R, C = a.shape
    tr = _row_tile(R, 256)

    def body(a_ref, b_ref, o_ref):
        o_ref[...] = a_ref[...] + b_ref[...]

    spec = _row_spec(tr, C)
    return _pcall(
        comm, body, name=name, grid=(R // tr,), in_specs=[spec, spec], out_specs=spec,
        out_shape=jax.ShapeDtypeStruct((R, C), F32), compiler_params=_params("parallel"),
    )(a, b)


def _place():
    return lax.axis_index("x"), lax.axis_index("y"), lax.axis_index("c")


class Comm:
    def __init__(self, arrays, out_shapes, n_remote, n_local, start, finish):
        self.arrays, self.out_shapes = list(arrays), list(out_shapes)
        self.n_remote, self.n_local = n_remote, max(n_local, 1)
        self.start, self.finish = start, finish

    def sem_shapes(self):
        return [pltpu.SemaphoreType.DMA((self.n_remote,)), pltpu.SemaphoreType.DMA((self.n_remote,)),
                pltpu.SemaphoreType.DMA((self.n_local,))]


class _Shifted:
    def __init__(self, ref, offset):
        self.ref, self.offset = ref, offset

    @property
    def at(self):
        return self

    def __getitem__(self, k):
        return self.ref.at[self.offset + k]


def merge_comms(comms):
    comms = [c for c in comms if c is not None]
    if not comms:
        return None

    def each(method):
        def run(ins, outs, sems):
            i = o = r = l = 0
            for c in comms:
                sub = (_Shifted(sems[0], r), _Shifted(sems[1], r), _Shifted(sems[2], l))
                getattr(c, method)(ins[i:i + len(c.arrays)], outs[o:o + len(c.out_shapes)], sub)
                i, o, r, l = i + len(c.arrays), o + len(c.out_shapes), r + c.n_remote, l + c.n_local
        return run

    merged = Comm([a for c in comms for a in c.arrays], [s for c in comms for s in c.out_shapes],
                  sum(c.n_remote for c in comms), sum(c.n_local for c in comms), each("start"), each("finish"))
    merged.sizes = [len(c.out_shapes) for c in comms]
    return merged


def split_results(comm, results):
    out, i = [], 0
    for n in comm.sizes:
        out.append(list(results[i:i + n]))
        i += n
    return out


def run_comm(comm, name):
    n_in, n_out = len(comm.arrays), len(comm.out_shapes)

    def body(*refs):
        ins, outs, sems = refs[:n_in], refs[n_in:n_in + n_out], refs[n_in + n_out:]
        comm.start(ins, outs, sems)
        comm.finish(ins, outs, sems)

    return pl.pallas_call(
        body, name=name, in_specs=[ANY] * n_in, out_specs=[ANY] * n_out, out_shape=comm.out_shapes,
        scratch_shapes=comm.sem_shapes(),
    )(*comm.arrays)


def _remote(src, dst, sems, k, to):
    return pltpu.make_async_remote_copy(src_ref=src, dst_ref=dst, send_sem=sems[0].at[k], recv_sem=sems[1].at[k],
                                        device_id=to, device_id_type=MESH)


def comm_all_gather(shards):
    n = len(shards)

    def parties():
        x, y, c = _place()
        return (x, y, c), (x, y, 1 - c), [(1 - x, y), (x, 1 - y), (1 - x, 1 - y)]

    def slab(outs, w, dev):
        return outs[w].at[4 * dev[0] + 2 * dev[1] + dev[2]]

    def own(ins, outs, sems):
        me, sibling, chips = parties()
        local = [pltpu.make_async_copy(ins[w], slab(outs, w, me), sems[2].at[w]) for w in range(n)]
        first = []
        for w in range(n):
            first.append(_remote(ins[w], slab(outs, w, me), sems, 7 * w, sibling))
            first += [_remote(ins[w], slab(outs, w, me), sems, 7 * w + 1 + j, (*chip, me[2]))
                      for j, chip in enumerate(chips)]
        return local, first

    def start(ins, outs, sems):
        local, first = own(ins, outs, sems)
        for cp in local + first:
            cp.start()

    def finish(ins, outs, sems):
        me, sibling, chips = parties()
        local, first = own(ins, outs, sems)
        passed = []
        for w in range(n):
            for j, chip in enumerate(chips):
                got = slab(outs, w, (*chip, me[2]))
                _remote(got, got, sems, 7 * w + 1 + j, me).wait_recv()
                cp = _remote(got, got, sems, 7 * w + 4 + j, sibling)
                cp.start()
                passed.append(cp)
        for w in range(n):
            got = slab(outs, w, sibling)
            _remote(got, got, sems, 7 * w, me).wait_recv()
            for j, chip in enumerate(chips):
                got = slab(outs, w, (*chip, sibling[2]))
                _remote(got, got, sems, 7 * w + 4 + j, me).wait_recv()
        for cp in first + passed:
            cp.wait_send()
        for cp in local:
            cp.wait()

    return Comm(shards, [jax.ShapeDtypeStruct((N_DEV,) + s.shape, s.dtype) for s in shards], 7 * n, n, start, finish)


def comm_pairs(items):
    slabbed = [a.ndim == 3 for a in items]
    first = [sum(4 if s else 1 for s in slabbed[:w]) for w in range(len(items))]

    def copies(ins, outs, sems):
        x, y, c = _place()
        sibling = (x, y, 1 - c)
        cps = []
        for w, s in enumerate(slabbed):
            if s:
                cps += [_remote(ins[w].at[2 * q + (1 - c)], outs[w].at[q], sems, first[w] + q, sibling) for q in range(4)]
            else:
                cps.append(_remote(ins[w], outs[w], sems, first[w], sibling))
        return cps

    def start(ins, outs, sems):
        for cp in copies(ins, outs, sems):
            cp.start()

    def finish(ins, outs, sems):
        for cp in copies(ins, outs, sems):
            cp.wait()

    out_shapes = [jax.ShapeDtypeStruct(((4,) + a.shape[1:]) if s else a.shape, a.dtype) for a, s in zip(items, slabbed)]
    return Comm(items, out_shapes, sum(4 if s else 1 for s in slabbed), 0, start, finish)


def comm_chips(items):
    n = len(items)
    slabbed = [a.ndim == 3 for a in items]

    def copies(ins, outs, sems):
        x, y, c = _place()
        mine = 2 * x + y
        local = [pltpu.make_async_copy(ins[w].at[mine] if slabbed[w] else ins[w], outs[w].at[mine], sems[2].at[w])
                 for w in range(n)]
        remote = []
        for w in range(n):
            for j, (px, py) in enumerate([(1 - x, y), (x, 1 - y), (1 - x, 1 - y)]):
                src = ins[w].at[2 * px + py] if slabbed[w] else ins[w]
                remote.append(_remote(src, outs[w].at[mine], sems, 3 * w + j, (px, py, c)))
        return local, remote

    def start(ins, outs, sems):
        local, remote = copies(ins, outs, sems)
        for cp in local + remote:
            cp.start()

    def finish(ins, outs, sems):
        local, remote = copies(ins, outs, sems)
        for cp in remote + local:
            cp.wait()

    out_shapes = [jax.ShapeDtypeStruct(a.shape if s else (4,) + a.shape, a.dtype) for a, s in zip(items, slabbed)]
    return Comm(items, out_shapes, 3 * n, n, start, finish)


_SMALL = ("ffn1_norm", "mix_norm", "ln_v_gain", "ln_v_bias", "spatial_w", "spatial_b", "gnorm_a", "gnorm_b",
          "cross_norm", "mem_norm", "ffn2_norm", "final_norm")
_BIG = ("ffn1_w_in", "ffn1_w_out", "w_mix_in", "w_mix_out", "w_cq", "w_ckv", "w_co", "ffn2_w_in", "ffn2_w_out")
_COL_SHARDED = ("ffn1_w_in", "w_mix_in", "w_ckv", "ffn2_w_in")
_ORDER = ("ffn1_norm", "ffn1_w_in", "ffn1_w_out", "mix_norm", "w_mix_in", "ln_v_gain", "ln_v_bias", "spatial_w",
          "spatial_b", "gnorm_a", "gnorm_b", "w_mix_out", "cross_norm", "mem_norm", "w_cq", "w_ckv", "w_co",
          "ffn2_norm", "ffn2_w_in", "ffn2_w_out", "final_norm")


_SMALL_PAD = 120


def _rows128(a):
    return a.reshape(-1, 128)


def kernel(x, mem, ffn1_norm, ffn1_w_in, ffn1_w_out, mix_norm, w_mix_in, ln_v_gain, ln_v_bias, spatial_w, spatial_b, gnorm_a, gnorm_b, w_mix_out, cross_norm, mem_norm, w_cq, w_ckv, w_co, ffn2_norm, ffn2_w_in, ffn2_w_out, final_norm, loss_target, m_ffn1_norm, m_ffn1_w_in, m_ffn1_w_out, m_mix_norm, m_w_mix_in, m_ln_v_gain, m_ln_v_bias, m_spatial_w, m_spatial_b, m_gnorm_a, m_gnorm_b, m_w_mix_out, m_cross_norm, m_mem_norm, m_w_cq, m_w_ckv, m_w_co, m_ffn2_norm, m_ffn2_w_in, m_ffn2_w_out, m_final_norm, v_ffn1_norm, v_ffn1_w_in, v_ffn1_w_out, v_mix_norm, v_w_mix_in, v_ln_v_gain, v_ln_v_bias, v_spatial_w, v_spatial_b, v_gnorm_a, v_gnorm_b, v_w_mix_out, v_cross_norm, v_mem_norm, v_w_cq, v_w_ckv, v_w_co, v_ffn2_norm, v_ffn2_w_in, v_ffn2_w_out, v_final_norm):
    given = dict(locals())
    wts = {k: given[k] for k in _ORDER}
    mom = {k: given["m_" + k] for k in _ORDER}
    var = {k: given["v_" + k] for k in _ORDER}

    D = D_MODEL
    xs = x.reshape(-1, D)
    mems = mem.reshape(-1, D)
    tgt = loss_target.reshape(-1, D)
    vec = lambda a: a.reshape(1, -1)
    g1, gmix, gcross, gmem, g2, gfin = (vec(wts[k]) for k in
                                        ("ffn1_norm", "mix_norm", "cross_norm", "mem_norm", "ffn2_norm", "final_norm"))
    ln_g, ln_b, ga, gb = (vec(wts[k]) for k in ("ln_v_gain", "ln_v_bias", "gnorm_a", "gnorm_b"))
    w_s = spatial_w.reshape(G_A, SGU_BLOCK, SGU_BLOCK)
    b_t = spatial_b.reshape(G_A, SGU_BLOCK).T

    shard2d = {k: wts[k].reshape(wts[k].shape[1:]) for k in _BIG}
    shard_b = {k: cast_bf16(shard2d[k], f"cast_{k}") for k in _BIG}
    full = {}

    def gathering(names, fn, *args, **kw):
        out, got = fn(*args, comm=comm_all_gather([shard_b[k] for k in names]), **kw)
        for k, g in zip(names, got):
            full[k] = g if k in _COL_SHARDED else g.reshape(-1, g.shape[2])
        return out

    n1 = gathering(("ffn1_w_in",), rms_fwd, xs, g1, "f_n1")
    a1, hsw1 = gathering(("ffn1_w_out",), mm_swiglu_g, n1, full["ffn1_w_in"], "f_a1")
    h1 = gathering(("w_mix_in",), mm_nn, hsw1, full["ffn1_w_out"], F32, "f_h1", scale=0.5, res=xs)
    n2 = rms_fwd(h1, gmix, "f_n2")
    z = gathering(("w_mix_out", "w_cq"), mm_nn_g, n2, full["w_mix_in"], F32, "f_z")
    ya = sgu_fwd(z, ln_g, ln_b, w_s, b_t, "f_sgu")
    yb = gathering(("w_ckv", "w_co"), sb_fwd, z, "f_sb")
    ycat = rmscat_fwd(ya, yb, ga, gb, "f_ycat")
    h2 = gathering(("ffn2_w_in",), mm_nn, ycat, full["w_mix_out"], F32, "f_h2", res=h1)
    n3 = rms_fwd(h2, gcross, "f_n3")
    memn = rms_fwd(mems, gmem, "f_memn")
    qc = mm_nn(n3, full["w_cq"], BF16, "f_qc", scale=X_DH ** -0.5)
    kv = mm_nn_g(memn, full["w_ckv"], BF16, "f_kv")
    o = xattn_fwd(qc, kv, "f_xattn")
    h3 = mm_nn(o, full["w_co"], F32, "f_h3", res=h2)
    n4 = rms_fwd(h3, g2, "f_n4")
    a2, hsw2 = gathering(("ffn2_w_out",), mm_swiglu_g, n4, full["ffn2_w_in"], "f_a2")
    h4 = mm_nn(hsw2, full["ffn2_w_out"], F32, "f_h4", scale=0.5, res=h3)

    grads, parts, sums, recv = {}, {}, {}, {}
    core = lax.axis_index("c").astype(jnp.int32).reshape(1)

    def partial_of(k, g):
        grads[k] = g
        parts[k] = g if g.ndim == 3 else g.reshape(N_DEV, -1, g.shape[1])

    def reducing(pairs, chips, fn, *args, **kw):
        comm = merge_comms([comm_pairs([parts[k] for k in pairs]) if pairs else None,
                            comm_chips([sums[k] for k in chips]) if chips else None])
        out, got = fn(*args, comm=comm, **kw)
        got = split_results(comm, got)
        if pairs:
            for k, r in zip(pairs, got[0]):
                sums[k] = pair_sum(parts[k], r, core, f"pair_sum_{k}")
        if chips:
            recv.update(zip(chips, got[-1]))
        return out

    loss_part, dh4, df2, grads["final_norm"] = loss_head(h4, tgt, gfin, "loss_head")
    partial_of("ffn2_w_out", mm_tn(hsw2, df2, "b_ffn2_dwout"))
    dhsw2 = reducing(("ffn2_w_out",), (), mm_nt, df2, full["ffn2_w_out"], BF16, "b_ffn2_dhsw")
    da2 = swiglu_bwd(a2, dhsw2, "b_ffn2_swiglu_bwd")
    partial_of("ffn2_w_in", reducing((), ("ffn2_w_out",), mm_tn_g, n4, da2, N_DEV, "b_ffn2_dwin"))
    dn4 = reducing(("ffn2_w_in",), (), mm_nt_g, da2, full["ffn2_w_in"], "b_ffn2_dn")
    dh3, dh3b, grads["ffn2_norm"] = rms_bwd(dn4, h3, g2, dh4, 1.0, "b_n4")

    partial_of("w_co", mm_tn(o, dh3b, "b_dwco"))
    do = reducing(("w_co",), (), mm_nt, dh3b, full["w_co"], BF16, "b_do")
    dqp, dkv = xattn_bwd(qc, kv, do, "b_xattn")
    partial_of("w_cq", mm_tn(n3, dqp, "b_dwcq"))
    dn3 = reducing(("w_cq",), (), mm_nt, dqp, full["w_cq"], F32, "b_dn3")
    partial_of("w_ckv", mm_tn_g(memn, dkv, N_DEV, "b_dwckv"))
    dmemn = reducing(("w_ckv",), (), mm_nt_g, dkv, full["w_ckv"], "b_dmemn")
    _, _, grads["mem_norm"] = rms_bwd(dmemn, mems, gmem, None, 1.0, "b_memn")
    dh2, dh2b, grads["cross_norm"] = rms_bwd(dn3, h2, gcross, dh3, 1.0, "b_n3")

    partial_of("w_mix_out", mm_tn(ycat, dh2b, "b_dwmixout"))
    dycat = reducing(("w_mix_out",), (), mm_nt, dh2b, full["w_mix_out"], F32, "b_dycat")
    dya, dyb, grads["gnorm_a"], grads["gnorm_b"] = rmscat_bwd(dycat, ya, yb, ga, gb, "b_ycat")
    dza, grads["ln_v_gain"], grads["ln_v_bias"], grads["spatial_w"], grads["spatial_b"] = sgu_bwd(
        z, dya, ln_g, ln_b, w_s, b_t, "b_sgu")
    dq, dk, dv = reducing((), ("ffn2_w_in", "w_co", "w_cq"), sb_bwd, z, yb, dyb, "b_sb")
    dz = jnp.concatenate([dza, dq, dk, dv], axis=1)
    partial_of("w_mix_in", reducing((), ("w_ckv",), mm_tn_g, n2, dz, N_DEV, "b_dwmixin"))
    dn2 = reducing(("w_mix_in",), ("w_mix_out",), mm_nt_g, dz, full["w_mix_in"], "b_dn2")
    dh1, dh1b, grads["mix_norm"] = rms_bwd(dn2, h1, gmix, dh2, 0.5, "b_n2")

    partial_of("ffn1_w_out", reducing((), ("w_mix_in",), mm_tn, hsw1, dh1b, "b_ffn1_dwout"))
    dhsw1 = reducing(("ffn1_w_out",), (), mm_nt, dh1b, full["ffn1_w_out"], BF16, "b_ffn1_dhsw")
    da1 = swiglu_bwd(a1, dhsw1, "b_ffn1_swiglu_bwd")
    partial_of("ffn1_w_in", reducing((), ("ffn1_w_out",), mm_tn_g, n1, da1, N_DEV, "b_ffn1_dwin"))
    dn1 = reducing(("ffn1_w_in",), (), mm_nt_g, da1, full["ffn1_w_in"], "b_ffn1_dn")
    (dx, _, grads["ffn1_norm"]) = reducing((), ("ffn1_w_in",), rms_bwd, dn1, xs, g1, dh1, 1.0, "b_n1")

    pack = lambda d: jnp.concatenate([_rows128(d[k]) for k in _SMALL] + [jnp.zeros((_SMALL_PAD, 128), F32)], axis=0)
    small_part = pack(grads)
    (small_sibling,) = run_comm(comm_pairs([small_part]), "comm_pairs_small")
    small_pair = add2(small_part, small_sibling, "pair_sum_small")
    (small_all,) = run_comm(comm_chips([small_pair]), "comm_chips_small")

    out_g, out_d, out_m, out_v = {}, {}, {}, {}
    for k in _BIG:
        res = adamw(recv[k], shard2d[k], mom[k].reshape(shard2d[k].shape), var[k].reshape(shard2d[k].shape), f"adamw_{k}")
        out_g[k], out_d[k], out_m[k], out_v[k] = (t.reshape(wts[k].shape) for t in res)
    res = adamw(small_all, pack(wts), pack(mom), pack(var), "adamw_small")
    row = 0
    for k in _SMALL:
        nrow = wts[k].size // 128
        for dst, t in zip((out_g, out_d, out_m, out_v), res):
            dst[k] = t[row:row + nrow].reshape(wts[k].shape)
        row += nrow

    loss = lax.psum(loss_part[0, 0], ("x", "y", "c"))
    grad_x = dx.reshape(x.shape)
    return (loss, grad_x, *[out_g[k] for k in _ORDER], *[out_d[k] for k in _ORDER],
            *[out_m[k] for k in _ORDER], *[out_v[k] for k in _ORDER])
```

```python
import functools
import math

import jax
import jax.numpy as jnp
from jax import lax
from jax.experimental import pallas as pl
from jax.experimental.pallas import tpu as pltpu

F32 = jnp.float32
BF16 = jnp.bfloat16

N_DEV = 8
D_MODEL = 2048
D_FF = 5632
W_A = 1024
G_A = 8
GA_DIM = 128
SGU_BLOCK = 128
CHUNK = 64
H_B = 8
DH_B = 128
Q_BLOCK = 128
X_HEADS = 4
X_DH = 512
N_MEM = 256
EPS = 1e-6

ADAM_LR = 0.001
ADAM_B1 = 0.9
ADAM_B2 = 0.999
ADAM_EPS = 1e-08
ADAM_WD = 0.01
ADAM_STEP = 10

VMEM_LIMIT = 56 * 2**20
ROW_TILE = 256

MESH = pl.DeviceIdType.MESH
ANY = pl.BlockSpec(memory_space=pl.ANY)

_NT = (((1,), (1,)), ((), ()))
_TN = (((0,), (0,)), ((), ()))


def _params(*sem):
    return pltpu.CompilerParams(dimension_semantics=sem, vmem_limit_bytes=VMEM_LIMIT)


def _zeros(ref):
    return jnp.zeros(ref.shape, ref.dtype)


def _pcall(comm, body, *, name, grid, in_specs, out_specs, out_shape, compiler_params, scratch_shapes=()):
    if comm is None:
        return pl.pallas_call(body, name=name, grid=grid, in_specs=in_specs, out_specs=out_specs, out_shape=out_shape,
                              scratch_shapes=list(scratch_shapes), compiler_params=compiler_params)
    multi = isinstance(out_shape, (list, tuple))
    out_shapes = list(out_shape) if multi else [out_shape]
    out_specs_l = list(out_specs) if multi else [out_specs]
    n_in, n_out, n_scr = len(in_specs), len(out_shapes), len(scratch_shapes)
    n_cin, n_cout = len(comm.arrays), len(comm.out_shapes)

    def with_comm(*refs):
        ins, refs = refs[:n_in], refs[n_in:]
        cins, refs = refs[:n_cin], refs[n_cin:]
        outs, refs = refs[:n_out], refs[n_out:]
        couts, refs = refs[:n_cout], refs[n_cout:]
        scr, sems = refs[:n_scr], refs[n_scr:]
        first = functools.reduce(jnp.logical_and, [pl.program_id(a) == 0 for a in range(len(grid))])
        last = functools.reduce(jnp.logical_and, [pl.program_id(a) == grid[a] - 1 for a in range(len(grid))])
        pl.when(first)(lambda: comm.start(cins, couts, sems))
        body(*ins, *outs, *scr)
        pl.when(last)(lambda: comm.finish(cins, couts, sems))

    call = pl.pallas_call(
        with_comm, name=name, grid=grid, in_specs=list(in_specs) + [ANY] * n_cin,
        out_specs=out_specs_l + [ANY] * n_cout, out_shape=out_shapes + comm.out_shapes,
        scratch_shapes=list(scratch_shapes) + comm.sem_shapes(), compiler_params=_params(*(("arbitrary",) * len(grid))),
        input_output_aliases={n_in + i: n_out + j for i, j in comm.aliases.items()})

    def run(*args):
        res = call(*args, *comm.arrays)
        main = res[:n_out]
        return (list(main) if multi else main[0]), list(res[n_out:])

    return run


def _dot(a, b):
    return jnp.dot(a, b, preferred_element_type=F32)


def _dot_nt(a, b):
    return lax.dot_general(a, b, _NT, preferred_element_type=F32)


def _dot_tn(a, b):
    return lax.dot_general(a, b, _TN, preferred_element_type=F32)


def mm_nn_g(a, bg, out_dtype, name, tm=512, comm=None):
    M, K = a.shape
    G, _, n = bg.shape
    tm = min(tm, M)

    def body(a_ref, b_ref, o_ref):
        o_ref[...] = _dot(a_ref[...], b_ref[...]).astype(o_ref.dtype)

    return _pcall(
        comm, body, name=name, grid=(G, M // tm),
        in_specs=[pl.BlockSpec((tm, K), lambda g, m: (m, 0)),
                  pl.BlockSpec((None, K, n), lambda g, m: (g, 0, 0))],
        out_specs=pl.BlockSpec((tm, n), lambda g, m: (m, g)),
        out_shape=jax.ShapeDtypeStruct((M, G * n), out_dtype),
        compiler_params=_params("parallel", "parallel"),
    )(a, bg)


def mm_swiglu_g(a, bg, name, tm=512, comm=None):
    M, K = a.shape
    G, _, n = bg.shape
    half = G // 2
    tm = min(tm, M)

    def body(a_ref, bgate_ref, bup_ref, gu_ref, h_ref):
        av = a_ref[...]
        gate = _dot(av, bgate_ref[...])
        up = _dot(av, bup_ref[...])
        gu_ref[0] = gate.astype(BF16)
        gu_ref[1] = up.astype(BF16)
        h_ref[...] = (gate * _sigmoid(gate) * up).astype(BF16)

    return _pcall(
        comm, body, name=name, grid=(half, M // tm),
        in_specs=[pl.BlockSpec((tm, K), lambda p, m: (m, 0)),
                  pl.BlockSpec((None, K, n), lambda p, m: (p, 0, 0)),
                  pl.BlockSpec((None, K, n), lambda p, m: (p + half, 0, 0))],
        out_specs=[pl.BlockSpec((2, tm, n), lambda p, m: (0, m, p)), pl.BlockSpec((tm, n), lambda p, m: (m, p))],
        out_shape=[jax.ShapeDtypeStruct((2, M, half * n), BF16), jax.ShapeDtypeStruct((M, half * n), BF16)],
        compiler_params=_params("parallel", "parallel"),
    )(a, bg, bg)


def mm_nn(a, b, out_dtype, name, tm=512, tn=1024, scale=1.0, res=None, comm=None):
    M, K = a.shape
    _, N = b.shape
    tm, tn = min(tm, M), min(tn, N)

    def body(*refs):
        if res is None:
            a_ref, b_ref, o_ref = refs
            acc = _dot(a_ref[...], b_ref[...])
            o_ref[...] = (acc * scale if scale != 1.0 else acc).astype(o_ref.dtype)
        else:
            a_ref, b_ref, r_ref, o_ref = refs
            o_ref[...] = (r_ref[...] + scale * _dot(a_ref[...], b_ref[...])).astype(o_ref.dtype)

    in_specs = [pl.BlockSpec((tm, K), lambda n, m: (m, 0)),
                pl.BlockSpec((K, tn), lambda n, m: (0, n))]
    args = [a, b]
    if res is not None:
        in_specs.append(pl.BlockSpec((tm, tn), lambda n, m: (m, n)))
        args.append(res)
    return _pcall(
        comm, body, name=name, grid=(N // tn, M // tm),
        in_specs=in_specs,
        out_specs=pl.BlockSpec((tm, tn), lambda n, m: (m, n)),
        out_shape=jax.ShapeDtypeStruct((M, N), out_dtype),
        compiler_params=_params("parallel", "parallel"),
    )(*args)


def mm_nt_g(dy, bg, name, tm=512, comm=None):
    M, _ = dy.shape
    G, K, n = bg.shape
    tm = min(tm, M)

    def body(dy_ref, b_ref, o_ref):
        part = _dot_nt(dy_ref[...], b_ref[...])

        @pl.when(pl.program_id(1) == 0)
        def _():
            o_ref[...] = part

        @pl.when(pl.program_id(1) > 0)
        def _():
            o_ref[...] += part

    return _pcall(
        comm, body, name=name, grid=(M // tm, G),
        in_specs=[pl.BlockSpec((tm, n), lambda m, g: (m, g)),
                  pl.BlockSpec((None, K, n), lambda m, g: (g, 0, 0))],
        out_specs=pl.BlockSpec((tm, K), lambda m, g: (m, 0)),
        out_shape=jax.ShapeDtypeStruct((M, K), F32),
        compiler_params=_params("parallel", "arbitrary"),
    )(dy, bg)


def mm_nt(dy, b, out_dtype, name, tk=512, comm=None):
    M, N = dy.shape
    K, _ = b.shape

    def body(dy_ref, b_ref, o_ref):
        o_ref[...] = _dot_nt(dy_ref[...], b_ref[...]).astype(o_ref.dtype)

    return _pcall(
        comm, body, name=name, grid=(K // tk,),
        in_specs=[pl.BlockSpec((M, N), lambda k: (0, 0)),
                  pl.BlockSpec((tk, N), lambda k: (k, 0))],
        out_specs=pl.BlockSpec((M, tk), lambda k: (0, k)),
        out_shape=jax.ShapeDtypeStruct((M, K), out_dtype),
        compiler_params=_params("parallel"),
    )(dy, b)


def mm_tn_g(x, dy, G, name, tk=512, comm=None):
    M, K = x.shape
    n = dy.shape[1] // G

    def body(x_ref, dy_ref, o_ref):
        o_ref[...] = _dot_tn(x_ref[...], dy_ref[...]).astype(o_ref.dtype)

    return _pcall(
        comm, body, name=name, grid=(G, K // tk),
        in_specs=[pl.BlockSpec((M, tk), lambda g, k: (0, k)),
                  pl.BlockSpec((M, n), lambda g, k: (0, g))],
        out_specs=pl.BlockSpec((None, tk, n), lambda g, k: (g, k, 0)),
        out_shape=jax.ShapeDtypeStruct((G, K, n), BF16),
        compiler_params=_params("parallel", "parallel"),
    )(x, dy)


def mm_tn(x, dy, name, tk=512, comm=None):
    M, K = x.shape
    _, N = dy.shape

    def body(x_ref, dy_ref, o_ref):
        o_ref[...] = _dot_tn(x_ref[...], dy_ref[...]).astype(o_ref.dtype)

    return _pcall(
        comm, body, name=name, grid=(K // tk,),
        in_specs=[pl.BlockSpec((M, tk), lambda k: (0, k)),
                  pl.BlockSpec((M, N), lambda k: (0, 0))],
        out_specs=pl.BlockSpec((tk, N), lambda k: (k, 0)),
        out_shape=jax.ShapeDtypeStruct((K, N), BF16),
        compiler_params=_params("parallel"),
    )(x, dy)


def _rstd(x):
    return lax.rsqrt(jnp.mean(x * x, axis=-1, keepdims=True) + EPS)


def _rms_bwd(dn, xhat, r, g):
    dxhat = dn * g
    return r * (dxhat - xhat * jnp.mean(dxhat * xhat, axis=-1, keepdims=True))


def _row_spec(tr, width, col=0):
    return pl.BlockSpec((tr, width), lambda i: (i, col))


def _vec_spec(width):
    return pl.BlockSpec((1, width), lambda i: (0, 0))


def rms_fwd(x, g, name, comm=None):
    M, D = x.shape
    tr = min(ROW_TILE, M)

    def body(x_ref, g_ref, o_ref):
        xv = x_ref[...]
        o_ref[...] = (xv * _rstd(xv) * g_ref[...]).astype(o_ref.dtype)

    return _pcall(
        comm, body, name=name, grid=(M // tr,),
        in_specs=[_row_spec(tr, D), _vec_spec(D)],
        out_specs=_row_spec(tr, D),
        out_shape=jax.ShapeDtypeStruct((M, D), BF16),
        compiler_params=_params("parallel"),
    )(x, g)


def rms_bwd(dn, h, g, dres, copy_scale, name, comm=None):
    M, D = h.shape
    tr = min(ROW_TILE, M)
    has_res = dres is not None

    def body(*refs):
        if has_res:
            dn_ref, h_ref, g_ref, dres_ref, dh_ref, dhb_ref, dg_ref = refs
        else:
            dn_ref, h_ref, g_ref, dh_ref, dhb_ref, dg_ref = refs
        hv = h_ref[...]
        r = _rstd(hv)
        xhat = hv * r
        dn = dn_ref[...]
        part = jnp.sum(dn * xhat, axis=0, keepdims=True)

        @pl.when(pl.program_id(0) == 0)
        def _():
            dg_ref[...] = part

        @pl.when(pl.program_id(0) > 0)
        def _():
            dg_ref[...] += part

        dh = _rms_bwd(dn, xhat, r, g_ref[...])
        if has_res:
            dh = dh + dres_ref[...]
        dh_ref[...] = dh
        dhb_ref[...] = (dh * copy_scale if copy_scale != 1.0 else dh).astype(BF16)

    in_specs = [_row_spec(tr, D), _row_spec(tr, D), _vec_spec(D)]
    args = [dn, h, g]
    if has_res:
        in_specs.append(_row_spec(tr, D))
        args.append(dres)
    return _pcall(
        comm, body, name=name, grid=(M // tr,),
        in_specs=in_specs,
        out_specs=[_row_spec(tr, D), _row_spec(tr, D), _vec_spec(D)],
        out_shape=[jax.ShapeDtypeStruct((M, D), F32), jax.ShapeDtypeStruct((M, D), BF16),
                   jax.ShapeDtypeStruct((1, D), F32)],
        compiler_params=_params("arbitrary"),
    )(*args)


def _sigmoid(x):
    return 1.0 / (1.0 + jnp.exp(-x))


def swiglu_bwd(a, dh, name, comm=None):
    _, M, F = a.shape
    F2 = 2 * F
    tr = min(ROW_TILE, M)

    def body(g_ref, u_ref, dh_ref, da_ref):
        gt = g_ref[...].astype(F32)
        up = u_ref[...].astype(F32)
        dh = dh_ref[...].astype(F32)
        sg = _sigmoid(gt)
        da_ref[:, :F] = (dh * up * (sg * (1.0 + gt * (1.0 - sg)))).astype(BF16)
        da_ref[:, F:] = (dh * (gt * sg)).astype(BF16)

    return _pcall(
        comm, body, name=name, grid=(M // tr,),
        in_specs=[pl.BlockSpec((None, tr, F), lambda i: (0, i, 0)), pl.BlockSpec((None, tr, F), lambda i: (1, i, 0)),
                  _row_spec(tr, F)],
        out_specs=_row_spec(tr, F2),
        out_shape=jax.ShapeDtypeStruct((M, F2), BF16),
        compiler_params=_params("parallel"),
    )(a, a, dh)


def rmscat_fwd(ya, yb, ga, gb, name, comm=None):
    M, W = ya.shape
    tr = min(ROW_TILE, M)

    def body(ya_ref, yb_ref, ga_ref, gb_ref, o_ref):
        a = ya_ref[...]
        b = yb_ref[...]
        o_ref[:, :W] = (a * _rstd(a) * ga_ref[...]).astype(BF16)
        o_ref[:, W:] = (b * _rstd(b) * gb_ref[...]).astype(BF16)

    return _pcall(
        comm, body, name=name, grid=(M // tr,),
        in_specs=[_row_spec(tr, W), _row_spec(tr, W), _vec_spec(W), _vec_spec(W)],
        out_specs=_row_spec(tr, 2 * W),
        out_shape=jax.ShapeDtypeStruct((M, 2 * W), BF16),
        compiler_params=_params("parallel"),
    )(ya, yb, ga, gb)


def rmscat_bwd(dycat, ya, yb, ga, gb, name, comm=None):
    M, W = ya.shape
    tr = min(ROW_TILE, M)

    def body(dc_ref, ya_ref, yb_ref, ga_ref, gb_ref, dya_ref, dyb_ref, dga_ref, dgb_ref):
        first = pl.program_id(0) == 0
        for y_ref, g_ref, dy_ref, dg_ref, lo in ((ya_ref, ga_ref, dya_ref, dga_ref, 0),
                                                 (yb_ref, gb_ref, dyb_ref, dgb_ref, W)):
            yv = y_ref[...]
            r = _rstd(yv)
            xhat = yv * r
            dn = dc_ref[:, lo:lo + W]
            part = jnp.sum(dn * xhat, axis=0, keepdims=True)

            @pl.when(first)
            def _():
                dg_ref[...] = part

            @pl.when(jnp.logical_not(first))
            def _():
                dg_ref[...] += part

            dy_ref[...] = _rms_bwd(dn, xhat, r, g_ref[...])

    return _pcall(
        comm, body, name=name, grid=(M // tr,),
        in_specs=[_row_spec(tr, 2 * W), _row_spec(tr, W), _row_spec(tr, W), _vec_spec(W), _vec_spec(W)],
        out_specs=[_row_spec(tr, W), _row_spec(tr, W), _vec_spec(W), _vec_spec(W)],
        out_shape=[jax.ShapeDtypeStruct((M, W), F32), jax.ShapeDtypeStruct((M, W), F32),
                   jax.ShapeDtypeStruct((1, W), F32), jax.ShapeDtypeStruct((1, W), F32)],
        compiler_params=_params("arbitrary"),
    )(dycat, ya, yb, ga, gb)


def loss_head(h, target, g, name, comm=None):
    M, D = h.shape
    tr = min(ROW_TILE, M)

    def body(h_ref, t_ref, g_ref, loss_ref, dh_ref, dhb_ref, dg_ref):
        hv = h_ref[...]
        gv = g_ref[...]
        r = _rstd(hv)
        xhat = hv * r
        err = xhat * gv - t_ref[...]
        lsum = jnp.sum(jnp.sum(err * err, axis=1, keepdims=True), axis=0, keepdims=True) * (0.5 / D)
        dy = err * (1.0 / D)
        part = jnp.sum(dy * xhat, axis=0, keepdims=True)

        @pl.when(pl.program_id(0) == 0)
        def _():
            dg_ref[...] = part
            loss_ref[...] = _zeros(loss_ref) + lsum

        @pl.when(pl.program_id(0) > 0)
        def _():
            dg_ref[...] += part
            loss_ref[...] += lsum

        dh = _rms_bwd(dy, xhat, r, gv)
        dh_ref[...] = dh
        dhb_ref[...] = (0.5 * dh).astype(BF16)

    return _pcall(
        comm, body, name=name, grid=(M // tr,),
        in_specs=[_row_spec(tr, D), _row_spec(tr, D), _vec_spec(D)],
        out_specs=[pl.BlockSpec((8, 128), lambda i: (0, 0)), _row_spec(tr, D), _row_spec(tr, D), _vec_spec(D)],
        out_shape=[jax.ShapeDtypeStruct((8, 128), F32), jax.ShapeDtypeStruct((M, D), F32),
                   jax.ShapeDtypeStruct((M, D), BF16), jax.ShapeDtypeStruct((1, D), F32)],
        compiler_params=_params("arbitrary"),
    )(h, target, g)


_GELU_C = math.sqrt(2.0 / math.pi)


def _gelu(x):
    return 0.5 * x * (1.0 + jnp.tanh(_GELU_C * (x + 0.044715 * (x * x * x))))


def _gelu_grad(x):
    t = jnp.tanh(_GELU_C * (x + 0.044715 * (x * x * x)))
    return 0.5 * (1.0 + t) + 0.5 * x * (1.0 - t * t) * (_GELU_C * (1.0 + 3.0 * 0.044715 * (x * x)))


def _sgu_mask():
    t = lax.broadcasted_iota(jnp.int32, (SGU_BLOCK, SGU_BLOCK), 0) // CHUNK
    s = lax.broadcasted_iota(jnp.int32, (SGU_BLOCK, SGU_BLOCK), 1) // CHUNK
    return s <= t


def _layernorm_stats(v):
    mu = jnp.mean(v, axis=-1, keepdims=True)
    cen = v - mu
    rstd = lax.rsqrt(jnp.mean(cen * cen, axis=-1, keepdims=True) + EPS)
    return cen * rstd, rstd


def sgu_fwd(z, ln_g, ln_b, w_s, b_t, name, comm=None):
    S = z.shape[0]

    def body(zu_ref, zv_ref, lg_ref, lb_ref, w_ref, bt_ref, o_ref):
        mask = _sgu_mask()
        for g in range(G_A):
            cols = slice(g * GA_DIM, (g + 1) * GA_DIM)
            u = _gelu(zu_ref[:, cols])
            vhat, _ = _layernorm_stats(_gelu(zv_ref[:, cols]))
            vln = vhat * lg_ref[:, cols] + lb_ref[:, cols]
            w = jnp.where(mask, w_ref[g], 0.0).astype(BF16)
            mixed = _dot(w, vln.astype(BF16)) + bt_ref[:, g:g + 1]
            o_ref[:, cols] = u * mixed

    return _pcall(
        comm, body, name=name, grid=(S // SGU_BLOCK,),
        in_specs=[_row_spec(SGU_BLOCK, W_A, 0), _row_spec(SGU_BLOCK, W_A, 1), _vec_spec(W_A), _vec_spec(W_A),
                  pl.BlockSpec((G_A, SGU_BLOCK, SGU_BLOCK), lambda i: (0, 0, 0)),
                  pl.BlockSpec((SGU_BLOCK, G_A), lambda i: (0, 0))],
        out_specs=_row_spec(SGU_BLOCK, W_A),
        out_shape=jax.ShapeDtypeStruct((S, W_A), F32),
        compiler_params=_params("parallel"),
    )(z, z, ln_g, ln_b, w_s, b_t)


def sgu_bwd(z, dya, ln_g, ln_b, w_s, b_t, name, comm=None):
    S = z.shape[0]
    nblk = S // SGU_BLOCK

    def body(zu_ref, zv_ref, dy_ref, lg_ref, lb_ref, w_ref, bt_ref,
             dz_ref, dlg_ref, dlb_ref, dw_ref, db_ref, dmix_acc):
        step = pl.program_id(0)
        mask = _sgu_mask()

        @pl.when(step == 0)
        def _():
            dlg_ref[...] = _zeros(dlg_ref)
            dlb_ref[...] = _zeros(dlb_ref)
            dw_ref[...] = _zeros(dw_ref)
            dmix_acc[...] = _zeros(dmix_acc)

        for g in range(G_A):
            cols = slice(g * GA_DIM, (g + 1) * GA_DIM)
            zu = zu_ref[:, cols]
            zv = zv_ref[:, cols]
            u = _gelu(zu)
            vhat, rstd = _layernorm_stats(_gelu(zv))
            lg = lg_ref[:, cols]
            vln = (vhat * lg + lb_ref[:, cols]).astype(BF16)
            w = jnp.where(mask, w_ref[g], 0.0)
            mixed = _dot(w.astype(BF16), vln) + bt_ref[:, g:g + 1]
            dy = dy_ref[:, cols]
            du = dy * mixed
            dmixed = dy * u
            dmixed_b = dmixed.astype(BF16)
            dmix_acc[g] += dmixed
            dw_ref[g] += jnp.where(mask, _dot_nt(dmixed_b, vln), 0.0)
            dvln = _dot(w.T.astype(BF16), dmixed_b)
            dlb_ref[:, cols] += jnp.sum(dvln, axis=0, keepdims=True)
            dlg_ref[:, cols] += jnp.sum(dvln * vhat, axis=0, keepdims=True)
            dvhat = dvln * lg
            dv = rstd * (dvhat - jnp.mean(dvhat, axis=-1, keepdims=True)
                         - vhat * jnp.mean(dvhat * vhat, axis=-1, keepdims=True))
            dz_ref[:, cols] = (du * _gelu_grad(zu)).astype(BF16)
            dz_ref[:, W_A + g * GA_DIM:W_A + (g + 1) * GA_DIM] = (dv * _gelu_grad(zv)).astype(BF16)

        @pl.when(step == nblk - 1)
        def _():
            for g in range(G_A):
                db_ref[g] = jnp.sum(dmix_acc[g], axis=1, keepdims=True)

    whole3 = lambda shape: pl.BlockSpec(shape, lambda i: (0, 0, 0))
    return _pcall(
        comm, body, name=name, grid=(nblk,),
        in_specs=[_row_spec(SGU_BLOCK, W_A, 0), _row_spec(SGU_BLOCK, W_A, 1), _row_spec(SGU_BLOCK, W_A),
                  _vec_spec(W_A), _vec_spec(W_A), whole3((G_A, SGU_BLOCK, SGU_BLOCK)),
                  pl.BlockSpec((SGU_BLOCK, G_A), lambda i: (0, 0))],
        out_specs=[_row_spec(SGU_BLOCK, 2 * W_A), _vec_spec(W_A), _vec_spec(W_A),
                   whole3((G_A, SGU_BLOCK, SGU_BLOCK)), whole3((G_A, SGU_BLOCK, 1))],
        out_shape=[jax.ShapeDtypeStruct((S, 2 * W_A), BF16), jax.ShapeDtypeStruct((1, W_A), F32),
                   jax.ShapeDtypeStruct((1, W_A), F32), jax.ShapeDtypeStruct((G_A, SGU_BLOCK, SGU_BLOCK), F32),
                   jax.ShapeDtypeStruct((G_A, SGU_BLOCK, 1), F32)],
        scratch_shapes=[pltpu.VMEM((G_A, SGU_BLOCK, SGU_BLOCK), F32)],
        compiler_params=_params("arbitrary"),
    )(z, z, dya, ln_g, ln_b, w_s, b_t)


def _log_sigmoid(z):
    return jnp.minimum(z, 0.0) - jnp.log(1.0 + jnp.exp(-jnp.abs(z)))


def _suffix_sum(x, upper):
    hi = x.astype(BF16)
    rem = x - hi.astype(F32)
    mid = rem.astype(BF16)
    lo = (rem - mid.astype(F32)).astype(BF16)
    return _dot(hi, upper) + _dot(mid, upper) + _dot(lo, upper)


SB_ROWS = 1024
_SB_SUB = SB_ROWS // Q_BLOCK


def _sb_upper():
    row = lax.broadcasted_iota(jnp.int32, (Q_BLOCK, Q_BLOCK), 0)
    col = lax.broadcasted_iota(jnp.int32, (Q_BLOCK, Q_BLOCK), 1)
    return (row > col).astype(BF16)


def _sb_sweep(step, tile):
    for r in reversed(range(_SB_SUB)):
        tile(step * _SB_SUB + r, r * Q_BLOCK)

    def group(g, _):
        base = (step - 1 - g) * _SB_SUB
        for r in reversed(range(_SB_SUB)):
            tile(base + r, None)
        return 0

    lax.fori_loop(0, step, group, 0)


def _sb_causal(n):
    return lax.broadcasted_iota(jnp.int32, (n, Q_BLOCK), 1) < lax.broadcasted_iota(jnp.int32, (n, Q_BLOCK), 0)


def _sb_col(part):
    return (2 * W_A + part * (H_B * DH_B)) // DH_B


def _sb_q_spec():
    return pl.BlockSpec((SB_ROWS, DH_B), lambda h, i: (i, _sb_col(0) + h))


def _sb_kv_spec(S, part):
    return pl.BlockSpec((S, DH_B), lambda h, i: (0, _sb_col(part) + h))


def sb_fwd(z, name, comm=None):
    S = z.shape[0]
    scale = DH_B ** -0.5

    def body(q_ref, k_ref, v_ref, o_ref, q_b, c_l1m):
        step = pl.program_id(1)
        q_b[...] = (q_ref[...] * scale).astype(BF16)
        o_ref[...] = _zeros(o_ref)
        c_l1m[...] = _zeros(c_l1m)
        upper = _sb_upper()

        def tile(j, row0):
            rq = slice(row0 or 0, SB_ROWS)
            causal = None if row0 is None else _sb_causal(SB_ROWS - row0)
            rows = pl.ds(pl.multiple_of(j * Q_BLOCK, Q_BLOCK), Q_BLOCK)
            zz = _dot_nt(q_b[rq, :], k_ref[rows, :].astype(BF16))
            lb = _log_sigmoid(zz)
            l1m = lb - zz
            if causal is not None:
                l1m = jnp.where(causal, l1m, 0.0)
            a = jnp.exp(lb + _suffix_sum(l1m, upper) + c_l1m[rq, :])
            if causal is not None:
                a = jnp.where(causal, a, 0.0)
            o_ref[rq, :] += _dot(a.astype(BF16), v_ref[rows, :].astype(BF16))
            c_l1m[rq, :] += jnp.sum(l1m, axis=1, keepdims=True)

        _sb_sweep(step, tile)

    return _pcall(
        comm, body, name=name, grid=(H_B, S // SB_ROWS),
        in_specs=[_sb_q_spec(), _sb_kv_spec(S, 1), _sb_kv_spec(S, 2)],
        out_specs=pl.BlockSpec((SB_ROWS, DH_B), lambda h, i: (i, h)),
        out_shape=jax.ShapeDtypeStruct((S, H_B * DH_B), F32),
        scratch_shapes=[pltpu.VMEM((SB_ROWS, DH_B), BF16), pltpu.VMEM((SB_ROWS, 1), F32)],
        compiler_params=_params("parallel", "parallel"),
    )(z, z, z)


def sb_bwd(z, out, dout, name, comm=None):
    S = z.shape[0]
    nstep = S // SB_ROWS
    scale = DH_B ** -0.5

    def body(q_ref, k_ref, v_ref, o_ref, do_ref, dq_ref, dk_ref, dv_ref,
             dq_acc, dkt_acc, dvt_acc, q_b, do_b, qt_b, dot_b, g_left, c_l1m):
        step = pl.program_id(1)

        @pl.when(step == 0)
        def _():
            dkt_acc[...] = _zeros(dkt_acc)
            dvt_acc[...] = _zeros(dvt_acc)

        q_b[...] = (q_ref[...] * scale).astype(BF16)
        do_b[...] = do_ref[...].astype(BF16)
        qt_b[...] = (q_ref[...].T * scale).astype(BF16)
        dot_b[...] = do_ref[...].T.astype(BF16)
        g_left[...] = jnp.sum(do_b[...].astype(F32) * o_ref[...], axis=1, keepdims=True)
        dq_acc[...] = _zeros(dq_acc)
        c_l1m[...] = _zeros(c_l1m)
        upper = _sb_upper()

        def tile(j, row0):
            rq = slice(row0 or 0, SB_ROWS)
            causal = None if row0 is None else _sb_causal(SB_ROWS - row0)
            rows = pl.ds(pl.multiple_of(j * Q_BLOCK, Q_BLOCK), Q_BLOCK)
            q, do_t = q_b[rq, :], do_b[rq, :]
            k_j = k_ref[rows, :].astype(BF16)
            zz = _dot_nt(q, k_j)
            lb = _log_sigmoid(zz)
            l1m = lb - zz
            if causal is not None:
                l1m = jnp.where(causal, l1m, 0.0)
            a = jnp.exp(lb + _suffix_sum(l1m, upper) + c_l1m[rq, :])
            if causal is not None:
                a = jnp.where(causal, a, 0.0)
            a_b = a.astype(BF16)
            dvt_acc[:, rows] += _dot(dot_b[:, rq], a_b)
            gmat = a_b.astype(F32) * _dot_nt(do_t, v_ref[rows, :].astype(BF16))
            before = g_left[rq, :] - gmat - _suffix_sum(gmat, upper)
            sig = jnp.exp(lb)
            dz = gmat * (1.0 - sig) - sig * before
            if causal is not None:
                dz = jnp.where(causal, dz, 0.0)
            dz_b = dz.astype(BF16)
            dkt_acc[:, rows] += _dot(qt_b[:, rq], dz_b)
            dq_acc[rq, :] += _dot(dz_b, k_j)
            c_l1m[rq, :] += jnp.sum(l1m, axis=1, keepdims=True)
            g_left[rq, :] -= jnp.sum(gmat, axis=1, keepdims=True)

        _sb_sweep(step, tile)
        dq_ref[...] = (dq_acc[...] * scale).astype(BF16)

        @pl.when(step == nstep - 1)
        def _():
            dk_ref[...] = dkt_acc[...].T.astype(BF16)
            dv_ref[...] = dvt_acc[...].T.astype(BF16)

    rows_spec = pl.BlockSpec((SB_ROWS, DH_B), lambda h, i: (i, h))
    head_spec = pl.BlockSpec((S, DH_B), lambda h, i: (0, h))
    out_sds = jax.ShapeDtypeStruct((S, H_B * DH_B), BF16)
    return _pcall(
        comm, body, name=name, grid=(H_B, nstep),
        in_specs=[_sb_q_spec(), _sb_kv_spec(S, 1), _sb_kv_spec(S, 2), rows_spec, rows_spec],
        out_specs=[rows_spec, head_spec, head_spec],
        out_shape=[out_sds, out_sds, out_sds],
        scratch_shapes=[pltpu.VMEM((SB_ROWS, DH_B), F32)] + [pltpu.VMEM((DH_B, S), F32)] * 2
        + [pltpu.VMEM((SB_ROWS, DH_B), BF16)] * 2 + [pltpu.VMEM((DH_B, SB_ROWS), BF16)] * 2
        + [pltpu.VMEM((SB_ROWS, 1), F32)] * 2,
        compiler_params=_params("parallel", "arbitrary"),
    )(z, z, z, out, dout)


def _softmax(s):
    e = jnp.exp(s - jnp.max(s, axis=-1, keepdims=True))
    return e / jnp.sum(e, axis=-1, keepdims=True)


def xattn_fwd(qc, kv, name, comm=None):
    S, D = qc.shape
    tr = min(ROW_TILE, S)

    def body(q_ref, kv_ref, o_ref):
        for h in range(X_HEADS):
            cols = slice(h * X_DH, (h + 1) * X_DH)
            p = _softmax(_dot_nt(q_ref[:, cols], kv_ref[:, cols]))
            o_ref[:, cols] = _dot(p.astype(BF16), kv_ref[:, D + h * X_DH:D + (h + 1) * X_DH]).astype(BF16)

    return _pcall(
        comm, body, name=name, grid=(S // tr,),
        in_specs=[_row_spec(tr, D), pl.BlockSpec((N_MEM, 2 * D), lambda i: (0, 0))],
        out_specs=_row_spec(tr, D),
        out_shape=jax.ShapeDtypeStruct((S, D), BF16),
        compiler_params=_params("parallel"),
    )(qc, kv)


def xattn_bwd(qc, kv, do, name, comm=None):
    S, D = qc.shape
    tr = min(ROW_TILE, S)
    nstep = S // tr
    scale = X_DH ** -0.5

    def body(q_ref, kv_ref, do_ref, dq_ref, dkv_ref, acc):
        step = pl.program_id(0)

        @pl.when(step == 0)
        def _():
            acc[...] = _zeros(acc)

        for h in range(X_HEADS):
            cols = slice(h * X_DH, (h + 1) * X_DH)
            vcols = slice(D + h * X_DH, D + (h + 1) * X_DH)
            q = q_ref[:, cols]
            k = kv_ref[:, cols]
            do_h = do_ref[:, cols]
            p = _softmax(_dot_nt(q, k))
            dp = _dot_nt(do_h, kv_ref[:, vcols])
            acc[:, vcols] += _dot_tn(p.astype(BF16), do_h)
            ds = (p * (dp - jnp.sum(p * dp, axis=-1, keepdims=True))).astype(BF16)
            dq_ref[:, cols] = (_dot(ds, k) * scale).astype(BF16)
            acc[:, cols] += _dot_tn(ds, q)

        @pl.when(step == nstep - 1)
        def _():
            dkv_ref[...] = acc[...].astype(BF16)

    whole = pl.BlockSpec((N_MEM, 2 * D), lambda i: (0, 0))
    return _pcall(
        comm, body, name=name, grid=(nstep,),
        in_specs=[_row_spec(tr, D), whole, _row_spec(tr, D)],
        out_specs=[_row_spec(tr, D), whole],
        out_shape=[jax.ShapeDtypeStruct((S, D), BF16), jax.ShapeDtypeStruct((N_MEM, 2 * D), BF16)],
        scratch_shapes=[pltpu.VMEM((N_MEM, 2 * D), F32)],
        compiler_params=_params("arbitrary"),
    )(qc, kv, do)


def _row_tile(rows, cap=128):
    return max(t for t in range(16, cap + 1, 16) if rows % t == 0)


def cast_bf16(w, name, comm=None):
    R, C = w.shape
    tr = _row_tile(R, 256)

    def body(w_ref, o_ref):
        o_ref[...] = w_ref[...].astype(BF16)

    return _pcall(
        comm, body, name=name, grid=(R // tr,),
        in_specs=[_row_spec(tr, C)], out_specs=_row_spec(tr, C),
        out_shape=jax.ShapeDtypeStruct((R, C), BF16),
        compiler_params=_params("parallel"),
    )(w)


def adamw(parts, w, m, v, name, comm=None):
    R, C = w.shape
    n_parts = parts.shape[0]
    tr = _row_tile(R, 256)
    c1 = 1.0 - ADAM_B1 ** ADAM_STEP
    c2 = 1.0 - ADAM_B2 ** ADAM_STEP

    def body(p_ref, w_ref, m_ref, v_ref, g_ref, d_ref, mo_ref, vo_ref):
        g = p_ref[0].astype(F32)
        for p in range(1, n_parts):
            g = g + p_ref[p].astype(F32)
        m_new = ADAM_B1 * m_ref[...] + (1.0 - ADAM_B1) * g
        v_new = ADAM_B2 * v_ref[...] + (1.0 - ADAM_B2) * (g * g)
        g_ref[...] = g
        mo_ref[...] = m_new
        vo_ref[...] = v_new
        d_ref[...] = -ADAM_LR * ((m_new / c1) / (jnp.sqrt(v_new / c2) + ADAM_EPS) + ADAM_WD * w_ref[...])

    spec = _row_spec(tr, C)
    sds = jax.ShapeDtypeStruct((R, C), F32)
    return _pcall(
        comm, body, name=name, grid=(R // tr,),
        in_specs=[pl.BlockSpec((n_parts, tr, C), lambda i: (0, i, 0)), spec, spec, spec],
        out_specs=[spec, spec, spec, spec],
        out_shape=[sds, sds, sds, sds],
        compiler_params=_params("parallel"),
    )(parts, w, m, v)


def pair_sum(parts, from_sibling, core, name):
    _, R, C = parts.shape
    tr = _row_tile(R, 1024)

    def body(core_ref, p_ref, s_ref, o_ref):
        o_ref[...] = (p_ref[...].astype(F32) + s_ref[...].astype(F32)).astype(o_ref.dtype)

    return pl.pallas_call(
        body, name=name,
        grid_spec=pltpu.PrefetchScalarGridSpec(
            num_scalar_prefetch=1, grid=(4, R // tr),
            in_specs=[pl.BlockSpec((None, tr, C), lambda q, i, core_ref: (2 * q + core_ref[0], i, 0)),
                      pl.BlockSpec((None, tr, C), lambda q, i, core_ref: (q, i, 0))],
            out_specs=pl.BlockSpec((None, tr, C), lambda q, i, core_ref: (q, i, 0))),
        out_shape=jax.ShapeDtypeStruct((4, R, C), BF16),
        compiler_params=_params("parallel", "parallel"),
    )(core, parts, from_sibling)


def add2(a, b, name, comm=None):
    R, C = a.shape
    tr = _row_tile(R, 256)

    def body(a_ref, b_ref, o_ref):
        o_ref[...] = a_ref[...] + b_ref[...]

    spec = _row_spec(tr, C)
    return _pcall(
        comm, body, name=name, grid=(R // tr,), in_specs=[spec, spec], out_specs=spec,
        out_shape=jax.ShapeDtypeStruct((R, C), F32), compiler_params=_params("parallel"),
    )(a, b)


def _place():
    return lax.axis_index("x"), lax.axis_index("y"), lax.axis_index("c")


class Comm:
    def __init__(self, arrays, out_shapes, n_remote, n_local, start, finish, aliases=None):
        self.arrays, self.out_shapes = list(arrays), list(out_shapes)
        self.n_remote, self.n_local = n_remote, max(n_local, 1)
        self.start, self.finish = start, finish
        self.aliases = dict(aliases or {})
        self.sizes = [len(self.out_shapes)]

    def sem_shapes(self):
        return [pltpu.SemaphoreType.DMA((self.n_remote,)), pltpu.SemaphoreType.DMA((self.n_remote,)),
                pltpu.SemaphoreType.DMA((self.n_local,))]


class _Shifted:
    def __init__(self, ref, offset):
        self.ref, self.offset = ref, offset

    @property
    def at(self):
        return self

    def __getitem__(self, k):
        return self.ref.at[self.offset + k]


def merge_comms(comms):
    comms = [c for c in comms if c is not None]
    if not comms:
        return None

    def each(method):
        def run(ins, outs, sems):
            i = o = r = l = 0
            for c in comms:
                sub = (_Shifted(sems[0], r), _Shifted(sems[1], r), _Shifted(sems[2], l))
                getattr(c, method)(ins[i:i + len(c.arrays)], outs[o:o + len(c.out_shapes)], sub)
                i, o, r, l = i + len(c.arrays), o + len(c.out_shapes), r + c.n_remote, l + c.n_local
        return run

    aliases, i, o = {}, 0, 0
    for c in comms:
        aliases.update({i + a: o + b for a, b in c.aliases.items()})
        i, o = i + len(c.arrays), o + len(c.out_shapes)
    merged = Comm([a for c in comms for a in c.arrays], [s for c in comms for s in c.out_shapes],
                  sum(c.n_remote for c in comms), sum(c.n_local for c in comms), each("start"), each("finish"), aliases)
    merged.sizes = [len(c.out_shapes) for c in comms]
    return merged


def split_results(comm, results):
    out, i = [], 0
    for n in comm.sizes:
        out.append(list(results[i:i + n]))
        i += n
    return out


def run_comm(comm, name):
    n_in, n_out = len(comm.arrays), len(comm.out_shapes)

    def body(*refs):
        ins, outs, sems = refs[:n_in], refs[n_in:n_in + n_out], refs[n_in + n_out:]
        comm.start(ins, outs, sems)
        comm.finish(ins, outs, sems)

    return pl.pallas_call(
        body, name=name, in_specs=[ANY] * n_in, out_specs=[ANY] * n_out, out_shape=comm.out_shapes,
        scratch_shapes=comm.sem_shapes(), input_output_aliases=comm.aliases,
    )(*comm.arrays)


def _remote(src, dst, sems, k, to):
    return pltpu.make_async_remote_copy(src_ref=src, dst_ref=dst, send_sem=sems[0].at[k], recv_sem=sems[1].at[k],
                                        device_id=to, device_id_type=MESH)


_AG_COPIES = 13


def comm_all_gather(shards, rows=None, into=None):
    n = len(shards)
    row0, nrows = rows if rows is not None else (0, None)

    def parties():
        x, y, c = _place()
        return (x, y, c), (x, y, 1 - c), [(1 - x, y), (x, 1 - y), (1 - x, 1 - y)]

    def span(w, half=None):
        count = nrows if nrows is not None else shards[w].shape[0]
        if half is None:
            return pl.ds(row0, count)
        return pl.ds(row0 + half * (count // 2), count // 2)

    def slab(outs, w, dev, half=None):
        return outs[w].at[4 * dev[0] + 2 * dev[1] + dev[2], span(w, half)]

    def own(ins, outs, sems):
        me, sibling, chips = parties()
        local = [pltpu.make_async_copy(ins[w].at[span(w)], slab(outs, w, me), sems[2].at[w]) for w in range(n)]
        first = []
        for w in range(n):
            k = _AG_COPIES * w
            first.append(_remote(ins[w].at[span(w)], slab(outs, w, me), sems, k, sibling))
            first += [_remote(ins[w].at[span(w, h)], slab(outs, w, me, h), sems, k + 1 + 2 * j + h, (*chip, me[2]))
                      for h in range(2) for j, chip in enumerate(chips)]
        return local, first

    def start(ins, outs, sems):
        local, first = own(ins, outs, sems)
        for cp in local + first:
            cp.start()

    def finish(ins, outs, sems):
        me, sibling, chips = parties()
        local, first = own(ins, outs, sems)
        passed = []
        for w in range(n):
            k = _AG_COPIES * w
            for h in range(2):
                for j, chip in enumerate(chips):
                    got = slab(outs, w, (*chip, me[2]), h)
                    _remote(got, got, sems, k + 1 + 2 * j + h, me).wait_recv()
                    cp = _remote(got, got, sems, k + 7 + 2 * j + h, sibling)
                    cp.start()
                    passed.append(cp)
        for w in range(n):
            k = _AG_COPIES * w
            got = slab(outs, w, sibling)
            _remote(got, got, sems, k, me).wait_recv()
            for h in range(2):
                for j, chip in enumerate(chips):
                    got = slab(outs, w, (*chip, sibling[2]), h)
                    _remote(got, got, sems, k + 7 + 2 * j + h, me).wait_recv()
        for cp in first + passed:
            cp.wait_send()
        for cp in local:
            cp.wait()

    out_shapes = [jax.ShapeDtypeStruct((N_DEV,) + s.shape, s.dtype) for s in shards]
    arrays = list(shards) + (list(into) if into is not None else [])
    aliases = {n + w: w for w in range(n)} if into is not None else None
    return Comm(arrays, out_shapes, _AG_COPIES * n, n, start, finish, aliases)


def comm_pairs(items):
    slabbed = [a.ndim == 3 for a in items]
    first = [sum(4 if s else 1 for s in slabbed[:w]) for w in range(len(items))]

    def copies(ins, outs, sems):
        x, y, c = _place()
        sibling = (x, y, 1 - c)
        cps = []
        for w, s in enumerate(slabbed):
            if s:
                cps += [_remote(ins[w].at[2 * q + (1 - c)], outs[w].at[q], sems, first[w] + q, sibling) for q in range(4)]
            else:
                cps.append(_remote(ins[w], outs[w], sems, first[w], sibling))
        return cps

    def start(ins, outs, sems):
        for cp in copies(ins, outs, sems):
            cp.start()

    def finish(ins, outs, sems):
        for cp in copies(ins, outs, sems):
            cp.wait()

    out_shapes = [jax.ShapeDtypeStruct(((4,) + a.shape[1:]) if s else a.shape, a.dtype) for a, s in zip(items, slabbed)]
    return Comm(items, out_shapes, sum(4 if s else 1 for s in slabbed), 0, start, finish)


def comm_chips(items, rows=None, into=None):
    n = len(items)
    slabbed = [a.ndim == 3 for a in items]

    def span(w):
        return pl.ds(*rows) if rows is not None else pl.ds(0, items[w].shape[-2])

    def copies(ins, outs, sems):
        x, y, c = _place()
        mine = 2 * x + y
        local = [pltpu.make_async_copy(ins[w].at[mine, span(w)] if slabbed[w] else ins[w].at[span(w)],
                                       outs[w].at[mine, span(w)], sems[2].at[w]) for w in range(n)]
        remote = []
        for w in range(n):
            for j, (px, py) in enumerate([(1 - x, y), (x, 1 - y), (1 - x, 1 - y)]):
                src = ins[w].at[2 * px + py, span(w)] if slabbed[w] else ins[w].at[span(w)]
                remote.append(_remote(src, outs[w].at[mine, span(w)], sems, 3 * w + j, (px, py, c)))
        return local, remote

    def start(ins, outs, sems):
        local, remote = copies(ins, outs, sems)
        for cp in local + remote:
            cp.start()

    def finish(ins, outs, sems):
        local, remote = copies(ins, outs, sems)
        for cp in remote + local:
            cp.wait()

    out_shapes = [jax.ShapeDtypeStruct(a.shape if s else (4,) + a.shape, a.dtype) for a, s in zip(items, slabbed)]
    arrays = list(items) + (list(into) if into is not None else [])
    aliases = {n + w: w for w in range(n)} if into is not None else None
    return Comm(arrays, out_shapes, 3 * n, n, start, finish, aliases)


_SMALL = ("ffn1_norm", "mix_norm", "ln_v_gain", "ln_v_bias", "spatial_w", "spatial_b", "gnorm_a", "gnorm_b",
          "cross_norm", "mem_norm", "ffn2_norm", "final_norm")
_BIG = ("ffn1_w_in", "ffn1_w_out", "w_mix_in", "w_mix_out", "w_cq", "w_ckv", "w_co", "ffn2_w_in", "ffn2_w_out")
_COL_SHARDED = ("ffn1_w_in", "w_mix_in", "w_ckv", "ffn2_w_in")
_ORDER = ("ffn1_norm", "ffn1_w_in", "ffn1_w_out", "mix_norm", "w_mix_in", "ln_v_gain", "ln_v_bias", "spatial_w",
          "spatial_b", "gnorm_a", "gnorm_b", "w_mix_out", "cross_norm", "mem_norm", "w_cq", "w_ckv", "w_co",
          "ffn2_norm", "ffn2_w_in", "ffn2_w_out", "final_norm")


_SMALL_PAD = 120


def _rows128(a):
    return a.reshape(-1, 128)


def kernel(x, mem, ffn1_norm, ffn1_w_in, ffn1_w_out, mix_norm, w_mix_in, ln_v_gain, ln_v_bias, spatial_w, spatial_b, gnorm_a, gnorm_b, w_mix_out, cross_norm, mem_norm, w_cq, w_ckv, w_co, ffn2_norm, ffn2_w_in, ffn2_w_out, final_norm, loss_target, m_ffn1_norm, m_ffn1_w_in, m_ffn1_w_out, m_mix_norm, m_w_mix_in, m_ln_v_gain, m_ln_v_bias, m_spatial_w, m_spatial_b, m_gnorm_a, m_gnorm_b, m_w_mix_out, m_cross_norm, m_mem_norm, m_w_cq, m_w_ckv, m_w_co, m_ffn2_norm, m_ffn2_w_in, m_ffn2_w_out, m_final_norm, v_ffn1_norm, v_ffn1_w_in, v_ffn1_w_out, v_mix_norm, v_w_mix_in, v_ln_v_gain, v_ln_v_bias, v_spatial_w, v_spatial_b, v_gnorm_a, v_gnorm_b, v_w_mix_out, v_cross_norm, v_mem_norm, v_w_cq, v_w_ckv, v_w_co, v_ffn2_norm, v_ffn2_w_in, v_ffn2_w_out, v_final_norm):
    given = dict(locals())
    wts = {k: given[k] for k in _ORDER}
    mom = {k: given["m_" + k] for k in _ORDER}
    var = {k: given["v_" + k] for k in _ORDER}

    D = D_MODEL
    xs = x.reshape(-1, D)
    mems = mem.reshape(-1, D)
    tgt = loss_target.reshape(-1, D)
    vec = lambda a: a.reshape(1, -1)
    g1, gmix, gcross, gmem, g2, gfin = (vec(wts[k]) for k in
                                        ("ffn1_norm", "mix_norm", "cross_norm", "mem_norm", "ffn2_norm", "final_norm"))
    ln_g, ln_b, ga, gb = (vec(wts[k]) for k in ("ln_v_gain", "ln_v_bias", "gnorm_a", "gnorm_b"))
    w_s = spatial_w.reshape(G_A, SGU_BLOCK, SGU_BLOCK)
    b_t = spatial_b.reshape(G_A, SGU_BLOCK).T

    shard2d = {k: wts[k].reshape(wts[k].shape[1:]) for k in _BIG}
    shard_b = {k: cast_bf16(shard2d[k], f"cast_{k}") for k in _BIG}
    full = {}

    landing, rows_done = {}, {}

    def gathering(pieces, fn, *args, **kw):
        pieces = [p if isinstance(p, tuple) else (p, None) for p in pieces]
        comm = merge_comms([comm_all_gather([shard_b[k]], rows, [landing[k]] if k in landing else None)
                            for k, rows in pieces])
        out, got = fn(*args, comm=comm, **kw)
        for (k, rows), (g,) in zip(pieces, split_results(comm, got)):
            landing[k] = g
            rows_done[k] = rows_done.get(k, 0) + (rows[1] if rows is not None else shard_b[k].shape[0])
            if rows_done[k] == shard_b[k].shape[0]:
                full[k] = g if k in _COL_SHARDED else g.reshape(-1, g.shape[2])
        return out

    n1 = gathering(("ffn1_w_in",), rms_fwd, xs, g1, "f_n1")
    a1, hsw1 = gathering(("ffn1_w_out",), mm_swiglu_g, n1, full["ffn1_w_in"], "f_a1")
    h1 = gathering(("w_mix_in",), mm_nn, hsw1, full["ffn1_w_out"], F32, "f_h1", scale=0.5, res=xs)
    n2 = gathering(("w_cq",), rms_fwd, h1, gmix, "f_n2")
    z = gathering(("w_mix_out", "w_co"), mm_nn_g, n2, full["w_mix_in"], F32, "f_z")
    ya = gathering((("w_ckv", (0, 512)),), sgu_fwd, z, ln_g, ln_b, w_s, b_t, "f_sgu")
    yb = gathering((("w_ckv", (512, 1536)), ("ffn2_w_in", (0, 512))), sb_fwd, z, "f_sb")
    ycat = gathering((("ffn2_w_in", (512, 128)),), rmscat_fwd, ya, yb, ga, gb, "f_ycat")
    h2 = gathering((("ffn2_w_in", (640, 256)),), mm_nn, ycat, full["w_mix_out"], F32, "f_h2", res=h1)
    n3 = gathering((("ffn2_w_in", (896, 128)),), rms_fwd, h2, gcross, "f_n3")
    memn = rms_fwd(mems, gmem, "f_memn")
    qc = gathering((("ffn2_w_in", (1024, 256)),), mm_nn, n3, full["w_cq"], BF16, "f_qc", scale=X_DH ** -0.5)
    kv = gathering((("ffn2_w_in", (1280, 128)),), mm_nn_g, memn, full["w_ckv"], BF16, "f_kv")
    o = gathering((("ffn2_w_in", (1408, 128)),), xattn_fwd, qc, kv, "f_xattn")
    h3 = gathering((("ffn2_w_in", (1536, 256)),), mm_nn, o, full["w_co"], F32, "f_h3", res=h2)
    n4 = gathering((("ffn2_w_in", (1792, 256)),), rms_fwd, h3, g2, "f_n4")
    a2, hsw2 = gathering(("ffn2_w_out",), mm_swiglu_g, n4, full["ffn2_w_in"], "f_a2")
    h4 = mm_nn(hsw2, full["ffn2_w_out"], F32, "f_h4", scale=0.5, res=h3)

    grads, parts, sums, recv = {}, {}, {}, {}
    core = lax.axis_index("c").astype(jnp.int32).reshape(1)

    def partial_of(k, g):
        grads[k] = g
        parts[k] = g if g.ndim == 3 else g.reshape(N_DEV, -1, g.shape[1])

    def reducing(pairs, chips, fn, *args, **kw):
        chips = [p if isinstance(p, tuple) else (p, None) for p in chips]
        comms = [comm_pairs([parts[k] for k in pairs])] if pairs else []
        comms += [comm_chips([sums[k]], rows, [recv[k]] if k in recv else None) for k, rows in chips]
        comm = merge_comms(comms)
        out, got = fn(*args, comm=comm, **kw)
        got = split_results(comm, got)
        if pairs:
            for k, r in zip(pairs, got.pop(0)):
                sums[k] = pair_sum(parts[k], r, core, f"pair_sum_{k}")
        for (k, _), (r,) in zip(chips, got):
            recv[k] = r
        return out

    loss_part, dh4, df2, grads["final_norm"] = loss_head(h4, tgt, gfin, "loss_head")
    partial_of("ffn2_w_out", mm_tn(hsw2, df2, "b_ffn2_dwout"))
    dhsw2 = reducing(("ffn2_w_out",), (), mm_nt, df2, full["ffn2_w_out"], BF16, "b_ffn2_dhsw")
    da2 = swiglu_bwd(a2, dhsw2, "b_ffn2_swiglu_bwd")
    partial_of("ffn2_w_in", reducing((), ("ffn2_w_out",), mm_tn_g, n4, da2, N_DEV, "b_ffn2_dwin"))
    dn4 = reducing(("ffn2_w_in",), (), mm_nt_g, da2, full["ffn2_w_in"], "b_ffn2_dn")
    dh3, dh3b, grads["ffn2_norm"] = rms_bwd(dn4, h3, g2, dh4, 1.0, "b_n4")

    partial_of("w_co", mm_tn(o, dh3b, "b_dwco"))
    do = reducing(("w_co",), (), mm_nt, dh3b, full["w_co"], BF16, "b_do")
    dqp, dkv = xattn_bwd(qc, kv, do, "b_xattn")
    partial_of("w_cq", mm_tn(n3, dqp, "b_dwcq"))
    dn3 = reducing(("w_cq",), (), mm_nt, dqp, full["w_cq"], F32, "b_dn3")
    partial_of("w_ckv", mm_tn_g(memn, dkv, N_DEV, "b_dwckv"))
    dmemn = reducing(("w_ckv",), (), mm_nt_g, dkv, full["w_ckv"], "b_dmemn")
    _, _, grads["mem_norm"] = rms_bwd(dmemn, mems, gmem, None, 1.0, "b_memn")
    dh2, dh2b, grads["cross_norm"] = rms_bwd(dn3, h2, gcross, dh3, 1.0, "b_n3")

    partial_of("w_mix_out", mm_tn(ycat, dh2b, "b_dwmixout"))
    dycat = reducing(("w_mix_out",), (), mm_nt, dh2b, full["w_mix_out"], F32, "b_dycat")
    dya, dyb, grads["gnorm_a"], grads["gnorm_b"] = rmscat_bwd(dycat, ya, yb, ga, gb, "b_ycat")
    dza, grads["ln_v_gain"], grads["ln_v_bias"], grads["spatial_w"], grads["spatial_b"] = sgu_bwd(
        z, dya, ln_g, ln_b, w_s, b_t, "b_sgu")
    dq, dk, dv = reducing((), ("ffn2_w_in", "w_co", "w_cq"), sb_bwd, z, yb, dyb, "b_sb")
    dz = jnp.concatenate([dza, dq, dk, dv], axis=1)
    partial_of("w_mix_in", reducing((), ("w_ckv",), mm_tn_g, n2, dz, N_DEV, "b_dwmixin"))
    dn2 = reducing(("w_mix_in",), ("w_mix_out",), mm_nt_g, dz, full["w_mix_in"], "b_dn2")
    dh1, dh1b, grads["mix_norm"] = reducing((), (("w_mix_in", (0, 512)),), rms_bwd, dn2, h1, gmix, dh2, 0.5, "b_n2")

    partial_of("ffn1_w_out", reducing((), (("w_mix_in", (512, 1024)),), mm_tn, hsw1, dh1b, "b_ffn1_dwout"))
    dhsw1 = reducing(("ffn1_w_out",), (("w_mix_in", (1536, 512)),), mm_nt, dh1b, full["ffn1_w_out"], BF16,
                     "b_ffn1_dhsw")
    da1 = swiglu_bwd(a1, dhsw1, "b_ffn1_swiglu_bwd")
    partial_of("ffn1_w_in", reducing((), ("ffn1_w_out",), mm_tn_g, n1, da1, N_DEV, "b_ffn1_dwin"))
    dn1 = reducing(("ffn1_w_in",), (), mm_nt_g, da1, full["ffn1_w_in"], "b_ffn1_dn")
    out_g, out_d, out_m, out_v = {}, {}, {}, {}

    def update(k, last_rows=None):
        args = (shard2d[k], mom[k].reshape(shard2d[k].shape), var[k].reshape(shard2d[k].shape), f"adamw_{k}")
        if last_rows is None:
            res = adamw(recv[k], *args)
        else:
            res = reducing((), (("ffn1_w_in", last_rows),), adamw, recv[k], *args)
        out_g[k], out_d[k], out_m[k], out_v[k] = (t.reshape(wts[k].shape) for t in res)

    (dx, _, grads["ffn1_norm"]) = reducing((), (("ffn1_w_in", (0, 256)),), rms_bwd, dn1, xs, g1, dh1, 1.0, "b_n1")
    update("ffn2_w_in", (256, 384))
    update("ffn2_w_out", (640, 256))
    update("w_mix_in", (896, 256))
    update("ffn1_w_out", (1152, 256))
    update("w_ckv", (1408, 256))
    update("w_co", (1664, 128))
    update("w_cq", (1792, 128))
    update("w_mix_out", (1920, 128))
    update("ffn1_w_in")

    pack = lambda d: jnp.concatenate([_rows128(d[k]) for k in _SMALL] + [jnp.zeros((_SMALL_PAD, 128), F32)], axis=0)
    small_part = pack(grads)
    (small_sibling,) = run_comm(comm_pairs([small_part]), "comm_pairs_small")
    small_pair = add2(small_part, small_sibling, "pair_sum_small")
    (small_all,) = run_comm(comm_chips([small_pair]), "comm_chips_small")

    res = adamw(small_all, pack(wts), pack(mom), pack(var), "adamw_small")
    row = 0
    for k in _SMALL:
        nrow = wts[k].size // 128
        for dst, t in zip((out_g, out_d, out_m, out_v), res):
            dst[k] = t[row:row + nrow].reshape(wts[k].shape)
        row += nrow

    loss = lax.psum(loss_part[0, 0], ("x", "y", "c"))
    grad_x = dx.reshape(x.shape)
    return (loss, grad_x, *[out_g[k] for k in _ORDER], *[out_d[k] for k in _ORDER],
            *[out_m[k] for k in _ORDER], *[out_v[k] for k in _ORDER])
```

```python
import functools
import math

import jax
import jax.numpy as jnp
from jax import lax
from jax.experimental import pallas as pl
from jax.experimental.pallas import tpu as pltpu

F32 = jnp.float32
BF16 = jnp.bfloat16

N_DEV = 8
D_MODEL = 2048
D_FF = 5632
W_A = 1024
G_A = 8
GA_DIM = 128
SGU_BLOCK = 128
CHUNK = 64
H_B = 8
DH_B = 128
Q_BLOCK = 128
X_HEADS = 4
X_DH = 512
N_MEM = 256
EPS = 1e-6

ADAM_LR = 0.001
ADAM_B1 = 0.9
ADAM_B2 = 0.999
ADAM_EPS = 1e-08
ADAM_WD = 0.01
ADAM_STEP = 10

VMEM_LIMIT = 56 * 2**20
ROW_TILE = 256

MESH = pl.DeviceIdType.MESH
ANY = pl.BlockSpec(memory_space=pl.ANY)

_NT = (((1,), (1,)), ((), ()))
_TN = (((0,), (0,)), ((), ()))


def _params(*sem):
    return pltpu.CompilerParams(dimension_semantics=sem, vmem_limit_bytes=VMEM_LIMIT)


def _zeros(ref):
    return jnp.zeros(ref.shape, ref.dtype)


def _pcall(comm, body, *, name, grid, in_specs, out_specs, out_shape, compiler_params, scratch_shapes=()):
    if comm is None:
        return pl.pallas_call(body, name=name, grid=grid, in_specs=in_specs, out_specs=out_specs, out_shape=out_shape,
                              scratch_shapes=list(scratch_shapes), compiler_params=compiler_params)
    multi = isinstance(out_shape, (list, tuple))
    out_shapes = list(out_shape) if multi else [out_shape]
    out_specs_l = list(out_specs) if multi else [out_specs]
    n_in, n_out, n_scr = len(in_specs), len(out_shapes), len(scratch_shapes)
    n_cin, n_cout = len(comm.arrays), len(comm.out_shapes)

    def with_comm(*refs):
        ins, refs = refs[:n_in], refs[n_in:]
        cins, refs = refs[:n_cin], refs[n_cin:]
        outs, refs = refs[:n_out], refs[n_out:]
        couts, refs = refs[:n_cout], refs[n_cout:]
        scr, sems = refs[:n_scr], refs[n_scr:]
        first = functools.reduce(jnp.logical_and, [pl.program_id(a) == 0 for a in range(len(grid))])
        last = functools.reduce(jnp.logical_and, [pl.program_id(a) == grid[a] - 1 for a in range(len(grid))])
        pl.when(first)(lambda: comm.start(cins, couts, sems))
        body(*ins, *outs, *scr)
        pl.when(last)(lambda: comm.finish(cins, couts, sems))

    call = pl.pallas_call(
        with_comm, name=name, grid=grid, in_specs=list(in_specs) + [ANY] * n_cin,
        out_specs=out_specs_l + [ANY] * n_cout, out_shape=out_shapes + comm.out_shapes,
        scratch_shapes=list(scratch_shapes) + comm.sem_shapes(), compiler_params=_params(*(("arbitrary",) * len(grid))),
        input_output_aliases={n_in + i: n_out + j for i, j in comm.aliases.items()})

    def run(*args):
        res = call(*args, *comm.arrays)
        main = res[:n_out]
        return (list(main) if multi else main[0]), list(res[n_out:])

    return run


def _dot(a, b):
    return jnp.dot(a, b, preferred_element_type=F32)


def _dot_nt(a, b):
    return lax.dot_general(a, b, _NT, preferred_element_type=F32)


def _dot_tn(a, b):
    return lax.dot_general(a, b, _TN, preferred_element_type=F32)


def mm_nn_g(a, bg, out_dtype, name, tm=512, comm=None):
    M, K = a.shape
    G, _, n = bg.shape
    tm = min(tm, M)

    def body(a_ref, b_ref, o_ref):
        o_ref[...] = _dot(a_ref[...], b_ref[...]).astype(o_ref.dtype)

    return _pcall(
        comm, body, name=name, grid=(G, M // tm),
        in_specs=[pl.BlockSpec((tm, K), lambda g, m: (m, 0)),
                  pl.BlockSpec((None, K, n), lambda g, m: (g, 0, 0))],
        out_specs=pl.BlockSpec((tm, n), lambda g, m: (m, g)),
        out_shape=jax.ShapeDtypeStruct((M, G * n), out_dtype),
        compiler_params=_params("parallel", "parallel"),
    )(a, bg)


def mm_swiglu_g(a, bg, name, tm=512, comm=None):
    M, K = a.shape
    G, _, n = bg.shape
    half = G // 2
    tm = min(tm, M)

    def body(a_ref, bgate_ref, bup_ref, gu_ref, h_ref):
        av = a_ref[...]
        gate = _dot(av, bgate_ref[...])
        up = _dot(av, bup_ref[...])
        gu_ref[0] = gate.astype(BF16)
        gu_ref[1] = up.astype(BF16)
        h_ref[...] = (gate * _sigmoid(gate) * up).astype(BF16)

    return _pcall(
        comm, body, name=name, grid=(half, M // tm),
        in_specs=[pl.BlockSpec((tm, K), lambda p, m: (m, 0)),
                  pl.BlockSpec((None, K, n), lambda p, m: (p, 0, 0)),
                  pl.BlockSpec((None, K, n), lambda p, m: (p + half, 0, 0))],
        out_specs=[pl.BlockSpec((2, tm, n), lambda p, m: (0, m, p)), pl.BlockSpec((tm, n), lambda p, m: (m, p))],
        out_shape=[jax.ShapeDtypeStruct((2, M, half * n), BF16), jax.ShapeDtypeStruct((M, half * n), BF16)],
        compiler_params=_params("parallel", "parallel"),
    )(a, bg, bg)


def mm_nn(a, b, out_dtype, name, tm=512, tn=1024, scale=1.0, res=None, comm=None):
    M, K = a.shape
    _, N = b.shape
    tm, tn = min(tm, M), min(tn, N)

    def body(*refs):
        if res is None:
            a_ref, b_ref, o_ref = refs
            acc = _dot(a_ref[...], b_ref[...])
            o_ref[...] = (acc * scale if scale != 1.0 else acc).astype(o_ref.dtype)
        else:
            a_ref, b_ref, r_ref, o_ref = refs
            o_ref[...] = (r_ref[...] + scale * _dot(a_ref[...], b_ref[...])).astype(o_ref.dtype)

    in_specs = [pl.BlockSpec((tm, K), lambda n, m: (m, 0)),
                pl.BlockSpec((K, tn), lambda n, m: (0, n))]
    args = [a, b]
    if res is not None:
        in_specs.append(pl.BlockSpec((tm, tn), lambda n, m: (m, n)))
        args.append(res)
    return _pcall(
        comm, body, name=name, grid=(N // tn, M // tm),
        in_specs=in_specs,
        out_specs=pl.BlockSpec((tm, tn), lambda n, m: (m, n)),
        out_shape=jax.ShapeDtypeStruct((M, N), out_dtype),
        compiler_params=_params("parallel", "parallel"),
    )(*args)


def mm_nt_g(dy, bg, name, tm=512, comm=None):
    M, _ = dy.shape
    G, K, n = bg.shape
    tm = min(tm, M)

    def body(dy_ref, b_ref, o_ref):
        part = _dot_nt(dy_ref[...], b_ref[...])

        @pl.when(pl.program_id(1) == 0)
        def _():
            o_ref[...] = part

        @pl.when(pl.program_id(1) > 0)
        def _():
            o_ref[...] += part

    return _pcall(
        comm, body, name=name, grid=(M // tm, G),
        in_specs=[pl.BlockSpec((tm, n), lambda m, g: (m, g)),
                  pl.BlockSpec((None, K, n), lambda m, g: (g, 0, 0))],
        out_specs=pl.BlockSpec((tm, K), lambda m, g: (m, 0)),
        out_shape=jax.ShapeDtypeStruct((M, K), F32),
        compiler_params=_params("parallel", "arbitrary"),
    )(dy, bg)


def mm_nt(dy, b, out_dtype, name, tk=512, comm=None):
    M, N = dy.shape
    K, _ = b.shape

    def body(dy_ref, b_ref, o_ref):
        o_ref[...] = _dot_nt(dy_ref[...], b_ref[...]).astype(o_ref.dtype)

    return _pcall(
        comm, body, name=name, grid=(K // tk,),
        in_specs=[pl.BlockSpec((M, N), lambda k: (0, 0)),
                  pl.BlockSpec((tk, N), lambda k: (k, 0))],
        out_specs=pl.BlockSpec((M, tk), lambda k: (0, k)),
        out_shape=jax.ShapeDtypeStruct((M, K), out_dtype),
        compiler_params=_params("parallel"),
    )(dy, b)


def mm_tn_g(x, dy, G, name, tk=512, cols=None, comm=None):
    M, K = x.shape
    k0, K = cols if cols is not None else (0, K)
    n = dy.shape[1] // G

    def body(x_ref, dy_ref, o_ref):
        o_ref[...] = _dot_tn(x_ref[...], dy_ref[...]).astype(o_ref.dtype)

    return _pcall(
        comm, body, name=name, grid=(G, K // tk),
        in_specs=[pl.BlockSpec((M, tk), lambda g, k: (0, k + k0 // tk)),
                  pl.BlockSpec((M, n), lambda g, k: (0, g))],
        out_specs=pl.BlockSpec((None, tk, n), lambda g, k: (g, k, 0)),
        out_shape=jax.ShapeDtypeStruct((G, K, n), BF16),
        compiler_params=_params("parallel", "parallel"),
    )(x, dy)


def mm_tn(x, dy, name, tk=512, comm=None):
    M, K = x.shape
    _, N = dy.shape

    def body(x_ref, dy_ref, o_ref):
        o_ref[...] = _dot_tn(x_ref[...], dy_ref[...]).astype(o_ref.dtype)

    return _pcall(
        comm, body, name=name, grid=(K // tk,),
        in_specs=[pl.BlockSpec((M, tk), lambda k: (0, k)),
                  pl.BlockSpec((M, N), lambda k: (0, 0))],
        out_specs=pl.BlockSpec((tk, N), lambda k: (k, 0)),
        out_shape=jax.ShapeDtypeStruct((K, N), BF16),
        compiler_params=_params("parallel"),
    )(x, dy)


def _rstd(x):
    return lax.rsqrt(jnp.mean(x * x, axis=-1, keepdims=True) + EPS)


def _rms_bwd(dn, xhat, r, g):
    dxhat = dn * g
    return r * (dxhat - xhat * jnp.mean(dxhat * xhat, axis=-1, keepdims=True))


def _row_spec(tr, width, col=0):
    return pl.BlockSpec((tr, width), lambda i: (i, col))


def _vec_spec(width):
    return pl.BlockSpec((1, width), lambda i: (0, 0))


def rms_fwd(x, g, name, comm=None):
    M, D = x.shape
    tr = min(ROW_TILE, M)

    def body(x_ref, g_ref, o_ref):
        xv = x_ref[...]
        o_ref[...] = (xv * _rstd(xv) * g_ref[...]).astype(o_ref.dtype)

    return _pcall(
        comm, body, name=name, grid=(M // tr,),
        in_specs=[_row_spec(tr, D), _vec_spec(D)],
        out_specs=_row_spec(tr, D),
        out_shape=jax.ShapeDtypeStruct((M, D), BF16),
        compiler_params=_params("parallel"),
    )(x, g)


def rms_bwd(dn, h, g, dres, copy_scale, name, comm=None):
    M, D = h.shape
    tr = min(ROW_TILE, M)
    has_res = dres is not None

    def body(*refs):
        if has_res:
            dn_ref, h_ref, g_ref, dres_ref, dh_ref, dhb_ref, dg_ref = refs
        else:
            dn_ref, h_ref, g_ref, dh_ref, dhb_ref, dg_ref = refs
        hv = h_ref[...]
        r = _rstd(hv)
        xhat = hv * r
        dn = dn_ref[...]
        part = jnp.sum(dn * xhat, axis=0, keepdims=True)

        @pl.when(pl.program_id(0) == 0)
        def _():
            dg_ref[...] = part

        @pl.when(pl.program_id(0) > 0)
        def _():
            dg_ref[...] += part

        dh = _rms_bwd(dn, xhat, r, g_ref[...])
        if has_res:
            dh = dh + dres_ref[...]
        dh_ref[...] = dh
        dhb_ref[...] = (dh * copy_scale if copy_scale != 1.0 else dh).astype(BF16)

    in_specs = [_row_spec(tr, D), _row_spec(tr, D), _vec_spec(D)]
    args = [dn, h, g]
    if has_res:
        in_specs.append(_row_spec(tr, D))
        args.append(dres)
    return _pcall(
        comm, body, name=name, grid=(M // tr,),
        in_specs=in_specs,
        out_specs=[_row_spec(tr, D), _row_spec(tr, D), _vec_spec(D)],
        out_shape=[jax.ShapeDtypeStruct((M, D), F32), jax.ShapeDtypeStruct((M, D), BF16),
                   jax.ShapeDtypeStruct((1, D), F32)],
        compiler_params=_params("arbitrary"),
    )(*args)


def _sigmoid(x):
    return 1.0 / (1.0 + jnp.exp(-x))


def swiglu_bwd(a, dh, name, comm=None):
    _, M, F = a.shape
    F2 = 2 * F
    tr = min(ROW_TILE, M)

    def body(g_ref, u_ref, dh_ref, da_ref):
        gt = g_ref[...].astype(F32)
        up = u_ref[...].astype(F32)
        dh = dh_ref[...].astype(F32)
        sg = _sigmoid(gt)
        da_ref[:, :F] = (dh * up * (sg * (1.0 + gt * (1.0 - sg)))).astype(BF16)
        da_ref[:, F:] = (dh * (gt * sg)).astype(BF16)

    return _pcall(
        comm, body, name=name, grid=(M // tr,),
        in_specs=[pl.BlockSpec((None, tr, F), lambda i: (0, i, 0)), pl.BlockSpec((None, tr, F), lambda i: (1, i, 0)),
                  _row_spec(tr, F)],
        out_specs=_row_spec(tr, F2),
        out_shape=jax.ShapeDtypeStruct((M, F2), BF16),
        compiler_params=_params("parallel"),
    )(a, a, dh)


def rmscat_fwd(ya, yb, ga, gb, name, comm=None):
    M, W = ya.shape
    tr = min(ROW_TILE, M)

    def body(ya_ref, yb_ref, ga_ref, gb_ref, o_ref):
        a = ya_ref[...]
        b = yb_ref[...]
        o_ref[:, :W] = (a * _rstd(a) * ga_ref[...]).astype(BF16)
        o_ref[:, W:] = (b * _rstd(b) * gb_ref[...]).astype(BF16)

    return _pcall(
        comm, body, name=name, grid=(M // tr,),
        in_specs=[_row_spec(tr, W), _row_spec(tr, W), _vec_spec(W), _vec_spec(W)],
        out_specs=_row_spec(tr, 2 * W),
        out_shape=jax.ShapeDtypeStruct((M, 2 * W), BF16),
        compiler_params=_params("parallel"),
    )(ya, yb, ga, gb)


def rmscat_bwd(dycat, ya, yb, ga, gb, name, comm=None):
    M, W = ya.shape
    tr = min(ROW_TILE, M)

    def body(dc_ref, ya_ref, yb_ref, ga_ref, gb_ref, dya_ref, dyb_ref, dga_ref, dgb_ref):
        first = pl.program_id(0) == 0
        for y_ref, g_ref, dy_ref, dg_ref, lo in ((ya_ref, ga_ref, dya_ref, dga_ref, 0),
                                                 (yb_ref, gb_ref, dyb_ref, dgb_ref, W)):
            yv = y_ref[...]
            r = _rstd(yv)
            xhat = yv * r
            dn = dc_ref[:, lo:lo + W]
            part = jnp.sum(dn * xhat, axis=0, keepdims=True)

            @pl.when(first)
            def _():
                dg_ref[...] = part

            @pl.when(jnp.logical_not(first))
            def _():
                dg_ref[...] += part

            dy_ref[...] = _rms_bwd(dn, xhat, r, g_ref[...])

    return _pcall(
        comm, body, name=name, grid=(M // tr,),
        in_specs=[_row_spec(tr, 2 * W), _row_spec(tr, W), _row_spec(tr, W), _vec_spec(W), _vec_spec(W)],
        out_specs=[_row_spec(tr, W), _row_spec(tr, W), _vec_spec(W), _vec_spec(W)],
        out_shape=[jax.ShapeDtypeStruct((M, W), F32), jax.ShapeDtypeStruct((M, W), F32),
                   jax.ShapeDtypeStruct((1, W), F32), jax.ShapeDtypeStruct((1, W), F32)],
        compiler_params=_params("arbitrary"),
    )(dycat, ya, yb, ga, gb)


def loss_head(h, target, g, name, comm=None):
    M, D = h.shape
    tr = min(ROW_TILE, M)

    def body(h_ref, t_ref, g_ref, loss_ref, dh_ref, dhb_ref, dg_ref):
        hv = h_ref[...]
        gv = g_ref[...]
        r = _rstd(hv)
        xhat = hv * r
        err = xhat * gv - t_ref[...]
        lsum = jnp.sum(jnp.sum(err * err, axis=1, keepdims=True), axis=0, keepdims=True) * (0.5 / D)
        dy = err * (1.0 / D)
        part = jnp.sum(dy * xhat, axis=0, keepdims=True)

        @pl.when(pl.program_id(0) == 0)
        def _():
            dg_ref[...] = part
            loss_ref[...] = _zeros(loss_ref) + lsum

        @pl.when(pl.program_id(0) > 0)
        def _():
            dg_ref[...] += part
            loss_ref[...] += lsum

        dh = _rms_bwd(dy, xhat, r, gv)
        dh_ref[...] = dh
        dhb_ref[...] = (0.5 * dh).astype(BF16)

    return _pcall(
        comm, body, name=name, grid=(M // tr,),
        in_specs=[_row_spec(tr, D), _row_spec(tr, D), _vec_spec(D)],
        out_specs=[pl.BlockSpec((8, 128), lambda i: (0, 0)), _row_spec(tr, D), _row_spec(tr, D), _vec_spec(D)],
        out_shape=[jax.ShapeDtypeStruct((8, 128), F32), jax.ShapeDtypeStruct((M, D), F32),
                   jax.ShapeDtypeStruct((M, D), BF16), jax.ShapeDtypeStruct((1, D), F32)],
        compiler_params=_params("arbitrary"),
    )(h, target, g)


_GELU_C = math.sqrt(2.0 / math.pi)


def _gelu(x):
    return 0.5 * x * (1.0 + jnp.tanh(_GELU_C * (x + 0.044715 * (x * x * x))))


def _gelu_grad(x):
    t = jnp.tanh(_GELU_C * (x + 0.044715 * (x * x * x)))
    return 0.5 * (1.0 + t) + 0.5 * x * (1.0 - t * t) * (_GELU_C * (1.0 + 3.0 * 0.044715 * (x * x)))


def _sgu_mask():
    t = lax.broadcasted_iota(jnp.int32, (SGU_BLOCK, SGU_BLOCK), 0) // CHUNK
    s = lax.broadcasted_iota(jnp.int32, (SGU_BLOCK, SGU_BLOCK), 1) // CHUNK
    return s <= t


def _layernorm_stats(v):
    mu = jnp.mean(v, axis=-1, keepdims=True)
    cen = v - mu
    rstd = lax.rsqrt(jnp.mean(cen * cen, axis=-1, keepdims=True) + EPS)
    return cen * rstd, rstd


def sgu_fwd(z, ln_g, ln_b, w_s, b_t, name, comm=None):
    S = z.shape[0]

    def body(zu_ref, zv_ref, lg_ref, lb_ref, w_ref, bt_ref, o_ref):
        mask = _sgu_mask()
        for g in range(G_A):
            cols = slice(g * GA_DIM, (g + 1) * GA_DIM)
            u = _gelu(zu_ref[:, cols])
            vhat, _ = _layernorm_stats(_gelu(zv_ref[:, cols]))
            vln = vhat * lg_ref[:, cols] + lb_ref[:, cols]
            w = jnp.where(mask, w_ref[g], 0.0).astype(BF16)
            mixed = _dot(w, vln.astype(BF16)) + bt_ref[:, g:g + 1]
            o_ref[:, cols] = u * mixed

    return _pcall(
        comm, body, name=name, grid=(S // SGU_BLOCK,),
        in_specs=[_row_spec(SGU_BLOCK, W_A, 0), _row_spec(SGU_BLOCK, W_A, 1), _vec_spec(W_A), _vec_spec(W_A),
                  pl.BlockSpec((G_A, SGU_BLOCK, SGU_BLOCK), lambda i: (0, 0, 0)),
                  pl.BlockSpec((SGU_BLOCK, G_A), lambda i: (0, 0))],
        out_specs=_row_spec(SGU_BLOCK, W_A),
        out_shape=jax.ShapeDtypeStruct((S, W_A), F32),
        compiler_params=_params("parallel"),
    )(z, z, ln_g, ln_b, w_s, b_t)


def sgu_bwd(z, dya, ln_g, ln_b, w_s, b_t, name, comm=None):
    S = z.shape[0]
    nblk = S // SGU_BLOCK

    def body(zu_ref, zv_ref, dy_ref, lg_ref, lb_ref, w_ref, bt_ref,
             dz_ref, dlg_ref, dlb_ref, dw_ref, db_ref, dmix_acc):
        step = pl.program_id(0)
        mask = _sgu_mask()

        @pl.when(step == 0)
        def _():
            dlg_ref[...] = _zeros(dlg_ref)
            dlb_ref[...] = _zeros(dlb_ref)
            dw_ref[...] = _zeros(dw_ref)
            dmix_acc[...] = _zeros(dmix_acc)

        for g in range(G_A):
            cols = slice(g * GA_DIM, (g + 1) * GA_DIM)
            zu = zu_ref[:, cols]
            zv = zv_ref[:, cols]
            u = _gelu(zu)
            vhat, rstd = _layernorm_stats(_gelu(zv))
            lg = lg_ref[:, cols]
            vln = (vhat * lg + lb_ref[:, cols]).astype(BF16)
            w = jnp.where(mask, w_ref[g], 0.0)
            mixed = _dot(w.astype(BF16), vln) + bt_ref[:, g:g + 1]
            dy = dy_ref[:, cols]
            du = dy * mixed
            dmixed = dy * u
            dmixed_b = dmixed.astype(BF16)
            dmix_acc[g] += dmixed
            dw_ref[g] += jnp.where(mask, _dot_nt(dmixed_b, vln), 0.0)
            dvln = _dot(w.T.astype(BF16), dmixed_b)
            dlb_ref[:, cols] += jnp.sum(dvln, axis=0, keepdims=True)
            dlg_ref[:, cols] += jnp.sum(dvln * vhat, axis=0, keepdims=True)
            dvhat = dvln * lg
            dv = rstd * (dvhat - jnp.mean(dvhat, axis=-1, keepdims=True)
                         - vhat * jnp.mean(dvhat * vhat, axis=-1, keepdims=True))
            dz_ref[:, cols] = (du * _gelu_grad(zu)).astype(BF16)
            dz_ref[:, W_A + g * GA_DIM:W_A + (g + 1) * GA_DIM] = (dv * _gelu_grad(zv)).astype(BF16)

        @pl.when(step == nblk - 1)
        def _():
            for g in range(G_A):
                db_ref[g] = jnp.sum(dmix_acc[g], axis=1, keepdims=True)

    whole3 = lambda shape: pl.BlockSpec(shape, lambda i: (0, 0, 0))
    return _pcall(
        comm, body, name=name, grid=(nblk,),
        in_specs=[_row_spec(SGU_BLOCK, W_A, 0), _row_spec(SGU_BLOCK, W_A, 1), _row_spec(SGU_BLOCK, W_A),
                  _vec_spec(W_A), _vec_spec(W_A), whole3((G_A, SGU_BLOCK, SGU_BLOCK)),
                  pl.BlockSpec((SGU_BLOCK, G_A), lambda i: (0, 0))],
        out_specs=[_row_spec(SGU_BLOCK, 2 * W_A), _vec_spec(W_A), _vec_spec(W_A),
                   whole3((G_A, SGU_BLOCK, SGU_BLOCK)), whole3((G_A, SGU_BLOCK, 1))],
        out_shape=[jax.ShapeDtypeStruct((S, 2 * W_A), BF16), jax.ShapeDtypeStruct((1, W_A), F32),
                   jax.ShapeDtypeStruct((1, W_A), F32), jax.ShapeDtypeStruct((G_A, SGU_BLOCK, SGU_BLOCK), F32),
                   jax.ShapeDtypeStruct((G_A, SGU_BLOCK, 1), F32)],
        scratch_shapes=[pltpu.VMEM((G_A, SGU_BLOCK, SGU_BLOCK), F32)],
        compiler_params=_params("arbitrary"),
    )(z, z, dya, ln_g, ln_b, w_s, b_t)


def _log_sigmoid(z):
    return jnp.minimum(z, 0.0) - jnp.log(1.0 + jnp.exp(-jnp.abs(z)))


def _suffix_sum(x, upper):
    hi = x.astype(BF16)
    rem = x - hi.astype(F32)
    mid = rem.astype(BF16)
    lo = (rem - mid.astype(F32)).astype(BF16)
    return _dot(hi, upper) + _dot(mid, upper) + _dot(lo, upper)


SB_ROWS = 1024
_SB_SUB = SB_ROWS // Q_BLOCK


def _sb_upper():
    row = lax.broadcasted_iota(jnp.int32, (Q_BLOCK, Q_BLOCK), 0)
    col = lax.broadcasted_iota(jnp.int32, (Q_BLOCK, Q_BLOCK), 1)
    return (row > col).astype(BF16)


def _sb_sweep(step, tile):
    for r in reversed(range(_SB_SUB)):
        tile(step * _SB_SUB + r, r * Q_BLOCK)

    def group(g, _):
        base = (step - 1 - g) * _SB_SUB
        for r in reversed(range(_SB_SUB)):
            tile(base + r, None)
        return 0

    lax.fori_loop(0, step, group, 0)


def _sb_causal(n):
    return lax.broadcasted_iota(jnp.int32, (n, Q_BLOCK), 1) < lax.broadcasted_iota(jnp.int32, (n, Q_BLOCK), 0)


def _sb_col(part):
    return (2 * W_A + part * (H_B * DH_B)) // DH_B


def _sb_q_spec():
    return pl.BlockSpec((SB_ROWS, DH_B), lambda h, i: (i, _sb_col(0) + h))


def _sb_kv_spec(S, part):
    return pl.BlockSpec((S, DH_B), lambda h, i: (0, _sb_col(part) + h))


def sb_fwd(z, name, comm=None):
    S = z.shape[0]
    scale = DH_B ** -0.5

    def body(q_ref, k_ref, v_ref, o_ref, q_b, c_l1m):
        step = pl.program_id(1)
        q_b[...] = (q_ref[...] * scale).astype(BF16)
        o_ref[...] = _zeros(o_ref)
        c_l1m[...] = _zeros(c_l1m)
        upper = _sb_upper()

        def tile(j, row0):
            rq = slice(row0 or 0, SB_ROWS)
            causal = None if row0 is None else _sb_causal(SB_ROWS - row0)
            rows = pl.ds(pl.multiple_of(j * Q_BLOCK, Q_BLOCK), Q_BLOCK)
            zz = _dot_nt(q_b[rq, :], k_ref[rows, :].astype(BF16))
            lb = _log_sigmoid(zz)
            l1m = lb - zz
            if causal is not None:
                l1m = jnp.where(causal, l1m, 0.0)
            a = jnp.exp(lb + _suffix_sum(l1m, upper) + c_l1m[rq, :])
            if causal is not None:
                a = jnp.where(causal, a, 0.0)
            o_ref[rq, :] += _dot(a.astype(BF16), v_ref[rows, :].astype(BF16))
            c_l1m[rq, :] += jnp.sum(l1m, axis=1, keepdims=True)

        _sb_sweep(step, tile)

    return _pcall(
        comm, body, name=name, grid=(H_B, S // SB_ROWS),
        in_specs=[_sb_q_spec(), _sb_kv_spec(S, 1), _sb_kv_spec(S, 2)],
        out_specs=pl.BlockSpec((SB_ROWS, DH_B), lambda h, i: (i, h)),
        out_shape=jax.ShapeDtypeStruct((S, H_B * DH_B), F32),
        scratch_shapes=[pltpu.VMEM((SB_ROWS, DH_B), BF16), pltpu.VMEM((SB_ROWS, 1), F32)],
        compiler_params=_params("parallel", "parallel"),
    )(z, z, z)


def sb_bwd(z, out, dout, name, comm=None):
    S = z.shape[0]
    nstep = S // SB_ROWS
    scale = DH_B ** -0.5

    def body(q_ref, k_ref, v_ref, o_ref, do_ref, dq_ref, dk_ref, dv_ref,
             dq_acc, dkt_acc, dvt_acc, q_b, do_b, qt_b, dot_b, g_left, c_l1m):
        step = pl.program_id(1)

        @pl.when(step == 0)
        def _():
            dkt_acc[...] = _zeros(dkt_acc)
            dvt_acc[...] = _zeros(dvt_acc)

        q_b[...] = (q_ref[...] * scale).astype(BF16)
        do_b[...] = do_ref[...].astype(BF16)
        qt_b[...] = (q_ref[...].T * scale).astype(BF16)
        dot_b[...] = do_ref[...].T.astype(BF16)
        g_left[...] = jnp.sum(do_b[...].astype(F32) * o_ref[...], axis=1, keepdims=True)
        dq_acc[...] = _zeros(dq_acc)
        c_l1m[...] = _zeros(c_l1m)
        upper = _sb_upper()

        def tile(j, row0):
            rq = slice(row0 or 0, SB_ROWS)
            causal = None if row0 is None else _sb_causal(SB_ROWS - row0)
            rows = pl.ds(pl.multiple_of(j * Q_BLOCK, Q_BLOCK), Q_BLOCK)
            q, do_t = q_b[rq, :], do_b[rq, :]
            k_j = k_ref[rows, :].astype(BF16)
            zz = _dot_nt(q, k_j)
            lb = _log_sigmoid(zz)
            l1m = lb - zz
            if causal is not None:
                l1m = jnp.where(causal, l1m, 0.0)
            a = jnp.exp(lb + _suffix_sum(l1m, upper) + c_l1m[rq, :])
            if causal is not None:
                a = jnp.where(causal, a, 0.0)
            a_b = a.astype(BF16)
            dvt_acc[:, rows] += _dot(dot_b[:, rq], a_b)
            gmat = a_b.astype(F32) * _dot_nt(do_t, v_ref[rows, :].astype(BF16))
            before = g_left[rq, :] - gmat - _suffix_sum(gmat, upper)
            sig = jnp.exp(lb)
            dz = gmat * (1.0 - sig) - sig * before
            if causal is not None:
                dz = jnp.where(causal, dz, 0.0)
            dz_b = dz.astype(BF16)
            dkt_acc[:, rows] += _dot(qt_b[:, rq], dz_b)
            dq_acc[rq, :] += _dot(dz_b, k_j)
            c_l1m[rq, :] += jnp.sum(l1m, axis=1, keepdims=True)
            g_left[rq, :] -= jnp.sum(gmat, axis=1, keepdims=True)

        _sb_sweep(step, tile)
        dq_ref[...] = (dq_acc[...] * scale).astype(BF16)

        @pl.when(step == nstep - 1)
        def _():
            dk_ref[...] = dkt_acc[...].T.astype(BF16)
            dv_ref[...] = dvt_acc[...].T.astype(BF16)

    rows_spec = pl.BlockSpec((SB_ROWS, DH_B), lambda h, i: (i, h))
    head_spec = pl.BlockSpec((S, DH_B), lambda h, i: (0, h))
    out_sds = jax.ShapeDtypeStruct((S, H_B * DH_B), BF16)
    return _pcall(
        comm, body, name=name, grid=(H_B, nstep),
        in_specs=[_sb_q_spec(), _sb_kv_spec(S, 1), _sb_kv_spec(S, 2), rows_spec, rows_spec],
        out_specs=[rows_spec, head_spec, head_spec],
        out_shape=[out_sds, out_sds, out_sds],
        scratch_shapes=[pltpu.VMEM((SB_ROWS, DH_B), F32)] + [pltpu.VMEM((DH_B, S), F32)] * 2
        + [pltpu.VMEM((SB_ROWS, DH_B), BF16)] * 2 + [pltpu.VMEM((DH_B, SB_ROWS), BF16)] * 2
        + [pltpu.VMEM((SB_ROWS, 1), F32)] * 2,
        compiler_params=_params("parallel", "arbitrary"),
    )(z, z, z, out, dout)


def _softmax(s):
    e = jnp.exp(s - jnp.max(s, axis=-1, keepdims=True))
    return e / jnp.sum(e, axis=-1, keepdims=True)


def xattn_fwd(qc, kv, name, comm=None):
    S, D = qc.shape
    tr = min(ROW_TILE, S)

    def body(q_ref, kv_ref, o_ref):
        for h in range(X_HEADS):
            cols = slice(h * X_DH, (h + 1) * X_DH)
            p = _softmax(_dot_nt(q_ref[:, cols], kv_ref[:, cols]))
            o_ref[:, cols] = _dot(p.astype(BF16), kv_ref[:, D + h * X_DH:D + (h + 1) * X_DH]).astype(BF16)

    return _pcall(
        comm, body, name=name, grid=(S // tr,),
        in_specs=[_row_spec(tr, D), pl.BlockSpec((N_MEM, 2 * D), lambda i: (0, 0))],
        out_specs=_row_spec(tr, D),
        out_shape=jax.ShapeDtypeStruct((S, D), BF16),
        compiler_params=_params("parallel"),
    )(qc, kv)


def xattn_bwd(qc, kv, do, name, comm=None):
    S, D = qc.shape
    tr = min(ROW_TILE, S)
    nstep = S // tr
    scale = X_DH ** -0.5

    def body(q_ref, kv_ref, do_ref, dq_ref, dkv_ref, acc):
        step = pl.program_id(0)

        @pl.when(step == 0)
        def _():
            acc[...] = _zeros(acc)

        for h in range(X_HEADS):
            cols = slice(h * X_DH, (h + 1) * X_DH)
            vcols = slice(D + h * X_DH, D + (h + 1) * X_DH)
            q = q_ref[:, cols]
            k = kv_ref[:, cols]
            do_h = do_ref[:, cols]
            p = _softmax(_dot_nt(q, k))
            dp = _dot_nt(do_h, kv_ref[:, vcols])
            acc[:, vcols] += _dot_tn(p.astype(BF16), do_h)
            ds = (p * (dp - jnp.sum(p * dp, axis=-1, keepdims=True))).astype(BF16)
            dq_ref[:, cols] = (_dot(ds, k) * scale).astype(BF16)
            acc[:, cols] += _dot_tn(ds, q)

        @pl.when(step == nstep - 1)
        def _():
            dkv_ref[...] = acc[...].astype(BF16)

    whole = pl.BlockSpec((N_MEM, 2 * D), lambda i: (0, 0))
    return _pcall(
        comm, body, name=name, grid=(nstep,),
        in_specs=[_row_spec(tr, D), whole, _row_spec(tr, D)],
        out_specs=[_row_spec(tr, D), whole],
        out_shape=[jax.ShapeDtypeStruct((S, D), BF16), jax.ShapeDtypeStruct((N_MEM, 2 * D), BF16)],
        scratch_shapes=[pltpu.VMEM((N_MEM, 2 * D), F32)],
        compiler_params=_params("arbitrary"),
    )(qc, kv, do)


def _row_tile(rows, cap=128):
    return max(t for t in range(16, cap + 1, 16) if rows % t == 0)


def cast_bf16(w, name, comm=None):
    R, C = w.shape
    tr = _row_tile(R, 256)

    def body(w_ref, o_ref):
        o_ref[...] = w_ref[...].astype(BF16)

    return _pcall(
        comm, body, name=name, grid=(R // tr,),
        in_specs=[_row_spec(tr, C)], out_specs=_row_spec(tr, C),
        out_shape=jax.ShapeDtypeStruct((R, C), BF16),
        compiler_params=_params("parallel"),
    )(w)


def adamw(parts, w, m, v, name, comm=None):
    R, C = w.shape
    n_parts = parts.shape[0]
    tr = _row_tile(R, 256)
    c1 = 1.0 - ADAM_B1 ** ADAM_STEP
    c2 = 1.0 - ADAM_B2 ** ADAM_STEP

    def body(p_ref, w_ref, m_ref, v_ref, g_ref, d_ref, mo_ref, vo_ref):
        g = p_ref[0].astype(F32)
        for p in range(1, n_parts):
            g = g + p_ref[p].astype(F32)
        m_new = ADAM_B1 * m_ref[...] + (1.0 - ADAM_B1) * g
        v_new = ADAM_B2 * v_ref[...] + (1.0 - ADAM_B2) * (g * g)
        g_ref[...] = g
        mo_ref[...] = m_new
        vo_ref[...] = v_new
        d_ref[...] = -ADAM_LR * ((m_new / c1) / (jnp.sqrt(v_new / c2) + ADAM_EPS) + ADAM_WD * w_ref[...])

    spec = _row_spec(tr, C)
    sds = jax.ShapeDtypeStruct((R, C), F32)
    return _pcall(
        comm, body, name=name, grid=(R // tr,),
        in_specs=[pl.BlockSpec((n_parts, tr, C), lambda i: (0, i, 0)), spec, spec, spec],
        out_specs=[spec, spec, spec, spec],
        out_shape=[sds, sds, sds, sds],
        compiler_params=_params("parallel"),
    )(parts, w, m, v)


def pair_sum(parts, from_sibling, core, name):
    _, R, C = parts.shape
    tr = _row_tile(R, 1024)

    def body(core_ref, p_ref, s_ref, o_ref):
        o_ref[...] = (p_ref[...].astype(F32) + s_ref[...].astype(F32)).astype(o_ref.dtype)

    return pl.pallas_call(
        body, name=name,
        grid_spec=pltpu.PrefetchScalarGridSpec(
            num_scalar_prefetch=1, grid=(4, R // tr),
            in_specs=[pl.BlockSpec((None, tr, C), lambda q, i, core_ref: (2 * q + core_ref[0], i, 0)),
                      pl.BlockSpec((None, tr, C), lambda q, i, core_ref: (q, i, 0))],
            out_specs=pl.BlockSpec((None, tr, C), lambda q, i, core_ref: (q, i, 0))),
        out_shape=jax.ShapeDtypeStruct((4, R, C), BF16),
        compiler_params=_params("parallel", "parallel"),
    )(core, parts, from_sibling)


def add2(a, b, name, comm=None):
    R, C = a.shape
    tr = _row_tile(R, 256)

    def body(a_ref, b_ref, o_ref):
        o_ref[...] = a_ref[...] + b_ref[...]

    spec = _row_spec(tr, C)
    return _pcall(
        comm, body, name=name, grid=(R // tr,), in_specs=[spec, spec], out_specs=spec,
        out_shape=jax.ShapeDtypeStruct((R, C), F32), compiler_params=_params("parallel"),
    )(a, b)


def _place():
    return lax.axis_index("x"), lax.axis_index("y"), lax.axis_index("c")


class Comm:
    def __init__(self, arrays, out_shapes, n_remote, n_local, start, finish, aliases=None):
        self.arrays, self.out_shapes = list(arrays), list(out_shapes)
        self.n_remote, self.n_local = n_remote, max(n_local, 1)
        self.start, self.finish = start, finish
        self.aliases = dict(aliases or {})
        self.sizes = [len(self.out_shapes)]

    def sem_shapes(self):
        return [pltpu.SemaphoreType.DMA((self.n_remote,)), pltpu.SemaphoreType.DMA((self.n_remote,)),
                pltpu.SemaphoreType.DMA((self.n_local,))]


class _Shifted:
    def __init__(self, ref, offset):
        self.ref, self.offset = ref, offset

    @property
    def at(self):
        return self

    def __getitem__(self, k):
        return self.ref.at[self.offset + k]


def merge_comms(comms):
    comms = [c for c in comms if c is not None]
    if not comms:
        return None

    def each(method):
        def run(ins, outs, sems):
            i = o = r = l = 0
            for c in comms:
                sub = (_Shifted(sems[0], r), _Shifted(sems[1], r), _Shifted(sems[2], l))
                getattr(c, method)(ins[i:i + len(c.arrays)], outs[o:o + len(c.out_shapes)], sub)
                i, o, r, l = i + len(c.arrays), o + len(c.out_shapes), r + c.n_remote, l + c.n_local
        return run

    aliases, i, o = {}, 0, 0
    for c in comms:
        aliases.update({i + a: o + b for a, b in c.aliases.items()})
        i, o = i + len(c.arrays), o + len(c.out_shapes)
    merged = Comm([a for c in comms for a in c.arrays], [s for c in comms for s in c.out_shapes],
                  sum(c.n_remote for c in comms), sum(c.n_local for c in comms), each("start"), each("finish"), aliases)
    merged.sizes = [len(c.out_shapes) for c in comms]
    return merged


def split_results(comm, results):
    out, i = [], 0
    for n in comm.sizes:
        out.append(list(results[i:i + n]))
        i += n
    return out


def run_comm(comm, name):
    n_in, n_out = len(comm.arrays), len(comm.out_shapes)

    def body(*refs):
        ins, outs, sems = refs[:n_in], refs[n_in:n_in + n_out], refs[n_in + n_out:]
        comm.start(ins, outs, sems)
        comm.finish(ins, outs, sems)

    return pl.pallas_call(
        body, name=name, in_specs=[ANY] * n_in, out_specs=[ANY] * n_out, out_shape=comm.out_shapes,
        scratch_shapes=comm.sem_shapes(), input_output_aliases=comm.aliases,
    )(*comm.arrays)


def _remote(src, dst, sems, k, to):
    return pltpu.make_async_remote_copy(src_ref=src, dst_ref=dst, send_sem=sems[0].at[k], recv_sem=sems[1].at[k],
                                        device_id=to, device_id_type=MESH)


_AG_COPIES = 13


def comm_all_gather(shards, rows=None, into=None):
    n = len(shards)
    row0, nrows = rows if rows is not None else (0, None)

    def parties():
        x, y, c = _place()
        return (x, y, c), (x, y, 1 - c), [(1 - x, y), (x, 1 - y), (1 - x, 1 - y)]

    def span(w, half=None):
        count = nrows if nrows is not None else shards[w].shape[0]
        if half is None:
            return pl.ds(row0, count)
        return pl.ds(row0 + half * (count // 2), count // 2)

    def slab(outs, w, dev, half=None):
        return outs[w].at[4 * dev[0] + 2 * dev[1] + dev[2], span(w, half)]

    def own(ins, outs, sems):
        me, sibling, chips = parties()
        local = [pltpu.make_async_copy(ins[w].at[span(w)], slab(outs, w, me), sems[2].at[w]) for w in range(n)]
        first = []
        for w in range(n):
            k = _AG_COPIES * w
            first.append(_remote(ins[w].at[span(w)], slab(outs, w, me), sems, k, sibling))
            first += [_remote(ins[w].at[span(w, h)], slab(outs, w, me, h), sems, k + 1 + 2 * j + h, (*chip, me[2]))
                      for h in range(2) for j, chip in enumerate(chips)]
        return local, first

    def start(ins, outs, sems):
        local, first = own(ins, outs, sems)
        for cp in local + first:
            cp.start()

    def finish(ins, outs, sems):
        me, sibling, chips = parties()
        local, first = own(ins, outs, sems)
        passed = []
        for w in range(n):
            k = _AG_COPIES * w
            for h in range(2):
                for j, chip in enumerate(chips):
                    got = slab(outs, w, (*chip, me[2]), h)
                    _remote(got, got, sems, k + 1 + 2 * j + h, me).wait_recv()
                    cp = _remote(got, got, sems, k + 7 + 2 * j + h, sibling)
                    cp.start()
                    passed.append(cp)
        for w in range(n):
            k = _AG_COPIES * w
            got = slab(outs, w, sibling)
            _remote(got, got, sems, k, me).wait_recv()
            for h in range(2):
                for j, chip in enumerate(chips):
                    got = slab(outs, w, (*chip, sibling[2]), h)
                    _remote(got, got, sems, k + 7 + 2 * j + h, me).wait_recv()
        for cp in first + passed:
            cp.wait_send()
        for cp in local:
            cp.wait()

    out_shapes = [jax.ShapeDtypeStruct((N_DEV,) + s.shape, s.dtype) for s in shards]
    arrays = list(shards) + (list(into) if into is not None else [])
    aliases = {n + w: w for w in range(n)} if into is not None else None
    return Comm(arrays, out_shapes, _AG_COPIES * n, n, start, finish, aliases)


def comm_pairs(items):
    slabbed = [a.ndim == 3 for a in items]
    first = [sum(4 if s else 1 for s in slabbed[:w]) for w in range(len(items))]

    def copies(ins, outs, sems):
        x, y, c = _place()
        sibling = (x, y, 1 - c)
        cps = []
        for w, s in enumerate(slabbed):
            if s:
                cps += [_remote(ins[w].at[2 * q + (1 - c)], outs[w].at[q], sems, first[w] + q, sibling) for q in range(4)]
            else:
                cps.append(_remote(ins[w], outs[w], sems, first[w], sibling))
        return cps

    def start(ins, outs, sems):
        for cp in copies(ins, outs, sems):
            cp.start()

    def finish(ins, outs, sems):
        for cp in copies(ins, outs, sems):
            cp.wait()

    out_shapes = [jax.ShapeDtypeStruct(((4,) + a.shape[1:]) if s else a.shape, a.dtype) for a, s in zip(items, slabbed)]
    return Comm(items, out_shapes, sum(4 if s else 1 for s in slabbed), 0, start, finish)


def comm_chips(items, rows=None, into=None, from_row=None, out_rows=None):
    n = len(items)
    slabbed = [a.ndim == 3 for a in items]

    def span(w, source=False):
        if rows is None:
            return pl.ds(0, items[w].shape[-2])
        return pl.ds(from_row if source and from_row is not None else rows[0], rows[1])

    def copies(ins, outs, sems):
        x, y, c = _place()
        mine = 2 * x + y
        local = [pltpu.make_async_copy(ins[w].at[mine, span(w, True)] if slabbed[w] else ins[w].at[span(w, True)],
                                       outs[w].at[mine, span(w)], sems[2].at[w]) for w in range(n)]
        remote = []
        for w in range(n):
            for j, (px, py) in enumerate([(1 - x, y), (x, 1 - y), (1 - x, 1 - y)]):
                src = ins[w].at[2 * px + py, span(w, True)] if slabbed[w] else ins[w].at[span(w, True)]
                remote.append(_remote(src, outs[w].at[mine, span(w)], sems, 3 * w + j, (px, py, c)))
        return local, remote

    def start(ins, outs, sems):
        local, remote = copies(ins, outs, sems)
        for cp in local + remote:
            cp.start()

    def finish(ins, outs, sems):
        local, remote = copies(ins, outs, sems)
        for cp in remote + local:
            cp.wait()

    def result(a):
        tall = a.shape[:-2] + (out_rows if out_rows is not None else a.shape[-2], a.shape[-1])
        return jax.ShapeDtypeStruct(tall if a.ndim == 3 else (4,) + tall, a.dtype)

    if into is None:
        return Comm(items, [result(a) for a in items], 3 * n, n, start, finish)
    out_shapes = [jax.ShapeDtypeStruct(b.shape, b.dtype) for b in into]
    return Comm(list(items) + list(into), out_shapes, 3 * n, n, start, finish, {n + w: w for w in range(n)})


_SMALL = ("ffn1_norm", "mix_norm", "ln_v_gain", "ln_v_bias", "spatial_w", "spatial_b", "gnorm_a", "gnorm_b",
          "cross_norm", "mem_norm", "ffn2_norm", "final_norm")
_BIG = ("ffn1_w_in", "ffn1_w_out", "w_mix_in", "w_mix_out", "w_cq", "w_ckv", "w_co", "ffn2_w_in", "ffn2_w_out")
_COL_SHARDED = ("ffn1_w_in", "w_mix_in", "w_ckv", "ffn2_w_in")
_ORDER = ("ffn1_norm", "ffn1_w_in", "ffn1_w_out", "mix_norm", "w_mix_in", "ln_v_gain", "ln_v_bias", "spatial_w",
          "spatial_b", "gnorm_a", "gnorm_b", "w_mix_out", "cross_norm", "mem_norm", "w_cq", "w_ckv", "w_co",
          "ffn2_norm", "ffn2_w_in", "ffn2_w_out", "final_norm")


_SMALL_PAD = 120


def _rows128(a):
    return a.reshape(-1, 128)


def kernel(x, mem, ffn1_norm, ffn1_w_in, ffn1_w_out, mix_norm, w_mix_in, ln_v_gain, ln_v_bias, spatial_w, spatial_b, gnorm_a, gnorm_b, w_mix_out, cross_norm, mem_norm, w_cq, w_ckv, w_co, ffn2_norm, ffn2_w_in, ffn2_w_out, final_norm, loss_target, m_ffn1_norm, m_ffn1_w_in, m_ffn1_w_out, m_mix_norm, m_w_mix_in, m_ln_v_gain, m_ln_v_bias, m_spatial_w, m_spatial_b, m_gnorm_a, m_gnorm_b, m_w_mix_out, m_cross_norm, m_mem_norm, m_w_cq, m_w_ckv, m_w_co, m_ffn2_norm, m_ffn2_w_in, m_ffn2_w_out, m_final_norm, v_ffn1_norm, v_ffn1_w_in, v_ffn1_w_out, v_mix_norm, v_w_mix_in, v_ln_v_gain, v_ln_v_bias, v_spatial_w, v_spatial_b, v_gnorm_a, v_gnorm_b, v_w_mix_out, v_cross_norm, v_mem_norm, v_w_cq, v_w_ckv, v_w_co, v_ffn2_norm, v_ffn2_w_in, v_ffn2_w_out, v_final_norm):
    given = dict(locals())
    wts = {k: given[k] for k in _ORDER}
    mom = {k: given["m_" + k] for k in _ORDER}
    var = {k: given["v_" + k] for k in _ORDER}

    D = D_MODEL
    xs = x.reshape(-1, D)
    mems = mem.reshape(-1, D)
    tgt = loss_target.reshape(-1, D)
    vec = lambda a: a.reshape(1, -1)
    g1, gmix, gcross, gmem, g2, gfin = (vec(wts[k]) for k in
                                        ("ffn1_norm", "mix_norm", "cross_norm", "mem_norm", "ffn2_norm", "final_norm"))
    ln_g, ln_b, ga, gb = (vec(wts[k]) for k in ("ln_v_gain", "ln_v_bias", "gnorm_a", "gnorm_b"))
    w_s = spatial_w.reshape(G_A, SGU_BLOCK, SGU_BLOCK)
    b_t = spatial_b.reshape(G_A, SGU_BLOCK).T

    shard2d = {k: wts[k].reshape(wts[k].shape[1:]) for k in _BIG}
    shard_b = {k: cast_bf16(shard2d[k], f"cast_{k}") for k in _BIG}
    full = {}

    landing, rows_done = {}, {}

    def gathering(pieces, fn, *args, **kw):
        pieces = [p if isinstance(p, tuple) else (p, None) for p in pieces]
        comm = merge_comms([comm_all_gather([shard_b[k]], rows, [landing[k]] if k in landing else None)
                            for k, rows in pieces])
        out, got = fn(*args, comm=comm, **kw)
        for (k, rows), (g,) in zip(pieces, split_results(comm, got)):
            landing[k] = g
            rows_done[k] = rows_done.get(k, 0) + (rows[1] if rows is not None else shard_b[k].shape[0])
            if rows_done[k] == shard_b[k].shape[0]:
                full[k] = g if k in _COL_SHARDED else g.reshape(-1, g.shape[2])
        return out

    n1 = gathering(("ffn1_w_in",), rms_fwd, xs, g1, "f_n1")
    a1, hsw1 = gathering(("ffn1_w_out",), mm_swiglu_g, n1, full["ffn1_w_in"], "f_a1")
    h1 = gathering(("w_mix_in",), mm_nn, hsw1, full["ffn1_w_out"], F32, "f_h1", scale=0.5, res=xs)
    n2 = gathering(("w_cq",), rms_fwd, h1, gmix, "f_n2")
    z = gathering(("w_mix_out", "w_co"), mm_nn_g, n2, full["w_mix_in"], F32, "f_z")
    ya = gathering((("w_ckv", (0, 512)),), sgu_fwd, z, ln_g, ln_b, w_s, b_t, "f_sgu")
    yb = gathering((("w_ckv", (512, 1536)), ("ffn2_w_in", (0, 512))), sb_fwd, z, "f_sb")
    ycat = gathering((("ffn2_w_in", (512, 128)),), rmscat_fwd, ya, yb, ga, gb, "f_ycat")
    h2 = gathering((("ffn2_w_in", (640, 256)),), mm_nn, ycat, full["w_mix_out"], F32, "f_h2", res=h1)
    n3 = gathering((("ffn2_w_in", (896, 128)),), rms_fwd, h2, gcross, "f_n3")
    memn = rms_fwd(mems, gmem, "f_memn")
    qc = gathering((("ffn2_w_in", (1024, 256)),), mm_nn, n3, full["w_cq"], BF16, "f_qc", scale=X_DH ** -0.5)
    kv = gathering((("ffn2_w_in", (1280, 128)),), mm_nn_g, memn, full["w_ckv"], BF16, "f_kv")
    o = gathering((("ffn2_w_in", (1408, 128)),), xattn_fwd, qc, kv, "f_xattn")
    h3 = gathering((("ffn2_w_in", (1536, 256)),), mm_nn, o, full["w_co"], F32, "f_h3", res=h2)
    n4 = gathering((("ffn2_w_in", (1792, 256)),), rms_fwd, h3, g2, "f_n4")
    a2, hsw2 = gathering(("ffn2_w_out",), mm_swiglu_g, n4, full["ffn2_w_in"], "f_a2")
    h4 = mm_nn(hsw2, full["ffn2_w_out"], F32, "f_h4", scale=0.5, res=h3)

    grads, parts, sums, recv = {}, {}, {}, {}
    core = lax.axis_index("c").astype(jnp.int32).reshape(1)

    def partial_of(k, g):
        grads[k] = g
        parts[k] = g if g.ndim == 3 else g.reshape(N_DEV, -1, g.shape[1])

    def reducing(pairs, chips, fn, *args, **kw):
        def piece(p):
            if isinstance(p, dict):
                return p
            k, rows = p if isinstance(p, tuple) else (p, None)
            return dict(sums=k, rows=rows, to=k)

        chips = [piece(p) for p in chips]
        comms = [comm_pairs([parts[k] for k in pairs])] if pairs else []
        comms += [comm_chips([sums[p["sums"]]], p["rows"], [recv[p["to"]]] if p["to"] in recv else None,
                             p.get("from_row"), p.get("out_rows")) for p in chips]
        comm = merge_comms(comms)
        out, got = fn(*args, comm=comm, **kw)
        got = split_results(comm, got)
        if pairs:
            for k, r in zip(pairs, got.pop(0)):
                sums[k] = pair_sum(parts[k], r, core, f"pair_sum_{k}")
        for p, (r,) in zip(chips, got):
            recv[p["to"]] = r
        return out

    loss_part, dh4, df2, grads["final_norm"] = loss_head(h4, tgt, gfin, "loss_head")
    partial_of("ffn2_w_out", mm_tn(hsw2, df2, "b_ffn2_dwout"))
    dhsw2 = reducing(("ffn2_w_out",), (), mm_nt, df2, full["ffn2_w_out"], BF16, "b_ffn2_dhsw")
    da2 = swiglu_bwd(a2, dhsw2, "b_ffn2_swiglu_bwd")
    partial_of("ffn2_w_in", reducing((), ("ffn2_w_out",), mm_tn_g, n4, da2, N_DEV, "b_ffn2_dwin"))
    dn4 = reducing(("ffn2_w_in",), (), mm_nt_g, da2, full["ffn2_w_in"], "b_ffn2_dn")
    dh3, dh3b, grads["ffn2_norm"] = rms_bwd(dn4, h3, g2, dh4, 1.0, "b_n4")

    partial_of("w_co", mm_tn(o, dh3b, "b_dwco"))
    do = reducing(("w_co",), (), mm_nt, dh3b, full["w_co"], BF16, "b_do")
    dqp, dkv = xattn_bwd(qc, kv, do, "b_xattn")
    partial_of("w_cq", mm_tn(n3, dqp, "b_dwcq"))
    dn3 = reducing(("w_cq",), (), mm_nt, dqp, full["w_cq"], F32, "b_dn3")
    partial_of("w_ckv", mm_tn_g(memn, dkv, N_DEV, "b_dwckv"))
    dmemn = reducing(("w_ckv",), (), mm_nt_g, dkv, full["w_ckv"], "b_dmemn")
    _, _, grads["mem_norm"] = rms_bwd(dmemn, mems, gmem, None, 1.0, "b_memn")
    dh2, dh2b, grads["cross_norm"] = rms_bwd(dn3, h2, gcross, dh3, 1.0, "b_n3")

    partial_of("w_mix_out", mm_tn(ycat, dh2b, "b_dwmixout"))
    dycat = reducing(("w_mix_out",), (), mm_nt, dh2b, full["w_mix_out"], F32, "b_dycat")
    dya, dyb, grads["gnorm_a"], grads["gnorm_b"] = rmscat_bwd(dycat, ya, yb, ga, gb, "b_ycat")
    dza, grads["ln_v_gain"], grads["ln_v_bias"], grads["spatial_w"], grads["spatial_b"] = sgu_bwd(
        z, dya, ln_g, ln_b, w_s, b_t, "b_sgu")
    dq, dk, dv = reducing((), ("ffn2_w_in", "w_co", "w_cq"), sb_bwd, z, yb, dyb, "b_sb")
    dz = jnp.concatenate([dza, dq, dk, dv], axis=1)
    partial_of("w_mix_in", reducing((), ("w_ckv",), mm_tn_g, n2, dz, N_DEV, "b_dwmixin"))
    dn2 = reducing(("w_mix_in",), ("w_mix_out",), mm_nt_g, dz, full["w_mix_in"], "b_dn2")
    dh1, dh1b, grads["mix_norm"] = reducing((), (("w_mix_in", (0, 512)),), rms_bwd, dn2, h1, gmix, dh2, 0.5, "b_n2")

    partial_of("ffn1_w_out", reducing((), (("w_mix_in", (512, 1024)),), mm_tn, hsw1, dh1b, "b_ffn1_dwout"))
    dhsw1 = reducing(("ffn1_w_out",), (("w_mix_in", (1536, 512)),), mm_nt, dh1b, full["ffn1_w_out"], BF16,
                     "b_ffn1_dhsw")
    da1 = swiglu_bwd(a1, dhsw1, "b_ffn1_swiglu_bwd")
    half = D // 2
    partial_of("ffn1_w_in_a", reducing((), (("ffn1_w_out", (0, 352)),), mm_tn_g, n1, da1, N_DEV, "b_ffn1_dwin_a",
                                       cols=(0, half)))
    partial_of("ffn1_w_in_b", reducing(("ffn1_w_in_a",), (("ffn1_w_out", (352, 352)),), mm_tn_g, n1, da1, N_DEV,
                                       "b_ffn1_dwin_b", cols=(half, half)))
    dn1 = reducing(("ffn1_w_in_b",), (dict(sums="ffn1_w_in_a", rows=(0, half), to="ffn1_w_in", out_rows=D),),
                   mm_nt_g, da1, full["ffn1_w_in"], "b_ffn1_dn")
    (dx, _, grads["ffn1_norm"]) = reducing((), (dict(sums="ffn1_w_in_b", rows=(half, half), from_row=0, to="ffn1_w_in"),),
                                           rms_bwd, dn1, xs, g1, dh1, 1.0, "b_n1")

    out_g, out_d, out_m, out_v = {}, {}, {}, {}
    for k in _BIG:
        res = adamw(recv[k], shard2d[k], mom[k].reshape(shard2d[k].shape), var[k].reshape(shard2d[k].shape), f"adamw_{k}")
        out_g[k], out_d[k], out_m[k], out_v[k] = (t.reshape(wts[k].shape) for t in res)

    pack = lambda d: jnp.concatenate([_rows128(d[k]) for k in _SMALL] + [jnp.zeros((_SMALL_PAD, 128), F32)], axis=0)
    small_part = pack(grads)
    (small_sibling,) = run_comm(comm_pairs([small_part]), "comm_pairs_small")
    small_pair = add2(small_part, small_sibling, "pair_sum_small")
    (small_all,) = run_comm(comm_chips([small_pair]), "comm_chips_small")

    res = adamw(small_all, pack(wts), pack(mom), pack(var), "adamw_small")
    row = 0
    for k in _SMALL:
        nrow = wts[k].size // 128
        for dst, t in zip((out_g, out_d, out_m, out_v), res):
            dst[k] = t[row:row + nrow].reshape(wts[k].shape)
        row += nrow

    loss = lax.psum(loss_part[0, 0], ("x", "y", "c"))
    grad_x = dx.reshape(x.shape)
    return (loss, grad_x, *[out_g[k] for k in _ORDER], *[out_d[k] for k in _ORDER],
            *[out_m[k] for k in _ORDER], *[out_v[k] for k in _ORDER])
```

```python
import functools
import math

import jax
import jax.numpy as jnp
from jax import lax
from jax.experimental import pallas as pl
from jax.experimental.pallas import tpu as pltpu

F32 = jnp.float32
BF16 = jnp.bfloat16

N_DEV = 8
D_MODEL = 2048
D_FF = 5632
W_A = 1024
G_A = 8
GA_DIM = 128
SGU_BLOCK = 128
CHUNK = 64
H_B = 8
DH_B = 128
Q_BLOCK = 128
X_HEADS = 4
X_DH = 512
N_MEM = 256
EPS = 1e-6

ADAM_LR = 0.001
ADAM_B1 = 0.9
ADAM_B2 = 0.999
ADAM_EPS = 1e-08
ADAM_WD = 0.01
ADAM_STEP = 10

VMEM_LIMIT = 56 * 2**20
ROW_TILE = 256

MESH = pl.DeviceIdType.MESH
ANY = pl.BlockSpec(memory_space=pl.ANY)

_NT = (((1,), (1,)), ((), ()))
_TN = (((0,), (0,)), ((), ()))


def _params(*sem):
    return pltpu.CompilerParams(dimension_semantics=sem, vmem_limit_bytes=VMEM_LIMIT)


def _zeros(ref):
    return jnp.zeros(ref.shape, ref.dtype)


def _pcall(comm, body, *, name, grid, in_specs, out_specs, out_shape, compiler_params, scratch_shapes=()):
    if comm is None:
        return pl.pallas_call(body, name=name, grid=grid, in_specs=in_specs, out_specs=out_specs, out_shape=out_shape,
                              scratch_shapes=list(scratch_shapes), compiler_params=compiler_params)
    multi = isinstance(out_shape, (list, tuple))
    out_shapes = list(out_shape) if multi else [out_shape]
    out_specs_l = list(out_specs) if multi else [out_specs]
    n_in, n_out, n_scr = len(in_specs), len(out_shapes), len(scratch_shapes)
    n_cin, n_cout = len(comm.arrays), len(comm.out_shapes)

    def with_comm(*refs):
        ins, refs = refs[:n_in], refs[n_in:]
        cins, refs = refs[:n_cin], refs[n_cin:]
        outs, refs = refs[:n_out], refs[n_out:]
        couts, refs = refs[:n_cout], refs[n_cout:]
        scr, sems = refs[:n_scr], refs[n_scr:]
        first = functools.reduce(jnp.logical_and, [pl.program_id(a) == 0 for a in range(len(grid))])
        last = functools.reduce(jnp.logical_and, [pl.program_id(a) == grid[a] - 1 for a in range(len(grid))])
        pl.when(first)(lambda: comm.start(cins, couts, sems))
        body(*ins, *outs, *scr)
        pl.when(last)(lambda: comm.finish(cins, couts, sems))

    call = pl.pallas_call(
        with_comm, name=name, grid=grid, in_specs=list(in_specs) + [ANY] * n_cin,
        out_specs=out_specs_l + [ANY] * n_cout, out_shape=out_shapes + comm.out_shapes,
        scratch_shapes=list(scratch_shapes) + comm.sem_shapes(), compiler_params=_params(*(("arbitrary",) * len(grid))),
        input_output_aliases={n_in + i: n_out + j for i, j in comm.aliases.items()})

    def run(*args):
        res = call(*args, *comm.arrays)
        main = res[:n_out]
        return (list(main) if multi else main[0]), list(res[n_out:])

    return run


def _dot(a, b):
    return jnp.dot(a, b, preferred_element_type=F32)


def _dot_nt(a, b):
    return lax.dot_general(a, b, _NT, preferred_element_type=F32)


def _dot_tn(a, b):
    return lax.dot_general(a, b, _TN, preferred_element_type=F32)


def mm_nn_g(a, bg, out_dtype, name, tm=512, comm=None):
    M, K = a.shape
    G, _, n = bg.shape
    tm = min(tm, M)

    def body(a_ref, b_ref, o_ref):
        o_ref[...] = _dot(a_ref[...], b_ref[...]).astype(o_ref.dtype)

    return _pcall(
        comm, body, name=name, grid=(G, M // tm),
        in_specs=[pl.BlockSpec((tm, K), lambda g, m: (m, 0)),
                  pl.BlockSpec((None, K, n), lambda g, m: (g, 0, 0))],
        out_specs=pl.BlockSpec((tm, n), lambda g, m: (m, g)),
        out_shape=jax.ShapeDtypeStruct((M, G * n), out_dtype),
        compiler_params=_params("parallel", "parallel"),
    )(a, bg)


def mm_swiglu_g(a, bg, name, tm=512, comm=None):
    M, K = a.shape
    G, _, n = bg.shape
    half = G // 2
    tm = min(tm, M)

    def body(a_ref, bgate_ref, bup_ref, gu_ref, h_ref):
        av = a_ref[...]
        gate = _dot(av, bgate_ref[...])
        up = _dot(av, bup_ref[...])
        gu_ref[0] = gate.astype(BF16)
        gu_ref[1] = up.astype(BF16)
        h_ref[...] = (gate * _sigmoid(gate) * up).astype(BF16)

    return _pcall(
        comm, body, name=name, grid=(half, M // tm),
        in_specs=[pl.BlockSpec((tm, K), lambda p, m: (m, 0)),
                  pl.BlockSpec((None, K, n), lambda p, m: (p, 0, 0)),
                  pl.BlockSpec((None, K, n), lambda p, m: (p + half, 0, 0))],
        out_specs=[pl.BlockSpec((2, tm, n), lambda p, m: (0, m, p)), pl.BlockSpec((tm, n), lambda p, m: (m, p))],
        out_shape=[jax.ShapeDtypeStruct((2, M, half * n), BF16), jax.ShapeDtypeStruct((M, half * n), BF16)],
        compiler_params=_params("parallel", "parallel"),
    )(a, bg, bg)


def mm_swiglu_bwd(dy, w_out, gate_up, name, tk=512, comm=None):
    M, N = dy.shape
    F = w_out.shape[0]

    def body(dy_ref, w_ref, gu_ref, o_ref):
        dh = _dot_nt(dy_ref[...], w_ref[...])
        gt = gu_ref[0].astype(F32)
        up = gu_ref[1].astype(F32)
        sg = _sigmoid(gt)
        o_ref[0] = (dh * up * (sg * (1.0 + gt * (1.0 - sg)))).astype(BF16)
        o_ref[1] = (dh * (gt * sg)).astype(BF16)

    planes = pl.BlockSpec((2, M, tk), lambda k: (0, 0, k))
    return _pcall(
        comm, body, name=name, grid=(F // tk,),
        in_specs=[pl.BlockSpec((M, N), lambda k: (0, 0)), pl.BlockSpec((tk, N), lambda k: (k, 0)), planes],
        out_specs=planes,
        out_shape=jax.ShapeDtypeStruct((2, M, F), BF16),
        compiler_params=_params("parallel"),
    )(dy, w_out, gate_up)


def _shard_cols_spec(dy, G, rows, index):
    if dy.ndim == 2:
        n = dy.shape[1] // G
        return pl.BlockSpec((rows, n), lambda *ids: index(*ids)), n
    half = G // 2
    n = dy.shape[2] // half

    def planes(*ids):
        r, g = index(*ids)
        return g // half, r, g % half

    return pl.BlockSpec((None, rows, n), planes), n


def mm_nn(a, b, out_dtype, name, tm=512, tn=1024, scale=1.0, res=None, comm=None):
    M, K = a.shape
    _, N = b.shape
    tm, tn = min(tm, M), min(tn, N)

    def body(*refs):
        if res is None:
            a_ref, b_ref, o_ref = refs
            acc = _dot(a_ref[...], b_ref[...])
            o_ref[...] = (acc * scale if scale != 1.0 else acc).astype(o_ref.dtype)
        else:
            a_ref, b_ref, r_ref, o_ref = refs
            o_ref[...] = (r_ref[...] + scale * _dot(a_ref[...], b_ref[...])).astype(o_ref.dtype)

    in_specs = [pl.BlockSpec((tm, K), lambda n, m: (m, 0)),
                pl.BlockSpec((K, tn), lambda n, m: (0, n))]
    args = [a, b]
    if res is not None:
        in_specs.append(pl.BlockSpec((tm, tn), lambda n, m: (m, n)))
        args.append(res)
    return _pcall(
        comm, body, name=name, grid=(N // tn, M // tm),
        in_specs=in_specs,
        out_specs=pl.BlockSpec((tm, tn), lambda n, m: (m, n)),
        out_shape=jax.ShapeDtypeStruct((M, N), out_dtype),
        compiler_params=_params("parallel", "parallel"),
    )(*args)


def mm_nt_g(dy, bg, name, tm=512, comm=None):
    M = dy.shape[-2]
    G, K, n = bg.shape
    tm = min(tm, M)
    dy_spec, _ = _shard_cols_spec(dy, G, tm, lambda m, g: (m, g))

    def body(dy_ref, b_ref, o_ref):
        part = _dot_nt(dy_ref[...], b_ref[...])

        @pl.when(pl.program_id(1) == 0)
        def _():
            o_ref[...] = part

        @pl.when(pl.program_id(1) > 0)
        def _():
            o_ref[...] += part

    return _pcall(
        comm, body, name=name, grid=(M // tm, G),
        in_specs=[dy_spec, pl.BlockSpec((None, K, n), lambda m, g: (g, 0, 0))],
        out_specs=pl.BlockSpec((tm, K), lambda m, g: (m, 0)),
        out_shape=jax.ShapeDtypeStruct((M, K), F32),
        compiler_params=_params("parallel", "arbitrary"),
    )(dy, bg)


def mm_nt(dy, b, out_dtype, name, tk=512, comm=None):
    M, N = dy.shape
    K, _ = b.shape

    def body(dy_ref, b_ref, o_ref):
        o_ref[...] = _dot_nt(dy_ref[...], b_ref[...]).astype(o_ref.dtype)

    return _pcall(
        comm, body, name=name, grid=(K // tk,),
        in_specs=[pl.BlockSpec((M, N), lambda k: (0, 0)),
                  pl.BlockSpec((tk, N), lambda k: (k, 0))],
        out_specs=pl.BlockSpec((M, tk), lambda k: (0, k)),
        out_shape=jax.ShapeDtypeStruct((M, K), out_dtype),
        compiler_params=_params("parallel"),
    )(dy, b)


def mm_tn_g(x, dy, G, name, tk=512, cols=None, comm=None):
    M, K = x.shape
    k0, K = cols if cols is not None else (0, K)
    dy_spec, n = _shard_cols_spec(dy, G, M, lambda g, k: (0, g))

    def body(x_ref, dy_ref, o_ref):
        o_ref[...] = _dot_tn(x_ref[...], dy_ref[...]).astype(o_ref.dtype)

    return _pcall(
        comm, body, name=name, grid=(G, K // tk),
        in_specs=[pl.BlockSpec((M, tk), lambda g, k: (0, k + k0 // tk)), dy_spec],
        out_specs=pl.BlockSpec((None, tk, n), lambda g, k: (g, k, 0)),
        out_shape=jax.ShapeDtypeStruct((G, K, n), BF16),
        compiler_params=_params("parallel", "parallel"),
    )(x, dy)


def mm_tn(x, dy, name, tk=512, comm=None):
    M, K = x.shape
    _, N = dy.shape

    def body(x_ref, dy_ref, o_ref):
        o_ref[...] = _dot_tn(x_ref[...], dy_ref[...]).astype(o_ref.dtype)

    return _pcall(
        comm, body, name=name, grid=(K // tk,),
        in_specs=[pl.BlockSpec((M, tk), lambda k: (0, k)),
                  pl.BlockSpec((M, N), lambda k: (0, 0))],
        out_specs=pl.BlockSpec((tk, N), lambda k: (k, 0)),
        out_shape=jax.ShapeDtypeStruct((K, N), BF16),
        compiler_params=_params("parallel"),
    )(x, dy)


def _rstd(x):
    return lax.rsqrt(jnp.mean(x * x, axis=-1, keepdims=True) + EPS)


def _rms_bwd(dn, xhat, r, g):
    dxhat = dn * g
    return r * (dxhat - xhat * jnp.mean(dxhat * xhat, axis=-1, keepdims=True))


def _row_spec(tr, width, col=0):
    return pl.BlockSpec((tr, width), lambda i: (i, col))


def _vec_spec(width):
    return pl.BlockSpec((1, width), lambda i: (0, 0))


def rms_fwd(x, g, name, comm=None):
    M, D = x.shape
    tr = min(ROW_TILE, M)

    def body(x_ref, g_ref, o_ref):
        xv = x_ref[...]
        o_ref[...] = (xv * _rstd(xv) * g_ref[...]).astype(o_ref.dtype)

    return _pcall(
        comm, body, name=name, grid=(M // tr,),
        in_specs=[_row_spec(tr, D), _vec_spec(D)],
        out_specs=_row_spec(tr, D),
        out_shape=jax.ShapeDtypeStruct((M, D), BF16),
        compiler_params=_params("parallel"),
    )(x, g)


def rms_bwd(dn, h, g, dres, copy_scale, name, comm=None):
    M, D = h.shape
    tr = min(ROW_TILE, M)
    has_res = dres is not None

    def body(*refs):
        if has_res:
            dn_ref, h_ref, g_ref, dres_ref, dh_ref, dhb_ref, dg_ref = refs
        else:
            dn_ref, h_ref, g_ref, dh_ref, dhb_ref, dg_ref = refs
        hv = h_ref[...]
        r = _rstd(hv)
        xhat = hv * r
        dn = dn_ref[...]
        part = jnp.sum(dn * xhat, axis=0, keepdims=True)

        @pl.when(pl.program_id(0) == 0)
        def _():
            dg_ref[...] = part

        @pl.when(pl.program_id(0) > 0)
        def _():
            dg_ref[...] += part

        dh = _rms_bwd(dn, xhat, r, g_ref[...])
        if has_res:
            dh = dh + dres_ref[...]
        dh_ref[...] = dh
        dhb_ref[...] = (dh * copy_scale if copy_scale != 1.0 else dh).astype(BF16)

    in_specs = [_row_spec(tr, D), _row_spec(tr, D), _vec_spec(D)]
    args = [dn, h, g]
    if has_res:
        in_specs.append(_row_spec(tr, D))
        args.append(dres)
    return _pcall(
        comm, body, name=name, grid=(M // tr,),
        in_specs=in_specs,
        out_specs=[_row_spec(tr, D), _row_spec(tr, D), _vec_spec(D)],
        out_shape=[jax.ShapeDtypeStruct((M, D), F32), jax.ShapeDtypeStruct((M, D), BF16),
                   jax.ShapeDtypeStruct((1, D), F32)],
        compiler_params=_params("arbitrary"),
    )(*args)


def _sigmoid(x):
    return 1.0 / (1.0 + jnp.exp(-x))


def rmscat_fwd(ya, yb, ga, gb, name, comm=None):
    M, W = ya.shape
    tr = min(ROW_TILE, M)

    def body(ya_ref, yb_ref, ga_ref, gb_ref, o_ref):
        a = ya_ref[...]
        b = yb_ref[...]
        o_ref[:, :W] = (a * _rstd(a) * ga_ref[...]).astype(BF16)
        o_ref[:, W:] = (b * _rstd(b) * gb_ref[...]).astype(BF16)

    return _pcall(
        comm, body, name=name, grid=(M // tr,),
        in_specs=[_row_spec(tr, W), _row_spec(tr, W), _vec_spec(W), _vec_spec(W)],
        out_specs=_row_spec(tr, 2 * W),
        out_shape=jax.ShapeDtypeStruct((M, 2 * W), BF16),
        compiler_params=_params("parallel"),
    )(ya, yb, ga, gb)


def rmscat_bwd(dycat, ya, yb, ga, gb, name, comm=None):
    M, W = ya.shape
    tr = min(ROW_TILE, M)

    def body(dc_ref, ya_ref, yb_ref, ga_ref, gb_ref, dya_ref, dyb_ref, dga_ref, dgb_ref):
        first = pl.program_id(0) == 0
        for y_ref, g_ref, dy_ref, dg_ref, lo in ((ya_ref, ga_ref, dya_ref, dga_ref, 0),
                                                 (yb_ref, gb_ref, dyb_ref, dgb_ref, W)):
            yv = y_ref[...]
            r = _rstd(yv)
            xhat = yv * r
            dn = dc_ref[:, lo:lo + W]
            part = jnp.sum(dn * xhat, axis=0, keepdims=True)

            @pl.when(first)
            def _():
                dg_ref[...] = part

            @pl.when(jnp.logical_not(first))
            def _():
                dg_ref[...] += part

            dy_ref[...] = _rms_bwd(dn, xhat, r, g_ref[...])

    return _pcall(
        comm, body, name=name, grid=(M // tr,),
        in_specs=[_row_spec(tr, 2 * W), _row_spec(tr, W), _row_spec(tr, W), _vec_spec(W), _vec_spec(W)],
        out_specs=[_row_spec(tr, W), _row_spec(tr, W), _vec_spec(W), _vec_spec(W)],
        out_shape=[jax.ShapeDtypeStruct((M, W), F32), jax.ShapeDtypeStruct((M, W), F32),
                   jax.ShapeDtypeStruct((1, W), F32), jax.ShapeDtypeStruct((1, W), F32)],
        compiler_params=_params("arbitrary"),
    )(dycat, ya, yb, ga, gb)


def loss_head(h, target, g, name, comm=None):
    M, D = h.shape
    tr = min(ROW_TILE, M)

    def body(h_ref, t_ref, g_ref, loss_ref, dh_ref, dhb_ref, dg_ref):
        hv = h_ref[...]
        gv = g_ref[...]
        r = _rstd(hv)
        xhat = hv * r
        err = xhat * gv - t_ref[...]
        lsum = jnp.sum(jnp.sum(err * err, axis=1, keepdims=True), axis=0, keepdims=True) * (0.5 / D)
        dy = err * (1.0 / D)
        part = jnp.sum(dy * xhat, axis=0, keepdims=True)

        @pl.when(pl.program_id(0) == 0)
        def _():
            dg_ref[...] = part
            loss_ref[...] = _zeros(loss_ref) + lsum

        @pl.when(pl.program_id(0) > 0)
        def _():
            dg_ref[...] += part
            loss_ref[...] += lsum

        dh = _rms_bwd(dy, xhat, r, gv)
        dh_ref[...] = dh
        dhb_ref[...] = (0.5 * dh).astype(BF16)

    return _pcall(
        comm, body, name=name, grid=(M // tr,),
        in_specs=[_row_spec(tr, D), _row_spec(tr, D), _vec_spec(D)],
        out_specs=[pl.BlockSpec((8, 128), lambda i: (0, 0)), _row_spec(tr, D), _row_spec(tr, D), _vec_spec(D)],
        out_shape=[jax.ShapeDtypeStruct((8, 128), F32), jax.ShapeDtypeStruct((M, D), F32),
                   jax.ShapeDtypeStruct((M, D), BF16), jax.ShapeDtypeStruct((1, D), F32)],
        compiler_params=_params("arbitrary"),
    )(h, target, g)


_GELU_C = math.sqrt(2.0 / math.pi)


def _gelu(x):
    return 0.5 * x * (1.0 + jnp.tanh(_GELU_C * (x + 0.044715 * (x * x * x))))


def _gelu_grad(x):
    t = jnp.tanh(_GELU_C * (x + 0.044715 * (x * x * x)))
    return 0.5 * (1.0 + t) + 0.5 * x * (1.0 - t * t) * (_GELU_C * (1.0 + 3.0 * 0.044715 * (x * x)))


def _sgu_mask():
    t = lax.broadcasted_iota(jnp.int32, (SGU_BLOCK, SGU_BLOCK), 0) // CHUNK
    s = lax.broadcasted_iota(jnp.int32, (SGU_BLOCK, SGU_BLOCK), 1) // CHUNK
    return s <= t


def _layernorm_stats(v):
    mu = jnp.mean(v, axis=-1, keepdims=True)
    cen = v - mu
    rstd = lax.rsqrt(jnp.mean(cen * cen, axis=-1, keepdims=True) + EPS)
    return cen * rstd, rstd


def sgu_fwd(z, ln_g, ln_b, w_s, b_t, name, comm=None):
    S = z.shape[0]

    def body(zu_ref, zv_ref, lg_ref, lb_ref, w_ref, bt_ref, o_ref):
        mask = _sgu_mask()
        for g in range(G_A):
            cols = slice(g * GA_DIM, (g + 1) * GA_DIM)
            u = _gelu(zu_ref[:, cols])
            vhat, _ = _layernorm_stats(_gelu(zv_ref[:, cols]))
            vln = vhat * lg_ref[:, cols] + lb_ref[:, cols]
            w = jnp.where(mask, w_ref[g], 0.0).astype(BF16)
            mixed = _dot(w, vln.astype(BF16)) + bt_ref[:, g:g + 1]
            o_ref[:, cols] = u * mixed

    return _pcall(
        comm, body, name=name, grid=(S // SGU_BLOCK,),
        in_specs=[_row_spec(SGU_BLOCK, W_A, 0), _row_spec(SGU_BLOCK, W_A, 1), _vec_spec(W_A), _vec_spec(W_A),
                  pl.BlockSpec((G_A, SGU_BLOCK, SGU_BLOCK), lambda i: (0, 0, 0)),
                  pl.BlockSpec((SGU_BLOCK, G_A), lambda i: (0, 0))],
        out_specs=_row_spec(SGU_BLOCK, W_A),
        out_shape=jax.ShapeDtypeStruct((S, W_A), F32),
        compiler_params=_params("parallel"),
    )(z, z, ln_g, ln_b, w_s, b_t)


def sgu_bwd(z, dya, ln_g, ln_b, w_s, b_t, name, comm=None):
    S = z.shape[0]
    nblk = S // SGU_BLOCK

    def body(zu_ref, zv_ref, dy_ref, lg_ref, lb_ref, w_ref, bt_ref,
             dz_ref, dlg_ref, dlb_ref, dw_ref, db_ref, dmix_acc):
        step = pl.program_id(0)
        mask = _sgu_mask()

        @pl.when(step == 0)
        def _():
            dlg_ref[...] = _zeros(dlg_ref)
            dlb_ref[...] = _zeros(dlb_ref)
            dw_ref[...] = _zeros(dw_ref)
            dmix_acc[...] = _zeros(dmix_acc)

        for g in range(G_A):
            cols = slice(g * GA_DIM, (g + 1) * GA_DIM)
            zu = zu_ref[:, cols]
            zv = zv_ref[:, cols]
            u = _gelu(zu)
            vhat, rstd = _layernorm_stats(_gelu(zv))
            lg = lg_ref[:, cols]
            vln = (vhat * lg + lb_ref[:, cols]).astype(BF16)
            w = jnp.where(mask, w_ref[g], 0.0)
            mixed = _dot(w.astype(BF16), vln) + bt_ref[:, g:g + 1]
            dy = dy_ref[:, cols]
            du = dy * mixed
            dmixed = dy * u
            dmixed_b = dmixed.astype(BF16)
            dmix_acc[g] += dmixed
            dw_ref[g] += jnp.where(mask, _dot_nt(dmixed_b, vln), 0.0)
            dvln = _dot(w.T.astype(BF16), dmixed_b)
            dlb_ref[:, cols] += jnp.sum(dvln, axis=0, keepdims=True)
            dlg_ref[:, cols] += jnp.sum(dvln * vhat, axis=0, keepdims=True)
            dvhat = dvln * lg
            dv = rstd * (dvhat - jnp.mean(dvhat, axis=-1, keepdims=True)
                         - vhat * jnp.mean(dvhat * vhat, axis=-1, keepdims=True))
            dz_ref[:, cols] = (du * _gelu_grad(zu)).astype(BF16)
            dz_ref[:, W_A + g * GA_DIM:W_A + (g + 1) * GA_DIM] = (dv * _gelu_grad(zv)).astype(BF16)

        @pl.when(step == nblk - 1)
        def _():
            for g in range(G_A):
                db_ref[g] = jnp.sum(dmix_acc[g], axis=1, keepdims=True)

    whole3 = lambda shape: pl.BlockSpec(shape, lambda i: (0, 0, 0))
    return _pcall(
        comm, body, name=name, grid=(nblk,),
        in_specs=[_row_spec(SGU_BLOCK, W_A, 0), _row_spec(SGU_BLOCK, W_A, 1), _row_spec(SGU_BLOCK, W_A),
                  _vec_spec(W_A), _vec_spec(W_A), whole3((G_A, SGU_BLOCK, SGU_BLOCK)),
                  pl.BlockSpec((SGU_BLOCK, G_A), lambda i: (0, 0))],
        out_specs=[_row_spec(SGU_BLOCK, 2 * W_A), _vec_spec(W_A), _vec_spec(W_A),
                   whole3((G_A, SGU_BLOCK, SGU_BLOCK)), whole3((G_A, SGU_BLOCK, 1))],
        out_shape=[jax.ShapeDtypeStruct((S, 2 * W_A), BF16), jax.ShapeDtypeStruct((1, W_A), F32),
                   jax.ShapeDtypeStruct((1, W_A), F32), jax.ShapeDtypeStruct((G_A, SGU_BLOCK, SGU_BLOCK), F32),
                   jax.ShapeDtypeStruct((G_A, SGU_BLOCK, 1), F32)],
        scratch_shapes=[pltpu.VMEM((G_A, SGU_BLOCK, SGU_BLOCK), F32)],
        compiler_params=_params("arbitrary"),
    )(z, z, dya, ln_g, ln_b, w_s, b_t)


def _log_sigmoid(z):
    return jnp.minimum(z, 0.0) - jnp.log(1.0 + jnp.exp(-jnp.abs(z)))


def _suffix_sum(x, upper):
    hi = x.astype(BF16)
    lo = (x - hi.astype(F32)).astype(BF16)
    return _dot(hi, upper) + _dot(lo, upper)


SB_ROWS = 1024
_SB_SUB = SB_ROWS // Q_BLOCK


def _sb_upper():
    row = lax.broadcasted_iota(jnp.int32, (Q_BLOCK, Q_BLOCK), 0)
    col = lax.broadcasted_iota(jnp.int32, (Q_BLOCK, Q_BLOCK), 1)
    return (row > col).astype(BF16)


def _sb_sweep(step, tile):
    for r in reversed(range(_SB_SUB)):
        tile(step * _SB_SUB + r, r * Q_BLOCK)

    def group(g, _):
        base = (step - 1 - g) * _SB_SUB
        for r in reversed(range(_SB_SUB)):
            tile(base + r, None)
        return 0

    lax.fori_loop(0, step, group, 0)


def _sb_causal(n):
    return lax.broadcasted_iota(jnp.int32, (n, Q_BLOCK), 1) < lax.broadcasted_iota(jnp.int32, (n, Q_BLOCK), 0)


def _sb_col(part):
    return (2 * W_A + part * (H_B * DH_B)) // DH_B


def _sb_q_spec():
    return pl.BlockSpec((SB_ROWS, DH_B), lambda h, i: (i, _sb_col(0) + h))


def _sb_kv_spec(S, part):
    return pl.BlockSpec((S, DH_B), lambda h, i: (0, _sb_col(part) + h))


def sb_fwd(z, name, comm=None):
    S = z.shape[0]
    scale = DH_B ** -0.5

    def body(q_ref, k_ref, v_ref, o_ref, q_b, c_l1m):
        step = pl.program_id(1)
        q_b[...] = (q_ref[...] * scale).astype(BF16)
        o_ref[...] = _zeros(o_ref)
        c_l1m[...] = _zeros(c_l1m)
        upper = _sb_upper()

        def tile(j, row0):
            rq = slice(row0 or 0, SB_ROWS)
            causal = None if row0 is None else _sb_causal(SB_ROWS - row0)
            rows = pl.ds(pl.multiple_of(j * Q_BLOCK, Q_BLOCK), Q_BLOCK)
            zz = _dot_nt(q_b[rq, :], k_ref[rows, :].astype(BF16))
            lb = _log_sigmoid(zz)
            l1m = lb - zz
            if causal is not None:
                l1m = jnp.where(causal, l1m, 0.0)
            a = jnp.exp(lb + _suffix_sum(l1m, upper) + c_l1m[rq, :])
            if causal is not None:
                a = jnp.where(causal, a, 0.0)
            o_ref[rq, :] += _dot(a.astype(BF16), v_ref[rows, :].astype(BF16))
            c_l1m[rq, :] += jnp.sum(l1m, axis=1, keepdims=True)

        _sb_sweep(step, tile)

    return _pcall(
        comm, body, name=name, grid=(H_B, S // SB_ROWS),
        in_specs=[_sb_q_spec(), _sb_kv_spec(S, 1), _sb_kv_spec(S, 2)],
        out_specs=pl.BlockSpec((SB_ROWS, DH_B), lambda h, i: (i, h)),
        out_shape=jax.ShapeDtypeStruct((S, H_B * DH_B), F32),
        scratch_shapes=[pltpu.VMEM((SB_ROWS, DH_B), BF16), pltpu.VMEM((SB_ROWS, 1), F32)],
        compiler_params=_params("parallel", "parallel"),
    )(z, z, z)


def sb_bwd(z, out, dout, name, comm=None):
    S = z.shape[0]
    nstep = S // SB_ROWS
    scale = DH_B ** -0.5

    def body(q_ref, k_ref, v_ref, o_ref, do_ref, dq_ref, dk_ref, dv_ref,
             dq_acc, dkt_acc, dvt_acc, q_b, do_b, qt_b, dot_b, g_left, c_l1m):
        step = pl.program_id(1)

        @pl.when(step == 0)
        def _():
            dkt_acc[...] = _zeros(dkt_acc)
            dvt_acc[...] = _zeros(dvt_acc)

        q_b[...] = (q_ref[...] * scale).astype(BF16)
        do_b[...] = do_ref[...].astype(BF16)
        qt_b[...] = (q_ref[...].T * scale).astype(BF16)
        dot_b[...] = do_ref[...].T.astype(BF16)
        g_left[...] = jnp.sum(do_b[...].astype(F32) * o_ref[...], axis=1, keepdims=True)
        dq_acc[...] = _zeros(dq_acc)
        c_l1m[...] = _zeros(c_l1m)
        upper = _sb_upper()

        def tile(j, row0):
            rq = slice(row0 or 0, SB_ROWS)
            causal = None if row0 is None else _sb_causal(SB_ROWS - row0)
            rows = pl.ds(pl.multiple_of(j * Q_BLOCK, Q_BLOCK), Q_BLOCK)
            q, do_t = q_b[rq, :], do_b[rq, :]
            k_j = k_ref[rows, :].astype(BF16)
            zz = _dot_nt(q, k_j)
            lb = _log_sigmoid(zz)
            l1m = lb - zz
            if causal is not None:
                l1m = jnp.where(causal, l1m, 0.0)
            a = jnp.exp(lb + _suffix_sum(l1m, upper) + c_l1m[rq, :])
            if causal is not None:
                a = jnp.where(causal, a, 0.0)
            a_b = a.astype(BF16)
            dvt_acc[:, rows] += _dot(dot_b[:, rq], a_b)
            gmat = a_b.astype(F32) * _dot_nt(do_t, v_ref[rows, :].astype(BF16))
            before = g_left[rq, :] - gmat - _suffix_sum(gmat, upper)
            sig = jnp.exp(lb)
            dz = gmat * (1.0 - sig) - sig * before
            if causal is not None:
                dz = jnp.where(causal, dz, 0.0)
            dz_b = dz.astype(BF16)
            dkt_acc[:, rows] += _dot(qt_b[:, rq], dz_b)
            dq_acc[rq, :] += _dot(dz_b, k_j)
            c_l1m[rq, :] += jnp.sum(l1m, axis=1, keepdims=True)
            g_left[rq, :] -= jnp.sum(gmat, axis=1, keepdims=True)

        _sb_sweep(step, tile)
        dq_ref[...] = (dq_acc[...] * scale).astype(BF16)

        @pl.when(step == nstep - 1)
        def _():
            dk_ref[...] = dkt_acc[...].T.astype(BF16)
            dv_ref[...] = dvt_acc[...].T.astype(BF16)

    rows_spec = pl.BlockSpec((SB_ROWS, DH_B), lambda h, i: (i, h))
    head_spec = pl.BlockSpec((S, DH_B), lambda h, i: (0, h))
    out_sds = jax.ShapeDtypeStruct((S, H_B * DH_B), BF16)
    return _pcall(
        comm, body, name=name, grid=(H_B, nstep),
        in_specs=[_sb_q_spec(), _sb_kv_spec(S, 1), _sb_kv_spec(S, 2), rows_spec, rows_spec],
        out_specs=[rows_spec, head_spec, head_spec],
        out_shape=[out_sds, out_sds, out_sds],
        scratch_shapes=[pltpu.VMEM((SB_ROWS, DH_B), F32)] + [pltpu.VMEM((DH_B, S), F32)] * 2
        + [pltpu.VMEM((SB_ROWS, DH_B), BF16)] * 2 + [pltpu.VMEM((DH_B, SB_ROWS), BF16)] * 2
        + [pltpu.VMEM((SB_ROWS, 1), F32)] * 2,
        compiler_params=_params("parallel", "arbitrary"),
    )(z, z, z, out, dout)


def _softmax(s):
    e = jnp.exp(s - jnp.max(s, axis=-1, keepdims=True))
    return e / jnp.sum(e, axis=-1, keepdims=True)


def xattn_fwd(qc, kv, name, comm=None):
    S, D = qc.shape
    tr = min(ROW_TILE, S)

    def body(q_ref, kv_ref, o_ref):
        for h in range(X_HEADS):
            cols = slice(h * X_DH, (h + 1) * X_DH)
            p = _softmax(_dot_nt(q_ref[:, cols], kv_ref[:, cols]))
            o_ref[:, cols] = _dot(p.astype(BF16), kv_ref[:, D + h * X_DH:D + (h + 1) * X_DH]).astype(BF16)

    return _pcall(
        comm, body, name=name, grid=(S // tr,),
        in_specs=[_row_spec(tr, D), pl.BlockSpec((N_MEM, 2 * D), lambda i: (0, 0))],
        out_specs=_row_spec(tr, D),
        out_shape=jax.ShapeDtypeStruct((S, D), BF16),
        compiler_params=_params("parallel"),
    )(qc, kv)


def xattn_bwd(qc, kv, do, name, comm=None):
    S, D = qc.shape
    tr = min(ROW_TILE, S)
    nstep = S // tr
    scale = X_DH ** -0.5

    def body(q_ref, kv_ref, do_ref, dq_ref, dkv_ref, acc):
        step = pl.program_id(0)

        @pl.when(step == 0)
        def _():
            acc[...] = _zeros(acc)

        for h in range(X_HEADS):
            cols = slice(h * X_DH, (h + 1) * X_DH)
            vcols = slice(D + h * X_DH, D + (h + 1) * X_DH)
            q = q_ref[:, cols]
            k = kv_ref[:, cols]
            do_h = do_ref[:, cols]
            p = _softmax(_dot_nt(q, k))
            dp = _dot_nt(do_h, kv_ref[:, vcols])
            acc[:, vcols] += _dot_tn(p.astype(BF16), do_h)
            ds = (p * (dp - jnp.sum(p * dp, axis=-1, keepdims=True))).astype(BF16)
            dq_ref[:, cols] = (_dot(ds, k) * scale).astype(BF16)
            acc[:, cols] += _dot_tn(ds, q)

        @pl.when(step == nstep - 1)
        def _():
            dkv_ref[...] = acc[...].astype(BF16)

    whole = pl.BlockSpec((N_MEM, 2 * D), lambda i: (0, 0))
    return _pcall(
        comm, body, name=name, grid=(nstep,),
        in_specs=[_row_spec(tr, D), whole, _row_spec(tr, D)],
        out_specs=[_row_spec(tr, D), whole],
        out_shape=[jax.ShapeDtypeStruct((S, D), BF16), jax.ShapeDtypeStruct((N_MEM, 2 * D), BF16)],
        scratch_shapes=[pltpu.VMEM((N_MEM, 2 * D), F32)],
        compiler_params=_params("arbitrary"),
    )(qc, kv, do)


def _row_tile(rows, cap=128):
    return max(t for t in range(16, cap + 1, 16) if rows % t == 0)


def cast_bf16(w, name, comm=None):
    R, C = w.shape
    tr = _row_tile(R, 256)

    def body(w_ref, o_ref):
        o_ref[...] = w_ref[...].astype(BF16)

    return _pcall(
        comm, body, name=name, grid=(R // tr,),
        in_specs=[_row_spec(tr, C)], out_specs=_row_spec(tr, C),
        out_shape=jax.ShapeDtypeStruct((R, C), BF16),
        compiler_params=_params("parallel"),
    )(w)


def adamw(parts, w, m, v, name, comm=None):
    R, C = w.shape
    n_parts = parts.shape[0]
    tr = _row_tile(R, 256)
    c1 = 1.0 - ADAM_B1 ** ADAM_STEP
    c2 = 1.0 - ADAM_B2 ** ADAM_STEP

    def body(p_ref, w_ref, m_ref, v_ref, g_ref, d_ref, mo_ref, vo_ref):
        g = p_ref[0].astype(F32)
        for p in range(1, n_parts):
            g = g + p_ref[p].astype(F32)
        m_new = ADAM_B1 * m_ref[...] + (1.0 - ADAM_B1) * g
        v_new = ADAM_B2 * v_ref[...] + (1.0 - ADAM_B2) * (g * g)
        g_ref[...] = g
        mo_ref[...] = m_new
        vo_ref[...] = v_new
        d_ref[...] = -ADAM_LR * ((m_new / c1) / (jnp.sqrt(v_new / c2) + ADAM_EPS) + ADAM_WD * w_ref[...])

    spec = _row_spec(tr, C)
    sds = jax.ShapeDtypeStruct((R, C), F32)
    return _pcall(
        comm, body, name=name, grid=(R // tr,),
        in_specs=[pl.BlockSpec((n_parts, tr, C), lambda i: (0, i, 0)), spec, spec, spec],
        out_specs=[spec, spec, spec, spec],
        out_shape=[sds, sds, sds, sds],
        compiler_params=_params("parallel"),
    )(parts, w, m, v)


def pair_sum(parts, from_sibling, core, name):
    _, R, C = parts.shape
    tr = _row_tile(R, 1024)

    def body(core_ref, p_ref, s_ref, o_ref):
        o_ref[...] = (p_ref[...].astype(F32) + s_ref[...].astype(F32)).astype(o_ref.dtype)

    return pl.pallas_call(
        body, name=name,
        grid_spec=pltpu.PrefetchScalarGridSpec(
            num_scalar_prefetch=1, grid=(4, R // tr),
            in_specs=[pl.BlockSpec((None, tr, C), lambda q, i, core_ref: (2 * q + core_ref[0], i, 0)),
                      pl.BlockSpec((None, tr, C), lambda q, i, core_ref: (q, i, 0))],
            out_specs=pl.BlockSpec((None, tr, C), lambda q, i, core_ref: (q, i, 0))),
        out_shape=jax.ShapeDtypeStruct((4, R, C), BF16),
        compiler_params=_params("parallel", "parallel"),
    )(core, parts, from_sibling)


def add2(a, b, name, comm=None):
    R, C = a.shape
    tr = _row_tile(R, 256)

    def body(a_ref, b_ref, o_ref):
        o_ref[...] = a_ref[...] + b_ref[...]

    spec = _row_spec(tr, C)
    return _pcall(
        comm, body, name=name, grid=(R // tr,), in_specs=[spec, spec], out_specs=spec,
        out_shape=jax.ShapeDtypeStruct((R, C), F32), compiler_params=_params("parallel"),
    )(a, b)


def _place():
    return lax.axis_index("x"), lax.axis_index("y"), lax.axis_index("c")


class Comm:
    def __init__(self, arrays, out_shapes, n_remote, n_local, start, finish, aliases=None):
        self.arrays, self.out_shapes = list(arrays), list(out_shapes)
        self.n_remote, self.n_local = n_remote, max(n_local, 1)
        self.start, self.finish = start, finish
        self.aliases = dict(aliases or {})
        self.sizes = [len(self.out_shapes)]

    def sem_shapes(self):
        return [pltpu.SemaphoreType.DMA((self.n_remote,)), pltpu.SemaphoreType.DMA((self.n_remote,)),
                pltpu.SemaphoreType.DMA((self.n_local,))]


class _Shifted:
    def __init__(self, ref, offset):
        self.ref, self.offset = ref, offset

    @property
    def at(self):
        return self

    def __getitem__(self, k):
        return self.ref.at[self.offset + k]


def merge_comms(comms):
    comms = [c for c in comms if c is not None]
    if not comms:
        return None

    def each(method):
        def run(ins, outs, sems):
            i = o = r = l = 0
            for c in comms:
                sub = (_Shifted(sems[0], r), _Shifted(sems[1], r), _Shifted(sems[2], l))
                getattr(c, method)(ins[i:i + len(c.arrays)], outs[o:o + len(c.out_shapes)], sub)
                i, o, r, l = i + len(c.arrays), o + len(c.out_shapes), r + c.n_remote, l + c.n_local
        return run

    aliases, i, o = {}, 0, 0
    for c in comms:
        aliases.update({i + a: o + b for a, b in c.aliases.items()})
        i, o = i + len(c.arrays), o + len(c.out_shapes)
    merged = Comm([a for c in comms for a in c.arrays], [s for c in comms for s in c.out_shapes],
                  sum(c.n_remote for c in comms), sum(c.n_local for c in comms), each("start"), each("finish"), aliases)
    merged.sizes = [len(c.out_shapes) for c in comms]
    return merged


def split_results(comm, results):
    out, i = [], 0
    for n in comm.sizes:
        out.append(list(results[i:i + n]))
        i += n
    return out


def run_comm(comm, name):
    n_in, n_out = len(comm.arrays), len(comm.out_shapes)

    def body(*refs):
        ins, outs, sems = refs[:n_in], refs[n_in:n_in + n_out], refs[n_in + n_out:]
        comm.start(ins, outs, sems)
        comm.finish(ins, outs, sems)

    return pl.pallas_call(
        body, name=name, in_specs=[ANY] * n_in, out_specs=[ANY] * n_out, out_shape=comm.out_shapes,
        scratch_shapes=comm.sem_shapes(), input_output_aliases=comm.aliases,
    )(*comm.arrays)


def _remote(src, dst, sems, k, to):
    return pltpu.make_async_remote_copy(src_ref=src, dst_ref=dst, send_sem=sems[0].at[k], recv_sem=sems[1].at[k],
                                        device_id=to, device_id_type=MESH)


_AG_COPIES = 13


def comm_all_gather(shards, rows=None, into=None):
    n = len(shards)
    row0, nrows = rows if rows is not None else (0, None)

    def parties():
        x, y, c = _place()
        return (x, y, c), (x, y, 1 - c), [(1 - x, y), (x, 1 - y), (1 - x, 1 - y)]

    def span(w, half=None):
        count = nrows if nrows is not None else shards[w].shape[0]
        if half is None:
            return pl.ds(row0, count)
        return pl.ds(row0 + half * (count // 2), count // 2)

    def slab(outs, w, dev, half=None):
        return outs[w].at[4 * dev[0] + 2 * dev[1] + dev[2], span(w, half)]

    def own(ins, outs, sems):
        me, sibling, chips = parties()
        local = [pltpu.make_async_copy(ins[w].at[span(w)], slab(outs, w, me), sems[2].at[w]) for w in range(n)]
        first = []
        for w in range(n):
            k = _AG_COPIES * w
            first.append(_remote(ins[w].at[span(w)], slab(outs, w, me), sems, k, sibling))
            first += [_remote(ins[w].at[span(w, h)], slab(outs, w, me, h), sems, k + 1 + 2 * j + h, (*chip, me[2]))
                      for h in range(2) for j, chip in enumerate(chips)]
        return local, first

    def start(ins, outs, sems):
        local, first = own(ins, outs, sems)
        for cp in local + first:
            cp.start()

    def finish(ins, outs, sems):
        me, sibling, chips = parties()
        local, first = own(ins, outs, sems)
        passed = []
        for w in range(n):
            k = _AG_COPIES * w
            for h in range(2):
                for j, chip in enumerate(chips):
                    got = slab(outs, w, (*chip, me[2]), h)
                    _remote(got, got, sems, k + 1 + 2 * j + h, me).wait_recv()
                    cp = _remote(got, got, sems, k + 7 + 2 * j + h, sibling)
                    cp.start()
                    passed.append(cp)
        for w in range(n):
            k = _AG_COPIES * w
            got = slab(outs, w, sibling)
            _remote(got, got, sems, k, me).wait_recv()
            for h in range(2):
                for j, chip in enumerate(chips):
                    got = slab(outs, w, (*chip, sibling[2]), h)
                    _remote(got, got, sems, k + 7 + 2 * j + h, me).wait_recv()
        for cp in first + passed:
            cp.wait_send()
        for cp in local:
            cp.wait()

    out_shapes = [jax.ShapeDtypeStruct((N_DEV,) + s.shape, s.dtype) for s in shards]
    arrays = list(shards) + (list(into) if into is not None else [])
    aliases = {n + w: w for w in range(n)} if into is not None else None
    return Comm(arrays, out_shapes, _AG_COPIES * n, n, start, finish, aliases)


def comm_pairs(items):
    slabbed = [a.ndim == 3 for a in items]
    first = [sum(4 if s else 1 for s in slabbed[:w]) for w in range(len(items))]

    def copies(ins, outs, sems):
        x, y, c = _place()
        sibling = (x, y, 1 - c)
        cps = []
        for w, s in enumerate(slabbed):
            if s:
                cps += [_remote(ins[w].at[2 * q + (1 - c)], outs[w].at[q], sems, first[w] + q, sibling) for q in range(4)]
            else:
                cps.append(_remote(ins[w], outs[w], sems, first[w], sibling))
        return cps

    def start(ins, outs, sems):
        for cp in copies(ins, outs, sems):
            cp.start()

    def finish(ins, outs, sems):
        for cp in copies(ins, outs, sems):
            cp.wait()

    out_shapes = [jax.ShapeDtypeStruct(((4,) + a.shape[1:]) if s else a.shape, a.dtype) for a, s in zip(items, slabbed)]
    return Comm(items, out_shapes, sum(4 if s else 1 for s in slabbed), 0, start, finish)


def comm_chips(items, rows=None, into=None, from_row=None, out_rows=None):
    n = len(items)
    slabbed = [a.ndim == 3 for a in items]

    def span(w, source=False):
        if rows is None:
            return pl.ds(0, items[w].shape[-2])
        return pl.ds(from_row if source and from_row is not None else rows[0], rows[1])

    def copies(ins, outs, sems):
        x, y, c = _place()
        mine = 2 * x + y
        local = [pltpu.make_async_copy(ins[w].at[mine, span(w, True)] if slabbed[w] else ins[w].at[span(w, True)],
                                       outs[w].at[mine, span(w)], sems[2].at[w]) for w in range(n)]
        remote = []
        for w in range(n):
            for j, (px, py) in enumerate([(1 - x, y), (x, 1 - y), (1 - x, 1 - y)]):
                src = ins[w].at[2 * px + py, span(w, True)] if slabbed[w] else ins[w].at[span(w, True)]
                remote.append(_remote(src, outs[w].at[mine, span(w)], sems, 3 * w + j, (px, py, c)))
        return local, remote

    def start(ins, outs, sems):
        local, remote = copies(ins, outs, sems)
        for cp in local + remote:
            cp.start()

    def finish(ins, outs, sems):
        local, remote = copies(ins, outs, sems)
        for cp in remote + local:
            cp.wait()

    def result(a):
        tall = a.shape[:-2] + (out_rows if out_rows is not None else a.shape[-2], a.shape[-1])
        return jax.ShapeDtypeStruct(tall if a.ndim == 3 else (4,) + tall, a.dtype)

    if into is None:
        return Comm(items, [result(a) for a in items], 3 * n, n, start, finish)
    out_shapes = [jax.ShapeDtypeStruct(b.shape, b.dtype) for b in into]
    return Comm(list(items) + list(into), out_shapes, 3 * n, n, start, finish, {n + w: w for w in range(n)})


_SMALL = ("ffn1_norm", "mix_norm", "ln_v_gain", "ln_v_bias", "spatial_w", "spatial_b", "gnorm_a", "gnorm_b",
          "cross_norm", "mem_norm", "ffn2_norm", "final_norm")
_BIG = ("ffn1_w_in", "ffn1_w_out", "w_mix_in", "w_mix_out", "w_cq", "w_ckv", "w_co", "ffn2_w_in", "ffn2_w_out")
_COL_SHARDED = ("ffn1_w_in", "w_mix_in", "w_ckv", "ffn2_w_in")
_ORDER = ("ffn1_norm", "ffn1_w_in", "ffn1_w_out", "mix_norm", "w_mix_in", "ln_v_gain", "ln_v_bias", "spatial_w",
          "spatial_b", "gnorm_a", "gnorm_b", "w_mix_out", "cross_norm", "mem_norm", "w_cq", "w_ckv", "w_co",
          "ffn2_norm", "ffn2_w_in", "ffn2_w_out", "final_norm")


_SMALL_PAD = 120


def _rows128(a):
    return a.reshape(-1, 128)


def kernel(x, mem, ffn1_norm, ffn1_w_in, ffn1_w_out, mix_norm, w_mix_in, ln_v_gain, ln_v_bias, spatial_w, spatial_b, gnorm_a, gnorm_b, w_mix_out, cross_norm, mem_norm, w_cq, w_ckv, w_co, ffn2_norm, ffn2_w_in, ffn2_w_out, final_norm, loss_target, m_ffn1_norm, m_ffn1_w_in, m_ffn1_w_out, m_mix_norm, m_w_mix_in, m_ln_v_gain, m_ln_v_bias, m_spatial_w, m_spatial_b, m_gnorm_a, m_gnorm_b, m_w_mix_out, m_cross_norm, m_mem_norm, m_w_cq, m_w_ckv, m_w_co, m_ffn2_norm, m_ffn2_w_in, m_ffn2_w_out, m_final_norm, v_ffn1_norm, v_ffn1_w_in, v_ffn1_w_out, v_mix_norm, v_w_mix_in, v_ln_v_gain, v_ln_v_bias, v_spatial_w, v_spatial_b, v_gnorm_a, v_gnorm_b, v_w_mix_out, v_cross_norm, v_mem_norm, v_w_cq, v_w_ckv, v_w_co, v_ffn2_norm, v_ffn2_w_in, v_ffn2_w_out, v_final_norm):
    given = dict(locals())
    wts = {k: given[k] for k in _ORDER}
    mom = {k: given["m_" + k] for k in _ORDER}
    var = {k: given["v_" + k] for k in _ORDER}

    D = D_MODEL
    xs = x.reshape(-1, D)
    mems = mem.reshape(-1, D)
    tgt = loss_target.reshape(-1, D)
    vec = lambda a: a.reshape(1, -1)
    g1, gmix, gcross, gmem, g2, gfin = (vec(wts[k]) for k in
                                        ("ffn1_norm", "mix_norm", "cross_norm", "mem_norm", "ffn2_norm", "final_norm"))
    ln_g, ln_b, ga, gb = (vec(wts[k]) for k in ("ln_v_gain", "ln_v_bias", "gnorm_a", "gnorm_b"))
    w_s = spatial_w.reshape(G_A, SGU_BLOCK, SGU_BLOCK)
    b_t = spatial_b.reshape(G_A, SGU_BLOCK).T

    shard2d = {k: wts[k].reshape(wts[k].shape[1:]) for k in _BIG}
    shard_b = {k: cast_bf16(shard2d[k], f"cast_{k}") for k in _BIG}
    full = {}

    landing, rows_done = {}, {}

    def gathering(pieces, fn, *args, **kw):
        pieces = [p if isinstance(p, tuple) else (p, None) for p in pieces]
        comm = merge_comms([comm_all_gather([shard_b[k]], rows, [landing[k]] if k in landing else None)
                            for k, rows in pieces])
        out, got = fn(*args, comm=comm, **kw)
        for (k, rows), (g,) in zip(pieces, split_results(comm, got)):
            landing[k] = g
            rows_done[k] = rows_done.get(k, 0) + (rows[1] if rows is not None else shard_b[k].shape[0])
            if rows_done[k] == shard_b[k].shape[0]:
                full[k] = g if k in _COL_SHARDED else g.reshape(-1, g.shape[2])
        return out

    n1 = gathering(("ffn1_w_in",), rms_fwd, xs, g1, "f_n1")
    a1, hsw1 = gathering(("ffn1_w_out",), mm_swiglu_g, n1, full["ffn1_w_in"], "f_a1")
    h1 = gathering(("w_mix_in",), mm_nn, hsw1, full["ffn1_w_out"], F32, "f_h1", scale=0.5, res=xs)
    n2 = gathering(("w_cq",), rms_fwd, h1, gmix, "f_n2")
    z = gathering(("w_mix_out", "w_co"), mm_nn_g, n2, full["w_mix_in"], F32, "f_z")
    ya = gathering((("w_ckv", (0, 512)),), sgu_fwd, z, ln_g, ln_b, w_s, b_t, "f_sgu")
    yb = gathering((("w_ckv", (512, 1536)), ("ffn2_w_in", (0, 512))), sb_fwd, z, "f_sb")
    ycat = gathering((("ffn2_w_in", (512, 128)),), rmscat_fwd, ya, yb, ga, gb, "f_ycat")
    h2 = gathering((("ffn2_w_in", (640, 256)),), mm_nn, ycat, full["w_mix_out"], F32, "f_h2", res=h1)
    n3 = gathering((("ffn2_w_in", (896, 128)),), rms_fwd, h2, gcross, "f_n3")
    memn = rms_fwd(mems, gmem, "f_memn")
    qc = gathering((("ffn2_w_in", (1024, 256)),), mm_nn, n3, full["w_cq"], BF16, "f_qc", scale=X_DH ** -0.5)
    kv = gathering((("ffn2_w_in", (1280, 128)),), mm_nn_g, memn, full["w_ckv"], BF16, "f_kv")
    o = gathering((("ffn2_w_in", (1408, 128)),), xattn_fwd, qc, kv, "f_xattn")
    h3 = gathering((("ffn2_w_in", (1536, 256)),), mm_nn, o, full["w_co"], F32, "f_h3", res=h2)
    n4 = gathering((("ffn2_w_in", (1792, 256)),), rms_fwd, h3, g2, "f_n4")
    a2, hsw2 = gathering(("ffn2_w_out",), mm_swiglu_g, n4, full["ffn2_w_in"], "f_a2")
    h4 = mm_nn(hsw2, full["ffn2_w_out"], F32, "f_h4", scale=0.5, res=h3)

    grads, parts, sums, recv = {}, {}, {}, {}
    core = lax.axis_index("c").astype(jnp.int32).reshape(1)

    def partial_of(k, g):
        grads[k] = g
        parts[k] = g if g.ndim == 3 else g.reshape(N_DEV, -1, g.shape[1])

    def reducing(pairs, chips, fn, *args, **kw):
        def piece(p):
            if isinstance(p, dict):
                return p
            k, rows = p if isinstance(p, tuple) else (p, None)
            return dict(sums=k, rows=rows, to=k)

        chips = [piece(p) for p in chips]
        comms = [comm_pairs([parts[k] for k in pairs])] if pairs else []
        comms += [comm_chips([sums[p["sums"]]], p["rows"], [recv[p["to"]]] if p["to"] in recv else None,
                             p.get("from_row"), p.get("out_rows")) for p in chips]
        comm = merge_comms(comms)
        out, got = fn(*args, comm=comm, **kw)
        got = split_results(comm, got)
        if pairs:
            for k, r in zip(pairs, got.pop(0)):
                sums[k] = pair_sum(parts[k], r, core, f"pair_sum_{k}")
        for p, (r,) in zip(chips, got):
            recv[p["to"]] = r
        return out

    loss_part, dh4, df2, grads["final_norm"] = loss_head(h4, tgt, gfin, "loss_head")
    partial_of("ffn2_w_out", mm_tn(hsw2, df2, "b_ffn2_dwout"))
    da2 = reducing(("ffn2_w_out",), (), mm_swiglu_bwd, df2, full["ffn2_w_out"], a2, "b_ffn2_da")
    partial_of("ffn2_w_in", reducing((), ("ffn2_w_out",), mm_tn_g, n4, da2, N_DEV, "b_ffn2_dwin"))
    dn4 = reducing(("ffn2_w_in",), (), mm_nt_g, da2, full["ffn2_w_in"], "b_ffn2_dn")
    dh3, dh3b, grads["ffn2_norm"] = rms_bwd(dn4, h3, g2, dh4, 1.0, "b_n4")

    partial_of("w_co", mm_tn(o, dh3b, "b_dwco"))
    do = reducing(("w_co",), (), mm_nt, dh3b, full["w_co"], BF16, "b_do")
    dqp, dkv = xattn_bwd(qc, kv, do, "b_xattn")
    partial_of("w_cq", mm_tn(n3, dqp, "b_dwcq"))
    dn3 = reducing(("w_cq",), (), mm_nt, dqp, full["w_cq"], F32, "b_dn3")
    partial_of("w_ckv", mm_tn_g(memn, dkv, N_DEV, "b_dwckv"))
    dmemn = reducing(("w_ckv",), (), mm_nt_g, dkv, full["w_ckv"], "b_dmemn")
    _, _, grads["mem_norm"] = rms_bwd(dmemn, mems, gmem, None, 1.0, "b_memn")
    dh2, dh2b, grads["cross_norm"] = rms_bwd(dn3, h2, gcross, dh3, 1.0, "b_n3")

    partial_of("w_mix_out", mm_tn(ycat, dh2b, "b_dwmixout"))
    dycat = reducing(("w_mix_out",), (), mm_nt, dh2b, full["w_mix_out"], F32, "b_dycat")
    dya, dyb, grads["gnorm_a"], grads["gnorm_b"] = rmscat_bwd(dycat, ya, yb, ga, gb, "b_ycat")
    dza, grads["ln_v_gain"], grads["ln_v_bias"], grads["spatial_w"], grads["spatial_b"] = sgu_bwd(
        z, dya, ln_g, ln_b, w_s, b_t, "b_sgu")
    dq, dk, dv = reducing((), ("ffn2_w_in", "w_co", "w_cq"), sb_bwd, z, yb, dyb, "b_sb")
    dz = jnp.concatenate([dza, dq, dk, dv], axis=1)
    partial_of("w_mix_in", reducing((), ("w_ckv",), mm_tn_g, n2, dz, N_DEV, "b_dwmixin"))
    dn2 = reducing(("w_mix_in",), ("w_mix_out",), mm_nt_g, dz, full["w_mix_in"], "b_dn2")
    dh1, dh1b, grads["mix_norm"] = reducing((), (("w_mix_in", (0, 512)),), rms_bwd, dn2, h1, gmix, dh2, 0.5, "b_n2")

    partial_of("ffn1_w_out", reducing((), (("w_mix_in", (512, 1024)),), mm_tn, hsw1, dh1b, "b_ffn1_dwout"))
    da1 = reducing(("ffn1_w_out",), (("w_mix_in", (1536, 512)),), mm_swiglu_bwd, dh1b, full["ffn1_w_out"], a1,
                   "b_ffn1_da")
    half = D // 2
    partial_of("ffn1_w_in_a", reducing((), (("ffn1_w_out", (0, 352)),), mm_tn_g, n1, da1, N_DEV, "b_ffn1_dwin_a",
                                       cols=(0, half)))
    partial_of("ffn1_w_in_b", reducing(("ffn1_w_in_a",), (("ffn1_w_out", (352, 352)),), mm_tn_g, n1, da1, N_DEV,
                                       "b_ffn1_dwin_b", cols=(half, half)))
    dn1 = reducing(("ffn1_w_in_b",), (dict(sums="ffn1_w_in_a", rows=(0, half), to="ffn1_w_in", out_rows=D),),
                   mm_nt_g, da1, full["ffn1_w_in"], "b_ffn1_dn")
    (dx, _, grads["ffn1_norm"]) = reducing((), (dict(sums="ffn1_w_in_b", rows=(half, half), from_row=0, to="ffn1_w_in"),),
                                           rms_bwd, dn1, xs, g1, dh1, 1.0, "b_n1")

    out_g, out_d, out_m, out_v = {}, {}, {}, {}
    for k in _BIG:
        res = adamw(recv[k], shard2d[k], mom[k].reshape(shard2d[k].shape), var[k].reshape(shard2d[k].shape), f"adamw_{k}")
        out_g[k], out_d[k], out_m[k], out_v[k] = (t.reshape(wts[k].shape) for t in res)

    pack = lambda d: jnp.concatenate([_rows128(d[k]) for k in _SMALL] + [jnp.zeros((_SMALL_PAD, 128), F32)], axis=0)
    small_part = pack(grads)
    (small_sibling,) = run_comm(comm_pairs([small_part]), "comm_pairs_small")
    small_pair = add2(small_part, small_sibling, "pair_sum_small")
    (small_all,) = run_comm(comm_chips([small_pair]), "comm_chips_small")

    res = adamw(small_all, pack(wts), pack(mom), pack(var), "adamw_small")
    row = 0
    for k in _SMALL:
        nrow = wts[k].size // 128
        for dst, t in zip((out_g, out_d, out_m, out_v), res):
            dst[k] = t[row:row + nrow].reshape(wts[k].shape)
        row += nrow

    loss = lax.psum(loss_part[0, 0], ("x", "y", "c"))
    grad_x = dx.reshape(x.shape)
    return (loss, grad_x, *[out_g[k] for k in _ORDER], *[out_d[k] for k in _ORDER],
            *[out_m[k] for k in _ORDER], *[out_v[k] for k in _ORDER])
```

```python
import functools
import math

import jax
import jax.numpy as jnp
from jax import lax
from jax.experimental import pallas as pl
from jax.experimental.pallas import tpu as pltpu

F32 = jnp.float32
BF16 = jnp.bfloat16

N_DEV = 8
D_MODEL = 2048
D_FF = 5632
W_A = 1024
G_A = 8
GA_DIM = 128
SGU_BLOCK = 128
CHUNK = 64
H_B = 8
DH_B = 128
Q_BLOCK = 128
X_HEADS = 4
X_DH = 512
N_MEM = 256
EPS = 1e-6

ADAM_LR = 0.001
ADAM_B1 = 0.9
ADAM_B2 = 0.999
ADAM_EPS = 1e-08
ADAM_WD = 0.01
ADAM_STEP = 10

VMEM_LIMIT = 56 * 2**20
ROW_TILE = 256

MESH = pl.DeviceIdType.MESH
ANY = pl.BlockSpec(memory_space=pl.ANY)

_NT = (((1,), (1,)), ((), ()))
_TN = (((0,), (0,)), ((), ()))


def _params(*sem):
    return pltpu.CompilerParams(dimension_semantics=sem, vmem_limit_bytes=VMEM_LIMIT)


def _zeros(ref):
    return jnp.zeros(ref.shape, ref.dtype)


def _pcall(comm, body, *, name, grid, in_specs, out_specs, out_shape, compiler_params, scratch_shapes=()):
    if comm is None:
        return pl.pallas_call(body, name=name, grid=grid, in_specs=in_specs, out_specs=out_specs, out_shape=out_shape,
                              scratch_shapes=list(scratch_shapes), compiler_params=compiler_params)
    multi = isinstance(out_shape, (list, tuple))
    out_shapes = list(out_shape) if multi else [out_shape]
    out_specs_l = list(out_specs) if multi else [out_specs]
    n_in, n_out, n_scr = len(in_specs), len(out_shapes), len(scratch_shapes)
    n_cin, n_cout = len(comm.arrays), len(comm.out_shapes)

    def with_comm(*refs):
        ins, refs = refs[:n_in], refs[n_in:]
        cins, refs = refs[:n_cin], refs[n_cin:]
        outs, refs = refs[:n_out], refs[n_out:]
        couts, refs = refs[:n_cout], refs[n_cout:]
        scr, sems = refs[:n_scr], refs[n_scr:]
        first = functools.reduce(jnp.logical_and, [pl.program_id(a) == 0 for a in range(len(grid))])
        last = functools.reduce(jnp.logical_and, [pl.program_id(a) == grid[a] - 1 for a in range(len(grid))])
        pl.when(first)(lambda: comm.start(cins, couts, sems))
        body(*ins, *outs, *scr)
        pl.when(last)(lambda: comm.finish(cins, couts, sems))

    call = pl.pallas_call(
        with_comm, name=name, grid=grid, in_specs=list(in_specs) + [ANY] * n_cin,
        out_specs=out_specs_l + [ANY] * n_cout, out_shape=out_shapes + comm.out_shapes,
        scratch_shapes=list(scratch_shapes) + comm.sem_shapes(), compiler_params=_params(*(("arbitrary",) * len(grid))),
        input_output_aliases={n_in + i: n_out + j for i, j in comm.aliases.items()})

    def run(*args):
        res = call(*args, *comm.arrays)
        main = res[:n_out]
        return (list(main) if multi else main[0]), list(res[n_out:])

    return run


def _dot(a, b):
    return jnp.dot(a, b, preferred_element_type=F32)


def _dot_nt(a, b):
    return lax.dot_general(a, b, _NT, preferred_element_type=F32)


def _dot_tn(a, b):
    return lax.dot_general(a, b, _TN, preferred_element_type=F32)


def mm_nn_g(a, bg, out_dtype, name, tm=512, comm=None):
    M, K = a.shape
    G, _, n = bg.shape
    tm = min(tm, M)

    def body(a_ref, b_ref, o_ref):
        o_ref[...] = _dot(a_ref[...], b_ref[...]).astype(o_ref.dtype)

    return _pcall(
        comm, body, name=name, grid=(G, M // tm),
        in_specs=[pl.BlockSpec((tm, K), lambda g, m: (m, 0)),
                  pl.BlockSpec((None, K, n), lambda g, m: (g, 0, 0))],
        out_specs=pl.BlockSpec((tm, n), lambda g, m: (m, g)),
        out_shape=jax.ShapeDtypeStruct((M, G * n), out_dtype),
        compiler_params=_params("parallel", "parallel"),
    )(a, bg)


def mm_swiglu_g(a, bg, name, tm=512, comm=None):
    M, K = a.shape
    G, _, n = bg.shape
    half = G // 2
    tm = min(tm, M)

    def body(a_ref, bgate_ref, bup_ref, gu_ref, h_ref):
        av = a_ref[...]
        gate = _dot(av, bgate_ref[...])
        up = _dot(av, bup_ref[...])
        gu_ref[0] = gate.astype(BF16)
        gu_ref[1] = up.astype(BF16)
        h_ref[...] = (gate * _sigmoid(gate) * up).astype(BF16)

    return _pcall(
        comm, body, name=name, grid=(half, M // tm),
        in_specs=[pl.BlockSpec((tm, K), lambda p, m: (m, 0)),
                  pl.BlockSpec((None, K, n), lambda p, m: (p, 0, 0)),
                  pl.BlockSpec((None, K, n), lambda p, m: (p + half, 0, 0))],
        out_specs=[pl.BlockSpec((2, tm, n), lambda p, m: (0, m, p)), pl.BlockSpec((tm, n), lambda p, m: (m, p))],
        out_shape=[jax.ShapeDtypeStruct((2, M, half * n), BF16), jax.ShapeDtypeStruct((M, half * n), BF16)],
        compiler_params=_params("parallel", "parallel"),
    )(a, bg, bg)


def mm_swiglu_bwd(dy, w_out, gate_up, name, tk=512, comm=None):
    M, N = dy.shape
    F = w_out.shape[0]

    def body(dy_ref, w_ref, gu_ref, o_ref):
        dh = _dot_nt(dy_ref[...], w_ref[...])
        gt = gu_ref[0].astype(F32)
        up = gu_ref[1].astype(F32)
        sg = _sigmoid(gt)
        o_ref[0] = (dh * up * (sg * (1.0 + gt * (1.0 - sg)))).astype(BF16)
        o_ref[1] = (dh * (gt * sg)).astype(BF16)

    planes = pl.BlockSpec((2, M, tk), lambda k: (0, 0, k))
    return _pcall(
        comm, body, name=name, grid=(F // tk,),
        in_specs=[pl.BlockSpec((M, N), lambda k: (0, 0)), pl.BlockSpec((tk, N), lambda k: (k, 0)), planes],
        out_specs=planes,
        out_shape=jax.ShapeDtypeStruct((2, M, F), BF16),
        compiler_params=_params("parallel"),
    )(dy, w_out, gate_up)


def _shard_cols_spec(dy, G, rows, index):
    if dy.ndim == 2:
        n = dy.shape[1] // G
        return pl.BlockSpec((rows, n), lambda *ids: index(*ids)), n
    half = G // 2
    n = dy.shape[2] // half

    def planes(*ids):
        r, g = index(*ids)
        return g // half, r, g % half

    return pl.BlockSpec((None, rows, n), planes), n


def mm_nn(a, b, out_dtype, name, tm=512, tn=1024, scale=1.0, res=None, comm=None):
    M, K = a.shape
    _, N = b.shape
    tm, tn = min(tm, M), min(tn, N)

    def body(*refs):
        if res is None:
            a_ref, b_ref, o_ref = refs
            acc = _dot(a_ref[...], b_ref[...])
            o_ref[...] = (acc * scale if scale != 1.0 else acc).astype(o_ref.dtype)
        else:
            a_ref, b_ref, r_ref, o_ref = refs
            o_ref[...] = (r_ref[...] + scale * _dot(a_ref[...], b_ref[...])).astype(o_ref.dtype)

    in_specs = [pl.BlockSpec((tm, K), lambda n, m: (m, 0)),
                pl.BlockSpec((K, tn), lambda n, m: (0, n))]
    args = [a, b]
    if res is not None:
        in_specs.append(pl.BlockSpec((tm, tn), lambda n, m: (m, n)))
        args.append(res)
    return _pcall(
        comm, body, name=name, grid=(N // tn, M // tm),
        in_specs=in_specs,
        out_specs=pl.BlockSpec((tm, tn), lambda n, m: (m, n)),
        out_shape=jax.ShapeDtypeStruct((M, N), out_dtype),
        compiler_params=_params("parallel", "parallel"),
    )(*args)


def mm_nt_g(dy, bg, name, tm=512, comm=None):
    M = dy.shape[-2]
    G, K, n = bg.shape
    tm = min(tm, M)
    dy_spec, _ = _shard_cols_spec(dy, G, tm, lambda m, g: (m, g))

    def body(dy_ref, b_ref, o_ref):
        part = _dot_nt(dy_ref[...], b_ref[...])

        @pl.when(pl.program_id(1) == 0)
        def _():
            o_ref[...] = part

        @pl.when(pl.program_id(1) > 0)
        def _():
            o_ref[...] += part

    return _pcall(
        comm, body, name=name, grid=(M // tm, G),
        in_specs=[dy_spec, pl.BlockSpec((None, K, n), lambda m, g: (g, 0, 0))],
        out_specs=pl.BlockSpec((tm, K), lambda m, g: (m, 0)),
        out_shape=jax.ShapeDtypeStruct((M, K), F32),
        compiler_params=_params("parallel", "arbitrary"),
    )(dy, bg)


def mm_nt(dy, b, out_dtype, name, tk=512, comm=None):
    M, N = dy.shape
    K, _ = b.shape

    def body(dy_ref, b_ref, o_ref):
        o_ref[...] = _dot_nt(dy_ref[...], b_ref[...]).astype(o_ref.dtype)

    return _pcall(
        comm, body, name=name, grid=(K // tk,),
        in_specs=[pl.BlockSpec((M, N), lambda k: (0, 0)),
                  pl.BlockSpec((tk, N), lambda k: (k, 0))],
        out_specs=pl.BlockSpec((M, tk), lambda k: (0, k)),
        out_shape=jax.ShapeDtypeStruct((M, K), out_dtype),
        compiler_params=_params("parallel"),
    )(dy, b)


def mm_tn_g(x, dy, G, name, tk=512, cols=None, comm=None):
    M, K = x.shape
    k0, K = cols if cols is not None else (0, K)
    dy_spec, n = _shard_cols_spec(dy, G, M, lambda g, k: (0, g))

    def body(x_ref, dy_ref, o_ref):
        o_ref[...] = _dot_tn(x_ref[...], dy_ref[...]).astype(o_ref.dtype)

    return _pcall(
        comm, body, name=name, grid=(G, K // tk),
        in_specs=[pl.BlockSpec((M, tk), lambda g, k: (0, k + k0 // tk)), dy_spec],
        out_specs=pl.BlockSpec((None, tk, n), lambda g, k: (g, k, 0)),
        out_shape=jax.ShapeDtypeStruct((G, K, n), BF16),
        compiler_params=_params("parallel", "parallel"),
    )(x, dy)


def mm_tn(x, dy, name, tk=512, comm=None):
    M, K = x.shape
    _, N = dy.shape

    def body(x_ref, dy_ref, o_ref):
        o_ref[...] = _dot_tn(x_ref[...], dy_ref[...]).astype(o_ref.dtype)

    return _pcall(
        comm, body, name=name, grid=(K // tk,),
        in_specs=[pl.BlockSpec((M, tk), lambda k: (0, k)),
                  pl.BlockSpec((M, N), lambda k: (0, 0))],
        out_specs=pl.BlockSpec((tk, N), lambda k: (k, 0)),
        out_shape=jax.ShapeDtypeStruct((K, N), BF16),
        compiler_params=_params("parallel"),
    )(x, dy)


def _rstd(x):
    return lax.rsqrt(jnp.mean(x * x, axis=-1, keepdims=True) + EPS)


def _rms_bwd(dn, xhat, r, g):
    dxhat = dn * g
    return r * (dxhat - xhat * jnp.mean(dxhat * xhat, axis=-1, keepdims=True))


def _row_spec(tr, width, col=0):
    return pl.BlockSpec((tr, width), lambda i: (i, col))


def _vec_spec(width):
    return pl.BlockSpec((1, width), lambda i: (0, 0))


def _heads_spec(tr):
    return pl.BlockSpec((H_B, tr, DH_B), lambda i: (0, i, 0))


def _heads_to_cols(ref):
    return jnp.concatenate([ref[h] for h in range(H_B)], axis=1)


def split_heads(z, name, comm=None):
    S = z.shape[0]
    tr = min(ROW_TILE, S)
    width = H_B * DH_B
    first = 2 * W_A // width

    def body(q_ref, k_ref, v_ref, o_ref):
        for p, (ref, scale) in enumerate(((q_ref, DH_B ** -0.5), (k_ref, 1.0), (v_ref, 1.0))):
            for h in range(H_B):
                cols = ref[:, h * DH_B:(h + 1) * DH_B]
                o_ref[p * H_B + h] = (cols * scale if scale != 1.0 else cols).astype(BF16)

    return _pcall(
        comm, body, name=name, grid=(S // tr,),
        in_specs=[_row_spec(tr, width, first), _row_spec(tr, width, first + 1), _row_spec(tr, width, first + 2)],
        out_specs=pl.BlockSpec((3 * H_B, tr, DH_B), lambda i: (0, i, 0)),
        out_shape=jax.ShapeDtypeStruct((3 * H_B, S, DH_B), BF16),
        compiler_params=_params("parallel"),
    )(z, z, z)


def join_dz(dza, dq, dk, dv, name, comm=None):
    S, wa = dza.shape
    tr = min(ROW_TILE, S)
    width = H_B * DH_B

    def body(dza_ref, dq_ref, dk_ref, dv_ref, o_ref):
        o_ref[:, :wa] = dza_ref[...]
        for p, ref in enumerate((dq_ref, dk_ref, dv_ref)):
            for h in range(H_B):
                lo = wa + p * width + h * DH_B
                o_ref[:, lo:lo + DH_B] = ref[h]

    return _pcall(
        comm, body, name=name, grid=(S // tr,),
        in_specs=[_row_spec(tr, wa), _heads_spec(tr), _heads_spec(tr), _heads_spec(tr)],
        out_specs=_row_spec(tr, wa + 3 * width),
        out_shape=jax.ShapeDtypeStruct((S, wa + 3 * width), BF16),
        compiler_params=_params("parallel"),
    )(dza, dq, dk, dv)


def rms_fwd(x, g, name, comm=None):
    M, D = x.shape
    tr = min(ROW_TILE, M)

    def body(x_ref, g_ref, o_ref):
        xv = x_ref[...]
        o_ref[...] = (xv * _rstd(xv) * g_ref[...]).astype(o_ref.dtype)

    return _pcall(
        comm, body, name=name, grid=(M // tr,),
        in_specs=[_row_spec(tr, D), _vec_spec(D)],
        out_specs=_row_spec(tr, D),
        out_shape=jax.ShapeDtypeStruct((M, D), BF16),
        compiler_params=_params("parallel"),
    )(x, g)


def rms_bwd(dn, h, g, dres, copy_scale, name, comm=None):
    M, D = h.shape
    tr = min(ROW_TILE, M)
    has_res = dres is not None

    def body(*refs):
        if has_res:
            dn_ref, h_ref, g_ref, dres_ref, dh_ref, dhb_ref, dg_ref = refs
        else:
            dn_ref, h_ref, g_ref, dh_ref, dhb_ref, dg_ref = refs
        hv = h_ref[...]
        r = _rstd(hv)
        xhat = hv * r
        dn = dn_ref[...]
        part = jnp.sum(dn * xhat, axis=0, keepdims=True)

        @pl.when(pl.program_id(0) == 0)
        def _():
            dg_ref[...] = part

        @pl.when(pl.program_id(0) > 0)
        def _():
            dg_ref[...] += part

        dh = _rms_bwd(dn, xhat, r, g_ref[...])
        if has_res:
            dh = dh + dres_ref[...]
        dh_ref[...] = dh
        dhb_ref[...] = (dh * copy_scale if copy_scale != 1.0 else dh).astype(BF16)

    in_specs = [_row_spec(tr, D), _row_spec(tr, D), _vec_spec(D)]
    args = [dn, h, g]
    if has_res:
        in_specs.append(_row_spec(tr, D))
        args.append(dres)
    return _pcall(
        comm, body, name=name, grid=(M // tr,),
        in_specs=in_specs,
        out_specs=[_row_spec(tr, D), _row_spec(tr, D), _vec_spec(D)],
        out_shape=[jax.ShapeDtypeStruct((M, D), F32), jax.ShapeDtypeStruct((M, D), BF16),
                   jax.ShapeDtypeStruct((1, D), F32)],
        compiler_params=_params("arbitrary"),
    )(*args)


def _sigmoid(x):
    return 1.0 / (1.0 + jnp.exp(-x))


def rmscat_fwd(ya, yb, ga, gb, name, comm=None):
    M, W = ya.shape
    tr = min(ROW_TILE, M)

    def body(ya_ref, yb_ref, ga_ref, gb_ref, o_ref):
        a = ya_ref[...]
        b = _heads_to_cols(yb_ref)
        o_ref[:, :W] = (a * _rstd(a) * ga_ref[...]).astype(BF16)
        o_ref[:, W:] = (b * _rstd(b) * gb_ref[...]).astype(BF16)

    return _pcall(
        comm, body, name=name, grid=(M // tr,),
        in_specs=[_row_spec(tr, W), _heads_spec(tr), _vec_spec(W), _vec_spec(W)],
        out_specs=_row_spec(tr, 2 * W),
        out_shape=jax.ShapeDtypeStruct((M, 2 * W), BF16),
        compiler_params=_params("parallel"),
    )(ya, yb, ga, gb)


def rmscat_bwd(dycat, ya, yb, ga, gb, name, comm=None):
    M, W = ya.shape
    tr = min(ROW_TILE, M)

    def body(dc_ref, ya_ref, yb_ref, ga_ref, gb_ref, dya_ref, dyb_ref, dga_ref, dgb_ref):
        first = pl.program_id(0) == 0
        for by_head, y_ref, g_ref, dy_ref, dg_ref, lo in ((False, ya_ref, ga_ref, dya_ref, dga_ref, 0),
                                                          (True, yb_ref, gb_ref, dyb_ref, dgb_ref, W)):
            yv = _heads_to_cols(y_ref) if by_head else y_ref[...]
            r = _rstd(yv)
            xhat = yv * r
            dn = dc_ref[:, lo:lo + W]
            part = jnp.sum(dn * xhat, axis=0, keepdims=True)

            @pl.when(first)
            def _():
                dg_ref[...] = part

            @pl.when(jnp.logical_not(first))
            def _():
                dg_ref[...] += part

            dy = _rms_bwd(dn, xhat, r, g_ref[...])
            if by_head:
                for h in range(H_B):
                    dy_ref[h] = dy[:, h * DH_B:(h + 1) * DH_B]
            else:
                dy_ref[...] = dy

    return _pcall(
        comm, body, name=name, grid=(M // tr,),
        in_specs=[_row_spec(tr, 2 * W), _row_spec(tr, W), _heads_spec(tr), _vec_spec(W), _vec_spec(W)],
        out_specs=[_row_spec(tr, W), _heads_spec(tr), _vec_spec(W), _vec_spec(W)],
        out_shape=[jax.ShapeDtypeStruct((M, W), F32), jax.ShapeDtypeStruct((H_B, M, DH_B), F32),
                   jax.ShapeDtypeStruct((1, W), F32), jax.ShapeDtypeStruct((1, W), F32)],
        compiler_params=_params("arbitrary"),
    )(dycat, ya, yb, ga, gb)


def loss_head(h, target, g, name, comm=None):
    M, D = h.shape
    tr = min(ROW_TILE, M)

    def body(h_ref, t_ref, g_ref, loss_ref, dh_ref, dhb_ref, dg_ref):
        hv = h_ref[...]
        gv = g_ref[...]
        r = _rstd(hv)
        xhat = hv * r
        err = xhat * gv - t_ref[...]
        lsum = jnp.sum(jnp.sum(err * err, axis=1, keepdims=True), axis=0, keepdims=True) * (0.5 / D)
        dy = err * (1.0 / D)
        part = jnp.sum(dy * xhat, axis=0, keepdims=True)

        @pl.when(pl.program_id(0) == 0)
        def _():
            dg_ref[...] = part
            loss_ref[...] = _zeros(loss_ref) + lsum

        @pl.when(pl.program_id(0) > 0)
        def _():
            dg_ref[...] += part
            loss_ref[...] += lsum

        dh = _rms_bwd(dy, xhat, r, gv)
        dh_ref[...] = dh
        dhb_ref[...] = (0.5 * dh).astype(BF16)

    return _pcall(
        comm, body, name=name, grid=(M // tr,),
        in_specs=[_row_spec(tr, D), _row_spec(tr, D), _vec_spec(D)],
        out_specs=[pl.BlockSpec((8, 128), lambda i: (0, 0)), _row_spec(tr, D), _row_spec(tr, D), _vec_spec(D)],
        out_shape=[jax.ShapeDtypeStruct((8, 128), F32), jax.ShapeDtypeStruct((M, D), F32),
                   jax.ShapeDtypeStruct((M, D), BF16), jax.ShapeDtypeStruct((1, D), F32)],
        compiler_params=_params("arbitrary"),
    )(h, target, g)


_GELU_C = math.sqrt(2.0 / math.pi)


def _gelu(x):
    return 0.5 * x * (1.0 + jnp.tanh(_GELU_C * (x + 0.044715 * (x * x * x))))


def _gelu_grad(x):
    t = jnp.tanh(_GELU_C * (x + 0.044715 * (x * x * x)))
    return 0.5 * (1.0 + t) + 0.5 * x * (1.0 - t * t) * (_GELU_C * (1.0 + 3.0 * 0.044715 * (x * x)))


def _sgu_mask():
    t = lax.broadcasted_iota(jnp.int32, (SGU_BLOCK, SGU_BLOCK), 0) // CHUNK
    s = lax.broadcasted_iota(jnp.int32, (SGU_BLOCK, SGU_BLOCK), 1) // CHUNK
    return s <= t


def _layernorm_stats(v):
    mu = jnp.mean(v, axis=-1, keepdims=True)
    cen = v - mu
    rstd = lax.rsqrt(jnp.mean(cen * cen, axis=-1, keepdims=True) + EPS)
    return cen * rstd, rstd


def sgu_fwd(z, ln_g, ln_b, w_s, b_t, name, comm=None):
    S = z.shape[0]

    def body(zu_ref, zv_ref, lg_ref, lb_ref, w_ref, bt_ref, o_ref):
        mask = _sgu_mask()
        for g in range(G_A):
            cols = slice(g * GA_DIM, (g + 1) * GA_DIM)
            u = _gelu(zu_ref[:, cols])
            vhat, _ = _layernorm_stats(_gelu(zv_ref[:, cols]))
            vln = vhat * lg_ref[:, cols] + lb_ref[:, cols]
            w = jnp.where(mask, w_ref[g], 0.0).astype(BF16)
            mixed = _dot(w, vln.astype(BF16)) + bt_ref[:, g:g + 1]
            o_ref[:, cols] = u * mixed

    return _pcall(
        comm, body, name=name, grid=(S // SGU_BLOCK,),
        in_specs=[_row_spec(SGU_BLOCK, W_A, 0), _row_spec(SGU_BLOCK, W_A, 1), _vec_spec(W_A), _vec_spec(W_A),
                  pl.BlockSpec((G_A, SGU_BLOCK, SGU_BLOCK), lambda i: (0, 0, 0)),
                  pl.BlockSpec((SGU_BLOCK, G_A), lambda i: (0, 0))],
        out_specs=_row_spec(SGU_BLOCK, W_A),
        out_shape=jax.ShapeDtypeStruct((S, W_A), F32),
        compiler_params=_params("parallel"),
    )(z, z, ln_g, ln_b, w_s, b_t)


def sgu_bwd(z, dya, ln_g, ln_b, w_s, b_t, name, comm=None):
    S = z.shape[0]
    nblk = S // SGU_BLOCK

    def body(zu_ref, zv_ref, dy_ref, lg_ref, lb_ref, w_ref, bt_ref,
             dz_ref, dlg_ref, dlb_ref, dw_ref, db_ref, dmix_acc):
        step = pl.program_id(0)
        mask = _sgu_mask()

        @pl.when(step == 0)
        def _():
            dlg_ref[...] = _zeros(dlg_ref)
            dlb_ref[...] = _zeros(dlb_ref)
            dw_ref[...] = _zeros(dw_ref)
            dmix_acc[...] = _zeros(dmix_acc)

        for g in range(G_A):
            cols = slice(g * GA_DIM, (g + 1) * GA_DIM)
            zu = zu_ref[:, cols]
            zv = zv_ref[:, cols]
            u = _gelu(zu)
            vhat, rstd = _layernorm_stats(_gelu(zv))
            lg = lg_ref[:, cols]
            vln = (vhat * lg + lb_ref[:, cols]).astype(BF16)
            w = jnp.where(mask, w_ref[g], 0.0)
            mixed = _dot(w.astype(BF16), vln) + bt_ref[:, g:g + 1]
            dy = dy_ref[:, cols]
            du = dy * mixed
            dmixed = dy * u
            dmixed_b = dmixed.astype(BF16)
            dmix_acc[g] += dmixed
            dw_ref[g] += jnp.where(mask, _dot_nt(dmixed_b, vln), 0.0)
            dvln = _dot(w.T.astype(BF16), dmixed_b)
            dlb_ref[:, cols] += jnp.sum(dvln, axis=0, keepdims=True)
            dlg_ref[:, cols] += jnp.sum(dvln * vhat, axis=0, keepdims=True)
            dvhat = dvln * lg
            dv = rstd * (dvhat - jnp.mean(dvhat, axis=-1, keepdims=True)
                         - vhat * jnp.mean(dvhat * vhat, axis=-1, keepdims=True))
            dz_ref[:, cols] = (du * _gelu_grad(zu)).astype(BF16)
            dz_ref[:, W_A + g * GA_DIM:W_A + (g + 1) * GA_DIM] = (dv * _gelu_grad(zv)).astype(BF16)

        @pl.when(step == nblk - 1)
        def _():
            for g in range(G_A):
                db_ref[g] = jnp.sum(dmix_acc[g], axis=1, keepdims=True)

    whole3 = lambda shape: pl.BlockSpec(shape, lambda i: (0, 0, 0))
    return _pcall(
        comm, body, name=name, grid=(nblk,),
        in_specs=[_row_spec(SGU_BLOCK, W_A, 0), _row_spec(SGU_BLOCK, W_A, 1), _row_spec(SGU_BLOCK, W_A),
                  _vec_spec(W_A), _vec_spec(W_A), whole3((G_A, SGU_BLOCK, SGU_BLOCK)),
                  pl.BlockSpec((SGU_BLOCK, G_A), lambda i: (0, 0))],
        out_specs=[_row_spec(SGU_BLOCK, 2 * W_A), _vec_spec(W_A), _vec_spec(W_A),
                   whole3((G_A, SGU_BLOCK, SGU_BLOCK)), whole3((G_A, SGU_BLOCK, 1))],
        out_shape=[jax.ShapeDtypeStruct((S, 2 * W_A), BF16), jax.ShapeDtypeStruct((1, W_A), F32),
                   jax.ShapeDtypeStruct((1, W_A), F32), jax.ShapeDtypeStruct((G_A, SGU_BLOCK, SGU_BLOCK), F32),
                   jax.ShapeDtypeStruct((G_A, SGU_BLOCK, 1), F32)],
        scratch_shapes=[pltpu.VMEM((G_A, SGU_BLOCK, SGU_BLOCK), F32)],
        compiler_params=_params("arbitrary"),
    )(z, z, dya, ln_g, ln_b, w_s, b_t)


def _log_sigmoid(z):
    return jnp.minimum(z, 0.0) - jnp.log(1.0 + jnp.exp(-jnp.abs(z)))


def _suffix_sum(x, upper):
    hi = x.astype(BF16)
    lo = (x - hi.astype(F32)).astype(BF16)
    return _dot(hi, upper) + _dot(lo, upper)


SB_ROWS = 1024
_SB_SUB = SB_ROWS // Q_BLOCK


def _sb_upper():
    row = lax.broadcasted_iota(jnp.int32, (Q_BLOCK, Q_BLOCK), 0)
    col = lax.broadcasted_iota(jnp.int32, (Q_BLOCK, Q_BLOCK), 1)
    return (row > col).astype(BF16)


def _sb_sweep(step, tile):
    for r in reversed(range(_SB_SUB)):
        tile(step * _SB_SUB + r, r * Q_BLOCK)

    def group(g, _):
        base = (step - 1 - g) * _SB_SUB
        for r in reversed(range(_SB_SUB)):
            tile(base + r, None)
        return 0

    lax.fori_loop(0, step, group, 0)


def _sb_causal(n):
    return lax.broadcasted_iota(jnp.int32, (n, Q_BLOCK), 1) < lax.broadcasted_iota(jnp.int32, (n, Q_BLOCK), 0)


def _sb_rows_spec():
    return pl.BlockSpec((None, SB_ROWS, DH_B), lambda h, i: (h, i, 0))


def _sb_head_spec(S, part=0):
    return pl.BlockSpec((None, S, DH_B), lambda h, i: (part * H_B + h, 0, 0))


def sb_fwd(qkv, name, comm=None):
    S = qkv.shape[1]

    def body(q_b, k_ref, v_ref, o_ref, c_l1m):
        step = pl.program_id(1)
        o_ref[...] = _zeros(o_ref)
        c_l1m[...] = _zeros(c_l1m)
        upper = _sb_upper()

        def tile(j, row0):
            rq = slice(row0 or 0, SB_ROWS)
            causal = None if row0 is None else _sb_causal(SB_ROWS - row0)
            rows = pl.ds(pl.multiple_of(j * Q_BLOCK, Q_BLOCK), Q_BLOCK)
            zz = _dot_nt(q_b[rq, :], k_ref[rows, :])
            lb = _log_sigmoid(zz)
            l1m = lb - zz
            if causal is not None:
                l1m = jnp.where(causal, l1m, 0.0)
            a = jnp.exp(lb + _suffix_sum(l1m, upper) + c_l1m[rq, :])
            if causal is not None:
                a = jnp.where(causal, a, 0.0)
            o_ref[rq, :] += _dot(a.astype(BF16), v_ref[rows, :])
            c_l1m[rq, :] += jnp.sum(l1m, axis=1, keepdims=True)

        _sb_sweep(step, tile)

    return _pcall(
        comm, body, name=name, grid=(H_B, S // SB_ROWS),
        in_specs=[_sb_rows_spec(), _sb_head_spec(S, 1), _sb_head_spec(S, 2)],
        out_specs=_sb_rows_spec(),
        out_shape=jax.ShapeDtypeStruct((H_B, S, DH_B), F32),
        scratch_shapes=[pltpu.VMEM((SB_ROWS, 1), F32)],
        compiler_params=_params("parallel", "parallel"),
    )(qkv, qkv, qkv)


def sb_bwd(qkv, out, dout, name, comm=None):
    S = qkv.shape[1]
    nstep = S // SB_ROWS
    scale = DH_B ** -0.5

    def body(q_b, k_ref, v_ref, o_ref, do_ref, dq_ref, dk_ref, dv_ref,
             dq_acc, dkt_acc, dvt_acc, do_b, qt_b, dot_b, g_left, c_l1m):
        step = pl.program_id(1)

        @pl.when(step == 0)
        def _():
            dkt_acc[...] = _zeros(dkt_acc)
            dvt_acc[...] = _zeros(dvt_acc)

        do_b[...] = do_ref[...].astype(BF16)
        qt_b[...] = q_b[...].astype(F32).T.astype(BF16)
        dot_b[...] = do_ref[...].T.astype(BF16)
        g_left[...] = jnp.sum(do_b[...].astype(F32) * o_ref[...], axis=1, keepdims=True)
        dq_acc[...] = _zeros(dq_acc)
        c_l1m[...] = _zeros(c_l1m)
        upper = _sb_upper()

        def tile(j, row0):
            rq = slice(row0 or 0, SB_ROWS)
            causal = None if row0 is None else _sb_causal(SB_ROWS - row0)
            rows = pl.ds(pl.multiple_of(j * Q_BLOCK, Q_BLOCK), Q_BLOCK)
            q, do_t = q_b[rq, :], do_b[rq, :]
            k_j = k_ref[rows, :]
            zz = _dot_nt(q, k_j)
            lb = _log_sigmoid(zz)
            l1m = lb - zz
            if causal is not None:
                l1m = jnp.where(causal, l1m, 0.0)
            a = jnp.exp(lb + _suffix_sum(l1m, upper) + c_l1m[rq, :])
            if causal is not None:
                a = jnp.where(causal, a, 0.0)
            a_b = a.astype(BF16)
            dvt_acc[:, rows] += _dot(dot_b[:, rq], a_b)
            gmat = a_b.astype(F32) * _dot_nt(do_t, v_ref[rows, :])
            before = g_left[rq, :] - gmat - _suffix_sum(gmat, upper)
            sig = jnp.exp(lb)
            dz = gmat * (1.0 - sig) - sig * before
            if causal is not None:
                dz = jnp.where(causal, dz, 0.0)
            dz_b = dz.astype(BF16)
            dkt_acc[:, rows] += _dot(qt_b[:, rq], dz_b)
            dq_acc[rq, :] += _dot(dz_b, k_j)
            c_l1m[rq, :] += jnp.sum(l1m, axis=1, keepdims=True)
            g_left[rq, :] -= jnp.sum(gmat, axis=1, keepdims=True)

        _sb_sweep(step, tile)
        dq_ref[...] = (dq_acc[...] * scale).astype(BF16)

        @pl.when(step == nstep - 1)
        def _():
            dk_ref[...] = dkt_acc[...].T.astype(BF16)
            dv_ref[...] = dvt_acc[...].T.astype(BF16)

    out_sds = jax.ShapeDtypeStruct((H_B, S, DH_B), BF16)
    return _pcall(
        comm, body, name=name, grid=(H_B, nstep),
        in_specs=[_sb_rows_spec(), _sb_head_spec(S, 1), _sb_head_spec(S, 2), _sb_rows_spec(), _sb_rows_spec()],
        out_specs=[_sb_rows_spec(), _sb_head_spec(S), _sb_head_spec(S)],
        out_shape=[out_sds, out_sds, out_sds],
        scratch_shapes=[pltpu.VMEM((SB_ROWS, DH_B), F32)] + [pltpu.VMEM((DH_B, S), F32)] * 2
        + [pltpu.VMEM((SB_ROWS, DH_B), BF16)] + [pltpu.VMEM((DH_B, SB_ROWS), BF16)] * 2
        + [pltpu.VMEM((SB_ROWS, 1), F32)] * 2,
        compiler_params=_params("parallel", "arbitrary"),
    )(qkv, qkv, qkv, out, dout)


def _softmax(s):
    e = jnp.exp(s - jnp.max(s, axis=-1, keepdims=True))
    return e / jnp.sum(e, axis=-1, keepdims=True)


def xattn_fwd(qc, kv, name, comm=None):
    S, D = qc.shape
    tr = min(ROW_TILE, S)

    def body(q_ref, kv_ref, o_ref):
        for h in range(X_HEADS):
            cols = slice(h * X_DH, (h + 1) * X_DH)
            p = _softmax(_dot_nt(q_ref[:, cols], kv_ref[:, cols]))
            o_ref[:, cols] = _dot(p.astype(BF16), kv_ref[:, D + h * X_DH:D + (h + 1) * X_DH]).astype(BF16)

    return _pcall(
        comm, body, name=name, grid=(S // tr,),
        in_specs=[_row_spec(tr, D), pl.BlockSpec((N_MEM, 2 * D), lambda i: (0, 0))],
        out_specs=_row_spec(tr, D),
        out_shape=jax.ShapeDtypeStruct((S, D), BF16),
        compiler_params=_params("parallel"),
    )(qc, kv)


def xattn_bwd(qc, kv, do, name, comm=None):
    S, D = qc.shape
    tr = min(ROW_TILE, S)
    nstep = S // tr
    scale = X_DH ** -0.5

    def body(q_ref, kv_ref, do_ref, dq_ref, dkv_ref, acc):
        step = pl.program_id(0)

        @pl.when(step == 0)
        def _():
            acc[...] = _zeros(acc)

        for h in range(X_HEADS):
            cols = slice(h * X_DH, (h + 1) * X_DH)
            vcols = slice(D + h * X_DH, D + (h + 1) * X_DH)
            q = q_ref[:, cols]
            k = kv_ref[:, cols]
            do_h = do_ref[:, cols]
            p = _softmax(_dot_nt(q, k))
            dp = _dot_nt(do_h, kv_ref[:, vcols])
            acc[:, vcols] += _dot_tn(p.astype(BF16), do_h)
            ds = (p * (dp - jnp.sum(p * dp, axis=-1, keepdims=True))).astype(BF16)
            dq_ref[:, cols] = (_dot(ds, k) * scale).astype(BF16)
            acc[:, cols] += _dot_tn(ds, q)

        @pl.when(step == nstep - 1)
        def _():
            dkv_ref[...] = acc[...].astype(BF16)

    whole = pl.BlockSpec((N_MEM, 2 * D), lambda i: (0, 0))
    return _pcall(
        comm, body, name=name, grid=(nstep,),
        in_specs=[_row_spec(tr, D), whole, _row_spec(tr, D)],
        out_specs=[_row_spec(tr, D), whole],
        out_shape=[jax.ShapeDtypeStruct((S, D), BF16), jax.ShapeDtypeStruct((N_MEM, 2 * D), BF16)],
        scratch_shapes=[pltpu.VMEM((N_MEM, 2 * D), F32)],
        compiler_params=_params("arbitrary"),
    )(qc, kv, do)


def _row_tile(rows, cap=128):
    return max(t for t in range(16, cap + 1, 16) if rows % t == 0)


def cast_bf16(w, name, comm=None):
    R, C = w.shape
    tr = _row_tile(R, 256)

    def body(w_ref, o_ref):
        o_ref[...] = w_ref[...].astype(BF16)

    return _pcall(
        comm, body, name=name, grid=(R // tr,),
        in_specs=[_row_spec(tr, C)], out_specs=_row_spec(tr, C),
        out_shape=jax.ShapeDtypeStruct((R, C), BF16),
        compiler_params=_params("parallel"),
    )(w)


def adamw(parts, w, m, v, name, comm=None):
    R, C = w.shape
    n_parts = parts.shape[0]
    tr = _row_tile(R, 256)
    c1 = 1.0 - ADAM_B1 ** ADAM_STEP
    c2 = 1.0 - ADAM_B2 ** ADAM_STEP

    def body(p_ref, w_ref, m_ref, v_ref, g_ref, d_ref, mo_ref, vo_ref):
        g = p_ref[0].astype(F32)
        for p in range(1, n_parts):
            g = g + p_ref[p].astype(F32)
        m_new = ADAM_B1 * m_ref[...] + (1.0 - ADAM_B1) * g
        v_new = ADAM_B2 * v_ref[...] + (1.0 - ADAM_B2) * (g * g)
        g_ref[...] = g
        mo_ref[...] = m_new
        vo_ref[...] = v_new
        d_ref[...] = -ADAM_LR * ((m_new / c1) / (jnp.sqrt(v_new / c2) + ADAM_EPS) + ADAM_WD * w_ref[...])

    spec = _row_spec(tr, C)
    sds = jax.ShapeDtypeStruct((R, C), F32)
    return _pcall(
        comm, body, name=name, grid=(R // tr,),
        in_specs=[pl.BlockSpec((n_parts, tr, C), lambda i: (0, i, 0)), spec, spec, spec],
        out_specs=[spec, spec, spec, spec],
        out_shape=[sds, sds, sds, sds],
        compiler_params=_params("parallel"),
    )(parts, w, m, v)


def pair_sum(parts, from_sibling, core, name):
    _, R, C = parts.shape
    tr = _row_tile(R, 1024)

    def body(core_ref, p_ref, s_ref, o_ref):
        o_ref[...] = (p_ref[...].astype(F32) + s_ref[...].astype(F32)).astype(o_ref.dtype)

    return pl.pallas_call(
        body, name=name,
        grid_spec=pltpu.PrefetchScalarGridSpec(
            num_scalar_prefetch=1, grid=(4, R // tr),
            in_specs=[pl.BlockSpec((None, tr, C), lambda q, i, core_ref: (2 * q + core_ref[0], i, 0)),
                      pl.BlockSpec((None, tr, C), lambda q, i, core_ref: (q, i, 0))],
            out_specs=pl.BlockSpec((None, tr, C), lambda q, i, core_ref: (q, i, 0))),
        out_shape=jax.ShapeDtypeStruct((4, R, C), BF16),
        compiler_params=_params("parallel", "parallel"),
    )(core, parts, from_sibling)


def add2(a, b, name, comm=None):
    R, C = a.shape
    tr = _row_tile(R, 256)

    def body(a_ref, b_ref, o_ref):
        o_ref[...] = a_ref[...] + b_ref[...]

    spec = _row_spec(tr, C)
    return _pcall(
        comm, body, name=name, grid=(R // tr,), in_specs=[spec, spec], out_specs=spec,
        out_shape=jax.ShapeDtypeStruct((R, C), F32), compiler_params=_params("parallel"),
    )(a, b)


def _place():
    return lax.axis_index("x"), lax.axis_index("y"), lax.axis_index("c")


class Comm:
    def __init__(self, arrays, out_shapes, n_remote, n_local, start, finish, aliases=None):
        self.arrays, self.out_shapes = list(arrays), list(out_shapes)
        self.n_remote, self.n_local = n_remote, max(n_local, 1)
        self.start, self.finish = start, finish
        self.aliases = dict(aliases or {})
        self.sizes = [len(self.out_shapes)]

    def sem_shapes(self):
        return [pltpu.SemaphoreType.DMA((self.n_remote,)), pltpu.SemaphoreType.DMA((self.n_remote,)),
                pltpu.SemaphoreType.DMA((self.n_local,))]


class _Shifted:
    def __init__(self, ref, offset):
        self.ref, self.offset = ref, offset

    @property
    def at(self):
        return self

    def __getitem__(self, k):
        return self.ref.at[self.offset + k]


def merge_comms(comms):
    comms = [c for c in comms if c is not None]
    if not comms:
        return None

    def each(method):
        def run(ins, outs, sems):
            i = o = r = l = 0
            for c in comms:
                sub = (_Shifted(sems[0], r), _Shifted(sems[1], r), _Shifted(sems[2], l))
                getattr(c, method)(ins[i:i + len(c.arrays)], outs[o:o + len(c.out_shapes)], sub)
                i, o, r, l = i + len(c.arrays), o + len(c.out_shapes), r + c.n_remote, l + c.n_local
        return run

    aliases, i, o = {}, 0, 0
    for c in comms:
        aliases.update({i + a: o + b for a, b in c.aliases.items()})
        i, o = i + len(c.arrays), o + len(c.out_shapes)
    merged = Comm([a for c in comms for a in c.arrays], [s for c in comms for s in c.out_shapes],
                  sum(c.n_remote for c in comms), sum(c.n_local for c in comms), each("start"), each("finish"), aliases)
    merged.sizes = [len(c.out_shapes) for c in comms]
    return merged


def split_results(comm, results):
    out, i = [], 0
    for n in comm.sizes:
        out.append(list(results[i:i + n]))
        i += n
    return out


def run_comm(comm, name):
    n_in, n_out = len(comm.arrays), len(comm.out_shapes)

    def body(*refs):
        ins, outs, sems = refs[:n_in], refs[n_in:n_in + n_out], refs[n_in + n_out:]
        comm.start(ins, outs, sems)
        comm.finish(ins, outs, sems)

    return pl.pallas_call(
        body, name=name, in_specs=[ANY] * n_in, out_specs=[ANY] * n_out, out_shape=comm.out_shapes,
        scratch_shapes=comm.sem_shapes(), input_output_aliases=comm.aliases,
    )(*comm.arrays)


def _remote(src, dst, sems, k, to):
    return pltpu.make_async_remote_copy(src_ref=src, dst_ref=dst, send_sem=sems[0].at[k], recv_sem=sems[1].at[k],
                                        device_id=to, device_id_type=MESH)


_AG_COPIES = 13


def comm_all_gather(shards, rows=None, into=None):
    n = len(shards)
    row0, nrows = rows if rows is not None else (0, None)

    def parties():
        x, y, c = _place()
        return (x, y, c), (x, y, 1 - c), [(1 - x, y), (x, 1 - y), (1 - x, 1 - y)]

    def span(w, half=None):
        count = nrows if nrows is not None else shards[w].shape[0]
        if half is None:
            return pl.ds(row0, count)
        return pl.ds(row0 + half * (count // 2), count // 2)

    def slab(outs, w, dev, half=None):
        return outs[w].at[4 * dev[0] + 2 * dev[1] + dev[2], span(w, half)]

    def own(ins, outs, sems):
        me, sibling, chips = parties()
        local = [pltpu.make_async_copy(ins[w].at[span(w)], slab(outs, w, me), sems[2].at[w]) for w in range(n)]
        first = []
        for w in range(n):
            k = _AG_COPIES * w
            first.append(_remote(ins[w].at[span(w)], slab(outs, w, me), sems, k, sibling))
            first += [_remote(ins[w].at[span(w, h)], slab(outs, w, me, h), sems, k + 1 + 2 * j + h, (*chip, me[2]))
                      for h in range(2) for j, chip in enumerate(chips)]
        return local, first

    def start(ins, outs, sems):
        local, first = own(ins, outs, sems)
        for cp in local + first:
            cp.start()

    def finish(ins, outs, sems):
        me, sibling, chips = parties()
        local, first = own(ins, outs, sems)
        passed = []
        for w in range(n):
            k = _AG_COPIES * w
            for h in range(2):
                for j, chip in enumerate(chips):
                    got = slab(outs, w, (*chip, me[2]), h)
                    _remote(got, got, sems, k + 1 + 2 * j + h, me).wait_recv()
                    cp = _remote(got, got, sems, k + 7 + 2 * j + h, sibling)
                    cp.start()
                    passed.append(cp)
        for w in range(n):
            k = _AG_COPIES * w
            got = slab(outs, w, sibling)
            _remote(got, got, sems, k, me).wait_recv()
            for h in range(2):
                for j, chip in enumerate(chips):
                    got = slab(outs, w, (*chip, sibling[2]), h)
                    _remote(got, got, sems, k + 7 + 2 * j + h, me).wait_recv()
        for cp in first + passed:
            cp.wait_send()
        for cp in local:
            cp.wait()

    out_shapes = [jax.ShapeDtypeStruct((N_DEV,) + s.shape, s.dtype) for s in shards]
    arrays = list(shards) + (list(into) if into is not None else [])
    aliases = {n + w: w for w in range(n)} if into is not None else None
    return Comm(arrays, out_shapes, _AG_COPIES * n, n, start, finish, aliases)


def comm_pairs(items):
    slabbed = [a.ndim == 3 for a in items]
    first = [sum(4 if s else 1 for s in slabbed[:w]) for w in range(len(items))]

    def copies(ins, outs, sems):
        x, y, c = _place()
        sibling = (x, y, 1 - c)
        cps = []
        for w, s in enumerate(slabbed):
            if s:
                cps += [_remote(ins[w].at[2 * q + (1 - c)], outs[w].at[q], sems, first[w] + q, sibling) for q in range(4)]
            else:
                cps.append(_remote(ins[w], outs[w], sems, first[w], sibling))
        return cps

    def start(ins, outs, sems):
        for cp in copies(ins, outs, sems):
            cp.start()

    def finish(ins, outs, sems):
        for cp in copies(ins, outs, sems):
            cp.wait()

    out_shapes = [jax.ShapeDtypeStruct(((4,) + a.shape[1:]) if s else a.shape, a.dtype) for a, s in zip(items, slabbed)]
    return Comm(items, out_shapes, sum(4 if s else 1 for s in slabbed), 0, start, finish)


def comm_chips(items, rows=None, into=None, from_row=None, out_rows=None):
    n = len(items)
    slabbed = [a.ndim == 3 for a in items]

    def span(w, source=False):
        if rows is None:
            return pl.ds(0, items[w].shape[-2])
        return pl.ds(from_row if source and from_row is not None else rows[0], rows[1])

    def copies(ins, outs, sems):
        x, y, c = _place()
        mine = 2 * x + y
        local = [pltpu.make_async_copy(ins[w].at[mine, span(w, True)] if slabbed[w] else ins[w].at[span(w, True)],
                                       outs[w].at[mine, span(w)], sems[2].at[w]) for w in range(n)]
        remote = []
        for w in range(n):
            for j, (px, py) in enumerate([(1 - x, y), (x, 1 - y), (1 - x, 1 - y)]):
                src = ins[w].at[2 * px + py, span(w, True)] if slabbed[w] else ins[w].at[span(w, True)]
                remote.append(_remote(src, outs[w].at[mine, span(w)], sems, 3 * w + j, (px, py, c)))
        return local, remote

    def start(ins, outs, sems):
        local, remote = copies(ins, outs, sems)
        for cp in local + remote:
            cp.start()

    def finish(ins, outs, sems):
        local, remote = copies(ins, outs, sems)
        for cp in remote + local:
            cp.wait()

    def result(a):
        tall = a.shape[:-2] + (out_rows if out_rows is not None else a.shape[-2], a.shape[-1])
        return jax.ShapeDtypeStruct(tall if a.ndim == 3 else (4,) + tall, a.dtype)

    if into is None:
        return Comm(items, [result(a) for a in items], 3 * n, n, start, finish)
    out_shapes = [jax.ShapeDtypeStruct(b.shape, b.dtype) for b in into]
    return Comm(list(items) + list(into), out_shapes, 3 * n, n, start, finish, {n + w: w for w in range(n)})


_SMALL = ("ffn1_norm", "mix_norm", "ln_v_gain", "ln_v_bias", "spatial_w", "spatial_b", "gnorm_a", "gnorm_b",
          "cross_norm", "mem_norm", "ffn2_norm", "final_norm")
_BIG = ("ffn1_w_in", "ffn1_w_out", "w_mix_in", "w_mix_out", "w_cq", "w_ckv", "w_co", "ffn2_w_in", "ffn2_w_out")
_COL_SHARDED = ("ffn1_w_in", "w_mix_in", "w_ckv", "ffn2_w_in")
_ORDER = ("ffn1_norm", "ffn1_w_in", "ffn1_w_out", "mix_norm", "w_mix_in", "ln_v_gain", "ln_v_bias", "spatial_w",
          "spatial_b", "gnorm_a", "gnorm_b", "w_mix_out", "cross_norm", "mem_norm", "w_cq", "w_ckv", "w_co",
          "ffn2_norm", "ffn2_w_in", "ffn2_w_out", "final_norm")


_SMALL_PAD = 120


def _rows128(a):
    return a.reshape(-1, 128)


def kernel(x, mem, ffn1_norm, ffn1_w_in, ffn1_w_out, mix_norm, w_mix_in, ln_v_gain, ln_v_bias, spatial_w, spatial_b, gnorm_a, gnorm_b, w_mix_out, cross_norm, mem_norm, w_cq, w_ckv, w_co, ffn2_norm, ffn2_w_in, ffn2_w_out, final_norm, loss_target, m_ffn1_norm, m_ffn1_w_in, m_ffn1_w_out, m_mix_norm, m_w_mix_in, m_ln_v_gain, m_ln_v_bias, m_spatial_w, m_spatial_b, m_gnorm_a, m_gnorm_b, m_w_mix_out, m_cross_norm, m_mem_norm, m_w_cq, m_w_ckv, m_w_co, m_ffn2_norm, m_ffn2_w_in, m_ffn2_w_out, m_final_norm, v_ffn1_norm, v_ffn1_w_in, v_ffn1_w_out, v_mix_norm, v_w_mix_in, v_ln_v_gain, v_ln_v_bias, v_spatial_w, v_spatial_b, v_gnorm_a, v_gnorm_b, v_w_mix_out, v_cross_norm, v_mem_norm, v_w_cq, v_w_ckv, v_w_co, v_ffn2_norm, v_ffn2_w_in, v_ffn2_w_out, v_final_norm):
    given = dict(locals())
    wts = {k: given[k] for k in _ORDER}
    mom = {k: given["m_" + k] for k in _ORDER}
    var = {k: given["v_" + k] for k in _ORDER}

    D = D_MODEL
    xs = x.reshape(-1, D)
    mems = mem.reshape(-1, D)
    tgt = loss_target.reshape(-1, D)
    vec = lambda a: a.reshape(1, -1)
    g1, gmix, gcross, gmem, g2, gfin = (vec(wts[k]) for k in
                                        ("ffn1_norm", "mix_norm", "cross_norm", "mem_norm", "ffn2_norm", "final_norm"))
    ln_g, ln_b, ga, gb = (vec(wts[k]) for k in ("ln_v_gain", "ln_v_bias", "gnorm_a", "gnorm_b"))
    w_s = spatial_w.reshape(G_A, SGU_BLOCK, SGU_BLOCK)
    b_t = spatial_b.reshape(G_A, SGU_BLOCK).T

    shard2d = {k: wts[k].reshape(wts[k].shape[1:]) for k in _BIG}
    shard_b = {k: cast_bf16(shard2d[k], f"cast_{k}") for k in _BIG}
    full = {}

    landing, rows_done = {}, {}

    def gathering(pieces, fn, *args, **kw):
        pieces = [p if isinstance(p, tuple) else (p, None) for p in pieces]
        comm = merge_comms([comm_all_gather([shard_b[k]], rows, [landing[k]] if k in landing else None)
                            for k, rows in pieces])
        out, got = fn(*args, comm=comm, **kw)
        for (k, rows), (g,) in zip(pieces, split_results(comm, got)):
            landing[k] = g
            rows_done[k] = rows_done.get(k, 0) + (rows[1] if rows is not None else shard_b[k].shape[0])
            if rows_done[k] == shard_b[k].shape[0]:
                full[k] = g if k in _COL_SHARDED else g.reshape(-1, g.shape[2])
        return out

    n1 = gathering(("ffn1_w_in",), rms_fwd, xs, g1, "f_n1")
    a1, hsw1 = gathering(("ffn1_w_out",), mm_swiglu_g, n1, full["ffn1_w_in"], "f_a1")
    h1 = gathering(("w_mix_in",), mm_nn, hsw1, full["ffn1_w_out"], F32, "f_h1", scale=0.5, res=xs)
    n2 = gathering(("w_cq",), rms_fwd, h1, gmix, "f_n2")
    z = gathering(("w_mix_out", "w_co"), mm_nn_g, n2, full["w_mix_in"], F32, "f_z")
    ya = gathering((("w_ckv", (0, 512)),), sgu_fwd, z, ln_g, ln_b, w_s, b_t, "f_sgu")
    qkv = split_heads(z, "f_qkv")
    yb = gathering((("w_ckv", (512, 1536)), ("ffn2_w_in", (0, 512))), sb_fwd, qkv, "f_sb")
    ycat = gathering((("ffn2_w_in", (512, 128)),), rmscat_fwd, ya, yb, ga, gb, "f_ycat")
    h2 = gathering((("ffn2_w_in", (640, 256)),), mm_nn, ycat, full["w_mix_out"], F32, "f_h2", res=h1)
    n3 = gathering((("ffn2_w_in", (896, 128)),), rms_fwd, h2, gcross, "f_n3")
    memn = rms_fwd(mems, gmem, "f_memn")
    qc = gathering((("ffn2_w_in", (1024, 256)),), mm_nn, n3, full["w_cq"], BF16, "f_qc", scale=X_DH ** -0.5)
    kv = gathering((("ffn2_w_in", (1280, 128)),), mm_nn_g, memn, full["w_ckv"], BF16, "f_kv")
    o = gathering((("ffn2_w_in", (1408, 128)),), xattn_fwd, qc, kv, "f_xattn")
    h3 = gathering((("ffn2_w_in", (1536, 256)),), mm_nn, o, full["w_co"], F32, "f_h3", res=h2)
    n4 = gathering((("ffn2_w_in", (1792, 256)),), rms_fwd, h3, g2, "f_n4")
    a2, hsw2 = gathering(("ffn2_w_out",), mm_swiglu_g, n4, full["ffn2_w_in"], "f_a2")
    h4 = mm_nn(hsw2, full["ffn2_w_out"], F32, "f_h4", scale=0.5, res=h3)

    grads, parts, sums, recv = {}, {}, {}, {}
    core = lax.axis_index("c").astype(jnp.int32).reshape(1)

    def partial_of(k, g):
        grads[k] = g
        parts[k] = g if g.ndim == 3 else g.reshape(N_DEV, -1, g.shape[1])

    def reducing(pairs, chips, fn, *args, **kw):
        def piece(p):
            if isinstance(p, dict):
                return p
            k, rows = p if isinstance(p, tuple) else (p, None)
            return dict(sums=k, rows=rows, to=k)

        chips = [piece(p) for p in chips]
        comms = [comm_pairs([parts[k] for k in pairs])] if pairs else []
        comms += [comm_chips([sums[p["sums"]]], p["rows"], [recv[p["to"]]] if p["to"] in recv else None,
                             p.get("from_row"), p.get("out_rows")) for p in chips]
        comm = merge_comms(comms)
        out, got = fn(*args, comm=comm, **kw)
        got = split_results(comm, got)
        if pairs:
            for k, r in zip(pairs, got.pop(0)):
                sums[k] = pair_sum(parts[k], r, core, f"pair_sum_{k}")
        for p, (r,) in zip(chips, got):
            recv[p["to"]] = r
        return out

    loss_part, dh4, df2, grads["final_norm"] = loss_head(h4, tgt, gfin, "loss_head")
    partial_of("ffn2_w_out", mm_tn(hsw2, df2, "b_ffn2_dwout"))
    da2 = reducing(("ffn2_w_out",), (), mm_swiglu_bwd, df2, full["ffn2_w_out"], a2, "b_ffn2_da")
    partial_of("ffn2_w_in", reducing((), ("ffn2_w_out",), mm_tn_g, n4, da2, N_DEV, "b_ffn2_dwin"))
    dn4 = reducing(("ffn2_w_in",), (), mm_nt_g, da2, full["ffn2_w_in"], "b_ffn2_dn")
    dh3, dh3b, grads["ffn2_norm"] = rms_bwd(dn4, h3, g2, dh4, 1.0, "b_n4")

    partial_of("w_co", mm_tn(o, dh3b, "b_dwco"))
    do = reducing(("w_co",), (), mm_nt, dh3b, full["w_co"], BF16, "b_do")
    dqp, dkv = xattn_bwd(qc, kv, do, "b_xattn")
    partial_of("w_cq", mm_tn(n3, dqp, "b_dwcq"))
    dn3 = reducing(("w_cq",), (), mm_nt, dqp, full["w_cq"], F32, "b_dn3")
    partial_of("w_ckv", mm_tn_g(memn, dkv, N_DEV, "b_dwckv"))
    dmemn = reducing(("w_ckv",), (), mm_nt_g, dkv, full["w_ckv"], "b_dmemn")
    _, _, grads["mem_norm"] = rms_bwd(dmemn, mems, gmem, None, 1.0, "b_memn")
    dh2, dh2b, grads["cross_norm"] = rms_bwd(dn3, h2, gcross, dh3, 1.0, "b_n3")

    partial_of("w_mix_out", mm_tn(ycat, dh2b, "b_dwmixout"))
    dycat = reducing(("w_mix_out",), (), mm_nt, dh2b, full["w_mix_out"], F32, "b_dycat")
    dya, dyb, grads["gnorm_a"], grads["gnorm_b"] = rmscat_bwd(dycat, ya, yb, ga, gb, "b_ycat")
    dza, grads["ln_v_gain"], grads["ln_v_bias"], grads["spatial_w"], grads["spatial_b"] = sgu_bwd(
        z, dya, ln_g, ln_b, w_s, b_t, "b_sgu")
    dq, dk, dv = reducing((), ("ffn2_w_in", "w_co", "w_cq"), sb_bwd, qkv, yb, dyb, "b_sb")
    dz = join_dz(dza, dq, dk, dv, "b_dz")
    partial_of("w_mix_in", reducing((), ("w_ckv",), mm_tn_g, n2, dz, N_DEV, "b_dwmixin"))
    dn2 = reducing(("w_mix_in",), ("w_mix_out",), mm_nt_g, dz, full["w_mix_in"], "b_dn2")
    dh1, dh1b, grads["mix_norm"] = reducing((), (("w_mix_in", (0, 512)),), rms_bwd, dn2, h1, gmix, dh2, 0.5, "b_n2")

    partial_of("ffn1_w_out", reducing((), (("w_mix_in", (512, 1024)),), mm_tn, hsw1, dh1b, "b_ffn1_dwout"))
    da1 = reducing(("ffn1_w_out",), (("w_mix_in", (1536, 512)),), mm_swiglu_bwd, dh1b, full["ffn1_w_out"], a1,
                   "b_ffn1_da")
    half = D // 2
    partial_of("ffn1_w_in_a", reducing((), (("ffn1_w_out", (0, 352)),), mm_tn_g, n1, da1, N_DEV, "b_ffn1_dwin_a",
                                       cols=(0, half)))
    partial_of("ffn1_w_in_b", reducing(("ffn1_w_in_a",), (("ffn1_w_out", (352, 352)),), mm_tn_g, n1, da1, N_DEV,
                                       "b_ffn1_dwin_b", cols=(half, half)))
    dn1 = reducing(("ffn1_w_in_b",), (dict(sums="ffn1_w_in_a", rows=(0, half), to="ffn1_w_in", out_rows=D),),
                   mm_nt_g, da1, full["ffn1_w_in"], "b_ffn1_dn")
    (dx, _, grads["ffn1_norm"]) = reducing((), (dict(sums="ffn1_w_in_b", rows=(half, half), from_row=0, to="ffn1_w_in"),),
                                           rms_bwd, dn1, xs, g1, dh1, 1.0, "b_n1")

    out_g, out_d, out_m, out_v = {}, {}, {}, {}
    for k in _BIG:
        res = adamw(recv[k], shard2d[k], mom[k].reshape(shard2d[k].shape), var[k].reshape(shard2d[k].shape), f"adamw_{k}")
        out_g[k], out_d[k], out_m[k], out_v[k] = (t.reshape(wts[k].shape) for t in res)

    pack = lambda d: jnp.concatenate([_rows128(d[k]) for k in _SMALL] + [jnp.zeros((_SMALL_PAD, 128), F32)], axis=0)
    small_part = pack(grads)
    (small_sibling,) = run_comm(comm_pairs([small_part]), "comm_pairs_small")
    small_pair = add2(small_part, small_sibling, "pair_sum_small")
    (small_all,) = run_comm(comm_chips([small_pair]), "comm_chips_small")

    res = adamw(small_all, pack(wts), pack(mom), pack(var), "adamw_small")
    row = 0
    for k in _SMALL:
        nrow = wts[k].size // 128
        for dst, t in zip((out_g, out_d, out_m, out_v), res):
            dst[k] = t[row:row + nrow].reshape(wts[k].shape)
        row += nrow

    loss = lax.psum(loss_part[0, 0], ("x", "y", "c"))
    grad_x = dx.reshape(x.shape)
    return (loss, grad_x, *[out_g[k] for k in _ORDER], *[out_d[k] for k in _ORDER],
            *[out_m[k] for k in _ORDER], *[out_v[k] for k in _ORDER])
```

```python
import functools
import math

import jax
import jax.numpy as jnp
from jax import lax
from jax.experimental import pallas as pl
from jax.experimental.pallas import tpu as pltpu

F32 = jnp.float32
BF16 = jnp.bfloat16

N_DEV = 8
D_MODEL = 2048
D_FF = 5632
W_A = 1024
G_A = 8
GA_DIM = 128
SGU_BLOCK = 128
CHUNK = 64
H_B = 8
DH_B = 128
Q_BLOCK = 128
X_HEADS = 4
X_DH = 512
N_MEM = 256
EPS = 1e-6

ADAM_LR = 0.001
ADAM_B1 = 0.9
ADAM_B2 = 0.999
ADAM_EPS = 1e-08
ADAM_WD = 0.01
ADAM_STEP = 10

VMEM_LIMIT = 56 * 2**20
ROW_TILE = 256

MESH = pl.DeviceIdType.MESH
ANY = pl.BlockSpec(memory_space=pl.ANY)

_NT = (((1,), (1,)), ((), ()))
_TN = (((0,), (0,)), ((), ()))


def _params(*sem):
    return pltpu.CompilerParams(dimension_semantics=sem, vmem_limit_bytes=VMEM_LIMIT)


def _zeros(ref):
    return jnp.zeros(ref.shape, ref.dtype)


def _pcall(comm, body, *, name, grid, in_specs, out_specs, out_shape, compiler_params, scratch_shapes=()):
    if comm is None:
        return pl.pallas_call(body, name=name, grid=grid, in_specs=in_specs, out_specs=out_specs, out_shape=out_shape,
                              scratch_shapes=list(scratch_shapes), compiler_params=compiler_params)
    multi = isinstance(out_shape, (list, tuple))
    out_shapes = list(out_shape) if multi else [out_shape]
    out_specs_l = list(out_specs) if multi else [out_specs]
    n_in, n_out, n_scr = len(in_specs), len(out_shapes), len(scratch_shapes)
    n_cin, n_cout = len(comm.arrays), len(comm.out_shapes)

    def with_comm(*refs):
        ins, refs = refs[:n_in], refs[n_in:]
        cins, refs = refs[:n_cin], refs[n_cin:]
        outs, refs = refs[:n_out], refs[n_out:]
        couts, refs = refs[:n_cout], refs[n_cout:]
        scr, sems = refs[:n_scr], refs[n_scr:]
        first = functools.reduce(jnp.logical_and, [pl.program_id(a) == 0 for a in range(len(grid))])
        last = functools.reduce(jnp.logical_and, [pl.program_id(a) == grid[a] - 1 for a in range(len(grid))])
        pl.when(first)(lambda: comm.start(cins, couts, sems))
        body(*ins, *outs, *scr)
        pl.when(last)(lambda: comm.finish(cins, couts, sems))

    call = pl.pallas_call(
        with_comm, name=name, grid=grid, in_specs=list(in_specs) + [ANY] * n_cin,
        out_specs=out_specs_l + [ANY] * n_cout, out_shape=out_shapes + comm.out_shapes,
        scratch_shapes=list(scratch_shapes) + comm.sem_shapes(), compiler_params=_params(*(("arbitrary",) * len(grid))),
        input_output_aliases={n_in + i: n_out + j for i, j in comm.aliases.items()})

    def run(*args):
        res = call(*args, *comm.arrays)
        main = res[:n_out]
        return (list(main) if multi else main[0]), list(res[n_out:])

    return run


def _dot(a, b):
    return jnp.dot(a, b, preferred_element_type=F32)


def _dot_nt(a, b):
    return lax.dot_general(a, b, _NT, preferred_element_type=F32)


def _dot_tn(a, b):
    return lax.dot_general(a, b, _TN, preferred_element_type=F32)


def mm_nn_g(a, bg, out_dtype, name, tm=512, comm=None):
    M, K = a.shape
    G, _, n = bg.shape
    tm = min(tm, M)

    def body(a_ref, b_ref, o_ref):
        o_ref[...] = _dot(a_ref[...], b_ref[...]).astype(o_ref.dtype)

    return _pcall(
        comm, body, name=name, grid=(G, M // tm),
        in_specs=[pl.BlockSpec((tm, K), lambda g, m: (m, 0)),
                  pl.BlockSpec((None, K, n), lambda g, m: (g, 0, 0))],
        out_specs=pl.BlockSpec((tm, n), lambda g, m: (m, g)),
        out_shape=jax.ShapeDtypeStruct((M, G * n), out_dtype),
        compiler_params=_params("parallel", "parallel"),
    )(a, bg)


def mm_swiglu_g(a, bg, name, tm=512, comm=None):
    M, K = a.shape
    G, _, n = bg.shape
    half = G // 2
    tm = min(tm, M)

    def body(a_ref, bgate_ref, bup_ref, gu_ref, h_ref):
        av = a_ref[...]
        gate = _dot(av, bgate_ref[...])
        up = _dot(av, bup_ref[...])
        gu_ref[0] = gate.astype(BF16)
        gu_ref[1] = up.astype(BF16)
        h_ref[...] = (gate * _sigmoid(gate) * up).astype(BF16)

    return _pcall(
        comm, body, name=name, grid=(half, M // tm),
        in_specs=[pl.BlockSpec((tm, K), lambda p, m: (m, 0)),
                  pl.BlockSpec((None, K, n), lambda p, m: (p, 0, 0)),
                  pl.BlockSpec((None, K, n), lambda p, m: (p + half, 0, 0))],
        out_specs=[pl.BlockSpec((2, tm, n), lambda p, m: (0, m, p)), pl.BlockSpec((tm, n), lambda p, m: (m, p))],
        out_shape=[jax.ShapeDtypeStruct((2, M, half * n), BF16), jax.ShapeDtypeStruct((M, half * n), BF16)],
        compiler_params=_params("parallel", "parallel"),
    )(a, bg, bg)


def mm_swiglu_bwd(dy, w_out, gate_up, name, tk=512, comm=None):
    M, N = dy.shape
    F = w_out.shape[0]

    def body(dy_ref, w_ref, gu_ref, o_ref):
        dh = _dot_nt(dy_ref[...], w_ref[...])
        gt = gu_ref[0].astype(F32)
        up = gu_ref[1].astype(F32)
        sg = _sigmoid(gt)
        o_ref[0] = (dh * up * (sg * (1.0 + gt * (1.0 - sg)))).astype(BF16)
        o_ref[1] = (dh * (gt * sg)).astype(BF16)

    planes = pl.BlockSpec((2, M, tk), lambda k: (0, 0, k))
    return _pcall(
        comm, body, name=name, grid=(F // tk,),
        in_specs=[pl.BlockSpec((M, N), lambda k: (0, 0)), pl.BlockSpec((tk, N), lambda k: (k, 0)), planes],
        out_specs=planes,
        out_shape=jax.ShapeDtypeStruct((2, M, F), BF16),
        compiler_params=_params("parallel"),
    )(dy, w_out, gate_up)


def _shard_cols_spec(dy, G, rows, index):
    if dy.ndim == 2:
        n = dy.shape[1] // G
        return pl.BlockSpec((rows, n), lambda *ids: index(*ids)), n
    half = G // 2
    n = dy.shape[2] // half

    def planes(*ids):
        r, g = index(*ids)
        return g // half, r, g % half

    return pl.BlockSpec((None, rows, n), planes), n


def mm_nn(a, b, out_dtype, name, tm=512, tn=1024, scale=1.0, res=None, comm=None):
    M, K = a.shape
    _, N = b.shape
    tm, tn = min(tm, M), min(tn, N)

    def body(*refs):
        if res is None:
            a_ref, b_ref, o_ref = refs
            acc = _dot(a_ref[...], b_ref[...])
            o_ref[...] = (acc * scale if scale != 1.0 else acc).astype(o_ref.dtype)
        else:
            a_ref, b_ref, r_ref, o_ref = refs
            o_ref[...] = (r_ref[...] + scale * _dot(a_ref[...], b_ref[...])).astype(o_ref.dtype)

    in_specs = [pl.BlockSpec((tm, K), lambda n, m: (m, 0)),
                pl.BlockSpec((K, tn), lambda n, m: (0, n))]
    args = [a, b]
    if res is not None:
        in_specs.append(pl.BlockSpec((tm, tn), lambda n, m: (m, n)))
        args.append(res)
    return _pcall(
        comm, body, name=name, grid=(N // tn, M // tm),
        in_specs=in_specs,
        out_specs=pl.BlockSpec((tm, tn), lambda n, m: (m, n)),
        out_shape=jax.ShapeDtypeStruct((M, N), out_dtype),
        compiler_params=_params("parallel", "parallel"),
    )(*args)


def mm_nt_g(dy, bg, name, tm=512, comm=None):
    M = dy.shape[-2]
    G, K, n = bg.shape
    tm = min(tm, M)
    dy_spec, _ = _shard_cols_spec(dy, G, tm, lambda m, g: (m, g))

    def body(dy_ref, b_ref, o_ref):
        part = _dot_nt(dy_ref[...], b_ref[...])

        @pl.when(pl.program_id(1) == 0)
        def _():
            o_ref[...] = part

        @pl.when(pl.program_id(1) > 0)
        def _():
            o_ref[...] += part

    return _pcall(
        comm, body, name=name, grid=(M // tm, G),
        in_specs=[dy_spec, pl.BlockSpec((None, K, n), lambda m, g: (g, 0, 0))],
        out_specs=pl.BlockSpec((tm, K), lambda m, g: (m, 0)),
        out_shape=jax.ShapeDtypeStruct((M, K), F32),
        compiler_params=_params("parallel", "arbitrary"),
    )(dy, bg)


def mm_nt(dy, b, out_dtype, name, tk=512, comm=None):
    M, N = dy.shape
    K, _ = b.shape

    def body(dy_ref, b_ref, o_ref):
        o_ref[...] = _dot_nt(dy_ref[...], b_ref[...]).astype(o_ref.dtype)

    return _pcall(
        comm, body, name=name, grid=(K // tk,),
        in_specs=[pl.BlockSpec((M, N), lambda k: (0, 0)),
                  pl.BlockSpec((tk, N), lambda k: (k, 0))],
        out_specs=pl.BlockSpec((M, tk), lambda k: (0, k)),
        out_shape=jax.ShapeDtypeStruct((M, K), out_dtype),
        compiler_params=_params("parallel"),
    )(dy, b)


def mm_tn_g(x, dy, G, name, tk=512, cols=None, comm=None):
    M, K = x.shape
    k0, K = cols if cols is not None else (0, K)
    dy_spec, n = _shard_cols_spec(dy, G, M, lambda g, k: (0, g))

    def body(x_ref, dy_ref, o_ref):
        o_ref[...] = _dot_tn(x_ref[...], dy_ref[...]).astype(o_ref.dtype)

    return _pcall(
        comm, body, name=name, grid=(G, K // tk),
        in_specs=[pl.BlockSpec((M, tk), lambda g, k: (0, k + k0 // tk)), dy_spec],
        out_specs=pl.BlockSpec((None, tk, n), lambda g, k: (g, k, 0)),
        out_shape=jax.ShapeDtypeStruct((G, K, n), BF16),
        compiler_params=_params("parallel", "parallel"),
    )(x, dy)


def mm_tn(x, dy, name, tk=512, comm=None):
    M, K = x.shape
    _, N = dy.shape

    def body(x_ref, dy_ref, o_ref):
        o_ref[...] = _dot_tn(x_ref[...], dy_ref[...]).astype(o_ref.dtype)

    return _pcall(
        comm, body, name=name, grid=(K // tk,),
        in_specs=[pl.BlockSpec((M, tk), lambda k: (0, k)),
                  pl.BlockSpec((M, N), lambda k: (0, 0))],
        out_specs=pl.BlockSpec((tk, N), lambda k: (k, 0)),
        out_shape=jax.ShapeDtypeStruct((K, N), BF16),
        compiler_params=_params("parallel"),
    )(x, dy)


def _rstd(x):
    return lax.rsqrt(jnp.mean(x * x, axis=-1, keepdims=True) + EPS)


def _rms_bwd(dn, xhat, r, g):
    dxhat = dn * g
    return r * (dxhat - xhat * jnp.mean(dxhat * xhat, axis=-1, keepdims=True))


def _row_spec(tr, width, col=0):
    return pl.BlockSpec((tr, width), lambda i: (i, col))


def _vec_spec(width):
    return pl.BlockSpec((1, width), lambda i: (0, 0))


def _heads_spec(tr):
    return pl.BlockSpec((H_B, tr, DH_B), lambda i: (0, i, 0))


def _heads_to_cols(ref):
    return jnp.concatenate([ref[h] for h in range(H_B)], axis=1)


def split_heads(z, name, comm=None):
    S = z.shape[0]
    tr = min(ROW_TILE, S)
    width = H_B * DH_B
    first = 2 * W_A // width

    def body(q_ref, k_ref, v_ref, o_ref):
        for p, (ref, scale) in enumerate(((q_ref, DH_B ** -0.5), (k_ref, 1.0), (v_ref, 1.0))):
            for h in range(H_B):
                cols = ref[:, h * DH_B:(h + 1) * DH_B]
                o_ref[p * H_B + h] = (cols * scale if scale != 1.0 else cols).astype(BF16)

    return _pcall(
        comm, body, name=name, grid=(S // tr,),
        in_specs=[_row_spec(tr, width, first), _row_spec(tr, width, first + 1), _row_spec(tr, width, first + 2)],
        out_specs=pl.BlockSpec((3 * H_B, tr, DH_B), lambda i: (0, i, 0)),
        out_shape=jax.ShapeDtypeStruct((3 * H_B, S, DH_B), BF16),
        compiler_params=_params("parallel"),
    )(z, z, z)


def join_dz(dza, dq, dk, dv, name, comm=None):
    S, wa = dza.shape
    tr = min(ROW_TILE, S)
    width = H_B * DH_B

    def body(dza_ref, dq_ref, dk_ref, dv_ref, o_ref):
        o_ref[:, :wa] = dza_ref[...]
        for p, ref in enumerate((dq_ref, dk_ref, dv_ref)):
            for h in range(H_B):
                lo = wa + p * width + h * DH_B
                o_ref[:, lo:lo + DH_B] = ref[h]

    return _pcall(
        comm, body, name=name, grid=(S // tr,),
        in_specs=[_row_spec(tr, wa), _heads_spec(tr), _heads_spec(tr), _heads_spec(tr)],
        out_specs=_row_spec(tr, wa + 3 * width),
        out_shape=jax.ShapeDtypeStruct((S, wa + 3 * width), BF16),
        compiler_params=_params("parallel"),
    )(dza, dq, dk, dv)


def rms_fwd(x, g, name, comm=None):
    M, D = x.shape
    tr = min(ROW_TILE, M)

    def body(x_ref, g_ref, o_ref):
        xv = x_ref[...]
        o_ref[...] = (xv * _rstd(xv) * g_ref[...]).astype(o_ref.dtype)

    return _pcall(
        comm, body, name=name, grid=(M // tr,),
        in_specs=[_row_spec(tr, D), _vec_spec(D)],
        out_specs=_row_spec(tr, D),
        out_shape=jax.ShapeDtypeStruct((M, D), BF16),
        compiler_params=_params("parallel"),
    )(x, g)


def rms_bwd(dn, h, g, dres, copy_scale, name, comm=None):
    M, D = h.shape
    tr = min(ROW_TILE, M)
    has_res = dres is not None

    def body(*refs):
        if has_res:
            dn_ref, h_ref, g_ref, dres_ref, dh_ref, dhb_ref, dg_ref = refs
        else:
            dn_ref, h_ref, g_ref, dh_ref, dhb_ref, dg_ref = refs
        hv = h_ref[...]
        r = _rstd(hv)
        xhat = hv * r
        dn = dn_ref[...]
        part = jnp.sum(dn * xhat, axis=0, keepdims=True)

        @pl.when(pl.program_id(0) == 0)
        def _():
            dg_ref[...] = part

        @pl.when(pl.program_id(0) > 0)
        def _():
            dg_ref[...] += part

        dh = _rms_bwd(dn, xhat, r, g_ref[...])
        if has_res:
            dh = dh + dres_ref[...]
        dh_ref[...] = dh
        dhb_ref[...] = (dh * copy_scale if copy_scale != 1.0 else dh).astype(BF16)

    in_specs = [_row_spec(tr, D), _row_spec(tr, D), _vec_spec(D)]
    args = [dn, h, g]
    if has_res:
        in_specs.append(_row_spec(tr, D))
        args.append(dres)
    return _pcall(
        comm, body, name=name, grid=(M // tr,),
        in_specs=in_specs,
        out_specs=[_row_spec(tr, D), _row_spec(tr, D), _vec_spec(D)],
        out_shape=[jax.ShapeDtypeStruct((M, D), F32), jax.ShapeDtypeStruct((M, D), BF16),
                   jax.ShapeDtypeStruct((1, D), F32)],
        compiler_params=_params("arbitrary"),
    )(*args)


def _sigmoid(x):
    return 1.0 / (1.0 + jnp.exp(-x))


def rmscat_fwd(ya, yb, ga, gb, name, comm=None):
    M, W = ya.shape
    tr = min(ROW_TILE, M)

    def body(ya_ref, yb_ref, ga_ref, gb_ref, o_ref):
        a = ya_ref[...]
        b = _heads_to_cols(yb_ref)
        o_ref[:, :W] = (a * _rstd(a) * ga_ref[...]).astype(BF16)
        o_ref[:, W:] = (b * _rstd(b) * gb_ref[...]).astype(BF16)

    return _pcall(
        comm, body, name=name, grid=(M // tr,),
        in_specs=[_row_spec(tr, W), _heads_spec(tr), _vec_spec(W), _vec_spec(W)],
        out_specs=_row_spec(tr, 2 * W),
        out_shape=jax.ShapeDtypeStruct((M, 2 * W), BF16),
        compiler_params=_params("parallel"),
    )(ya, yb, ga, gb)


def rmscat_bwd(dycat, ya, yb, ga, gb, name, comm=None):
    M, W = ya.shape
    tr = min(ROW_TILE, M)

    def body(dc_ref, ya_ref, yb_ref, ga_ref, gb_ref, dya_ref, dyb_ref, dga_ref, dgb_ref):
        first = pl.program_id(0) == 0
        for by_head, y_ref, g_ref, dy_ref, dg_ref, lo in ((False, ya_ref, ga_ref, dya_ref, dga_ref, 0),
                                                          (True, yb_ref, gb_ref, dyb_ref, dgb_ref, W)):
            yv = _heads_to_cols(y_ref) if by_head else y_ref[...]
            r = _rstd(yv)
            xhat = yv * r
            dn = dc_ref[:, lo:lo + W]
            part = jnp.sum(dn * xhat, axis=0, keepdims=True)

            @pl.when(first)
            def _():
                dg_ref[...] = part

            @pl.when(jnp.logical_not(first))
            def _():
                dg_ref[...] += part

            dy = _rms_bwd(dn, xhat, r, g_ref[...])
            if by_head:
                for h in range(H_B):
                    dy_ref[h] = dy[:, h * DH_B:(h + 1) * DH_B]
            else:
                dy_ref[...] = dy

    return _pcall(
        comm, body, name=name, grid=(M // tr,),
        in_specs=[_row_spec(tr, 2 * W), _row_spec(tr, W), _heads_spec(tr), _vec_spec(W), _vec_spec(W)],
        out_specs=[_row_spec(tr, W), _heads_spec(tr), _vec_spec(W), _vec_spec(W)],
        out_shape=[jax.ShapeDtypeStruct((M, W), F32), jax.ShapeDtypeStruct((H_B, M, DH_B), F32),
                   jax.ShapeDtypeStruct((1, W), F32), jax.ShapeDtypeStruct((1, W), F32)],
        compiler_params=_params("arbitrary"),
    )(dycat, ya, yb, ga, gb)


def loss_head(h, target, g, name, comm=None):
    M, D = h.shape
    tr = min(ROW_TILE, M)

    def body(h_ref, t_ref, g_ref, loss_ref, dh_ref, dhb_ref, dg_ref):
        hv = h_ref[...]
        gv = g_ref[...]
        r = _rstd(hv)
        xhat = hv * r
        err = xhat * gv - t_ref[...]
        lsum = jnp.sum(jnp.sum(err * err, axis=1, keepdims=True), axis=0, keepdims=True) * (0.5 / D)
        dy = err * (1.0 / D)
        part = jnp.sum(dy * xhat, axis=0, keepdims=True)

        @pl.when(pl.program_id(0) == 0)
        def _():
            dg_ref[...] = part
            loss_ref[...] = _zeros(loss_ref) + lsum

        @pl.when(pl.program_id(0) > 0)
        def _():
            dg_ref[...] += part
            loss_ref[...] += lsum

        dh = _rms_bwd(dy, xhat, r, gv)
        dh_ref[...] = dh
        dhb_ref[...] = (0.5 * dh).astype(BF16)

    return _pcall(
        comm, body, name=name, grid=(M // tr,),
        in_specs=[_row_spec(tr, D), _row_spec(tr, D), _vec_spec(D)],
        out_specs=[pl.BlockSpec((8, 128), lambda i: (0, 0)), _row_spec(tr, D), _row_spec(tr, D), _vec_spec(D)],
        out_shape=[jax.ShapeDtypeStruct((8, 128), F32), jax.ShapeDtypeStruct((M, D), F32),
                   jax.ShapeDtypeStruct((M, D), BF16), jax.ShapeDtypeStruct((1, D), F32)],
        compiler_params=_params("arbitrary"),
    )(h, target, g)


_GELU_C = math.sqrt(2.0 / math.pi)


def _gelu(x):
    return 0.5 * x * (1.0 + jnp.tanh(_GELU_C * (x + 0.044715 * (x * x * x))))


def _gelu_grad(x):
    t = jnp.tanh(_GELU_C * (x + 0.044715 * (x * x * x)))
    return 0.5 * (1.0 + t) + 0.5 * x * (1.0 - t * t) * (_GELU_C * (1.0 + 3.0 * 0.044715 * (x * x)))


def _sgu_mask():
    t = lax.broadcasted_iota(jnp.int32, (SGU_BLOCK, SGU_BLOCK), 0) // CHUNK
    s = lax.broadcasted_iota(jnp.int32, (SGU_BLOCK, SGU_BLOCK), 1) // CHUNK
    return s <= t


def _layernorm_stats(v):
    mu = jnp.mean(v, axis=-1, keepdims=True)
    cen = v - mu
    rstd = lax.rsqrt(jnp.mean(cen * cen, axis=-1, keepdims=True) + EPS)
    return cen * rstd, rstd


def sgu_fwd(z, ln_g, ln_b, w_s, b_t, name, comm=None):
    S = z.shape[0]

    def body(zu_ref, zv_ref, lg_ref, lb_ref, w_ref, bt_ref, o_ref):
        mask = _sgu_mask()
        for g in range(G_A):
            cols = slice(g * GA_DIM, (g + 1) * GA_DIM)
            u = _gelu(zu_ref[:, cols])
            vhat, _ = _layernorm_stats(_gelu(zv_ref[:, cols]))
            vln = vhat * lg_ref[:, cols] + lb_ref[:, cols]
            w = jnp.where(mask, w_ref[g], 0.0).astype(BF16)
            mixed = _dot(w, vln.astype(BF16)) + bt_ref[:, g:g + 1]
            o_ref[:, cols] = u * mixed

    return _pcall(
        comm, body, name=name, grid=(S // SGU_BLOCK,),
        in_specs=[_row_spec(SGU_BLOCK, W_A, 0), _row_spec(SGU_BLOCK, W_A, 1), _vec_spec(W_A), _vec_spec(W_A),
                  pl.BlockSpec((G_A, SGU_BLOCK, SGU_BLOCK), lambda i: (0, 0, 0)),
                  pl.BlockSpec((SGU_BLOCK, G_A), lambda i: (0, 0))],
        out_specs=_row_spec(SGU_BLOCK, W_A),
        out_shape=jax.ShapeDtypeStruct((S, W_A), F32),
        compiler_params=_params("parallel"),
    )(z, z, ln_g, ln_b, w_s, b_t)


def sgu_bwd(z, dya, ln_g, ln_b, w_s, b_t, name, comm=None):
    S = z.shape[0]
    nblk = S // SGU_BLOCK

    def body(zu_ref, zv_ref, dy_ref, lg_ref, lb_ref, w_ref, bt_ref,
             dz_ref, dlg_ref, dlb_ref, dw_ref, db_ref, dmix_acc):
        step = pl.program_id(0)
        mask = _sgu_mask()

        @pl.when(step == 0)
        def _():
            dlg_ref[...] = _zeros(dlg_ref)
            dlb_ref[...] = _zeros(dlb_ref)
            dw_ref[...] = _zeros(dw_ref)
            dmix_acc[...] = _zeros(dmix_acc)

        for g in range(G_A):
            cols = slice(g * GA_DIM, (g + 1) * GA_DIM)
            zu = zu_ref[:, cols]
            zv = zv_ref[:, cols]
            u = _gelu(zu)
            vhat, rstd = _layernorm_stats(_gelu(zv))
            lg = lg_ref[:, cols]
            vln = (vhat * lg + lb_ref[:, cols]).astype(BF16)
            w = jnp.where(mask, w_ref[g], 0.0)
            mixed = _dot(w.astype(BF16), vln) + bt_ref[:, g:g + 1]
            dy = dy_ref[:, cols]
            du = dy * mixed
            dmixed = dy * u
            dmixed_b = dmixed.astype(BF16)
            dmix_acc[g] += dmixed
            dw_ref[g] += jnp.where(mask, _dot_nt(dmixed_b, vln), 0.0)
            dvln = _dot(w.T.astype(BF16), dmixed_b)
            dlb_ref[:, cols] += jnp.sum(dvln, axis=0, keepdims=True)
            dlg_ref[:, cols] += jnp.sum(dvln * vhat, axis=0, keepdims=True)
            dvhat = dvln * lg
            dv = rstd * (dvhat - jnp.mean(dvhat, axis=-1, keepdims=True)
                         - vhat * jnp.mean(dvhat * vhat, axis=-1, keepdims=True))
            dz_ref[:, cols] = (du * _gelu_grad(zu)).astype(BF16)
            dz_ref[:, W_A + g * GA_DIM:W_A + (g + 1) * GA_DIM] = (dv * _gelu_grad(zv)).astype(BF16)

        @pl.when(step == nblk - 1)
        def _():
            for g in range(G_A):
                db_ref[g] = jnp.sum(dmix_acc[g], axis=1, keepdims=True)

    whole3 = lambda shape: pl.BlockSpec(shape, lambda i: (0, 0, 0))
    return _pcall(
        comm, body, name=name, grid=(nblk,),
        in_specs=[_row_spec(SGU_BLOCK, W_A, 0), _row_spec(SGU_BLOCK, W_A, 1), _row_spec(SGU_BLOCK, W_A),
                  _vec_spec(W_A), _vec_spec(W_A), whole3((G_A, SGU_BLOCK, SGU_BLOCK)),
                  pl.BlockSpec((SGU_BLOCK, G_A), lambda i: (0, 0))],
        out_specs=[_row_spec(SGU_BLOCK, 2 * W_A), _vec_spec(W_A), _vec_spec(W_A),
                   whole3((G_A, SGU_BLOCK, SGU_BLOCK)), whole3((G_A, SGU_BLOCK, 1))],
        out_shape=[jax.ShapeDtypeStruct((S, 2 * W_A), BF16), jax.ShapeDtypeStruct((1, W_A), F32),
                   jax.ShapeDtypeStruct((1, W_A), F32), jax.ShapeDtypeStruct((G_A, SGU_BLOCK, SGU_BLOCK), F32),
                   jax.ShapeDtypeStruct((G_A, SGU_BLOCK, 1), F32)],
        scratch_shapes=[pltpu.VMEM((G_A, SGU_BLOCK, SGU_BLOCK), F32)],
        compiler_params=_params("arbitrary"),
    )(z, z, dya, ln_g, ln_b, w_s, b_t)


def _log_sigmoid(z):
    return jnp.minimum(z, 0.0) - jnp.log(1.0 + jnp.exp(-jnp.abs(z)))


def _suffix_sum(x, upper):
    hi = x.astype(BF16)
    lo = (x - hi.astype(F32)).astype(BF16)
    return _dot(hi, upper) + _dot(lo, upper)


SB_ROWS = 1024
_SB_SUB = SB_ROWS // Q_BLOCK


def _sb_upper():
    row = lax.broadcasted_iota(jnp.int32, (Q_BLOCK, Q_BLOCK), 0)
    col = lax.broadcasted_iota(jnp.int32, (Q_BLOCK, Q_BLOCK), 1)
    return (row > col).astype(BF16)


def _sb_sweep(step, tile):
    for r in reversed(range(_SB_SUB)):
        tile(step * _SB_SUB + r, r * Q_BLOCK)

    def group(g, _):
        base = (step - 1 - g) * _SB_SUB
        for r in reversed(range(_SB_SUB)):
            tile(base + r, None)
        return 0

    lax.fori_loop(0, step, group, 0)


def _sb_causal(n):
    return lax.broadcasted_iota(jnp.int32, (n, Q_BLOCK), 1) < lax.broadcasted_iota(jnp.int32, (n, Q_BLOCK), 0)


def _sb_rows_spec():
    return pl.BlockSpec((None, SB_ROWS, DH_B), lambda h, i: (h, i, 0))


def _sb_head_spec(S, part=0):
    return pl.BlockSpec((None, S, DH_B), lambda h, i: (part * H_B + h, 0, 0))


def sb_fwd(qkv, name, comm=None):
    S = qkv.shape[1]

    def body(q_b, k_ref, v_ref, o_ref, c_l1m):
        step = pl.program_id(1)
        o_ref[...] = _zeros(o_ref)
        c_l1m[...] = _zeros(c_l1m)
        upper = _sb_upper()

        def tile(j, row0):
            rq = slice(row0 or 0, SB_ROWS)
            causal = None if row0 is None else _sb_causal(SB_ROWS - row0)
            rows = pl.ds(pl.multiple_of(j * Q_BLOCK, Q_BLOCK), Q_BLOCK)
            zz = _dot_nt(q_b[rq, :], k_ref[rows, :])
            lb = _log_sigmoid(zz)
            l1m = lb - zz
            if causal is not None:
                l1m = jnp.where(causal, l1m, 0.0)
            a = jnp.exp(lb + _suffix_sum(l1m, upper) + c_l1m[rq, :])
            if causal is not None:
                a = jnp.where(causal, a, 0.0)
            o_ref[rq, :] += _dot(a.astype(BF16), v_ref[rows, :])
            c_l1m[rq, :] += jnp.sum(l1m, axis=1, keepdims=True)

        _sb_sweep(step, tile)

    return _pcall(
        comm, body, name=name, grid=(H_B, S // SB_ROWS),
        in_specs=[_sb_rows_spec(), _sb_head_spec(S, 1), _sb_head_spec(S, 2)],
        out_specs=_sb_rows_spec(),
        out_shape=jax.ShapeDtypeStruct((H_B, S, DH_B), F32),
        scratch_shapes=[pltpu.VMEM((SB_ROWS, 1), F32)],
        compiler_params=_params("parallel", "parallel"),
    )(qkv, qkv, qkv)


def sb_bwd(qkv, out, dout, name, comm=None):
    S = qkv.shape[1]
    nstep = S // SB_ROWS
    scale = DH_B ** -0.5

    def body(q_b, k_ref, v_ref, o_ref, do_ref, dq_ref, dk_ref, dv_ref,
             dq_acc, dkt_acc, dvt_acc, do_b, qt_b, dot_b, g_left, c_l1m):
        step = pl.program_id(1)

        @pl.when(step == 0)
        def _():
            dkt_acc[...] = _zeros(dkt_acc)
            dvt_acc[...] = _zeros(dvt_acc)

        do_b[...] = do_ref[...].astype(BF16)
        qt_b[...] = q_b[...].astype(F32).T.astype(BF16)
        dot_b[...] = do_ref[...].T.astype(BF16)
        g_left[...] = jnp.sum(do_b[...].astype(F32) * o_ref[...], axis=1, keepdims=True)
        dq_acc[...] = _zeros(dq_acc)
        c_l1m[...] = _zeros(c_l1m)
        upper = _sb_upper()

        def tile(j, row0):
            rq = slice(row0 or 0, SB_ROWS)
            causal = None if row0 is None else _sb_causal(SB_ROWS - row0)
            rows = pl.ds(pl.multiple_of(j * Q_BLOCK, Q_BLOCK), Q_BLOCK)
            q, do_t = q_b[rq, :], do_b[rq, :]
            k_j = k_ref[rows, :]
            zz = _dot_nt(q, k_j)
            lb = _log_sigmoid(zz)
            l1m = lb - zz
            if causal is not None:
                l1m = jnp.where(causal, l1m, 0.0)
            a = jnp.exp(lb + _suffix_sum(l1m, upper) + c_l1m[rq, :])
            if causal is not None:
                a = jnp.where(causal, a, 0.0)
            a_b = a.astype(BF16)
            dvt_acc[:, rows] += _dot(dot_b[:, rq], a_b)
            gmat = a_b.astype(F32) * _dot_nt(do_t, v_ref[rows, :])
            before = g_left[rq, :] - gmat - _suffix_sum(gmat, upper)
            sig = jnp.exp(lb)
            dz = gmat * (1.0 - sig) - sig * before
            if causal is not None:
                dz = jnp.where(causal, dz, 0.0)
            dz_b = dz.astype(BF16)
            dkt_acc[:, rows] += _dot(qt_b[:, rq], dz_b)
            dq_acc[rq, :] += _dot(dz_b, k_j)
            c_l1m[rq, :] += jnp.sum(l1m, axis=1, keepdims=True)
            g_left[rq, :] -= jnp.sum(gmat, axis=1, keepdims=True)

        _sb_sweep(step, tile)
        dq_ref[...] = (dq_acc[...] * scale).astype(BF16)

        @pl.when(step == nstep - 1)
        def _():
            dk_ref[...] = dkt_acc[...].T.astype(BF16)
            dv_ref[...] = dvt_acc[...].T.astype(BF16)

    out_sds = jax.ShapeDtypeStruct((H_B, S, DH_B), BF16)
    return _pcall(
        comm, body, name=name, grid=(H_B, nstep),
        in_specs=[_sb_rows_spec(), _sb_head_spec(S, 1), _sb_head_spec(S, 2), _sb_rows_spec(), _sb_rows_spec()],
        out_specs=[_sb_rows_spec(), _sb_head_spec(S), _sb_head_spec(S)],
        out_shape=[out_sds, out_sds, out_sds],
        scratch_shapes=[pltpu.VMEM((SB_ROWS, DH_B), F32)] + [pltpu.VMEM((DH_B, S), F32)] * 2
        + [pltpu.VMEM((SB_ROWS, DH_B), BF16)] + [pltpu.VMEM((DH_B, SB_ROWS), BF16)] * 2
        + [pltpu.VMEM((SB_ROWS, 1), F32)] * 2,
        compiler_params=_params("parallel", "arbitrary"),
    )(qkv, qkv, qkv, out, dout)


def _softmax(s):
    e = jnp.exp(s - jnp.max(s, axis=-1, keepdims=True))
    return e / jnp.sum(e, axis=-1, keepdims=True)


def xattn_fwd(qc, kv, name, comm=None):
    S, D = qc.shape
    tr = min(ROW_TILE, S)

    def body(q_ref, kv_ref, o_ref):
        for h in range(X_HEADS):
            cols = slice(h * X_DH, (h + 1) * X_DH)
            p = _softmax(_dot_nt(q_ref[:, cols], kv_ref[:, cols]))
            o_ref[:, cols] = _dot(p.astype(BF16), kv_ref[:, D + h * X_DH:D + (h + 1) * X_DH]).astype(BF16)

    return _pcall(
        comm, body, name=name, grid=(S // tr,),
        in_specs=[_row_spec(tr, D), pl.BlockSpec((N_MEM, 2 * D), lambda i: (0, 0))],
        out_specs=_row_spec(tr, D),
        out_shape=jax.ShapeDtypeStruct((S, D), BF16),
        compiler_params=_params("parallel"),
    )(qc, kv)


def xattn_bwd(qc, kv, do, name, comm=None):
    S, D = qc.shape
    tr = min(ROW_TILE, S)
    nstep = S // tr
    scale = X_DH ** -0.5

    def body(q_ref, kv_ref, do_ref, dq_ref, dkv_ref, acc):
        step = pl.program_id(0)

        @pl.when(step == 0)
        def _():
            acc[...] = _zeros(acc)

        for h in range(X_HEADS):
            cols = slice(h * X_DH, (h + 1) * X_DH)
            vcols = slice(D + h * X_DH, D + (h + 1) * X_DH)
            q = q_ref[:, cols]
            k = kv_ref[:, cols]
            do_h = do_ref[:, cols]
            p = _softmax(_dot_nt(q, k))
            dp = _dot_nt(do_h, kv_ref[:, vcols])
            acc[:, vcols] += _dot_tn(p.astype(BF16), do_h)
            ds = (p * (dp - jnp.sum(p * dp, axis=-1, keepdims=True))).astype(BF16)
            dq_ref[:, cols] = (_dot(ds, k) * scale).astype(BF16)
            acc[:, cols] += _dot_tn(ds, q)

        @pl.when(step == nstep - 1)
        def _():
            dkv_ref[...] = acc[...].astype(BF16)

    whole = pl.BlockSpec((N_MEM, 2 * D), lambda i: (0, 0))
    return _pcall(
        comm, body, name=name, grid=(nstep,),
        in_specs=[_row_spec(tr, D), whole, _row_spec(tr, D)],
        out_specs=[_row_spec(tr, D), whole],
        out_shape=[jax.ShapeDtypeStruct((S, D), BF16), jax.ShapeDtypeStruct((N_MEM, 2 * D), BF16)],
        scratch_shapes=[pltpu.VMEM((N_MEM, 2 * D), F32)],
        compiler_params=_params("arbitrary"),
    )(qc, kv, do)


def _row_tile(rows, cap=128):
    return max(t for t in range(16, cap + 1, 16) if rows % t == 0)


def cast_bf16(w, name, comm=None):
    R, C = w.shape
    tr = _row_tile(R, 256)

    def body(w_ref, o_ref):
        o_ref[...] = w_ref[...].astype(BF16)

    return _pcall(
        comm, body, name=name, grid=(R // tr,),
        in_specs=[_row_spec(tr, C)], out_specs=_row_spec(tr, C),
        out_shape=jax.ShapeDtypeStruct((R, C), BF16),
        compiler_params=_params("parallel"),
    )(w)


def adamw(parts, w, m, v, name, comm=None):
    R, C = w.shape
    n_parts = parts.shape[0]
    tr = _row_tile(R, 256)
    c1 = 1.0 - ADAM_B1 ** ADAM_STEP
    c2 = 1.0 - ADAM_B2 ** ADAM_STEP

    def body(p_ref, w_ref, m_ref, v_ref, g_ref, d_ref, mo_ref, vo_ref):
        g = p_ref[0].astype(F32)
        for p in range(1, n_parts):
            g = g + p_ref[p].astype(F32)
        m_new = ADAM_B1 * m_ref[...] + (1.0 - ADAM_B1) * g
        v_new = ADAM_B2 * v_ref[...] + (1.0 - ADAM_B2) * (g * g)
        g_ref[...] = g
        mo_ref[...] = m_new
        vo_ref[...] = v_new
        d_ref[...] = -ADAM_LR * ((m_new / c1) / (jnp.sqrt(v_new / c2) + ADAM_EPS) + ADAM_WD * w_ref[...])

    spec = _row_spec(tr, C)
    sds = jax.ShapeDtypeStruct((R, C), F32)
    return _pcall(
        comm, body, name=name, grid=(R // tr,),
        in_specs=[pl.BlockSpec((n_parts, tr, C), lambda i: (0, i, 0)), spec, spec, spec],
        out_specs=[spec, spec, spec, spec],
        out_shape=[sds, sds, sds, sds],
        compiler_params=_params("parallel"),
    )(parts, w, m, v)


def pair_sum(parts, from_sibling, core, name):
    _, R, C = parts.shape
    tr = _row_tile(R, 1024)

    def body(core_ref, p_ref, s_ref, o_ref):
        o_ref[...] = (p_ref[...].astype(F32) + s_ref[...].astype(F32)).astype(o_ref.dtype)

    return pl.pallas_call(
        body, name=name,
        grid_spec=pltpu.PrefetchScalarGridSpec(
            num_scalar_prefetch=1, grid=(4, R // tr),
            in_specs=[pl.BlockSpec((None, tr, C), lambda q, i, core_ref: (2 * q + core_ref[0], i, 0)),
                      pl.BlockSpec((None, tr, C), lambda q, i, core_ref: (q, i, 0))],
            out_specs=pl.BlockSpec((None, tr, C), lambda q, i, core_ref: (q, i, 0))),
        out_shape=jax.ShapeDtypeStruct((4, R, C), BF16),
        compiler_params=_params("parallel", "parallel"),
    )(core, parts, from_sibling)


def add2(a, b, name, comm=None):
    R, C = a.shape
    tr = _row_tile(R, 256)

    def body(a_ref, b_ref, o_ref):
        o_ref[...] = a_ref[...] + b_ref[...]

    spec = _row_spec(tr, C)
    return _pcall(
        comm, body, name=name, grid=(R // tr,), in_specs=[spec, spec], out_specs=spec,
        out_shape=jax.ShapeDtypeStruct((R, C), F32), compiler_params=_params("parallel"),
    )(a, b)


def _place():
    return lax.axis_index("x"), lax.axis_index("y"), lax.axis_index("c")


class Comm:
    def __init__(self, arrays, out_shapes, n_remote, n_local, start, finish, aliases=None):
        self.arrays, self.out_shapes = list(arrays), list(out_shapes)
        self.n_remote, self.n_local = n_remote, max(n_local, 1)
        self.start, self.finish = start, finish
        self.aliases = dict(aliases or {})
        self.sizes = [len(self.out_shapes)]

    def sem_shapes(self):
        return [pltpu.SemaphoreType.DMA((self.n_remote,)), pltpu.SemaphoreType.DMA((self.n_remote,)),
                pltpu.SemaphoreType.DMA((self.n_local,))]


class _Shifted:
    def __init__(self, ref, offset):
        self.ref, self.offset = ref, offset

    @property
    def at(self):
        return self

    def __getitem__(self, k):
        return self.ref.at[self.offset + k]


def merge_comms(comms):
    comms = [c for c in comms if c is not None]
    if not comms:
        return None

    def each(method):
        def run(ins, outs, sems):
            i = o = r = l = 0
            for c in comms:
                sub = (_Shifted(sems[0], r), _Shifted(sems[1], r), _Shifted(sems[2], l))
                getattr(c, method)(ins[i:i + len(c.arrays)], outs[o:o + len(c.out_shapes)], sub)
                i, o, r, l = i + len(c.arrays), o + len(c.out_shapes), r + c.n_remote, l + c.n_local
        return run

    aliases, i, o = {}, 0, 0
    for c in comms:
        aliases.update({i + a: o + b for a, b in c.aliases.items()})
        i, o = i + len(c.arrays), o + len(c.out_shapes)
    merged = Comm([a for c in comms for a in c.arrays], [s for c in comms for s in c.out_shapes],
                  sum(c.n_remote for c in comms), sum(c.n_local for c in comms), each("start"), each("finish"), aliases)
    merged.sizes = [len(c.out_shapes) for c in comms]
    return merged


def split_results(comm, results):
    out, i = [], 0
    for n in comm.sizes:
        out.append(list(results[i:i + n]))
        i += n
    return out


def run_comm(comm, name):
    n_in, n_out = len(comm.arrays), len(comm.out_shapes)

    def body(*refs):
        ins, outs, sems = refs[:n_in], refs[n_in:n_in + n_out], refs[n_in + n_out:]
        comm.start(ins, outs, sems)
        comm.finish(ins, outs, sems)

    return pl.pallas_call(
        body, name=name, in_specs=[ANY] * n_in, out_specs=[ANY] * n_out, out_shape=comm.out_shapes,
        scratch_shapes=comm.sem_shapes(), input_output_aliases=comm.aliases,
    )(*comm.arrays)


def _remote(src, dst, sems, k, to):
    return pltpu.make_async_remote_copy(src_ref=src, dst_ref=dst, send_sem=sems[0].at[k], recv_sem=sems[1].at[k],
                                        device_id=to, device_id_type=MESH)


_AG_COPIES = 13


def comm_all_gather(shards, rows=None, into=None):
    n = len(shards)
    row0, nrows = rows if rows is not None else (0, None)

    def parties():
        x, y, c = _place()
        return (x, y, c), (x, y, 1 - c), [(1 - x, y), (x, 1 - y), (1 - x, 1 - y)]

    def span(w, half=None):
        count = nrows if nrows is not None else shards[w].shape[0]
        if half is None:
            return pl.ds(row0, count)
        return pl.ds(row0 + half * (count // 2), count // 2)

    def slab(outs, w, dev, half=None):
        return outs[w].at[4 * dev[0] + 2 * dev[1] + dev[2], span(w, half)]

    def own(ins, outs, sems):
        me, sibling, chips = parties()
        local = [pltpu.make_async_copy(ins[w].at[span(w)], slab(outs, w, me), sems[2].at[w]) for w in range(n)]
        first = []
        for w in range(n):
            k = _AG_COPIES * w
            first.append(_remote(ins[w].at[span(w)], slab(outs, w, me), sems, k, sibling))
            first += [_remote(ins[w].at[span(w, h)], slab(outs, w, me, h), sems, k + 1 + 2 * j + h, (*chip, me[2]))
                      for h in range(2) for j, chip in enumerate(chips)]
        return local, first

    def start(ins, outs, sems):
        local, first = own(ins, outs, sems)
        for cp in local + first:
            cp.start()

    def finish(ins, outs, sems):
        me, sibling, chips = parties()
        local, first = own(ins, outs, sems)
        passed = []
        for w in range(n):
            k = _AG_COPIES * w
            for h in range(2):
                for j, chip in enumerate(chips):
                    got = slab(outs, w, (*chip, me[2]), h)
                    _remote(got, got, sems, k + 1 + 2 * j + h, me).wait_recv()
                    cp = _remote(got, got, sems, k + 7 + 2 * j + h, sibling)
                    cp.start()
                    passed.append(cp)
        for w in range(n):
            k = _AG_COPIES * w
            got = slab(outs, w, sibling)
            _remote(got, got, sems, k, me).wait_recv()
            for h in range(2):
                for j, chip in enumerate(chips):
                    got = slab(outs, w, (*chip, sibling[2]), h)
                    _remote(got, got, sems, k + 7 + 2 * j + h, me).wait_recv()
        for cp in first + passed:
            cp.wait_send()
        for cp in local:
            cp.wait()

    out_shapes = [jax.ShapeDtypeStruct((N_DEV,) + s.shape, s.dtype) for s in shards]
    arrays = list(shards) + (list(into) if into is not None else [])
    aliases = {n + w: w for w in range(n)} if into is not None else None
    return Comm(arrays, out_shapes, _AG_COPIES * n, n, start, finish, aliases)


def comm_pairs(items):
    slabbed = [a.ndim == 3 for a in items]
    first = [sum(4 if s else 1 for s in slabbed[:w]) for w in range(len(items))]

    def copies(ins, outs, sems):
        x, y, c = _place()
        sibling = (x, y, 1 - c)
        cps = []
        for w, s in enumerate(slabbed):
            if s:
                cps += [_remote(ins[w].at[2 * q + (1 - c)], outs[w].at[q], sems, first[w] + q, sibling) for q in range(4)]
            else:
                cps.append(_remote(ins[w], outs[w], sems, first[w], sibling))
        return cps

    def start(ins, outs, sems):
        for cp in copies(ins, outs, sems):
            cp.start()

    def finish(ins, outs, sems):
        for cp in copies(ins, outs, sems):
            cp.wait()

    out_shapes = [jax.ShapeDtypeStruct(((4,) + a.shape[1:]) if s else a.shape, a.dtype) for a, s in zip(items, slabbed)]
    return Comm(items, out_shapes, sum(4 if s else 1 for s in slabbed), 0, start, finish)


def comm_chips(items, rows=None, into=None, from_row=None, out_rows=None):
    n = len(items)
    slabbed = [a.ndim == 3 for a in items]

    def span(w, source=False):
        if rows is None:
            return pl.ds(0, items[w].shape[-2])
        return pl.ds(from_row if source and from_row is not None else rows[0], rows[1])

    def copies(ins, outs, sems):
        x, y, c = _place()
        mine = 2 * x + y
        local = [pltpu.make_async_copy(ins[w].at[mine, span(w, True)] if slabbed[w] else ins[w].at[span(w, True)],
                                       outs[w].at[mine, span(w)], sems[2].at[w]) for w in range(n)]
        remote = []
        for w in range(n):
            for j, (px, py) in enumerate([(1 - x, y), (x, 1 - y), (1 - x, 1 - y)]):
                src = ins[w].at[2 * px + py, span(w, True)] if slabbed[w] else ins[w].at[span(w, True)]
                remote.append(_remote(src, outs[w].at[mine, span(w)], sems, 3 * w + j, (px, py, c)))
        return local, remote

    def start(ins, outs, sems):
        local, remote = copies(ins, outs, sems)
        for cp in local + remote:
            cp.start()

    def finish(ins, outs, sems):
        local, remote = copies(ins, outs, sems)
        for cp in remote + local:
            cp.wait()

    def result(a):
        tall = a.shape[:-2] + (out_rows if out_rows is not None else a.shape[-2], a.shape[-1])
        return jax.ShapeDtypeStruct(tall if a.ndim == 3 else (4,) + tall, a.dtype)

    if into is None:
        return Comm(items, [result(a) for a in items], 3 * n, n, start, finish)
    out_shapes = [jax.ShapeDtypeStruct(b.shape, b.dtype) for b in into]
    return Comm(list(items) + list(into), out_shapes, 3 * n, n, start, finish, {n + w: w for w in range(n)})


_SMALL = ("ffn1_norm", "mix_norm", "ln_v_gain", "ln_v_bias", "spatial_w", "spatial_b", "gnorm_a", "gnorm_b",
          "cross_norm", "mem_norm", "ffn2_norm", "final_norm")
_BIG = ("ffn1_w_in", "ffn1_w_out", "w_mix_in", "w_mix_out", "w_cq", "w_ckv", "w_co", "ffn2_w_in", "ffn2_w_out")
_COL_SHARDED = ("ffn1_w_in", "w_mix_in", "w_ckv", "ffn2_w_in")
_ORDER = ("ffn1_norm", "ffn1_w_in", "ffn1_w_out", "mix_norm", "w_mix_in", "ln_v_gain", "ln_v_bias", "spatial_w",
          "spatial_b", "gnorm_a", "gnorm_b", "w_mix_out", "cross_norm", "mem_norm", "w_cq", "w_ckv", "w_co",
          "ffn2_norm", "ffn2_w_in", "ffn2_w_out", "final_norm")


_SMALL_PAD = 120


def _rows128(a):
    return a.reshape(-1, 128)


def kernel(x, mem, ffn1_norm, ffn1_w_in, ffn1_w_out, mix_norm, w_mix_in, ln_v_gain, ln_v_bias, spatial_w, spatial_b, gnorm_a, gnorm_b, w_mix_out, cross_norm, mem_norm, w_cq, w_ckv, w_co, ffn2_norm, ffn2_w_in, ffn2_w_out, final_norm, loss_target, m_ffn1_norm, m_ffn1_w_in, m_ffn1_w_out, m_mix_norm, m_w_mix_in, m_ln_v_gain, m_ln_v_bias, m_spatial_w, m_spatial_b, m_gnorm_a, m_gnorm_b, m_w_mix_out, m_cross_norm, m_mem_norm, m_w_cq, m_w_ckv, m_w_co, m_ffn2_norm, m_ffn2_w_in, m_ffn2_w_out, m_final_norm, v_ffn1_norm, v_ffn1_w_in, v_ffn1_w_out, v_mix_norm, v_w_mix_in, v_ln_v_gain, v_ln_v_bias, v_spatial_w, v_spatial_b, v_gnorm_a, v_gnorm_b, v_w_mix_out, v_cross_norm, v_mem_norm, v_w_cq, v_w_ckv, v_w_co, v_ffn2_norm, v_ffn2_w_in, v_ffn2_w_out, v_final_norm):
    given = dict(locals())
    wts = {k: given[k] for k in _ORDER}
    mom = {k: given["m_" + k] for k in _ORDER}
    var = {k: given["v_" + k] for k in _ORDER}

    D = D_MODEL
    xs = x.reshape(-1, D)
    mems = mem.reshape(-1, D)
    tgt = loss_target.reshape(-1, D)
    vec = lambda a: a.reshape(1, -1)
    g1, gmix, gcross, gmem, g2, gfin = (vec(wts[k]) for k in
                                        ("ffn1_norm", "mix_norm", "cross_norm", "mem_norm", "ffn2_norm", "final_norm"))
    ln_g, ln_b, ga, gb = (vec(wts[k]) for k in ("ln_v_gain", "ln_v_bias", "gnorm_a", "gnorm_b"))
    w_s = spatial_w.reshape(G_A, SGU_BLOCK, SGU_BLOCK)
    b_t = spatial_b.reshape(G_A, SGU_BLOCK).T

    shard2d = {k: wts[k].reshape(wts[k].shape[1:]) for k in _BIG}
    shard_b = {k: cast_bf16(shard2d[k], f"cast_{k}") for k in _BIG}
    full = {}

    landing, rows_done = {}, {}

    def gathering(pieces, fn, *args, **kw):
        pieces = [p if isinstance(p, tuple) else (p, None) for p in pieces]
        comm = merge_comms([comm_all_gather([shard_b[k]], rows, [landing[k]] if k in landing else None)
                            for k, rows in pieces])
        out, got = fn(*args, comm=comm, **kw)
        for (k, rows), (g,) in zip(pieces, split_results(comm, got)):
            landing[k] = g
            rows_done[k] = rows_done.get(k, 0) + (rows[1] if rows is not None else shard_b[k].shape[0])
            if rows_done[k] == shard_b[k].shape[0]:
                full[k] = g if k in _COL_SHARDED else g.reshape(-1, g.shape[2])
        return out

    n1 = gathering(("ffn1_w_in",), rms_fwd, xs, g1, "f_n1")
    a1, hsw1 = gathering(("ffn1_w_out",), mm_swiglu_g, n1, full["ffn1_w_in"], "f_a1")
    h1 = gathering(("w_mix_in",), mm_nn, hsw1, full["ffn1_w_out"], F32, "f_h1", scale=0.5, res=xs)
    n2 = gathering(("w_cq",), rms_fwd, h1, gmix, "f_n2")
    z = gathering(("w_mix_out", "w_co"), mm_nn_g, n2, full["w_mix_in"], F32, "f_z")
    ya = gathering((("w_ckv", (0, 512)),), sgu_fwd, z, ln_g, ln_b, w_s, b_t, "f_sgu")
    qkv = split_heads(z, "f_qkv")
    yb = gathering((("w_ckv", (512, 1536)), ("ffn2_w_in", (0, 512))), sb_fwd, qkv, "f_sb")
    ycat = gathering((("ffn2_w_in", (512, 128)),), rmscat_fwd, ya, yb, ga, gb, "f_ycat")
    h2 = gathering((("ffn2_w_in", (640, 256)),), mm_nn, ycat, full["w_mix_out"], F32, "f_h2", res=h1)
    n3 = gathering((("ffn2_w_in", (896, 128)),), rms_fwd, h2, gcross, "f_n3")
    memn = rms_fwd(mems, gmem, "f_memn")
    qc = gathering((("ffn2_w_in", (1024, 256)),), mm_nn, n3, full["w_cq"], BF16, "f_qc", scale=X_DH ** -0.5)
    kv = gathering((("ffn2_w_in", (1280, 128)),), mm_nn_g, memn, full["w_ckv"], BF16, "f_kv")
    o = gathering((("ffn2_w_in", (1408, 128)),), xattn_fwd, qc, kv, "f_xattn")
    h3 = gathering((("ffn2_w_in", (1536, 256)),), mm_nn, o, full["w_co"], F32, "f_h3", res=h2)
    n4 = gathering((("ffn2_w_in", (1792, 256)),), rms_fwd, h3, g2, "f_n4")
    a2, hsw2 = gathering(("ffn2_w_out",), mm_swiglu_g, n4, full["ffn2_w_in"], "f_a2")
    h4 = mm_nn(hsw2, full["ffn2_w_out"], F32, "f_h4", scale=0.5, res=h3)

    grads, parts, sums, recv = {}, {}, {}, {}
    core = lax.axis_index("c").astype(jnp.int32).reshape(1)

    def partial_of(k, g):
        grads[k] = g
        parts[k] = g if g.ndim == 3 else g.reshape(N_DEV, -1, g.shape[1])

    def reducing(pairs, chips, fn, *args, **kw):
        def piece(p):
            if isinstance(p, dict):
                return p
            k, rows = p if isinstance(p, tuple) else (p, None)
            return dict(sums=k, rows=rows, to=k)

        chips = [piece(p) for p in chips]
        comms = [comm_pairs([parts[k] for k in pairs])] if pairs else []
        comms += [comm_chips([sums[p["sums"]]], p["rows"], [recv[p["to"]]] if p["to"] in recv else None,
                             p.get("from_row"), p.get("out_rows")) for p in chips]
        comm = merge_comms(comms)
        out, got = fn(*args, comm=comm, **kw)
        got = split_results(comm, got)
        if pairs:
            for k, r in zip(pairs, got.pop(0)):
                sums[k] = pair_sum(parts[k], r, core, f"pair_sum_{k}")
        for p, (r,) in zip(chips, got):
            recv[p["to"]] = r
        return out

    loss_part, dh4, df2, grads["final_norm"] = loss_head(h4, tgt, gfin, "loss_head")
    partial_of("ffn2_w_out", mm_tn(hsw2, df2, "b_ffn2_dwout"))
    da2 = reducing(("ffn2_w_out",), (), mm_swiglu_bwd, df2, full["ffn2_w_out"], a2, "b_ffn2_da")
    partial_of("ffn2_w_in", reducing((), ("ffn2_w_out",), mm_tn_g, n4, da2, N_DEV, "b_ffn2_dwin"))
    dn4 = reducing(("ffn2_w_in",), (), mm_nt_g, da2, full["ffn2_w_in"], "b_ffn2_dn")
    dh3, dh3b, grads["ffn2_norm"] = rms_bwd(dn4, h3, g2, dh4, 1.0, "b_n4")

    partial_of("w_co", mm_tn(o, dh3b, "b_dwco"))
    do = reducing(("w_co",), (("ffn2_w_in", (0, 256)),), mm_nt, dh3b, full["w_co"], BF16, "b_do")
    dqp, dkv = reducing((), (("ffn2_w_in", (256, 256)),), xattn_bwd, qc, kv, do, "b_xattn")
    partial_of("w_cq", mm_tn(n3, dqp, "b_dwcq"))
    dn3 = reducing(("w_cq",), (("ffn2_w_in", (512, 256)),), mm_nt, dqp, full["w_cq"], F32, "b_dn3")
    partial_of("w_ckv", mm_tn_g(memn, dkv, N_DEV, "b_dwckv"))
    dmemn = reducing(("w_ckv",), (("ffn2_w_in", (768, 128)),), mm_nt_g, dkv, full["w_ckv"], "b_dmemn")
    _, _, grads["mem_norm"] = rms_bwd(dmemn, mems, gmem, None, 1.0, "b_memn")
    dh2, dh2b, grads["cross_norm"] = rms_bwd(dn3, h2, gcross, dh3, 1.0, "b_n3")

    partial_of("w_mix_out", mm_tn(ycat, dh2b, "b_dwmixout"))
    dycat = reducing(("w_mix_out",), (("ffn2_w_in", (896, 256)),), mm_nt, dh2b, full["w_mix_out"], F32, "b_dycat")
    dya, dyb, grads["gnorm_a"], grads["gnorm_b"] = rmscat_bwd(dycat, ya, yb, ga, gb, "b_ycat")
    dza, grads["ln_v_gain"], grads["ln_v_bias"], grads["spatial_w"], grads["spatial_b"] = reducing(
        (), (("ffn2_w_in", (1152, 384)),), sgu_bwd, z, dya, ln_g, ln_b, w_s, b_t, "b_sgu")
    dq, dk, dv = reducing((), (("ffn2_w_in", (1536, 512)), "w_co", "w_cq", ("w_ckv", (0, 1024))), sb_bwd, qkv, yb,
                          dyb, "b_sb")
    dz = join_dz(dza, dq, dk, dv, "b_dz")
    partial_of("w_mix_in", reducing((), (("w_ckv", (1024, 1024)),), mm_tn_g, n2, dz, N_DEV, "b_dwmixin"))
    dn2 = reducing(("w_mix_in",), ("w_mix_out",), mm_nt_g, dz, full["w_mix_in"], "b_dn2")
    dh1, dh1b, grads["mix_norm"] = reducing((), (("w_mix_in", (0, 512)),), rms_bwd, dn2, h1, gmix, dh2, 0.5, "b_n2")

    partial_of("ffn1_w_out", reducing((), (("w_mix_in", (512, 1024)),), mm_tn, hsw1, dh1b, "b_ffn1_dwout"))
    da1 = reducing(("ffn1_w_out",), (("w_mix_in", (1536, 512)),), mm_swiglu_bwd, dh1b, full["ffn1_w_out"], a1,
                   "b_ffn1_da")
    half = D // 2
    partial_of("ffn1_w_in_a", reducing((), (("ffn1_w_out", (0, 352)),), mm_tn_g, n1, da1, N_DEV, "b_ffn1_dwin_a",
                                       cols=(0, half)))
    partial_of("ffn1_w_in_b", reducing(("ffn1_w_in_a",), (("ffn1_w_out", (352, 352)),), mm_tn_g, n1, da1, N_DEV,
                                       "b_ffn1_dwin_b", cols=(half, half)))
    dn1 = reducing(("ffn1_w_in_b",), (dict(sums="ffn1_w_in_a", rows=(0, half), to="ffn1_w_in", out_rows=D),),
                   mm_nt_g, da1, full["ffn1_w_in"], "b_ffn1_dn")
    (dx, _, grads["ffn1_norm"]) = reducing((), (dict(sums="ffn1_w_in_b", rows=(half, half), from_row=0, to="ffn1_w_in"),),
                                           rms_bwd, dn1, xs, g1, dh1, 1.0, "b_n1")

    out_g, out_d, out_m, out_v = {}, {}, {}, {}
    for k in _BIG:
        res = adamw(recv[k], shard2d[k], mom[k].reshape(shard2d[k].shape), var[k].reshape(shard2d[k].shape), f"adamw_{k}")
        out_g[k], out_d[k], out_m[k], out_v[k] = (t.reshape(wts[k].shape) for t in res)

    pack = lambda d: jnp.concatenate([_rows128(d[k]) for k in _SMALL] + [jnp.zeros((_SMALL_PAD, 128), F32)], axis=0)
    small_part = pack(grads)
    (small_sibling,) = run_comm(comm_pairs([small_part]), "comm_pairs_small")
    small_pair = add2(small_part, small_sibling, "pair_sum_small")
    (small_all,) = run_comm(comm_chips([small_pair]), "comm_chips_small")

    res = adamw(small_all, pack(wts), pack(mom), pack(var), "adamw_small")
    row = 0
    for k in _SMALL:
        nrow = wts[k].size // 128
        for dst, t in zip((out_g, out_d, out_m, out_v), res):
            dst[k] = t[row:row + nrow].reshape(wts[k].shape)
        row += nrow

    loss = lax.psum(loss_part[0, 0], ("x", "y", "c"))
    grad_x = dx.reshape(x.shape)
    return (loss, grad_x, *[out_g[k] for k in _ORDER], *[out_d[k] for k in _ORDER],
            *[out_m[k] for k in _ORDER], *[out_v[k] for k in _ORDER])
```

```python
import functools
import math

import jax
import jax.numpy as jnp
from jax import lax
from jax.experimental import pallas as pl
from jax.experimental.pallas import tpu as pltpu

F32 = jnp.float32
BF16 = jnp.bfloat16

N_DEV = 8
D_MODEL = 2048
D_FF = 5632
W_A = 1024
G_A = 8
GA_DIM = 128
SGU_BLOCK = 128
CHUNK = 64
H_B = 8
DH_B = 128
Q_BLOCK = 128
X_HEADS = 4
X_DH = 512
N_MEM = 256
EPS = 1e-6

ADAM_LR = 0.001
ADAM_B1 = 0.9
ADAM_B2 = 0.999
ADAM_EPS = 1e-08
ADAM_WD = 0.01
ADAM_STEP = 10

VMEM_LIMIT = 56 * 2**20
ROW_TILE = 256

MESH = pl.DeviceIdType.MESH
ANY = pl.BlockSpec(memory_space=pl.ANY)

_NT = (((1,), (1,)), ((), ()))
_TN = (((0,), (0,)), ((), ()))


def _params(*sem):
    return pltpu.CompilerParams(dimension_semantics=sem, vmem_limit_bytes=VMEM_LIMIT)


def _zeros(ref):
    return jnp.zeros(ref.shape, ref.dtype)


def _pcall(comm, body, *, name, grid, in_specs, out_specs, out_shape, compiler_params, scratch_shapes=()):
    if comm is None:
        return pl.pallas_call(body, name=name, grid=grid, in_specs=in_specs, out_specs=out_specs, out_shape=out_shape,
                              scratch_shapes=list(scratch_shapes), compiler_params=compiler_params)
    multi = isinstance(out_shape, (list, tuple))
    out_shapes = list(out_shape) if multi else [out_shape]
    out_specs_l = list(out_specs) if multi else [out_specs]
    n_in, n_out, n_scr = len(in_specs), len(out_shapes), len(scratch_shapes)
    n_cin, n_cout = len(comm.arrays), len(comm.out_shapes)

    def with_comm(*refs):
        ins, refs = refs[:n_in], refs[n_in:]
        cins, refs = refs[:n_cin], refs[n_cin:]
        outs, refs = refs[:n_out], refs[n_out:]
        couts, refs = refs[:n_cout], refs[n_cout:]
        scr, sems = refs[:n_scr], refs[n_scr:]
        first = functools.reduce(jnp.logical_and, [pl.program_id(a) == 0 for a in range(len(grid))])
        last = functools.reduce(jnp.logical_and, [pl.program_id(a) == grid[a] - 1 for a in range(len(grid))])
        pl.when(first)(lambda: comm.start(cins, couts, sems))
        body(*ins, *outs, *scr)
        pl.when(last)(lambda: comm.finish(cins, couts, sems))

    call = pl.pallas_call(
        with_comm, name=name, grid=grid, in_specs=list(in_specs) + [ANY] * n_cin,
        out_specs=out_specs_l + [ANY] * n_cout, out_shape=out_shapes + comm.out_shapes,
        scratch_shapes=list(scratch_shapes) + comm.sem_shapes(), compiler_params=_params(*(("arbitrary",) * len(grid))),
        input_output_aliases={n_in + i: n_out + j for i, j in comm.aliases.items()})

    def run(*args):
        res = call(*args, *comm.arrays)
        main = res[:n_out]
        return (list(main) if multi else main[0]), list(res[n_out:])

    return run


def _dot(a, b):
    return jnp.dot(a, b, preferred_element_type=F32)


def _dot_nt(a, b):
    return lax.dot_general(a, b, _NT, preferred_element_type=F32)


def _dot_tn(a, b):
    return lax.dot_general(a, b, _TN, preferred_element_type=F32)


def mm_nn_g(a, bg, out_dtype, name, tm=512, comm=None):
    M, K = a.shape
    G, _, n = bg.shape
    tm = min(tm, M)

    def body(a_ref, b_ref, o_ref):
        o_ref[...] = _dot(a_ref[...], b_ref[...]).astype(o_ref.dtype)

    return _pcall(
        comm, body, name=name, grid=(G, M // tm),
        in_specs=[pl.BlockSpec((tm, K), lambda g, m: (m, 0)),
                  pl.BlockSpec((None, K, n), lambda g, m: (g, 0, 0))],
        out_specs=pl.BlockSpec((tm, n), lambda g, m: (m, g)),
        out_shape=jax.ShapeDtypeStruct((M, G * n), out_dtype),
        compiler_params=_params("parallel", "parallel"),
    )(a, bg)


def mm_swiglu_g(a, bg, name, tm=512, comm=None):
    M, K = a.shape
    G, _, n = bg.shape
    half = G // 2
    tm = min(tm, M)

    def body(a_ref, bgate_ref, bup_ref, gu_ref, h_ref):
        av = a_ref[...]
        gate = _dot(av, bgate_ref[...])
        up = _dot(av, bup_ref[...])
        gu_ref[0] = gate.astype(BF16)
        gu_ref[1] = up.astype(BF16)
        h_ref[...] = (gate * _sigmoid(gate) * up).astype(BF16)

    return _pcall(
        comm, body, name=name, grid=(half, M // tm),
        in_specs=[pl.BlockSpec((tm, K), lambda p, m: (m, 0)),
                  pl.BlockSpec((None, K, n), lambda p, m: (p, 0, 0)),
                  pl.BlockSpec((None, K, n), lambda p, m: (p + half, 0, 0))],
        out_specs=[pl.BlockSpec((2, tm, n), lambda p, m: (0, m, p)), pl.BlockSpec((tm, n), lambda p, m: (m, p))],
        out_shape=[jax.ShapeDtypeStruct((2, M, half * n), BF16), jax.ShapeDtypeStruct((M, half * n), BF16)],
        compiler_params=_params("parallel", "parallel"),
    )(a, bg, bg)


def mm_swiglu_bwd(dy, w_out, gate_up, name, tk=512, comm=None):
    M, N = dy.shape
    F = w_out.shape[0]

    def body(dy_ref, w_ref, gu_ref, o_ref):
        dh = _dot_nt(dy_ref[...], w_ref[...])
        gt = gu_ref[0].astype(F32)
        up = gu_ref[1].astype(F32)
        sg = _sigmoid(gt)
        o_ref[0] = (dh * up * (sg * (1.0 + gt * (1.0 - sg)))).astype(BF16)
        o_ref[1] = (dh * (gt * sg)).astype(BF16)

    planes = pl.BlockSpec((2, M, tk), lambda k: (0, 0, k))
    return _pcall(
        comm, body, name=name, grid=(F // tk,),
        in_specs=[pl.BlockSpec((M, N), lambda k: (0, 0)), pl.BlockSpec((tk, N), lambda k: (k, 0)), planes],
        out_specs=planes,
        out_shape=jax.ShapeDtypeStruct((2, M, F), BF16),
        compiler_params=_params("parallel"),
    )(dy, w_out, gate_up)


def _shard_cols_spec(dy, G, rows, index):
    if dy.ndim == 2:
        n = dy.shape[1] // G
        return pl.BlockSpec((rows, n), lambda *ids: index(*ids)), n
    half = G // 2
    n = dy.shape[2] // half

    def planes(*ids):
        r, g = index(*ids)
        return g // half, r, g % half

    return pl.BlockSpec((None, rows, n), planes), n


def mm_nn(a, b, out_dtype, name, tm=512, tn=1024, scale=1.0, res=None, comm=None):
    M, K = a.shape
    _, N = b.shape
    tm, tn = min(tm, M), min(tn, N)

    def body(*refs):
        if res is None:
            a_ref, b_ref, o_ref = refs
            acc = _dot(a_ref[...], b_ref[...])
            o_ref[...] = (acc * scale if scale != 1.0 else acc).astype(o_ref.dtype)
        else:
            a_ref, b_ref, r_ref, o_ref = refs
            o_ref[...] = (r_ref[...] + scale * _dot(a_ref[...], b_ref[...])).astype(o_ref.dtype)

    in_specs = [pl.BlockSpec((tm, K), lambda n, m: (m, 0)),
                pl.BlockSpec((K, tn), lambda n, m: (0, n))]
    args = [a, b]
    if res is not None:
        in_specs.append(pl.BlockSpec((tm, tn), lambda n, m: (m, n)))
        args.append(res)
    return _pcall(
        comm, body, name=name, grid=(N // tn, M // tm),
        in_specs=in_specs,
        out_specs=pl.BlockSpec((tm, tn), lambda n, m: (m, n)),
        out_shape=jax.ShapeDtypeStruct((M, N), out_dtype),
        compiler_params=_params("parallel", "parallel"),
    )(*args)


def mm_nt_g(dy, bg, name, tm=512, comm=None):
    M = dy.shape[-2]
    G, K, n = bg.shape
    tm = min(tm, M)
    dy_spec, _ = _shard_cols_spec(dy, G, tm, lambda m, g: (m, g))

    def body(dy_ref, b_ref, o_ref):
        part = _dot_nt(dy_ref[...], b_ref[...])

        @pl.when(pl.program_id(1) == 0)
        def _():
            o_ref[...] = part

        @pl.when(pl.program_id(1) > 0)
        def _():
            o_ref[...] += part

    return _pcall(
        comm, body, name=name, grid=(M // tm, G),
        in_specs=[dy_spec, pl.BlockSpec((None, K, n), lambda m, g: (g, 0, 0))],
        out_specs=pl.BlockSpec((tm, K), lambda m, g: (m, 0)),
        out_shape=jax.ShapeDtypeStruct((M, K), F32),
        compiler_params=_params("parallel", "arbitrary"),
    )(dy, bg)


def mm_nt(dy, b, out_dtype, name, tk=512, comm=None):
    M, N = dy.shape
    K, _ = b.shape

    def body(dy_ref, b_ref, o_ref):
        o_ref[...] = _dot_nt(dy_ref[...], b_ref[...]).astype(o_ref.dtype)

    return _pcall(
        comm, body, name=name, grid=(K // tk,),
        in_specs=[pl.BlockSpec((M, N), lambda k: (0, 0)),
                  pl.BlockSpec((tk, N), lambda k: (k, 0))],
        out_specs=pl.BlockSpec((M, tk), lambda k: (0, k)),
        out_shape=jax.ShapeDtypeStruct((M, K), out_dtype),
        compiler_params=_params("parallel"),
    )(dy, b)


def mm_tn_g(x, dy, G, name, tk=512, cols=None, comm=None):
    M, K = x.shape
    k0, K = cols if cols is not None else (0, K)
    dy_spec, n = _shard_cols_spec(dy, G, M, lambda g, k: (0, g))

    def body(x_ref, dy_ref, o_ref):
        o_ref[...] = _dot_tn(x_ref[...], dy_ref[...]).astype(o_ref.dtype)

    return _pcall(
        comm, body, name=name, grid=(G, K // tk),
        in_specs=[pl.BlockSpec((M, tk), lambda g, k: (0, k + k0 // tk)), dy_spec],
        out_specs=pl.BlockSpec((None, tk, n), lambda g, k: (g, k, 0)),
        out_shape=jax.ShapeDtypeStruct((G, K, n), BF16),
        compiler_params=_params("parallel", "parallel"),
    )(x, dy)


def mm_tn(x, dy, name, tk=512, comm=None):
    M, K = x.shape
    _, N = dy.shape

    def body(x_ref, dy_ref, o_ref):
        o_ref[...] = _dot_tn(x_ref[...], dy_ref[...]).astype(o_ref.dtype)

    return _pcall(
        comm, body, name=name, grid=(K // tk,),
        in_specs=[pl.BlockSpec((M, tk), lambda k: (0, k)),
                  pl.BlockSpec((M, N), lambda k: (0, 0))],
        out_specs=pl.BlockSpec((tk, N), lambda k: (k, 0)),
        out_shape=jax.ShapeDtypeStruct((K, N), BF16),
        compiler_params=_params("parallel"),
    )(x, dy)


def _rstd(x):
    return lax.rsqrt(jnp.mean(x * x, axis=-1, keepdims=True) + EPS)


def _rms_bwd(dn, xhat, r, g):
    dxhat = dn * g
    return r * (dxhat - xhat * jnp.mean(dxhat * xhat, axis=-1, keepdims=True))


def _row_spec(tr, width, col=0):
    return pl.BlockSpec((tr, width), lambda i: (i, col))


def _vec_spec(width):
    return pl.BlockSpec((1, width), lambda i: (0, 0))


def _heads_spec(tr):
    return pl.BlockSpec((H_B, tr, DH_B), lambda i: (0, i, 0))


def _heads_to_cols(ref):
    return jnp.concatenate([ref[h] for h in range(H_B)], axis=1)


def split_heads(z, name, comm=None):
    S = z.shape[0]
    tr = min(ROW_TILE, S)
    width = H_B * DH_B
    first = 2 * W_A // width

    def body(q_ref, k_ref, v_ref, o_ref):
        for p, (ref, scale) in enumerate(((q_ref, DH_B ** -0.5), (k_ref, 1.0), (v_ref, 1.0))):
            for h in range(H_B):
                cols = ref[:, h * DH_B:(h + 1) * DH_B]
                o_ref[p * H_B + h] = (cols * scale if scale != 1.0 else cols).astype(BF16)

    return _pcall(
        comm, body, name=name, grid=(S // tr,),
        in_specs=[_row_spec(tr, width, first), _row_spec(tr, width, first + 1), _row_spec(tr, width, first + 2)],
        out_specs=pl.BlockSpec((3 * H_B, tr, DH_B), lambda i: (0, i, 0)),
        out_shape=jax.ShapeDtypeStruct((3 * H_B, S, DH_B), BF16),
        compiler_params=_params("parallel"),
    )(z, z, z)


def join_dz(dza, dq, dk, dv, name, comm=None):
    S, wa = dza.shape
    tr = min(ROW_TILE, S)
    width = H_B * DH_B

    def body(dza_ref, dq_ref, dk_ref, dv_ref, o_ref):
        o_ref[:, :wa] = dza_ref[...]
        for p, ref in enumerate((dq_ref, dk_ref, dv_ref)):
            for h in range(H_B):
                lo = wa + p * width + h * DH_B
                o_ref[:, lo:lo + DH_B] = ref[h]

    return _pcall(
        comm, body, name=name, grid=(S // tr,),
        in_specs=[_row_spec(tr, wa), _heads_spec(tr), _heads_spec(tr), _heads_spec(tr)],
        out_specs=_row_spec(tr, wa + 3 * width),
        out_shape=jax.ShapeDtypeStruct((S, wa + 3 * width), BF16),
        compiler_params=_params("parallel"),
    )(dza, dq, dk, dv)


def rms_fwd(x, g, name, comm=None):
    M, D = x.shape
    tr = min(ROW_TILE, M)

    def body(x_ref, g_ref, o_ref):
        xv = x_ref[...]
        o_ref[...] = (xv * _rstd(xv) * g_ref[...]).astype(o_ref.dtype)

    return _pcall(
        comm, body, name=name, grid=(M // tr,),
        in_specs=[_row_spec(tr, D), _vec_spec(D)],
        out_specs=_row_spec(tr, D),
        out_shape=jax.ShapeDtypeStruct((M, D), BF16),
        compiler_params=_params("parallel"),
    )(x, g)


def rms_bwd(dn, h, g, dres, copy_scale, name, comm=None):
    M, D = h.shape
    tr = min(ROW_TILE, M)
    has_res = dres is not None

    def body(*refs):
        if has_res:
            dn_ref, h_ref, g_ref, dres_ref, dh_ref, dhb_ref, dg_ref = refs
        else:
            dn_ref, h_ref, g_ref, dh_ref, dhb_ref, dg_ref = refs
        hv = h_ref[...]
        r = _rstd(hv)
        xhat = hv * r
        dn = dn_ref[...]
        part = jnp.sum(dn * xhat, axis=0, keepdims=True)

        @pl.when(pl.program_id(0) == 0)
        def _():
            dg_ref[...] = part

        @pl.when(pl.program_id(0) > 0)
        def _():
            dg_ref[...] += part

        dh = _rms_bwd(dn, xhat, r, g_ref[...])
        if has_res:
            dh = dh + dres_ref[...]
        dh_ref[...] = dh
        dhb_ref[...] = (dh * copy_scale if copy_scale != 1.0 else dh).astype(BF16)

    in_specs = [_row_spec(tr, D), _row_spec(tr, D), _vec_spec(D)]
    args = [dn, h, g]
    if has_res:
        in_specs.append(_row_spec(tr, D))
        args.append(dres)
    return _pcall(
        comm, body, name=name, grid=(M // tr,),
        in_specs=in_specs,
        out_specs=[_row_spec(tr, D), _row_spec(tr, D), _vec_spec(D)],
        out_shape=[jax.ShapeDtypeStruct((M, D), F32), jax.ShapeDtypeStruct((M, D), BF16),
                   jax.ShapeDtypeStruct((1, D), F32)],
        compiler_params=_params("arbitrary"),
    )(*args)


def _sigmoid(x):
    return 1.0 / (1.0 + jnp.exp(-x))


def rmscat_fwd(ya, yb, ga, gb, name, comm=None):
    M, W = ya.shape
    tr = min(ROW_TILE, M)

    def body(ya_ref, yb_ref, ga_ref, gb_ref, o_ref):
        a = ya_ref[...]
        b = _heads_to_cols(yb_ref)
        o_ref[:, :W] = (a * _rstd(a) * ga_ref[...]).astype(BF16)
        o_ref[:, W:] = (b * _rstd(b) * gb_ref[...]).astype(BF16)

    return _pcall(
        comm, body, name=name, grid=(M // tr,),
        in_specs=[_row_spec(tr, W), _heads_spec(tr), _vec_spec(W), _vec_spec(W)],
        out_specs=_row_spec(tr, 2 * W),
        out_shape=jax.ShapeDtypeStruct((M, 2 * W), BF16),
        compiler_params=_params("parallel"),
    )(ya, yb, ga, gb)


def rmscat_bwd(dycat, ya, yb, ga, gb, name, comm=None):
    M, W = ya.shape
    tr = min(ROW_TILE, M)

    def body(dc_ref, ya_ref, yb_ref, ga_ref, gb_ref, dya_ref, dyb_ref, dga_ref, dgb_ref):
        first = pl.program_id(0) == 0
        for by_head, y_ref, g_ref, dy_ref, dg_ref, lo in ((False, ya_ref, ga_ref, dya_ref, dga_ref, 0),
                                                          (True, yb_ref, gb_ref, dyb_ref, dgb_ref, W)):
            yv = _heads_to_cols(y_ref) if by_head else y_ref[...]
            r = _rstd(yv)
            xhat = yv * r
            dn = dc_ref[:, lo:lo + W]
            part = jnp.sum(dn * xhat, axis=0, keepdims=True)

            @pl.when(first)
            def _():
                dg_ref[...] = part

            @pl.when(jnp.logical_not(first))
            def _():
                dg_ref[...] += part

            dy = _rms_bwd(dn, xhat, r, g_ref[...])
            if by_head:
                for h in range(H_B):
                    dy_ref[h] = dy[:, h * DH_B:(h + 1) * DH_B]
            else:
                dy_ref[...] = dy

    return _pcall(
        comm, body, name=name, grid=(M // tr,),
        in_specs=[_row_spec(tr, 2 * W), _row_spec(tr, W), _heads_spec(tr), _vec_spec(W), _vec_spec(W)],
        out_specs=[_row_spec(tr, W), _heads_spec(tr), _vec_spec(W), _vec_spec(W)],
        out_shape=[jax.ShapeDtypeStruct((M, W), F32), jax.ShapeDtypeStruct((H_B, M, DH_B), F32),
                   jax.ShapeDtypeStruct((1, W), F32), jax.ShapeDtypeStruct((1, W), F32)],
        compiler_params=_params("arbitrary"),
    )(dycat, ya, yb, ga, gb)


def loss_head(h, target, g, name, comm=None):
    M, D = h.shape
    tr = min(ROW_TILE, M)

    def body(h_ref, t_ref, g_ref, loss_ref, dh_ref, dhb_ref, dg_ref):
        hv = h_ref[...]
        gv = g_ref[...]
        r = _rstd(hv)
        xhat = hv * r
        err = xhat * gv - t_ref[...]
        lsum = jnp.sum(jnp.sum(err * err, axis=1, keepdims=True), axis=0, keepdims=True) * (0.5 / D)
        dy = err * (1.0 / D)
        part = jnp.sum(dy * xhat, axis=0, keepdims=True)

        @pl.when(pl.program_id(0) == 0)
        def _():
            dg_ref[...] = part
            loss_ref[...] = _zeros(loss_ref) + lsum

        @pl.when(pl.program_id(0) > 0)
        def _():
            dg_ref[...] += part
            loss_ref[...] += lsum

        dh = _rms_bwd(dy, xhat, r, gv)
        dh_ref[...] = dh
        dhb_ref[...] = (0.5 * dh).astype(BF16)

    return _pcall(
        comm, body, name=name, grid=(M // tr,),
        in_specs=[_row_spec(tr, D), _row_spec(tr, D), _vec_spec(D)],
        out_specs=[pl.BlockSpec((8, 128), lambda i: (0, 0)), _row_spec(tr, D), _row_spec(tr, D), _vec_spec(D)],
        out_shape=[jax.ShapeDtypeStruct((8, 128), F32), jax.ShapeDtypeStruct((M, D), F32),
                   jax.ShapeDtypeStruct((M, D), BF16), jax.ShapeDtypeStruct((1, D), F32)],
        compiler_params=_params("arbitrary"),
    )(h, target, g)


_GELU_C = math.sqrt(2.0 / math.pi)


def _gelu(x):
    return 0.5 * x * (1.0 + jnp.tanh(_GELU_C * (x + 0.044715 * (x * x * x))))


def _gelu_grad(x):
    t = jnp.tanh(_GELU_C * (x + 0.044715 * (x * x * x)))
    return 0.5 * (1.0 + t) + 0.5 * x * (1.0 - t * t) * (_GELU_C * (1.0 + 3.0 * 0.044715 * (x * x)))


def _sgu_mask():
    t = lax.broadcasted_iota(jnp.int32, (SGU_BLOCK, SGU_BLOCK), 0) // CHUNK
    s = lax.broadcasted_iota(jnp.int32, (SGU_BLOCK, SGU_BLOCK), 1) // CHUNK
    return s <= t


def _layernorm_stats(v):
    mu = jnp.mean(v, axis=-1, keepdims=True)
    cen = v - mu
    rstd = lax.rsqrt(jnp.mean(cen * cen, axis=-1, keepdims=True) + EPS)
    return cen * rstd, rstd


def sgu_fwd(z, ln_g, ln_b, w_s, b_t, name, comm=None):
    S = z.shape[0]

    def body(zu_ref, zv_ref, lg_ref, lb_ref, w_ref, bt_ref, o_ref):
        mask = _sgu_mask()
        for g in range(G_A):
            cols = slice(g * GA_DIM, (g + 1) * GA_DIM)
            u = _gelu(zu_ref[:, cols])
            vhat, _ = _layernorm_stats(_gelu(zv_ref[:, cols]))
            vln = vhat * lg_ref[:, cols] + lb_ref[:, cols]
            w = jnp.where(mask, w_ref[g], 0.0).astype(BF16)
            mixed = _dot(w, vln.astype(BF16)) + bt_ref[:, g:g + 1]
            o_ref[:, cols] = u * mixed

    return _pcall(
        comm, body, name=name, grid=(S // SGU_BLOCK,),
        in_specs=[_row_spec(SGU_BLOCK, W_A, 0), _row_spec(SGU_BLOCK, W_A, 1), _vec_spec(W_A), _vec_spec(W_A),
                  pl.BlockSpec((G_A, SGU_BLOCK, SGU_BLOCK), lambda i: (0, 0, 0)),
                  pl.BlockSpec((SGU_BLOCK, G_A), lambda i: (0, 0))],
        out_specs=_row_spec(SGU_BLOCK, W_A),
        out_shape=jax.ShapeDtypeStruct((S, W_A), F32),
        compiler_params=_params("parallel"),
    )(z, z, ln_g, ln_b, w_s, b_t)


def sgu_bwd(z, dya, ln_g, ln_b, w_s, b_t, name, comm=None):
    S = z.shape[0]
    nblk = S // SGU_BLOCK

    def body(zu_ref, zv_ref, dy_ref, lg_ref, lb_ref, w_ref, bt_ref,
             dz_ref, dlg_ref, dlb_ref, dw_ref, db_ref, dmix_acc):
        step = pl.program_id(0)
        mask = _sgu_mask()

        @pl.when(step == 0)
        def _():
            dlg_ref[...] = _zeros(dlg_ref)
            dlb_ref[...] = _zeros(dlb_ref)
            dw_ref[...] = _zeros(dw_ref)
            dmix_acc[...] = _zeros(dmix_acc)

        for g in range(G_A):
            cols = slice(g * GA_DIM, (g + 1) * GA_DIM)
            zu = zu_ref[:, cols]
            zv = zv_ref[:, cols]
            u = _gelu(zu)
            vhat, rstd = _layernorm_stats(_gelu(zv))
            lg = lg_ref[:, cols]
            vln = (vhat * lg + lb_ref[:, cols]).astype(BF16)
            w = jnp.where(mask, w_ref[g], 0.0)
            mixed = _dot(w.astype(BF16), vln) + bt_ref[:, g:g + 1]
            dy = dy_ref[:, cols]
            du = dy * mixed
            dmixed = dy * u
            dmixed_b = dmixed.astype(BF16)
            dmix_acc[g] += dmixed
            dw_ref[g] += jnp.where(mask, _dot_nt(dmixed_b, vln), 0.0)
            dvln = _dot(w.T.astype(BF16), dmixed_b)
            dlb_ref[:, cols] += jnp.sum(dvln, axis=0, keepdims=True)
            dlg_ref[:, cols] += jnp.sum(dvln * vhat, axis=0, keepdims=True)
            dvhat = dvln * lg
            dv = rstd * (dvhat - jnp.mean(dvhat, axis=-1, keepdims=True)
                         - vhat * jnp.mean(dvhat * vhat, axis=-1, keepdims=True))
            dz_ref[:, cols] = (du * _gelu_grad(zu)).astype(BF16)
            dz_ref[:, W_A + g * GA_DIM:W_A + (g + 1) * GA_DIM] = (dv * _gelu_grad(zv)).astype(BF16)

        @pl.when(step == nblk - 1)
        def _():
            for g in range(G_A):
                db_ref[g] = jnp.sum(dmix_acc[g], axis=1, keepdims=True)

    whole3 = lambda shape: pl.BlockSpec(shape, lambda i: (0, 0, 0))
    return _pcall(
        comm, body, name=name, grid=(nblk,),
        in_specs=[_row_spec(SGU_BLOCK, W_A, 0), _row_spec(SGU_BLOCK, W_A, 1), _row_spec(SGU_BLOCK, W_A),
                  _vec_spec(W_A), _vec_spec(W_A), whole3((G_A, SGU_BLOCK, SGU_BLOCK)),
                  pl.BlockSpec((SGU_BLOCK, G_A), lambda i: (0, 0))],
        out_specs=[_row_spec(SGU_BLOCK, 2 * W_A), _vec_spec(W_A), _vec_spec(W_A),
                   whole3((G_A, SGU_BLOCK, SGU_BLOCK)), whole3((G_A, SGU_BLOCK, 1))],
        out_shape=[jax.ShapeDtypeStruct((S, 2 * W_A), BF16), jax.ShapeDtypeStruct((1, W_A), F32),
                   jax.ShapeDtypeStruct((1, W_A), F32), jax.ShapeDtypeStruct((G_A, SGU_BLOCK, SGU_BLOCK), F32),
                   jax.ShapeDtypeStruct((G_A, SGU_BLOCK, 1), F32)],
        scratch_shapes=[pltpu.VMEM((G_A, SGU_BLOCK, SGU_BLOCK), F32)],
        compiler_params=_params("arbitrary"),
    )(z, z, dya, ln_g, ln_b, w_s, b_t)


def _log_sigmoid(z):
    return jnp.minimum(z, 0.0) - jnp.log(1.0 + jnp.exp(-jnp.abs(z)))


def _suffix_sum(x, upper):
    hi = x.astype(BF16)
    lo = (x - hi.astype(F32)).astype(BF16)
    return _dot(hi, upper) + _dot(lo, upper)


SB_ROWS = 1024
_SB_SUB = SB_ROWS // Q_BLOCK


def _sb_upper():
    row = lax.broadcasted_iota(jnp.int32, (Q_BLOCK, Q_BLOCK), 0)
    col = lax.broadcasted_iota(jnp.int32, (Q_BLOCK, Q_BLOCK), 1)
    return (row > col).astype(BF16)


def _sb_sweep(step, tile):
    for r in reversed(range(_SB_SUB)):
        tile(step * _SB_SUB + r, r * Q_BLOCK)

    def group(g, _):
        base = (step - 1 - g) * _SB_SUB
        for r in reversed(range(_SB_SUB)):
            tile(base + r, None)
        return 0

    lax.fori_loop(0, step, group, 0)


def _sb_causal(n):
    return lax.broadcasted_iota(jnp.int32, (n, Q_BLOCK), 1) < lax.broadcasted_iota(jnp.int32, (n, Q_BLOCK), 0)


def _sb_rows_spec():
    return pl.BlockSpec((None, SB_ROWS, DH_B), lambda h, i: (h, i, 0))


def _sb_head_spec(S, part=0):
    return pl.BlockSpec((None, S, DH_B), lambda h, i: (part * H_B + h, 0, 0))


def sb_fwd(qkv, name, comm=None):
    S = qkv.shape[1]

    def body(q_b, k_ref, v_ref, o_ref, c_l1m):
        step = pl.program_id(1)
        o_ref[...] = _zeros(o_ref)
        c_l1m[...] = _zeros(c_l1m)
        upper = _sb_upper()

        def tile(j, row0):
            rq = slice(row0 or 0, SB_ROWS)
            causal = None if row0 is None else _sb_causal(SB_ROWS - row0)
            rows = pl.ds(pl.multiple_of(j * Q_BLOCK, Q_BLOCK), Q_BLOCK)
            zz = _dot_nt(q_b[rq, :], k_ref[rows, :])
            lb = _log_sigmoid(zz)
            l1m = lb - zz
            if causal is not None:
                l1m = jnp.where(causal, l1m, 0.0)
            a = jnp.exp(lb + _suffix_sum(l1m, upper) + c_l1m[rq, :])
            if causal is not None:
                a = jnp.where(causal, a, 0.0)
            o_ref[rq, :] += _dot(a.astype(BF16), v_ref[rows, :])
            c_l1m[rq, :] += jnp.sum(l1m, axis=1, keepdims=True)

        _sb_sweep(step, tile)

    return _pcall(
        comm, body, name=name, grid=(H_B, S // SB_ROWS),
        in_specs=[_sb_rows_spec(), _sb_head_spec(S, 1), _sb_head_spec(S, 2)],
        out_specs=_sb_rows_spec(),
        out_shape=jax.ShapeDtypeStruct((H_B, S, DH_B), F32),
        scratch_shapes=[pltpu.VMEM((SB_ROWS, 1), F32)],
        compiler_params=_params("parallel", "parallel"),
    )(qkv, qkv, qkv)


def sb_bwd(qkv, out, dout, name, comm=None):
    S = qkv.shape[1]
    nstep = S // SB_ROWS
    scale = DH_B ** -0.5

    def body(q_b, k_ref, v_ref, o_ref, do_ref, dq_ref, dk_ref, dv_ref,
             dq_acc, dkt_acc, dvt_acc, do_b, qt_b, dot_b, g_left, c_l1m):
        step = pl.program_id(1)

        @pl.when(step == 0)
        def _():
            dkt_acc[...] = _zeros(dkt_acc)
            dvt_acc[...] = _zeros(dvt_acc)

        do_b[...] = do_ref[...].astype(BF16)
        qt_b[...] = q_b[...].astype(F32).T.astype(BF16)
        dot_b[...] = do_ref[...].T.astype(BF16)
        g_left[...] = jnp.sum(do_b[...].astype(F32) * o_ref[...], axis=1, keepdims=True)
        dq_acc[...] = _zeros(dq_acc)
        c_l1m[...] = _zeros(c_l1m)
        upper = _sb_upper()

        def tile(j, row0):
            rq = slice(row0 or 0, SB_ROWS)
            causal = None if row0 is None else _sb_causal(SB_ROWS - row0)
            rows = pl.ds(pl.multiple_of(j * Q_BLOCK, Q_BLOCK), Q_BLOCK)
            q, do_t = q_b[rq, :], do_b[rq, :]
            k_j = k_ref[rows, :]
            zz = _dot_nt(q, k_j)
            lb = _log_sigmoid(zz)
            l1m = lb - zz
            if causal is not None:
                l1m = jnp.where(causal, l1m, 0.0)
            a = jnp.exp(lb + _suffix_sum(l1m, upper) + c_l1m[rq, :])
            if causal is not None:
                a = jnp.where(causal, a, 0.0)
            a_b = a.astype(BF16)
            dvt_acc[:, rows] += _dot(dot_b[:, rq], a_b)
            gmat = a_b.astype(F32) * _dot_nt(do_t, v_ref[rows, :])
            before = g_left[rq, :] - gmat - _suffix_sum(gmat, upper)
            sig = jnp.exp(lb)
            dz = gmat * (1.0 - sig) - sig * before
            if causal is not None:
                dz = jnp.where(causal, dz, 0.0)
            dz_b = dz.astype(BF16)
            dkt_acc[:, rows] += _dot(qt_b[:, rq], dz_b)
            dq_acc[rq, :] += _dot(dz_b, k_j)
            c_l1m[rq, :] += jnp.sum(l1m, axis=1, keepdims=True)
            g_left[rq, :] -= jnp.sum(gmat, axis=1, keepdims=True)

        _sb_sweep(step, tile)
        dq_ref[...] = (dq_acc[...] * scale).astype(BF16)

        @pl.when(step == nstep - 1)
        def _():
            dk_ref[...] = dkt_acc[...].T.astype(BF16)
            dv_ref[...] = dvt_acc[...].T.astype(BF16)

    out_sds = jax.ShapeDtypeStruct((H_B, S, DH_B), BF16)
    return _pcall(
        comm, body, name=name, grid=(H_B, nstep),
        in_specs=[_sb_rows_spec(), _sb_head_spec(S, 1), _sb_head_spec(S, 2), _sb_rows_spec(), _sb_rows_spec()],
        out_specs=[_sb_rows_spec(), _sb_head_spec(S), _sb_head_spec(S)],
        out_shape=[out_sds, out_sds, out_sds],
        scratch_shapes=[pltpu.VMEM((SB_ROWS, DH_B), F32)] + [pltpu.VMEM((DH_B, S), F32)] * 2
        + [pltpu.VMEM((SB_ROWS, DH_B), BF16)] + [pltpu.VMEM((DH_B, SB_ROWS), BF16)] * 2
        + [pltpu.VMEM((SB_ROWS, 1), F32)] * 2,
        compiler_params=_params("parallel", "arbitrary"),
    )(qkv, qkv, qkv, out, dout)


def _softmax(s):
    e = jnp.exp(s - jnp.max(s, axis=-1, keepdims=True))
    return e / jnp.sum(e, axis=-1, keepdims=True)


def xattn_fwd(qc, kv, name, comm=None):
    S, D = qc.shape
    tr = min(ROW_TILE, S)

    def body(q_ref, kv_ref, o_ref):
        for h in range(X_HEADS):
            cols = slice(h * X_DH, (h + 1) * X_DH)
            p = _softmax(_dot_nt(q_ref[:, cols], kv_ref[:, cols]))
            o_ref[:, cols] = _dot(p.astype(BF16), kv_ref[:, D + h * X_DH:D + (h + 1) * X_DH]).astype(BF16)

    return _pcall(
        comm, body, name=name, grid=(S // tr,),
        in_specs=[_row_spec(tr, D), pl.BlockSpec((N_MEM, 2 * D), lambda i: (0, 0))],
        out_specs=_row_spec(tr, D),
        out_shape=jax.ShapeDtypeStruct((S, D), BF16),
        compiler_params=_params("parallel"),
    )(qc, kv)


def xattn_bwd(qc, kv, do, name, comm=None):
    S, D = qc.shape
    tr = min(ROW_TILE, S)
    nstep = S // tr
    scale = X_DH ** -0.5

    def body(q_ref, kv_ref, do_ref, dq_ref, dkv_ref, acc):
        step = pl.program_id(0)

        @pl.when(step == 0)
        def _():
            acc[...] = _zeros(acc)

        for h in range(X_HEADS):
            cols = slice(h * X_DH, (h + 1) * X_DH)
            vcols = slice(D + h * X_DH, D + (h + 1) * X_DH)
            q = q_ref[:, cols]
            k = kv_ref[:, cols]
            do_h = do_ref[:, cols]
            p = _softmax(_dot_nt(q, k))
            dp = _dot_nt(do_h, kv_ref[:, vcols])
            acc[:, vcols] += _dot_tn(p.astype(BF16), do_h)
            ds = (p * (dp - jnp.sum(p * dp, axis=-1, keepdims=True))).astype(BF16)
            dq_ref[:, cols] = (_dot(ds, k) * scale).astype(BF16)
            acc[:, cols] += _dot_tn(ds, q)

        @pl.when(step == nstep - 1)
        def _():
            dkv_ref[...] = acc[...].astype(BF16)

    whole = pl.BlockSpec((N_MEM, 2 * D), lambda i: (0, 0))
    return _pcall(
        comm, body, name=name, grid=(nstep,),
        in_specs=[_row_spec(tr, D), whole, _row_spec(tr, D)],
        out_specs=[_row_spec(tr, D), whole],
        out_shape=[jax.ShapeDtypeStruct((S, D), BF16), jax.ShapeDtypeStruct((N_MEM, 2 * D), BF16)],
        scratch_shapes=[pltpu.VMEM((N_MEM, 2 * D), F32)],
        compiler_params=_params("arbitrary"),
    )(qc, kv, do)


def _row_tile(rows, cap=128):
    return max(t for t in range(16, cap + 1, 16) if rows % t == 0)


def cast_bf16(w, name, comm=None):
    R, C = w.shape
    tr = _row_tile(R, 256)

    def body(w_ref, o_ref):
        o_ref[...] = w_ref[...].astype(BF16)

    return _pcall(
        comm, body, name=name, grid=(R // tr,),
        in_specs=[_row_spec(tr, C)], out_specs=_row_spec(tr, C),
        out_shape=jax.ShapeDtypeStruct((R, C), BF16),
        compiler_params=_params("parallel"),
    )(w)


def adamw(parts, w, m, v, name, comm=None):
    R, C = w.shape
    n_parts = parts.shape[0]
    tr = _row_tile(R, 256)
    c1 = 1.0 - ADAM_B1 ** ADAM_STEP
    c2 = 1.0 - ADAM_B2 ** ADAM_STEP

    def body(p_ref, w_ref, m_ref, v_ref, g_ref, d_ref, mo_ref, vo_ref):
        g = p_ref[0].astype(F32)
        for p in range(1, n_parts):
            g = g + p_ref[p].astype(F32)
        m_new = ADAM_B1 * m_ref[...] + (1.0 - ADAM_B1) * g
        v_new = ADAM_B2 * v_ref[...] + (1.0 - ADAM_B2) * (g * g)
        g_ref[...] = g
        mo_ref[...] = m_new
        vo_ref[...] = v_new
        d_ref[...] = -ADAM_LR * ((m_new / c1) / (jnp.sqrt(v_new / c2) + ADAM_EPS) + ADAM_WD * w_ref[...])

    spec = _row_spec(tr, C)
    sds = jax.ShapeDtypeStruct((R, C), F32)
    return _pcall(
        comm, body, name=name, grid=(R // tr,),
        in_specs=[pl.BlockSpec((n_parts, tr, C), lambda i: (0, i, 0)), spec, spec, spec],
        out_specs=[spec, spec, spec, spec],
        out_shape=[sds, sds, sds, sds],
        compiler_params=_params("parallel"),
    )(parts, w, m, v)


def pair_sum(parts, from_sibling, core, name):
    _, R, C = parts.shape
    tr = _row_tile(R, 1024)

    def body(core_ref, p_ref, s_ref, o_ref):
        o_ref[...] = (p_ref[...].astype(F32) + s_ref[...].astype(F32)).astype(o_ref.dtype)

    return pl.pallas_call(
        body, name=name,
        grid_spec=pltpu.PrefetchScalarGridSpec(
            num_scalar_prefetch=1, grid=(4, R // tr),
            in_specs=[pl.BlockSpec((None, tr, C), lambda q, i, core_ref: (2 * q + core_ref[0], i, 0)),
                      pl.BlockSpec((None, tr, C), lambda q, i, core_ref: (q, i, 0))],
            out_specs=pl.BlockSpec((None, tr, C), lambda q, i, core_ref: (q, i, 0))),
        out_shape=jax.ShapeDtypeStruct((4, R, C), BF16),
        compiler_params=_params("parallel", "parallel"),
    )(core, parts, from_sibling)


def add2(a, b, name, comm=None):
    R, C = a.shape
    tr = _row_tile(R, 256)

    def body(a_ref, b_ref, o_ref):
        o_ref[...] = a_ref[...] + b_ref[...]

    spec = _row_spec(tr, C)
    return _pcall(
        comm, body, name=name, grid=(R // tr,), in_specs=[spec, spec], out_specs=spec,
        out_shape=jax.ShapeDtypeStruct((R, C), F32), compiler_params=_params("parallel"),
    )(a, b)


def _place():
    return lax.axis_index("x"), lax.axis_index("y"), lax.axis_index("c")


class Comm:
    def __init__(self, arrays, out_shapes, n_remote, n_local, start, finish, aliases=None):
        self.arrays, self.out_shapes = list(arrays), list(out_shapes)
        self.n_remote, self.n_local = n_remote, max(n_local, 1)
        self.start, self.finish = start, finish
        self.aliases = dict(aliases or {})
        self.sizes = [len(self.out_shapes)]

    def sem_shapes(self):
        return [pltpu.SemaphoreType.DMA((self.n_remote,)), pltpu.SemaphoreType.DMA((self.n_remote,)),
                pltpu.SemaphoreType.DMA((self.n_local,))]


class _Shifted:
    def __init__(self, ref, offset):
        self.ref, self.offset = ref, offset

    @property
    def at(self):
        return self

    def __getitem__(self, k):
        return self.ref.at[self.offset + k]


def merge_comms(comms):
    comms = [c for c in comms if c is not None]
    if not comms:
        return None

    def each(method):
        def run(ins, outs, sems):
            i = o = r = l = 0
            for c in comms:
                sub = (_Shifted(sems[0], r), _Shifted(sems[1], r), _Shifted(sems[2], l))
                getattr(c, method)(ins[i:i + len(c.arrays)], outs[o:o + len(c.out_shapes)], sub)
                i, o, r, l = i + len(c.arrays), o + len(c.out_shapes), r + c.n_remote, l + c.n_local
        return run

    aliases, i, o = {}, 0, 0
    for c in comms:
        aliases.update({i + a: o + b for a, b in c.aliases.items()})
        i, o = i + len(c.arrays), o + len(c.out_shapes)
    merged = Comm([a for c in comms for a in c.arrays], [s for c in comms for s in c.out_shapes],
                  sum(c.n_remote for c in comms), sum(c.n_local for c in comms), each("start"), each("finish"), aliases)
    merged.sizes = [len(c.out_shapes) for c in comms]
    return merged


def split_results(comm, results):
    out, i = [], 0
    for n in comm.sizes:
        out.append(list(results[i:i + n]))
        i += n
    return out


def run_comm(comm, name):
    n_in, n_out = len(comm.arrays), len(comm.out_shapes)

    def body(*refs):
        ins, outs, sems = refs[:n_in], refs[n_in:n_in + n_out], refs[n_in + n_out:]
        comm.start(ins, outs, sems)
        comm.finish(ins, outs, sems)

    return pl.pallas_call(
        body, name=name, in_specs=[ANY] * n_in, out_specs=[ANY] * n_out, out_shape=comm.out_shapes,
        scratch_shapes=comm.sem_shapes(), input_output_aliases=comm.aliases,
    )(*comm.arrays)


def _remote(src, dst, sems, k, to):
    return pltpu.make_async_remote_copy(src_ref=src, dst_ref=dst, send_sem=sems[0].at[k], recv_sem=sems[1].at[k],
                                        device_id=to, device_id_type=MESH)


_AG_COPIES = 13


def comm_all_gather(shards, rows=None, into=None):
    n = len(shards)
    row0, nrows = rows if rows is not None else (0, None)

    def parties():
        x, y, c = _place()
        return (x, y, c), (x, y, 1 - c), [(1 - x, y), (x, 1 - y), (1 - x, 1 - y)]

    def span(w, half=None):
        count = nrows if nrows is not None else shards[w].shape[0]
        if half is None:
            return pl.ds(row0, count)
        return pl.ds(row0 + half * (count // 2), count // 2)

    def slab(outs, w, dev, half=None):
        return outs[w].at[4 * dev[0] + 2 * dev[1] + dev[2], span(w, half)]

    def own(ins, outs, sems):
        me, sibling, chips = parties()
        local = [pltpu.make_async_copy(ins[w].at[span(w)], slab(outs, w, me), sems[2].at[w]) for w in range(n)]
        first = []
        for w in range(n):
            k = _AG_COPIES * w
            first.append(_remote(ins[w].at[span(w)], slab(outs, w, me), sems, k, sibling))
            first += [_remote(ins[w].at[span(w, h)], slab(outs, w, me, h), sems, k + 1 + 2 * j + h, (*chip, me[2]))
                      for h in range(2) for j, chip in enumerate(chips)]
        return local, first

    def start(ins, outs, sems):
        local, first = own(ins, outs, sems)
        for cp in local + first:
            cp.start()

    def finish(ins, outs, sems):
        me, sibling, chips = parties()
        local, first = own(ins, outs, sems)
        passed = []
        for w in range(n):
            k = _AG_COPIES * w
            for h in range(2):
                for j, chip in enumerate(chips):
                    got = slab(outs, w, (*chip, me[2]), h)
                    _remote(got, got, sems, k + 1 + 2 * j + h, me).wait_recv()
                    cp = _remote(got, got, sems, k + 7 + 2 * j + h, sibling)
                    cp.start()
                    passed.append(cp)
        for w in range(n):
            k = _AG_COPIES * w
            got = slab(outs, w, sibling)
            _remote(got, got, sems, k, me).wait_recv()
            for h in range(2):
                for j, chip in enumerate(chips):
                    got = slab(outs, w, (*chip, sibling[2]), h)
                    _remote(got, got, sems, k + 7 + 2 * j + h, me).wait_recv()
        for cp in first + passed:
            cp.wait_send()
        for cp in local:
            cp.wait()

    out_shapes = [jax.ShapeDtypeStruct((N_DEV,) + s.shape, s.dtype) for s in shards]
    arrays = list(shards) + (list(into) if into is not None else [])
    aliases = {n + w: w for w in range(n)} if into is not None else None
    return Comm(arrays, out_shapes, _AG_COPIES * n, n, start, finish, aliases)


def comm_pairs(items):
    slabbed = [a.ndim == 3 for a in items]
    first = [sum(4 if s else 1 for s in slabbed[:w]) for w in range(len(items))]

    def copies(ins, outs, sems):
        x, y, c = _place()
        sibling = (x, y, 1 - c)
        cps = []
        for w, s in enumerate(slabbed):
            if s:
                cps += [_remote(ins[w].at[2 * q + (1 - c)], outs[w].at[q], sems, first[w] + q, sibling) for q in range(4)]
            else:
                cps.append(_remote(ins[w], outs[w], sems, first[w], sibling))
        return cps

    def start(ins, outs, sems):
        for cp in copies(ins, outs, sems):
            cp.start()

    def finish(ins, outs, sems):
        for cp in copies(ins, outs, sems):
            cp.wait()

    out_shapes = [jax.ShapeDtypeStruct(((4,) + a.shape[1:]) if s else a.shape, a.dtype) for a, s in zip(items, slabbed)]
    return Comm(items, out_shapes, sum(4 if s else 1 for s in slabbed), 0, start, finish)


def comm_chips(items, rows=None, into=None, from_row=None, out_rows=None):
    n = len(items)
    slabbed = [a.ndim == 3 for a in items]

    def span(w, source=False):
        if rows is None:
            return pl.ds(0, items[w].shape[-2])
        return pl.ds(from_row if source and from_row is not None else rows[0], rows[1])

    def copies(ins, outs, sems):
        x, y, c = _place()
        mine = 2 * x + y
        local = [pltpu.make_async_copy(ins[w].at[mine, span(w, True)] if slabbed[w] else ins[w].at[span(w, True)],
                                       outs[w].at[mine, span(w)], sems[2].at[w]) for w in range(n)]
        remote = []
        for w in range(n):
            for j, (px, py) in enumerate([(1 - x, y), (x, 1 - y), (1 - x, 1 - y)]):
                src = ins[w].at[2 * px + py, span(w, True)] if slabbed[w] else ins[w].at[span(w, True)]
                remote.append(_remote(src, outs[w].at[mine, span(w)], sems, 3 * w + j, (px, py, c)))
        return local, remote

    def start(ins, outs, sems):
        local, remote = copies(ins, outs, sems)
        for cp in local + remote:
            cp.start()

    def finish(ins, outs, sems):
        local, remote = copies(ins, outs, sems)
        for cp in remote + local:
            cp.wait()

    def result(a):
        tall = a.shape[:-2] + (out_rows if out_rows is not None else a.shape[-2], a.shape[-1])
        return jax.ShapeDtypeStruct(tall if a.ndim == 3 else (4,) + tall, a.dtype)

    if into is None:
        return Comm(items, [result(a) for a in items], 3 * n, n, start, finish)
    out_shapes = [jax.ShapeDtypeStruct(b.shape, b.dtype) for b in into]
    return Comm(list(items) + list(into), out_shapes, 3 * n, n, start, finish, {n + w: w for w in range(n)})


_SMALL = ("ffn1_norm", "mix_norm", "ln_v_gain", "ln_v_bias", "spatial_w", "spatial_b", "gnorm_a", "gnorm_b",
          "cross_norm", "mem_norm", "ffn2_norm", "final_norm")
_BIG = ("ffn1_w_in", "ffn1_w_out", "w_mix_in", "w_mix_out", "w_cq", "w_ckv", "w_co", "ffn2_w_in", "ffn2_w_out")
_COL_SHARDED = ("ffn1_w_in", "w_mix_in", "w_ckv", "ffn2_w_in")
_ORDER = ("ffn1_norm", "ffn1_w_in", "ffn1_w_out", "mix_norm", "w_mix_in", "ln_v_gain", "ln_v_bias", "spatial_w",
          "spatial_b", "gnorm_a", "gnorm_b", "w_mix_out", "cross_norm", "mem_norm", "w_cq", "w_ckv", "w_co",
          "ffn2_norm", "ffn2_w_in", "ffn2_w_out", "final_norm")


_SMALL_PAD = 136


def _rows128(a):
    return a.reshape(-1, 128)


def kernel(x, mem, ffn1_norm, ffn1_w_in, ffn1_w_out, mix_norm, w_mix_in, ln_v_gain, ln_v_bias, spatial_w, spatial_b, gnorm_a, gnorm_b, w_mix_out, cross_norm, mem_norm, w_cq, w_ckv, w_co, ffn2_norm, ffn2_w_in, ffn2_w_out, final_norm, loss_target, m_ffn1_norm, m_ffn1_w_in, m_ffn1_w_out, m_mix_norm, m_w_mix_in, m_ln_v_gain, m_ln_v_bias, m_spatial_w, m_spatial_b, m_gnorm_a, m_gnorm_b, m_w_mix_out, m_cross_norm, m_mem_norm, m_w_cq, m_w_ckv, m_w_co, m_ffn2_norm, m_ffn2_w_in, m_ffn2_w_out, m_final_norm, v_ffn1_norm, v_ffn1_w_in, v_ffn1_w_out, v_mix_norm, v_w_mix_in, v_ln_v_gain, v_ln_v_bias, v_spatial_w, v_spatial_b, v_gnorm_a, v_gnorm_b, v_w_mix_out, v_cross_norm, v_mem_norm, v_w_cq, v_w_ckv, v_w_co, v_ffn2_norm, v_ffn2_w_in, v_ffn2_w_out, v_final_norm):
    given = dict(locals())
    wts = {k: given[k] for k in _ORDER}
    mom = {k: given["m_" + k] for k in _ORDER}
    var = {k: given["v_" + k] for k in _ORDER}

    D = D_MODEL
    xs = x.reshape(-1, D)
    mems = mem.reshape(-1, D)
    tgt = loss_target.reshape(-1, D)
    vec = lambda a: a.reshape(1, -1)
    g1, gmix, gcross, gmem, g2, gfin = (vec(wts[k]) for k in
                                        ("ffn1_norm", "mix_norm", "cross_norm", "mem_norm", "ffn2_norm", "final_norm"))
    ln_g, ln_b, ga, gb = (vec(wts[k]) for k in ("ln_v_gain", "ln_v_bias", "gnorm_a", "gnorm_b"))
    w_s = spatial_w.reshape(G_A, SGU_BLOCK, SGU_BLOCK)
    b_t = spatial_b.reshape(G_A, SGU_BLOCK).T

    shard2d = {k: wts[k].reshape(wts[k].shape[1:]) for k in _BIG}
    shard_b = {k: cast_bf16(shard2d[k], f"cast_{k}") for k in _BIG}
    full = {}

    landing, rows_done = {}, {}

    def gathering(pieces, fn, *args, **kw):
        pieces = [p if isinstance(p, tuple) else (p, None) for p in pieces]
        comm = merge_comms([comm_all_gather([shard_b[k]], rows, [landing[k]] if k in landing else None)
                            for k, rows in pieces])
        out, got = fn(*args, comm=comm, **kw)
        for (k, rows), (g,) in zip(pieces, split_results(comm, got)):
            landing[k] = g
            rows_done[k] = rows_done.get(k, 0) + (rows[1] if rows is not None else shard_b[k].shape[0])
            if rows_done[k] == shard_b[k].shape[0]:
                full[k] = g if k in _COL_SHARDED else g.reshape(-1, g.shape[2])
        return out

    n1 = gathering(("ffn1_w_in",), rms_fwd, xs, g1, "f_n1")
    a1, hsw1 = gathering(("ffn1_w_out",), mm_swiglu_g, n1, full["ffn1_w_in"], "f_a1")
    h1 = gathering(("w_mix_in",), mm_nn, hsw1, full["ffn1_w_out"], F32, "f_h1", scale=0.5, res=xs)
    n2 = gathering(("w_cq",), rms_fwd, h1, gmix, "f_n2")
    z = gathering(("w_mix_out", "w_co"), mm_nn_g, n2, full["w_mix_in"], F32, "f_z")
    ya = gathering((("w_ckv", (0, 512)),), sgu_fwd, z, ln_g, ln_b, w_s, b_t, "f_sgu")
    qkv = split_heads(z, "f_qkv")
    yb = gathering((("w_ckv", (512, 1536)), ("ffn2_w_in", (0, 512))), sb_fwd, qkv, "f_sb")
    ycat = gathering((("ffn2_w_in", (512, 128)),), rmscat_fwd, ya, yb, ga, gb, "f_ycat")
    h2 = gathering((("ffn2_w_in", (640, 256)),), mm_nn, ycat, full["w_mix_out"], F32, "f_h2", res=h1)
    n3 = gathering((("ffn2_w_in", (896, 128)),), rms_fwd, h2, gcross, "f_n3")
    memn = rms_fwd(mems, gmem, "f_memn")
    qc = gathering((("ffn2_w_in", (1024, 256)),), mm_nn, n3, full["w_cq"], BF16, "f_qc", scale=X_DH ** -0.5)
    kv = gathering((("ffn2_w_in", (1280, 128)),), mm_nn_g, memn, full["w_ckv"], BF16, "f_kv")
    o = gathering((("ffn2_w_in", (1408, 128)),), xattn_fwd, qc, kv, "f_xattn")
    h3 = gathering((("ffn2_w_in", (1536, 256)),), mm_nn, o, full["w_co"], F32, "f_h3", res=h2)
    n4 = gathering((("ffn2_w_in", (1792, 256)),), rms_fwd, h3, g2, "f_n4")
    a2, hsw2 = gathering(("ffn2_w_out",), mm_swiglu_g, n4, full["ffn2_w_in"], "f_a2")
    h4 = mm_nn(hsw2, full["ffn2_w_out"], F32, "f_h4", scale=0.5, res=h3)

    grads, parts, sums, recv = {}, {}, {}, {}
    core = lax.axis_index("c").astype(jnp.int32).reshape(1)

    def partial_of(k, g):
        grads[k] = g
        parts[k] = g if g.ndim == 3 else g.reshape(N_DEV, -1, g.shape[1])

    def reducing(pairs, chips, fn, *args, also=None, **kw):
        def piece(p):
            if isinstance(p, dict):
                return p
            k, rows = p if isinstance(p, tuple) else (p, None)
            return dict(sums=k, rows=rows, to=k)

        chips = [piece(p) for p in chips]
        comms = [comm_pairs([parts[k] for k in pairs])] if pairs else []
        comms += [comm_chips([sums[p["sums"]]], p["rows"], [recv[p["to"]]] if p["to"] in recv else None,
                             p.get("from_row"), p.get("out_rows")) for p in chips]
        comm = merge_comms(comms + ([also] if also is not None else []))
        out, got = fn(*args, comm=comm, **kw)
        got = split_results(comm, got)
        if pairs:
            for k, r in zip(pairs, got.pop(0)):
                sums[k] = pair_sum(parts[k], r, core, f"pair_sum_{k}")
        for p, (r,) in zip(chips, got):
            recv[p["to"]] = r
        return out if also is None else (out, got[-1])

    loss_part, dh4, df2, grads["final_norm"] = loss_head(h4, tgt, gfin, "loss_head")
    partial_of("ffn2_w_out", mm_tn(hsw2, df2, "b_ffn2_dwout"))
    da2 = reducing(("ffn2_w_out",), (), mm_swiglu_bwd, df2, full["ffn2_w_out"], a2, "b_ffn2_da")
    partial_of("ffn2_w_in", reducing((), ("ffn2_w_out",), mm_tn_g, n4, da2, N_DEV, "b_ffn2_dwin"))
    dn4 = reducing(("ffn2_w_in",), (), mm_nt_g, da2, full["ffn2_w_in"], "b_ffn2_dn")
    dh3, dh3b, grads["ffn2_norm"] = rms_bwd(dn4, h3, g2, dh4, 1.0, "b_n4")

    partial_of("w_co", mm_tn(o, dh3b, "b_dwco"))
    do = reducing(("w_co",), (("ffn2_w_in", (0, 256)),), mm_nt, dh3b, full["w_co"], BF16, "b_do")
    dqp, dkv = reducing((), (("ffn2_w_in", (256, 256)),), xattn_bwd, qc, kv, do, "b_xattn")
    partial_of("w_cq", mm_tn(n3, dqp, "b_dwcq"))
    dn3 = reducing(("w_cq",), (("ffn2_w_in", (512, 256)),), mm_nt, dqp, full["w_cq"], F32, "b_dn3")
    partial_of("w_ckv", mm_tn_g(memn, dkv, N_DEV, "b_dwckv"))
    dmemn = reducing(("w_ckv",), (("ffn2_w_in", (768, 128)),), mm_nt_g, dkv, full["w_ckv"], "b_dmemn")
    _, _, grads["mem_norm"] = rms_bwd(dmemn, mems, gmem, None, 1.0, "b_memn")
    dh2, dh2b, grads["cross_norm"] = rms_bwd(dn3, h2, gcross, dh3, 1.0, "b_n3")

    partial_of("w_mix_out", mm_tn(ycat, dh2b, "b_dwmixout"))
    dycat = reducing(("w_mix_out",), (("ffn2_w_in", (896, 256)),), mm_nt, dh2b, full["w_mix_out"], F32, "b_dycat")
    dya, dyb, grads["gnorm_a"], grads["gnorm_b"] = rmscat_bwd(dycat, ya, yb, ga, gb, "b_ycat")
    dza, grads["ln_v_gain"], grads["ln_v_bias"], grads["spatial_w"], grads["spatial_b"] = reducing(
        (), (("ffn2_w_in", (1152, 384)),), sgu_bwd, z, dya, ln_g, ln_b, w_s, b_t, "b_sgu")
    dq, dk, dv = reducing((), (("ffn2_w_in", (1536, 512)), "w_co", "w_cq", ("w_ckv", (0, 1024))), sb_bwd, qkv, yb,
                          dyb, "b_sb")
    dz = join_dz(dza, dq, dk, dv, "b_dz")
    partial_of("w_mix_in", reducing((), (("w_ckv", (1024, 1024)),), mm_tn_g, n2, dz, N_DEV, "b_dwmixin"))
    dn2 = reducing(("w_mix_in",), ("w_mix_out",), mm_nt_g, dz, full["w_mix_in"], "b_dn2")
    dh1, dh1b, grads["mix_norm"] = reducing((), (("w_mix_in", (0, 512)),), rms_bwd, dn2, h1, gmix, dh2, 0.5, "b_n2")

    pad = jnp.zeros((_SMALL_PAD, 128), F32)
    small_early = jnp.concatenate([_rows128(grads[k]) for k in _SMALL[1:]] + [pad], axis=0)
    g_w1out, (early_sibling,) = reducing((), (("w_mix_in", (512, 1024)),), mm_tn, hsw1, dh1b, "b_ffn1_dwout",
                                         also=comm_pairs([small_early]))
    partial_of("ffn1_w_out", g_w1out)
    early_pair = add2(small_early, early_sibling, "pair_sum_small_early")
    da1, (early_all,) = reducing(("ffn1_w_out",), (("w_mix_in", (1536, 512)),), mm_swiglu_bwd, dh1b,
                                 full["ffn1_w_out"], a1, "b_ffn1_da", also=comm_chips([early_pair]))
    half = D // 2
    partial_of("ffn1_w_in_a", reducing((), (("ffn1_w_out", (0, 352)),), mm_tn_g, n1, da1, N_DEV, "b_ffn1_dwin_a",
                                       cols=(0, half)))
    partial_of("ffn1_w_in_b", reducing(("ffn1_w_in_a",), (("ffn1_w_out", (352, 352)),), mm_tn_g, n1, da1, N_DEV,
                                       "b_ffn1_dwin_b", cols=(half, half)))
    dn1 = reducing(("ffn1_w_in_b",), (dict(sums="ffn1_w_in_a", rows=(0, half), to="ffn1_w_in", out_rows=D),),
                   mm_nt_g, da1, full["ffn1_w_in"], "b_ffn1_dn")
    (dx, _, grads["ffn1_norm"]) = reducing((), (dict(sums="ffn1_w_in_b", rows=(half, half), from_row=0, to="ffn1_w_in"),),
                                           rms_bwd, dn1, xs, g1, dh1, 1.0, "b_n1")

    out_g, out_d, out_m, out_v = {}, {}, {}, {}
    for k in _BIG:
        res = adamw(recv[k], shard2d[k], mom[k].reshape(shard2d[k].shape), var[k].reshape(shard2d[k].shape), f"adamw_{k}")
        out_g[k], out_d[k], out_m[k], out_v[k] = (t.reshape(wts[k].shape) for t in res)

    small_late = _rows128(grads[_SMALL[0]])
    (late_sibling,) = run_comm(comm_pairs([small_late]), "comm_pairs_small_late")
    late_pair = add2(small_late, late_sibling, "pair_sum_small_late")
    (late_all,) = run_comm(comm_chips([late_pair]), "comm_chips_small_late")
    small_all = jnp.concatenate([late_all, early_all], axis=1)
    pack = lambda d: jnp.concatenate([_rows128(d[k]) for k in _SMALL] + [pad], axis=0)
    res = adamw(small_all, pack(wts), pack(mom), pack(var), "adamw_small")
    row = 0
    for k in _SMALL:
        nrow = wts[k].size // 128
        for dst, t in zip((out_g, out_d, out_m, out_v), res):
            dst[k] = t[row:row + nrow].reshape(wts[k].shape)
        row += nrow

    loss = lax.psum(loss_part[0, 0], ("x", "y", "c"))
    grad_x = dx.reshape(x.shape)
    return (loss, grad_x, *[out_g[k] for k in _ORDER], *[out_d[k] for k in _ORDER],
            *[out_m[k] for k in _ORDER], *[out_v[k] for k in _ORDER])
```

```python
import functools
import math

import jax
import jax.numpy as jnp
from jax import lax
from jax.experimental import pallas as pl
from jax.experimental.pallas import tpu as pltpu

F32 = jnp.float32
BF16 = jnp.bfloat16

N_DEV = 8
D_MODEL = 2048
D_FF = 5632
W_A = 1024
G_A = 8
GA_DIM = 128
SGU_BLOCK = 128
CHUNK = 64
H_B = 8
DH_B = 128
Q_BLOCK = 128
X_HEADS = 4
X_DH = 512
N_MEM = 256
EPS = 1e-6

ADAM_LR = 0.001
ADAM_B1 = 0.9
ADAM_B2 = 0.999
ADAM_EPS = 1e-08
ADAM_WD = 0.01
ADAM_STEP = 10

VMEM_LIMIT = 56 * 2**20
ROW_TILE = 256

MESH = pl.DeviceIdType.MESH
ANY = pl.BlockSpec(memory_space=pl.ANY)

_NT = (((1,), (1,)), ((), ()))
_TN = (((0,), (0,)), ((), ()))


def _params(*sem):
    return pltpu.CompilerParams(dimension_semantics=sem, vmem_limit_bytes=VMEM_LIMIT)


def _zeros(ref):
    return jnp.zeros(ref.shape, ref.dtype)


def _pcall(comm, body, *, name, grid, in_specs, out_specs, out_shape, compiler_params, scratch_shapes=()):
    if comm is None:
        return pl.pallas_call(body, name=name, grid=grid, in_specs=in_specs, out_specs=out_specs, out_shape=out_shape,
                              scratch_shapes=list(scratch_shapes), compiler_params=compiler_params)
    multi = isinstance(out_shape, (list, tuple))
    out_shapes = list(out_shape) if multi else [out_shape]
    out_specs_l = list(out_specs) if multi else [out_specs]
    n_in, n_out, n_scr = len(in_specs), len(out_shapes), len(scratch_shapes)
    n_cin, n_cout = len(comm.arrays), len(comm.out_shapes)

    def with_comm(*refs):
        ins, refs = refs[:n_in], refs[n_in:]
        cins, refs = refs[:n_cin], refs[n_cin:]
        outs, refs = refs[:n_out], refs[n_out:]
        couts, refs = refs[:n_cout], refs[n_cout:]
        scr, sems = refs[:n_scr], refs[n_scr:]
        first = functools.reduce(jnp.logical_and, [pl.program_id(a) == 0 for a in range(len(grid))])
        last = functools.reduce(jnp.logical_and, [pl.program_id(a) == grid[a] - 1 for a in range(len(grid))])
        pl.when(first)(lambda: comm.start(cins, couts, sems))
        body(*ins, *outs, *scr)
        pl.when(last)(lambda: comm.finish(cins, couts, sems))

    call = pl.pallas_call(
        with_comm, name=name, grid=grid, in_specs=list(in_specs) + [ANY] * n_cin,
        out_specs=out_specs_l + [ANY] * n_cout, out_shape=out_shapes + comm.out_shapes,
        scratch_shapes=list(scratch_shapes) + comm.sem_shapes(), compiler_params=_params(*(("arbitrary",) * len(grid))),
        input_output_aliases={n_in + i: n_out + j for i, j in comm.aliases.items()})

    def run(*args):
        res = call(*args, *comm.arrays)
        main = res[:n_out]
        return (list(main) if multi else main[0]), list(res[n_out:])

    return run


def _dot(a, b):
    return jnp.dot(a, b, preferred_element_type=F32)


def _dot_nt(a, b):
    return lax.dot_general(a, b, _NT, preferred_element_type=F32)


def _dot_tn(a, b):
    return lax.dot_general(a, b, _TN, preferred_element_type=F32)


def mm_nn_g(a, bg, out_dtype, name, tm=512, comm=None):
    M, K = a.shape
    G, _, n = bg.shape
    tm = min(tm, M)

    def body(a_ref, b_ref, o_ref):
        o_ref[...] = _dot(a_ref[...], b_ref[...]).astype(o_ref.dtype)

    return _pcall(
        comm, body, name=name, grid=(G, M // tm),
        in_specs=[pl.BlockSpec((tm, K), lambda g, m: (m, 0)),
                  pl.BlockSpec((None, K, n), lambda g, m: (g, 0, 0))],
        out_specs=pl.BlockSpec((tm, n), lambda g, m: (m, g)),
        out_shape=jax.ShapeDtypeStruct((M, G * n), out_dtype),
        compiler_params=_params("parallel", "parallel"),
    )(a, bg)


def mm_swiglu_g(a, bg, name, tm=512, comm=None):
    M, K = a.shape
    G, _, n = bg.shape
    half = G // 2
    tm = min(tm, M)

    def body(a_ref, bgate_ref, bup_ref, gu_ref, h_ref):
        av = a_ref[...]
        gate = _dot(av, bgate_ref[...])
        up = _dot(av, bup_ref[...])
        gu_ref[0] = gate.astype(BF16)
        gu_ref[1] = up.astype(BF16)
        h_ref[...] = (gate * _sigmoid(gate) * up).astype(BF16)

    return _pcall(
        comm, body, name=name, grid=(half, M // tm),
        in_specs=[pl.BlockSpec((tm, K), lambda p, m: (m, 0)),
                  pl.BlockSpec((None, K, n), lambda p, m: (p, 0, 0)),
                  pl.BlockSpec((None, K, n), lambda p, m: (p + half, 0, 0))],
        out_specs=[pl.BlockSpec((2, tm, n), lambda p, m: (0, m, p)), pl.BlockSpec((tm, n), lambda p, m: (m, p))],
        out_shape=[jax.ShapeDtypeStruct((2, M, half * n), BF16), jax.ShapeDtypeStruct((M, half * n), BF16)],
        compiler_params=_params("parallel", "parallel"),
    )(a, bg, bg)


def mm_swiglu_bwd(dy, w_out, gate_up, name, tk=512, comm=None):
    M, N = dy.shape
    F = w_out.shape[0]

    def body(dy_ref, w_ref, gu_ref, o_ref):
        dh = _dot_nt(dy_ref[...], w_ref[...])
        gt = gu_ref[0].astype(F32)
        up = gu_ref[1].astype(F32)
        sg = _sigmoid(gt)
        o_ref[0] = (dh * up * (sg * (1.0 + gt * (1.0 - sg)))).astype(BF16)
        o_ref[1] = (dh * (gt * sg)).astype(BF16)

    planes = pl.BlockSpec((2, M, tk), lambda k: (0, 0, k))
    return _pcall(
        comm, body, name=name, grid=(F // tk,),
        in_specs=[pl.BlockSpec((M, N), lambda k: (0, 0)), pl.BlockSpec((tk, N), lambda k: (k, 0)), planes],
        out_specs=planes,
        out_shape=jax.ShapeDtypeStruct((2, M, F), BF16),
        compiler_params=_params("parallel"),
    )(dy, w_out, gate_up)


def _shard_cols_spec(dy, G, rows, index):
    if dy.ndim == 2:
        n = dy.shape[1] // G
        return pl.BlockSpec((rows, n), lambda *ids: index(*ids)), n
    half = G // 2
    n = dy.shape[2] // half

    def planes(*ids):
        r, g = index(*ids)
        return g // half, r, g % half

    return pl.BlockSpec((None, rows, n), planes), n


def mm_nn(a, b, out_dtype, name, tm=512, tn=1024, scale=1.0, res=None, comm=None):
    M, K = a.shape
    _, N = b.shape
    tm, tn = min(tm, M), min(tn, N)

    def body(*refs):
        if res is None:
            a_ref, b_ref, o_ref = refs
            acc = _dot(a_ref[...], b_ref[...])
            o_ref[...] = (acc * scale if scale != 1.0 else acc).astype(o_ref.dtype)
        else:
            a_ref, b_ref, r_ref, o_ref = refs
            o_ref[...] = (r_ref[...] + scale * _dot(a_ref[...], b_ref[...])).astype(o_ref.dtype)

    in_specs = [pl.BlockSpec((tm, K), lambda n, m: (m, 0)),
                pl.BlockSpec((K, tn), lambda n, m: (0, n))]
    args = [a, b]
    if res is not None:
        in_specs.append(pl.BlockSpec((tm, tn), lambda n, m: (m, n)))
        args.append(res)
    return _pcall(
        comm, body, name=name, grid=(N // tn, M // tm),
        in_specs=in_specs,
        out_specs=pl.BlockSpec((tm, tn), lambda n, m: (m, n)),
        out_shape=jax.ShapeDtypeStruct((M, N), out_dtype),
        compiler_params=_params("parallel", "parallel"),
    )(*args)


def mm_nt_g(dy, bg, name, tm=512, comm=None):
    M = dy.shape[-2]
    G, K, n = bg.shape
    tm = min(tm, M)
    dy_spec, _ = _shard_cols_spec(dy, G, tm, lambda m, g: (m, g))

    def body(dy_ref, b_ref, o_ref):
        part = _dot_nt(dy_ref[...], b_ref[...])

        @pl.when(pl.program_id(1) == 0)
        def _():
            o_ref[...] = part

        @pl.when(pl.program_id(1) > 0)
        def _():
            o_ref[...] += part

    return _pcall(
        comm, body, name=name, grid=(M // tm, G),
        in_specs=[dy_spec, pl.BlockSpec((None, K, n), lambda m, g: (g, 0, 0))],
        out_specs=pl.BlockSpec((tm, K), lambda m, g: (m, 0)),
        out_shape=jax.ShapeDtypeStruct((M, K), F32),
        compiler_params=_params("parallel", "arbitrary"),
    )(dy, bg)


def mm_nt(dy, b, out_dtype, name, tk=512, comm=None):
    M, N = dy.shape
    K, _ = b.shape

    def body(dy_ref, b_ref, o_ref):
        o_ref[...] = _dot_nt(dy_ref[...], b_ref[...]).astype(o_ref.dtype)

    return _pcall(
        comm, body, name=name, grid=(K // tk,),
        in_specs=[pl.BlockSpec((M, N), lambda k: (0, 0)),
                  pl.BlockSpec((tk, N), lambda k: (k, 0))],
        out_specs=pl.BlockSpec((M, tk), lambda k: (0, k)),
        out_shape=jax.ShapeDtypeStruct((M, K), out_dtype),
        compiler_params=_params("parallel"),
    )(dy, b)


def mm_tn_g(x, dy, G, name, tk=512, cols=None, comm=None):
    M, K = x.shape
    k0, K = cols if cols is not None else (0, K)
    dy_spec, n = _shard_cols_spec(dy, G, M, lambda g, k: (0, g))

    def body(x_ref, dy_ref, o_ref):
        o_ref[...] = _dot_tn(x_ref[...], dy_ref[...]).astype(o_ref.dtype)

    return _pcall(
        comm, body, name=name, grid=(G, K // tk),
        in_specs=[pl.BlockSpec((M, tk), lambda g, k: (0, k + k0 // tk)), dy_spec],
        out_specs=pl.BlockSpec((None, tk, n), lambda g, k: (g, k, 0)),
        out_shape=jax.ShapeDtypeStruct((G, K, n), BF16),
        compiler_params=_params("parallel", "parallel"),
    )(x, dy)


def mm_tn(x, dy, name, tk=512, comm=None):
    M, K = x.shape
    _, N = dy.shape

    def body(x_ref, dy_ref, o_ref):
        o_ref[...] = _dot_tn(x_ref[...], dy_ref[...]).astype(o_ref.dtype)

    return _pcall(
        comm, body, name=name, grid=(K // tk,),
        in_specs=[pl.BlockSpec((M, tk), lambda k: (0, k)),
                  pl.BlockSpec((M, N), lambda k: (0, 0))],
        out_specs=pl.BlockSpec((tk, N), lambda k: (k, 0)),
        out_shape=jax.ShapeDtypeStruct((K, N), BF16),
        compiler_params=_params("parallel"),
    )(x, dy)


def _rstd(x):
    return lax.rsqrt(jnp.mean(x * x, axis=-1, keepdims=True) + EPS)


def _rms_bwd(dn, xhat, r, g):
    dxhat = dn * g
    return r * (dxhat - xhat * jnp.mean(dxhat * xhat, axis=-1, keepdims=True))


def _row_spec(tr, width, col=0):
    return pl.BlockSpec((tr, width), lambda i: (i, col))


def _vec_spec(width):
    return pl.BlockSpec((1, width), lambda i: (0, 0))


def _heads_spec(tr):
    return pl.BlockSpec((H_B, tr, DH_B), lambda i: (0, i, 0))


def _heads_to_cols(ref):
    return jnp.concatenate([ref[h] for h in range(H_B)], axis=1)


def split_heads(z, name, comm=None):
    S = z.shape[0]
    tr = min(ROW_TILE, S)
    width = H_B * DH_B
    first = 2 * W_A // width

    def body(q_ref, k_ref, v_ref, o_ref):
        for p, (ref, scale) in enumerate(((q_ref, DH_B ** -0.5), (k_ref, 1.0), (v_ref, 1.0))):
            for h in range(H_B):
                cols = ref[:, h * DH_B:(h + 1) * DH_B]
                o_ref[p * H_B + h] = (cols * scale if scale != 1.0 else cols).astype(BF16)

    return _pcall(
        comm, body, name=name, grid=(S // tr,),
        in_specs=[_row_spec(tr, width, first), _row_spec(tr, width, first + 1), _row_spec(tr, width, first + 2)],
        out_specs=pl.BlockSpec((3 * H_B, tr, DH_B), lambda i: (0, i, 0)),
        out_shape=jax.ShapeDtypeStruct((3 * H_B, S, DH_B), BF16),
        compiler_params=_params("parallel"),
    )(z, z, z)


def join_dz(dza, dq, dk, dv, name, comm=None):
    S, wa = dza.shape
    tr = min(ROW_TILE, S)
    width = H_B * DH_B

    def body(dza_ref, dq_ref, dk_ref, dv_ref, o_ref):
        o_ref[:, :wa] = dza_ref[...]
        for p, ref in enumerate((dq_ref, dk_ref, dv_ref)):
            for h in range(H_B):
                lo = wa + p * width + h * DH_B
                o_ref[:, lo:lo + DH_B] = ref[h]

    return _pcall(
        comm, body, name=name, grid=(S // tr,),
        in_specs=[_row_spec(tr, wa), _heads_spec(tr), _heads_spec(tr), _heads_spec(tr)],
        out_specs=_row_spec(tr, wa + 3 * width),
        out_shape=jax.ShapeDtypeStruct((S, wa + 3 * width), BF16),
        compiler_params=_params("parallel"),
    )(dza, dq, dk, dv)


def rms_fwd(x, g, name, comm=None):
    M, D = x.shape
    tr = min(ROW_TILE, M)

    def body(x_ref, g_ref, o_ref):
        xv = x_ref[...]
        o_ref[...] = (xv * _rstd(xv) * g_ref[...]).astype(o_ref.dtype)

    return _pcall(
        comm, body, name=name, grid=(M // tr,),
        in_specs=[_row_spec(tr, D), _vec_spec(D)],
        out_specs=_row_spec(tr, D),
        out_shape=jax.ShapeDtypeStruct((M, D), BF16),
        compiler_params=_params("parallel"),
    )(x, g)


def rms_bwd(dn, h, g, dres, copy_scale, name, comm=None):
    M, D = h.shape
    tr = min(ROW_TILE, M)
    has_res = dres is not None

    def body(*refs):
        if has_res:
            dn_ref, h_ref, g_ref, dres_ref, dh_ref, dhb_ref, dg_ref = refs
        else:
            dn_ref, h_ref, g_ref, dh_ref, dhb_ref, dg_ref = refs
        hv = h_ref[...]
        r = _rstd(hv)
        xhat = hv * r
        dn = dn_ref[...]
        part = jnp.sum(dn * xhat, axis=0, keepdims=True)

        @pl.when(pl.program_id(0) == 0)
        def _():
            dg_ref[...] = part

        @pl.when(pl.program_id(0) > 0)
        def _():
            dg_ref[...] += part

        dh = _rms_bwd(dn, xhat, r, g_ref[...])
        if has_res:
            dh = dh + dres_ref[...]
        dh_ref[...] = dh
        dhb_ref[...] = (dh * copy_scale if copy_scale != 1.0 else dh).astype(BF16)

    in_specs = [_row_spec(tr, D), _row_spec(tr, D), _vec_spec(D)]
    args = [dn, h, g]
    if has_res:
        in_specs.append(_row_spec(tr, D))
        args.append(dres)
    return _pcall(
        comm, body, name=name, grid=(M // tr,),
        in_specs=in_specs,
        out_specs=[_row_spec(tr, D), _row_spec(tr, D), _vec_spec(D)],
        out_shape=[jax.ShapeDtypeStruct((M, D), F32), jax.ShapeDtypeStruct((M, D), BF16),
                   jax.ShapeDtypeStruct((1, D), F32)],
        compiler_params=_params("arbitrary"),
    )(*args)


def _sigmoid(x):
    return 1.0 / (1.0 + jnp.exp(-x))


def rmscat_fwd(ya, yb, ga, gb, name, comm=None):
    M, W = ya.shape
    tr = min(ROW_TILE, M)

    def body(ya_ref, yb_ref, ga_ref, gb_ref, o_ref):
        a = ya_ref[...]
        b = _heads_to_cols(yb_ref)
        o_ref[:, :W] = (a * _rstd(a) * ga_ref[...]).astype(BF16)
        o_ref[:, W:] = (b * _rstd(b) * gb_ref[...]).astype(BF16)

    return _pcall(
        comm, body, name=name, grid=(M // tr,),
        in_specs=[_row_spec(tr, W), _heads_spec(tr), _vec_spec(W), _vec_spec(W)],
        out_specs=_row_spec(tr, 2 * W),
        out_shape=jax.ShapeDtypeStruct((M, 2 * W), BF16),
        compiler_params=_params("parallel"),
    )(ya, yb, ga, gb)


def rmscat_bwd(dycat, ya, yb, ga, gb, name, comm=None):
    M, W = ya.shape
    tr = min(ROW_TILE, M)

    def body(dc_ref, ya_ref, yb_ref, ga_ref, gb_ref, dya_ref, dyb_ref, dga_ref, dgb_ref):
        first = pl.program_id(0) == 0
        for by_head, y_ref, g_ref, dy_ref, dg_ref, lo in ((False, ya_ref, ga_ref, dya_ref, dga_ref, 0),
                                                          (True, yb_ref, gb_ref, dyb_ref, dgb_ref, W)):
            yv = _heads_to_cols(y_ref) if by_head else y_ref[...]
            r = _rstd(yv)
            xhat = yv * r
            dn = dc_ref[:, lo:lo + W]
            part = jnp.sum(dn * xhat, axis=0, keepdims=True)

            @pl.when(first)
            def _():
                dg_ref[...] = part

            @pl.when(jnp.logical_not(first))
            def _():
                dg_ref[...] += part

            dy = _rms_bwd(dn, xhat, r, g_ref[...])
            if by_head:
                for h in range(H_B):
                    dy_ref[h] = dy[:, h * DH_B:(h + 1) * DH_B]
            else:
                dy_ref[...] = dy

    return _pcall(
        comm, body, name=name, grid=(M // tr,),
        in_specs=[_row_spec(tr, 2 * W), _row_spec(tr, W), _heads_spec(tr), _vec_spec(W), _vec_spec(W)],
        out_specs=[_row_spec(tr, W), _heads_spec(tr), _vec_spec(W), _vec_spec(W)],
        out_shape=[jax.ShapeDtypeStruct((M, W), F32), jax.ShapeDtypeStruct((H_B, M, DH_B), F32),
                   jax.ShapeDtypeStruct((1, W), F32), jax.ShapeDtypeStruct((1, W), F32)],
        compiler_params=_params("arbitrary"),
    )(dycat, ya, yb, ga, gb)


def loss_head(h, target, g, name, comm=None):
    M, D = h.shape
    tr = min(ROW_TILE, M)

    def body(h_ref, t_ref, g_ref, loss_ref, dh_ref, dhb_ref, dg_ref):
        hv = h_ref[...]
        gv = g_ref[...]
        r = _rstd(hv)
        xhat = hv * r
        err = xhat * gv - t_ref[...]
        lsum = jnp.sum(jnp.sum(err * err, axis=1, keepdims=True), axis=0, keepdims=True) * (0.5 / D)
        dy = err * (1.0 / D)
        part = jnp.sum(dy * xhat, axis=0, keepdims=True)

        @pl.when(pl.program_id(0) == 0)
        def _():
            dg_ref[...] = part
            loss_ref[...] = _zeros(loss_ref) + lsum

        @pl.when(pl.program_id(0) > 0)
        def _():
            dg_ref[...] += part
            loss_ref[...] += lsum

        dh = _rms_bwd(dy, xhat, r, gv)
        dh_ref[...] = dh
        dhb_ref[...] = (0.5 * dh).astype(BF16)

    return _pcall(
        comm, body, name=name, grid=(M // tr,),
        in_specs=[_row_spec(tr, D), _row_spec(tr, D), _vec_spec(D)],
        out_specs=[pl.BlockSpec((8, 128), lambda i: (0, 0)), _row_spec(tr, D), _row_spec(tr, D), _vec_spec(D)],
        out_shape=[jax.ShapeDtypeStruct((8, 128), F32), jax.ShapeDtypeStruct((M, D), F32),
                   jax.ShapeDtypeStruct((M, D), BF16), jax.ShapeDtypeStruct((1, D), F32)],
        compiler_params=_params("arbitrary"),
    )(h, target, g)


_GELU_C = math.sqrt(2.0 / math.pi)


def _gelu(x):
    return 0.5 * x * (1.0 + jnp.tanh(_GELU_C * (x + 0.044715 * (x * x * x))))


def _gelu_grad(x):
    t = jnp.tanh(_GELU_C * (x + 0.044715 * (x * x * x)))
    return 0.5 * (1.0 + t) + 0.5 * x * (1.0 - t * t) * (_GELU_C * (1.0 + 3.0 * 0.044715 * (x * x)))


def _sgu_mask():
    t = lax.broadcasted_iota(jnp.int32, (SGU_BLOCK, SGU_BLOCK), 0) // CHUNK
    s = lax.broadcasted_iota(jnp.int32, (SGU_BLOCK, SGU_BLOCK), 1) // CHUNK
    return s <= t


def _layernorm_stats(v):
    mu = jnp.mean(v, axis=-1, keepdims=True)
    cen = v - mu
    rstd = lax.rsqrt(jnp.mean(cen * cen, axis=-1, keepdims=True) + EPS)
    return cen * rstd, rstd


def sgu_fwd(z, ln_g, ln_b, w_s, b_t, name, comm=None):
    S = z.shape[0]

    def body(zu_ref, zv_ref, lg_ref, lb_ref, w_ref, bt_ref, o_ref):
        mask = _sgu_mask()
        for g in range(G_A):
            cols = slice(g * GA_DIM, (g + 1) * GA_DIM)
            u = _gelu(zu_ref[:, cols])
            vhat, _ = _layernorm_stats(_gelu(zv_ref[:, cols]))
            vln = vhat * lg_ref[:, cols] + lb_ref[:, cols]
            w = jnp.where(mask, w_ref[g], 0.0).astype(BF16)
            mixed = _dot(w, vln.astype(BF16)) + bt_ref[:, g:g + 1]
            o_ref[:, cols] = u * mixed

    return _pcall(
        comm, body, name=name, grid=(S // SGU_BLOCK,),
        in_specs=[_row_spec(SGU_BLOCK, W_A, 0), _row_spec(SGU_BLOCK, W_A, 1), _vec_spec(W_A), _vec_spec(W_A),
                  pl.BlockSpec((G_A, SGU_BLOCK, SGU_BLOCK), lambda i: (0, 0, 0)),
                  pl.BlockSpec((SGU_BLOCK, G_A), lambda i: (0, 0))],
        out_specs=_row_spec(SGU_BLOCK, W_A),
        out_shape=jax.ShapeDtypeStruct((S, W_A), F32),
        compiler_params=_params("parallel"),
    )(z, z, ln_g, ln_b, w_s, b_t)


def sgu_bwd(z, dya, ln_g, ln_b, w_s, b_t, name, comm=None):
    S = z.shape[0]
    nblk = S // SGU_BLOCK

    def body(zu_ref, zv_ref, dy_ref, lg_ref, lb_ref, w_ref, bt_ref,
             dz_ref, dlg_ref, dlb_ref, dw_ref, db_ref, dmix_acc):
        step = pl.program_id(0)
        mask = _sgu_mask()

        @pl.when(step == 0)
        def _():
            dlg_ref[...] = _zeros(dlg_ref)
            dlb_ref[...] = _zeros(dlb_ref)
            dw_ref[...] = _zeros(dw_ref)
            dmix_acc[...] = _zeros(dmix_acc)

        for g in range(G_A):
            cols = slice(g * GA_DIM, (g + 1) * GA_DIM)
            zu = zu_ref[:, cols]
            zv = zv_ref[:, cols]
            u = _gelu(zu)
            vhat, rstd = _layernorm_stats(_gelu(zv))
            lg = lg_ref[:, cols]
            vln = (vhat * lg + lb_ref[:, cols]).astype(BF16)
            w = jnp.where(mask, w_ref[g], 0.0)
            mixed = _dot(w.astype(BF16), vln) + bt_ref[:, g:g + 1]
            dy = dy_ref[:, cols]
            du = dy * mixed
            dmixed = dy * u
            dmixed_b = dmixed.astype(BF16)
            dmix_acc[g] += dmixed
            dw_ref[g] += jnp.where(mask, _dot_nt(dmixed_b, vln), 0.0)
            dvln = _dot(w.T.astype(BF16), dmixed_b)
            dlb_ref[:, cols] += jnp.sum(dvln, axis=0, keepdims=True)
            dlg_ref[:, cols] += jnp.sum(dvln * vhat, axis=0, keepdims=True)
            dvhat = dvln * lg
            dv = rstd * (dvhat - jnp.mean(dvhat, axis=-1, keepdims=True)
                         - vhat * jnp.mean(dvhat * vhat, axis=-1, keepdims=True))
            dz_ref[:, cols] = (du * _gelu_grad(zu)).astype(BF16)
            dz_ref[:, W_A + g * GA_DIM:W_A + (g + 1) * GA_DIM] = (dv * _gelu_grad(zv)).astype(BF16)

        @pl.when(step == nblk - 1)
        def _():
            for g in range(G_A):
                db_ref[g] = jnp.sum(dmix_acc[g], axis=1, keepdims=True)

    whole3 = lambda shape: pl.BlockSpec(shape, lambda i: (0, 0, 0))
    return _pcall(
        comm, body, name=name, grid=(nblk,),
        in_specs=[_row_spec(SGU_BLOCK, W_A, 0), _row_spec(SGU_BLOCK, W_A, 1), _row_spec(SGU_BLOCK, W_A),
                  _vec_spec(W_A), _vec_spec(W_A), whole3((G_A, SGU_BLOCK, SGU_BLOCK)),
                  pl.BlockSpec((SGU_BLOCK, G_A), lambda i: (0, 0))],
        out_specs=[_row_spec(SGU_BLOCK, 2 * W_A), _vec_spec(W_A), _vec_spec(W_A),
                   whole3((G_A, SGU_BLOCK, SGU_BLOCK)), whole3((G_A, SGU_BLOCK, 1))],
        out_shape=[jax.ShapeDtypeStruct((S, 2 * W_A), BF16), jax.ShapeDtypeStruct((1, W_A), F32),
                   jax.ShapeDtypeStruct((1, W_A), F32), jax.ShapeDtypeStruct((G_A, SGU_BLOCK, SGU_BLOCK), F32),
                   jax.ShapeDtypeStruct((G_A, SGU_BLOCK, 1), F32)],
        scratch_shapes=[pltpu.VMEM((G_A, SGU_BLOCK, SGU_BLOCK), F32)],
        compiler_params=_params("arbitrary"),
    )(z, z, dya, ln_g, ln_b, w_s, b_t)


def _log_sigmoid(z):
    return jnp.minimum(z, 0.0) - jnp.log(1.0 + jnp.exp(-jnp.abs(z)))


def _suffix_sum(x, upper):
    hi = x.astype(BF16)
    lo = (x - hi.astype(F32)).astype(BF16)
    return _dot(hi, upper) + _dot(lo, upper)


SB_ROWS = 1024
_SB_SUB = SB_ROWS // Q_BLOCK


def _sb_upper():
    row = lax.broadcasted_iota(jnp.int32, (Q_BLOCK, Q_BLOCK), 0)
    col = lax.broadcasted_iota(jnp.int32, (Q_BLOCK, Q_BLOCK), 1)
    return (row > col).astype(BF16)


def _sb_sweep(step, tile):
    for r in reversed(range(_SB_SUB)):
        tile(step * _SB_SUB + r, r * Q_BLOCK)

    def group(g, _):
        base = (step - 1 - g) * _SB_SUB
        for r in reversed(range(_SB_SUB)):
            tile(base + r, None)
        return 0

    lax.fori_loop(0, step, group, 0)


def _sb_causal(n):
    return lax.broadcasted_iota(jnp.int32, (n, Q_BLOCK), 1) < lax.broadcasted_iota(jnp.int32, (n, Q_BLOCK), 0)


def _sb_rows_spec():
    return pl.BlockSpec((None, SB_ROWS, DH_B), lambda h, i: (h, i, 0))


def _sb_head_spec(S, part=0):
    return pl.BlockSpec((None, S, DH_B), lambda h, i: (part * H_B + h, 0, 0))


def sb_fwd(qkv, name, comm=None):
    S = qkv.shape[1]

    def body(q_b, k_ref, v_ref, o_ref, c_l1m):
        step = pl.program_id(1)
        o_ref[...] = _zeros(o_ref)
        c_l1m[...] = _zeros(c_l1m)
        upper = _sb_upper()

        def tile(j, row0):
            rq = slice(row0 or 0, SB_ROWS)
            causal = None if row0 is None else _sb_causal(SB_ROWS - row0)
            rows = pl.ds(pl.multiple_of(j * Q_BLOCK, Q_BLOCK), Q_BLOCK)
            zz = _dot_nt(q_b[rq, :], k_ref[rows, :])
            lb = _log_sigmoid(zz)
            l1m = lb - zz
            if causal is not None:
                l1m = jnp.where(causal, l1m, 0.0)
            a = jnp.exp(lb + _suffix_sum(l1m, upper) + c_l1m[rq, :])
            if causal is not None:
                a = jnp.where(causal, a, 0.0)
            o_ref[rq, :] += _dot(a.astype(BF16), v_ref[rows, :])
            c_l1m[rq, :] += jnp.sum(l1m, axis=1, keepdims=True)

        _sb_sweep(step, tile)

    return _pcall(
        comm, body, name=name, grid=(H_B, S // SB_ROWS),
        in_specs=[_sb_rows_spec(), _sb_head_spec(S, 1), _sb_head_spec(S, 2)],
        out_specs=_sb_rows_spec(),
        out_shape=jax.ShapeDtypeStruct((H_B, S, DH_B), F32),
        scratch_shapes=[pltpu.VMEM((SB_ROWS, 1), F32)],
        compiler_params=_params("parallel", "parallel"),
    )(qkv, qkv, qkv)


def sb_bwd(qkv, out, dout, name, comm=None):
    S = qkv.shape[1]
    nstep = S // SB_ROWS
    scale = DH_B ** -0.5

    def body(q_b, k_ref, v_ref, o_ref, do_ref, dq_ref, dk_ref, dv_ref,
             dq_acc, dkt_acc, dvt_acc, do_b, qt_b, dot_b, g_left, c_l1m):
        step = pl.program_id(1)

        @pl.when(step == 0)
        def _():
            dkt_acc[...] = _zeros(dkt_acc)
            dvt_acc[...] = _zeros(dvt_acc)

        do_b[...] = do_ref[...].astype(BF16)
        qt_b[...] = q_b[...].astype(F32).T.astype(BF16)
        dot_b[...] = do_ref[...].T.astype(BF16)
        g_left[...] = jnp.sum(do_b[...].astype(F32) * o_ref[...], axis=1, keepdims=True)
        dq_acc[...] = _zeros(dq_acc)
        c_l1m[...] = _zeros(c_l1m)
        upper = _sb_upper()

        def tile(j, row0):
            rq = slice(row0 or 0, SB_ROWS)
            causal = None if row0 is None else _sb_causal(SB_ROWS - row0)
            rows = pl.ds(pl.multiple_of(j * Q_BLOCK, Q_BLOCK), Q_BLOCK)
            q, do_t = q_b[rq, :], do_b[rq, :]
            k_j = k_ref[rows, :]
            zz = _dot_nt(q, k_j)
            lb = _log_sigmoid(zz)
            l1m = lb - zz
            if causal is not None:
                l1m = jnp.where(causal, l1m, 0.0)
            a = jnp.exp(lb + _suffix_sum(l1m, upper) + c_l1m[rq, :])
            if causal is not None:
                a = jnp.where(causal, a, 0.0)
            a_b = a.astype(BF16)
            dvt_acc[:, rows] += _dot(dot_b[:, rq], a_b)
            gmat = a_b.astype(F32) * _dot_nt(do_t, v_ref[rows, :])
            before = g_left[rq, :] - gmat - _suffix_sum(gmat, upper)
            sig = jnp.exp(lb)
            dz = gmat * (1.0 - sig) - sig * before
            if causal is not None:
                dz = jnp.where(causal, dz, 0.0)
            dz_b = dz.astype(BF16)
            dkt_acc[:, rows] += _dot(qt_b[:, rq], dz_b)
            dq_acc[rq, :] += _dot(dz_b, k_j)
            c_l1m[rq, :] += jnp.sum(l1m, axis=1, keepdims=True)
            g_left[rq, :] -= jnp.sum(gmat, axis=1, keepdims=True)

        _sb_sweep(step, tile)
        dq_ref[...] = (dq_acc[...] * scale).astype(BF16)

        @pl.when(step == nstep - 1)
        def _():
            dk_ref[...] = dkt_acc[...].T.astype(BF16)
            dv_ref[...] = dvt_acc[...].T.astype(BF16)

    out_sds = jax.ShapeDtypeStruct((H_B, S, DH_B), BF16)
    return _pcall(
        comm, body, name=name, grid=(H_B, nstep),
        in_specs=[_sb_rows_spec(), _sb_head_spec(S, 1), _sb_head_spec(S, 2), _sb_rows_spec(), _sb_rows_spec()],
        out_specs=[_sb_rows_spec(), _sb_head_spec(S), _sb_head_spec(S)],
        out_shape=[out_sds, out_sds, out_sds],
        scratch_shapes=[pltpu.VMEM((SB_ROWS, DH_B), F32)] + [pltpu.VMEM((DH_B, S), F32)] * 2
        + [pltpu.VMEM((SB_ROWS, DH_B), BF16)] + [pltpu.VMEM((DH_B, SB_ROWS), BF16)] * 2
        + [pltpu.VMEM((SB_ROWS, 1), F32)] * 2,
        compiler_params=_params("parallel", "arbitrary"),
    )(qkv, qkv, qkv, out, dout)


def _softmax(s):
    e = jnp.exp(s - jnp.max(s, axis=-1, keepdims=True))
    return e / jnp.sum(e, axis=-1, keepdims=True)


def xattn_fwd(qc, kv, name, comm=None):
    S, D = qc.shape
    tr = min(ROW_TILE, S)

    def body(q_ref, kv_ref, o_ref):
        for h in range(X_HEADS):
            cols = slice(h * X_DH, (h + 1) * X_DH)
            p = _softmax(_dot_nt(q_ref[:, cols], kv_ref[:, cols]))
            o_ref[:, cols] = _dot(p.astype(BF16), kv_ref[:, D + h * X_DH:D + (h + 1) * X_DH]).astype(BF16)

    return _pcall(
        comm, body, name=name, grid=(S // tr,),
        in_specs=[_row_spec(tr, D), pl.BlockSpec((N_MEM, 2 * D), lambda i: (0, 0))],
        out_specs=_row_spec(tr, D),
        out_shape=jax.ShapeDtypeStruct((S, D), BF16),
        compiler_params=_params("parallel"),
    )(qc, kv)


def xattn_bwd(qc, kv, do, name, comm=None):
    S, D = qc.shape
    tr = min(ROW_TILE, S)
    nstep = S // tr
    scale = X_DH ** -0.5

    def body(q_ref, kv_ref, do_ref, dq_ref, dkv_ref, acc):
        step = pl.program_id(0)

        @pl.when(step == 0)
        def _():
            acc[...] = _zeros(acc)

        for h in range(X_HEADS):
            cols = slice(h * X_DH, (h + 1) * X_DH)
            vcols = slice(D + h * X_DH, D + (h + 1) * X_DH)
            q = q_ref[:, cols]
            k = kv_ref[:, cols]
            do_h = do_ref[:, cols]
            p = _softmax(_dot_nt(q, k))
            dp = _dot_nt(do_h, kv_ref[:, vcols])
            acc[:, vcols] += _dot_tn(p.astype(BF16), do_h)
            ds = (p * (dp - jnp.sum(p * dp, axis=-1, keepdims=True))).astype(BF16)
            dq_ref[:, cols] = (_dot(ds, k) * scale).astype(BF16)
            acc[:, cols] += _dot_tn(ds, q)

        @pl.when(step == nstep - 1)
        def _():
            dkv_ref[...] = acc[...].astype(BF16)

    whole = pl.BlockSpec((N_MEM, 2 * D), lambda i: (0, 0))
    return _pcall(
        comm, body, name=name, grid=(nstep,),
        in_specs=[_row_spec(tr, D), whole, _row_spec(tr, D)],
        out_specs=[_row_spec(tr, D), whole],
        out_shape=[jax.ShapeDtypeStruct((S, D), BF16), jax.ShapeDtypeStruct((N_MEM, 2 * D), BF16)],
        scratch_shapes=[pltpu.VMEM((N_MEM, 2 * D), F32)],
        compiler_params=_params("arbitrary"),
    )(qc, kv, do)


def _row_tile(rows, cap=128):
    return max(t for t in range(16, cap + 1, 16) if rows % t == 0)


def cast_bf16(ws, name, steps=4, comm=None):
    n = len(ws)

    def body(*refs):
        for i in range(n):
            refs[n + i][...] = refs[i][...].astype(BF16)

    specs = [_row_spec(w.shape[0] // steps, w.shape[1]) for w in ws]
    return _pcall(
        comm, body, name=name, grid=(steps,), in_specs=specs, out_specs=specs,
        out_shape=[jax.ShapeDtypeStruct(w.shape, BF16) for w in ws],
        compiler_params=_params("parallel"),
    )(*ws)


def adamw(parts, w, m, v, name, comm=None):
    R, C = w.shape
    n_parts = parts.shape[0]
    tr = _row_tile(R, 256)
    c1 = 1.0 - ADAM_B1 ** ADAM_STEP
    c2 = 1.0 - ADAM_B2 ** ADAM_STEP

    def body(p_ref, w_ref, m_ref, v_ref, g_ref, d_ref, mo_ref, vo_ref):
        g = p_ref[0].astype(F32)
        for p in range(1, n_parts):
            g = g + p_ref[p].astype(F32)
        m_new = ADAM_B1 * m_ref[...] + (1.0 - ADAM_B1) * g
        v_new = ADAM_B2 * v_ref[...] + (1.0 - ADAM_B2) * (g * g)
        g_ref[...] = g
        mo_ref[...] = m_new
        vo_ref[...] = v_new
        d_ref[...] = -ADAM_LR * ((m_new / c1) / (jnp.sqrt(v_new / c2) + ADAM_EPS) + ADAM_WD * w_ref[...])

    spec = _row_spec(tr, C)
    sds = jax.ShapeDtypeStruct((R, C), F32)
    return _pcall(
        comm, body, name=name, grid=(R // tr,),
        in_specs=[pl.BlockSpec((n_parts, tr, C), lambda i: (0, i, 0)), spec, spec, spec],
        out_specs=[spec, spec, spec, spec],
        out_shape=[sds, sds, sds, sds],
        compiler_params=_params("parallel"),
    )(parts, w, m, v)


def pair_sum(parts, from_sibling, core, name):
    _, R, C = parts.shape
    tr = _row_tile(R, 1024)

    def body(core_ref, p_ref, s_ref, o_ref):
        o_ref[...] = (p_ref[...].astype(F32) + s_ref[...].astype(F32)).astype(o_ref.dtype)

    return pl.pallas_call(
        body, name=name,
        grid_spec=pltpu.PrefetchScalarGridSpec(
            num_scalar_prefetch=1, grid=(4, R // tr),
            in_specs=[pl.BlockSpec((None, tr, C), lambda q, i, core_ref: (2 * q + core_ref[0], i, 0)),
                      pl.BlockSpec((None, tr, C), lambda q, i, core_ref: (q, i, 0))],
            out_specs=pl.BlockSpec((None, tr, C), lambda q, i, core_ref: (q, i, 0))),
        out_shape=jax.ShapeDtypeStruct((4, R, C), BF16),
        compiler_params=_params("parallel", "parallel"),
    )(core, parts, from_sibling)


def add2(a, b, name, comm=None):
    R, C = a.shape
    tr = _row_tile(R, 256)

    def body(a_ref, b_ref, o_ref):
        o_ref[...] = a_ref[...] + b_ref[...]

    spec = _row_spec(tr, C)
    return _pcall(
        comm, body, name=name, grid=(R // tr,), in_specs=[spec, spec], out_specs=spec,
        out_shape=jax.ShapeDtypeStruct((R, C), F32), compiler_params=_params("parallel"),
    )(a, b)


def _place():
    return lax.axis_index("x"), lax.axis_index("y"), lax.axis_index("c")


class Comm:
    def __init__(self, arrays, out_shapes, n_remote, n_local, start, finish, aliases=None):
        self.arrays, self.out_shapes = list(arrays), list(out_shapes)
        self.n_remote, self.n_local = n_remote, max(n_local, 1)
        self.start, self.finish = start, finish
        self.aliases = dict(aliases or {})
        self.sizes = [len(self.out_shapes)]

    def sem_shapes(self):
        return [pltpu.SemaphoreType.DMA((self.n_remote,)), pltpu.SemaphoreType.DMA((self.n_remote,)),
                pltpu.SemaphoreType.DMA((self.n_local,))]


class _Shifted:
    def __init__(self, ref, offset):
        self.ref, self.offset = ref, offset

    @property
    def at(self):
        return self

    def __getitem__(self, k):
        return self.ref.at[self.offset + k]


def merge_comms(comms):
    comms = [c for c in comms if c is not None]
    if not comms:
        return None

    def each(method):
        def run(ins, outs, sems):
            i = o = r = l = 0
            for c in comms:
                sub = (_Shifted(sems[0], r), _Shifted(sems[1], r), _Shifted(sems[2], l))
                getattr(c, method)(ins[i:i + len(c.arrays)], outs[o:o + len(c.out_shapes)], sub)
                i, o, r, l = i + len(c.arrays), o + len(c.out_shapes), r + c.n_remote, l + c.n_local
        return run

    aliases, i, o = {}, 0, 0
    for c in comms:
        aliases.update({i + a: o + b for a, b in c.aliases.items()})
        i, o = i + len(c.arrays), o + len(c.out_shapes)
    merged = Comm([a for c in comms for a in c.arrays], [s for c in comms for s in c.out_shapes],
                  sum(c.n_remote for c in comms), sum(c.n_local for c in comms), each("start"), each("finish"), aliases)
    merged.sizes = [len(c.out_shapes) for c in comms]
    return merged


def split_results(comm, results):
    out, i = [], 0
    for n in comm.sizes:
        out.append(list(results[i:i + n]))
        i += n
    return out


def run_comm(comm, name):
    n_in, n_out = len(comm.arrays), len(comm.out_shapes)

    def body(*refs):
        ins, outs, sems = refs[:n_in], refs[n_in:n_in + n_out], refs[n_in + n_out:]
        comm.start(ins, outs, sems)
        comm.finish(ins, outs, sems)

    return pl.pallas_call(
        body, name=name, in_specs=[ANY] * n_in, out_specs=[ANY] * n_out, out_shape=comm.out_shapes,
        scratch_shapes=comm.sem_shapes(), input_output_aliases=comm.aliases,
    )(*comm.arrays)


def _remote(src, dst, sems, k, to):
    return pltpu.make_async_remote_copy(src_ref=src, dst_ref=dst, send_sem=sems[0].at[k], recv_sem=sems[1].at[k],
                                        device_id=to, device_id_type=MESH)


_AG_COPIES = 13


def comm_all_gather(shards, rows=None, into=None):
    n = len(shards)
    row0, nrows = rows if rows is not None else (0, None)

    def parties():
        x, y, c = _place()
        return (x, y, c), (x, y, 1 - c), [(1 - x, y), (x, 1 - y), (1 - x, 1 - y)]

    def span(w, half=None):
        count = nrows if nrows is not None else shards[w].shape[0]
        if half is None:
            return pl.ds(row0, count)
        return pl.ds(row0 + half * (count // 2), count // 2)

    def slab(outs, w, dev, half=None):
        return outs[w].at[4 * dev[0] + 2 * dev[1] + dev[2], span(w, half)]

    def own(ins, outs, sems):
        me, sibling, chips = parties()
        local = [pltpu.make_async_copy(ins[w].at[span(w)], slab(outs, w, me), sems[2].at[w]) for w in range(n)]
        first = []
        for w in range(n):
            k = _AG_COPIES * w
            first.append(_remote(ins[w].at[span(w)], slab(outs, w, me), sems, k, sibling))
            first += [_remote(ins[w].at[span(w, h)], slab(outs, w, me, h), sems, k + 1 + 2 * j + h, (*chip, me[2]))
                      for h in range(2) for j, chip in enumerate(chips)]
        return local, first

    def start(ins, outs, sems):
        local, first = own(ins, outs, sems)
        for cp in local + first:
            cp.start()

    def finish(ins, outs, sems):
        me, sibling, chips = parties()
        local, first = own(ins, outs, sems)
        passed = []
        for w in range(n):
            k = _AG_COPIES * w
            for h in range(2):
                for j, chip in enumerate(chips):
                    got = slab(outs, w, (*chip, me[2]), h)
                    _remote(got, got, sems, k + 1 + 2 * j + h, me).wait_recv()
                    cp = _remote(got, got, sems, k + 7 + 2 * j + h, sibling)
                    cp.start()
                    passed.append(cp)
        for w in range(n):
            k = _AG_COPIES * w
            got = slab(outs, w, sibling)
            _remote(got, got, sems, k, me).wait_recv()
            for h in range(2):
                for j, chip in enumerate(chips):
                    got = slab(outs, w, (*chip, sibling[2]), h)
                    _remote(got, got, sems, k + 7 + 2 * j + h, me).wait_recv()
        for cp in first + passed:
            cp.wait_send()
        for cp in local:
            cp.wait()

    out_shapes = [jax.ShapeDtypeStruct((N_DEV,) + s.shape, s.dtype) for s in shards]
    arrays = list(shards) + (list(into) if into is not None else [])
    aliases = {n + w: w for w in range(n)} if into is not None else None
    return Comm(arrays, out_shapes, _AG_COPIES * n, n, start, finish, aliases)


def comm_pairs(items):
    slabbed = [a.ndim == 3 for a in items]
    first = [sum(4 if s else 1 for s in slabbed[:w]) for w in range(len(items))]

    def copies(ins, outs, sems):
        x, y, c = _place()
        sibling = (x, y, 1 - c)
        cps = []
        for w, s in enumerate(slabbed):
            if s:
                cps += [_remote(ins[w].at[2 * q + (1 - c)], outs[w].at[q], sems, first[w] + q, sibling) for q in range(4)]
            else:
                cps.append(_remote(ins[w], outs[w], sems, first[w], sibling))
        return cps

    def start(ins, outs, sems):
        for cp in copies(ins, outs, sems):
            cp.start()

    def finish(ins, outs, sems):
        for cp in copies(ins, outs, sems):
            cp.wait()

    out_shapes = [jax.ShapeDtypeStruct(((4,) + a.shape[1:]) if s else a.shape, a.dtype) for a, s in zip(items, slabbed)]
    return Comm(items, out_shapes, sum(4 if s else 1 for s in slabbed), 0, start, finish)


def comm_chips(items, rows=None, into=None, from_row=None, out_rows=None):
    n = len(items)
    slabbed = [a.ndim == 3 for a in items]

    def span(w, source=False):
        if rows is None:
            return pl.ds(0, items[w].shape[-2])
        return pl.ds(from_row if source and from_row is not None else rows[0], rows[1])

    def copies(ins, outs, sems):
        x, y, c = _place()
        mine = 2 * x + y
        local = [pltpu.make_async_copy(ins[w].at[mine, span(w, True)] if slabbed[w] else ins[w].at[span(w, True)],
                                       outs[w].at[mine, span(w)], sems[2].at[w]) for w in range(n)]
        remote = []
        for w in range(n):
            for j, (px, py) in enumerate([(1 - x, y), (x, 1 - y), (1 - x, 1 - y)]):
                src = ins[w].at[2 * px + py, span(w, True)] if slabbed[w] else ins[w].at[span(w, True)]
                remote.append(_remote(src, outs[w].at[mine, span(w)], sems, 3 * w + j, (px, py, c)))
        return local, remote

    def start(ins, outs, sems):
        local, remote = copies(ins, outs, sems)
        for cp in local + remote:
            cp.start()

    def finish(ins, outs, sems):
        local, remote = copies(ins, outs, sems)
        for cp in remote + local:
            cp.wait()

    def result(a):
        tall = a.shape[:-2] + (out_rows if out_rows is not None else a.shape[-2], a.shape[-1])
        return jax.ShapeDtypeStruct(tall if a.ndim == 3 else (4,) + tall, a.dtype)

    if into is None:
        return Comm(items, [result(a) for a in items], 3 * n, n, start, finish)
    out_shapes = [jax.ShapeDtypeStruct(b.shape, b.dtype) for b in into]
    return Comm(list(items) + list(into), out_shapes, 3 * n, n, start, finish, {n + w: w for w in range(n)})


_SMALL = ("ffn1_norm", "mix_norm", "ln_v_gain", "ln_v_bias", "spatial_w", "spatial_b", "gnorm_a", "gnorm_b",
          "cross_norm", "mem_norm", "ffn2_norm", "final_norm")
_BIG = ("ffn1_w_in", "ffn1_w_out", "w_mix_in", "w_mix_out", "w_cq", "w_ckv", "w_co", "ffn2_w_in", "ffn2_w_out")
_COL_SHARDED = ("ffn1_w_in", "w_mix_in", "w_ckv", "ffn2_w_in")
_ORDER = ("ffn1_norm", "ffn1_w_in", "ffn1_w_out", "mix_norm", "w_mix_in", "ln_v_gain", "ln_v_bias", "spatial_w",
          "spatial_b", "gnorm_a", "gnorm_b", "w_mix_out", "cross_norm", "mem_norm", "w_cq", "w_ckv", "w_co",
          "ffn2_norm", "ffn2_w_in", "ffn2_w_out", "final_norm")


_SMALL_PAD = 136


def _rows128(a):
    return a.reshape(-1, 128)


def kernel(x, mem, ffn1_norm, ffn1_w_in, ffn1_w_out, mix_norm, w_mix_in, ln_v_gain, ln_v_bias, spatial_w, spatial_b, gnorm_a, gnorm_b, w_mix_out, cross_norm, mem_norm, w_cq, w_ckv, w_co, ffn2_norm, ffn2_w_in, ffn2_w_out, final_norm, loss_target, m_ffn1_norm, m_ffn1_w_in, m_ffn1_w_out, m_mix_norm, m_w_mix_in, m_ln_v_gain, m_ln_v_bias, m_spatial_w, m_spatial_b, m_gnorm_a, m_gnorm_b, m_w_mix_out, m_cross_norm, m_mem_norm, m_w_cq, m_w_ckv, m_w_co, m_ffn2_norm, m_ffn2_w_in, m_ffn2_w_out, m_final_norm, v_ffn1_norm, v_ffn1_w_in, v_ffn1_w_out, v_mix_norm, v_w_mix_in, v_ln_v_gain, v_ln_v_bias, v_spatial_w, v_spatial_b, v_gnorm_a, v_gnorm_b, v_w_mix_out, v_cross_norm, v_mem_norm, v_w_cq, v_w_ckv, v_w_co, v_ffn2_norm, v_ffn2_w_in, v_ffn2_w_out, v_final_norm):
    given = dict(locals())
    wts = {k: given[k] for k in _ORDER}
    mom = {k: given["m_" + k] for k in _ORDER}
    var = {k: given["v_" + k] for k in _ORDER}

    D = D_MODEL
    xs = x.reshape(-1, D)
    mems = mem.reshape(-1, D)
    tgt = loss_target.reshape(-1, D)
    vec = lambda a: a.reshape(1, -1)
    g1, gmix, gcross, gmem, g2, gfin = (vec(wts[k]) for k in
                                        ("ffn1_norm", "mix_norm", "cross_norm", "mem_norm", "ffn2_norm", "final_norm"))
    ln_g, ln_b, ga, gb = (vec(wts[k]) for k in ("ln_v_gain", "ln_v_bias", "gnorm_a", "gnorm_b"))
    w_s = spatial_w.reshape(G_A, SGU_BLOCK, SGU_BLOCK)
    b_t = spatial_b.reshape(G_A, SGU_BLOCK).T

    shard2d = {k: wts[k].reshape(wts[k].shape[1:]) for k in _BIG}
    (first_b,) = cast_bf16([shard2d[_BIG[0]]], "cast_first")
    shard_b = {_BIG[0]: first_b}
    full = {}

    landing, rows_done = {}, {}

    def gathering(pieces, fn, *args, **kw):
        pieces = [p if isinstance(p, tuple) else (p, None) for p in pieces]
        comm = merge_comms([comm_all_gather([shard_b[k]], rows, [landing[k]] if k in landing else None)
                            for k, rows in pieces])
        out, got = fn(*args, comm=comm, **kw)
        for (k, rows), (g,) in zip(pieces, split_results(comm, got)):
            landing[k] = g
            rows_done[k] = rows_done.get(k, 0) + (rows[1] if rows is not None else shard_b[k].shape[0])
            if rows_done[k] == shard_b[k].shape[0]:
                full[k] = g if k in _COL_SHARDED else g.reshape(-1, g.shape[2])
        return out

    shard_b.update(zip(_BIG[1:], gathering((_BIG[0],), cast_bf16, [shard2d[k] for k in _BIG[1:]], "cast_rest")))
    n1 = rms_fwd(xs, g1, "f_n1")
    a1, hsw1 = gathering(("ffn1_w_out",), mm_swiglu_g, n1, full["ffn1_w_in"], "f_a1")
    h1 = gathering(("w_mix_in",), mm_nn, hsw1, full["ffn1_w_out"], F32, "f_h1", scale=0.5, res=xs)
    n2 = gathering(("w_cq",), rms_fwd, h1, gmix, "f_n2")
    z = gathering(("w_mix_out", "w_co"), mm_nn_g, n2, full["w_mix_in"], F32, "f_z")
    ya = gathering((("w_ckv", (0, 512)),), sgu_fwd, z, ln_g, ln_b, w_s, b_t, "f_sgu")
    qkv = split_heads(z, "f_qkv")
    yb = gathering((("w_ckv", (512, 1536)), ("ffn2_w_in", (0, 512))), sb_fwd, qkv, "f_sb")
    ycat = gathering((("ffn2_w_in", (512, 128)),), rmscat_fwd, ya, yb, ga, gb, "f_ycat")
    h2 = gathering((("ffn2_w_in", (640, 256)),), mm_nn, ycat, full["w_mix_out"], F32, "f_h2", res=h1)
    n3 = gathering((("ffn2_w_in", (896, 128)),), rms_fwd, h2, gcross, "f_n3")
    memn = rms_fwd(mems, gmem, "f_memn")
    qc = gathering((("ffn2_w_in", (1024, 256)),), mm_nn, n3, full["w_cq"], BF16, "f_qc", scale=X_DH ** -0.5)
    kv = gathering((("ffn2_w_in", (1280, 128)),), mm_nn_g, memn, full["w_ckv"], BF16, "f_kv")
    o = gathering((("ffn2_w_in", (1408, 128)),), xattn_fwd, qc, kv, "f_xattn")
    h3 = gathering((("ffn2_w_in", (1536, 256)),), mm_nn, o, full["w_co"], F32, "f_h3", res=h2)
    n4 = gathering((("ffn2_w_in", (1792, 256)),), rms_fwd, h3, g2, "f_n4")
    a2, hsw2 = gathering(("ffn2_w_out",), mm_swiglu_g, n4, full["ffn2_w_in"], "f_a2")
    h4 = mm_nn(hsw2, full["ffn2_w_out"], F32, "f_h4", scale=0.5, res=h3)

    grads, parts, sums, recv = {}, {}, {}, {}
    core = lax.axis_index("c").astype(jnp.int32).reshape(1)

    def partial_of(k, g):
        grads[k] = g
        parts[k] = g if g.ndim == 3 else g.reshape(N_DEV, -1, g.shape[1])

    def reducing(pairs, chips, fn, *args, also=None, **kw):
        def piece(p):
            if isinstance(p, dict):
                return p
            k, rows = p if isinstance(p, tuple) else (p, None)
            return dict(sums=k, rows=rows, to=k)

        chips = [piece(p) for p in chips]
        comms = [comm_pairs([parts[k] for k in pairs])] if pairs else []
        comms += [comm_chips([sums[p["sums"]]], p["rows"], [recv[p["to"]]] if p["to"] in recv else None,
                             p.get("from_row"), p.get("out_rows")) for p in chips]
        comm = merge_comms(comms + ([also] if also is not None else []))
        out, got = fn(*args, comm=comm, **kw)
        got = split_results(comm, got)
        if pairs:
            for k, r in zip(pairs, got.pop(0)):
                sums[k] = pair_sum(parts[k], r, core, f"pair_sum_{k}")
        for p, (r,) in zip(chips, got):
            recv[p["to"]] = r
        return out if also is None else (out, got[-1])

    loss_part, dh4, df2, grads["final_norm"] = loss_head(h4, tgt, gfin, "loss_head")
    partial_of("ffn2_w_out", mm_tn(hsw2, df2, "b_ffn2_dwout"))
    da2 = reducing(("ffn2_w_out",), (), mm_swiglu_bwd, df2, full["ffn2_w_out"], a2, "b_ffn2_da")
    partial_of("ffn2_w_in", reducing((), ("ffn2_w_out",), mm_tn_g, n4, da2, N_DEV, "b_ffn2_dwin"))
    dn4 = reducing(("ffn2_w_in",), (), mm_nt_g, da2, full["ffn2_w_in"], "b_ffn2_dn")
    dh3, dh3b, grads["ffn2_norm"] = rms_bwd(dn4, h3, g2, dh4, 1.0, "b_n4")

    partial_of("w_co", mm_tn(o, dh3b, "b_dwco"))
    do = reducing(("w_co",), (("ffn2_w_in", (0, 256)),), mm_nt, dh3b, full["w_co"], BF16, "b_do")
    dqp, dkv = reducing((), (("ffn2_w_in", (256, 256)),), xattn_bwd, qc, kv, do, "b_xattn")
    partial_of("w_cq", mm_tn(n3, dqp, "b_dwcq"))
    dn3 = reducing(("w_cq",), (("ffn2_w_in", (512, 256)),), mm_nt, dqp, full["w_cq"], F32, "b_dn3")
    partial_of("w_ckv", mm_tn_g(memn, dkv, N_DEV, "b_dwckv"))
    dmemn = reducing(("w_ckv",), (("ffn2_w_in", (768, 128)),), mm_nt_g, dkv, full["w_ckv"], "b_dmemn")
    _, _, grads["mem_norm"] = rms_bwd(dmemn, mems, gmem, None, 1.0, "b_memn")
    dh2, dh2b, grads["cross_norm"] = rms_bwd(dn3, h2, gcross, dh3, 1.0, "b_n3")

    partial_of("w_mix_out", mm_tn(ycat, dh2b, "b_dwmixout"))
    dycat = reducing(("w_mix_out",), (("ffn2_w_in", (896, 256)),), mm_nt, dh2b, full["w_mix_out"], F32, "b_dycat")
    dya, dyb, grads["gnorm_a"], grads["gnorm_b"] = rmscat_bwd(dycat, ya, yb, ga, gb, "b_ycat")
    dza, grads["ln_v_gain"], grads["ln_v_bias"], grads["spatial_w"], grads["spatial_b"] = reducing(
        (), (("ffn2_w_in", (1152, 384)),), sgu_bwd, z, dya, ln_g, ln_b, w_s, b_t, "b_sgu")
    dq, dk, dv = reducing((), (("ffn2_w_in", (1536, 512)), "w_co", "w_cq", ("w_ckv", (0, 1024))), sb_bwd, qkv, yb,
                          dyb, "b_sb")
    dz = join_dz(dza, dq, dk, dv, "b_dz")
    partial_of("w_mix_in", reducing((), (("w_ckv", (1024, 1024)),), mm_tn_g, n2, dz, N_DEV, "b_dwmixin"))
    dn2 = reducing(("w_mix_in",), ("w_mix_out",), mm_nt_g, dz, full["w_mix_in"], "b_dn2")
    dh1, dh1b, grads["mix_norm"] = reducing((), (("w_mix_in", (0, 512)),), rms_bwd, dn2, h1, gmix, dh2, 0.5, "b_n2")

    pad = jnp.zeros((_SMALL_PAD, 128), F32)
    small_early = jnp.concatenate([_rows128(grads[k]) for k in _SMALL[1:]] + [pad], axis=0)
    g_w1out, (early_sibling,) = reducing((), (("w_mix_in", (512, 1024)),), mm_tn, hsw1, dh1b, "b_ffn1_dwout",
                                         also=comm_pairs([small_early]))
    partial_of("ffn1_w_out", g_w1out)
    early_pair = add2(small_early, early_sibling, "pair_sum_small_early")
    da1, (early_all,) = reducing(("ffn1_w_out",), (("w_mix_in", (1536, 512)),), mm_swiglu_bwd, dh1b,
                                 full["ffn1_w_out"], a1, "b_ffn1_da", also=comm_chips([early_pair]))
    half = D // 2
    partial_of("ffn1_w_in_a", reducing((), (("ffn1_w_out", (0, 352)),), mm_tn_g, n1, da1, N_DEV, "b_ffn1_dwin_a",
                                       cols=(0, half)))
    partial_of("ffn1_w_in_b", reducing(("ffn1_w_in_a",), (("ffn1_w_out", (352, 352)),), mm_tn_g, n1, da1, N_DEV,
                                       "b_ffn1_dwin_b", cols=(half, half)))
    dn1 = reducing(("ffn1_w_in_b",), (dict(sums="ffn1_w_in_a", rows=(0, half), to="ffn1_w_in", out_rows=D),),
                   mm_nt_g, da1, full["ffn1_w_in"], "b_ffn1_dn")
    (dx, _, grads["ffn1_norm"]) = reducing((), (dict(sums="ffn1_w_in_b", rows=(half, half), from_row=0, to="ffn1_w_in"),),
                                           rms_bwd, dn1, xs, g1, dh1, 1.0, "b_n1")

    out_g, out_d, out_m, out_v = {}, {}, {}, {}
    for k in _BIG:
        res = adamw(recv[k], shard2d[k], mom[k].reshape(shard2d[k].shape), var[k].reshape(shard2d[k].shape), f"adamw_{k}")
        out_g[k], out_d[k], out_m[k], out_v[k] = (t.reshape(wts[k].shape) for t in res)

    small_late = _rows128(grads[_SMALL[0]])
    (late_sibling,) = run_comm(comm_pairs([small_late]), "comm_pairs_small_late")
    late_pair = add2(small_late, late_sibling, "pair_sum_small_late")
    (late_all,) = run_comm(comm_chips([late_pair]), "comm_chips_small_late")
    small_all = jnp.concatenate([late_all, early_all], axis=1)
    pack = lambda d: jnp.concatenate([_rows128(d[k]) for k in _SMALL] + [pad], axis=0)
    res = adamw(small_all, pack(wts), pack(mom), pack(var), "adamw_small")
    row = 0
    for k in _SMALL:
        nrow = wts[k].size // 128
        for dst, t in zip((out_g, out_d, out_m, out_v), res):
            dst[k] = t[row:row + nrow].reshape(wts[k].shape)
        row += nrow

    loss = lax.psum(loss_part[0, 0], ("x", "y", "c"))
    grad_x = dx.reshape(x.shape)
    return (loss, grad_x, *[out_g[k] for k in _ORDER], *[out_d[k] for k in _ORDER],
            *[out_m[k] for k in _ORDER], *[out_v[k] for k in _ORDER])
```

```python
import functools
import math

import jax
import jax.numpy as jnp
from jax import lax
from jax.experimental import pallas as pl
from jax.experimental.pallas import tpu as pltpu

F32 = jnp.float32
BF16 = jnp.bfloat16

N_DEV = 8
D_MODEL = 2048
D_FF = 5632
W_A = 1024
G_A = 8
GA_DIM = 128
SGU_BLOCK = 128
CHUNK = 64
H_B = 8
DH_B = 128
Q_BLOCK = 128
X_HEADS = 4
X_DH = 512
N_MEM = 256
EPS = 1e-6

ADAM_LR = 0.001
ADAM_B1 = 0.9
ADAM_B2 = 0.999
ADAM_EPS = 1e-08
ADAM_WD = 0.01
ADAM_STEP = 10

VMEM_LIMIT = 56 * 2**20
ROW_TILE = 256

MESH = pl.DeviceIdType.MESH
ANY = pl.BlockSpec(memory_space=pl.ANY)

_NT = (((1,), (1,)), ((), ()))
_TN = (((0,), (0,)), ((), ()))


def _params(*sem, collective_id=None):
    return pltpu.CompilerParams(dimension_semantics=sem, vmem_limit_bytes=VMEM_LIMIT, collective_id=collective_id)


_COLLECTIVE_ID = {frozenset({"sibling"}): 0, frozenset({"chips"}): 1, frozenset({"sibling", "chips"}): 2}


def _handshake(partners):
    x, y, c = lax.axis_index("x"), lax.axis_index("y"), lax.axis_index("c")
    peers = [(x, y, 1 - c)] if "sibling" in partners else []
    if "chips" in partners:
        peers += [(1 - x, y, c), (x, 1 - y, c), (1 - x, 1 - y, c)]
    barrier = pltpu.get_barrier_semaphore()
    for peer in peers:
        pl.semaphore_signal(barrier, inc=1, device_id=peer, device_id_type=MESH)
    pl.semaphore_wait(barrier, len(peers))


def _zeros(ref):
    return jnp.zeros(ref.shape, ref.dtype)


def _pcall(comm, body, *, name, grid, in_specs, out_specs, out_shape, compiler_params, scratch_shapes=()):
    if comm is None:
        return pl.pallas_call(body, name=name, grid=grid, in_specs=in_specs, out_specs=out_specs, out_shape=out_shape,
                              scratch_shapes=list(scratch_shapes), compiler_params=compiler_params)
    multi = isinstance(out_shape, (list, tuple))
    out_shapes = list(out_shape) if multi else [out_shape]
    out_specs_l = list(out_specs) if multi else [out_specs]
    n_in, n_out, n_scr = len(in_specs), len(out_shapes), len(scratch_shapes)
    n_cin, n_cout = len(comm.arrays), len(comm.out_shapes)

    def with_comm(*refs):
        ins, refs = refs[:n_in], refs[n_in:]
        cins, refs = refs[:n_cin], refs[n_cin:]
        outs, refs = refs[:n_out], refs[n_out:]
        couts, refs = refs[:n_cout], refs[n_cout:]
        scr, sems = refs[:n_scr], refs[n_scr:]
        first = functools.reduce(jnp.logical_and, [pl.program_id(a) == 0 for a in range(len(grid))])
        last = functools.reduce(jnp.logical_and, [pl.program_id(a) == grid[a] - 1 for a in range(len(grid))])
        @pl.when(first)
        def _():
            _handshake(comm.partners)
            comm.start(cins, couts, sems)

        body(*ins, *outs, *scr)
        pl.when(last)(lambda: comm.finish(cins, couts, sems))

    call = pl.pallas_call(
        with_comm, name=name, grid=grid, in_specs=list(in_specs) + [ANY] * n_cin,
        out_specs=out_specs_l + [ANY] * n_cout, out_shape=out_shapes + comm.out_shapes,
        scratch_shapes=list(scratch_shapes) + comm.sem_shapes(),
        compiler_params=_params(*(("arbitrary",) * len(grid)), collective_id=_COLLECTIVE_ID[comm.partners]),
        input_output_aliases={n_in + i: n_out + j for i, j in comm.aliases.items()})

    def run(*args):
        res = call(*args, *comm.arrays)
        main = res[:n_out]
        return (list(main) if multi else main[0]), list(res[n_out:])

    return run


def _dot(a, b):
    return jnp.dot(a, b, preferred_element_type=F32)


def _dot_nt(a, b):
    return lax.dot_general(a, b, _NT, preferred_element_type=F32)


def _dot_tn(a, b):
    return lax.dot_general(a, b, _TN, preferred_element_type=F32)


def mm_nn_g(a, bg, out_dtype, name, tm=512, comm=None):
    M, K = a.shape
    G, _, n = bg.shape
    tm = min(tm, M)

    def body(a_ref, b_ref, o_ref):
        o_ref[...] = _dot(a_ref[...], b_ref[...]).astype(o_ref.dtype)

    return _pcall(
        comm, body, name=name, grid=(G, M // tm),
        in_specs=[pl.BlockSpec((tm, K), lambda g, m: (m, 0)),
                  pl.BlockSpec((None, K, n), lambda g, m: (g, 0, 0))],
        out_specs=pl.BlockSpec((tm, n), lambda g, m: (m, g)),
        out_shape=jax.ShapeDtypeStruct((M, G * n), out_dtype),
        compiler_params=_params("parallel", "parallel"),
    )(a, bg)


def mm_swiglu_g(a, bg, name, tm=512, comm=None):
    M, K = a.shape
    G, _, n = bg.shape
    half = G // 2
    tm = min(tm, M)

    def body(a_ref, bgate_ref, bup_ref, gu_ref, h_ref):
        av = a_ref[...]
        gate = _dot(av, bgate_ref[...])
        up = _dot(av, bup_ref[...])
        gu_ref[0] = gate.astype(BF16)
        gu_ref[1] = up.astype(BF16)
        h_ref[...] = (gate * _sigmoid(gate) * up).astype(BF16)

    return _pcall(
        comm, body, name=name, grid=(half, M // tm),
        in_specs=[pl.BlockSpec((tm, K), lambda p, m: (m, 0)),
                  pl.BlockSpec((None, K, n), lambda p, m: (p, 0, 0)),
                  pl.BlockSpec((None, K, n), lambda p, m: (p + half, 0, 0))],
        out_specs=[pl.BlockSpec((2, tm, n), lambda p, m: (0, m, p)), pl.BlockSpec((tm, n), lambda p, m: (m, p))],
        out_shape=[jax.ShapeDtypeStruct((2, M, half * n), BF16), jax.ShapeDtypeStruct((M, half * n), BF16)],
        compiler_params=_params("parallel", "parallel"),
    )(a, bg, bg)


def mm_swiglu_bwd(dy, w_out, gate_up, name, tk=512, comm=None):
    M, N = dy.shape
    F = w_out.shape[0]

    def body(dy_ref, w_ref, gu_ref, o_ref):
        dh = _dot_nt(dy_ref[...], w_ref[...])
        gt = gu_ref[0].astype(F32)
        up = gu_ref[1].astype(F32)
        sg = _sigmoid(gt)
        o_ref[0] = (dh * up * (sg * (1.0 + gt * (1.0 - sg)))).astype(BF16)
        o_ref[1] = (dh * (gt * sg)).astype(BF16)

    planes = pl.BlockSpec((2, M, tk), lambda k: (0, 0, k))
    return _pcall(
        comm, body, name=name, grid=(F // tk,),
        in_specs=[pl.BlockSpec((M, N), lambda k: (0, 0)), pl.BlockSpec((tk, N), lambda k: (k, 0)), planes],
        out_specs=planes,
        out_shape=jax.ShapeDtypeStruct((2, M, F), BF16),
        compiler_params=_params("parallel"),
    )(dy, w_out, gate_up)


def _shard_cols_spec(dy, G, rows, index):
    if dy.ndim == 2:
        n = dy.shape[1] // G
        return pl.BlockSpec((rows, n), lambda *ids: index(*ids)), n
    half = G // 2
    n = dy.shape[2] // half

    def planes(*ids):
        r, g = index(*ids)
        return g // half, r, g % half

    return pl.BlockSpec((None, rows, n), planes), n


def mm_nn(a, b, out_dtype, name, tm=512, tn=1024, scale=1.0, res=None, comm=None):
    M, K = a.shape
    _, N = b.shape
    tm, tn = min(tm, M), min(tn, N)

    def body(*refs):
        if res is None:
            a_ref, b_ref, o_ref = refs
            acc = _dot(a_ref[...], b_ref[...])
            o_ref[...] = (acc * scale if scale != 1.0 else acc).astype(o_ref.dtype)
        else:
            a_ref, b_ref, r_ref, o_ref = refs
            o_ref[...] = (r_ref[...] + scale * _dot(a_ref[...], b_ref[...])).astype(o_ref.dtype)

    in_specs = [pl.BlockSpec((tm, K), lambda n, m: (m, 0)),
                pl.BlockSpec((K, tn), lambda n, m: (0, n))]
    args = [a, b]
    if res is not None:
        in_specs.append(pl.BlockSpec((tm, tn), lambda n, m: (m, n)))
        args.append(res)
    return _pcall(
        comm, body, name=name, grid=(N // tn, M // tm),
        in_specs=in_specs,
        out_specs=pl.BlockSpec((tm, tn), lambda n, m: (m, n)),
        out_shape=jax.ShapeDtypeStruct((M, N), out_dtype),
        compiler_params=_params("parallel", "parallel"),
    )(*args)


def mm_nt_g(dy, bg, name, tm=512, comm=None):
    M = dy.shape[-2]
    G, K, n = bg.shape
    tm = min(tm, M)
    dy_spec, _ = _shard_cols_spec(dy, G, tm, lambda m, g: (m, g))

    def body(dy_ref, b_ref, o_ref):
        part = _dot_nt(dy_ref[...], b_ref[...])

        @pl.when(pl.program_id(1) == 0)
        def _():
            o_ref[...] = part

        @pl.when(pl.program_id(1) > 0)
        def _():
            o_ref[...] += part

    return _pcall(
        comm, body, name=name, grid=(M // tm, G),
        in_specs=[dy_spec, pl.BlockSpec((None, K, n), lambda m, g: (g, 0, 0))],
        out_specs=pl.BlockSpec((tm, K), lambda m, g: (m, 0)),
        out_shape=jax.ShapeDtypeStruct((M, K), F32),
        compiler_params=_params("parallel", "arbitrary"),
    )(dy, bg)


def mm_nt(dy, b, out_dtype, name, tk=512, comm=None):
    M, N = dy.shape
    K, _ = b.shape

    def body(dy_ref, b_ref, o_ref):
        o_ref[...] = _dot_nt(dy_ref[...], b_ref[...]).astype(o_ref.dtype)

    return _pcall(
        comm, body, name=name, grid=(K // tk,),
        in_specs=[pl.BlockSpec((M, N), lambda k: (0, 0)),
                  pl.BlockSpec((tk, N), lambda k: (k, 0))],
        out_specs=pl.BlockSpec((M, tk), lambda k: (0, k)),
        out_shape=jax.ShapeDtypeStruct((M, K), out_dtype),
        compiler_params=_params("parallel"),
    )(dy, b)


def mm_tn_g(x, dy, G, name, tk=512, cols=None, comm=None):
    M, K = x.shape
    k0, K = cols if cols is not None else (0, K)
    dy_spec, n = _shard_cols_spec(dy, G, M, lambda g, k: (0, g))

    def body(x_ref, dy_ref, o_ref):
        o_ref[...] = _dot_tn(x_ref[...], dy_ref[...]).astype(o_ref.dtype)

    return _pcall(
        comm, body, name=name, grid=(G, K // tk),
        in_specs=[pl.BlockSpec((M, tk), lambda g, k: (0, k + k0 // tk)), dy_spec],
        out_specs=pl.BlockSpec((None, tk, n), lambda g, k: (g, k, 0)),
        out_shape=jax.ShapeDtypeStruct((G, K, n), BF16),
        compiler_params=_params("parallel", "parallel"),
    )(x, dy)


def mm_tn(x, dy, name, tk=512, comm=None):
    M, K = x.shape
    _, N = dy.shape

    def body(x_ref, dy_ref, o_ref):
        o_ref[...] = _dot_tn(x_ref[...], dy_ref[...]).astype(o_ref.dtype)

    return _pcall(
        comm, body, name=name, grid=(K // tk,),
        in_specs=[pl.BlockSpec((M, tk), lambda k: (0, k)),
                  pl.BlockSpec((M, N), lambda k: (0, 0))],
        out_specs=pl.BlockSpec((tk, N), lambda k: (k, 0)),
        out_shape=jax.ShapeDtypeStruct((K, N), BF16),
        compiler_params=_params("parallel"),
    )(x, dy)


def _rstd(x):
    return lax.rsqrt(jnp.mean(x * x, axis=-1, keepdims=True) + EPS)


def _rms_bwd(dn, xhat, r, g):
    dxhat = dn * g
    return r * (dxhat - xhat * jnp.mean(dxhat * xhat, axis=-1, keepdims=True))


def _row_spec(tr, width, col=0):
    return pl.BlockSpec((tr, width), lambda i: (i, col))


def _vec_spec(width):
    return pl.BlockSpec((1, width), lambda i: (0, 0))


def _heads_spec(tr):
    return pl.BlockSpec((H_B, tr, DH_B), lambda i: (0, i, 0))


def _heads_to_cols(ref):
    return jnp.concatenate([ref[h] for h in range(H_B)], axis=1)


def split_heads(z, name, comm=None):
    S = z.shape[0]
    tr = min(ROW_TILE, S)
    width = H_B * DH_B
    first = 2 * W_A // width

    def body(q_ref, k_ref, v_ref, o_ref):
        for p, (ref, scale) in enumerate(((q_ref, DH_B ** -0.5), (k_ref, 1.0), (v_ref, 1.0))):
            for h in range(H_B):
                cols = ref[:, h * DH_B:(h + 1) * DH_B]
                o_ref[p * H_B + h] = (cols * scale if scale != 1.0 else cols).astype(BF16)

    return _pcall(
        comm, body, name=name, grid=(S // tr,),
        in_specs=[_row_spec(tr, width, first), _row_spec(tr, width, first + 1), _row_spec(tr, width, first + 2)],
        out_specs=pl.BlockSpec((3 * H_B, tr, DH_B), lambda i: (0, i, 0)),
        out_shape=jax.ShapeDtypeStruct((3 * H_B, S, DH_B), BF16),
        compiler_params=_params("parallel"),
    )(z, z, z)


def join_dz(dza, dq, dk, dv, name, comm=None):
    S, wa = dza.shape
    tr = min(ROW_TILE, S)
    width = H_B * DH_B

    def body(dza_ref, dq_ref, dk_ref, dv_ref, o_ref):
        o_ref[:, :wa] = dza_ref[...]
        for p, ref in enumerate((dq_ref, dk_ref, dv_ref)):
            for h in range(H_B):
                lo = wa + p * width + h * DH_B
                o_ref[:, lo:lo + DH_B] = ref[h]

    return _pcall(
        comm, body, name=name, grid=(S // tr,),
        in_specs=[_row_spec(tr, wa), _heads_spec(tr), _heads_spec(tr), _heads_spec(tr)],
        out_specs=_row_spec(tr, wa + 3 * width),
        out_shape=jax.ShapeDtypeStruct((S, wa + 3 * width), BF16),
        compiler_params=_params("parallel"),
    )(dza, dq, dk, dv)


def rms_fwd(x, g, name, comm=None):
    M, D = x.shape
    tr = min(ROW_TILE, M)

    def body(x_ref, g_ref, o_ref):
        xv = x_ref[...]
        o_ref[...] = (xv * _rstd(xv) * g_ref[...]).astype(o_ref.dtype)

    return _pcall(
        comm, body, name=name, grid=(M // tr,),
        in_specs=[_row_spec(tr, D), _vec_spec(D)],
        out_specs=_row_spec(tr, D),
        out_shape=jax.ShapeDtypeStruct((M, D), BF16),
        compiler_params=_params("parallel"),
    )(x, g)


def rms_bwd(dn, h, g, dres, copy_scale, name, comm=None):
    M, D = h.shape
    tr = min(ROW_TILE, M)
    has_res = dres is not None

    def body(*refs):
        if has_res:
            dn_ref, h_ref, g_ref, dres_ref, dh_ref, dhb_ref, dg_ref = refs
        else:
            dn_ref, h_ref, g_ref, dh_ref, dhb_ref, dg_ref = refs
        hv = h_ref[...]
        r = _rstd(hv)
        xhat = hv * r
        dn = dn_ref[...]
        part = jnp.sum(dn * xhat, axis=0, keepdims=True)

        @pl.when(pl.program_id(0) == 0)
        def _():
            dg_ref[...] = part

        @pl.when(pl.program_id(0) > 0)
        def _():
            dg_ref[...] += part

        dh = _rms_bwd(dn, xhat, r, g_ref[...])
        if has_res:
            dh = dh + dres_ref[...]
        dh_ref[...] = dh
        dhb_ref[...] = (dh * copy_scale if copy_scale != 1.0 else dh).astype(BF16)

    in_specs = [_row_spec(tr, D), _row_spec(tr, D), _vec_spec(D)]
    args = [dn, h, g]
    if has_res:
        in_specs.append(_row_spec(tr, D))
        args.append(dres)
    return _pcall(
        comm, body, name=name, grid=(M // tr,),
        in_specs=in_specs,
        out_specs=[_row_spec(tr, D), _row_spec(tr, D), _vec_spec(D)],
        out_shape=[jax.ShapeDtypeStruct((M, D), F32), jax.ShapeDtypeStruct((M, D), BF16),
                   jax.ShapeDtypeStruct((1, D), F32)],
        compiler_params=_params("arbitrary"),
    )(*args)


def _sigmoid(x):
    return 1.0 / (1.0 + jnp.exp(-x))


def rmscat_fwd(ya, yb, ga, gb, name, comm=None):
    M, W = ya.shape
    tr = min(ROW_TILE, M)

    def body(ya_ref, yb_ref, ga_ref, gb_ref, o_ref):
        a = ya_ref[...]
        b = _heads_to_cols(yb_ref)
        o_ref[:, :W] = (a * _rstd(a) * ga_ref[...]).astype(BF16)
        o_ref[:, W:] = (b * _rstd(b) * gb_ref[...]).astype(BF16)

    return _pcall(
        comm, body, name=name, grid=(M // tr,),
        in_specs=[_row_spec(tr, W), _heads_spec(tr), _vec_spec(W), _vec_spec(W)],
        out_specs=_row_spec(tr, 2 * W),
        out_shape=jax.ShapeDtypeStruct((M, 2 * W), BF16),
        compiler_params=_params("parallel"),
    )(ya, yb, ga, gb)


def rmscat_bwd(dycat, ya, yb, ga, gb, name, comm=None):
    M, W = ya.shape
    tr = min(ROW_TILE, M)

    def body(dc_ref, ya_ref, yb_ref, ga_ref, gb_ref, dya_ref, dyb_ref, dga_ref, dgb_ref):
        first = pl.program_id(0) == 0
        for by_head, y_ref, g_ref, dy_ref, dg_ref, lo in ((False, ya_ref, ga_ref, dya_ref, dga_ref, 0),
                                                          (True, yb_ref, gb_ref, dyb_ref, dgb_ref, W)):
            yv = _heads_to_cols(y_ref) if by_head else y_ref[...]
            r = _rstd(yv)
            xhat = yv * r
            dn = dc_ref[:, lo:lo + W]
            part = jnp.sum(dn * xhat, axis=0, keepdims=True)

            @pl.when(first)
            def _():
                dg_ref[...] = part

            @pl.when(jnp.logical_not(first))
            def _():
                dg_ref[...] += part

            dy = _rms_bwd(dn, xhat, r, g_ref[...])
            if by_head:
                for h in range(H_B):
                    dy_ref[h] = dy[:, h * DH_B:(h + 1) * DH_B]
            else:
                dy_ref[...] = dy

    return _pcall(
        comm, body, name=name, grid=(M // tr,),
        in_specs=[_row_spec(tr, 2 * W), _row_spec(tr, W), _heads_spec(tr), _vec_spec(W), _vec_spec(W)],
        out_specs=[_row_spec(tr, W), _heads_spec(tr), _vec_spec(W), _vec_spec(W)],
        out_shape=[jax.ShapeDtypeStruct((M, W), F32), jax.ShapeDtypeStruct((H_B, M, DH_B), F32),
                   jax.ShapeDtypeStruct((1, W), F32), jax.ShapeDtypeStruct((1, W), F32)],
        compiler_params=_params("arbitrary"),
    )(dycat, ya, yb, ga, gb)


def loss_head(h, target, g, name, comm=None):
    M, D = h.shape
    tr = min(ROW_TILE, M)

    def body(h_ref, t_ref, g_ref, loss_ref, dh_ref, dhb_ref, dg_ref):
        hv = h_ref[...]
        gv = g_ref[...]
        r = _rstd(hv)
        xhat = hv * r
        err = xhat * gv - t_ref[...]
        lsum = jnp.sum(jnp.sum(err * err, axis=1, keepdims=True), axis=0, keepdims=True) * (0.5 / D)
        dy = err * (1.0 / D)
        part = jnp.sum(dy * xhat, axis=0, keepdims=True)

        @pl.when(pl.program_id(0) == 0)
        def _():
            dg_ref[...] = part
            loss_ref[...] = _zeros(loss_ref) + lsum

        @pl.when(pl.program_id(0) > 0)
        def _():
            dg_ref[...] += part
            loss_ref[...] += lsum

        dh = _rms_bwd(dy, xhat, r, gv)
        dh_ref[...] = dh
        dhb_ref[...] = (0.5 * dh).astype(BF16)

    return _pcall(
        comm, body, name=name, grid=(M // tr,),
        in_specs=[_row_spec(tr, D), _row_spec(tr, D), _vec_spec(D)],
        out_specs=[pl.BlockSpec((8, 128), lambda i: (0, 0)), _row_spec(tr, D), _row_spec(tr, D), _vec_spec(D)],
        out_shape=[jax.ShapeDtypeStruct((8, 128), F32), jax.ShapeDtypeStruct((M, D), F32),
                   jax.ShapeDtypeStruct((M, D), BF16), jax.ShapeDtypeStruct((1, D), F32)],
        compiler_params=_params("arbitrary"),
    )(h, target, g)


_GELU_C = math.sqrt(2.0 / math.pi)


def _gelu(x):
    return 0.5 * x * (1.0 + jnp.tanh(_GELU_C * (x + 0.044715 * (x * x * x))))


def _gelu_grad(x):
    t = jnp.tanh(_GELU_C * (x + 0.044715 * (x * x * x)))
    return 0.5 * (1.0 + t) + 0.5 * x * (1.0 - t * t) * (_GELU_C * (1.0 + 3.0 * 0.044715 * (x * x)))


def _sgu_mask():
    t = lax.broadcasted_iota(jnp.int32, (SGU_BLOCK, SGU_BLOCK), 0) // CHUNK
    s = lax.broadcasted_iota(jnp.int32, (SGU_BLOCK, SGU_BLOCK), 1) // CHUNK
    return s <= t


def _layernorm_stats(v):
    mu = jnp.mean(v, axis=-1, keepdims=True)
    cen = v - mu
    rstd = lax.rsqrt(jnp.mean(cen * cen, axis=-1, keepdims=True) + EPS)
    return cen * rstd, rstd


def sgu_fwd(z, ln_g, ln_b, w_s, b_t, name, comm=None):
    S = z.shape[0]

    def body(zu_ref, zv_ref, lg_ref, lb_ref, w_ref, bt_ref, o_ref):
        mask = _sgu_mask()
        for g in range(G_A):
            cols = slice(g * GA_DIM, (g + 1) * GA_DIM)
            u = _gelu(zu_ref[:, cols])
            vhat, _ = _layernorm_stats(_gelu(zv_ref[:, cols]))
            vln = vhat * lg_ref[:, cols] + lb_ref[:, cols]
            w = jnp.where(mask, w_ref[g], 0.0).astype(BF16)
            mixed = _dot(w, vln.astype(BF16)) + bt_ref[:, g:g + 1]
            o_ref[:, cols] = u * mixed

    return _pcall(
        comm, body, name=name, grid=(S // SGU_BLOCK,),
        in_specs=[_row_spec(SGU_BLOCK, W_A, 0), _row_spec(SGU_BLOCK, W_A, 1), _vec_spec(W_A), _vec_spec(W_A),
                  pl.BlockSpec((G_A, SGU_BLOCK, SGU_BLOCK), lambda i: (0, 0, 0)),
                  pl.BlockSpec((SGU_BLOCK, G_A), lambda i: (0, 0))],
        out_specs=_row_spec(SGU_BLOCK, W_A),
        out_shape=jax.ShapeDtypeStruct((S, W_A), F32),
        compiler_params=_params("parallel"),
    )(z, z, ln_g, ln_b, w_s, b_t)


def sgu_bwd(z, dya, ln_g, ln_b, w_s, b_t, name, comm=None):
    S = z.shape[0]
    nblk = S // SGU_BLOCK

    def body(zu_ref, zv_ref, dy_ref, lg_ref, lb_ref, w_ref, bt_ref,
             dz_ref, dlg_ref, dlb_ref, dw_ref, db_ref, dmix_acc):
        step = pl.program_id(0)
        mask = _sgu_mask()

        @pl.when(step == 0)
        def _():
            dlg_ref[...] = _zeros(dlg_ref)
            dlb_ref[...] = _zeros(dlb_ref)
            dw_ref[...] = _zeros(dw_ref)
            dmix_acc[...] = _zeros(dmix_acc)

        for g in range(G_A):
            cols = slice(g * GA_DIM, (g + 1) * GA_DIM)
            zu = zu_ref[:, cols]
            zv = zv_ref[:, cols]
            u = _gelu(zu)
            vhat, rstd = _layernorm_stats(_gelu(zv))
            lg = lg_ref[:, cols]
            vln = (vhat * lg + lb_ref[:, cols]).astype(BF16)
            w = jnp.where(mask, w_ref[g], 0.0)
            mixed = _dot(w.astype(BF16), vln) + bt_ref[:, g:g + 1]
            dy = dy_ref[:, cols]
            du = dy * mixed
            dmixed = dy * u
            dmixed_b = dmixed.astype(BF16)
            dmix_acc[g] += dmixed
            dw_ref[g] += jnp.where(mask, _dot_nt(dmixed_b, vln), 0.0)
            dvln = _dot(w.T.astype(BF16), dmixed_b)
            dlb_ref[:, cols] += jnp.sum(dvln, axis=0, keepdims=True)
            dlg_ref[:, cols] += jnp.sum(dvln * vhat, axis=0, keepdims=True)
            dvhat = dvln * lg
            dv = rstd * (dvhat - jnp.mean(dvhat, axis=-1, keepdims=True)
                         - vhat * jnp.mean(dvhat * vhat, axis=-1, keepdims=True))
            dz_ref[:, cols] = (du * _gelu_grad(zu)).astype(BF16)
            dz_ref[:, W_A + g * GA_DIM:W_A + (g + 1) * GA_DIM] = (dv * _gelu_grad(zv)).astype(BF16)

        @pl.when(step == nblk - 1)
        def _():
            for g in range(G_A):
                db_ref[g] = jnp.sum(dmix_acc[g], axis=1, keepdims=True)

    whole3 = lambda shape: pl.BlockSpec(shape, lambda i: (0, 0, 0))
    return _pcall(
        comm, body, name=name, grid=(nblk,),
        in_specs=[_row_spec(SGU_BLOCK, W_A, 0), _row_spec(SGU_BLOCK, W_A, 1), _row_spec(SGU_BLOCK, W_A),
                  _vec_spec(W_A), _vec_spec(W_A), whole3((G_A, SGU_BLOCK, SGU_BLOCK)),
                  pl.BlockSpec((SGU_BLOCK, G_A), lambda i: (0, 0))],
        out_specs=[_row_spec(SGU_BLOCK, 2 * W_A), _vec_spec(W_A), _vec_spec(W_A),
                   whole3((G_A, SGU_BLOCK, SGU_BLOCK)), whole3((G_A, SGU_BLOCK, 1))],
        out_shape=[jax.ShapeDtypeStruct((S, 2 * W_A), BF16), jax.ShapeDtypeStruct((1, W_A), F32),
                   jax.ShapeDtypeStruct((1, W_A), F32), jax.ShapeDtypeStruct((G_A, SGU_BLOCK, SGU_BLOCK), F32),
                   jax.ShapeDtypeStruct((G_A, SGU_BLOCK, 1), F32)],
        scratch_shapes=[pltpu.VMEM((G_A, SGU_BLOCK, SGU_BLOCK), F32)],
        compiler_params=_params("arbitrary"),
    )(z, z, dya, ln_g, ln_b, w_s, b_t)


def _log_sigmoid(z):
    return jnp.minimum(z, 0.0) - jnp.log(1.0 + jnp.exp(-jnp.abs(z)))


def _suffix_sum(x, upper):
    hi = x.astype(BF16)
    lo = (x - hi.astype(F32)).astype(BF16)
    return _dot(hi, upper) + _dot(lo, upper)


SB_ROWS = 1024
_SB_SUB = SB_ROWS // Q_BLOCK


def _sb_upper():
    row = lax.broadcasted_iota(jnp.int32, (Q_BLOCK, Q_BLOCK), 0)
    col = lax.broadcasted_iota(jnp.int32, (Q_BLOCK, Q_BLOCK), 1)
    return (row > col).astype(BF16)


def _sb_sweep(step, tile):
    for r in reversed(range(_SB_SUB)):
        tile(step * _SB_SUB + r, r * Q_BLOCK)

    def group(g, _):
        base = (step - 1 - g) * _SB_SUB
        for r in reversed(range(_SB_SUB)):
            tile(base + r, None)
        return 0

    lax.fori_loop(0, step, group, 0)


def _sb_causal(n):
    return lax.broadcasted_iota(jnp.int32, (n, Q_BLOCK), 1) < lax.broadcasted_iota(jnp.int32, (n, Q_BLOCK), 0)


def _sb_rows_spec():
    return pl.BlockSpec((None, SB_ROWS, DH_B), lambda h, i: (h, i, 0))


def _sb_head_spec(S, part=0):
    return pl.BlockSpec((None, S, DH_B), lambda h, i: (part * H_B + h, 0, 0))


def sb_fwd(qkv, name, comm=None):
    S = qkv.shape[1]

    def body(q_b, k_ref, v_ref, o_ref, c_l1m):
        step = pl.program_id(1)
        o_ref[...] = _zeros(o_ref)
        c_l1m[...] = _zeros(c_l1m)
        upper = _sb_upper()

        def tile(j, row0):
            rq = slice(row0 or 0, SB_ROWS)
            causal = None if row0 is None else _sb_causal(SB_ROWS - row0)
            rows = pl.ds(pl.multiple_of(j * Q_BLOCK, Q_BLOCK), Q_BLOCK)
            zz = _dot_nt(q_b[rq, :], k_ref[rows, :])
            lb = _log_sigmoid(zz)
            l1m = lb - zz
            if causal is not None:
                l1m = jnp.where(causal, l1m, 0.0)
            a = jnp.exp(lb + _suffix_sum(l1m, upper) + c_l1m[rq, :])
            if causal is not None:
                a = jnp.where(causal, a, 0.0)
            o_ref[rq, :] += _dot(a.astype(BF16), v_ref[rows, :])
            c_l1m[rq, :] += jnp.sum(l1m, axis=1, keepdims=True)

        _sb_sweep(step, tile)

    return _pcall(
        comm, body, name=name, grid=(H_B, S // SB_ROWS),
        in_specs=[_sb_rows_spec(), _sb_head_spec(S, 1), _sb_head_spec(S, 2)],
        out_specs=_sb_rows_spec(),
        out_shape=jax.ShapeDtypeStruct((H_B, S, DH_B), F32),
        scratch_shapes=[pltpu.VMEM((SB_ROWS, 1), F32)],
        compiler_params=_params("parallel", "parallel"),
    )(qkv, qkv, qkv)


def sb_bwd(qkv, out, dout, name, comm=None):
    S = qkv.shape[1]
    nstep = S // SB_ROWS
    scale = DH_B ** -0.5

    def body(q_b, k_ref, v_ref, o_ref, do_ref, dq_ref, dk_ref, dv_ref,
             dq_acc, dkt_acc, dvt_acc, do_b, qt_b, dot_b, g_left, c_l1m):
        step = pl.program_id(1)

        @pl.when(step == 0)
        def _():
            dkt_acc[...] = _zeros(dkt_acc)
            dvt_acc[...] = _zeros(dvt_acc)

        do_b[...] = do_ref[...].astype(BF16)
        qt_b[...] = q_b[...].astype(F32).T.astype(BF16)
        dot_b[...] = do_ref[...].T.astype(BF16)
        g_left[...] = jnp.sum(do_b[...].astype(F32) * o_ref[...], axis=1, keepdims=True)
        dq_acc[...] = _zeros(dq_acc)
        c_l1m[...] = _zeros(c_l1m)
        upper = _sb_upper()

        def tile(j, row0):
            rq = slice(row0 or 0, SB_ROWS)
            causal = None if row0 is None else _sb_causal(SB_ROWS - row0)
            rows = pl.ds(pl.multiple_of(j * Q_BLOCK, Q_BLOCK), Q_BLOCK)
            q, do_t = q_b[rq, :], do_b[rq, :]
            k_j = k_ref[rows, :]
            zz = _dot_nt(q, k_j)
            lb = _log_sigmoid(zz)
            l1m = lb - zz
            if causal is not None:
                l1m = jnp.where(causal, l1m, 0.0)
            a = jnp.exp(lb + _suffix_sum(l1m, upper) + c_l1m[rq, :])
            if causal is not None:
                a = jnp.where(causal, a, 0.0)
            a_b = a.astype(BF16)
            dvt_acc[:, rows] += _dot(dot_b[:, rq], a_b)
            gmat = a_b.astype(F32) * _dot_nt(do_t, v_ref[rows, :])
            before = g_left[rq, :] - gmat - _suffix_sum(gmat, upper)
            sig = jnp.exp(lb)
            dz = gmat * (1.0 - sig) - sig * before
            if causal is not None:
                dz = jnp.where(causal, dz, 0.0)
            dz_b = dz.astype(BF16)
            dkt_acc[:, rows] += _dot(qt_b[:, rq], dz_b)
            dq_acc[rq, :] += _dot(dz_b, k_j)
            c_l1m[rq, :] += jnp.sum(l1m, axis=1, keepdims=True)
            g_left[rq, :] -= jnp.sum(gmat, axis=1, keepdims=True)

        _sb_sweep(step, tile)
        dq_ref[...] = (dq_acc[...] * scale).astype(BF16)

        @pl.when(step == nstep - 1)
        def _():
            dk_ref[...] = dkt_acc[...].T.astype(BF16)
            dv_ref[...] = dvt_acc[...].T.astype(BF16)

    out_sds = jax.ShapeDtypeStruct((H_B, S, DH_B), BF16)
    return _pcall(
        comm, body, name=name, grid=(H_B, nstep),
        in_specs=[_sb_rows_spec(), _sb_head_spec(S, 1), _sb_head_spec(S, 2), _sb_rows_spec(), _sb_rows_spec()],
        out_specs=[_sb_rows_spec(), _sb_head_spec(S), _sb_head_spec(S)],
        out_shape=[out_sds, out_sds, out_sds],
        scratch_shapes=[pltpu.VMEM((SB_ROWS, DH_B), F32)] + [pltpu.VMEM((DH_B, S), F32)] * 2
        + [pltpu.VMEM((SB_ROWS, DH_B), BF16)] + [pltpu.VMEM((DH_B, SB_ROWS), BF16)] * 2
        + [pltpu.VMEM((SB_ROWS, 1), F32)] * 2,
        compiler_params=_params("parallel", "arbitrary"),
    )(qkv, qkv, qkv, out, dout)


def _softmax(s):
    e = jnp.exp(s - jnp.max(s, axis=-1, keepdims=True))
    return e / jnp.sum(e, axis=-1, keepdims=True)


def xattn_fwd(qc, kv, name, comm=None):
    S, D = qc.shape
    tr = min(ROW_TILE, S)

    def body(q_ref, kv_ref, o_ref):
        for h in range(X_HEADS):
            cols = slice(h * X_DH, (h + 1) * X_DH)
            p = _softmax(_dot_nt(q_ref[:, cols], kv_ref[:, cols]))
            o_ref[:, cols] = _dot(p.astype(BF16), kv_ref[:, D + h * X_DH:D + (h + 1) * X_DH]).astype(BF16)

    return _pcall(
        comm, body, name=name, grid=(S // tr,),
        in_specs=[_row_spec(tr, D), pl.BlockSpec((N_MEM, 2 * D), lambda i: (0, 0))],
        out_specs=_row_spec(tr, D),
        out_shape=jax.ShapeDtypeStruct((S, D), BF16),
        compiler_params=_params("parallel"),
    )(qc, kv)


def xattn_bwd(qc, kv, do, name, comm=None):
    S, D = qc.shape
    tr = min(ROW_TILE, S)
    nstep = S // tr
    scale = X_DH ** -0.5

    def body(q_ref, kv_ref, do_ref, dq_ref, dkv_ref, acc):
        step = pl.program_id(0)

        @pl.when(step == 0)
        def _():
            acc[...] = _zeros(acc)

        for h in range(X_HEADS):
            cols = slice(h * X_DH, (h + 1) * X_DH)
            vcols = slice(D + h * X_DH, D + (h + 1) * X_DH)
            q = q_ref[:, cols]
            k = kv_ref[:, cols]
            do_h = do_ref[:, cols]
            p = _softmax(_dot_nt(q, k))
            dp = _dot_nt(do_h, kv_ref[:, vcols])
            acc[:, vcols] += _dot_tn(p.astype(BF16), do_h)
            ds = (p * (dp - jnp.sum(p * dp, axis=-1, keepdims=True))).astype(BF16)
            dq_ref[:, cols] = (_dot(ds, k) * scale).astype(BF16)
            acc[:, cols] += _dot_tn(ds, q)

        @pl.when(step == nstep - 1)
        def _():
            dkv_ref[...] = acc[...].astype(BF16)

    whole = pl.BlockSpec((N_MEM, 2 * D), lambda i: (0, 0))
    return _pcall(
        comm, body, name=name, grid=(nstep,),
        in_specs=[_row_spec(tr, D), whole, _row_spec(tr, D)],
        out_specs=[_row_spec(tr, D), whole],
        out_shape=[jax.ShapeDtypeStruct((S, D), BF16), jax.ShapeDtypeStruct((N_MEM, 2 * D), BF16)],
        scratch_shapes=[pltpu.VMEM((N_MEM, 2 * D), F32)],
        compiler_params=_params("arbitrary"),
    )(qc, kv, do)


def _row_tile(rows, cap=128):
    return max(t for t in range(16, cap + 1, 16) if rows % t == 0)


def cast_bf16(ws, name, steps=4, comm=None):
    n = len(ws)

    def body(*refs):
        for i in range(n):
            refs[n + i][...] = refs[i][...].astype(BF16)

    specs = [_row_spec(w.shape[0] // steps, w.shape[1]) for w in ws]
    return _pcall(
        comm, body, name=name, grid=(steps,), in_specs=specs, out_specs=specs,
        out_shape=[jax.ShapeDtypeStruct(w.shape, BF16) for w in ws],
        compiler_params=_params("parallel"),
    )(*ws)


def adamw(parts, w, m, v, name, comm=None):
    R, C = w.shape
    n_parts = parts.shape[0]
    tr = _row_tile(R, 256)
    c1 = 1.0 - ADAM_B1 ** ADAM_STEP
    c2 = 1.0 - ADAM_B2 ** ADAM_STEP

    def body(p_ref, w_ref, m_ref, v_ref, g_ref, d_ref, mo_ref, vo_ref):
        g = p_ref[0].astype(F32)
        for p in range(1, n_parts):
            g = g + p_ref[p].astype(F32)
        m_new = ADAM_B1 * m_ref[...] + (1.0 - ADAM_B1) * g
        v_new = ADAM_B2 * v_ref[...] + (1.0 - ADAM_B2) * (g * g)
        g_ref[...] = g
        mo_ref[...] = m_new
        vo_ref[...] = v_new
        d_ref[...] = -ADAM_LR * ((m_new / c1) / (jnp.sqrt(v_new / c2) + ADAM_EPS) + ADAM_WD * w_ref[...])

    spec = _row_spec(tr, C)
    sds = jax.ShapeDtypeStruct((R, C), F32)
    return _pcall(
        comm, body, name=name, grid=(R // tr,),
        in_specs=[pl.BlockSpec((n_parts, tr, C), lambda i: (0, i, 0)), spec, spec, spec],
        out_specs=[spec, spec, spec, spec],
        out_shape=[sds, sds, sds, sds],
        compiler_params=_params("parallel"),
    )(parts, w, m, v)


def pair_sum(parts, from_sibling, core, name):
    _, R, C = parts.shape
    tr = _row_tile(R, 1024)

    def body(core_ref, p_ref, s_ref, o_ref):
        o_ref[...] = (p_ref[...].astype(F32) + s_ref[...].astype(F32)).astype(o_ref.dtype)

    return pl.pallas_call(
        body, name=name,
        grid_spec=pltpu.PrefetchScalarGridSpec(
            num_scalar_prefetch=1, grid=(4, R // tr),
            in_specs=[pl.BlockSpec((None, tr, C), lambda q, i, core_ref: (2 * q + core_ref[0], i, 0)),
                      pl.BlockSpec((None, tr, C), lambda q, i, core_ref: (q, i, 0))],
            out_specs=pl.BlockSpec((None, tr, C), lambda q, i, core_ref: (q, i, 0))),
        out_shape=jax.ShapeDtypeStruct((4, R, C), BF16),
        compiler_params=_params("parallel", "parallel"),
    )(core, parts, from_sibling)


def add2(a, b, name, comm=None):
    R, C = a.shape
    tr = _row_tile(R, 256)

    def body(a_ref, b_ref, o_ref):
        o_ref[...] = a_ref[...] + b_ref[...]

    spec = _row_spec(tr, C)
    return _pcall(
        comm, body, name=name, grid=(R // tr,), in_specs=[spec, spec], out_specs=spec,
        out_shape=jax.ShapeDtypeStruct((R, C), F32), compiler_params=_params("parallel"),
    )(a, b)


def _place():
    return lax.axis_index("x"), lax.axis_index("y"), lax.axis_index("c")


class Comm:
    def __init__(self, partners, arrays, out_shapes, n_remote, n_local, start, finish, aliases=None):
        self.partners = frozenset(partners)
        self.arrays, self.out_shapes = list(arrays), list(out_shapes)
        self.n_remote, self.n_local = n_remote, max(n_local, 1)
        self.start, self.finish = start, finish
        self.aliases = dict(aliases or {})
        self.sizes = [len(self.out_shapes)]

    def sem_shapes(self):
        return [pltpu.SemaphoreType.DMA((self.n_remote,)), pltpu.SemaphoreType.DMA((self.n_remote,)),
                pltpu.SemaphoreType.DMA((self.n_local,))]


class _Shifted:
    def __init__(self, ref, offset):
        self.ref, self.offset = ref, offset

    @property
    def at(self):
        return self

    def __getitem__(self, k):
        return self.ref.at[self.offset + k]


def merge_comms(comms):
    comms = [c for c in comms if c is not None]
    if not comms:
        return None

    def each(method):
        def run(ins, outs, sems):
            i = o = r = l = 0
            for c in comms:
                sub = (_Shifted(sems[0], r), _Shifted(sems[1], r), _Shifted(sems[2], l))
                getattr(c, method)(ins[i:i + len(c.arrays)], outs[o:o + len(c.out_shapes)], sub)
                i, o, r, l = i + len(c.arrays), o + len(c.out_shapes), r + c.n_remote, l + c.n_local
        return run

    aliases, i, o = {}, 0, 0
    for c in comms:
        aliases.update({i + a: o + b for a, b in c.aliases.items()})
        i, o = i + len(c.arrays), o + len(c.out_shapes)
    merged = Comm(frozenset().union(*[c.partners for c in comms]),
                  [a for c in comms for a in c.arrays], [s for c in comms for s in c.out_shapes],
                  sum(c.n_remote for c in comms), sum(c.n_local for c in comms), each("start"), each("finish"), aliases)
    merged.sizes = [len(c.out_shapes) for c in comms]
    return merged


def split_results(comm, results):
    out, i = [], 0
    for n in comm.sizes:
        out.append(list(results[i:i + n]))
        i += n
    return out


def run_comm(comm, name):
    n_in, n_out = len(comm.arrays), len(comm.out_shapes)

    def body(*refs):
        ins, outs, sems = refs[:n_in], refs[n_in:n_in + n_out], refs[n_in + n_out:]
        _handshake(comm.partners)
        comm.start(ins, outs, sems)
        comm.finish(ins, outs, sems)

    return pl.pallas_call(
        body, name=name, in_specs=[ANY] * n_in, out_specs=[ANY] * n_out, out_shape=comm.out_shapes,
        scratch_shapes=comm.sem_shapes(), input_output_aliases=comm.aliases,
        compiler_params=pltpu.CompilerParams(collective_id=_COLLECTIVE_ID[comm.partners]),
    )(*comm.arrays)


def _remote(src, dst, sems, k, to):
    return pltpu.make_async_remote_copy(src_ref=src, dst_ref=dst, send_sem=sems[0].at[k], recv_sem=sems[1].at[k],
                                        device_id=to, device_id_type=MESH)


_AG_COPIES = 13


def comm_all_gather(shards, rows=None, into=None):
    n = len(shards)
    row0, nrows = rows if rows is not None else (0, None)

    def parties():
        x, y, c = _place()
        return (x, y, c), (x, y, 1 - c), [(1 - x, y), (x, 1 - y), (1 - x, 1 - y)]

    def span(w, half=None):
        count = nrows if nrows is not None else shards[w].shape[0]
        if half is None:
            return pl.ds(row0, count)
        return pl.ds(row0 + half * (count // 2), count // 2)

    def slab(outs, w, dev, half=None):
        return outs[w].at[4 * dev[0] + 2 * dev[1] + dev[2], span(w, half)]

    def own(ins, outs, sems):
        me, sibling, chips = parties()
        local = [pltpu.make_async_copy(ins[w].at[span(w)], slab(outs, w, me), sems[2].at[w]) for w in range(n)]
        first = []
        for w in range(n):
            k = _AG_COPIES * w
            first.append(_remote(ins[w].at[span(w)], slab(outs, w, me), sems, k, sibling))
            first += [_remote(ins[w].at[span(w, h)], slab(outs, w, me, h), sems, k + 1 + 2 * j + h, (*chip, me[2]))
                      for h in range(2) for j, chip in enumerate(chips)]
        return local, first

    def start(ins, outs, sems):
        local, first = own(ins, outs, sems)
        for cp in local + first:
            cp.start()

    def finish(ins, outs, sems):
        me, sibling, chips = parties()
        local, first = own(ins, outs, sems)
        passed = []
        for w in range(n):
            k = _AG_COPIES * w
            for h in range(2):
                for j, chip in enumerate(chips):
                    got = slab(outs, w, (*chip, me[2]), h)
                    _remote(got, got, sems, k + 1 + 2 * j + h, me).wait_recv()
                    cp = _remote(got, got, sems, k + 7 + 2 * j + h, sibling)
                    cp.start()
                    passed.append(cp)
        for w in range(n):
            k = _AG_COPIES * w
            got = slab(outs, w, sibling)
            _remote(got, got, sems, k, me).wait_recv()
            for h in range(2):
                for j, chip in enumerate(chips):
                    got = slab(outs, w, (*chip, sibling[2]), h)
                    _remote(got, got, sems, k + 7 + 2 * j + h, me).wait_recv()
        for cp in first + passed:
            cp.wait_send()
        for cp in local:
            cp.wait()

    out_shapes = [jax.ShapeDtypeStruct((N_DEV,) + s.shape, s.dtype) for s in shards]
    arrays = list(shards) + (list(into) if into is not None else [])
    aliases = {n + w: w for w in range(n)} if into is not None else None
    return Comm(("sibling", "chips"), arrays, out_shapes, _AG_COPIES * n, n, start, finish, aliases)


def comm_pairs(items):
    slabbed = [a.ndim == 3 for a in items]
    first = [sum(4 if s else 1 for s in slabbed[:w]) for w in range(len(items))]

    def copies(ins, outs, sems):
        x, y, c = _place()
        sibling = (x, y, 1 - c)
        cps = []
        for w, s in enumerate(slabbed):
            if s:
                cps += [_remote(ins[w].at[2 * q + (1 - c)], outs[w].at[q], sems, first[w] + q, sibling) for q in range(4)]
            else:
                cps.append(_remote(ins[w], outs[w], sems, first[w], sibling))
        return cps

    def start(ins, outs, sems):
        for cp in copies(ins, outs, sems):
            cp.start()

    def finish(ins, outs, sems):
        for cp in copies(ins, outs, sems):
            cp.wait()

    out_shapes = [jax.ShapeDtypeStruct(((4,) + a.shape[1:]) if s else a.shape, a.dtype) for a, s in zip(items, slabbed)]
    return Comm(("sibling",), items, out_shapes, sum(4 if s else 1 for s in slabbed), 0, start, finish)


def comm_chips(items, rows=None, into=None, from_row=None, out_rows=None):
    n = len(items)
    slabbed = [a.ndim == 3 for a in items]

    def span(w, source=False):
        if rows is None:
            return pl.ds(0, items[w].shape[-2])
        return pl.ds(from_row if source and from_row is not None else rows[0], rows[1])

    def copies(ins, outs, sems):
        x, y, c = _place()
        mine = 2 * x + y
        local = [pltpu.make_async_copy(ins[w].at[mine, span(w, True)] if slabbed[w] else ins[w].at[span(w, True)],
                                       outs[w].at[mine, span(w)], sems[2].at[w]) for w in range(n)]
        remote = []
        for w in range(n):
            for j, (px, py) in enumerate([(1 - x, y), (x, 1 - y), (1 - x, 1 - y)]):
                src = ins[w].at[2 * px + py, span(w, True)] if slabbed[w] else ins[w].at[span(w, True)]
                remote.append(_remote(src, outs[w].at[mine, span(w)], sems, 3 * w + j, (px, py, c)))
        return local, remote

    def start(ins, outs, sems):
        local, remote = copies(ins, outs, sems)
        for cp in local + remote:
            cp.start()

    def finish(ins, outs, sems):
        local, remote = copies(ins, outs, sems)
        for cp in remote + local:
            cp.wait()

    def result(a):
        tall = a.shape[:-2] + (out_rows if out_rows is not None else a.shape[-2], a.shape[-1])
        return jax.ShapeDtypeStruct(tall if a.ndim == 3 else (4,) + tall, a.dtype)

    if into is None:
        return Comm(("chips",), items, [result(a) for a in items], 3 * n, n, start, finish)
    out_shapes = [jax.ShapeDtypeStruct(b.shape, b.dtype) for b in into]
    return Comm(("chips",), list(items) + list(into), out_shapes, 3 * n, n, start, finish, {n + w: w for w in range(n)})


_SMALL = ("ffn1_norm", "mix_norm", "ln_v_gain", "ln_v_bias", "spatial_w", "spatial_b", "gnorm_a", "gnorm_b",
          "cross_norm", "mem_norm", "ffn2_norm", "final_norm")
_BIG = ("ffn1_w_in", "ffn1_w_out", "w_mix_in", "w_mix_out", "w_cq", "w_ckv", "w_co", "ffn2_w_in", "ffn2_w_out")
_COL_SHARDED = ("ffn1_w_in", "w_mix_in", "w_ckv", "ffn2_w_in")
_ORDER = ("ffn1_norm", "ffn1_w_in", "ffn1_w_out", "mix_norm", "w_mix_in", "ln_v_gain", "ln_v_bias", "spatial_w",
          "spatial_b", "gnorm_a", "gnorm_b", "w_mix_out", "cross_norm", "mem_norm", "w_cq", "w_ckv", "w_co",
          "ffn2_norm", "ffn2_w_in", "ffn2_w_out", "final_norm")


_SMALL_PAD = 136


def _rows128(a):
    return a.reshape(-1, 128)


def kernel(x, mem, ffn1_norm, ffn1_w_in, ffn1_w_out, mix_norm, w_mix_in, ln_v_gain, ln_v_bias, spatial_w, spatial_b, gnorm_a, gnorm_b, w_mix_out, cross_norm, mem_norm, w_cq, w_ckv, w_co, ffn2_norm, ffn2_w_in, ffn2_w_out, final_norm, loss_target, m_ffn1_norm, m_ffn1_w_in, m_ffn1_w_out, m_mix_norm, m_w_mix_in, m_ln_v_gain, m_ln_v_bias, m_spatial_w, m_spatial_b, m_gnorm_a, m_gnorm_b, m_w_mix_out, m_cross_norm, m_mem_norm, m_w_cq, m_w_ckv, m_w_co, m_ffn2_norm, m_ffn2_w_in, m_ffn2_w_out, m_final_norm, v_ffn1_norm, v_ffn1_w_in, v_ffn1_w_out, v_mix_norm, v_w_mix_in, v_ln_v_gain, v_ln_v_bias, v_spatial_w, v_spatial_b, v_gnorm_a, v_gnorm_b, v_w_mix_out, v_cross_norm, v_mem_norm, v_w_cq, v_w_ckv, v_w_co, v_ffn2_norm, v_ffn2_w_in, v_ffn2_w_out, v_final_norm):
    given = dict(locals())
    wts = {k: given[k] for k in _ORDER}
    mom = {k: given["m_" + k] for k in _ORDER}
    var = {k: given["v_" + k] for k in _ORDER}

    D = D_MODEL
    xs = x.reshape(-1, D)
    mems = mem.reshape(-1, D)
    tgt = loss_target.reshape(-1, D)
    vec = lambda a: a.reshape(1, -1)
    g1, gmix, gcross, gmem, g2, gfin = (vec(wts[k]) for k in
                                        ("ffn1_norm", "mix_norm", "cross_norm", "mem_norm", "ffn2_norm", "final_norm"))
    ln_g, ln_b, ga, gb = (vec(wts[k]) for k in ("ln_v_gain", "ln_v_bias", "gnorm_a", "gnorm_b"))
    w_s = spatial_w.reshape(G_A, SGU_BLOCK, SGU_BLOCK)
    b_t = spatial_b.reshape(G_A, SGU_BLOCK).T

    shard2d = {k: wts[k].reshape(wts[k].shape[1:]) for k in _BIG}
    (first_b,) = cast_bf16([shard2d[_BIG[0]]], "cast_first")
    shard_b = {_BIG[0]: first_b}
    full = {}

    landing, rows_done = {}, {}

    def gathering(pieces, fn, *args, **kw):
        pieces = [p if isinstance(p, tuple) else (p, None) for p in pieces]
        comm = merge_comms([comm_all_gather([shard_b[k]], rows, [landing[k]] if k in landing else None)
                            for k, rows in pieces])
        out, got = fn(*args, comm=comm, **kw)
        for (k, rows), (g,) in zip(pieces, split_results(comm, got)):
            landing[k] = g
            rows_done[k] = rows_done.get(k, 0) + (rows[1] if rows is not None else shard_b[k].shape[0])
            if rows_done[k] == shard_b[k].shape[0]:
                full[k] = g if k in _COL_SHARDED else g.reshape(-1, g.shape[2])
        return out

    shard_b.update(zip(_BIG[1:], gathering((_BIG[0],), cast_bf16, [shard2d[k] for k in _BIG[1:]], "cast_rest")))
    n1 = rms_fwd(xs, g1, "f_n1")
    a1, hsw1 = gathering(("ffn1_w_out",), mm_swiglu_g, n1, full["ffn1_w_in"], "f_a1")
    h1 = gathering(("w_mix_in",), mm_nn, hsw1, full["ffn1_w_out"], F32, "f_h1", scale=0.5, res=xs)
    n2 = gathering(("w_cq",), rms_fwd, h1, gmix, "f_n2")
    z = gathering(("w_mix_out", "w_co"), mm_nn_g, n2, full["w_mix_in"], F32, "f_z")
    ya = gathering((("w_ckv", (0, 512)),), sgu_fwd, z, ln_g, ln_b, w_s, b_t, "f_sgu")
    qkv = split_heads(z, "f_qkv")
    yb = gathering((("w_ckv", (512, 1536)), ("ffn2_w_in", (0, 512))), sb_fwd, qkv, "f_sb")
    ycat = gathering((("ffn2_w_in", (512, 128)),), rmscat_fwd, ya, yb, ga, gb, "f_ycat")
    h2 = gathering((("ffn2_w_in", (640, 256)),), mm_nn, ycat, full["w_mix_out"], F32, "f_h2", res=h1)
    n3 = gathering((("ffn2_w_in", (896, 128)),), rms_fwd, h2, gcross, "f_n3")
    memn = rms_fwd(mems, gmem, "f_memn")
    qc = gathering((("ffn2_w_in", (1024, 256)),), mm_nn, n3, full["w_cq"], BF16, "f_qc", scale=X_DH ** -0.5)
    kv = gathering((("ffn2_w_in", (1280, 128)),), mm_nn_g, memn, full["w_ckv"], BF16, "f_kv")
    o = gathering((("ffn2_w_in", (1408, 128)),), xattn_fwd, qc, kv, "f_xattn")
    h3 = gathering((("ffn2_w_in", (1536, 256)),), mm_nn, o, full["w_co"], F32, "f_h3", res=h2)
    n4 = gathering((("ffn2_w_in", (1792, 256)),), rms_fwd, h3, g2, "f_n4")
    a2, hsw2 = gathering(("ffn2_w_out",), mm_swiglu_g, n4, full["ffn2_w_in"], "f_a2")
    h4 = mm_nn(hsw2, full["ffn2_w_out"], F32, "f_h4", scale=0.5, res=h3)

    grads, parts, sums, recv = {}, {}, {}, {}
    core = lax.axis_index("c").astype(jnp.int32).reshape(1)

    def partial_of(k, g):
        grads[k] = g
        parts[k] = g if g.ndim == 3 else g.reshape(N_DEV, -1, g.shape[1])

    def reducing(pairs, chips, fn, *args, also=None, **kw):
        def piece(p):
            if isinstance(p, dict):
                return p
            k, rows = p if isinstance(p, tuple) else (p, None)
            return dict(sums=k, rows=rows, to=k)

        chips = [piece(p) for p in chips]
        comms = [comm_pairs([parts[k] for k in pairs])] if pairs else []
        comms += [comm_chips([sums[p["sums"]]], p["rows"], [recv[p["to"]]] if p["to"] in recv else None,
                             p.get("from_row"), p.get("out_rows")) for p in chips]
        comm = merge_comms(comms + ([also] if also is not None else []))
        out, got = fn(*args, comm=comm, **kw)
        got = split_results(comm, got)
        if pairs:
            for k, r in zip(pairs, got.pop(0)):
                sums[k] = pair_sum(parts[k], r, core, f"pair_sum_{k}")
        for p, (r,) in zip(chips, got):
            recv[p["to"]] = r
        return out if also is None else (out, got[-1])

    loss_part, dh4, df2, grads["final_norm"] = loss_head(h4, tgt, gfin, "loss_head")
    partial_of("ffn2_w_out", mm_tn(hsw2, df2, "b_ffn2_dwout"))
    da2 = reducing(("ffn2_w_out",), (), mm_swiglu_bwd, df2, full["ffn2_w_out"], a2, "b_ffn2_da")
    partial_of("ffn2_w_in", reducing((), ("ffn2_w_out",), mm_tn_g, n4, da2, N_DEV, "b_ffn2_dwin"))
    dn4 = reducing(("ffn2_w_in",), (), mm_nt_g, da2, full["ffn2_w_in"], "b_ffn2_dn")
    dh3, dh3b, grads["ffn2_norm"] = rms_bwd(dn4, h3, g2, dh4, 1.0, "b_n4")

    partial_of("w_co", mm_tn(o, dh3b, "b_dwco"))
    do = reducing(("w_co",), (("ffn2_w_in", (0, 256)),), mm_nt, dh3b, full["w_co"], BF16, "b_do")
    dqp, dkv = reducing((), (("ffn2_w_in", (256, 256)),), xattn_bwd, qc, kv, do, "b_xattn")
    partial_of("w_cq", mm_tn(n3, dqp, "b_dwcq"))
    dn3 = reducing(("w_cq",), (("ffn2_w_in", (512, 256)),), mm_nt, dqp, full["w_cq"], F32, "b_dn3")
    partial_of("w_ckv", mm_tn_g(memn, dkv, N_DEV, "b_dwckv"))
    dmemn = reducing(("w_ckv",), (("ffn2_w_in", (768, 128)),), mm_nt_g, dkv, full["w_ckv"], "b_dmemn")
    _, _, grads["mem_norm"] = rms_bwd(dmemn, mems, gmem, None, 1.0, "b_memn")
    dh2, dh2b, grads["cross_norm"] = rms_bwd(dn3, h2, gcross, dh3, 1.0, "b_n3")

    partial_of("w_mix_out", mm_tn(ycat, dh2b, "b_dwmixout"))
    dycat = reducing(("w_mix_out",), (("ffn2_w_in", (896, 256)),), mm_nt, dh2b, full["w_mix_out"], F32, "b_dycat")
    dya, dyb, grads["gnorm_a"], grads["gnorm_b"] = rmscat_bwd(dycat, ya, yb, ga, gb, "b_ycat")
    dza, grads["ln_v_gain"], grads["ln_v_bias"], grads["spatial_w"], grads["spatial_b"] = reducing(
        (), (("ffn2_w_in", (1152, 384)),), sgu_bwd, z, dya, ln_g, ln_b, w_s, b_t, "b_sgu")
    dq, dk, dv = reducing((), (("ffn2_w_in", (1536, 512)), "w_co", "w_cq", ("w_ckv", (0, 1024))), sb_bwd, qkv, yb,
                          dyb, "b_sb")
    dz = join_dz(dza, dq, dk, dv, "b_dz")
    partial_of("w_mix_in", reducing((), (("w_ckv", (1024, 1024)),), mm_tn_g, n2, dz, N_DEV, "b_dwmixin"))
    dn2 = reducing(("w_mix_in",), ("w_mix_out",), mm_nt_g, dz, full["w_mix_in"], "b_dn2")
    dh1, dh1b, grads["mix_norm"] = reducing((), (("w_mix_in", (0, 512)),), rms_bwd, dn2, h1, gmix, dh2, 0.5, "b_n2")

    pad = jnp.zeros((_SMALL_PAD, 128), F32)
    small_early = jnp.concatenate([_rows128(grads[k]) for k in _SMALL[1:]] + [pad], axis=0)
    g_w1out, (early_sibling,) = reducing((), (("w_mix_in", (512, 1024)),), mm_tn, hsw1, dh1b, "b_ffn1_dwout",
                                         also=comm_pairs([small_early]))
    partial_of("ffn1_w_out", g_w1out)
    early_pair = add2(small_early, early_sibling, "pair_sum_small_early")
    da1, (early_all,) = reducing(("ffn1_w_out",), (("w_mix_in", (1536, 512)),), mm_swiglu_bwd, dh1b,
                                 full["ffn1_w_out"], a1, "b_ffn1_da", also=comm_chips([early_pair]))
    half = D // 2
    partial_of("ffn1_w_in_a", reducing((), (("ffn1_w_out", (0, 352)),), mm_tn_g, n1, da1, N_DEV, "b_ffn1_dwin_a",
                                       cols=(0, half)))
    partial_of("ffn1_w_in_b", reducing(("ffn1_w_in_a",), (("ffn1_w_out", (352, 352)),), mm_tn_g, n1, da1, N_DEV,
                                       "b_ffn1_dwin_b", cols=(half, half)))
    dn1 = reducing(("ffn1_w_in_b",), (dict(sums="ffn1_w_in_a", rows=(0, half), to="ffn1_w_in", out_rows=D),),
                   mm_nt_g, da1, full["ffn1_w_in"], "b_ffn1_dn")
    (dx, _, grads["ffn1_norm"]) = reducing((), (dict(sums="ffn1_w_in_b", rows=(half, half), from_row=0, to="ffn1_w_in"),),
                                           rms_bwd, dn1, xs, g1, dh1, 1.0, "b_n1")

    out_g, out_d, out_m, out_v = {}, {}, {}, {}
    for k in _BIG:
        res = adamw(recv[k], shard2d[k], mom[k].reshape(shard2d[k].shape), var[k].reshape(shard2d[k].shape), f"adamw_{k}")
        out_g[k], out_d[k], out_m[k], out_v[k] = (t.reshape(wts[k].shape) for t in res)

    small_late = _rows128(grads[_SMALL[0]])
    (late_sibling,) = run_comm(comm_pairs([small_late]), "comm_pairs_small_late")
    late_pair = add2(small_late, late_sibling, "pair_sum_small_late")
    (late_all,) = run_comm(comm_chips([late_pair]), "comm_chips_small_late")
    small_all = jnp.concatenate([late_all, early_all], axis=1)
    pack = lambda d: jnp.concatenate([_rows128(d[k]) for k in _SMALL] + [pad], axis=0)
    res = adamw(small_all, pack(wts), pack(mom), pack(var), "adamw_small")
    row = 0
    for k in _SMALL:
        nrow = wts[k].size // 128
        for dst, t in zip((out_g, out_d, out_m, out_v), res):
            dst[k] = t[row:row + nrow].reshape(wts[k].shape)
        row += nrow

    loss = lax.psum(loss_part[0, 0], ("x", "y", "c"))
    grad_x = dx.reshape(x.shape)
    return (loss, grad_x, *[out_g[k] for k in _ORDER], *[out_d[k] for k in _ORDER],
            *[out_m[k] for k in _ORDER], *[out_v[k] for k in _ORDER])
```

```python
import functools
import math

import jax
import jax.numpy as jnp
from jax import lax
from jax.experimental import pallas as pl
from jax.experimental.pallas import tpu as pltpu

F32 = jnp.float32
BF16 = jnp.bfloat16

N_DEV = 8
D_MODEL = 2048
D_FF = 5632
W_A = 1024
G_A = 8
GA_DIM = 128
SGU_BLOCK = 128
CHUNK = 64
H_B = 8
DH_B = 128
Q_BLOCK = 128
X_HEADS = 4
X_DH = 512
N_MEM = 256
EPS = 1e-6

ADAM_LR = 0.001
ADAM_B1 = 0.9
ADAM_B2 = 0.999
ADAM_EPS = 1e-08
ADAM_WD = 0.01
ADAM_STEP = 10

VMEM_LIMIT = 56 * 2**20
ROW_TILE = 256

MESH = pl.DeviceIdType.MESH
ANY = pl.BlockSpec(memory_space=pl.ANY)

_NT = (((1,), (1,)), ((), ()))
_TN = (((0,), (0,)), ((), ()))


def _params(*sem, collective_id=None):
    return pltpu.CompilerParams(dimension_semantics=sem, vmem_limit_bytes=VMEM_LIMIT, collective_id=collective_id)


_COLLECTIVE_ID = {frozenset({"sibling"}): 0, frozenset({"chips"}): 1, frozenset({"sibling", "chips"}): 2}


def _handshake(partners):
    x, y, c = lax.axis_index("x"), lax.axis_index("y"), lax.axis_index("c")
    peers = [(x, y, 1 - c)] if "sibling" in partners else []
    if "chips" in partners:
        peers += [(1 - x, y, c), (x, 1 - y, c), (1 - x, 1 - y, c)]
    barrier = pltpu.get_barrier_semaphore()
    for peer in peers:
        pl.semaphore_signal(barrier, inc=1, device_id=peer, device_id_type=MESH)
    pl.semaphore_wait(barrier, len(peers))


def _zeros(ref):
    return jnp.zeros(ref.shape, ref.dtype)


def _pcall(comm, body, *, name, grid, in_specs, out_specs, out_shape, compiler_params, scratch_shapes=()):
    if comm is None:
        return pl.pallas_call(body, name=name, grid=grid, in_specs=in_specs, out_specs=out_specs, out_shape=out_shape,
                              scratch_shapes=list(scratch_shapes), compiler_params=compiler_params)
    multi = isinstance(out_shape, (list, tuple))
    out_shapes = list(out_shape) if multi else [out_shape]
    out_specs_l = list(out_specs) if multi else [out_specs]
    n_in, n_out, n_scr = len(in_specs), len(out_shapes), len(scratch_shapes)
    n_cin, n_cout = len(comm.arrays), len(comm.out_shapes)

    def with_comm(*refs):
        ins, refs = refs[:n_in], refs[n_in:]
        cins, refs = refs[:n_cin], refs[n_cin:]
        outs, refs = refs[:n_out], refs[n_out:]
        couts, refs = refs[:n_cout], refs[n_cout:]
        scr, sems = refs[:n_scr], refs[n_scr:]
        first = functools.reduce(jnp.logical_and, [pl.program_id(a) == 0 for a in range(len(grid))])
        last = functools.reduce(jnp.logical_and, [pl.program_id(a) == grid[a] - 1 for a in range(len(grid))])
        @pl.when(first)
        def _():
            _handshake(comm.partners)
            comm.start(cins, couts, sems)

        body(*ins, *outs, *scr)
        pl.when(last)(lambda: comm.finish(cins, couts, sems))

    call = pl.pallas_call(
        with_comm, name=name, grid=grid, in_specs=list(in_specs) + [ANY] * n_cin,
        out_specs=out_specs_l + [ANY] * n_cout, out_shape=out_shapes + comm.out_shapes,
        scratch_shapes=list(scratch_shapes) + comm.sem_shapes(),
        compiler_params=_params(*(("arbitrary",) * len(grid)), collective_id=_COLLECTIVE_ID[comm.partners]),
        input_output_aliases={n_in + i: n_out + j for i, j in comm.aliases.items()})

    def run(*args):
        res = call(*args, *comm.arrays)
        main = res[:n_out]
        return (list(main) if multi else main[0]), list(res[n_out:])

    return run


def _dot(a, b):
    return jnp.dot(a, b, preferred_element_type=F32)


def _dot_nt(a, b):
    return lax.dot_general(a, b, _NT, preferred_element_type=F32)


def _dot_tn(a, b):
    return lax.dot_general(a, b, _TN, preferred_element_type=F32)


def mm_nn_g(a, bg, out_dtype, name, tm=512, comm=None):
    M, K = a.shape
    G, _, n = bg.shape
    tm = min(tm, M)

    def body(a_ref, b_ref, o_ref):
        o_ref[...] = _dot(a_ref[...], b_ref[...]).astype(o_ref.dtype)

    return _pcall(
        comm, body, name=name, grid=(G, M // tm),
        in_specs=[pl.BlockSpec((tm, K), lambda g, m: (m, 0)),
                  pl.BlockSpec((None, K, n), lambda g, m: (g, 0, 0))],
        out_specs=pl.BlockSpec((tm, n), lambda g, m: (m, g)),
        out_shape=jax.ShapeDtypeStruct((M, G * n), out_dtype),
        compiler_params=_params("parallel", "parallel"),
    )(a, bg)


def mm_swiglu_g(a, bg, name, tm=512, comm=None):
    M, K = a.shape
    G, _, n = bg.shape
    half = G // 2
    tm = min(tm, M)

    def body(a_ref, bgate_ref, bup_ref, gu_ref, h_ref):
        av = a_ref[...]
        gate = _dot(av, bgate_ref[...])
        up = _dot(av, bup_ref[...])
        gu_ref[0] = gate.astype(BF16)
        gu_ref[1] = up.astype(BF16)
        h_ref[...] = (gate * _sigmoid(gate) * up).astype(BF16)

    return _pcall(
        comm, body, name=name, grid=(half, M // tm),
        in_specs=[pl.BlockSpec((tm, K), lambda p, m: (m, 0)),
                  pl.BlockSpec((None, K, n), lambda p, m: (p, 0, 0)),
                  pl.BlockSpec((None, K, n), lambda p, m: (p + half, 0, 0))],
        out_specs=[pl.BlockSpec((2, tm, n), lambda p, m: (0, m, p)), pl.BlockSpec((tm, n), lambda p, m: (m, p))],
        out_shape=[jax.ShapeDtypeStruct((2, M, half * n), BF16), jax.ShapeDtypeStruct((M, half * n), BF16)],
        compiler_params=_params("parallel", "parallel"),
    )(a, bg, bg)


def mm_swiglu_bwd(dy, w_out, gate_up, name, tk=512, comm=None):
    M, N = dy.shape
    F = w_out.shape[0]

    def body(dy_ref, w_ref, gu_ref, o_ref):
        dh = _dot_nt(dy_ref[...], w_ref[...])
        gt = gu_ref[0].astype(F32)
        up = gu_ref[1].astype(F32)
        sg = _sigmoid(gt)
        o_ref[0] = (dh * up * (sg * (1.0 + gt * (1.0 - sg)))).astype(BF16)
        o_ref[1] = (dh * (gt * sg)).astype(BF16)

    planes = pl.BlockSpec((2, M, tk), lambda k: (0, 0, k))
    return _pcall(
        comm, body, name=name, grid=(F // tk,),
        in_specs=[pl.BlockSpec((M, N), lambda k: (0, 0)), pl.BlockSpec((tk, N), lambda k: (k, 0)), planes],
        out_specs=planes,
        out_shape=jax.ShapeDtypeStruct((2, M, F), BF16),
        compiler_params=_params("parallel"),
    )(dy, w_out, gate_up)


def _shard_cols_spec(dy, G, rows, index):
    if dy.ndim == 2:
        n = dy.shape[1] // G
        return pl.BlockSpec((rows, n), lambda *ids: index(*ids)), n
    half = G // 2
    n = dy.shape[2] // half

    def planes(*ids):
        r, g = index(*ids)
        return g // half, r, g % half

    return pl.BlockSpec((None, rows, n), planes), n


def mm_nn(a, b, out_dtype, name, tm=512, tn=1024, scale=1.0, res=None, comm=None):
    M, K = a.shape
    _, N = b.shape
    tm, tn = min(tm, M), min(tn, N)

    def body(*refs):
        if res is None:
            a_ref, b_ref, o_ref = refs
            acc = _dot(a_ref[...], b_ref[...])
            o_ref[...] = (acc * scale if scale != 1.0 else acc).astype(o_ref.dtype)
        else:
            a_ref, b_ref, r_ref, o_ref = refs
            o_ref[...] = (r_ref[...] + scale * _dot(a_ref[...], b_ref[...])).astype(o_ref.dtype)

    in_specs = [pl.BlockSpec((tm, K), lambda n, m: (m, 0)),
                pl.BlockSpec((K, tn), lambda n, m: (0, n))]
    args = [a, b]
    if res is not None:
        in_specs.append(pl.BlockSpec((tm, tn), lambda n, m: (m, n)))
        args.append(res)
    return _pcall(
        comm, body, name=name, grid=(N // tn, M // tm),
        in_specs=in_specs,
        out_specs=pl.BlockSpec((tm, tn), lambda n, m: (m, n)),
        out_shape=jax.ShapeDtypeStruct((M, N), out_dtype),
        compiler_params=_params("parallel", "parallel"),
    )(*args)


def mm_nt_g(dy, bg, name, tm=512, comm=None):
    M = dy.shape[-2]
    G, K, n = bg.shape
    tm = min(tm, M)
    dy_spec, _ = _shard_cols_spec(dy, G, tm, lambda m, g: (m, g))

    def body(dy_ref, b_ref, o_ref):
        part = _dot_nt(dy_ref[...], b_ref[...])

        @pl.when(pl.program_id(1) == 0)
        def _():
            o_ref[...] = part

        @pl.when(pl.program_id(1) > 0)
        def _():
            o_ref[...] += part

    return _pcall(
        comm, body, name=name, grid=(M // tm, G),
        in_specs=[dy_spec, pl.BlockSpec((None, K, n), lambda m, g: (g, 0, 0))],
        out_specs=pl.BlockSpec((tm, K), lambda m, g: (m, 0)),
        out_shape=jax.ShapeDtypeStruct((M, K), F32),
        compiler_params=_params("parallel", "arbitrary"),
    )(dy, bg)


def mm_nt(dy, b, out_dtype, name, tk=512, comm=None):
    M, N = dy.shape
    K, _ = b.shape

    def body(dy_ref, b_ref, o_ref):
        o_ref[...] = _dot_nt(dy_ref[...], b_ref[...]).astype(o_ref.dtype)

    return _pcall(
        comm, body, name=name, grid=(K // tk,),
        in_specs=[pl.BlockSpec((M, N), lambda k: (0, 0)),
                  pl.BlockSpec((tk, N), lambda k: (k, 0))],
        out_specs=pl.BlockSpec((M, tk), lambda k: (0, k)),
        out_shape=jax.ShapeDtypeStruct((M, K), out_dtype),
        compiler_params=_params("parallel"),
    )(dy, b)


def mm_tn_g(x, dy, G, name, tk=512, cols=None, comm=None):
    M, K = x.shape
    k0, K = cols if cols is not None else (0, K)
    dy_spec, n = _shard_cols_spec(dy, G, M, lambda g, k: (0, g))

    def body(x_ref, dy_ref, o_ref):
        o_ref[...] = _dot_tn(x_ref[...], dy_ref[...]).astype(o_ref.dtype)

    return _pcall(
        comm, body, name=name, grid=(G, K // tk),
        in_specs=[pl.BlockSpec((M, tk), lambda g, k: (0, k + k0 // tk)), dy_spec],
        out_specs=pl.BlockSpec((None, tk, n), lambda g, k: (g, k, 0)),
        out_shape=jax.ShapeDtypeStruct((G, K, n), BF16),
        compiler_params=_params("parallel", "parallel"),
    )(x, dy)


def mm_tn(x, dy, name, tk=512, comm=None):
    M, K = x.shape
    _, N = dy.shape

    def body(x_ref, dy_ref, o_ref):
        o_ref[...] = _dot_tn(x_ref[...], dy_ref[...]).astype(o_ref.dtype)

    return _pcall(
        comm, body, name=name, grid=(K // tk,),
        in_specs=[pl.BlockSpec((M, tk), lambda k: (0, k)),
                  pl.BlockSpec((M, N), lambda k: (0, 0))],
        out_specs=pl.BlockSpec((tk, N), lambda k: (k, 0)),
        out_shape=jax.ShapeDtypeStruct((K, N), BF16),
        compiler_params=_params("parallel"),
    )(x, dy)


def _rstd(x):
    return lax.rsqrt(jnp.mean(x * x, axis=-1, keepdims=True) + EPS)


def _rms_bwd(dn, xhat, r, g):
    dxhat = dn * g
    return r * (dxhat - xhat * jnp.mean(dxhat * xhat, axis=-1, keepdims=True))


def _row_spec(tr, width, col=0):
    return pl.BlockSpec((tr, width), lambda i: (i, col))


def _vec_spec(width):
    return pl.BlockSpec((1, width), lambda i: (0, 0))


def _heads_spec(tr):
    return pl.BlockSpec((H_B, tr, DH_B), lambda i: (0, i, 0))


def _heads_to_cols(ref):
    return jnp.concatenate([ref[h] for h in range(H_B)], axis=1)


def split_heads(z, name, comm=None):
    S = z.shape[0]
    tr = min(ROW_TILE, S)
    width = H_B * DH_B
    first = 2 * W_A // width

    def body(q_ref, k_ref, v_ref, o_ref):
        for p, (ref, scale) in enumerate(((q_ref, DH_B ** -0.5), (k_ref, 1.0), (v_ref, 1.0))):
            for h in range(H_B):
                cols = ref[:, h * DH_B:(h + 1) * DH_B]
                o_ref[p * H_B + h] = (cols * scale if scale != 1.0 else cols).astype(BF16)

    return _pcall(
        comm, body, name=name, grid=(S // tr,),
        in_specs=[_row_spec(tr, width, first), _row_spec(tr, width, first + 1), _row_spec(tr, width, first + 2)],
        out_specs=pl.BlockSpec((3 * H_B, tr, DH_B), lambda i: (0, i, 0)),
        out_shape=jax.ShapeDtypeStruct((3 * H_B, S, DH_B), BF16),
        compiler_params=_params("parallel"),
    )(z, z, z)


def join_dz(dza, dq, dk, dv, name, comm=None):
    S, wa = dza.shape
    tr = min(ROW_TILE, S)
    width = H_B * DH_B

    def body(dza_ref, dq_ref, dk_ref, dv_ref, o_ref):
        o_ref[:, :wa] = dza_ref[...]
        for p, ref in enumerate((dq_ref, dk_ref, dv_ref)):
            for h in range(H_B):
                lo = wa + p * width + h * DH_B
                o_ref[:, lo:lo + DH_B] = ref[h]

    return _pcall(
        comm, body, name=name, grid=(S // tr,),
        in_specs=[_row_spec(tr, wa), _heads_spec(tr), _heads_spec(tr), _heads_spec(tr)],
        out_specs=_row_spec(tr, wa + 3 * width),
        out_shape=jax.ShapeDtypeStruct((S, wa + 3 * width), BF16),
        compiler_params=_params("parallel"),
    )(dza, dq, dk, dv)


def rms_fwd(x, g, name, comm=None):
    M, D = x.shape
    tr = min(ROW_TILE, M)

    def body(x_ref, g_ref, o_ref):
        xv = x_ref[...]
        o_ref[...] = (xv * _rstd(xv) * g_ref[...]).astype(o_ref.dtype)

    return _pcall(
        comm, body, name=name, grid=(M // tr,),
        in_specs=[_row_spec(tr, D), _vec_spec(D)],
        out_specs=_row_spec(tr, D),
        out_shape=jax.ShapeDtypeStruct((M, D), BF16),
        compiler_params=_params("parallel"),
    )(x, g)


def rms_bwd(dn, h, g, dres, copy_scale, name, comm=None):
    M, D = h.shape
    tr = min(ROW_TILE, M)
    has_res = dres is not None

    def body(*refs):
        if has_res:
            dn_ref, h_ref, g_ref, dres_ref, dh_ref, dhb_ref, dg_ref = refs
        else:
            dn_ref, h_ref, g_ref, dh_ref, dhb_ref, dg_ref = refs
        hv = h_ref[...]
        r = _rstd(hv)
        xhat = hv * r
        dn = dn_ref[...]
        part = jnp.sum(dn * xhat, axis=0, keepdims=True)

        @pl.when(pl.program_id(0) == 0)
        def _():
            dg_ref[...] = part

        @pl.when(pl.program_id(0) > 0)
        def _():
            dg_ref[...] += part

        dh = _rms_bwd(dn, xhat, r, g_ref[...])
        if has_res:
            dh = dh + dres_ref[...]
        dh_ref[...] = dh
        dhb_ref[...] = (dh * copy_scale if copy_scale != 1.0 else dh).astype(BF16)

    in_specs = [_row_spec(tr, D), _row_spec(tr, D), _vec_spec(D)]
    args = [dn, h, g]
    if has_res:
        in_specs.append(_row_spec(tr, D))
        args.append(dres)
    return _pcall(
        comm, body, name=name, grid=(M // tr,),
        in_specs=in_specs,
        out_specs=[_row_spec(tr, D), _row_spec(tr, D), _vec_spec(D)],
        out_shape=[jax.ShapeDtypeStruct((M, D), F32), jax.ShapeDtypeStruct((M, D), BF16),
                   jax.ShapeDtypeStruct((1, D), F32)],
        compiler_params=_params("arbitrary"),
    )(*args)


def _sigmoid(x):
    return 1.0 / (1.0 + jnp.exp(-x))


def rmscat_fwd(ya, yb, ga, gb, name, comm=None):
    M, W = ya.shape
    tr = min(ROW_TILE, M)

    def body(ya_ref, yb_ref, ga_ref, gb_ref, o_ref):
        a = ya_ref[...]
        b = _heads_to_cols(yb_ref)
        o_ref[:, :W] = (a * _rstd(a) * ga_ref[...]).astype(BF16)
        o_ref[:, W:] = (b * _rstd(b) * gb_ref[...]).astype(BF16)

    return _pcall(
        comm, body, name=name, grid=(M // tr,),
        in_specs=[_row_spec(tr, W), _heads_spec(tr), _vec_spec(W), _vec_spec(W)],
        out_specs=_row_spec(tr, 2 * W),
        out_shape=jax.ShapeDtypeStruct((M, 2 * W), BF16),
        compiler_params=_params("parallel"),
    )(ya, yb, ga, gb)


def rmscat_bwd(dycat, ya, yb, ga, gb, name, comm=None):
    M, W = ya.shape
    tr = min(ROW_TILE, M)

    def body(dc_ref, ya_ref, yb_ref, ga_ref, gb_ref, dya_ref, dyb_ref, dga_ref, dgb_ref):
        first = pl.program_id(0) == 0
        for by_head, y_ref, g_ref, dy_ref, dg_ref, lo in ((False, ya_ref, ga_ref, dya_ref, dga_ref, 0),
                                                          (True, yb_ref, gb_ref, dyb_ref, dgb_ref, W)):
            yv = _heads_to_cols(y_ref) if by_head else y_ref[...]
            r = _rstd(yv)
            xhat = yv * r
            dn = dc_ref[:, lo:lo + W]
            part = jnp.sum(dn * xhat, axis=0, keepdims=True)

            @pl.when(first)
            def _():
                dg_ref[...] = part

            @pl.when(jnp.logical_not(first))
            def _():
                dg_ref[...] += part

            dy = _rms_bwd(dn, xhat, r, g_ref[...])
            if by_head:
                for h in range(H_B):
                    dy_ref[h] = dy[:, h * DH_B:(h + 1) * DH_B]
            else:
                dy_ref[...] = dy

    return _pcall(
        comm, body, name=name, grid=(M // tr,),
        in_specs=[_row_spec(tr, 2 * W), _row_spec(tr, W), _heads_spec(tr), _vec_spec(W), _vec_spec(W)],
        out_specs=[_row_spec(tr, W), _heads_spec(tr), _vec_spec(W), _vec_spec(W)],
        out_shape=[jax.ShapeDtypeStruct((M, W), F32), jax.ShapeDtypeStruct((H_B, M, DH_B), F32),
                   jax.ShapeDtypeStruct((1, W), F32), jax.ShapeDtypeStruct((1, W), F32)],
        compiler_params=_params("arbitrary"),
    )(dycat, ya, yb, ga, gb)


def loss_head(h, target, g, name, comm=None):
    M, D = h.shape
    tr = min(ROW_TILE, M)

    def body(h_ref, t_ref, g_ref, loss_ref, dh_ref, dhb_ref, dg_ref):
        hv = h_ref[...]
        gv = g_ref[...]
        r = _rstd(hv)
        xhat = hv * r
        err = xhat * gv - t_ref[...]
        lsum = jnp.sum(jnp.sum(err * err, axis=1, keepdims=True), axis=0, keepdims=True) * (0.5 / D)
        dy = err * (1.0 / D)
        part = jnp.sum(dy * xhat, axis=0, keepdims=True)

        @pl.when(pl.program_id(0) == 0)
        def _():
            dg_ref[...] = part
            loss_ref[...] = _zeros(loss_ref) + lsum

        @pl.when(pl.program_id(0) > 0)
        def _():
            dg_ref[...] += part
            loss_ref[...] += lsum

        dh = _rms_bwd(dy, xhat, r, gv)
        dh_ref[...] = dh
        dhb_ref[...] = (0.5 * dh).astype(BF16)

    return _pcall(
        comm, body, name=name, grid=(M // tr,),
        in_specs=[_row_spec(tr, D), _row_spec(tr, D), _vec_spec(D)],
        out_specs=[pl.BlockSpec((8, 128), lambda i: (0, 0)), _row_spec(tr, D), _row_spec(tr, D), _vec_spec(D)],
        out_shape=[jax.ShapeDtypeStruct((8, 128), F32), jax.ShapeDtypeStruct((M, D), F32),
                   jax.ShapeDtypeStruct((M, D), BF16), jax.ShapeDtypeStruct((1, D), F32)],
        compiler_params=_params("arbitrary"),
    )(h, target, g)


_GELU_C = math.sqrt(2.0 / math.pi)


def _gelu(x):
    return 0.5 * x * (1.0 + jnp.tanh(_GELU_C * (x + 0.044715 * (x * x * x))))


def _gelu_grad(x):
    t = jnp.tanh(_GELU_C * (x + 0.044715 * (x * x * x)))
    return 0.5 * (1.0 + t) + 0.5 * x * (1.0 - t * t) * (_GELU_C * (1.0 + 3.0 * 0.044715 * (x * x)))


def _sgu_mask():
    t = lax.broadcasted_iota(jnp.int32, (SGU_BLOCK, SGU_BLOCK), 0) // CHUNK
    s = lax.broadcasted_iota(jnp.int32, (SGU_BLOCK, SGU_BLOCK), 1) // CHUNK
    return s <= t


def _layernorm_stats(v):
    mu = jnp.mean(v, axis=-1, keepdims=True)
    cen = v - mu
    rstd = lax.rsqrt(jnp.mean(cen * cen, axis=-1, keepdims=True) + EPS)
    return cen * rstd, rstd


def sgu_fwd(z, ln_g, ln_b, w_s, b_t, name, comm=None):
    S = z.shape[0]

    def body(zu_ref, zv_ref, lg_ref, lb_ref, w_ref, bt_ref, o_ref):
        mask = _sgu_mask()
        for g in range(G_A):
            cols = slice(g * GA_DIM, (g + 1) * GA_DIM)
            u = _gelu(zu_ref[:, cols])
            vhat, _ = _layernorm_stats(_gelu(zv_ref[:, cols]))
            vln = vhat * lg_ref[:, cols] + lb_ref[:, cols]
            w = jnp.where(mask, w_ref[g], 0.0).astype(BF16)
            mixed = _dot(w, vln.astype(BF16)) + bt_ref[:, g:g + 1]
            o_ref[:, cols] = u * mixed

    return _pcall(
        comm, body, name=name, grid=(S // SGU_BLOCK,),
        in_specs=[_row_spec(SGU_BLOCK, W_A, 0), _row_spec(SGU_BLOCK, W_A, 1), _vec_spec(W_A), _vec_spec(W_A),
                  pl.BlockSpec((G_A, SGU_BLOCK, SGU_BLOCK), lambda i: (0, 0, 0)),
                  pl.BlockSpec((SGU_BLOCK, G_A), lambda i: (0, 0))],
        out_specs=_row_spec(SGU_BLOCK, W_A),
        out_shape=jax.ShapeDtypeStruct((S, W_A), F32),
        compiler_params=_params("parallel"),
    )(z, z, ln_g, ln_b, w_s, b_t)


def sgu_bwd(z, dya, ln_g, ln_b, w_s, b_t, name, comm=None):
    S = z.shape[0]
    nblk = S // SGU_BLOCK

    def body(zu_ref, zv_ref, dy_ref, lg_ref, lb_ref, w_ref, bt_ref,
             dz_ref, dlg_ref, dlb_ref, dw_ref, db_ref, dmix_acc):
        step = pl.program_id(0)
        mask = _sgu_mask()

        @pl.when(step == 0)
        def _():
            dlg_ref[...] = _zeros(dlg_ref)
            dlb_ref[...] = _zeros(dlb_ref)
            dw_ref[...] = _zeros(dw_ref)
            dmix_acc[...] = _zeros(dmix_acc)

        for g in range(G_A):
            cols = slice(g * GA_DIM, (g + 1) * GA_DIM)
            zu = zu_ref[:, cols]
            zv = zv_ref[:, cols]
            u = _gelu(zu)
            vhat, rstd = _layernorm_stats(_gelu(zv))
            lg = lg_ref[:, cols]
            vln = (vhat * lg + lb_ref[:, cols]).astype(BF16)
            w = jnp.where(mask, w_ref[g], 0.0)
            mixed = _dot(w.astype(BF16), vln) + bt_ref[:, g:g + 1]
            dy = dy_ref[:, cols]
            du = dy * mixed
            dmixed = dy * u
            dmixed_b = dmixed.astype(BF16)
            dmix_acc[g] += dmixed
            dw_ref[g] += jnp.where(mask, _dot_nt(dmixed_b, vln), 0.0)
            dvln = _dot(w.T.astype(BF16), dmixed_b)
            dlb_ref[:, cols] += jnp.sum(dvln, axis=0, keepdims=True)
            dlg_ref[:, cols] += jnp.sum(dvln * vhat, axis=0, keepdims=True)
            dvhat = dvln * lg
            dv = rstd * (dvhat - jnp.mean(dvhat, axis=-1, keepdims=True)
                         - vhat * jnp.mean(dvhat * vhat, axis=-1, keepdims=True))
            dz_ref[:, cols] = (du * _gelu_grad(zu)).astype(BF16)
            dz_ref[:, W_A + g * GA_DIM:W_A + (g + 1) * GA_DIM] = (dv * _gelu_grad(zv)).astype(BF16)

        @pl.when(step == nblk - 1)
        def _():
            for g in range(G_A):
                db_ref[g] = jnp.sum(dmix_acc[g], axis=1, keepdims=True)

    whole3 = lambda shape: pl.BlockSpec(shape, lambda i: (0, 0, 0))
    return _pcall(
        comm, body, name=name, grid=(nblk,),
        in_specs=[_row_spec(SGU_BLOCK, W_A, 0), _row_spec(SGU_BLOCK, W_A, 1), _row_spec(SGU_BLOCK, W_A),
                  _vec_spec(W_A), _vec_spec(W_A), whole3((G_A, SGU_BLOCK, SGU_BLOCK)),
                  pl.BlockSpec((SGU_BLOCK, G_A), lambda i: (0, 0))],
        out_specs=[_row_spec(SGU_BLOCK, 2 * W_A), _vec_spec(W_A), _vec_spec(W_A),
                   whole3((G_A, SGU_BLOCK, SGU_BLOCK)), whole3((G_A, SGU_BLOCK, 1))],
        out_shape=[jax.ShapeDtypeStruct((S, 2 * W_A), BF16), jax.ShapeDtypeStruct((1, W_A), F32),
                   jax.ShapeDtypeStruct((1, W_A), F32), jax.ShapeDtypeStruct((G_A, SGU_BLOCK, SGU_BLOCK), F32),
                   jax.ShapeDtypeStruct((G_A, SGU_BLOCK, 1), F32)],
        scratch_shapes=[pltpu.VMEM((G_A, SGU_BLOCK, SGU_BLOCK), F32)],
        compiler_params=_params("arbitrary"),
    )(z, z, dya, ln_g, ln_b, w_s, b_t)


def _log_sigmoid(z):
    return jnp.minimum(z, 0.0) - jnp.log(1.0 + jnp.exp(-jnp.abs(z)))


def _suffix_sum(x, upper):
    hi = x.astype(BF16)
    lo = (x - hi.astype(F32)).astype(BF16)
    return _dot(hi, upper) + _dot(lo, upper)


SB_ROWS = 2048
_SB_SUB = SB_ROWS // Q_BLOCK


def _sb_upper():
    row = lax.broadcasted_iota(jnp.int32, (Q_BLOCK, Q_BLOCK), 0)
    col = lax.broadcasted_iota(jnp.int32, (Q_BLOCK, Q_BLOCK), 1)
    return (row > col).astype(BF16)


def _sb_sweep(step, tile):
    for r in reversed(range(_SB_SUB)):
        tile(step * _SB_SUB + r, r * Q_BLOCK)

    def group(g, _):
        base = (step - 1 - g) * _SB_SUB
        for r in reversed(range(_SB_SUB)):
            tile(base + r, None)
        return 0

    lax.fori_loop(0, step, group, 0)


def _sb_causal(n):
    return lax.broadcasted_iota(jnp.int32, (n, Q_BLOCK), 1) < lax.broadcasted_iota(jnp.int32, (n, Q_BLOCK), 0)


def _sb_rows_spec():
    return pl.BlockSpec((None, SB_ROWS, DH_B), lambda h, i: (h, i, 0))


def _sb_head_spec(S, part=0):
    return pl.BlockSpec((None, S, DH_B), lambda h, i: (part * H_B + h, 0, 0))


def sb_fwd(qkv, name, comm=None):
    S = qkv.shape[1]

    def body(q_b, k_ref, v_ref, o_ref, c_l1m):
        step = pl.program_id(1)
        o_ref[...] = _zeros(o_ref)
        c_l1m[...] = _zeros(c_l1m)
        upper = _sb_upper()

        def tile(j, row0):
            rq = slice(row0 or 0, SB_ROWS)
            causal = None if row0 is None else _sb_causal(SB_ROWS - row0)
            rows = pl.ds(pl.multiple_of(j * Q_BLOCK, Q_BLOCK), Q_BLOCK)
            zz = _dot_nt(q_b[rq, :], k_ref[rows, :])
            lb = _log_sigmoid(zz)
            l1m = lb - zz
            if causal is not None:
                l1m = jnp.where(causal, l1m, 0.0)
            a = jnp.exp(lb + _suffix_sum(l1m, upper) + c_l1m[rq, :])
            if causal is not None:
                a = jnp.where(causal, a, 0.0)
            o_ref[rq, :] += _dot(a.astype(BF16), v_ref[rows, :])
            c_l1m[rq, :] += jnp.sum(l1m, axis=1, keepdims=True)

        _sb_sweep(step, tile)

    return _pcall(
        comm, body, name=name, grid=(H_B, S // SB_ROWS),
        in_specs=[_sb_rows_spec(), _sb_head_spec(S, 1), _sb_head_spec(S, 2)],
        out_specs=_sb_rows_spec(),
        out_shape=jax.ShapeDtypeStruct((H_B, S, DH_B), F32),
        scratch_shapes=[pltpu.VMEM((SB_ROWS, 1), F32)],
        compiler_params=_params("parallel", "parallel"),
    )(qkv, qkv, qkv)


def sb_bwd(qkv, out, dout, name, comm=None):
    S = qkv.shape[1]
    nstep = S // SB_ROWS
    scale = DH_B ** -0.5

    def body(q_b, k_ref, v_ref, o_ref, do_ref, dq_ref, dk_ref, dv_ref,
             dq_acc, dkt_acc, dvt_acc, do_b, qt_b, dot_b, g_left, c_l1m):
        step = pl.program_id(1)

        @pl.when(step == 0)
        def _():
            dkt_acc[...] = _zeros(dkt_acc)
            dvt_acc[...] = _zeros(dvt_acc)

        do_b[...] = do_ref[...].astype(BF16)
        qt_b[...] = q_b[...].astype(F32).T.astype(BF16)
        dot_b[...] = do_ref[...].T.astype(BF16)
        g_left[...] = jnp.sum(do_b[...].astype(F32) * o_ref[...], axis=1, keepdims=True)
        dq_acc[...] = _zeros(dq_acc)
        c_l1m[...] = _zeros(c_l1m)
        upper = _sb_upper()

        def tile(j, row0):
            rq = slice(row0 or 0, SB_ROWS)
            causal = None if row0 is None else _sb_causal(SB_ROWS - row0)
            rows = pl.ds(pl.multiple_of(j * Q_BLOCK, Q_BLOCK), Q_BLOCK)
            q, do_t = q_b[rq, :], do_b[rq, :]
            k_j = k_ref[rows, :]
            zz = _dot_nt(q, k_j)
            lb = _log_sigmoid(zz)
            l1m = lb - zz
            if causal is not None:
                l1m = jnp.where(causal, l1m, 0.0)
            a = jnp.exp(lb + _suffix_sum(l1m, upper) + c_l1m[rq, :])
            if causal is not None:
                a = jnp.where(causal, a, 0.0)
            a_b = a.astype(BF16)
            dvt_acc[:, rows] += _dot(dot_b[:, rq], a_b)
            gmat = a_b.astype(F32) * _dot_nt(do_t, v_ref[rows, :])
            before = g_left[rq, :] - gmat - _suffix_sum(gmat, upper)
            sig = jnp.exp(lb)
            dz = gmat * (1.0 - sig) - sig * before
            if causal is not None:
                dz = jnp.where(causal, dz, 0.0)
            dz_b = dz.astype(BF16)
            dkt_acc[:, rows] += _dot(qt_b[:, rq], dz_b)
            dq_acc[rq, :] += _dot(dz_b, k_j)
            c_l1m[rq, :] += jnp.sum(l1m, axis=1, keepdims=True)
            g_left[rq, :] -= jnp.sum(gmat, axis=1, keepdims=True)

        _sb_sweep(step, tile)
        dq_ref[...] = (dq_acc[...] * scale).astype(BF16)

        @pl.when(step == nstep - 1)
        def _():
            dk_ref[...] = dkt_acc[...].T.astype(BF16)
            dv_ref[...] = dvt_acc[...].T.astype(BF16)

    out_sds = jax.ShapeDtypeStruct((H_B, S, DH_B), BF16)
    return _pcall(
        comm, body, name=name, grid=(H_B, nstep),
        in_specs=[_sb_rows_spec(), _sb_head_spec(S, 1), _sb_head_spec(S, 2), _sb_rows_spec(), _sb_rows_spec()],
        out_specs=[_sb_rows_spec(), _sb_head_spec(S), _sb_head_spec(S)],
        out_shape=[out_sds, out_sds, out_sds],
        scratch_shapes=[pltpu.VMEM((SB_ROWS, DH_B), F32)] + [pltpu.VMEM((DH_B, S), F32)] * 2
        + [pltpu.VMEM((SB_ROWS, DH_B), BF16)] + [pltpu.VMEM((DH_B, SB_ROWS), BF16)] * 2
        + [pltpu.VMEM((SB_ROWS, 1), F32)] * 2,
        compiler_params=_params("parallel", "arbitrary"),
    )(qkv, qkv, qkv, out, dout)


def _softmax(s):
    e = jnp.exp(s - jnp.max(s, axis=-1, keepdims=True))
    return e / jnp.sum(e, axis=-1, keepdims=True)


def xattn_fwd(qc, kv, name, comm=None):
    S, D = qc.shape
    tr = min(ROW_TILE, S)

    def body(q_ref, kv_ref, o_ref):
        for h in range(X_HEADS):
            cols = slice(h * X_DH, (h + 1) * X_DH)
            p = _softmax(_dot_nt(q_ref[:, cols], kv_ref[:, cols]))
            o_ref[:, cols] = _dot(p.astype(BF16), kv_ref[:, D + h * X_DH:D + (h + 1) * X_DH]).astype(BF16)

    return _pcall(
        comm, body, name=name, grid=(S // tr,),
        in_specs=[_row_spec(tr, D), pl.BlockSpec((N_MEM, 2 * D), lambda i: (0, 0))],
        out_specs=_row_spec(tr, D),
        out_shape=jax.ShapeDtypeStruct((S, D), BF16),
        compiler_params=_params("parallel"),
    )(qc, kv)


def xattn_bwd(qc, kv, do, name, comm=None):
    S, D = qc.shape
    tr = min(ROW_TILE, S)
    nstep = S // tr
    scale = X_DH ** -0.5

    def body(q_ref, kv_ref, do_ref, dq_ref, dkv_ref, acc):
        step = pl.program_id(0)

        @pl.when(step == 0)
        def _():
            acc[...] = _zeros(acc)

        for h in range(X_HEADS):
            cols = slice(h * X_DH, (h + 1) * X_DH)
            vcols = slice(D + h * X_DH, D + (h + 1) * X_DH)
            q = q_ref[:, cols]
            k = kv_ref[:, cols]
            do_h = do_ref[:, cols]
            p = _softmax(_dot_nt(q, k))
            dp = _dot_nt(do_h, kv_ref[:, vcols])
            acc[:, vcols] += _dot_tn(p.astype(BF16), do_h)
            ds = (p * (dp - jnp.sum(p * dp, axis=-1, keepdims=True))).astype(BF16)
            dq_ref[:, cols] = (_dot(ds, k) * scale).astype(BF16)
            acc[:, cols] += _dot_tn(ds, q)

        @pl.when(step == nstep - 1)
        def _():
            dkv_ref[...] = acc[...].astype(BF16)

    whole = pl.BlockSpec((N_MEM, 2 * D), lambda i: (0, 0))
    return _pcall(
        comm, body, name=name, grid=(nstep,),
        in_specs=[_row_spec(tr, D), whole, _row_spec(tr, D)],
        out_specs=[_row_spec(tr, D), whole],
        out_shape=[jax.ShapeDtypeStruct((S, D), BF16), jax.ShapeDtypeStruct((N_MEM, 2 * D), BF16)],
        scratch_shapes=[pltpu.VMEM((N_MEM, 2 * D), F32)],
        compiler_params=_params("arbitrary"),
    )(qc, kv, do)


def _row_tile(rows, cap=128):
    return max(t for t in range(16, cap + 1, 16) if rows % t == 0)


def cast_bf16(ws, name, steps=4, comm=None):
    n = len(ws)

    def body(*refs):
        for i in range(n):
            refs[n + i][...] = refs[i][...].astype(BF16)

    specs = [_row_spec(w.shape[0] // steps, w.shape[1]) for w in ws]
    return _pcall(
        comm, body, name=name, grid=(steps,), in_specs=specs, out_specs=specs,
        out_shape=[jax.ShapeDtypeStruct(w.shape, BF16) for w in ws],
        compiler_params=_params("parallel"),
    )(*ws)


def adamw(parts, w, m, v, name, comm=None):
    R, C = w.shape
    n_parts = parts.shape[0]
    tr = _row_tile(R, 256)
    c1 = 1.0 - ADAM_B1 ** ADAM_STEP
    c2 = 1.0 - ADAM_B2 ** ADAM_STEP

    def body(p_ref, w_ref, m_ref, v_ref, g_ref, d_ref, mo_ref, vo_ref):
        g = p_ref[0].astype(F32)
        for p in range(1, n_parts):
            g = g + p_ref[p].astype(F32)
        m_new = ADAM_B1 * m_ref[...] + (1.0 - ADAM_B1) * g
        v_new = ADAM_B2 * v_ref[...] + (1.0 - ADAM_B2) * (g * g)
        g_ref[...] = g
        mo_ref[...] = m_new
        vo_ref[...] = v_new
        d_ref[...] = -ADAM_LR * ((m_new / c1) / (jnp.sqrt(v_new / c2) + ADAM_EPS) + ADAM_WD * w_ref[...])

    spec = _row_spec(tr, C)
    sds = jax.ShapeDtypeStruct((R, C), F32)
    return _pcall(
        comm, body, name=name, grid=(R // tr,),
        in_specs=[pl.BlockSpec((n_parts, tr, C), lambda i: (0, i, 0)), spec, spec, spec],
        out_specs=[spec, spec, spec, spec],
        out_shape=[sds, sds, sds, sds],
        compiler_params=_params("parallel"),
    )(parts, w, m, v)


def pair_sum(parts, from_sibling, core, name):
    _, R, C = parts.shape
    tr = _row_tile(R, 1024)

    def body(core_ref, p_ref, s_ref, o_ref):
        o_ref[...] = (p_ref[...].astype(F32) + s_ref[...].astype(F32)).astype(o_ref.dtype)

    return pl.pallas_call(
        body, name=name,
        grid_spec=pltpu.PrefetchScalarGridSpec(
            num_scalar_prefetch=1, grid=(4, R // tr),
            in_specs=[pl.BlockSpec((None, tr, C), lambda q, i, core_ref: (2 * q + core_ref[0], i, 0)),
                      pl.BlockSpec((None, tr, C), lambda q, i, core_ref: (q, i, 0))],
            out_specs=pl.BlockSpec((None, tr, C), lambda q, i, core_ref: (q, i, 0))),
        out_shape=jax.ShapeDtypeStruct((4, R, C), BF16),
        compiler_params=_params("parallel", "parallel"),
    )(core, parts, from_sibling)


def add2(a, b, name, comm=None):
    R, C = a.shape
    tr = _row_tile(R, 256)

    def body(a_ref, b_ref, o_ref):
        o_ref[...] = a_ref[...] + b_ref[...]

    spec = _row_spec(tr, C)
    return _pcall(
        comm, body, name=name, grid=(R // tr,), in_specs=[spec, spec], out_specs=spec,
        out_shape=jax.ShapeDtypeStruct((R, C), F32), compiler_params=_params("parallel"),
    )(a, b)


def _place():
    return lax.axis_index("x"), lax.axis_index("y"), lax.axis_index("c")


class Comm:
    def __init__(self, partners, arrays, out_shapes, n_remote, n_local, start, finish, aliases=None):
        self.partners = frozenset(partners)
        self.arrays, self.out_shapes = list(arrays), list(out_shapes)
        self.n_remote, self.n_local = n_remote, max(n_local, 1)
        self.start, self.finish = start, finish
        self.aliases = dict(aliases or {})
        self.sizes = [len(self.out_shapes)]

    def sem_shapes(self):
        return [pltpu.SemaphoreType.DMA((self.n_remote,)), pltpu.SemaphoreType.DMA((self.n_remote,)),
                pltpu.SemaphoreType.DMA((self.n_local,))]


class _Shifted:
    def __init__(self, ref, offset):
        self.ref, self.offset = ref, offset

    @property
    def at(self):
        return self

    def __getitem__(self, k):
        return self.ref.at[self.offset + k]


def merge_comms(comms):
    comms = [c for c in comms if c is not None]
    if not comms:
        return None

    def each(method):
        def run(ins, outs, sems):
            i = o = r = l = 0
            for c in comms:
                sub = (_Shifted(sems[0], r), _Shifted(sems[1], r), _Shifted(sems[2], l))
                getattr(c, method)(ins[i:i + len(c.arrays)], outs[o:o + len(c.out_shapes)], sub)
                i, o, r, l = i + len(c.arrays), o + len(c.out_shapes), r + c.n_remote, l + c.n_local
        return run

    aliases, i, o = {}, 0, 0
    for c in comms:
        aliases.update({i + a: o + b for a, b in c.aliases.items()})
        i, o = i + len(c.arrays), o + len(c.out_shapes)
    merged = Comm(frozenset().union(*[c.partners for c in comms]),
                  [a for c in comms for a in c.arrays], [s for c in comms for s in c.out_shapes],
                  sum(c.n_remote for c in comms), sum(c.n_local for c in comms), each("start"), each("finish"), aliases)
    merged.sizes = [len(c.out_shapes) for c in comms]
    return merged


def split_results(comm, results):
    out, i = [], 0
    for n in comm.sizes:
        out.append(list(results[i:i + n]))
        i += n
    return out


def run_comm(comm, name):
    n_in, n_out = len(comm.arrays), len(comm.out_shapes)

    def body(*refs):
        ins, outs, sems = refs[:n_in], refs[n_in:n_in + n_out], refs[n_in + n_out:]
        _handshake(comm.partners)
        comm.start(ins, outs, sems)
        comm.finish(ins, outs, sems)

    return pl.pallas_call(
        body, name=name, in_specs=[ANY] * n_in, out_specs=[ANY] * n_out, out_shape=comm.out_shapes,
        scratch_shapes=comm.sem_shapes(), input_output_aliases=comm.aliases,
        compiler_params=pltpu.CompilerParams(collective_id=_COLLECTIVE_ID[comm.partners]),
    )(*comm.arrays)


def _remote(src, dst, sems, k, to):
    return pltpu.make_async_remote_copy(src_ref=src, dst_ref=dst, send_sem=sems[0].at[k], recv_sem=sems[1].at[k],
                                        device_id=to, device_id_type=MESH)


_AG_COPIES = 13


def comm_all_gather(shards, rows=None, into=None):
    n = len(shards)
    row0, nrows = rows if rows is not None else (0, None)

    def parties():
        x, y, c = _place()
        return (x, y, c), (x, y, 1 - c), [(1 - x, y), (x, 1 - y), (1 - x, 1 - y)]

    def span(w, half=None):
        count = nrows if nrows is not None else shards[w].shape[0]
        if half is None:
            return pl.ds(row0, count)
        return pl.ds(row0 + half * (count // 2), count // 2)

    def slab(outs, w, dev, half=None):
        return outs[w].at[4 * dev[0] + 2 * dev[1] + dev[2], span(w, half)]

    def own(ins, outs, sems):
        me, sibling, chips = parties()
        local = [pltpu.make_async_copy(ins[w].at[span(w)], slab(outs, w, me), sems[2].at[w]) for w in range(n)]
        first = []
        for w in range(n):
            k = _AG_COPIES * w
            first.append(_remote(ins[w].at[span(w)], slab(outs, w, me), sems, k, sibling))
            first += [_remote(ins[w].at[span(w, h)], slab(outs, w, me, h), sems, k + 1 + 2 * j + h, (*chip, me[2]))
                      for h in range(2) for j, chip in enumerate(chips)]
        return local, first

    def start(ins, outs, sems):
        local, first = own(ins, outs, sems)
        for cp in local + first:
            cp.start()

    def finish(ins, outs, sems):
        me, sibling, chips = parties()
        local, first = own(ins, outs, sems)
        passed = []
        for w in range(n):
            k = _AG_COPIES * w
            for h in range(2):
                for j, chip in enumerate(chips):
                    got = slab(outs, w, (*chip, me[2]), h)
                    _remote(got, got, sems, k + 1 + 2 * j + h, me).wait_recv()
                    cp = _remote(got, got, sems, k + 7 + 2 * j + h, sibling)
                    cp.start()
                    passed.append(cp)
        for w in range(n):
            k = _AG_COPIES * w
            got = slab(outs, w, sibling)
            _remote(got, got, sems, k, me).wait_recv()
            for h in range(2):
                for j, chip in enumerate(chips):
                    got = slab(outs, w, (*chip, sibling[2]), h)
                    _remote(got, got, sems, k + 7 + 2 * j + h, me).wait_recv()
        for cp in first + passed:
            cp.wait_send()
        for cp in local:
            cp.wait()

    out_shapes = [jax.ShapeDtypeStruct((N_DEV,) + s.shape, s.dtype) for s in shards]
    arrays = list(shards) + (list(into) if into is not None else [])
    aliases = {n + w: w for w in range(n)} if into is not None else None
    return Comm(("sibling", "chips"), arrays, out_shapes, _AG_COPIES * n, n, start, finish, aliases)


def comm_pairs(items):
    slabbed = [a.ndim == 3 for a in items]
    first = [sum(4 if s else 1 for s in slabbed[:w]) for w in range(len(items))]

    def copies(ins, outs, sems):
        x, y, c = _place()
        sibling = (x, y, 1 - c)
        cps = []
        for w, s in enumerate(slabbed):
            if s:
                cps += [_remote(ins[w].at[2 * q + (1 - c)], outs[w].at[q], sems, first[w] + q, sibling) for q in range(4)]
            else:
                cps.append(_remote(ins[w], outs[w], sems, first[w], sibling))
        return cps

    def start(ins, outs, sems):
        for cp in copies(ins, outs, sems):
            cp.start()

    def finish(ins, outs, sems):
        for cp in copies(ins, outs, sems):
            cp.wait()

    out_shapes = [jax.ShapeDtypeStruct(((4,) + a.shape[1:]) if s else a.shape, a.dtype) for a, s in zip(items, slabbed)]
    return Comm(("sibling",), items, out_shapes, sum(4 if s else 1 for s in slabbed), 0, start, finish)


def comm_chips(items, rows=None, into=None, from_row=None, out_rows=None):
    n = len(items)
    slabbed = [a.ndim == 3 for a in items]

    def span(w, source=False):
        if rows is None:
            return pl.ds(0, items[w].shape[-2])
        return pl.ds(from_row if source and from_row is not None else rows[0], rows[1])

    def copies(ins, outs, sems):
        x, y, c = _place()
        mine = 2 * x + y
        local = [pltpu.make_async_copy(ins[w].at[mine, span(w, True)] if slabbed[w] else ins[w].at[span(w, True)],
                                       outs[w].at[mine, span(w)], sems[2].at[w]) for w in range(n)]
        remote = []
        for w in range(n):
            for j, (px, py) in enumerate([(1 - x, y), (x, 1 - y), (1 - x, 1 - y)]):
                src = ins[w].at[2 * px + py, span(w, True)] if slabbed[w] else ins[w].at[span(w, True)]
                remote.append(_remote(src, outs[w].at[mine, span(w)], sems, 3 * w + j, (px, py, c)))
        return local, remote

    def start(ins, outs, sems):
        local, remote = copies(ins, outs, sems)
        for cp in local + remote:
            cp.start()

    def finish(ins, outs, sems):
        local, remote = copies(ins, outs, sems)
        for cp in remote + local:
            cp.wait()

    def result(a):
        tall = a.shape[:-2] + (out_rows if out_rows is not None else a.shape[-2], a.shape[-1])
        return jax.ShapeDtypeStruct(tall if a.ndim == 3 else (4,) + tall, a.dtype)

    if into is None:
        return Comm(("chips",), items, [result(a) for a in items], 3 * n, n, start, finish)
    out_shapes = [jax.ShapeDtypeStruct(b.shape, b.dtype) for b in into]
    return Comm(("chips",), list(items) + list(into), out_shapes, 3 * n, n, start, finish, {n + w: w for w in range(n)})


_SMALL = ("ffn1_norm", "mix_norm", "ln_v_gain", "ln_v_bias", "spatial_w", "spatial_b", "gnorm_a", "gnorm_b",
          "cross_norm", "mem_norm", "ffn2_norm", "final_norm")
_BIG = ("ffn1_w_in", "ffn1_w_out", "w_mix_in", "w_mix_out", "w_cq", "w_ckv", "w_co", "ffn2_w_in", "ffn2_w_out")
_COL_SHARDED = ("ffn1_w_in", "w_mix_in", "w_ckv", "ffn2_w_in")
_ORDER = ("ffn1_norm", "ffn1_w_in", "ffn1_w_out", "mix_norm", "w_mix_in", "ln_v_gain", "ln_v_bias", "spatial_w",
          "spatial_b", "gnorm_a", "gnorm_b", "w_mix_out", "cross_norm", "mem_norm", "w_cq", "w_ckv", "w_co",
          "ffn2_norm", "ffn2_w_in", "ffn2_w_out", "final_norm")


_SMALL_PAD = 136


def _rows128(a):
    return a.reshape(-1, 128)


def kernel(x, mem, ffn1_norm, ffn1_w_in, ffn1_w_out, mix_norm, w_mix_in, ln_v_gain, ln_v_bias, spatial_w, spatial_b, gnorm_a, gnorm_b, w_mix_out, cross_norm, mem_norm, w_cq, w_ckv, w_co, ffn2_norm, ffn2_w_in, ffn2_w_out, final_norm, loss_target, m_ffn1_norm, m_ffn1_w_in, m_ffn1_w_out, m_mix_norm, m_w_mix_in, m_ln_v_gain, m_ln_v_bias, m_spatial_w, m_spatial_b, m_gnorm_a, m_gnorm_b, m_w_mix_out, m_cross_norm, m_mem_norm, m_w_cq, m_w_ckv, m_w_co, m_ffn2_norm, m_ffn2_w_in, m_ffn2_w_out, m_final_norm, v_ffn1_norm, v_ffn1_w_in, v_ffn1_w_out, v_mix_norm, v_w_mix_in, v_ln_v_gain, v_ln_v_bias, v_spatial_w, v_spatial_b, v_gnorm_a, v_gnorm_b, v_w_mix_out, v_cross_norm, v_mem_norm, v_w_cq, v_w_ckv, v_w_co, v_ffn2_norm, v_ffn2_w_in, v_ffn2_w_out, v_final_norm):
    given = dict(locals())
    wts = {k: given[k] for k in _ORDER}
    mom = {k: given["m_" + k] for k in _ORDER}
    var = {k: given["v_" + k] for k in _ORDER}

    D = D_MODEL
    xs = x.reshape(-1, D)
    mems = mem.reshape(-1, D)
    tgt = loss_target.reshape(-1, D)
    vec = lambda a: a.reshape(1, -1)
    g1, gmix, gcross, gmem, g2, gfin = (vec(wts[k]) for k in
                                        ("ffn1_norm", "mix_norm", "cross_norm", "mem_norm", "ffn2_norm", "final_norm"))
    ln_g, ln_b, ga, gb = (vec(wts[k]) for k in ("ln_v_gain", "ln_v_bias", "gnorm_a", "gnorm_b"))
    w_s = spatial_w.reshape(G_A, SGU_BLOCK, SGU_BLOCK)
    b_t = spatial_b.reshape(G_A, SGU_BLOCK).T

    shard2d = {k: wts[k].reshape(wts[k].shape[1:]) for k in _BIG}
    (first_b,) = cast_bf16([shard2d[_BIG[0]]], "cast_first")
    shard_b = {_BIG[0]: first_b}
    full = {}

    landing, rows_done = {}, {}

    def gathering(pieces, fn, *args, **kw):
        pieces = [p if isinstance(p, tuple) else (p, None) for p in pieces]
        comm = merge_comms([comm_all_gather([shard_b[k]], rows, [landing[k]] if k in landing else None)
                            for k, rows in pieces])
        out, got = fn(*args, comm=comm, **kw)
        for (k, rows), (g,) in zip(pieces, split_results(comm, got)):
            landing[k] = g
            rows_done[k] = rows_done.get(k, 0) + (rows[1] if rows is not None else shard_b[k].shape[0])
            if rows_done[k] == shard_b[k].shape[0]:
                full[k] = g if k in _COL_SHARDED else g.reshape(-1, g.shape[2])
        return out

    shard_b.update(zip(_BIG[1:], gathering((_BIG[0],), cast_bf16, [shard2d[k] for k in _BIG[1:]], "cast_rest")))
    n1 = rms_fwd(xs, g1, "f_n1")
    a1, hsw1 = gathering(("ffn1_w_out",), mm_swiglu_g, n1, full["ffn1_w_in"], "f_a1")
    h1 = gathering(("w_mix_in",), mm_nn, hsw1, full["ffn1_w_out"], F32, "f_h1", scale=0.5, res=xs)
    n2 = gathering(("w_cq",), rms_fwd, h1, gmix, "f_n2")
    z = gathering(("w_mix_out", "w_co"), mm_nn_g, n2, full["w_mix_in"], F32, "f_z")
    ya = gathering((("w_ckv", (0, 512)),), sgu_fwd, z, ln_g, ln_b, w_s, b_t, "f_sgu")
    qkv = split_heads(z, "f_qkv")
    yb = gathering((("w_ckv", (512, 1536)), ("ffn2_w_in", (0, 512))), sb_fwd, qkv, "f_sb")
    ycat = gathering((("ffn2_w_in", (512, 128)),), rmscat_fwd, ya, yb, ga, gb, "f_ycat")
    h2 = gathering((("ffn2_w_in", (640, 256)),), mm_nn, ycat, full["w_mix_out"], F32, "f_h2", res=h1)
    n3 = gathering((("ffn2_w_in", (896, 128)),), rms_fwd, h2, gcross, "f_n3")
    memn = rms_fwd(mems, gmem, "f_memn")
    qc = gathering((("ffn2_w_in", (1024, 256)),), mm_nn, n3, full["w_cq"], BF16, "f_qc", scale=X_DH ** -0.5)
    kv = gathering((("ffn2_w_in", (1280, 128)),), mm_nn_g, memn, full["w_ckv"], BF16, "f_kv")
    o = gathering((("ffn2_w_in", (1408, 128)),), xattn_fwd, qc, kv, "f_xattn")
    h3 = gathering((("ffn2_w_in", (1536, 256)),), mm_nn, o, full["w_co"], F32, "f_h3", res=h2)
    n4 = gathering((("ffn2_w_in", (1792, 256)),), rms_fwd, h3, g2, "f_n4")
    a2, hsw2 = gathering(("ffn2_w_out",), mm_swiglu_g, n4, full["ffn2_w_in"], "f_a2")
    h4 = mm_nn(hsw2, full["ffn2_w_out"], F32, "f_h4", scale=0.5, res=h3)

    grads, parts, sums, recv = {}, {}, {}, {}
    core = lax.axis_index("c").astype(jnp.int32).reshape(1)

    def partial_of(k, g):
        grads[k] = g
        parts[k] = g if g.ndim == 3 else g.reshape(N_DEV, -1, g.shape[1])

    def reducing(pairs, chips, fn, *args, also=None, **kw):
        def piece(p):
            if isinstance(p, dict):
                return p
            k, rows = p if isinstance(p, tuple) else (p, None)
            return dict(sums=k, rows=rows, to=k)

        chips = [piece(p) for p in chips]
        comms = [comm_pairs([parts[k] for k in pairs])] if pairs else []
        comms += [comm_chips([sums[p["sums"]]], p["rows"], [recv[p["to"]]] if p["to"] in recv else None,
                             p.get("from_row"), p.get("out_rows")) for p in chips]
        comm = merge_comms(comms + ([also] if also is not None else []))
        out, got = fn(*args, comm=comm, **kw)
        got = split_results(comm, got)
        if pairs:
            for k, r in zip(pairs, got.pop(0)):
                sums[k] = pair_sum(parts[k], r, core, f"pair_sum_{k}")
        for p, (r,) in zip(chips, got):
            recv[p["to"]] = r
        return out if also is None else (out, got[-1])

    loss_part, dh4, df2, grads["final_norm"] = loss_head(h4, tgt, gfin, "loss_head")
    partial_of("ffn2_w_out", mm_tn(hsw2, df2, "b_ffn2_dwout"))
    da2 = reducing(("ffn2_w_out",), (), mm_swiglu_bwd, df2, full["ffn2_w_out"], a2, "b_ffn2_da")
    partial_of("ffn2_w_in", reducing((), ("ffn2_w_out",), mm_tn_g, n4, da2, N_DEV, "b_ffn2_dwin"))
    dn4 = reducing(("ffn2_w_in",), (), mm_nt_g, da2, full["ffn2_w_in"], "b_ffn2_dn")
    dh3, dh3b, grads["ffn2_norm"] = rms_bwd(dn4, h3, g2, dh4, 1.0, "b_n4")

    partial_of("w_co", mm_tn(o, dh3b, "b_dwco"))
    do = reducing(("w_co",), (("ffn2_w_in", (0, 256)),), mm_nt, dh3b, full["w_co"], BF16, "b_do")
    dqp, dkv = reducing((), (("ffn2_w_in", (256, 256)),), xattn_bwd, qc, kv, do, "b_xattn")
    partial_of("w_cq", mm_tn(n3, dqp, "b_dwcq"))
    dn3 = reducing(("w_cq",), (("ffn2_w_in", (512, 256)),), mm_nt, dqp, full["w_cq"], F32, "b_dn3")
    partial_of("w_ckv", mm_tn_g(memn, dkv, N_DEV, "b_dwckv"))
    dmemn = reducing(("w_ckv",), (("ffn2_w_in", (768, 128)),), mm_nt_g, dkv, full["w_ckv"], "b_dmemn")
    _, _, grads["mem_norm"] = rms_bwd(dmemn, mems, gmem, None, 1.0, "b_memn")
    dh2, dh2b, grads["cross_norm"] = rms_bwd(dn3, h2, gcross, dh3, 1.0, "b_n3")

    partial_of("w_mix_out", mm_tn(ycat, dh2b, "b_dwmixout"))
    dycat = reducing(("w_mix_out",), (("ffn2_w_in", (896, 256)),), mm_nt, dh2b, full["w_mix_out"], F32, "b_dycat")
    dya, dyb, grads["gnorm_a"], grads["gnorm_b"] = rmscat_bwd(dycat, ya, yb, ga, gb, "b_ycat")
    dza, grads["ln_v_gain"], grads["ln_v_bias"], grads["spatial_w"], grads["spatial_b"] = reducing(
        (), (("ffn2_w_in", (1152, 384)),), sgu_bwd, z, dya, ln_g, ln_b, w_s, b_t, "b_sgu")
    dq, dk, dv = reducing((), (("ffn2_w_in", (1536, 512)), "w_co", "w_cq", ("w_ckv", (0, 1024))), sb_bwd, qkv, yb,
                          dyb, "b_sb")
    dz = join_dz(dza, dq, dk, dv, "b_dz")
    partial_of("w_mix_in", reducing((), (("w_ckv", (1024, 1024)),), mm_tn_g, n2, dz, N_DEV, "b_dwmixin"))
    dn2 = reducing(("w_mix_in",), ("w_mix_out",), mm_nt_g, dz, full["w_mix_in"], "b_dn2")
    dh1, dh1b, grads["mix_norm"] = reducing((), (("w_mix_in", (0, 512)),), rms_bwd, dn2, h1, gmix, dh2, 0.5, "b_n2")

    pad = jnp.zeros((_SMALL_PAD, 128), F32)
    small_early = jnp.concatenate([_rows128(grads[k]) for k in _SMALL[1:]] + [pad], axis=0)
    g_w1out, (early_sibling,) = reducing((), (("w_mix_in", (512, 1024)),), mm_tn, hsw1, dh1b, "b_ffn1_dwout",
                                         also=comm_pairs([small_early]))
    partial_of("ffn1_w_out", g_w1out)
    early_pair = add2(small_early, early_sibling, "pair_sum_small_early")
    da1, (early_all,) = reducing(("ffn1_w_out",), (("w_mix_in", (1536, 512)),), mm_swiglu_bwd, dh1b,
                                 full["ffn1_w_out"], a1, "b_ffn1_da", also=comm_chips([early_pair]))
    half = D // 2
    partial_of("ffn1_w_in_a", reducing((), (("ffn1_w_out", (0, 352)),), mm_tn_g, n1, da1, N_DEV, "b_ffn1_dwin_a",
                                       cols=(0, half)))
    partial_of("ffn1_w_in_b", reducing(("ffn1_w_in_a",), (("ffn1_w_out", (352, 352)),), mm_tn_g, n1, da1, N_DEV,
                                       "b_ffn1_dwin_b", cols=(half, half)))
    dn1 = reducing(("ffn1_w_in_b",), (dict(sums="ffn1_w_in_a", rows=(0, half), to="ffn1_w_in", out_rows=D),),
                   mm_nt_g, da1, full["ffn1_w_in"], "b_ffn1_dn")
    (dx, _, grads["ffn1_norm"]) = reducing((), (dict(sums="ffn1_w_in_b", rows=(half, half), from_row=0, to="ffn1_w_in"),),
                                           rms_bwd, dn1, xs, g1, dh1, 1.0, "b_n1")

    out_g, out_d, out_m, out_v = {}, {}, {}, {}
    for k in _BIG:
        res = adamw(recv[k], shard2d[k], mom[k].reshape(shard2d[k].shape), var[k].reshape(shard2d[k].shape), f"adamw_{k}")
        out_g[k], out_d[k], out_m[k], out_v[k] = (t.reshape(wts[k].shape) for t in res)

    small_late = _rows128(grads[_SMALL[0]])
    (late_sibling,) = run_comm(comm_pairs([small_late]), "comm_pairs_small_late")
    late_pair = add2(small_late, late_sibling, "pair_sum_small_late")
    (late_all,) = run_comm(comm_chips([late_pair]), "comm_chips_small_late")
    small_all = jnp.concatenate([late_all, early_all], axis=1)
    pack = lambda d: jnp.concatenate([_rows128(d[k]) for k in _SMALL] + [pad], axis=0)
    res = adamw(small_all, pack(wts), pack(mom), pack(var), "adamw_small")
    row = 0
    for k in _SMALL:
        nrow = wts[k].size // 128
        for dst, t in zip((out_g, out_d, out_m, out_v), res):
            dst[k] = t[row:row + nrow].reshape(wts[k].shape)
        row += nrow

    loss = lax.psum(loss_part[0, 0], ("x", "y", "c"))
    grad_x = dx.reshape(x.shape)
    return (loss, grad_x, *[out_g[k] for k in _ORDER], *[out_d[k] for k in _ORDER],
            *[out_m[k] for k in _ORDER], *[out_v[k] for k in _ORDER])
```

```python
import functools
import math

import jax
import jax.numpy as jnp
from jax import lax
from jax.experimental import pallas as pl
from jax.experimental.pallas import tpu as pltpu

F32 = jnp.float32
BF16 = jnp.bfloat16

N_DEV = 8
D_MODEL = 2048
D_FF = 5632
W_A = 1024
G_A = 8
GA_DIM = 128
SGU_BLOCK = 128
CHUNK = 64
H_B = 8
DH_B = 128
Q_BLOCK = 128
X_HEADS = 4
X_DH = 512
N_MEM = 256
EPS = 1e-6

ADAM_LR = 0.001
ADAM_B1 = 0.9
ADAM_B2 = 0.999
ADAM_EPS = 1e-08
ADAM_WD = 0.01
ADAM_STEP = 10

VMEM_LIMIT = 56 * 2**20
ROW_TILE = 256

MESH = pl.DeviceIdType.MESH
ANY = pl.BlockSpec(memory_space=pl.ANY)

_NT = (((1,), (1,)), ((), ()))
_TN = (((0,), (0,)), ((), ()))


def _params(*sem, collective_id=None):
    return pltpu.CompilerParams(dimension_semantics=sem, vmem_limit_bytes=VMEM_LIMIT, collective_id=collective_id)


_COLLECTIVE_ID = {frozenset({"sibling"}): 0, frozenset({"chips"}): 1, frozenset({"sibling", "chips"}): 2}


def _handshake(partners):
    x, y, c = lax.axis_index("x"), lax.axis_index("y"), lax.axis_index("c")
    peers = [(x, y, 1 - c)] if "sibling" in partners else []
    if "chips" in partners:
        peers += [(1 - x, y, c), (x, 1 - y, c), (1 - x, 1 - y, c)]
    barrier = pltpu.get_barrier_semaphore()
    for peer in peers:
        pl.semaphore_signal(barrier, inc=1, device_id=peer, device_id_type=MESH)
    pl.semaphore_wait(barrier, len(peers))


def _zeros(ref):
    return jnp.zeros(ref.shape, ref.dtype)


def _pcall(comm, body, *, name, grid, in_specs, out_specs, out_shape, compiler_params, scratch_shapes=()):
    if comm is None:
        return pl.pallas_call(body, name=name, grid=grid, in_specs=in_specs, out_specs=out_specs, out_shape=out_shape,
                              scratch_shapes=list(scratch_shapes), compiler_params=compiler_params)
    multi = isinstance(out_shape, (list, tuple))
    out_shapes = list(out_shape) if multi else [out_shape]
    out_specs_l = list(out_specs) if multi else [out_specs]
    n_in, n_out, n_scr = len(in_specs), len(out_shapes), len(scratch_shapes)
    n_cin, n_cout = len(comm.arrays), len(comm.out_shapes)

    def with_comm(*refs):
        ins, refs = refs[:n_in], refs[n_in:]
        cins, refs = refs[:n_cin], refs[n_cin:]
        outs, refs = refs[:n_out], refs[n_out:]
        couts, refs = refs[:n_cout], refs[n_cout:]
        scr, sems = refs[:n_scr], refs[n_scr:]
        first = functools.reduce(jnp.logical_and, [pl.program_id(a) == 0 for a in range(len(grid))])
        last = functools.reduce(jnp.logical_and, [pl.program_id(a) == grid[a] - 1 for a in range(len(grid))])
        @pl.when(first)
        def _():
            _handshake(comm.partners)
            comm.start(cins, couts, sems)

        body(*ins, *outs, *scr)
        pl.when(last)(lambda: comm.finish(cins, couts, sems))

    call = pl.pallas_call(
        with_comm, name=name, grid=grid, in_specs=list(in_specs) + [ANY] * n_cin,
        out_specs=out_specs_l + [ANY] * n_cout, out_shape=out_shapes + comm.out_shapes,
        scratch_shapes=list(scratch_shapes) + comm.sem_shapes(),
        compiler_params=_params(*(("arbitrary",) * len(grid)), collective_id=_COLLECTIVE_ID[comm.partners]),
        input_output_aliases={n_in + i: n_out + j for i, j in comm.aliases.items()})

    def run(*args):
        res = call(*args, *comm.arrays)
        main = res[:n_out]
        return (list(main) if multi else main[0]), list(res[n_out:])

    return run


def _dot(a, b):
    return jnp.dot(a, b, preferred_element_type=F32)


def _dot_nt(a, b):
    return lax.dot_general(a, b, _NT, preferred_element_type=F32)


def _dot_tn(a, b):
    return lax.dot_general(a, b, _TN, preferred_element_type=F32)


def mm_nn_g(a, bg, out_dtype, name, tm=512, comm=None):
    M, K = a.shape
    G, _, n = bg.shape
    tm = min(tm, M)

    def body(a_ref, b_ref, o_ref):
        o_ref[...] = _dot(a_ref[...], b_ref[...]).astype(o_ref.dtype)

    return _pcall(
        comm, body, name=name, grid=(G, M // tm),
        in_specs=[pl.BlockSpec((tm, K), lambda g, m: (m, 0)),
                  pl.BlockSpec((None, K, n), lambda g, m: (g, 0, 0))],
        out_specs=pl.BlockSpec((tm, n), lambda g, m: (m, g)),
        out_shape=jax.ShapeDtypeStruct((M, G * n), out_dtype),
        compiler_params=_params("parallel", "parallel"),
    )(a, bg)


def mm_swiglu_g(a, bg, name, tm=512, comm=None):
    M, K = a.shape
    G, _, n = bg.shape
    half = G // 2
    tm = min(tm, M)

    def body(a_ref, bgate_ref, bup_ref, gu_ref, h_ref):
        av = a_ref[...]
        gate = _dot(av, bgate_ref[...])
        up = _dot(av, bup_ref[...])
        gu_ref[0] = gate.astype(BF16)
        gu_ref[1] = up.astype(BF16)
        h_ref[...] = (gate * _sigmoid(gate) * up).astype(BF16)

    return _pcall(
        comm, body, name=name, grid=(half, M // tm),
        in_specs=[pl.BlockSpec((tm, K), lambda p, m: (m, 0)),
                  pl.BlockSpec((None, K, n), lambda p, m: (p, 0, 0)),
                  pl.BlockSpec((None, K, n), lambda p, m: (p + half, 0, 0))],
        out_specs=[pl.BlockSpec((2, tm, n), lambda p, m: (0, m, p)), pl.BlockSpec((tm, n), lambda p, m: (m, p))],
        out_shape=[jax.ShapeDtypeStruct((2, M, half * n), BF16), jax.ShapeDtypeStruct((M, half * n), BF16)],
        compiler_params=_params("parallel", "parallel"),
    )(a, bg, bg)


def mm_swiglu_bwd(dy, w_out, gate_up, name, tk=512, comm=None):
    M, N = dy.shape
    F = w_out.shape[0]

    def body(dy_ref, w_ref, gu_ref, o_ref):
        dh = _dot_nt(dy_ref[...], w_ref[...])
        gt = gu_ref[0].astype(F32)
        up = gu_ref[1].astype(F32)
        sg = _sigmoid(gt)
        o_ref[0] = (dh * up * (sg * (1.0 + gt * (1.0 - sg)))).astype(BF16)
        o_ref[1] = (dh * (gt * sg)).astype(BF16)

    planes = pl.BlockSpec((2, M, tk), lambda k: (0, 0, k))
    return _pcall(
        comm, body, name=name, grid=(F // tk,),
        in_specs=[pl.BlockSpec((M, N), lambda k: (0, 0)), pl.BlockSpec((tk, N), lambda k: (k, 0)), planes],
        out_specs=planes,
        out_shape=jax.ShapeDtypeStruct((2, M, F), BF16),
        compiler_params=_params("parallel"),
    )(dy, w_out, gate_up)


def _shard_cols_spec(dy, G, rows, index):
    if dy.ndim == 2:
        n = dy.shape[1] // G
        return pl.BlockSpec((rows, n), lambda *ids: index(*ids)), n
    half = G // 2
    n = dy.shape[2] // half

    def planes(*ids):
        r, g = index(*ids)
        return g // half, r, g % half

    return pl.BlockSpec((None, rows, n), planes), n


def mm_nn(a, b, out_dtype, name, tm=512, tn=1024, scale=1.0, res=None, comm=None):
    M, K = a.shape
    _, N = b.shape
    tm, tn = min(tm, M), min(tn, N)

    def body(*refs):
        if res is None:
            a_ref, b_ref, o_ref = refs
            acc = _dot(a_ref[...], b_ref[...])
            o_ref[...] = (acc * scale if scale != 1.0 else acc).astype(o_ref.dtype)
        else:
            a_ref, b_ref, r_ref, o_ref = refs
            o_ref[...] = (r_ref[...] + scale * _dot(a_ref[...], b_ref[...])).astype(o_ref.dtype)

    in_specs = [pl.BlockSpec((tm, K), lambda n, m: (m, 0)),
                pl.BlockSpec((K, tn), lambda n, m: (0, n))]
    args = [a, b]
    if res is not None:
        in_specs.append(pl.BlockSpec((tm, tn), lambda n, m: (m, n)))
        args.append(res)
    return _pcall(
        comm, body, name=name, grid=(N // tn, M // tm),
        in_specs=in_specs,
        out_specs=pl.BlockSpec((tm, tn), lambda n, m: (m, n)),
        out_shape=jax.ShapeDtypeStruct((M, N), out_dtype),
        compiler_params=_params("parallel", "parallel"),
    )(*args)


def mm_nt_g(dy, bg, name, tm=512, comm=None):
    M = dy.shape[-2]
    G, K, n = bg.shape
    tm = min(tm, M)
    dy_spec, _ = _shard_cols_spec(dy, G, tm, lambda m, g: (m, g))

    def body(dy_ref, b_ref, o_ref):
        part = _dot_nt(dy_ref[...], b_ref[...])

        @pl.when(pl.program_id(1) == 0)
        def _():
            o_ref[...] = part

        @pl.when(pl.program_id(1) > 0)
        def _():
            o_ref[...] += part

    return _pcall(
        comm, body, name=name, grid=(M // tm, G),
        in_specs=[dy_spec, pl.BlockSpec((None, K, n), lambda m, g: (g, 0, 0))],
        out_specs=pl.BlockSpec((tm, K), lambda m, g: (m, 0)),
        out_shape=jax.ShapeDtypeStruct((M, K), F32),
        compiler_params=_params("parallel", "arbitrary"),
    )(dy, bg)


def mm_nt(dy, b, out_dtype, name, tk=512, comm=None):
    M, N = dy.shape
    K, _ = b.shape

    def body(dy_ref, b_ref, o_ref):
        o_ref[...] = _dot_nt(dy_ref[...], b_ref[...]).astype(o_ref.dtype)

    return _pcall(
        comm, body, name=name, grid=(K // tk,),
        in_specs=[pl.BlockSpec((M, N), lambda k: (0, 0)),
                  pl.BlockSpec((tk, N), lambda k: (k, 0))],
        out_specs=pl.BlockSpec((M, tk), lambda k: (0, k)),
        out_shape=jax.ShapeDtypeStruct((M, K), out_dtype),
        compiler_params=_params("parallel"),
    )(dy, b)


def mm_tn_g(x, dy, G, name, tk=512, cols=None, comm=None):
    M, K = x.shape
    k0, K = cols if cols is not None else (0, K)
    dy_spec, n = _shard_cols_spec(dy, G, M, lambda g, k: (0, g))

    def body(x_ref, dy_ref, o_ref):
        o_ref[...] = _dot_tn(x_ref[...], dy_ref[...]).astype(o_ref.dtype)

    return _pcall(
        comm, body, name=name, grid=(G, K // tk),
        in_specs=[pl.BlockSpec((M, tk), lambda g, k: (0, k + k0 // tk)), dy_spec],
        out_specs=pl.BlockSpec((None, tk, n), lambda g, k: (g, k, 0)),
        out_shape=jax.ShapeDtypeStruct((G, K, n), BF16),
        compiler_params=_params("parallel", "parallel"),
    )(x, dy)


def mm_tn(x, dy, name, tk=512, comm=None):
    M, K = x.shape
    _, N = dy.shape

    def body(x_ref, dy_ref, o_ref):
        o_ref[...] = _dot_tn(x_ref[...], dy_ref[...]).astype(o_ref.dtype)

    return _pcall(
        comm, body, name=name, grid=(K // tk,),
        in_specs=[pl.BlockSpec((M, tk), lambda k: (0, k)),
                  pl.BlockSpec((M, N), lambda k: (0, 0))],
        out_specs=pl.BlockSpec((tk, N), lambda k: (k, 0)),
        out_shape=jax.ShapeDtypeStruct((K, N), BF16),
        compiler_params=_params("parallel"),
    )(x, dy)


def _rstd(x):
    return lax.rsqrt(jnp.mean(x * x, axis=-1, keepdims=True) + EPS)


def _rms_bwd(dn, xhat, r, g):
    dxhat = dn * g
    return r * (dxhat - xhat * jnp.mean(dxhat * xhat, axis=-1, keepdims=True))


def _row_spec(tr, width, col=0):
    return pl.BlockSpec((tr, width), lambda i: (i, col))


def _vec_spec(width):
    return pl.BlockSpec((1, width), lambda i: (0, 0))


def _heads_spec(tr):
    return pl.BlockSpec((H_B, tr, DH_B), lambda i: (0, i, 0))


def _heads_to_cols(ref):
    return jnp.concatenate([ref[h] for h in range(H_B)], axis=1)


def split_heads(z, name, comm=None):
    S = z.shape[0]
    tr = min(ROW_TILE, S)
    width = H_B * DH_B
    first = 2 * W_A // width

    def body(q_ref, k_ref, v_ref, o_ref):
        for p, (ref, scale) in enumerate(((q_ref, DH_B ** -0.5), (k_ref, 1.0), (v_ref, 1.0))):
            for h in range(H_B):
                cols = ref[:, h * DH_B:(h + 1) * DH_B]
                o_ref[p * H_B + h] = (cols * scale if scale != 1.0 else cols).astype(BF16)

    return _pcall(
        comm, body, name=name, grid=(S // tr,),
        in_specs=[_row_spec(tr, width, first), _row_spec(tr, width, first + 1), _row_spec(tr, width, first + 2)],
        out_specs=pl.BlockSpec((3 * H_B, tr, DH_B), lambda i: (0, i, 0)),
        out_shape=jax.ShapeDtypeStruct((3 * H_B, S, DH_B), BF16),
        compiler_params=_params("parallel"),
    )(z, z, z)


def join_dz(dza, dq, dk, dv, name, comm=None):
    S, wa = dza.shape
    tr = min(ROW_TILE, S)
    width = H_B * DH_B

    def body(dza_ref, dq_ref, dk_ref, dv_ref, o_ref):
        o_ref[:, :wa] = dza_ref[...]
        for p, ref in enumerate((dq_ref, dk_ref, dv_ref)):
            for h in range(H_B):
                lo = wa + p * width + h * DH_B
                o_ref[:, lo:lo + DH_B] = ref[h]

    return _pcall(
        comm, body, name=name, grid=(S // tr,),
        in_specs=[_row_spec(tr, wa), _heads_spec(tr), _heads_spec(tr), _heads_spec(tr)],
        out_specs=_row_spec(tr, wa + 3 * width),
        out_shape=jax.ShapeDtypeStruct((S, wa + 3 * width), BF16),
        compiler_params=_params("parallel"),
    )(dza, dq, dk, dv)


def rms_fwd(x, g, name, comm=None):
    M, D = x.shape
    tr = min(ROW_TILE, M)

    def body(x_ref, g_ref, o_ref):
        xv = x_ref[...]
        o_ref[...] = (xv * _rstd(xv) * g_ref[...]).astype(o_ref.dtype)

    return _pcall(
        comm, body, name=name, grid=(M // tr,),
        in_specs=[_row_spec(tr, D), _vec_spec(D)],
        out_specs=_row_spec(tr, D),
        out_shape=jax.ShapeDtypeStruct((M, D), BF16),
        compiler_params=_params("parallel"),
    )(x, g)


def rms_bwd(dn, h, g, dres, copy_scale, name, comm=None):
    M, D = h.shape
    tr = min(ROW_TILE, M)
    has_res = dres is not None

    def body(*refs):
        if has_res:
            dn_ref, h_ref, g_ref, dres_ref, dh_ref, dhb_ref, dg_ref = refs
        else:
            dn_ref, h_ref, g_ref, dh_ref, dhb_ref, dg_ref = refs
        hv = h_ref[...]
        r = _rstd(hv)
        xhat = hv * r
        dn = dn_ref[...]
        part = jnp.sum(dn * xhat, axis=0, keepdims=True)

        @pl.when(pl.program_id(0) == 0)
        def _():
            dg_ref[...] = part

        @pl.when(pl.program_id(0) > 0)
        def _():
            dg_ref[...] += part

        dh = _rms_bwd(dn, xhat, r, g_ref[...])
        if has_res:
            dh = dh + dres_ref[...]
        dh_ref[...] = dh
        dhb_ref[...] = (dh * copy_scale if copy_scale != 1.0 else dh).astype(BF16)

    in_specs = [_row_spec(tr, D), _row_spec(tr, D), _vec_spec(D)]
    args = [dn, h, g]
    if has_res:
        in_specs.append(_row_spec(tr, D))
        args.append(dres)
    return _pcall(
        comm, body, name=name, grid=(M // tr,),
        in_specs=in_specs,
        out_specs=[_row_spec(tr, D), _row_spec(tr, D), _vec_spec(D)],
        out_shape=[jax.ShapeDtypeStruct((M, D), F32), jax.ShapeDtypeStruct((M, D), BF16),
                   jax.ShapeDtypeStruct((1, D), F32)],
        compiler_params=_params("arbitrary"),
    )(*args)


def _sigmoid(x):
    return 1.0 / (1.0 + jnp.exp(-x))


def rmscat_fwd(ya, yb, ga, gb, name, comm=None):
    M, W = ya.shape
    tr = min(ROW_TILE, M)

    def body(ya_ref, yb_ref, ga_ref, gb_ref, o_ref):
        a = ya_ref[...]
        b = _heads_to_cols(yb_ref)
        o_ref[:, :W] = (a * _rstd(a) * ga_ref[...]).astype(BF16)
        o_ref[:, W:] = (b * _rstd(b) * gb_ref[...]).astype(BF16)

    return _pcall(
        comm, body, name=name, grid=(M // tr,),
        in_specs=[_row_spec(tr, W), _heads_spec(tr), _vec_spec(W), _vec_spec(W)],
        out_specs=_row_spec(tr, 2 * W),
        out_shape=jax.ShapeDtypeStruct((M, 2 * W), BF16),
        compiler_params=_params("parallel"),
    )(ya, yb, ga, gb)


def rmscat_bwd(dycat, ya, yb, ga, gb, name, comm=None):
    M, W = ya.shape
    tr = min(ROW_TILE, M)

    def body(dc_ref, ya_ref, yb_ref, ga_ref, gb_ref, dya_ref, dyb_ref, dga_ref, dgb_ref):
        first = pl.program_id(0) == 0
        for by_head, y_ref, g_ref, dy_ref, dg_ref, lo in ((False, ya_ref, ga_ref, dya_ref, dga_ref, 0),
                                                          (True, yb_ref, gb_ref, dyb_ref, dgb_ref, W)):
            yv = _heads_to_cols(y_ref) if by_head else y_ref[...]
            r = _rstd(yv)
            xhat = yv * r
            dn = dc_ref[:, lo:lo + W]
            part = jnp.sum(dn * xhat, axis=0, keepdims=True)

            @pl.when(first)
            def _():
                dg_ref[...] = part

            @pl.when(jnp.logical_not(first))
            def _():
                dg_ref[...] += part

            dy = _rms_bwd(dn, xhat, r, g_ref[...])
            if by_head:
                for h in range(H_B):
                    dy_ref[h] = dy[:, h * DH_B:(h + 1) * DH_B]
            else:
                dy_ref[...] = dy

    return _pcall(
        comm, body, name=name, grid=(M // tr,),
        in_specs=[_row_spec(tr, 2 * W), _row_spec(tr, W), _heads_spec(tr), _vec_spec(W), _vec_spec(W)],
        out_specs=[_row_spec(tr, W), _heads_spec(tr), _vec_spec(W), _vec_spec(W)],
        out_shape=[jax.ShapeDtypeStruct((M, W), F32), jax.ShapeDtypeStruct((H_B, M, DH_B), F32),
                   jax.ShapeDtypeStruct((1, W), F32), jax.ShapeDtypeStruct((1, W), F32)],
        compiler_params=_params("arbitrary"),
    )(dycat, ya, yb, ga, gb)


def loss_head(h, target, g, name, comm=None):
    M, D = h.shape
    tr = min(ROW_TILE, M)

    def body(h_ref, t_ref, g_ref, loss_ref, dh_ref, dhb_ref, dg_ref):
        hv = h_ref[...]
        gv = g_ref[...]
        r = _rstd(hv)
        xhat = hv * r
        err = xhat * gv - t_ref[...]
        lsum = jnp.sum(jnp.sum(err * err, axis=1, keepdims=True), axis=0, keepdims=True) * (0.5 / D)
        dy = err * (1.0 / D)
        part = jnp.sum(dy * xhat, axis=0, keepdims=True)

        @pl.when(pl.program_id(0) == 0)
        def _():
            dg_ref[...] = part
            loss_ref[...] = _zeros(loss_ref) + lsum

        @pl.when(pl.program_id(0) > 0)
        def _():
            dg_ref[...] += part
            loss_ref[...] += lsum

        dh = _rms_bwd(dy, xhat, r, gv)
        dh_ref[...] = dh
        dhb_ref[...] = (0.5 * dh).astype(BF16)

    return _pcall(
        comm, body, name=name, grid=(M // tr,),
        in_specs=[_row_spec(tr, D), _row_spec(tr, D), _vec_spec(D)],
        out_specs=[pl.BlockSpec((8, 128), lambda i: (0, 0)), _row_spec(tr, D), _row_spec(tr, D), _vec_spec(D)],
        out_shape=[jax.ShapeDtypeStruct((8, 128), F32), jax.ShapeDtypeStruct((M, D), F32),
                   jax.ShapeDtypeStruct((M, D), BF16), jax.ShapeDtypeStruct((1, D), F32)],
        compiler_params=_params("arbitrary"),
    )(h, target, g)


_GELU_C = math.sqrt(2.0 / math.pi)


def _gelu(x):
    return 0.5 * x * (1.0 + jnp.tanh(_GELU_C * (x + 0.044715 * (x * x * x))))


def _gelu_grad(x):
    t = jnp.tanh(_GELU_C * (x + 0.044715 * (x * x * x)))
    return 0.5 * (1.0 + t) + 0.5 * x * (1.0 - t * t) * (_GELU_C * (1.0 + 3.0 * 0.044715 * (x * x)))


def _sgu_mask():
    t = lax.broadcasted_iota(jnp.int32, (SGU_BLOCK, SGU_BLOCK), 0) // CHUNK
    s = lax.broadcasted_iota(jnp.int32, (SGU_BLOCK, SGU_BLOCK), 1) // CHUNK
    return s <= t


def _layernorm_stats(v):
    mu = jnp.mean(v, axis=-1, keepdims=True)
    cen = v - mu
    rstd = lax.rsqrt(jnp.mean(cen * cen, axis=-1, keepdims=True) + EPS)
    return cen * rstd, rstd


def sgu_fwd(z, ln_g, ln_b, w_s, b_t, name, comm=None):
    S = z.shape[0]

    def body(zu_ref, zv_ref, lg_ref, lb_ref, w_ref, bt_ref, o_ref):
        mask = _sgu_mask()
        for g in range(G_A):
            cols = slice(g * GA_DIM, (g + 1) * GA_DIM)
            u = _gelu(zu_ref[:, cols])
            vhat, _ = _layernorm_stats(_gelu(zv_ref[:, cols]))
            vln = vhat * lg_ref[:, cols] + lb_ref[:, cols]
            w = jnp.where(mask, w_ref[g], 0.0).astype(BF16)
            mixed = _dot(w, vln.astype(BF16)) + bt_ref[:, g:g + 1]
            o_ref[:, cols] = u * mixed

    return _pcall(
        comm, body, name=name, grid=(S // SGU_BLOCK,),
        in_specs=[_row_spec(SGU_BLOCK, W_A, 0), _row_spec(SGU_BLOCK, W_A, 1), _vec_spec(W_A), _vec_spec(W_A),
                  pl.BlockSpec((G_A, SGU_BLOCK, SGU_BLOCK), lambda i: (0, 0, 0)),
                  pl.BlockSpec((SGU_BLOCK, G_A), lambda i: (0, 0))],
        out_specs=_row_spec(SGU_BLOCK, W_A),
        out_shape=jax.ShapeDtypeStruct((S, W_A), F32),
        compiler_params=_params("parallel"),
    )(z, z, ln_g, ln_b, w_s, b_t)


def sgu_bwd(z, dya, ln_g, ln_b, w_s, b_t, name, comm=None):
    S = z.shape[0]
    nblk = S // SGU_BLOCK

    def body(zu_ref, zv_ref, dy_ref, lg_ref, lb_ref, w_ref, bt_ref,
             dz_ref, dlg_ref, dlb_ref, dw_ref, db_ref, dmix_acc):
        step = pl.program_id(0)
        mask = _sgu_mask()

        @pl.when(step == 0)
        def _():
            dlg_ref[...] = _zeros(dlg_ref)
            dlb_ref[...] = _zeros(dlb_ref)
            dw_ref[...] = _zeros(dw_ref)
            dmix_acc[...] = _zeros(dmix_acc)

        for g in range(G_A):
            cols = slice(g * GA_DIM, (g + 1) * GA_DIM)
            zu = zu_ref[:, cols]
            zv = zv_ref[:, cols]
            u = _gelu(zu)
            vhat, rstd = _layernorm_stats(_gelu(zv))
            lg = lg_ref[:, cols]
            vln = (vhat * lg + lb_ref[:, cols]).astype(BF16)
            w = jnp.where(mask, w_ref[g], 0.0)
            mixed = _dot(w.astype(BF16), vln) + bt_ref[:, g:g + 1]
            dy = dy_ref[:, cols]
            du = dy * mixed
            dmixed = dy * u
            dmixed_b = dmixed.astype(BF16)
            dmix_acc[g] += dmixed
            dw_ref[g] += jnp.where(mask, _dot_nt(dmixed_b, vln), 0.0)
            dvln = _dot(w.T.astype(BF16), dmixed_b)
            dlb_ref[:, cols] += jnp.sum(dvln, axis=0, keepdims=True)
            dlg_ref[:, cols] += jnp.sum(dvln * vhat, axis=0, keepdims=True)
            dvhat = dvln * lg
            dv = rstd * (dvhat - jnp.mean(dvhat, axis=-1, keepdims=True)
                         - vhat * jnp.mean(dvhat * vhat, axis=-1, keepdims=True))
            dz_ref[:, cols] = (du * _gelu_grad(zu)).astype(BF16)
            dz_ref[:, W_A + g * GA_DIM:W_A + (g + 1) * GA_DIM] = (dv * _gelu_grad(zv)).astype(BF16)

        @pl.when(step == nblk - 1)
        def _():
            for g in range(G_A):
                db_ref[g] = jnp.sum(dmix_acc[g], axis=1, keepdims=True)

    whole3 = lambda shape: pl.BlockSpec(shape, lambda i: (0, 0, 0))
    return _pcall(
        comm, body, name=name, grid=(nblk,),
        in_specs=[_row_spec(SGU_BLOCK, W_A, 0), _row_spec(SGU_BLOCK, W_A, 1), _row_spec(SGU_BLOCK, W_A),
                  _vec_spec(W_A), _vec_spec(W_A), whole3((G_A, SGU_BLOCK, SGU_BLOCK)),
                  pl.BlockSpec((SGU_BLOCK, G_A), lambda i: (0, 0))],
        out_specs=[_row_spec(SGU_BLOCK, 2 * W_A), _vec_spec(W_A), _vec_spec(W_A),
                   whole3((G_A, SGU_BLOCK, SGU_BLOCK)), whole3((G_A, SGU_BLOCK, 1))],
        out_shape=[jax.ShapeDtypeStruct((S, 2 * W_A), BF16), jax.ShapeDtypeStruct((1, W_A), F32),
                   jax.ShapeDtypeStruct((1, W_A), F32), jax.ShapeDtypeStruct((G_A, SGU_BLOCK, SGU_BLOCK), F32),
                   jax.ShapeDtypeStruct((G_A, SGU_BLOCK, 1), F32)],
        scratch_shapes=[pltpu.VMEM((G_A, SGU_BLOCK, SGU_BLOCK), F32)],
        compiler_params=_params("arbitrary"),
    )(z, z, dya, ln_g, ln_b, w_s, b_t)


def _log_sigmoid(z):
    return jnp.minimum(z, 0.0) - jnp.log(1.0 + jnp.exp(-jnp.abs(z)))


def _suffix_sum(x, upper):
    hi = x.astype(BF16)
    lo = (x - hi.astype(F32)).astype(BF16)
    return _dot(hi, upper) + _dot(lo, upper)


SB_ROWS = 2048
_SB_SUB = SB_ROWS // Q_BLOCK


def _sb_upper():
    row = lax.broadcasted_iota(jnp.int32, (Q_BLOCK, Q_BLOCK), 0)
    col = lax.broadcasted_iota(jnp.int32, (Q_BLOCK, Q_BLOCK), 1)
    return (row > col).astype(BF16)


def _sb_sweep(step, tile):
    for r in reversed(range(_SB_SUB)):
        tile(step * _SB_SUB + r, r * Q_BLOCK)

    def group(g, _):
        base = (step - 1 - g) * _SB_SUB
        for r in reversed(range(_SB_SUB)):
            tile(base + r, None)
        return 0

    lax.fori_loop(0, step, group, 0)


def _sb_causal(n):
    return lax.broadcasted_iota(jnp.int32, (n, Q_BLOCK), 1) < lax.broadcasted_iota(jnp.int32, (n, Q_BLOCK), 0)


def _sb_rows_spec():
    return pl.BlockSpec((None, SB_ROWS, DH_B), lambda h, i: (h, i, 0))


def _sb_head_spec(S, part=0):
    return pl.BlockSpec((None, S, DH_B), lambda h, i: (part * H_B + h, 0, 0))


def sb_fwd(qkv, name, comm=None):
    S = qkv.shape[1]

    def body(q_b, k_ref, v_ref, o_ref, c_l1m):
        step = pl.program_id(1)
        o_ref[...] = _zeros(o_ref)
        c_l1m[...] = _zeros(c_l1m)
        upper = _sb_upper()

        def tile(j, row0):
            rq = slice(row0 or 0, SB_ROWS)
            causal = None if row0 is None else _sb_causal(SB_ROWS - row0)
            rows = pl.ds(pl.multiple_of(j * Q_BLOCK, Q_BLOCK), Q_BLOCK)
            zz = _dot_nt(q_b[rq, :], k_ref[rows, :])
            lb = _log_sigmoid(zz)
            l1m = lb - zz
            if causal is not None:
                l1m = jnp.where(causal, l1m, 0.0)
            a = jnp.exp(lb + _suffix_sum(l1m, upper) + c_l1m[rq, :])
            if causal is not None:
                a = jnp.where(causal, a, 0.0)
            o_ref[rq, :] += _dot(a.astype(BF16), v_ref[rows, :])
            c_l1m[rq, :] += jnp.sum(l1m, axis=1, keepdims=True)

        _sb_sweep(step, tile)

    return _pcall(
        comm, body, name=name, grid=(H_B, S // SB_ROWS),
        in_specs=[_sb_rows_spec(), _sb_head_spec(S, 1), _sb_head_spec(S, 2)],
        out_specs=_sb_rows_spec(),
        out_shape=jax.ShapeDtypeStruct((H_B, S, DH_B), F32),
        scratch_shapes=[pltpu.VMEM((SB_ROWS, 1), F32)],
        compiler_params=_params("parallel", "parallel"),
    )(qkv, qkv, qkv)


def sb_bwd(qkv, out, dout, name, comm=None):
    S = qkv.shape[1]
    nstep = S // SB_ROWS
    scale = DH_B ** -0.5

    def body(q_b, k_ref, v_ref, o_ref, do_ref, dq_ref, dk_ref, dv_ref,
             dq_acc, dkt_acc, dvt_acc, do_b, qt_b, dot_b, g_left, c_l1m):
        step = pl.program_id(1)

        @pl.when(step == 0)
        def _():
            dkt_acc[...] = _zeros(dkt_acc)
            dvt_acc[...] = _zeros(dvt_acc)

        do_b[...] = do_ref[...].astype(BF16)
        qt_b[...] = q_b[...].astype(F32).T.astype(BF16)
        dot_b[...] = do_ref[...].T.astype(BF16)
        g_left[...] = jnp.sum(do_b[...].astype(F32) * o_ref[...], axis=1, keepdims=True)
        dq_acc[...] = _zeros(dq_acc)
        c_l1m[...] = _zeros(c_l1m)
        upper = _sb_upper()

        def tile(j, row0):
            rq = slice(row0 or 0, SB_ROWS)
            causal = None if row0 is None else _sb_causal(SB_ROWS - row0)
            rows = pl.ds(pl.multiple_of(j * Q_BLOCK, Q_BLOCK), Q_BLOCK)
            q, do_t = q_b[rq, :], do_b[rq, :]
            k_j = k_ref[rows, :]
            zz = _dot_nt(q, k_j)
            lb = _log_sigmoid(zz)
            l1m = lb - zz
            if causal is not None:
                l1m = jnp.where(causal, l1m, 0.0)
            a = jnp.exp(lb + _suffix_sum(l1m, upper) + c_l1m[rq, :])
            if causal is not None:
                a = jnp.where(causal, a, 0.0)
            a_b = a.astype(BF16)
            dvt_acc[:, rows] += _dot(dot_b[:, rq], a_b)
            gmat = a_b.astype(F32) * _dot_nt(do_t, v_ref[rows, :])
            before = g_left[rq, :] - gmat - _suffix_sum(gmat, upper)
            sig = jnp.exp(lb)
            dz = gmat * (1.0 - sig) - sig * before
            if causal is not None:
                dz = jnp.where(causal, dz, 0.0)
            dz_b = dz.astype(BF16)
            dkt_acc[:, rows] += _dot(qt_b[:, rq], dz_b)
            dq_acc[rq, :] += _dot(dz_b, k_j)
            c_l1m[rq, :] += jnp.sum(l1m, axis=1, keepdims=True)
            g_left[rq, :] -= jnp.sum(gmat, axis=1, keepdims=True)

        _sb_sweep(step, tile)
        dq_ref[...] = (dq_acc[...] * scale).astype(BF16)

        @pl.when(step == nstep - 1)
        def _():
            dk_ref[...] = dkt_acc[...].T.astype(BF16)
            dv_ref[...] = dvt_acc[...].T.astype(BF16)

    out_sds = jax.ShapeDtypeStruct((H_B, S, DH_B), BF16)
    return _pcall(
        comm, body, name=name, grid=(H_B, nstep),
        in_specs=[_sb_rows_spec(), _sb_head_spec(S, 1), _sb_head_spec(S, 2), _sb_rows_spec(), _sb_rows_spec()],
        out_specs=[_sb_rows_spec(), _sb_head_spec(S), _sb_head_spec(S)],
        out_shape=[out_sds, out_sds, out_sds],
        scratch_shapes=[pltpu.VMEM((SB_ROWS, DH_B), F32)] + [pltpu.VMEM((DH_B, S), F32)] * 2
        + [pltpu.VMEM((SB_ROWS, DH_B), BF16)] + [pltpu.VMEM((DH_B, SB_ROWS), BF16)] * 2
        + [pltpu.VMEM((SB_ROWS, 1), F32)] * 2,
        compiler_params=_params("parallel", "arbitrary"),
    )(qkv, qkv, qkv, out, dout)


def _softmax(s):
    e = jnp.exp(s - jnp.max(s, axis=-1, keepdims=True))
    return e / jnp.sum(e, axis=-1, keepdims=True)


def xattn_fwd(qc, kv, name, comm=None):
    S, D = qc.shape
    tr = min(ROW_TILE, S)

    def body(q_ref, kv_ref, o_ref):
        for h in range(X_HEADS):
            cols = slice(h * X_DH, (h + 1) * X_DH)
            p = _softmax(_dot_nt(q_ref[:, cols], kv_ref[:, cols]))
            o_ref[:, cols] = _dot(p.astype(BF16), kv_ref[:, D + h * X_DH:D + (h + 1) * X_DH]).astype(BF16)

    return _pcall(
        comm, body, name=name, grid=(S // tr,),
        in_specs=[_row_spec(tr, D), pl.BlockSpec((N_MEM, 2 * D), lambda i: (0, 0))],
        out_specs=_row_spec(tr, D),
        out_shape=jax.ShapeDtypeStruct((S, D), BF16),
        compiler_params=_params("parallel"),
    )(qc, kv)


def xattn_bwd(qc, kv, do, name, comm=None):
    S, D = qc.shape
    tr = min(ROW_TILE, S)
    nstep = S // tr
    scale = X_DH ** -0.5

    def body(q_ref, kv_ref, do_ref, dq_ref, dkv_ref, acc):
        step = pl.program_id(0)

        @pl.when(step == 0)
        def _():
            acc[...] = _zeros(acc)

        for h in range(X_HEADS):
            cols = slice(h * X_DH, (h + 1) * X_DH)
            vcols = slice(D + h * X_DH, D + (h + 1) * X_DH)
            q = q_ref[:, cols]
            k = kv_ref[:, cols]
            do_h = do_ref[:, cols]
            p = _softmax(_dot_nt(q, k))
            dp = _dot_nt(do_h, kv_ref[:, vcols])
            acc[:, vcols] += _dot_tn(p.astype(BF16), do_h)
            ds = (p * (dp - jnp.sum(p * dp, axis=-1, keepdims=True))).astype(BF16)
            dq_ref[:, cols] = (_dot(ds, k) * scale).astype(BF16)
            acc[:, cols] += _dot_tn(ds, q)

        @pl.when(step == nstep - 1)
        def _():
            dkv_ref[...] = acc[...].astype(BF16)

    whole = pl.BlockSpec((N_MEM, 2 * D), lambda i: (0, 0))
    return _pcall(
        comm, body, name=name, grid=(nstep,),
        in_specs=[_row_spec(tr, D), whole, _row_spec(tr, D)],
        out_specs=[_row_spec(tr, D), whole],
        out_shape=[jax.ShapeDtypeStruct((S, D), BF16), jax.ShapeDtypeStruct((N_MEM, 2 * D), BF16)],
        scratch_shapes=[pltpu.VMEM((N_MEM, 2 * D), F32)],
        compiler_params=_params("arbitrary"),
    )(qc, kv, do)


def _row_tile(rows, cap=128):
    return max(t for t in range(16, cap + 1, 16) if rows % t == 0)


def cast_bf16(ws, name, steps=4, comm=None):
    n = len(ws)

    def body(*refs):
        for i in range(n):
            refs[n + i][...] = refs[i][...].astype(BF16)

    specs = [_row_spec(w.shape[0] // steps, w.shape[1]) for w in ws]
    return _pcall(
        comm, body, name=name, grid=(steps,), in_specs=specs, out_specs=specs,
        out_shape=[jax.ShapeDtypeStruct(w.shape, BF16) for w in ws],
        compiler_params=_params("parallel"),
    )(*ws)


def adamw(parts, w, m, v, name, comm=None):
    R, C = w.shape
    n_parts = parts.shape[0]
    tr = _row_tile(R, 256)
    c1 = 1.0 - ADAM_B1 ** ADAM_STEP
    c2 = 1.0 - ADAM_B2 ** ADAM_STEP

    def body(p_ref, w_ref, m_ref, v_ref, g_ref, d_ref, mo_ref, vo_ref):
        g = p_ref[0].astype(F32)
        for p in range(1, n_parts):
            g = g + p_ref[p].astype(F32)
        m_new = ADAM_B1 * m_ref[...] + (1.0 - ADAM_B1) * g
        v_new = ADAM_B2 * v_ref[...] + (1.0 - ADAM_B2) * (g * g)
        g_ref[...] = g
        mo_ref[...] = m_new
        vo_ref[...] = v_new
        d_ref[...] = -ADAM_LR * ((m_new / c1) / (jnp.sqrt(v_new / c2) + ADAM_EPS) + ADAM_WD * w_ref[...])

    spec = _row_spec(tr, C)
    sds = jax.ShapeDtypeStruct((R, C), F32)
    return _pcall(
        comm, body, name=name, grid=(R // tr,),
        in_specs=[pl.BlockSpec((n_parts, tr, C), lambda i: (0, i, 0)), spec, spec, spec],
        out_specs=[spec, spec, spec, spec],
        out_shape=[sds, sds, sds, sds],
        compiler_params=_params("parallel"),
    )(parts, w, m, v)


def pair_sum(parts, from_sibling, core, name):
    _, R, C = parts.shape
    tr = _row_tile(R, 1024)

    def body(core_ref, p_ref, s_ref, o_ref):
        o_ref[...] = (p_ref[...].astype(F32) + s_ref[...].astype(F32)).astype(o_ref.dtype)

    return pl.pallas_call(
        body, name=name,
        grid_spec=pltpu.PrefetchScalarGridSpec(
            num_scalar_prefetch=1, grid=(4, R // tr),
            in_specs=[pl.BlockSpec((None, tr, C), lambda q, i, core_ref: (2 * q + core_ref[0], i, 0)),
                      pl.BlockSpec((None, tr, C), lambda q, i, core_ref: (q, i, 0))],
            out_specs=pl.BlockSpec((None, tr, C), lambda q, i, core_ref: (q, i, 0))),
        out_shape=jax.ShapeDtypeStruct((4, R, C), BF16),
        compiler_params=_params("parallel", "parallel"),
    )(core, parts, from_sibling)


def add2(a, b, name, comm=None):
    R, C = a.shape
    tr = _row_tile(R, 256)

    def body(a_ref, b_ref, o_ref):
        o_ref[...] = a_ref[...] + b_ref[...]

    spec = _row_spec(tr, C)
    return _pcall(
        comm, body, name=name, grid=(R // tr,), in_specs=[spec, spec], out_specs=spec,
        out_shape=jax.ShapeDtypeStruct((R, C), F32), compiler_params=_params("parallel"),
    )(a, b)


def _place():
    return lax.axis_index("x"), lax.axis_index("y"), lax.axis_index("c")


class Comm:
    def __init__(self, partners, arrays, out_shapes, n_remote, n_local, start, finish, aliases=None):
        self.partners = frozenset(partners)
        self.arrays, self.out_shapes = list(arrays), list(out_shapes)
        self.n_remote, self.n_local = n_remote, max(n_local, 1)
        self.start, self.finish = start, finish
        self.aliases = dict(aliases or {})
        self.sizes = [len(self.out_shapes)]

    def sem_shapes(self):
        return [pltpu.SemaphoreType.DMA((self.n_remote,)), pltpu.SemaphoreType.DMA((self.n_remote,)),
                pltpu.SemaphoreType.DMA((self.n_local,))]


class _Shifted:
    def __init__(self, ref, offset):
        self.ref, self.offset = ref, offset

    @property
    def at(self):
        return self

    def __getitem__(self, k):
        return self.ref.at[self.offset + k]


def merge_comms(comms):
    comms = [c for c in comms if c is not None]
    if not comms:
        return None

    def each(method):
        def run(ins, outs, sems):
            i = o = r = l = 0
            for c in comms:
                sub = (_Shifted(sems[0], r), _Shifted(sems[1], r), _Shifted(sems[2], l))
                getattr(c, method)(ins[i:i + len(c.arrays)], outs[o:o + len(c.out_shapes)], sub)
                i, o, r, l = i + len(c.arrays), o + len(c.out_shapes), r + c.n_remote, l + c.n_local
        return run

    aliases, i, o = {}, 0, 0
    for c in comms:
        aliases.update({i + a: o + b for a, b in c.aliases.items()})
        i, o = i + len(c.arrays), o + len(c.out_shapes)
    merged = Comm(frozenset().union(*[c.partners for c in comms]),
                  [a for c in comms for a in c.arrays], [s for c in comms for s in c.out_shapes],
                  sum(c.n_remote for c in comms), sum(c.n_local for c in comms), each("start"), each("finish"), aliases)
    merged.sizes = [len(c.out_shapes) for c in comms]
    return merged


def split_results(comm, results):
    out, i = [], 0
    for n in comm.sizes:
        out.append(list(results[i:i + n]))
        i += n
    return out


def run_comm(comm, name):
    n_in, n_out = len(comm.arrays), len(comm.out_shapes)

    def body(*refs):
        ins, outs, sems = refs[:n_in], refs[n_in:n_in + n_out], refs[n_in + n_out:]
        _handshake(comm.partners)
        comm.start(ins, outs, sems)
        comm.finish(ins, outs, sems)

    return pl.pallas_call(
        body, name=name, in_specs=[ANY] * n_in, out_specs=[ANY] * n_out, out_shape=comm.out_shapes,
        scratch_shapes=comm.sem_shapes(), input_output_aliases=comm.aliases,
        compiler_params=pltpu.CompilerParams(collective_id=_COLLECTIVE_ID[comm.partners]),
    )(*comm.arrays)


def _remote(src, dst, sems, k, to):
    return pltpu.make_async_remote_copy(src_ref=src, dst_ref=dst, send_sem=sems[0].at[k], recv_sem=sems[1].at[k],
                                        device_id=to, device_id_type=MESH)


_AG_COPIES = 13


def comm_all_gather(shards, rows=None, into=None):
    n = len(shards)
    row0, nrows = rows if rows is not None else (0, None)

    def parties():
        x, y, c = _place()
        return (x, y, c), (x, y, 1 - c), [(1 - x, y), (x, 1 - y), (1 - x, 1 - y)]

    def span(w, half=None):
        count = nrows if nrows is not None else shards[w].shape[0]
        if half is None:
            return pl.ds(row0, count)
        return pl.ds(row0 + half * (count // 2), count // 2)

    def slab(outs, w, dev, half=None):
        return outs[w].at[4 * dev[0] + 2 * dev[1] + dev[2], span(w, half)]

    def own(ins, outs, sems):
        me, sibling, chips = parties()
        local = [pltpu.make_async_copy(ins[w].at[span(w)], slab(outs, w, me), sems[2].at[w]) for w in range(n)]
        first = []
        for w in range(n):
            k = _AG_COPIES * w
            first.append(_remote(ins[w].at[span(w)], slab(outs, w, me), sems, k, sibling))
            first += [_remote(ins[w].at[span(w, h)], slab(outs, w, me, h), sems, k + 1 + 2 * j + h, (*chip, me[2]))
                      for h in range(2) for j, chip in enumerate(chips)]
        return local, first

    def start(ins, outs, sems):
        local, first = own(ins, outs, sems)
        for cp in local + first:
            cp.start()

    def finish(ins, outs, sems):
        me, sibling, chips = parties()
        local, first = own(ins, outs, sems)
        passed = []
        for w in range(n):
            k = _AG_COPIES * w
            for h in range(2):
                for j, chip in enumerate(chips):
                    got = slab(outs, w, (*chip, me[2]), h)
                    _remote(got, got, sems, k + 1 + 2 * j + h, me).wait_recv()
                    cp = _remote(got, got, sems, k + 7 + 2 * j + h, sibling)
                    cp.start()
                    passed.append(cp)
        for w in range(n):
            k = _AG_COPIES * w
            got = slab(outs, w, sibling)
            _remote(got, got, sems, k, me).wait_recv()
            for h in range(2):
                for j, chip in enumerate(chips):
                    got = slab(outs, w, (*chip, sibling[2]), h)
                    _remote(got, got, sems, k + 7 + 2 * j + h, me).wait_recv()
        for cp in first + passed:
            cp.wait_send()
        for cp in local:
            cp.wait()

    out_shapes = [jax.ShapeDtypeStruct((N_DEV,) + s.shape, s.dtype) for s in shards]
    arrays = list(shards) + (list(into) if into is not None else [])
    aliases = {n + w: w for w in range(n)} if into is not None else None
    return Comm(("sibling", "chips"), arrays, out_shapes, _AG_COPIES * n, n, start, finish, aliases)


def comm_pairs(items):
    slabbed = [a.ndim == 3 for a in items]
    first = [sum(4 if s else 1 for s in slabbed[:w]) for w in range(len(items))]

    def copies(ins, outs, sems):
        x, y, c = _place()
        sibling = (x, y, 1 - c)
        cps = []
        for w, s in enumerate(slabbed):
            if s:
                cps += [_remote(ins[w].at[2 * q + (1 - c)], outs[w].at[q], sems, first[w] + q, sibling) for q in range(4)]
            else:
                cps.append(_remote(ins[w], outs[w], sems, first[w], sibling))
        return cps

    def start(ins, outs, sems):
        for cp in copies(ins, outs, sems):
            cp.start()

    def finish(ins, outs, sems):
        for cp in copies(ins, outs, sems):
            cp.wait()

    out_shapes = [jax.ShapeDtypeStruct(((4,) + a.shape[1:]) if s else a.shape, a.dtype) for a, s in zip(items, slabbed)]
    return Comm(("sibling",), items, out_shapes, sum(4 if s else 1 for s in slabbed), 0, start, finish)


def comm_chips(items, rows=None, into=None, from_row=None, out_rows=None):
    n = len(items)
    slabbed = [a.ndim == 3 for a in items]

    def span(w, source=False):
        if rows is None:
            return pl.ds(0, items[w].shape[-2])
        return pl.ds(from_row if source and from_row is not None else rows[0], rows[1])

    def copies(ins, outs, sems):
        x, y, c = _place()
        mine = 2 * x + y
        local = [pltpu.make_async_copy(ins[w].at[mine, span(w, True)] if slabbed[w] else ins[w].at[span(w, True)],
                                       outs[w].at[mine, span(w)], sems[2].at[w]) for w in range(n)]
        remote = []
        for w in range(n):
            for j, (px, py) in enumerate([(1 - x, y), (x, 1 - y), (1 - x, 1 - y)]):
                src = ins[w].at[2 * px + py, span(w, True)] if slabbed[w] else ins[w].at[span(w, True)]
                remote.append(_remote(src, outs[w].at[mine, span(w)], sems, 3 * w + j, (px, py, c)))
        return local, remote

    def start(ins, outs, sems):
        local, remote = copies(ins, outs, sems)
        for cp in local + remote:
            cp.start()

    def finish(ins, outs, sems):
        local, remote = copies(ins, outs, sems)
        for cp in remote + local:
            cp.wait()

    def result(a):
        tall = a.shape[:-2] + (out_rows if out_rows is not None else a.shape[-2], a.shape[-1])
        return jax.ShapeDtypeStruct(tall if a.ndim == 3 else (4,) + tall, a.dtype)

    if into is None:
        return Comm(("chips",), items, [result(a) for a in items], 3 * n, n, start, finish)
    out_shapes = [jax.ShapeDtypeStruct(b.shape, b.dtype) for b in into]
    return Comm(("chips",), list(items) + list(into), out_shapes, 3 * n, n, start, finish, {n + w: w for w in range(n)})


_SMALL = ("ffn1_norm", "mix_norm", "ln_v_gain", "ln_v_bias", "spatial_w", "spatial_b", "gnorm_a", "gnorm_b",
          "cross_norm", "mem_norm", "ffn2_norm", "final_norm")
_BIG = ("ffn1_w_in", "ffn1_w_out", "w_mix_in", "w_mix_out", "w_cq", "w_ckv", "w_co", "ffn2_w_in", "ffn2_w_out")
_COL_SHARDED = ("ffn1_w_in", "w_mix_in", "w_ckv", "ffn2_w_in")
_ORDER = ("ffn1_norm", "ffn1_w_in", "ffn1_w_out", "mix_norm", "w_mix_in", "ln_v_gain", "ln_v_bias", "spatial_w",
          "spatial_b", "gnorm_a", "gnorm_b", "w_mix_out", "cross_norm", "mem_norm", "w_cq", "w_ckv", "w_co",
          "ffn2_norm", "ffn2_w_in", "ffn2_w_out", "final_norm")


_SMALL_PAD = 136


def _rows128(a):
    return a.reshape(-1, 128)


def kernel(x, mem, ffn1_norm, ffn1_w_in, ffn1_w_out, mix_norm, w_mix_in, ln_v_gain, ln_v_bias, spatial_w, spatial_b, gnorm_a, gnorm_b, w_mix_out, cross_norm, mem_norm, w_cq, w_ckv, w_co, ffn2_norm, ffn2_w_in, ffn2_w_out, final_norm, loss_target, m_ffn1_norm, m_ffn1_w_in, m_ffn1_w_out, m_mix_norm, m_w_mix_in, m_ln_v_gain, m_ln_v_bias, m_spatial_w, m_spatial_b, m_gnorm_a, m_gnorm_b, m_w_mix_out, m_cross_norm, m_mem_norm, m_w_cq, m_w_ckv, m_w_co, m_ffn2_norm, m_ffn2_w_in, m_ffn2_w_out, m_final_norm, v_ffn1_norm, v_ffn1_w_in, v_ffn1_w_out, v_mix_norm, v_w_mix_in, v_ln_v_gain, v_ln_v_bias, v_spatial_w, v_spatial_b, v_gnorm_a, v_gnorm_b, v_w_mix_out, v_cross_norm, v_mem_norm, v_w_cq, v_w_ckv, v_w_co, v_ffn2_norm, v_ffn2_w_in, v_ffn2_w_out, v_final_norm):
    given = dict(locals())
    wts = {k: given[k] for k in _ORDER}
    mom = {k: given["m_" + k] for k in _ORDER}
    var = {k: given["v_" + k] for k in _ORDER}

    D = D_MODEL
    xs = x.reshape(-1, D)
    mems = mem.reshape(-1, D)
    tgt = loss_target.reshape(-1, D)
    vec = lambda a: a.reshape(1, -1)
    g1, gmix, gcross, gmem, g2, gfin = (vec(wts[k]) for k in
                                        ("ffn1_norm", "mix_norm", "cross_norm", "mem_norm", "ffn2_norm", "final_norm"))
    ln_g, ln_b, ga, gb = (vec(wts[k]) for k in ("ln_v_gain", "ln_v_bias", "gnorm_a", "gnorm_b"))
    w_s = spatial_w.reshape(G_A, SGU_BLOCK, SGU_BLOCK)
    b_t = spatial_b.reshape(G_A, SGU_BLOCK).T

    shard2d = {k: wts[k].reshape(wts[k].shape[1:]) for k in _BIG}
    (first_b,) = cast_bf16([shard2d[_BIG[0]]], "cast_first")
    shard_b = {_BIG[0]: first_b}
    full = {}

    landing, rows_done = {}, {}

    def gathering(pieces, fn, *args, **kw):
        pieces = [p if isinstance(p, tuple) else (p, None) for p in pieces]
        comm = merge_comms([comm_all_gather([shard_b[k]], rows, [landing[k]] if k in landing else None)
                            for k, rows in pieces])
        out, got = fn(*args, comm=comm, **kw)
        for (k, rows), (g,) in zip(pieces, split_results(comm, got)):
            landing[k] = g
            rows_done[k] = rows_done.get(k, 0) + (rows[1] if rows is not None else shard_b[k].shape[0])
            if rows_done[k] == shard_b[k].shape[0]:
                full[k] = g if k in _COL_SHARDED else g.reshape(-1, g.shape[2])
        return out

    shard_b.update(zip(_BIG[1:], gathering((_BIG[0],), cast_bf16, [shard2d[k] for k in _BIG[1:]], "cast_rest")))
    n1 = rms_fwd(xs, g1, "f_n1")
    a1, hsw1 = gathering(("ffn1_w_out",), mm_swiglu_g, n1, full["ffn1_w_in"], "f_a1")
    h1 = gathering(("w_mix_in",), mm_nn, hsw1, full["ffn1_w_out"], F32, "f_h1", scale=0.5, res=xs)
    n2 = gathering(("w_cq",), rms_fwd, h1, gmix, "f_n2")
    z = gathering(("w_mix_out", "w_co"), mm_nn_g, n2, full["w_mix_in"], F32, "f_z")
    ya = gathering((("w_ckv", (0, 512)),), sgu_fwd, z, ln_g, ln_b, w_s, b_t, "f_sgu")
    qkv = split_heads(z, "f_qkv")
    yb = gathering((("w_ckv", (512, 1536)), ("ffn2_w_in", (0, 512))), sb_fwd, qkv, "f_sb")
    ycat = gathering((("ffn2_w_in", (512, 128)),), rmscat_fwd, ya, yb, ga, gb, "f_ycat")
    h2 = gathering((("ffn2_w_in", (640, 256)),), mm_nn, ycat, full["w_mix_out"], F32, "f_h2", res=h1)
    n3 = gathering((("ffn2_w_in", (896, 128)),), rms_fwd, h2, gcross, "f_n3")
    memn = rms_fwd(mems, gmem, "f_memn")
    qc = gathering((("ffn2_w_in", (1024, 256)),), mm_nn, n3, full["w_cq"], BF16, "f_qc", scale=X_DH ** -0.5)
    kv = gathering((("ffn2_w_in", (1280, 128)),), mm_nn_g, memn, full["w_ckv"], BF16, "f_kv")
    o = gathering((("ffn2_w_in", (1408, 128)),), xattn_fwd, qc, kv, "f_xattn")
    h3 = gathering((("ffn2_w_in", (1536, 256)),), mm_nn, o, full["w_co"], F32, "f_h3", res=h2)
    n4 = gathering((("ffn2_w_in", (1792, 256)),), rms_fwd, h3, g2, "f_n4")
    a2, hsw2 = gathering(("ffn2_w_out",), mm_swiglu_g, n4, full["ffn2_w_in"], "f_a2")
    h4 = mm_nn(hsw2, full["ffn2_w_out"], F32, "f_h4", scale=0.5, res=h3)

    grads, parts, sums, recv = {}, {}, {}, {}
    core = lax.axis_index("c").astype(jnp.int32).reshape(1)

    def partial_of(k, g):
        grads[k] = g
        parts[k] = g if g.ndim == 3 else g.reshape(N_DEV, -1, g.shape[1])

    def reducing(pairs, chips, fn, *args, also=None, **kw):
        def piece(p):
            if isinstance(p, dict):
                return p
            k, rows = p if isinstance(p, tuple) else (p, None)
            return dict(sums=k, rows=rows, to=k)

        chips = [piece(p) for p in chips]
        comms = [comm_pairs([parts[k] for k in pairs])] if pairs else []
        comms += [comm_chips([sums[p["sums"]]], p["rows"], [recv[p["to"]]] if p["to"] in recv else None,
                             p.get("from_row"), p.get("out_rows")) for p in chips]
        comm = merge_comms(comms + ([also] if also is not None else []))
        out, got = fn(*args, comm=comm, **kw)
        got = split_results(comm, got)
        if pairs:
            for k, r in zip(pairs, got.pop(0)):
                sums[k] = pair_sum(parts[k], r, core, f"pair_sum_{k}")
        for p, (r,) in zip(chips, got):
            recv[p["to"]] = r
        return out if also is None else (out, got[-1])

    loss_part, dh4, df2, grads["final_norm"] = loss_head(h4, tgt, gfin, "loss_head")
    partial_of("ffn2_w_out", mm_tn(hsw2, df2, "b_ffn2_dwout"))
    da2 = reducing(("ffn2_w_out",), (), mm_swiglu_bwd, df2, full["ffn2_w_out"], a2, "b_ffn2_da")
    partial_of("ffn2_w_in", reducing((), (("ffn2_w_out", (0, 352)),), mm_tn_g, n4, da2, N_DEV, "b_ffn2_dwin"))
    dn4 = reducing(("ffn2_w_in",), (("ffn2_w_out", (352, 352)),), mm_nt_g, da2, full["ffn2_w_in"], "b_ffn2_dn")
    dh3, dh3b, grads["ffn2_norm"] = rms_bwd(dn4, h3, g2, dh4, 1.0, "b_n4")

    partial_of("w_co", mm_tn(o, dh3b, "b_dwco"))
    do = reducing(("w_co",), (("ffn2_w_in", (0, 256)),), mm_nt, dh3b, full["w_co"], BF16, "b_do")
    dqp, dkv = reducing((), (("ffn2_w_in", (256, 256)),), xattn_bwd, qc, kv, do, "b_xattn")
    partial_of("w_cq", mm_tn(n3, dqp, "b_dwcq"))
    dn3 = reducing(("w_cq",), (("ffn2_w_in", (512, 256)),), mm_nt, dqp, full["w_cq"], F32, "b_dn3")
    partial_of("w_ckv", mm_tn_g(memn, dkv, N_DEV, "b_dwckv"))
    dmemn = reducing(("w_ckv",), (("ffn2_w_in", (768, 128)),), mm_nt_g, dkv, full["w_ckv"], "b_dmemn")
    _, _, grads["mem_norm"] = rms_bwd(dmemn, mems, gmem, None, 1.0, "b_memn")
    dh2, dh2b, grads["cross_norm"] = rms_bwd(dn3, h2, gcross, dh3, 1.0, "b_n3")

    partial_of("w_mix_out", mm_tn(ycat, dh2b, "b_dwmixout"))
    dycat = reducing(("w_mix_out",), (("ffn2_w_in", (896, 256)),), mm_nt, dh2b, full["w_mix_out"], F32, "b_dycat")
    dya, dyb, grads["gnorm_a"], grads["gnorm_b"] = rmscat_bwd(dycat, ya, yb, ga, gb, "b_ycat")
    dza, grads["ln_v_gain"], grads["ln_v_bias"], grads["spatial_w"], grads["spatial_b"] = reducing(
        (), (("ffn2_w_in", (1152, 384)),), sgu_bwd, z, dya, ln_g, ln_b, w_s, b_t, "b_sgu")
    dq, dk, dv = reducing((), (("ffn2_w_in", (1536, 512)), "w_co", "w_cq", ("w_ckv", (0, 512))), sb_bwd, qkv, yb,
                          dyb, "b_sb")
    dz = join_dz(dza, dq, dk, dv, "b_dz")
    partial_of("w_mix_in", reducing((), (("w_ckv", (512, 1536)),), mm_tn_g, n2, dz, N_DEV, "b_dwmixin"))
    dn2 = reducing(("w_mix_in",), ("w_mix_out",), mm_nt_g, dz, full["w_mix_in"], "b_dn2")
    dh1, dh1b, grads["mix_norm"] = reducing((), (("w_mix_in", (0, 512)),), rms_bwd, dn2, h1, gmix, dh2, 0.5, "b_n2")

    pad = jnp.zeros((_SMALL_PAD, 128), F32)
    small_early = jnp.concatenate([_rows128(grads[k]) for k in _SMALL[1:]] + [pad], axis=0)
    g_w1out, (early_sibling,) = reducing((), (("w_mix_in", (512, 1024)),), mm_tn, hsw1, dh1b, "b_ffn1_dwout",
                                         also=comm_pairs([small_early]))
    partial_of("ffn1_w_out", g_w1out)
    early_pair = add2(small_early, early_sibling, "pair_sum_small_early")
    da1, (early_all,) = reducing(("ffn1_w_out",), (("w_mix_in", (1536, 512)),), mm_swiglu_bwd, dh1b,
                                 full["ffn1_w_out"], a1, "b_ffn1_da", also=comm_chips([early_pair]))
    half = D // 2
    partial_of("ffn1_w_in_a", reducing((), (("ffn1_w_out", (0, 352)),), mm_tn_g, n1, da1, N_DEV, "b_ffn1_dwin_a",
                                       cols=(0, half)))
    partial_of("ffn1_w_in_b", reducing(("ffn1_w_in_a",), (("ffn1_w_out", (352, 352)),), mm_tn_g, n1, da1, N_DEV,
                                       "b_ffn1_dwin_b", cols=(half, half)))
    dn1 = reducing(("ffn1_w_in_b",), (dict(sums="ffn1_w_in_a", rows=(0, half), to="ffn1_w_in", out_rows=D),),
                   mm_nt_g, da1, full["ffn1_w_in"], "b_ffn1_dn")
    (dx, _, grads["ffn1_norm"]) = reducing((), (dict(sums="ffn1_w_in_b", rows=(half, half), from_row=0, to="ffn1_w_in"),),
                                           rms_bwd, dn1, xs, g1, dh1, 1.0, "b_n1")

    out_g, out_d, out_m, out_v = {}, {}, {}, {}
    for k in _BIG:
        res = adamw(recv[k], shard2d[k], mom[k].reshape(shard2d[k].shape), var[k].reshape(shard2d[k].shape), f"adamw_{k}")
        out_g[k], out_d[k], out_m[k], out_v[k] = (t.reshape(wts[k].shape) for t in res)

    small_late = _rows128(grads[_SMALL[0]])
    (late_sibling,) = run_comm(comm_pairs([small_late]), "comm_pairs_small_late")
    late_pair = add2(small_late, late_sibling, "pair_sum_small_late")
    (late_all,) = run_comm(comm_chips([late_pair]), "comm_chips_small_late")
    small_all = jnp.concatenate([late_all, early_all], axis=1)
    pack = lambda d: jnp.concatenate([_rows128(d[k]) for k in _SMALL] + [pad], axis=0)
    res = adamw(small_all, pack(wts), pack(mom), pack(var), "adamw_small")
    row = 0
    for k in _SMALL:
        nrow = wts[k].size // 128
        for dst, t in zip((out_g, out_d, out_m, out_v), res):
            dst[k] = t[row:row + nrow].reshape(wts[k].shape)
        row += nrow

    loss = lax.psum(loss_part[0, 0], ("x", "y", "c"))
    grad_x = dx.reshape(x.shape)
    return (loss, grad_x, *[out_g[k] for k in _ORDER], *[out_d[k] for k in _ORDER],
            *[out_m[k] for k in _ORDER], *[out_v[k] for k in _ORDER])
```

```python
import functools
import math

import jax
import jax.numpy as jnp
from jax import lax
from jax.experimental import pallas as pl
from jax.experimental.pallas import tpu as pltpu

F32 = jnp.float32
BF16 = jnp.bfloat16

N_DEV = 8
D_MODEL = 2048
D_FF = 5632
W_A = 1024
G_A = 8
GA_DIM = 128
SGU_BLOCK = 128
CHUNK = 64
H_B = 8
DH_B = 128
Q_BLOCK = 128
X_HEADS = 4
X_DH = 512
N_MEM = 256
EPS = 1e-6

ADAM_LR = 0.001
ADAM_B1 = 0.9
ADAM_B2 = 0.999
ADAM_EPS = 1e-08
ADAM_WD = 0.01
ADAM_STEP = 10

VMEM_LIMIT = 56 * 2**20
ROW_TILE = 256

MESH = pl.DeviceIdType.MESH
ANY = pl.BlockSpec(memory_space=pl.ANY)

_NT = (((1,), (1,)), ((), ()))
_TN = (((0,), (0,)), ((), ()))


def _params(*sem, collective_id=None):
    return pltpu.CompilerParams(dimension_semantics=sem, vmem_limit_bytes=VMEM_LIMIT, collective_id=collective_id)


_COLLECTIVE_ID = {frozenset({"sibling"}): 0, frozenset({"chips"}): 1, frozenset({"sibling", "chips"}): 2}


def _handshake(partners):
    x, y, c = lax.axis_index("x"), lax.axis_index("y"), lax.axis_index("c")
    peers = [(x, y, 1 - c)] if "sibling" in partners else []
    if "chips" in partners:
        peers += [(1 - x, y, c), (x, 1 - y, c), (1 - x, 1 - y, c)]
    barrier = pltpu.get_barrier_semaphore()
    for peer in peers:
        pl.semaphore_signal(barrier, inc=1, device_id=peer, device_id_type=MESH)
    pl.semaphore_wait(barrier, len(peers))


def _zeros(ref):
    return jnp.zeros(ref.shape, ref.dtype)


def _pcall(comm, body, *, name, grid, in_specs, out_specs, out_shape, compiler_params, scratch_shapes=()):
    if comm is None:
        return pl.pallas_call(body, name=name, grid=grid, in_specs=in_specs, out_specs=out_specs, out_shape=out_shape,
                              scratch_shapes=list(scratch_shapes), compiler_params=compiler_params)
    multi = isinstance(out_shape, (list, tuple))
    out_shapes = list(out_shape) if multi else [out_shape]
    out_specs_l = list(out_specs) if multi else [out_specs]
    n_in, n_out, n_scr = len(in_specs), len(out_shapes), len(scratch_shapes)
    n_cin, n_cout = len(comm.arrays), len(comm.out_shapes)

    def with_comm(*refs):
        ins, refs = refs[:n_in], refs[n_in:]
        cins, refs = refs[:n_cin], refs[n_cin:]
        outs, refs = refs[:n_out], refs[n_out:]
        couts, refs = refs[:n_cout], refs[n_cout:]
        scr, sems = refs[:n_scr], refs[n_scr:]
        first = functools.reduce(jnp.logical_and, [pl.program_id(a) == 0 for a in range(len(grid))])
        last = functools.reduce(jnp.logical_and, [pl.program_id(a) == grid[a] - 1 for a in range(len(grid))])
        @pl.when(first)
        def _():
            _handshake(comm.partners)
            comm.start(cins, couts, sems)

        body(*ins, *outs, *scr)
        pl.when(last)(lambda: comm.finish(cins, couts, sems))

    call = pl.pallas_call(
        with_comm, name=name, grid=grid, in_specs=list(in_specs) + [ANY] * n_cin,
        out_specs=out_specs_l + [ANY] * n_cout, out_shape=out_shapes + comm.out_shapes,
        scratch_shapes=list(scratch_shapes) + comm.sem_shapes(),
        compiler_params=_params(*(("arbitrary",) * len(grid)), collective_id=_COLLECTIVE_ID[comm.partners]),
        input_output_aliases={n_in + i: n_out + j for i, j in comm.aliases.items()})

    def run(*args):
        res = call(*args, *comm.arrays)
        main = res[:n_out]
        return (list(main) if multi else main[0]), list(res[n_out:])

    return run


def _dot(a, b):
    return jnp.dot(a, b, preferred_element_type=F32)


def _dot_nt(a, b):
    return lax.dot_general(a, b, _NT, preferred_element_type=F32)


def _dot_tn(a, b):
    return lax.dot_general(a, b, _TN, preferred_element_type=F32)


def mm_nn_g(a, bg, out_dtype, name, tm=512, comm=None):
    M, K = a.shape
    G, _, n = bg.shape
    tm = min(tm, M)

    def body(a_ref, b_ref, o_ref):
        o_ref[...] = _dot(a_ref[...], b_ref[...]).astype(o_ref.dtype)

    return _pcall(
        comm, body, name=name, grid=(G, M // tm),
        in_specs=[pl.BlockSpec((tm, K), lambda g, m: (m, 0)),
                  pl.BlockSpec((None, K, n), lambda g, m: (g, 0, 0))],
        out_specs=pl.BlockSpec((tm, n), lambda g, m: (m, g)),
        out_shape=jax.ShapeDtypeStruct((M, G * n), out_dtype),
        compiler_params=_params("parallel", "parallel"),
    )(a, bg)


def mm_swiglu_g(a, bg, name, tm=512, comm=None):
    M, K = a.shape
    G, _, n = bg.shape
    half = G // 2
    tm = min(tm, M)

    def body(a_ref, bgate_ref, bup_ref, gu_ref, h_ref):
        av = a_ref[...]
        gate = _dot(av, bgate_ref[...])
        up = _dot(av, bup_ref[...])
        gu_ref[0] = gate.astype(BF16)
        gu_ref[1] = up.astype(BF16)
        h_ref[...] = (gate * _sigmoid(gate) * up).astype(BF16)

    return _pcall(
        comm, body, name=name, grid=(half, M // tm),
        in_specs=[pl.BlockSpec((tm, K), lambda p, m: (m, 0)),
                  pl.BlockSpec((None, K, n), lambda p, m: (p, 0, 0)),
                  pl.BlockSpec((None, K, n), lambda p, m: (p + half, 0, 0))],
        out_specs=[pl.BlockSpec((2, tm, n), lambda p, m: (0, m, p)), pl.BlockSpec((tm, n), lambda p, m: (m, p))],
        out_shape=[jax.ShapeDtypeStruct((2, M, half * n), BF16), jax.ShapeDtypeStruct((M, half * n), BF16)],
        compiler_params=_params("parallel", "parallel"),
    )(a, bg, bg)


def mm_swiglu_bwd(dy, w_out, gate_up, name, tk=512, comm=None):
    M, N = dy.shape
    F = w_out.shape[0]

    def body(dy_ref, w_ref, gu_ref, o_ref):
        dh = _dot_nt(dy_ref[...], w_ref[...])
        gt = gu_ref[0].astype(F32)
        up = gu_ref[1].astype(F32)
        sg = _sigmoid(gt)
        o_ref[0] = (dh * up * (sg * (1.0 + gt * (1.0 - sg)))).astype(BF16)
        o_ref[1] = (dh * (gt * sg)).astype(BF16)

    planes = pl.BlockSpec((2, M, tk), lambda k: (0, 0, k))
    return _pcall(
        comm, body, name=name, grid=(F // tk,),
        in_specs=[pl.BlockSpec((M, N), lambda k: (0, 0)), pl.BlockSpec((tk, N), lambda k: (k, 0)), planes],
        out_specs=planes,
        out_shape=jax.ShapeDtypeStruct((2, M, F), BF16),
        compiler_params=_params("parallel"),
    )(dy, w_out, gate_up)


def _shard_cols_spec(dy, G, rows, index):
    if dy.ndim == 2:
        n = dy.shape[1] // G
        return pl.BlockSpec((rows, n), lambda *ids: index(*ids)), n
    half = G // 2
    n = dy.shape[2] // half

    def planes(*ids):
        r, g = index(*ids)
        return g // half, r, g % half

    return pl.BlockSpec((None, rows, n), planes), n


def mm_nn(a, b, out_dtype, name, tm=512, tn=1024, scale=1.0, res=None, comm=None):
    M, K = a.shape
    _, N = b.shape
    tm, tn = min(tm, M), min(tn, N)

    def body(*refs):
        if res is None:
            a_ref, b_ref, o_ref = refs
            acc = _dot(a_ref[...], b_ref[...])
            o_ref[...] = (acc * scale if scale != 1.0 else acc).astype(o_ref.dtype)
        else:
            a_ref, b_ref, r_ref, o_ref = refs
            o_ref[...] = (r_ref[...] + scale * _dot(a_ref[...], b_ref[...])).astype(o_ref.dtype)

    in_specs = [pl.BlockSpec((tm, K), lambda n, m: (m, 0)),
                pl.BlockSpec((K, tn), lambda n, m: (0, n))]
    args = [a, b]
    if res is not None:
        in_specs.append(pl.BlockSpec((tm, tn), lambda n, m: (m, n)))
        args.append(res)
    return _pcall(
        comm, body, name=name, grid=(N // tn, M // tm),
        in_specs=in_specs,
        out_specs=pl.BlockSpec((tm, tn), lambda n, m: (m, n)),
        out_shape=jax.ShapeDtypeStruct((M, N), out_dtype),
        compiler_params=_params("parallel", "parallel"),
    )(*args)


def mm_nt_g(dy, bg, name, tm=512, comm=None):
    M = dy.shape[-2]
    G, K, n = bg.shape
    tm = min(tm, M)
    dy_spec, _ = _shard_cols_spec(dy, G, tm, lambda m, g: (m, g))

    def body(dy_ref, b_ref, o_ref):
        part = _dot_nt(dy_ref[...], b_ref[...])

        @pl.when(pl.program_id(1) == 0)
        def _():
            o_ref[...] = part

        @pl.when(pl.program_id(1) > 0)
        def _():
            o_ref[...] += part

    return _pcall(
        comm, body, name=name, grid=(M // tm, G),
        in_specs=[dy_spec, pl.BlockSpec((None, K, n), lambda m, g: (g, 0, 0))],
        out_specs=pl.BlockSpec((tm, K), lambda m, g: (m, 0)),
        out_shape=jax.ShapeDtypeStruct((M, K), F32),
        compiler_params=_params("parallel", "arbitrary"),
    )(dy, bg)


def mm_nt(dy, b, out_dtype, name, tk=512, comm=None):
    M, N = dy.shape
    K, _ = b.shape

    def body(dy_ref, b_ref, o_ref):
        o_ref[...] = _dot_nt(dy_ref[...], b_ref[...]).astype(o_ref.dtype)

    return _pcall(
        comm, body, name=name, grid=(K // tk,),
        in_specs=[pl.BlockSpec((M, N), lambda k: (0, 0)),
                  pl.BlockSpec((tk, N), lambda k: (k, 0))],
        out_specs=pl.BlockSpec((M, tk), lambda k: (0, k)),
        out_shape=jax.ShapeDtypeStruct((M, K), out_dtype),
        compiler_params=_params("parallel"),
    )(dy, b)


def mm_tn_g(x, dy, G, name, tk=512, cols=None, comm=None):
    M, K = x.shape
    k0, K = cols if cols is not None else (0, K)
    dy_spec, n = _shard_cols_spec(dy, G, M, lambda g, k: (0, g))

    def body(x_ref, dy_ref, o_ref):
        o_ref[...] = _dot_tn(x_ref[...], dy_ref[...]).astype(o_ref.dtype)

    return _pcall(
        comm, body, name=name, grid=(G, K // tk),
        in_specs=[pl.BlockSpec((M, tk), lambda g, k: (0, k + k0 // tk)), dy_spec],
        out_specs=pl.BlockSpec((None, tk, n), lambda g, k: (g, k, 0)),
        out_shape=jax.ShapeDtypeStruct((G, K, n), BF16),
        compiler_params=_params("parallel", "parallel"),
    )(x, dy)


def mm_tn(x, dy, name, tk=512, comm=None):
    M, K = x.shape
    _, N = dy.shape

    def body(x_ref, dy_ref, o_ref):
        o_ref[...] = _dot_tn(x_ref[...], dy_ref[...]).astype(o_ref.dtype)

    return _pcall(
        comm, body, name=name, grid=(K // tk,),
        in_specs=[pl.BlockSpec((M, tk), lambda k: (0, k)),
                  pl.BlockSpec((M, N), lambda k: (0, 0))],
        out_specs=pl.BlockSpec((tk, N), lambda k: (k, 0)),
        out_shape=jax.ShapeDtypeStruct((K, N), BF16),
        compiler_params=_params("parallel"),
    )(x, dy)


def _rstd(x):
    return lax.rsqrt(jnp.mean(x * x, axis=-1, keepdims=True) + EPS)


def _rms_bwd(dn, xhat, r, g):
    dxhat = dn * g
    return r * (dxhat - xhat * jnp.mean(dxhat * xhat, axis=-1, keepdims=True))


def _row_spec(tr, width, col=0):
    return pl.BlockSpec((tr, width), lambda i: (i, col))


def _vec_spec(width):
    return pl.BlockSpec((1, width), lambda i: (0, 0))


def _heads_spec(tr):
    return pl.BlockSpec((H_B, tr, DH_B), lambda i: (0, i, 0))


def _heads_to_cols(ref):
    return jnp.concatenate([ref[h] for h in range(H_B)], axis=1)


def split_heads(z, name, comm=None):
    S = z.shape[0]
    tr = min(ROW_TILE, S)
    width = H_B * DH_B
    first = 2 * W_A // width

    def body(q_ref, k_ref, v_ref, o_ref):
        for p, (ref, scale) in enumerate(((q_ref, DH_B ** -0.5), (k_ref, 1.0), (v_ref, 1.0))):
            for h in range(H_B):
                cols = ref[:, h * DH_B:(h + 1) * DH_B]
                o_ref[p * H_B + h] = (cols * scale if scale != 1.0 else cols).astype(BF16)

    return _pcall(
        comm, body, name=name, grid=(S // tr,),
        in_specs=[_row_spec(tr, width, first), _row_spec(tr, width, first + 1), _row_spec(tr, width, first + 2)],
        out_specs=pl.BlockSpec((3 * H_B, tr, DH_B), lambda i: (0, i, 0)),
        out_shape=jax.ShapeDtypeStruct((3 * H_B, S, DH_B), BF16),
        compiler_params=_params("parallel"),
    )(z, z, z)


def join_dz(dza, dq, dk, dv, name, comm=None):
    S, wa = dza.shape
    tr = min(ROW_TILE, S)
    width = H_B * DH_B

    def body(dza_ref, dq_ref, dk_ref, dv_ref, o_ref):
        o_ref[:, :wa] = dza_ref[...]
        for p, ref in enumerate((dq_ref, dk_ref, dv_ref)):
            for h in range(H_B):
                lo = wa + p * width + h * DH_B
                o_ref[:, lo:lo + DH_B] = ref[h]

    return _pcall(
        comm, body, name=name, grid=(S // tr,),
        in_specs=[_row_spec(tr, wa), _heads_spec(tr), _heads_spec(tr), _heads_spec(tr)],
        out_specs=_row_spec(tr, wa + 3 * width),
        out_shape=jax.ShapeDtypeStruct((S, wa + 3 * width), BF16),
        compiler_params=_params("parallel"),
    )(dza, dq, dk, dv)


def rms_fwd(x, g, name, comm=None):
    M, D = x.shape
    tr = min(ROW_TILE, M)

    def body(x_ref, g_ref, o_ref):
        xv = x_ref[...]
        o_ref[...] = (xv * _rstd(xv) * g_ref[...]).astype(o_ref.dtype)

    return _pcall(
        comm, body, name=name, grid=(M // tr,),
        in_specs=[_row_spec(tr, D), _vec_spec(D)],
        out_specs=_row_spec(tr, D),
        out_shape=jax.ShapeDtypeStruct((M, D), BF16),
        compiler_params=_params("parallel"),
    )(x, g)


def rms_bwd(dn, h, g, dres, copy_scale, name, comm=None):
    M, D = h.shape
    tr = min(ROW_TILE, M)
    has_res = dres is not None

    def body(*refs):
        if has_res:
            dn_ref, h_ref, g_ref, dres_ref, dh_ref, dhb_ref, dg_ref = refs
        else:
            dn_ref, h_ref, g_ref, dh_ref, dhb_ref, dg_ref = refs
        hv = h_ref[...]
        r = _rstd(hv)
        xhat = hv * r
        dn = dn_ref[...]
        part = jnp.sum(dn * xhat, axis=0, keepdims=True)

        @pl.when(pl.program_id(0) == 0)
        def _():
            dg_ref[...] = part

        @pl.when(pl.program_id(0) > 0)
        def _():
            dg_ref[...] += part

        dh = _rms_bwd(dn, xhat, r, g_ref[...])
        if has_res:
            dh = dh + dres_ref[...]
        dh_ref[...] = dh
        dhb_ref[...] = (dh * copy_scale if copy_scale != 1.0 else dh).astype(BF16)

    in_specs = [_row_spec(tr, D), _row_spec(tr, D), _vec_spec(D)]
    args = [dn, h, g]
    if has_res:
        in_specs.append(_row_spec(tr, D))
        args.append(dres)
    return _pcall(
        comm, body, name=name, grid=(M // tr,),
        in_specs=in_specs,
        out_specs=[_row_spec(tr, D), _row_spec(tr, D), _vec_spec(D)],
        out_shape=[jax.ShapeDtypeStruct((M, D), F32), jax.ShapeDtypeStruct((M, D), BF16),
                   jax.ShapeDtypeStruct((1, D), F32)],
        compiler_params=_params("arbitrary"),
    )(*args)


def _sigmoid(x):
    return 1.0 / (1.0 + jnp.exp(-x))


def rmscat_fwd(ya, yb, ga, gb, name, comm=None):
    M, W = ya.shape
    tr = min(ROW_TILE, M)

    def body(ya_ref, yb_ref, ga_ref, gb_ref, o_ref):
        a = ya_ref[...]
        b = _heads_to_cols(yb_ref)
        o_ref[:, :W] = (a * _rstd(a) * ga_ref[...]).astype(BF16)
        o_ref[:, W:] = (b * _rstd(b) * gb_ref[...]).astype(BF16)

    return _pcall(
        comm, body, name=name, grid=(M // tr,),
        in_specs=[_row_spec(tr, W), _heads_spec(tr), _vec_spec(W), _vec_spec(W)],
        out_specs=_row_spec(tr, 2 * W),
        out_shape=jax.ShapeDtypeStruct((M, 2 * W), BF16),
        compiler_params=_params("parallel"),
    )(ya, yb, ga, gb)


def rmscat_bwd(dycat, ya, yb, ga, gb, name, comm=None):
    M, W = ya.shape
    tr = min(ROW_TILE, M)

    def body(dc_ref, ya_ref, yb_ref, ga_ref, gb_ref, dya_ref, dyb_ref, dga_ref, dgb_ref):
        first = pl.program_id(0) == 0
        for by_head, y_ref, g_ref, dy_ref, dg_ref, lo in ((False, ya_ref, ga_ref, dya_ref, dga_ref, 0),
                                                          (True, yb_ref, gb_ref, dyb_ref, dgb_ref, W)):
            yv = _heads_to_cols(y_ref) if by_head else y_ref[...]
            r = _rstd(yv)
            xhat = yv * r
            dn = dc_ref[:, lo:lo + W]
            part = jnp.sum(dn * xhat, axis=0, keepdims=True)

            @pl.when(first)
            def _():
                dg_ref[...] = part

            @pl.when(jnp.logical_not(first))
            def _():
                dg_ref[...] += part

            dy = _rms_bwd(dn, xhat, r, g_ref[...])
            if by_head:
                for h in range(H_B):
                    dy_ref[h] = dy[:, h * DH_B:(h + 1) * DH_B]
            else:
                dy_ref[...] = dy

    return _pcall(
        comm, body, name=name, grid=(M // tr,),
        in_specs=[_row_spec(tr, 2 * W), _row_spec(tr, W), _heads_spec(tr), _vec_spec(W), _vec_spec(W)],
        out_specs=[_row_spec(tr, W), _heads_spec(tr), _vec_spec(W), _vec_spec(W)],
        out_shape=[jax.ShapeDtypeStruct((M, W), F32), jax.ShapeDtypeStruct((H_B, M, DH_B), F32),
                   jax.ShapeDtypeStruct((1, W), F32), jax.ShapeDtypeStruct((1, W), F32)],
        compiler_params=_params("arbitrary"),
    )(dycat, ya, yb, ga, gb)


def loss_head(h, target, g, name, comm=None):
    M, D = h.shape
    tr = min(ROW_TILE, M)

    def body(h_ref, t_ref, g_ref, loss_ref, dh_ref, dhb_ref, dg_ref):
        hv = h_ref[...]
        gv = g_ref[...]
        r = _rstd(hv)
        xhat = hv * r
        err = xhat * gv - t_ref[...]
        lsum = jnp.sum(jnp.sum(err * err, axis=1, keepdims=True), axis=0, keepdims=True) * (0.5 / D)
        dy = err * (1.0 / D)
        part = jnp.sum(dy * xhat, axis=0, keepdims=True)

        @pl.when(pl.program_id(0) == 0)
        def _():
            dg_ref[...] = part
            loss_ref[...] = _zeros(loss_ref) + lsum

        @pl.when(pl.program_id(0) > 0)
        def _():
            dg_ref[...] += part
            loss_ref[...] += lsum

        dh = _rms_bwd(dy, xhat, r, gv)
        dh_ref[...] = dh
        dhb_ref[...] = (0.5 * dh).astype(BF16)

    return _pcall(
        comm, body, name=name, grid=(M // tr,),
        in_specs=[_row_spec(tr, D), _row_spec(tr, D), _vec_spec(D)],
        out_specs=[pl.BlockSpec((8, 128), lambda i: (0, 0)), _row_spec(tr, D), _row_spec(tr, D), _vec_spec(D)],
        out_shape=[jax.ShapeDtypeStruct((8, 128), F32), jax.ShapeDtypeStruct((M, D), F32),
                   jax.ShapeDtypeStruct((M, D), BF16), jax.ShapeDtypeStruct((1, D), F32)],
        compiler_params=_params("arbitrary"),
    )(h, target, g)


_GELU_C = math.sqrt(2.0 / math.pi)


def _gelu(x):
    return 0.5 * x * (1.0 + jnp.tanh(_GELU_C * (x + 0.044715 * (x * x * x))))


def _gelu_grad(x):
    t = jnp.tanh(_GELU_C * (x + 0.044715 * (x * x * x)))
    return 0.5 * (1.0 + t) + 0.5 * x * (1.0 - t * t) * (_GELU_C * (1.0 + 3.0 * 0.044715 * (x * x)))


def _sgu_mask():
    t = lax.broadcasted_iota(jnp.int32, (SGU_BLOCK, SGU_BLOCK), 0) // CHUNK
    s = lax.broadcasted_iota(jnp.int32, (SGU_BLOCK, SGU_BLOCK), 1) // CHUNK
    return s <= t


def _layernorm_stats(v):
    mu = jnp.mean(v, axis=-1, keepdims=True)
    cen = v - mu
    rstd = lax.rsqrt(jnp.mean(cen * cen, axis=-1, keepdims=True) + EPS)
    return cen * rstd, rstd


def sgu_fwd(z, ln_g, ln_b, w_s, b_t, name, comm=None):
    S = z.shape[0]

    def body(zu_ref, zv_ref, lg_ref, lb_ref, w_ref, bt_ref, o_ref):
        mask = _sgu_mask()
        for g in range(G_A):
            cols = slice(g * GA_DIM, (g + 1) * GA_DIM)
            u = _gelu(zu_ref[:, cols])
            vhat, _ = _layernorm_stats(_gelu(zv_ref[:, cols]))
            vln = vhat * lg_ref[:, cols] + lb_ref[:, cols]
            w = jnp.where(mask, w_ref[g], 0.0).astype(BF16)
            mixed = _dot(w, vln.astype(BF16)) + bt_ref[:, g:g + 1]
            o_ref[:, cols] = u * mixed

    return _pcall(
        comm, body, name=name, grid=(S // SGU_BLOCK,),
        in_specs=[_row_spec(SGU_BLOCK, W_A, 0), _row_spec(SGU_BLOCK, W_A, 1), _vec_spec(W_A), _vec_spec(W_A),
                  pl.BlockSpec((G_A, SGU_BLOCK, SGU_BLOCK), lambda i: (0, 0, 0)),
                  pl.BlockSpec((SGU_BLOCK, G_A), lambda i: (0, 0))],
        out_specs=_row_spec(SGU_BLOCK, W_A),
        out_shape=jax.ShapeDtypeStruct((S, W_A), F32),
        compiler_params=_params("parallel"),
    )(z, z, ln_g, ln_b, w_s, b_t)


def sgu_bwd(z, dya, ln_g, ln_b, w_s, b_t, name, comm=None):
    S = z.shape[0]
    nblk = S // SGU_BLOCK

    def body(zu_ref, zv_ref, dy_ref, lg_ref, lb_ref, w_ref, bt_ref,
             dz_ref, dlg_ref, dlb_ref, dw_ref, db_ref, dmix_acc):
        step = pl.program_id(0)
        mask = _sgu_mask()

        @pl.when(step == 0)
        def _():
            dlg_ref[...] = _zeros(dlg_ref)
            dlb_ref[...] = _zeros(dlb_ref)
            dw_ref[...] = _zeros(dw_ref)
            dmix_acc[...] = _zeros(dmix_acc)

        for g in range(G_A):
            cols = slice(g * GA_DIM, (g + 1) * GA_DIM)
            zu = zu_ref[:, cols]
            zv = zv_ref[:, cols]
            u = _gelu(zu)
            vhat, rstd = _layernorm_stats(_gelu(zv))
            lg = lg_ref[:, cols]
            vln = (vhat * lg + lb_ref[:, cols]).astype(BF16)
            w = jnp.where(mask, w_ref[g], 0.0)
            mixed = _dot(w.astype(BF16), vln) + bt_ref[:, g:g + 1]
            dy = dy_ref[:, cols]
            du = dy * mixed
            dmixed = dy * u
            dmixed_b = dmixed.astype(BF16)
            dmix_acc[g] += dmixed
            dw_ref[g] += jnp.where(mask, _dot_nt(dmixed_b, vln), 0.0)
            dvln = _dot(w.T.astype(BF16), dmixed_b)
            dlb_ref[:, cols] += jnp.sum(dvln, axis=0, keepdims=True)
            dlg_ref[:, cols] += jnp.sum(dvln * vhat, axis=0, keepdims=True)
            dvhat = dvln * lg
            dv = rstd * (dvhat - jnp.mean(dvhat, axis=-1, keepdims=True)
                         - vhat * jnp.mean(dvhat * vhat, axis=-1, keepdims=True))
            dz_ref[:, cols] = (du * _gelu_grad(zu)).astype(BF16)
            dz_ref[:, W_A + g * GA_DIM:W_A + (g + 1) * GA_DIM] = (dv * _gelu_grad(zv)).astype(BF16)

        @pl.when(step == nblk - 1)
        def _():
            for g in range(G_A):
                db_ref[g] = jnp.sum(dmix_acc[g], axis=1, keepdims=True)

    whole3 = lambda shape: pl.BlockSpec(shape, lambda i: (0, 0, 0))
    return _pcall(
        comm, body, name=name, grid=(nblk,),
        in_specs=[_row_spec(SGU_BLOCK, W_A, 0), _row_spec(SGU_BLOCK, W_A, 1), _row_spec(SGU_BLOCK, W_A),
                  _vec_spec(W_A), _vec_spec(W_A), whole3((G_A, SGU_BLOCK, SGU_BLOCK)),
                  pl.BlockSpec((SGU_BLOCK, G_A), lambda i: (0, 0))],
        out_specs=[_row_spec(SGU_BLOCK, 2 * W_A), _vec_spec(W_A), _vec_spec(W_A),
                   whole3((G_A, SGU_BLOCK, SGU_BLOCK)), whole3((G_A, SGU_BLOCK, 1))],
        out_shape=[jax.ShapeDtypeStruct((S, 2 * W_A), BF16), jax.ShapeDtypeStruct((1, W_A), F32),
                   jax.ShapeDtypeStruct((1, W_A), F32), jax.ShapeDtypeStruct((G_A, SGU_BLOCK, SGU_BLOCK), F32),
                   jax.ShapeDtypeStruct((G_A, SGU_BLOCK, 1), F32)],
        scratch_shapes=[pltpu.VMEM((G_A, SGU_BLOCK, SGU_BLOCK), F32)],
        compiler_params=_params("arbitrary"),
    )(z, z, dya, ln_g, ln_b, w_s, b_t)


def _log_sigmoid(z):
    return jnp.minimum(z, 0.0) - jnp.log(1.0 + jnp.exp(-jnp.abs(z)))


def _suffix_sum(x, upper):
    hi = x.astype(BF16)
    lo = (x - hi.astype(F32)).astype(BF16)
    return _dot(hi, upper) + _dot(lo, upper)


SB_ROWS = 2048
_SB_SUB = SB_ROWS // Q_BLOCK


def _sb_upper():
    row = lax.broadcasted_iota(jnp.int32, (Q_BLOCK, Q_BLOCK), 0)
    col = lax.broadcasted_iota(jnp.int32, (Q_BLOCK, Q_BLOCK), 1)
    return (row > col).astype(BF16)


def _sb_sweep(step, tile):
    for r in reversed(range(_SB_SUB)):
        tile(step * _SB_SUB + r, r * Q_BLOCK)

    def group(g, _):
        base = (step - 1 - g) * _SB_SUB
        for r in reversed(range(_SB_SUB)):
            tile(base + r, None)
        return 0

    lax.fori_loop(0, step, group, 0)


def _sb_causal(n):
    return lax.broadcasted_iota(jnp.int32, (n, Q_BLOCK), 1) < lax.broadcasted_iota(jnp.int32, (n, Q_BLOCK), 0)


def _sb_rows_spec():
    return pl.BlockSpec((None, SB_ROWS, DH_B), lambda h, i: (h, i, 0))


def _sb_head_spec(S, part=0):
    return pl.BlockSpec((None, S, DH_B), lambda h, i: (part * H_B + h, 0, 0))


def sb_fwd(qkv, name, comm=None):
    S = qkv.shape[1]

    def body(q_b, k_ref, v_ref, o_ref, c_l1m):
        step = pl.program_id(1)
        o_ref[...] = _zeros(o_ref)
        c_l1m[...] = _zeros(c_l1m)
        upper = _sb_upper()

        def tile(j, row0):
            rq = slice(row0 or 0, SB_ROWS)
            causal = None if row0 is None else _sb_causal(SB_ROWS - row0)
            rows = pl.ds(pl.multiple_of(j * Q_BLOCK, Q_BLOCK), Q_BLOCK)
            zz = _dot_nt(q_b[rq, :], k_ref[rows, :])
            lb = _log_sigmoid(zz)
            l1m = lb - zz
            if causal is not None:
                l1m = jnp.where(causal, l1m, 0.0)
            a = jnp.exp(lb + _suffix_sum(l1m, upper) + c_l1m[rq, :])
            if causal is not None:
                a = jnp.where(causal, a, 0.0)
            o_ref[rq, :] += _dot(a.astype(BF16), v_ref[rows, :])
            c_l1m[rq, :] += jnp.sum(l1m, axis=1, keepdims=True)

        _sb_sweep(step, tile)

    return _pcall(
        comm, body, name=name, grid=(H_B, S // SB_ROWS),
        in_specs=[_sb_rows_spec(), _sb_head_spec(S, 1), _sb_head_spec(S, 2)],
        out_specs=_sb_rows_spec(),
        out_shape=jax.ShapeDtypeStruct((H_B, S, DH_B), F32),
        scratch_shapes=[pltpu.VMEM((SB_ROWS, 1), F32)],
        compiler_params=_params("parallel", "parallel"),
    )(qkv, qkv, qkv)


def sb_bwd(qkv, out, dout, name, comm=None):
    S = qkv.shape[1]
    nstep = S // SB_ROWS
    scale = DH_B ** -0.5

    def body(q_b, k_ref, v_ref, o_ref, do_ref, dq_ref, dk_ref, dv_ref,
             dq_acc, dkt_acc, dvt_acc, do_b, qt_b, dot_b, g_left, c_l1m):
        step = pl.program_id(1)

        @pl.when(step == 0)
        def _():
            dkt_acc[...] = _zeros(dkt_acc)
            dvt_acc[...] = _zeros(dvt_acc)

        do_b[...] = do_ref[...].astype(BF16)
        qt_b[...] = q_b[...].astype(F32).T.astype(BF16)
        dot_b[...] = do_ref[...].T.astype(BF16)
        g_left[...] = jnp.sum(do_b[...].astype(F32) * o_ref[...], axis=1, keepdims=True)
        dq_acc[...] = _zeros(dq_acc)
        c_l1m[...] = _zeros(c_l1m)
        upper = _sb_upper()

        def tile(j, row0):
            rq = slice(row0 or 0, SB_ROWS)
            causal = None if row0 is None else _sb_causal(SB_ROWS - row0)
            rows = pl.ds(pl.multiple_of(j * Q_BLOCK, Q_BLOCK), Q_BLOCK)
            q, do_t = q_b[rq, :], do_b[rq, :]
            k_j = k_ref[rows, :]
            zz = _dot_nt(q, k_j)
            lb = _log_sigmoid(zz)
            l1m = lb - zz
            if causal is not None:
                l1m = jnp.where(causal, l1m, 0.0)
            a = jnp.exp(lb + _suffix_sum(l1m, upper) + c_l1m[rq, :])
            if causal is not None:
                a = jnp.where(causal, a, 0.0)
            a_b = a.astype(BF16)
            dvt_acc[:, rows] += _dot(dot_b[:, rq], a_b)
            gmat = a_b.astype(F32) * _dot_nt(do_t, v_ref[rows, :])
            before = g_left[rq, :] - gmat - _suffix_sum(gmat, upper)
            sig = jnp.exp(lb)
            dz = gmat * (1.0 - sig) - sig * before
            if causal is not None:
                dz = jnp.where(causal, dz, 0.0)
            dz_b = dz.astype(BF16)
            dkt_acc[:, rows] += _dot(qt_b[:, rq], dz_b)
            dq_acc[rq, :] += _dot(dz_b, k_j)
            c_l1m[rq, :] += jnp.sum(l1m, axis=1, keepdims=True)
            g_left[rq, :] -= jnp.sum(gmat, axis=1, keepdims=True)

        _sb_sweep(step, tile)
        dq_ref[...] = (dq_acc[...] * scale).astype(BF16)

        @pl.when(step == nstep - 1)
        def _():
            dk_ref[...] = dkt_acc[...].T.astype(BF16)
            dv_ref[...] = dvt_acc[...].T.astype(BF16)

    out_sds = jax.ShapeDtypeStruct((H_B, S, DH_B), BF16)
    return _pcall(
        comm, body, name=name, grid=(H_B, nstep),
        in_specs=[_sb_rows_spec(), _sb_head_spec(S, 1), _sb_head_spec(S, 2), _sb_rows_spec(), _sb_rows_spec()],
        out_specs=[_sb_rows_spec(), _sb_head_spec(S), _sb_head_spec(S)],
        out_shape=[out_sds, out_sds, out_sds],
        scratch_shapes=[pltpu.VMEM((SB_ROWS, DH_B), F32)] + [pltpu.VMEM((DH_B, S), F32)] * 2
        + [pltpu.VMEM((SB_ROWS, DH_B), BF16)] + [pltpu.VMEM((DH_B, SB_ROWS), BF16)] * 2
        + [pltpu.VMEM((SB_ROWS, 1), F32)] * 2,
        compiler_params=_params("parallel", "arbitrary"),
    )(qkv, qkv, qkv, out, dout)


def _softmax(s):
    e = jnp.exp(s - jnp.max(s, axis=-1, keepdims=True))
    return e / jnp.sum(e, axis=-1, keepdims=True)


def xattn_fwd(qc, kv, name, comm=None):
    S, D = qc.shape
    tr = min(ROW_TILE, S)

    def body(q_ref, kv_ref, o_ref):
        for h in range(X_HEADS):
            cols = slice(h * X_DH, (h + 1) * X_DH)
            p = _softmax(_dot_nt(q_ref[:, cols], kv_ref[:, cols]))
            o_ref[:, cols] = _dot(p.astype(BF16), kv_ref[:, D + h * X_DH:D + (h + 1) * X_DH]).astype(BF16)

    return _pcall(
        comm, body, name=name, grid=(S // tr,),
        in_specs=[_row_spec(tr, D), pl.BlockSpec((N_MEM, 2 * D), lambda i: (0, 0))],
        out_specs=_row_spec(tr, D),
        out_shape=jax.ShapeDtypeStruct((S, D), BF16),
        compiler_params=_params("parallel"),
    )(qc, kv)


def xattn_bwd(qc, kv, do, name, comm=None):
    S, D = qc.shape
    tr = min(ROW_TILE, S)
    nstep = S // tr
    scale = X_DH ** -0.5

    def body(q_ref, kv_ref, do_ref, dq_ref, dkv_ref, acc):
        step = pl.program_id(0)

        @pl.when(step == 0)
        def _():
            acc[...] = _zeros(acc)

        for h in range(X_HEADS):
            cols = slice(h * X_DH, (h + 1) * X_DH)
            vcols = slice(D + h * X_DH, D + (h + 1) * X_DH)
            q = q_ref[:, cols]
            k = kv_ref[:, cols]
            do_h = do_ref[:, cols]
            p = _softmax(_dot_nt(q, k))
            dp = _dot_nt(do_h, kv_ref[:, vcols])
            acc[:, vcols] += _dot_tn(p.astype(BF16), do_h)
            ds = (p * (dp - jnp.sum(p * dp, axis=-1, keepdims=True))).astype(BF16)
            dq_ref[:, cols] = (_dot(ds, k) * scale).astype(BF16)
            acc[:, cols] += _dot_tn(ds, q)

        @pl.when(step == nstep - 1)
        def _():
            dkv_ref[...] = acc[...].astype(BF16)

    whole = pl.BlockSpec((N_MEM, 2 * D), lambda i: (0, 0))
    return _pcall(
        comm, body, name=name, grid=(nstep,),
        in_specs=[_row_spec(tr, D), whole, _row_spec(tr, D)],
        out_specs=[_row_spec(tr, D), whole],
        out_shape=[jax.ShapeDtypeStruct((S, D), BF16), jax.ShapeDtypeStruct((N_MEM, 2 * D), BF16)],
        scratch_shapes=[pltpu.VMEM((N_MEM, 2 * D), F32)],
        compiler_params=_params("arbitrary"),
    )(qc, kv, do)


def _row_tile(rows, cap=128):
    return max(t for t in range(16, cap + 1, 16) if rows % t == 0)


def cast_bf16(ws, name, steps=4, comm=None):
    n = len(ws)

    def body(*refs):
        for i in range(n):
            refs[n + i][...] = refs[i][...].astype(BF16)

    specs = [_row_spec(w.shape[0] // steps, w.shape[1]) for w in ws]
    return _pcall(
        comm, body, name=name, grid=(steps,), in_specs=specs, out_specs=specs,
        out_shape=[jax.ShapeDtypeStruct(w.shape, BF16) for w in ws],
        compiler_params=_params("parallel"),
    )(*ws)


def adamw(parts, w, m, v, name, comm=None):
    R, C = w.shape
    n_parts = parts.shape[0]
    tr = _row_tile(R, 256)
    c1 = 1.0 - ADAM_B1 ** ADAM_STEP
    c2 = 1.0 - ADAM_B2 ** ADAM_STEP

    def body(p_ref, w_ref, m_ref, v_ref, g_ref, d_ref, mo_ref, vo_ref):
        g = p_ref[0].astype(F32)
        for p in range(1, n_parts):
            g = g + p_ref[p].astype(F32)
        m_new = ADAM_B1 * m_ref[...] + (1.0 - ADAM_B1) * g
        v_new = ADAM_B2 * v_ref[...] + (1.0 - ADAM_B2) * (g * g)
        g_ref[...] = g
        mo_ref[...] = m_new
        vo_ref[...] = v_new
        d_ref[...] = -ADAM_LR * ((m_new / c1) / (jnp.sqrt(v_new / c2) + ADAM_EPS) + ADAM_WD * w_ref[...])

    spec = _row_spec(tr, C)
    sds = jax.ShapeDtypeStruct((R, C), F32)
    return _pcall(
        comm, body, name=name, grid=(R // tr,),
        in_specs=[pl.BlockSpec((n_parts, tr, C), lambda i: (0, i, 0)), spec, spec, spec],
        out_specs=[spec, spec, spec, spec],
        out_shape=[sds, sds, sds, sds],
        compiler_params=_params("parallel"),
    )(parts, w, m, v)


def pair_sum(parts, from_sibling, core, name):
    _, R, C = parts.shape
    tr = _row_tile(R, 1024)

    def body(core_ref, p_ref, s_ref, o_ref):
        o_ref[...] = (p_ref[...].astype(F32) + s_ref[...].astype(F32)).astype(o_ref.dtype)

    return pl.pallas_call(
        body, name=name,
        grid_spec=pltpu.PrefetchScalarGridSpec(
            num_scalar_prefetch=1, grid=(4, R // tr),
            in_specs=[pl.BlockSpec((None, tr, C), lambda q, i, core_ref: (2 * q + core_ref[0], i, 0)),
                      pl.BlockSpec((None, tr, C), lambda q, i, core_ref: (q, i, 0))],
            out_specs=pl.BlockSpec((None, tr, C), lambda q, i, core_ref: (q, i, 0))),
        out_shape=jax.ShapeDtypeStruct((4, R, C), BF16),
        compiler_params=_params("parallel", "parallel"),
    )(core, parts, from_sibling)


def add2(a, b, name, comm=None):
    R, C = a.shape
    tr = _row_tile(R, 256)

    def body(a_ref, b_ref, o_ref):
        o_ref[...] = a_ref[...] + b_ref[...]

    spec = _row_spec(tr, C)
    return _pcall(
        comm, body, name=name, grid=(R // tr,), in_specs=[spec, spec], out_specs=spec,
        out_shape=jax.ShapeDtypeStruct((R, C), F32), compiler_params=_params("parallel"),
    )(a, b)


def _place():
    return lax.axis_index("x"), lax.axis_index("y"), lax.axis_index("c")


class Comm:
    def __init__(self, partners, arrays, out_shapes, n_remote, n_local, start, finish, aliases=None):
        self.partners = frozenset(partners)
        self.arrays, self.out_shapes = list(arrays), list(out_shapes)
        self.n_remote, self.n_local = n_remote, max(n_local, 1)
        self.start, self.finish = start, finish
        self.aliases = dict(aliases or {})
        self.sizes = [len(self.out_shapes)]

    def sem_shapes(self):
        return [pltpu.SemaphoreType.DMA((self.n_remote,)), pltpu.SemaphoreType.DMA((self.n_remote,)),
                pltpu.SemaphoreType.DMA((self.n_local,))]


class _Shifted:
    def __init__(self, ref, offset):
        self.ref, self.offset = ref, offset

    @property
    def at(self):
        return self

    def __getitem__(self, k):
        return self.ref.at[self.offset + k]


def merge_comms(comms):
    comms = [c for c in comms if c is not None]
    if not comms:
        return None

    def each(method):
        def run(ins, outs, sems):
            i = o = r = l = 0
            for c in comms:
                sub = (_Shifted(sems[0], r), _Shifted(sems[1], r), _Shifted(sems[2], l))
                getattr(c, method)(ins[i:i + len(c.arrays)], outs[o:o + len(c.out_shapes)], sub)
                i, o, r, l = i + len(c.arrays), o + len(c.out_shapes), r + c.n_remote, l + c.n_local
        return run

    aliases, i, o = {}, 0, 0
    for c in comms:
        aliases.update({i + a: o + b for a, b in c.aliases.items()})
        i, o = i + len(c.arrays), o + len(c.out_shapes)
    merged = Comm(frozenset().union(*[c.partners for c in comms]),
                  [a for c in comms for a in c.arrays], [s for c in comms for s in c.out_shapes],
                  sum(c.n_remote for c in comms), sum(c.n_local for c in comms), each("start"), each("finish"), aliases)
    merged.sizes = [len(c.out_shapes) for c in comms]
    return merged


def split_results(comm, results):
    out, i = [], 0
    for n in comm.sizes:
        out.append(list(results[i:i + n]))
        i += n
    return out


def run_comm(comm, name):
    n_in, n_out = len(comm.arrays), len(comm.out_shapes)

    def body(*refs):
        ins, outs, sems = refs[:n_in], refs[n_in:n_in + n_out], refs[n_in + n_out:]
        _handshake(comm.partners)
        comm.start(ins, outs, sems)
        comm.finish(ins, outs, sems)

    return pl.pallas_call(
        body, name=name, in_specs=[ANY] * n_in, out_specs=[ANY] * n_out, out_shape=comm.out_shapes,
        scratch_shapes=comm.sem_shapes(), input_output_aliases=comm.aliases,
        compiler_params=pltpu.CompilerParams(collective_id=_COLLECTIVE_ID[comm.partners]),
    )(*comm.arrays)


def _remote(src, dst, sems, k, to):
    return pltpu.make_async_remote_copy(src_ref=src, dst_ref=dst, send_sem=sems[0].at[k], recv_sem=sems[1].at[k],
                                        device_id=to, device_id_type=MESH)


_AG_PIECES = 4
_AG_COPIES = 1 + 6 * _AG_PIECES


def comm_all_gather(shards, rows=None, into=None):
    n = len(shards)
    row0, nrows = rows if rows is not None else (0, None)

    def parties():
        x, y, c = _place()
        return (x, y, c), (x, y, 1 - c), [(1 - x, y), (x, 1 - y), (1 - x, 1 - y)]

    def span(w, half=None):
        count = nrows if nrows is not None else shards[w].shape[0]
        if half is None:
            return pl.ds(row0, count)
        return pl.ds(row0 + half * (count // _AG_PIECES), count // _AG_PIECES)

    def slab(outs, w, dev, half=None):
        return outs[w].at[4 * dev[0] + 2 * dev[1] + dev[2], span(w, half)]

    def own(ins, outs, sems):
        me, sibling, chips = parties()
        local = [pltpu.make_async_copy(ins[w].at[span(w)], slab(outs, w, me), sems[2].at[w]) for w in range(n)]
        first = []
        for w in range(n):
            k = _AG_COPIES * w
            first.append(_remote(ins[w].at[span(w)], slab(outs, w, me), sems, k, sibling))
            first += [_remote(ins[w].at[span(w, h)], slab(outs, w, me, h), sems, k + 1 + _AG_PIECES * j + h,
                              (*chip, me[2])) for h in range(_AG_PIECES) for j, chip in enumerate(chips)]
        return local, first

    def start(ins, outs, sems):
        local, first = own(ins, outs, sems)
        for cp in local + first:
            cp.start()

    def finish(ins, outs, sems):
        me, sibling, chips = parties()
        local, first = own(ins, outs, sems)
        passed = []
        for w in range(n):
            k = _AG_COPIES * w
            for h in range(_AG_PIECES):
                for j, chip in enumerate(chips):
                    got = slab(outs, w, (*chip, me[2]), h)
                    _remote(got, got, sems, k + 1 + _AG_PIECES * j + h, me).wait_recv()
                    cp = _remote(got, got, sems, k + 1 + _AG_PIECES * (3 + j) + h, sibling)
                    cp.start()
                    passed.append(cp)
        for w in range(n):
            k = _AG_COPIES * w
            got = slab(outs, w, sibling)
            _remote(got, got, sems, k, me).wait_recv()
            for h in range(_AG_PIECES):
                for j, chip in enumerate(chips):
                    got = slab(outs, w, (*chip, sibling[2]), h)
                    _remote(got, got, sems, k + 1 + _AG_PIECES * (3 + j) + h, me).wait_recv()
        for cp in first + passed:
            cp.wait_send()
        for cp in local:
            cp.wait()

    out_shapes = [jax.ShapeDtypeStruct((N_DEV,) + s.shape, s.dtype) for s in shards]
    arrays = list(shards) + (list(into) if into is not None else [])
    aliases = {n + w: w for w in range(n)} if into is not None else None
    return Comm(("sibling", "chips"), arrays, out_shapes, _AG_COPIES * n, n, start, finish, aliases)


def comm_pairs(items):
    slabbed = [a.ndim == 3 for a in items]
    first = [sum(4 if s else 1 for s in slabbed[:w]) for w in range(len(items))]

    def copies(ins, outs, sems):
        x, y, c = _place()
        sibling = (x, y, 1 - c)
        cps = []
        for w, s in enumerate(slabbed):
            if s:
                cps += [_remote(ins[w].at[2 * q + (1 - c)], outs[w].at[q], sems, first[w] + q, sibling) for q in range(4)]
            else:
                cps.append(_remote(ins[w], outs[w], sems, first[w], sibling))
        return cps

    def start(ins, outs, sems):
        for cp in copies(ins, outs, sems):
            cp.start()

    def finish(ins, outs, sems):
        for cp in copies(ins, outs, sems):
            cp.wait()

    out_shapes = [jax.ShapeDtypeStruct(((4,) + a.shape[1:]) if s else a.shape, a.dtype) for a, s in zip(items, slabbed)]
    return Comm(("sibling",), items, out_shapes, sum(4 if s else 1 for s in slabbed), 0, start, finish)


def comm_chips(items, rows=None, into=None, from_row=None, out_rows=None):
    n = len(items)
    slabbed = [a.ndim == 3 for a in items]

    def span(w, source=False):
        if rows is None:
            return pl.ds(0, items[w].shape[-2])
        return pl.ds(from_row if source and from_row is not None else rows[0], rows[1])

    def copies(ins, outs, sems):
        x, y, c = _place()
        mine = 2 * x + y
        local = [pltpu.make_async_copy(ins[w].at[mine, span(w, True)] if slabbed[w] else ins[w].at[span(w, True)],
                                       outs[w].at[mine, span(w)], sems[2].at[w]) for w in range(n)]
        remote = []
        for w in range(n):
            for j, (px, py) in enumerate([(1 - x, y), (x, 1 - y), (1 - x, 1 - y)]):
                src = ins[w].at[2 * px + py, span(w, True)] if slabbed[w] else ins[w].at[span(w, True)]
                remote.append(_remote(src, outs[w].at[mine, span(w)], sems, 3 * w + j, (px, py, c)))
        return local, remote

    def start(ins, outs, sems):
        local, remote = copies(ins, outs, sems)
        for cp in local + remote:
            cp.start()

    def finish(ins, outs, sems):
        local, remote = copies(ins, outs, sems)
        for cp in remote + local:
            cp.wait()

    def result(a):
        tall = a.shape[:-2] + (out_rows if out_rows is not None else a.shape[-2], a.shape[-1])
        return jax.ShapeDtypeStruct(tall if a.ndim == 3 else (4,) + tall, a.dtype)

    if into is None:
        return Comm(("chips",), items, [result(a) for a in items], 3 * n, n, start, finish)
    out_shapes = [jax.ShapeDtypeStruct(b.shape, b.dtype) for b in into]
    return Comm(("chips",), list(items) + list(into), out_shapes, 3 * n, n, start, finish, {n + w: w for w in range(n)})


_SMALL = ("ffn1_norm", "mix_norm", "ln_v_gain", "ln_v_bias", "spatial_w", "spatial_b", "gnorm_a", "gnorm_b",
          "cross_norm", "mem_norm", "ffn2_norm", "final_norm")
_BIG = ("ffn1_w_in", "ffn1_w_out", "w_mix_in", "w_mix_out", "w_cq", "w_ckv", "w_co", "ffn2_w_in", "ffn2_w_out")
_COL_SHARDED = ("ffn1_w_in", "w_mix_in", "w_ckv", "ffn2_w_in")
_ORDER = ("ffn1_norm", "ffn1_w_in", "ffn1_w_out", "mix_norm", "w_mix_in", "ln_v_gain", "ln_v_bias", "spatial_w",
          "spatial_b", "gnorm_a", "gnorm_b", "w_mix_out", "cross_norm", "mem_norm", "w_cq", "w_ckv", "w_co",
          "ffn2_norm", "ffn2_w_in", "ffn2_w_out", "final_norm")


_SMALL_PAD = 136


def _rows128(a):
    return a.reshape(-1, 128)


def kernel(x, mem, ffn1_norm, ffn1_w_in, ffn1_w_out, mix_norm, w_mix_in, ln_v_gain, ln_v_bias, spatial_w, spatial_b, gnorm_a, gnorm_b, w_mix_out, cross_norm, mem_norm, w_cq, w_ckv, w_co, ffn2_norm, ffn2_w_in, ffn2_w_out, final_norm, loss_target, m_ffn1_norm, m_ffn1_w_in, m_ffn1_w_out, m_mix_norm, m_w_mix_in, m_ln_v_gain, m_ln_v_bias, m_spatial_w, m_spatial_b, m_gnorm_a, m_gnorm_b, m_w_mix_out, m_cross_norm, m_mem_norm, m_w_cq, m_w_ckv, m_w_co, m_ffn2_norm, m_ffn2_w_in, m_ffn2_w_out, m_final_norm, v_ffn1_norm, v_ffn1_w_in, v_ffn1_w_out, v_mix_norm, v_w_mix_in, v_ln_v_gain, v_ln_v_bias, v_spatial_w, v_spatial_b, v_gnorm_a, v_gnorm_b, v_w_mix_out, v_cross_norm, v_mem_norm, v_w_cq, v_w_ckv, v_w_co, v_ffn2_norm, v_ffn2_w_in, v_ffn2_w_out, v_final_norm):
    given = dict(locals())
    wts = {k: given[k] for k in _ORDER}
    mom = {k: given["m_" + k] for k in _ORDER}
    var = {k: given["v_" + k] for k in _ORDER}

    D = D_MODEL
    xs = x.reshape(-1, D)
    mems = mem.reshape(-1, D)
    tgt = loss_target.reshape(-1, D)
    vec = lambda a: a.reshape(1, -1)
    g1, gmix, gcross, gmem, g2, gfin = (vec(wts[k]) for k in
                                        ("ffn1_norm", "mix_norm", "cross_norm", "mem_norm", "ffn2_norm", "final_norm"))
    ln_g, ln_b, ga, gb = (vec(wts[k]) for k in ("ln_v_gain", "ln_v_bias", "gnorm_a", "gnorm_b"))
    w_s = spatial_w.reshape(G_A, SGU_BLOCK, SGU_BLOCK)
    b_t = spatial_b.reshape(G_A, SGU_BLOCK).T

    shard2d = {k: wts[k].reshape(wts[k].shape[1:]) for k in _BIG}
    (first_b,) = cast_bf16([shard2d[_BIG[0]]], "cast_first")
    shard_b = {_BIG[0]: first_b}
    full = {}

    landing, rows_done = {}, {}

    def gathering(pieces, fn, *args, **kw):
        pieces = [p if isinstance(p, tuple) else (p, None) for p in pieces]
        comm = merge_comms([comm_all_gather([shard_b[k]], rows, [landing[k]] if k in landing else None)
                            for k, rows in pieces])
        out, got = fn(*args, comm=comm, **kw)
        for (k, rows), (g,) in zip(pieces, split_results(comm, got)):
            landing[k] = g
            rows_done[k] = rows_done.get(k, 0) + (rows[1] if rows is not None else shard_b[k].shape[0])
            if rows_done[k] == shard_b[k].shape[0]:
                full[k] = g if k in _COL_SHARDED else g.reshape(-1, g.shape[2])
        return out

    shard_b.update(zip(_BIG[1:], gathering((_BIG[0],), cast_bf16, [shard2d[k] for k in _BIG[1:]], "cast_rest")))
    n1 = rms_fwd(xs, g1, "f_n1")
    a1, hsw1 = gathering(("ffn1_w_out",), mm_swiglu_g, n1, full["ffn1_w_in"], "f_a1")
    h1 = gathering(("w_mix_in",), mm_nn, hsw1, full["ffn1_w_out"], F32, "f_h1", scale=0.5, res=xs)
    n2 = gathering(("w_cq",), rms_fwd, h1, gmix, "f_n2")
    z = gathering(("w_mix_out", "w_co"), mm_nn_g, n2, full["w_mix_in"], F32, "f_z")
    ya = gathering((("w_ckv", (0, 512)),), sgu_fwd, z, ln_g, ln_b, w_s, b_t, "f_sgu")
    qkv = split_heads(z, "f_qkv")
    yb = gathering((("w_ckv", (512, 1536)), ("ffn2_w_in", (0, 512))), sb_fwd, qkv, "f_sb")
    ycat = gathering((("ffn2_w_in", (512, 128)),), rmscat_fwd, ya, yb, ga, gb, "f_ycat")
    h2 = gathering((("ffn2_w_in", (640, 256)),), mm_nn, ycat, full["w_mix_out"], F32, "f_h2", res=h1)
    n3 = gathering((("ffn2_w_in", (896, 128)),), rms_fwd, h2, gcross, "f_n3")
    memn = rms_fwd(mems, gmem, "f_memn")
    qc = gathering((("ffn2_w_in", (1024, 256)),), mm_nn, n3, full["w_cq"], BF16, "f_qc", scale=X_DH ** -0.5)
    kv = gathering((("ffn2_w_in", (1280, 128)),), mm_nn_g, memn, full["w_ckv"], BF16, "f_kv")
    o = gathering((("ffn2_w_in", (1408, 128)),), xattn_fwd, qc, kv, "f_xattn")
    h3 = gathering((("ffn2_w_in", (1536, 256)),), mm_nn, o, full["w_co"], F32, "f_h3", res=h2)
    n4 = gathering((("ffn2_w_in", (1792, 256)),), rms_fwd, h3, g2, "f_n4")
    a2, hsw2 = gathering(("ffn2_w_out",), mm_swiglu_g, n4, full["ffn2_w_in"], "f_a2")
    h4 = mm_nn(hsw2, full["ffn2_w_out"], F32, "f_h4", scale=0.5, res=h3)

    grads, parts, sums, recv = {}, {}, {}, {}
    core = lax.axis_index("c").astype(jnp.int32).reshape(1)

    def partial_of(k, g):
        grads[k] = g
        parts[k] = g if g.ndim == 3 else g.reshape(N_DEV, -1, g.shape[1])

    def reducing(pairs, chips, fn, *args, also=None, **kw):
        def piece(p):
            if isinstance(p, dict):
                return p
            k, rows = p if isinstance(p, tuple) else (p, None)
            return dict(sums=k, rows=rows, to=k)

        chips = [piece(p) for p in chips]
        comms = [comm_pairs([parts[k] for k in pairs])] if pairs else []
        comms += [comm_chips([sums[p["sums"]]], p["rows"], [recv[p["to"]]] if p["to"] in recv else None,
                             p.get("from_row"), p.get("out_rows")) for p in chips]
        comm = merge_comms(comms + ([also] if also is not None else []))
        out, got = fn(*args, comm=comm, **kw)
        got = split_results(comm, got)
        if pairs:
            for k, r in zip(pairs, got.pop(0)):
                sums[k] = pair_sum(parts[k], r, core, f"pair_sum_{k}")
        for p, (r,) in zip(chips, got):
            recv[p["to"]] = r
        return out if also is None else (out, got[-1])

    loss_part, dh4, df2, grads["final_norm"] = loss_head(h4, tgt, gfin, "loss_head")
    partial_of("ffn2_w_out", mm_tn(hsw2, df2, "b_ffn2_dwout"))
    da2 = reducing(("ffn2_w_out",), (), mm_swiglu_bwd, df2, full["ffn2_w_out"], a2, "b_ffn2_da")
    partial_of("ffn2_w_in", reducing((), (("ffn2_w_out", (0, 352)),), mm_tn_g, n4, da2, N_DEV, "b_ffn2_dwin"))
    dn4 = reducing(("ffn2_w_in",), (("ffn2_w_out", (352, 352)),), mm_nt_g, da2, full["ffn2_w_in"], "b_ffn2_dn")
    dh3, dh3b, grads["ffn2_norm"] = rms_bwd(dn4, h3, g2, dh4, 1.0, "b_n4")

    partial_of("w_co", mm_tn(o, dh3b, "b_dwco"))
    do = reducing(("w_co",), (("ffn2_w_in", (0, 256)),), mm_nt, dh3b, full["w_co"], BF16, "b_do")
    dqp, dkv = reducing((), (("ffn2_w_in", (256, 256)),), xattn_bwd, qc, kv, do, "b_xattn")
    partial_of("w_cq", mm_tn(n3, dqp, "b_dwcq"))
    dn3 = reducing(("w_cq",), (("ffn2_w_in", (512, 256)),), mm_nt, dqp, full["w_cq"], F32, "b_dn3")
    partial_of("w_ckv", mm_tn_g(memn, dkv, N_DEV, "b_dwckv"))
    dmemn = reducing(("w_ckv",), (("ffn2_w_in", (768, 128)),), mm_nt_g, dkv, full["w_ckv"], "b_dmemn")
    _, _, grads["mem_norm"] = rms_bwd(dmemn, mems, gmem, None, 1.0, "b_memn")
    dh2, dh2b, grads["cross_norm"] = rms_bwd(dn3, h2, gcross, dh3, 1.0, "b_n3")

    partial_of("w_mix_out", mm_tn(ycat, dh2b, "b_dwmixout"))
    dycat = reducing(("w_mix_out",), (("ffn2_w_in", (896, 256)),), mm_nt, dh2b, full["w_mix_out"], F32, "b_dycat")
    dya, dyb, grads["gnorm_a"], grads["gnorm_b"] = rmscat_bwd(dycat, ya, yb, ga, gb, "b_ycat")
    dza, grads["ln_v_gain"], grads["ln_v_bias"], grads["spatial_w"], grads["spatial_b"] = reducing(
        (), (("ffn2_w_in", (1152, 384)),), sgu_bwd, z, dya, ln_g, ln_b, w_s, b_t, "b_sgu")
    dq, dk, dv = reducing((), (("ffn2_w_in", (1536, 512)), "w_co", "w_cq", ("w_ckv", (0, 512))), sb_bwd, qkv, yb,
                          dyb, "b_sb")
    dz = join_dz(dza, dq, dk, dv, "b_dz")
    partial_of("w_mix_in", reducing((), (("w_ckv", (512, 1536)),), mm_tn_g, n2, dz, N_DEV, "b_dwmixin"))
    dn2 = reducing(("w_mix_in",), ("w_mix_out",), mm_nt_g, dz, full["w_mix_in"], "b_dn2")
    dh1, dh1b, grads["mix_norm"] = reducing((), (("w_mix_in", (0, 512)),), rms_bwd, dn2, h1, gmix, dh2, 0.5, "b_n2")

    pad = jnp.zeros((_SMALL_PAD, 128), F32)
    small_early = jnp.concatenate([_rows128(grads[k]) for k in _SMALL[1:]] + [pad], axis=0)
    g_w1out, (early_sibling,) = reducing((), (("w_mix_in", (512, 1024)),), mm_tn, hsw1, dh1b, "b_ffn1_dwout",
                                         also=comm_pairs([small_early]))
    partial_of("ffn1_w_out", g_w1out)
    early_pair = add2(small_early, early_sibling, "pair_sum_small_early")
    da1, (early_all,) = reducing(("ffn1_w_out",), (("w_mix_in", (1536, 512)),), mm_swiglu_bwd, dh1b,
                                 full["ffn1_w_out"], a1, "b_ffn1_da", also=comm_chips([early_pair]))
    half = D // 2
    partial_of("ffn1_w_in_a", reducing((), (("ffn1_w_out", (0, 352)),), mm_tn_g, n1, da1, N_DEV, "b_ffn1_dwin_a",
                                       cols=(0, half)))
    partial_of("ffn1_w_in_b", reducing(("ffn1_w_in_a",), (("ffn1_w_out", (352, 352)),), mm_tn_g, n1, da1, N_DEV,
                                       "b_ffn1_dwin_b", cols=(half, half)))
    dn1 = reducing(("ffn1_w_in_b",), (dict(sums="ffn1_w_in_a", rows=(0, half), to="ffn1_w_in", out_rows=D),),
                   mm_nt_g, da1, full["ffn1_w_in"], "b_ffn1_dn")
    (dx, _, grads["ffn1_norm"]) = reducing((), (dict(sums="ffn1_w_in_b", rows=(half, half), from_row=0, to="ffn1_w_in"),),
                                           rms_bwd, dn1, xs, g1, dh1, 1.0, "b_n1")

    out_g, out_d, out_m, out_v = {}, {}, {}, {}
    for k in _BIG:
        res = adamw(recv[k], shard2d[k], mom[k].reshape(shard2d[k].shape), var[k].reshape(shard2d[k].shape), f"adamw_{k}")
        out_g[k], out_d[k], out_m[k], out_v[k] = (t.reshape(wts[k].shape) for t in res)

    small_late = _rows128(grads[_SMALL[0]])
    (late_sibling,) = run_comm(comm_pairs([small_late]), "comm_pairs_small_late")
    late_pair = add2(small_late, late_sibling, "pair_sum_small_late")
    (late_all,) = run_comm(comm_chips([late_pair]), "comm_chips_small_late")
    small_all = jnp.concatenate([late_all, early_all], axis=1)
    pack = lambda d: jnp.concatenate([_rows128(d[k]) for k in _SMALL] + [pad], axis=0)
    res = adamw(small_all, pack(wts), pack(mom), pack(var), "adamw_small")
    row = 0
    for k in _SMALL:
        nrow = wts[k].size // 128
        for dst, t in zip((out_g, out_d, out_m, out_v), res):
            dst[k] = t[row:row + nrow].reshape(wts[k].shape)
        row += nrow

    loss = lax.psum(loss_part[0, 0], ("x", "y", "c"))
    grad_x = dx.reshape(x.shape)
    return (loss, grad_x, *[out_g[k] for k in _ORDER], *[out_d[k] for k in _ORDER],
            *[out_m[k] for k in _ORDER], *[out_v[k] for k in _ORDER])
```

```python
import functools
import math

import jax
import jax.numpy as jnp
from jax import lax
from jax.experimental import pallas as pl
from jax.experimental.pallas import tpu as pltpu

F32 = jnp.float32
BF16 = jnp.bfloat16

N_DEV = 8
D_MODEL = 2048
D_FF = 5632
W_A = 1024
G_A = 8
GA_DIM = 128
SGU_BLOCK = 128
CHUNK = 64
H_B = 8
DH_B = 128
Q_BLOCK = 128
X_HEADS = 4
X_DH = 512
N_MEM = 256
EPS = 1e-6

ADAM_LR = 0.001
ADAM_B1 = 0.9
ADAM_B2 = 0.999
ADAM_EPS = 1e-08
ADAM_WD = 0.01
ADAM_STEP = 10

VMEM_LIMIT = 56 * 2**20
ROW_TILE = 256

MESH = pl.DeviceIdType.MESH
ANY = pl.BlockSpec(memory_space=pl.ANY)

_NT = (((1,), (1,)), ((), ()))
_TN = (((0,), (0,)), ((), ()))


def _params(*sem, collective_id=None):
    return pltpu.CompilerParams(dimension_semantics=sem, vmem_limit_bytes=VMEM_LIMIT, collective_id=collective_id)


_COLLECTIVE_ID = {frozenset({"sibling"}): 0, frozenset({"chips"}): 1, frozenset({"sibling", "chips"}): 2}


def _handshake(partners):
    x, y, c = lax.axis_index("x"), lax.axis_index("y"), lax.axis_index("c")
    peers = [(x, y, 1 - c)] if "sibling" in partners else []
    if "chips" in partners:
        peers += [(1 - x, y, c), (x, 1 - y, c), (1 - x, 1 - y, c)]
    barrier = pltpu.get_barrier_semaphore()
    for peer in peers:
        pl.semaphore_signal(barrier, inc=1, device_id=peer, device_id_type=MESH)
    pl.semaphore_wait(barrier, len(peers))


def _zeros(ref):
    return jnp.zeros(ref.shape, ref.dtype)


def _pcall(comm, body, *, name, grid, in_specs, out_specs, out_shape, compiler_params, scratch_shapes=()):
    if comm is None:
        return pl.pallas_call(body, name=name, grid=grid, in_specs=in_specs, out_specs=out_specs, out_shape=out_shape,
                              scratch_shapes=list(scratch_shapes), compiler_params=compiler_params)
    multi = isinstance(out_shape, (list, tuple))
    out_shapes = list(out_shape) if multi else [out_shape]
    out_specs_l = list(out_specs) if multi else [out_specs]
    n_in, n_out, n_scr = len(in_specs), len(out_shapes), len(scratch_shapes)
    n_cin, n_cout = len(comm.arrays), len(comm.out_shapes)

    def with_comm(*refs):
        ins, refs = refs[:n_in], refs[n_in:]
        cins, refs = refs[:n_cin], refs[n_cin:]
        outs, refs = refs[:n_out], refs[n_out:]
        couts, refs = refs[:n_cout], refs[n_cout:]
        scr, sems = refs[:n_scr], refs[n_scr:]
        first = functools.reduce(jnp.logical_and, [pl.program_id(a) == 0 for a in range(len(grid))])
        last = functools.reduce(jnp.logical_and, [pl.program_id(a) == grid[a] - 1 for a in range(len(grid))])
        @pl.when(first)
        def _():
            _handshake(comm.partners)
            comm.start(cins, couts, sems)

        body(*ins, *outs, *scr)
        pl.when(last)(lambda: comm.finish(cins, couts, sems))

    call = pl.pallas_call(
        with_comm, name=name, grid=grid, in_specs=list(in_specs) + [ANY] * n_cin,
        out_specs=out_specs_l + [ANY] * n_cout, out_shape=out_shapes + comm.out_shapes,
        scratch_shapes=list(scratch_shapes) + comm.sem_shapes(),
        compiler_params=_params(*(("arbitrary",) * len(grid)), collective_id=_COLLECTIVE_ID[comm.partners]),
        input_output_aliases={n_in + i: n_out + j for i, j in comm.aliases.items()})

    def run(*args):
        res = call(*args, *comm.arrays)
        main = res[:n_out]
        return (list(main) if multi else main[0]), list(res[n_out:])

    return run


def _dot(a, b):
    return jnp.dot(a, b, preferred_element_type=F32)


def _dot_nt(a, b):
    return lax.dot_general(a, b, _NT, preferred_element_type=F32)


def _dot_tn(a, b):
    return lax.dot_general(a, b, _TN, preferred_element_type=F32)


def mm_nn_g(a, bg, out_dtype, name, tm=512, comm=None):
    M, K = a.shape
    G, _, n = bg.shape
    tm = min(tm, M)

    def body(a_ref, b_ref, o_ref):
        o_ref[...] = _dot(a_ref[...], b_ref[...]).astype(o_ref.dtype)

    return _pcall(
        comm, body, name=name, grid=(G, M // tm),
        in_specs=[pl.BlockSpec((tm, K), lambda g, m: (m, 0)),
                  pl.BlockSpec((None, K, n), lambda g, m: (g, 0, 0))],
        out_specs=pl.BlockSpec((tm, n), lambda g, m: (m, g)),
        out_shape=jax.ShapeDtypeStruct((M, G * n), out_dtype),
        compiler_params=_params("parallel", "parallel"),
    )(a, bg)


def mm_swiglu_g(a, bg, name, tm=512, comm=None):
    M, K = a.shape
    G, _, n = bg.shape
    half = G // 2
    tm = min(tm, M)

    def body(a_ref, bgate_ref, bup_ref, gu_ref, h_ref):
        av = a_ref[...]
        gate = _dot(av, bgate_ref[...])
        up = _dot(av, bup_ref[...])
        gu_ref[0] = gate.astype(BF16)
        gu_ref[1] = up.astype(BF16)
        h_ref[...] = (gate * _sigmoid(gate) * up).astype(BF16)

    return _pcall(
        comm, body, name=name, grid=(half, M // tm),
        in_specs=[pl.BlockSpec((tm, K), lambda p, m: (m, 0)),
                  pl.BlockSpec((None, K, n), lambda p, m: (p, 0, 0)),
                  pl.BlockSpec((None, K, n), lambda p, m: (p + half, 0, 0))],
        out_specs=[pl.BlockSpec((2, tm, n), lambda p, m: (0, m, p)), pl.BlockSpec((tm, n), lambda p, m: (m, p))],
        out_shape=[jax.ShapeDtypeStruct((2, M, half * n), BF16), jax.ShapeDtypeStruct((M, half * n), BF16)],
        compiler_params=_params("parallel", "parallel"),
    )(a, bg, bg)


def mm_swiglu_bwd(dy, w_out, gate_up, name, tk=512, comm=None):
    M, N = dy.shape
    F = w_out.shape[0]

    def body(dy_ref, w_ref, gu_ref, o_ref):
        dh = _dot_nt(dy_ref[...], w_ref[...])
        gt = gu_ref[0].astype(F32)
        up = gu_ref[1].astype(F32)
        sg = _sigmoid(gt)
        o_ref[0] = (dh * up * (sg * (1.0 + gt * (1.0 - sg)))).astype(BF16)
        o_ref[1] = (dh * (gt * sg)).astype(BF16)

    planes = pl.BlockSpec((2, M, tk), lambda k: (0, 0, k))
    return _pcall(
        comm, body, name=name, grid=(F // tk,),
        in_specs=[pl.BlockSpec((M, N), lambda k: (0, 0)), pl.BlockSpec((tk, N), lambda k: (k, 0)), planes],
        out_specs=planes,
        out_shape=jax.ShapeDtypeStruct((2, M, F), BF16),
        compiler_params=_params("parallel"),
    )(dy, w_out, gate_up)


def _shard_cols_spec(dy, G, rows, index):
    if dy.ndim == 2:
        n = dy.shape[1] // G
        return pl.BlockSpec((rows, n), lambda *ids: index(*ids)), n
    half = G // 2
    n = dy.shape[2] // half

    def planes(*ids):
        r, g = index(*ids)
        return g // half, r, g % half

    return pl.BlockSpec((None, rows, n), planes), n


def mm_nn(a, b, out_dtype, name, tm=512, tn=1024, scale=1.0, res=None, comm=None):
    M, K = a.shape
    _, N = b.shape
    tm, tn = min(tm, M), min(tn, N)

    def body(*refs):
        if res is None:
            a_ref, b_ref, o_ref = refs
            acc = _dot(a_ref[...], b_ref[...])
            o_ref[...] = (acc * scale if scale != 1.0 else acc).astype(o_ref.dtype)
        else:
            a_ref, b_ref, r_ref, o_ref = refs
            o_ref[...] = (r_ref[...] + scale * _dot(a_ref[...], b_ref[...])).astype(o_ref.dtype)

    in_specs = [pl.BlockSpec((tm, K), lambda n, m: (m, 0)),
                pl.BlockSpec((K, tn), lambda n, m: (0, n))]
    args = [a, b]
    if res is not None:
        in_specs.append(pl.BlockSpec((tm, tn), lambda n, m: (m, n)))
        args.append(res)
    return _pcall(
        comm, body, name=name, grid=(N // tn, M // tm),
        in_specs=in_specs,
        out_specs=pl.BlockSpec((tm, tn), lambda n, m: (m, n)),
        out_shape=jax.ShapeDtypeStruct((M, N), out_dtype),
        compiler_params=_params("parallel", "parallel"),
    )(*args)


def mm_nt_g(dy, bg, name, tm=1024, comm=None):
    M = dy.shape[-2]
    G, K, n = bg.shape
    tm = min(tm, M)
    dy_spec, _ = _shard_cols_spec(dy, G, tm, lambda m, g: (m, g))

    def body(dy_ref, b_ref, o_ref):
        part = _dot_nt(dy_ref[...], b_ref[...])

        @pl.when(pl.program_id(1) == 0)
        def _():
            o_ref[...] = part

        @pl.when(pl.program_id(1) > 0)
        def _():
            o_ref[...] += part

    return _pcall(
        comm, body, name=name, grid=(M // tm, G),
        in_specs=[dy_spec, pl.BlockSpec((None, K, n), lambda m, g: (g, 0, 0))],
        out_specs=pl.BlockSpec((tm, K), lambda m, g: (m, 0)),
        out_shape=jax.ShapeDtypeStruct((M, K), F32),
        compiler_params=_params("parallel", "arbitrary"),
    )(dy, bg)


def mm_nt(dy, b, out_dtype, name, tk=512, comm=None):
    M, N = dy.shape
    K, _ = b.shape

    def body(dy_ref, b_ref, o_ref):
        o_ref[...] = _dot_nt(dy_ref[...], b_ref[...]).astype(o_ref.dtype)

    return _pcall(
        comm, body, name=name, grid=(K // tk,),
        in_specs=[pl.BlockSpec((M, N), lambda k: (0, 0)),
                  pl.BlockSpec((tk, N), lambda k: (k, 0))],
        out_specs=pl.BlockSpec((M, tk), lambda k: (0, k)),
        out_shape=jax.ShapeDtypeStruct((M, K), out_dtype),
        compiler_params=_params("parallel"),
    )(dy, b)


def mm_tn_g(x, dy, G, name, tk=512, cols=None, comm=None):
    M, K = x.shape
    k0, K = cols if cols is not None else (0, K)
    dy_spec, n = _shard_cols_spec(dy, G, M, lambda g, k: (0, g))

    def body(x_ref, dy_ref, o_ref):
        o_ref[...] = _dot_tn(x_ref[...], dy_ref[...]).astype(o_ref.dtype)

    return _pcall(
        comm, body, name=name, grid=(G, K // tk),
        in_specs=[pl.BlockSpec((M, tk), lambda g, k: (0, k + k0 // tk)), dy_spec],
        out_specs=pl.BlockSpec((None, tk, n), lambda g, k: (g, k, 0)),
        out_shape=jax.ShapeDtypeStruct((G, K, n), BF16),
        compiler_params=_params("parallel", "parallel"),
    )(x, dy)


def mm_tn(x, dy, name, tk=512, comm=None):
    M, K = x.shape
    _, N = dy.shape

    def body(x_ref, dy_ref, o_ref):
        o_ref[...] = _dot_tn(x_ref[...], dy_ref[...]).astype(o_ref.dtype)

    return _pcall(
        comm, body, name=name, grid=(K // tk,),
        in_specs=[pl.BlockSpec((M, tk), lambda k: (0, k)),
                  pl.BlockSpec((M, N), lambda k: (0, 0))],
        out_specs=pl.BlockSpec((tk, N), lambda k: (k, 0)),
        out_shape=jax.ShapeDtypeStruct((K, N), BF16),
        compiler_params=_params("parallel"),
    )(x, dy)


def _rstd(x):
    return lax.rsqrt(jnp.mean(x * x, axis=-1, keepdims=True) + EPS)


def _rms_bwd(dn, xhat, r, g):
    dxhat = dn * g
    return r * (dxhat - xhat * jnp.mean(dxhat * xhat, axis=-1, keepdims=True))


def _row_spec(tr, width, col=0):
    return pl.BlockSpec((tr, width), lambda i: (i, col))


def _vec_spec(width):
    return pl.BlockSpec((1, width), lambda i: (0, 0))


def _heads_spec(tr):
    return pl.BlockSpec((H_B, tr, DH_B), lambda i: (0, i, 0))


def _heads_to_cols(ref):
    return jnp.concatenate([ref[h] for h in range(H_B)], axis=1)


def split_heads(z, name, comm=None):
    S = z.shape[0]
    tr = min(ROW_TILE, S)
    width = H_B * DH_B
    first = 2 * W_A // width

    def body(q_ref, k_ref, v_ref, o_ref):
        for p, (ref, scale) in enumerate(((q_ref, DH_B ** -0.5), (k_ref, 1.0), (v_ref, 1.0))):
            for h in range(H_B):
                cols = ref[:, h * DH_B:(h + 1) * DH_B]
                o_ref[p * H_B + h] = (cols * scale if scale != 1.0 else cols).astype(BF16)

    return _pcall(
        comm, body, name=name, grid=(S // tr,),
        in_specs=[_row_spec(tr, width, first), _row_spec(tr, width, first + 1), _row_spec(tr, width, first + 2)],
        out_specs=pl.BlockSpec((3 * H_B, tr, DH_B), lambda i: (0, i, 0)),
        out_shape=jax.ShapeDtypeStruct((3 * H_B, S, DH_B), BF16),
        compiler_params=_params("parallel"),
    )(z, z, z)


def join_dz(dza, dq, dk, dv, name, comm=None):
    S, wa = dza.shape
    tr = min(ROW_TILE, S)
    width = H_B * DH_B

    def body(dza_ref, dq_ref, dk_ref, dv_ref, o_ref):
        o_ref[:, :wa] = dza_ref[...]
        for p, ref in enumerate((dq_ref, dk_ref, dv_ref)):
            for h in range(H_B):
                lo = wa + p * width + h * DH_B
                o_ref[:, lo:lo + DH_B] = ref[h]

    return _pcall(
        comm, body, name=name, grid=(S // tr,),
        in_specs=[_row_spec(tr, wa), _heads_spec(tr), _heads_spec(tr), _heads_spec(tr)],
        out_specs=_row_spec(tr, wa + 3 * width),
        out_shape=jax.ShapeDtypeStruct((S, wa + 3 * width), BF16),
        compiler_params=_params("parallel"),
    )(dza, dq, dk, dv)


def rms_fwd(x, g, name, comm=None):
    M, D = x.shape
    tr = min(ROW_TILE, M)

    def body(x_ref, g_ref, o_ref):
        xv = x_ref[...]
        o_ref[...] = (xv * _rstd(xv) * g_ref[...]).astype(o_ref.dtype)

    return _pcall(
        comm, body, name=name, grid=(M // tr,),
        in_specs=[_row_spec(tr, D), _vec_spec(D)],
        out_specs=_row_spec(tr, D),
        out_shape=jax.ShapeDtypeStruct((M, D), BF16),
        compiler_params=_params("parallel"),
    )(x, g)


def rms_bwd(dn, h, g, dres, copy_scale, name, comm=None):
    M, D = h.shape
    tr = min(ROW_TILE, M)
    has_res = dres is not None

    def body(*refs):
        if has_res:
            dn_ref, h_ref, g_ref, dres_ref, dh_ref, dhb_ref, dg_ref = refs
        else:
            dn_ref, h_ref, g_ref, dh_ref, dhb_ref, dg_ref = refs
        hv = h_ref[...]
        r = _rstd(hv)
        xhat = hv * r
        dn = dn_ref[...]
        part = jnp.sum(dn * xhat, axis=0, keepdims=True)

        @pl.when(pl.program_id(0) == 0)
        def _():
            dg_ref[...] = part

        @pl.when(pl.program_id(0) > 0)
        def _():
            dg_ref[...] += part

        dh = _rms_bwd(dn, xhat, r, g_ref[...])
        if has_res:
            dh = dh + dres_ref[...]
        dh_ref[...] = dh
        dhb_ref[...] = (dh * copy_scale if copy_scale != 1.0 else dh).astype(BF16)

    in_specs = [_row_spec(tr, D), _row_spec(tr, D), _vec_spec(D)]
    args = [dn, h, g]
    if has_res:
        in_specs.append(_row_spec(tr, D))
        args.append(dres)
    return _pcall(
        comm, body, name=name, grid=(M // tr,),
        in_specs=in_specs,
        out_specs=[_row_spec(tr, D), _row_spec(tr, D), _vec_spec(D)],
        out_shape=[jax.ShapeDtypeStruct((M, D), F32), jax.ShapeDtypeStruct((M, D), BF16),
                   jax.ShapeDtypeStruct((1, D), F32)],
        compiler_params=_params("arbitrary"),
    )(*args)


def _sigmoid(x):
    return 1.0 / (1.0 + jnp.exp(-x))


def rmscat_fwd(ya, yb, ga, gb, name, comm=None):
    M, W = ya.shape
    tr = min(ROW_TILE, M)

    def body(ya_ref, yb_ref, ga_ref, gb_ref, o_ref):
        a = ya_ref[...]
        b = _heads_to_cols(yb_ref)
        o_ref[:, :W] = (a * _rstd(a) * ga_ref[...]).astype(BF16)
        o_ref[:, W:] = (b * _rstd(b) * gb_ref[...]).astype(BF16)

    return _pcall(
        comm, body, name=name, grid=(M // tr,),
        in_specs=[_row_spec(tr, W), _heads_spec(tr), _vec_spec(W), _vec_spec(W)],
        out_specs=_row_spec(tr, 2 * W),
        out_shape=jax.ShapeDtypeStruct((M, 2 * W), BF16),
        compiler_params=_params("parallel"),
    )(ya, yb, ga, gb)


def rmscat_bwd(dycat, ya, yb, ga, gb, name, comm=None):
    M, W = ya.shape
    tr = min(ROW_TILE, M)

    def body(dc_ref, ya_ref, yb_ref, ga_ref, gb_ref, dya_ref, dyb_ref, dga_ref, dgb_ref):
        first = pl.program_id(0) == 0
        for by_head, y_ref, g_ref, dy_ref, dg_ref, lo in ((False, ya_ref, ga_ref, dya_ref, dga_ref, 0),
                                                          (True, yb_ref, gb_ref, dyb_ref, dgb_ref, W)):
            yv = _heads_to_cols(y_ref) if by_head else y_ref[...]
            r = _rstd(yv)
            xhat = yv * r
            dn = dc_ref[:, lo:lo + W]
            part = jnp.sum(dn * xhat, axis=0, keepdims=True)

            @pl.when(first)
            def _():
                dg_ref[...] = part

            @pl.when(jnp.logical_not(first))
            def _():
                dg_ref[...] += part

            dy = _rms_bwd(dn, xhat, r, g_ref[...])
            if by_head:
                for h in range(H_B):
                    dy_ref[h] = dy[:, h * DH_B:(h + 1) * DH_B]
            else:
                dy_ref[...] = dy

    return _pcall(
        comm, body, name=name, grid=(M // tr,),
        in_specs=[_row_spec(tr, 2 * W), _row_spec(tr, W), _heads_spec(tr), _vec_spec(W), _vec_spec(W)],
        out_specs=[_row_spec(tr, W), _heads_spec(tr), _vec_spec(W), _vec_spec(W)],
        out_shape=[jax.ShapeDtypeStruct((M, W), F32), jax.ShapeDtypeStruct((H_B, M, DH_B), F32),
                   jax.ShapeDtypeStruct((1, W), F32), jax.ShapeDtypeStruct((1, W), F32)],
        compiler_params=_params("arbitrary"),
    )(dycat, ya, yb, ga, gb)


def loss_head(h, target, g, name, comm=None):
    M, D = h.shape
    tr = min(ROW_TILE, M)

    def body(h_ref, t_ref, g_ref, loss_ref, dh_ref, dhb_ref, dg_ref):
        hv = h_ref[...]
        gv = g_ref[...]
        r = _rstd(hv)
        xhat = hv * r
        err = xhat * gv - t_ref[...]
        lsum = jnp.sum(jnp.sum(err * err, axis=1, keepdims=True), axis=0, keepdims=True) * (0.5 / D)
        dy = err * (1.0 / D)
        part = jnp.sum(dy * xhat, axis=0, keepdims=True)

        @pl.when(pl.program_id(0) == 0)
        def _():
            dg_ref[...] = part
            loss_ref[...] = _zeros(loss_ref) + lsum

        @pl.when(pl.program_id(0) > 0)
        def _():
            dg_ref[...] += part
            loss_ref[...] += lsum

        dh = _rms_bwd(dy, xhat, r, gv)
        dh_ref[...] = dh
        dhb_ref[...] = (0.5 * dh).astype(BF16)

    return _pcall(
        comm, body, name=name, grid=(M // tr,),
        in_specs=[_row_spec(tr, D), _row_spec(tr, D), _vec_spec(D)],
        out_specs=[pl.BlockSpec((8, 128), lambda i: (0, 0)), _row_spec(tr, D), _row_spec(tr, D), _vec_spec(D)],
        out_shape=[jax.ShapeDtypeStruct((8, 128), F32), jax.ShapeDtypeStruct((M, D), F32),
                   jax.ShapeDtypeStruct((M, D), BF16), jax.ShapeDtypeStruct((1, D), F32)],
        compiler_params=_params("arbitrary"),
    )(h, target, g)


_GELU_C = math.sqrt(2.0 / math.pi)


def _gelu(x):
    return 0.5 * x * (1.0 + jnp.tanh(_GELU_C * (x + 0.044715 * (x * x * x))))


def _gelu_grad(x):
    t = jnp.tanh(_GELU_C * (x + 0.044715 * (x * x * x)))
    return 0.5 * (1.0 + t) + 0.5 * x * (1.0 - t * t) * (_GELU_C * (1.0 + 3.0 * 0.044715 * (x * x)))


def _sgu_mask():
    t = lax.broadcasted_iota(jnp.int32, (SGU_BLOCK, SGU_BLOCK), 0) // CHUNK
    s = lax.broadcasted_iota(jnp.int32, (SGU_BLOCK, SGU_BLOCK), 1) // CHUNK
    return s <= t


def _layernorm_stats(v):
    mu = jnp.mean(v, axis=-1, keepdims=True)
    cen = v - mu
    rstd = lax.rsqrt(jnp.mean(cen * cen, axis=-1, keepdims=True) + EPS)
    return cen * rstd, rstd


def sgu_fwd(z, ln_g, ln_b, w_s, b_t, name, comm=None):
    S = z.shape[0]

    def body(zu_ref, zv_ref, lg_ref, lb_ref, w_ref, bt_ref, o_ref):
        mask = _sgu_mask()
        for g in range(G_A):
            cols = slice(g * GA_DIM, (g + 1) * GA_DIM)
            u = _gelu(zu_ref[:, cols])
            vhat, _ = _layernorm_stats(_gelu(zv_ref[:, cols]))
            vln = vhat * lg_ref[:, cols] + lb_ref[:, cols]
            w = jnp.where(mask, w_ref[g], 0.0).astype(BF16)
            mixed = _dot(w, vln.astype(BF16)) + bt_ref[:, g:g + 1]
            o_ref[:, cols] = u * mixed

    return _pcall(
        comm, body, name=name, grid=(S // SGU_BLOCK,),
        in_specs=[_row_spec(SGU_BLOCK, W_A, 0), _row_spec(SGU_BLOCK, W_A, 1), _vec_spec(W_A), _vec_spec(W_A),
                  pl.BlockSpec((G_A, SGU_BLOCK, SGU_BLOCK), lambda i: (0, 0, 0)),
                  pl.BlockSpec((SGU_BLOCK, G_A), lambda i: (0, 0))],
        out_specs=_row_spec(SGU_BLOCK, W_A),
        out_shape=jax.ShapeDtypeStruct((S, W_A), F32),
        compiler_params=_params("parallel"),
    )(z, z, ln_g, ln_b, w_s, b_t)


def sgu_bwd(z, dya, ln_g, ln_b, w_s, b_t, name, comm=None):
    S = z.shape[0]
    nblk = S // SGU_BLOCK

    def body(zu_ref, zv_ref, dy_ref, lg_ref, lb_ref, w_ref, bt_ref,
             dz_ref, dlg_ref, dlb_ref, dw_ref, db_ref, dmix_acc):
        step = pl.program_id(0)
        mask = _sgu_mask()

        @pl.when(step == 0)
        def _():
            dlg_ref[...] = _zeros(dlg_ref)
            dlb_ref[...] = _zeros(dlb_ref)
            dw_ref[...] = _zeros(dw_ref)
            dmix_acc[...] = _zeros(dmix_acc)

        for g in range(G_A):
            cols = slice(g * GA_DIM, (g + 1) * GA_DIM)
            zu = zu_ref[:, cols]
            zv = zv_ref[:, cols]
            u = _gelu(zu)
            vhat, rstd = _layernorm_stats(_gelu(zv))
            lg = lg_ref[:, cols]
            vln = (vhat * lg + lb_ref[:, cols]).astype(BF16)
            w = jnp.where(mask, w_ref[g], 0.0)
            mixed = _dot(w.astype(BF16), vln) + bt_ref[:, g:g + 1]
            dy = dy_ref[:, cols]
            du = dy * mixed
            dmixed = dy * u
            dmixed_b = dmixed.astype(BF16)
            dmix_acc[g] += dmixed
            dw_ref[g] += jnp.where(mask, _dot_nt(dmixed_b, vln), 0.0)
            dvln = _dot(w.T.astype(BF16), dmixed_b)
            dlb_ref[:, cols] += jnp.sum(dvln, axis=0, keepdims=True)
            dlg_ref[:, cols] += jnp.sum(dvln * vhat, axis=0, keepdims=True)
            dvhat = dvln * lg
            dv = rstd * (dvhat - jnp.mean(dvhat, axis=-1, keepdims=True)
                         - vhat * jnp.mean(dvhat * vhat, axis=-1, keepdims=True))
            dz_ref[:, cols] = (du * _gelu_grad(zu)).astype(BF16)
            dz_ref[:, W_A + g * GA_DIM:W_A + (g + 1) * GA_DIM] = (dv * _gelu_grad(zv)).astype(BF16)

        @pl.when(step == nblk - 1)
        def _():
            for g in range(G_A):
                db_ref[g] = jnp.sum(dmix_acc[g], axis=1, keepdims=True)

    whole3 = lambda shape: pl.BlockSpec(shape, lambda i: (0, 0, 0))
    return _pcall(
        comm, body, name=name, grid=(nblk,),
        in_specs=[_row_spec(SGU_BLOCK, W_A, 0), _row_spec(SGU_BLOCK, W_A, 1), _row_spec(SGU_BLOCK, W_A),
                  _vec_spec(W_A), _vec_spec(W_A), whole3((G_A, SGU_BLOCK, SGU_BLOCK)),
                  pl.BlockSpec((SGU_BLOCK, G_A), lambda i: (0, 0))],
        out_specs=[_row_spec(SGU_BLOCK, 2 * W_A), _vec_spec(W_A), _vec_spec(W_A),
                   whole3((G_A, SGU_BLOCK, SGU_BLOCK)), whole3((G_A, SGU_BLOCK, 1))],
        out_shape=[jax.ShapeDtypeStruct((S, 2 * W_A), BF16), jax.ShapeDtypeStruct((1, W_A), F32),
                   jax.ShapeDtypeStruct((1, W_A), F32), jax.ShapeDtypeStruct((G_A, SGU_BLOCK, SGU_BLOCK), F32),
                   jax.ShapeDtypeStruct((G_A, SGU_BLOCK, 1), F32)],
        scratch_shapes=[pltpu.VMEM((G_A, SGU_BLOCK, SGU_BLOCK), F32)],
        compiler_params=_params("arbitrary"),
    )(z, z, dya, ln_g, ln_b, w_s, b_t)


def _log_sigmoid(z):
    return jnp.minimum(z, 0.0) - jnp.log(1.0 + jnp.exp(-jnp.abs(z)))


def _suffix_sum(x, upper):
    hi = x.astype(BF16)
    lo = (x - hi.astype(F32)).astype(BF16)
    return _dot(hi, upper) + _dot(lo, upper)


SB_ROWS = 2048
_SB_SUB = SB_ROWS // Q_BLOCK


def _sb_upper():
    row = lax.broadcasted_iota(jnp.int32, (Q_BLOCK, Q_BLOCK), 0)
    col = lax.broadcasted_iota(jnp.int32, (Q_BLOCK, Q_BLOCK), 1)
    return (row > col).astype(BF16)


def _sb_sweep(step, tile):
    for r in reversed(range(_SB_SUB)):
        tile(step * _SB_SUB + r, r * Q_BLOCK)

    def group(g, _):
        base = (step - 1 - g) * _SB_SUB
        for r in reversed(range(_SB_SUB)):
            tile(base + r, None)
        return 0

    lax.fori_loop(0, step, group, 0)


def _sb_causal(n):
    return lax.broadcasted_iota(jnp.int32, (n, Q_BLOCK), 1) < lax.broadcasted_iota(jnp.int32, (n, Q_BLOCK), 0)


def _sb_rows_spec():
    return pl.BlockSpec((None, SB_ROWS, DH_B), lambda h, i: (h, i, 0))


def _sb_head_spec(S, part=0):
    return pl.BlockSpec((None, S, DH_B), lambda h, i: (part * H_B + h, 0, 0))


def sb_fwd(qkv, name, comm=None):
    S = qkv.shape[1]

    def body(q_b, k_ref, v_ref, o_ref, c_l1m):
        step = pl.program_id(1)
        o_ref[...] = _zeros(o_ref)
        c_l1m[...] = _zeros(c_l1m)
        upper = _sb_upper()

        def tile(j, row0):
            rq = slice(row0 or 0, SB_ROWS)
            causal = None if row0 is None else _sb_causal(SB_ROWS - row0)
            rows = pl.ds(pl.multiple_of(j * Q_BLOCK, Q_BLOCK), Q_BLOCK)
            zz = _dot_nt(q_b[rq, :], k_ref[rows, :])
            lb = _log_sigmoid(zz)
            l1m = lb - zz
            if causal is not None:
                l1m = jnp.where(causal, l1m, 0.0)
            a = jnp.exp(lb + _suffix_sum(l1m, upper) + c_l1m[rq, :])
            if causal is not None:
                a = jnp.where(causal, a, 0.0)
            o_ref[rq, :] += _dot(a.astype(BF16), v_ref[rows, :])
            c_l1m[rq, :] += jnp.sum(l1m, axis=1, keepdims=True)

        _sb_sweep(step, tile)

    return _pcall(
        comm, body, name=name, grid=(H_B, S // SB_ROWS),
        in_specs=[_sb_rows_spec(), _sb_head_spec(S, 1), _sb_head_spec(S, 2)],
        out_specs=_sb_rows_spec(),
        out_shape=jax.ShapeDtypeStruct((H_B, S, DH_B), F32),
        scratch_shapes=[pltpu.VMEM((SB_ROWS, 1), F32)],
        compiler_params=_params("parallel", "parallel"),
    )(qkv, qkv, qkv)


def sb_bwd(qkv, out, dout, name, comm=None):
    S = qkv.shape[1]
    nstep = S // SB_ROWS
    scale = DH_B ** -0.5

    def body(q_b, k_ref, v_ref, o_ref, do_ref, dq_ref, dk_ref, dv_ref,
             dq_acc, dkt_acc, dvt_acc, do_b, qt_b, dot_b, g_left, c_l1m):
        step = pl.program_id(1)

        @pl.when(step == 0)
        def _():
            dkt_acc[...] = _zeros(dkt_acc)
            dvt_acc[...] = _zeros(dvt_acc)

        do_b[...] = do_ref[...].astype(BF16)
        qt_b[...] = q_b[...].astype(F32).T.astype(BF16)
        dot_b[...] = do_ref[...].T.astype(BF16)
        g_left[...] = jnp.sum(do_b[...].astype(F32) * o_ref[...], axis=1, keepdims=True)
        dq_acc[...] = _zeros(dq_acc)
        c_l1m[...] = _zeros(c_l1m)
        upper = _sb_upper()

        def tile(j, row0):
            rq = slice(row0 or 0, SB_ROWS)
            causal = None if row0 is None else _sb_causal(SB_ROWS - row0)
            rows = pl.ds(pl.multiple_of(j * Q_BLOCK, Q_BLOCK), Q_BLOCK)
            q, do_t = q_b[rq, :], do_b[rq, :]
            k_j = k_ref[rows, :]
            zz = _dot_nt(q, k_j)
            lb = _log_sigmoid(zz)
            l1m = lb - zz
            if causal is not None:
                l1m = jnp.where(causal, l1m, 0.0)
            a = jnp.exp(lb + _suffix_sum(l1m, upper) + c_l1m[rq, :])
            if causal is not None:
                a = jnp.where(causal, a, 0.0)
            a_b = a.astype(BF16)
            dvt_acc[:, rows] += _dot(dot_b[:, rq], a_b)
            gmat = a_b.astype(F32) * _dot_nt(do_t, v_ref[rows, :])
            before = g_left[rq, :] - gmat - _suffix_sum(gmat, upper)
            sig = jnp.exp(lb)
            dz = gmat * (1.0 - sig) - sig * before
            if causal is not None:
                dz = jnp.where(causal, dz, 0.0)
            dz_b = dz.astype(BF16)
            dkt_acc[:, rows] += _dot(qt_b[:, rq], dz_b)
            dq_acc[rq, :] += _dot(dz_b, k_j)
            c_l1m[rq, :] += jnp.sum(l1m, axis=1, keepdims=True)
            g_left[rq, :] -= jnp.sum(gmat, axis=1, keepdims=True)

        _sb_sweep(step, tile)
        dq_ref[...] = (dq_acc[...] * scale).astype(BF16)

        @pl.when(step == nstep - 1)
        def _():
            dk_ref[...] = dkt_acc[...].T.astype(BF16)
            dv_ref[...] = dvt_acc[...].T.astype(BF16)

    out_sds = jax.ShapeDtypeStruct((H_B, S, DH_B), BF16)
    return _pcall(
        comm, body, name=name, grid=(H_B, nstep),
        in_specs=[_sb_rows_spec(), _sb_head_spec(S, 1), _sb_head_spec(S, 2), _sb_rows_spec(), _sb_rows_spec()],
        out_specs=[_sb_rows_spec(), _sb_head_spec(S), _sb_head_spec(S)],
        out_shape=[out_sds, out_sds, out_sds],
        scratch_shapes=[pltpu.VMEM((SB_ROWS, DH_B), F32)] + [pltpu.VMEM((DH_B, S), F32)] * 2
        + [pltpu.VMEM((SB_ROWS, DH_B), BF16)] + [pltpu.VMEM((DH_B, SB_ROWS), BF16)] * 2
        + [pltpu.VMEM((SB_ROWS, 1), F32)] * 2,
        compiler_params=_params("parallel", "arbitrary"),
    )(qkv, qkv, qkv, out, dout)


def _softmax(s):
    e = jnp.exp(s - jnp.max(s, axis=-1, keepdims=True))
    return e / jnp.sum(e, axis=-1, keepdims=True)


def xattn_fwd(qc, kv, name, comm=None):
    S, D = qc.shape
    tr = min(ROW_TILE, S)

    def body(q_ref, kv_ref, o_ref):
        for h in range(X_HEADS):
            cols = slice(h * X_DH, (h + 1) * X_DH)
            p = _softmax(_dot_nt(q_ref[:, cols], kv_ref[:, cols]))
            o_ref[:, cols] = _dot(p.astype(BF16), kv_ref[:, D + h * X_DH:D + (h + 1) * X_DH]).astype(BF16)

    return _pcall(
        comm, body, name=name, grid=(S // tr,),
        in_specs=[_row_spec(tr, D), pl.BlockSpec((N_MEM, 2 * D), lambda i: (0, 0))],
        out_specs=_row_spec(tr, D),
        out_shape=jax.ShapeDtypeStruct((S, D), BF16),
        compiler_params=_params("parallel"),
    )(qc, kv)


def xattn_bwd(qc, kv, do, name, comm=None):
    S, D = qc.shape
    tr = min(ROW_TILE, S)
    nstep = S // tr
    scale = X_DH ** -0.5

    def body(q_ref, kv_ref, do_ref, dq_ref, dkv_ref, acc):
        step = pl.program_id(0)

        @pl.when(step == 0)
        def _():
            acc[...] = _zeros(acc)

        for h in range(X_HEADS):
            cols = slice(h * X_DH, (h + 1) * X_DH)
            vcols = slice(D + h * X_DH, D + (h + 1) * X_DH)
            q = q_ref[:, cols]
            k = kv_ref[:, cols]
            do_h = do_ref[:, cols]
            p = _softmax(_dot_nt(q, k))
            dp = _dot_nt(do_h, kv_ref[:, vcols])
            acc[:, vcols] += _dot_tn(p.astype(BF16), do_h)
            ds = (p * (dp - jnp.sum(p * dp, axis=-1, keepdims=True))).astype(BF16)
            dq_ref[:, cols] = (_dot(ds, k) * scale).astype(BF16)
            acc[:, cols] += _dot_tn(ds, q)

        @pl.when(step == nstep - 1)
        def _():
            dkv_ref[...] = acc[...].astype(BF16)

    whole = pl.BlockSpec((N_MEM, 2 * D), lambda i: (0, 0))
    return _pcall(
        comm, body, name=name, grid=(nstep,),
        in_specs=[_row_spec(tr, D), whole, _row_spec(tr, D)],
        out_specs=[_row_spec(tr, D), whole],
        out_shape=[jax.ShapeDtypeStruct((S, D), BF16), jax.ShapeDtypeStruct((N_MEM, 2 * D), BF16)],
        scratch_shapes=[pltpu.VMEM((N_MEM, 2 * D), F32)],
        compiler_params=_params("arbitrary"),
    )(qc, kv, do)


def _row_tile(rows, cap=128):
    return max(t for t in range(16, cap + 1, 16) if rows % t == 0)


def cast_bf16(ws, name, steps=4, comm=None):
    n = len(ws)

    def body(*refs):
        for i in range(n):
            refs[n + i][...] = refs[i][...].astype(BF16)

    specs = [_row_spec(w.shape[0] // steps, w.shape[1]) for w in ws]
    return _pcall(
        comm, body, name=name, grid=(steps,), in_specs=specs, out_specs=specs,
        out_shape=[jax.ShapeDtypeStruct(w.shape, BF16) for w in ws],
        compiler_params=_params("parallel"),
    )(*ws)


def adamw(parts, w, m, v, name, comm=None):
    R, C = w.shape
    n_parts = parts.shape[0]
    tr = _row_tile(R, 256)
    c1 = 1.0 - ADAM_B1 ** ADAM_STEP
    c2 = 1.0 - ADAM_B2 ** ADAM_STEP

    def body(p_ref, w_ref, m_ref, v_ref, g_ref, d_ref, mo_ref, vo_ref):
        g = p_ref[0].astype(F32)
        for p in range(1, n_parts):
            g = g + p_ref[p].astype(F32)
        m_new = ADAM_B1 * m_ref[...] + (1.0 - ADAM_B1) * g
        v_new = ADAM_B2 * v_ref[...] + (1.0 - ADAM_B2) * (g * g)
        g_ref[...] = g
        mo_ref[...] = m_new
        vo_ref[...] = v_new
        d_ref[...] = -ADAM_LR * ((m_new / c1) / (jnp.sqrt(v_new / c2) + ADAM_EPS) + ADAM_WD * w_ref[...])

    spec = _row_spec(tr, C)
    sds = jax.ShapeDtypeStruct((R, C), F32)
    return _pcall(
        comm, body, name=name, grid=(R // tr,),
        in_specs=[pl.BlockSpec((n_parts, tr, C), lambda i: (0, i, 0)), spec, spec, spec],
        out_specs=[spec, spec, spec, spec],
        out_shape=[sds, sds, sds, sds],
        compiler_params=_params("parallel"),
    )(parts, w, m, v)


def pair_sum(parts, from_sibling, core, name):
    _, R, C = parts.shape
    tr = _row_tile(R, 1024)

    def body(core_ref, p_ref, s_ref, o_ref):
        o_ref[...] = (p_ref[...].astype(F32) + s_ref[...].astype(F32)).astype(o_ref.dtype)

    return pl.pallas_call(
        body, name=name,
        grid_spec=pltpu.PrefetchScalarGridSpec(
            num_scalar_prefetch=1, grid=(4, R // tr),
            in_specs=[pl.BlockSpec((None, tr, C), lambda q, i, core_ref: (2 * q + core_ref[0], i, 0)),
                      pl.BlockSpec((None, tr, C), lambda q, i, core_ref: (q, i, 0))],
            out_specs=pl.BlockSpec((None, tr, C), lambda q, i, core_ref: (q, i, 0))),
        out_shape=jax.ShapeDtypeStruct((4, R, C), BF16),
        compiler_params=_params("parallel", "parallel"),
    )(core, parts, from_sibling)


def add2(a, b, name, comm=None):
    R, C = a.shape
    tr = _row_tile(R, 256)

    def body(a_ref, b_ref, o_ref):
        o_ref[...] = a_ref[...] + b_ref[...]

    spec = _row_spec(tr, C)
    return _pcall(
        comm, body, name=name, grid=(R // tr,), in_specs=[spec, spec], out_specs=spec,
        out_shape=jax.ShapeDtypeStruct((R, C), F32), compiler_params=_params("parallel"),
    )(a, b)


def _place():
    return lax.axis_index("x"), lax.axis_index("y"), lax.axis_index("c")


class Comm:
    def __init__(self, partners, arrays, out_shapes, n_remote, n_local, start, finish, aliases=None):
        self.partners = frozenset(partners)
        self.arrays, self.out_shapes = list(arrays), list(out_shapes)
        self.n_remote, self.n_local = n_remote, max(n_local, 1)
        self.start, self.finish = start, finish
        self.aliases = dict(aliases or {})
        self.sizes = [len(self.out_shapes)]

    def sem_shapes(self):
        return [pltpu.SemaphoreType.DMA((self.n_remote,)), pltpu.SemaphoreType.DMA((self.n_remote,)),
                pltpu.SemaphoreType.DMA((self.n_local,))]


class _Shifted:
    def __init__(self, ref, offset):
        self.ref, self.offset = ref, offset

    @property
    def at(self):
        return self

    def __getitem__(self, k):
        return self.ref.at[self.offset + k]


def merge_comms(comms):
    comms = [c for c in comms if c is not None]
    if not comms:
        return None

    def each(method):
        def run(ins, outs, sems):
            i = o = r = l = 0
            for c in comms:
                sub = (_Shifted(sems[0], r), _Shifted(sems[1], r), _Shifted(sems[2], l))
                getattr(c, method)(ins[i:i + len(c.arrays)], outs[o:o + len(c.out_shapes)], sub)
                i, o, r, l = i + len(c.arrays), o + len(c.out_shapes), r + c.n_remote, l + c.n_local
        return run

    aliases, i, o = {}, 0, 0
    for c in comms:
        aliases.update({i + a: o + b for a, b in c.aliases.items()})
        i, o = i + len(c.arrays), o + len(c.out_shapes)
    merged = Comm(frozenset().union(*[c.partners for c in comms]),
                  [a for c in comms for a in c.arrays], [s for c in comms for s in c.out_shapes],
                  sum(c.n_remote for c in comms), sum(c.n_local for c in comms), each("start"), each("finish"), aliases)
    merged.sizes = [len(c.out_shapes) for c in comms]
    return merged


def split_results(comm, results):
    out, i = [], 0
    for n in comm.sizes:
        out.append(list(results[i:i + n]))
        i += n
    return out


def run_comm(comm, name):
    n_in, n_out = len(comm.arrays), len(comm.out_shapes)

    def body(*refs):
        ins, outs, sems = refs[:n_in], refs[n_in:n_in + n_out], refs[n_in + n_out:]
        _handshake(comm.partners)
        comm.start(ins, outs, sems)
        comm.finish(ins, outs, sems)

    return pl.pallas_call(
        body, name=name, in_specs=[ANY] * n_in, out_specs=[ANY] * n_out, out_shape=comm.out_shapes,
        scratch_shapes=comm.sem_shapes(), input_output_aliases=comm.aliases,
        compiler_params=pltpu.CompilerParams(collective_id=_COLLECTIVE_ID[comm.partners]),
    )(*comm.arrays)


def _remote(src, dst, sems, k, to):
    return pltpu.make_async_remote_copy(src_ref=src, dst_ref=dst, send_sem=sems[0].at[k], recv_sem=sems[1].at[k],
                                        device_id=to, device_id_type=MESH)


_AG_PIECES = 4
_AG_COPIES = 1 + 6 * _AG_PIECES


def comm_all_gather(shards, rows=None, into=None):
    n = len(shards)
    row0, nrows = rows if rows is not None else (0, None)

    def parties():
        x, y, c = _place()
        return (x, y, c), (x, y, 1 - c), [(1 - x, y), (x, 1 - y), (1 - x, 1 - y)]

    def span(w, half=None):
        count = nrows if nrows is not None else shards[w].shape[0]
        if half is None:
            return pl.ds(row0, count)
        return pl.ds(row0 + half * (count // _AG_PIECES), count // _AG_PIECES)

    def slab(outs, w, dev, half=None):
        return outs[w].at[4 * dev[0] + 2 * dev[1] + dev[2], span(w, half)]

    def own(ins, outs, sems):
        me, sibling, chips = parties()
        local = [pltpu.make_async_copy(ins[w].at[span(w)], slab(outs, w, me), sems[2].at[w]) for w in range(n)]
        first = []
        for w in range(n):
            k = _AG_COPIES * w
            first.append(_remote(ins[w].at[span(w)], slab(outs, w, me), sems, k, sibling))
            first += [_remote(ins[w].at[span(w, h)], slab(outs, w, me, h), sems, k + 1 + _AG_PIECES * j + h,
                              (*chip, me[2])) for h in range(_AG_PIECES) for j, chip in enumerate(chips)]
        return local, first

    def start(ins, outs, sems):
        local, first = own(ins, outs, sems)
        for cp in local + first:
            cp.start()

    def finish(ins, outs, sems):
        me, sibling, chips = parties()
        local, first = own(ins, outs, sems)
        passed = []
        for w in range(n):
            k = _AG_COPIES * w
            for h in range(_AG_PIECES):
                for j, chip in enumerate(chips):
                    got = slab(outs, w, (*chip, me[2]), h)
                    _remote(got, got, sems, k + 1 + _AG_PIECES * j + h, me).wait_recv()
                    cp = _remote(got, got, sems, k + 1 + _AG_PIECES * (3 + j) + h, sibling)
                    cp.start()
                    passed.append(cp)
        for w in range(n):
            k = _AG_COPIES * w
            got = slab(outs, w, sibling)
            _remote(got, got, sems, k, me).wait_recv()
            for h in range(_AG_PIECES):
                for j, chip in enumerate(chips):
                    got = slab(outs, w, (*chip, sibling[2]), h)
                    _remote(got, got, sems, k + 1 + _AG_PIECES * (3 + j) + h, me).wait_recv()
        for cp in first + passed:
            cp.wait_send()
        for cp in local:
            cp.wait()

    out_shapes = [jax.ShapeDtypeStruct((N_DEV,) + s.shape, s.dtype) for s in shards]
    arrays = list(shards) + (list(into) if into is not None else [])
    aliases = {n + w: w for w in range(n)} if into is not None else None
    return Comm(("sibling", "chips"), arrays, out_shapes, _AG_COPIES * n, n, start, finish, aliases)


def comm_pairs(items):
    slabbed = [a.ndim == 3 for a in items]
    first = [sum(4 if s else 1 for s in slabbed[:w]) for w in range(len(items))]

    def copies(ins, outs, sems):
        x, y, c = _place()
        sibling = (x, y, 1 - c)
        cps = []
        for w, s in enumerate(slabbed):
            if s:
                cps += [_remote(ins[w].at[2 * q + (1 - c)], outs[w].at[q], sems, first[w] + q, sibling) for q in range(4)]
            else:
                cps.append(_remote(ins[w], outs[w], sems, first[w], sibling))
        return cps

    def start(ins, outs, sems):
        for cp in copies(ins, outs, sems):
            cp.start()

    def finish(ins, outs, sems):
        for cp in copies(ins, outs, sems):
            cp.wait()

    out_shapes = [jax.ShapeDtypeStruct(((4,) + a.shape[1:]) if s else a.shape, a.dtype) for a, s in zip(items, slabbed)]
    return Comm(("sibling",), items, out_shapes, sum(4 if s else 1 for s in slabbed), 0, start, finish)


def comm_chips(items, rows=None, into=None, from_row=None, out_rows=None):
    n = len(items)
    slabbed = [a.ndim == 3 for a in items]

    def span(w, source=False):
        if rows is None:
            return pl.ds(0, items[w].shape[-2])
        return pl.ds(from_row if source and from_row is not None else rows[0], rows[1])

    def copies(ins, outs, sems):
        x, y, c = _place()
        mine = 2 * x + y
        local = [pltpu.make_async_copy(ins[w].at[mine, span(w, True)] if slabbed[w] else ins[w].at[span(w, True)],
                                       outs[w].at[mine, span(w)], sems[2].at[w]) for w in range(n)]
        remote = []
        for w in range(n):
            for j, (px, py) in enumerate([(1 - x, y), (x, 1 - y), (1 - x, 1 - y)]):
                src = ins[w].at[2 * px + py, span(w, True)] if slabbed[w] else ins[w].at[span(w, True)]
                remote.append(_remote(src, outs[w].at[mine, span(w)], sems, 3 * w + j, (px, py, c)))
        return local, remote

    def start(ins, outs, sems):
        local, remote = copies(ins, outs, sems)
        for cp in local + remote:
            cp.start()

    def finish(ins, outs, sems):
        local, remote = copies(ins, outs, sems)
        for cp in remote + local:
            cp.wait()

    def result(a):
        tall = a.shape[:-2] + (out_rows if out_rows is not None else a.shape[-2], a.shape[-1])
        return jax.ShapeDtypeStruct(tall if a.ndim == 3 else (4,) + tall, a.dtype)

    if into is None:
        return Comm(("chips",), items, [result(a) for a in items], 3 * n, n, start, finish)
    out_shapes = [jax.ShapeDtypeStruct(b.shape, b.dtype) for b in into]
    return Comm(("chips",), list(items) + list(into), out_shapes, 3 * n, n, start, finish, {n + w: w for w in range(n)})


_SMALL = ("ffn1_norm", "mix_norm", "ln_v_gain", "ln_v_bias", "spatial_w", "spatial_b", "gnorm_a", "gnorm_b",
          "cross_norm", "mem_norm", "ffn2_norm", "final_norm")
_BIG = ("ffn1_w_in", "ffn1_w_out", "w_mix_in", "w_mix_out", "w_cq", "w_ckv", "w_co", "ffn2_w_in", "ffn2_w_out")
_COL_SHARDED = ("ffn1_w_in", "w_mix_in", "w_ckv", "ffn2_w_in")
_ORDER = ("ffn1_norm", "ffn1_w_in", "ffn1_w_out", "mix_norm", "w_mix_in", "ln_v_gain", "ln_v_bias", "spatial_w",
          "spatial_b", "gnorm_a", "gnorm_b", "w_mix_out", "cross_norm", "mem_norm", "w_cq", "w_ckv", "w_co",
          "ffn2_norm", "ffn2_w_in", "ffn2_w_out", "final_norm")


_SMALL_PAD = 136


def _rows128(a):
    return a.reshape(-1, 128)


def kernel(x, mem, ffn1_norm, ffn1_w_in, ffn1_w_out, mix_norm, w_mix_in, ln_v_gain, ln_v_bias, spatial_w, spatial_b, gnorm_a, gnorm_b, w_mix_out, cross_norm, mem_norm, w_cq, w_ckv, w_co, ffn2_norm, ffn2_w_in, ffn2_w_out, final_norm, loss_target, m_ffn1_norm, m_ffn1_w_in, m_ffn1_w_out, m_mix_norm, m_w_mix_in, m_ln_v_gain, m_ln_v_bias, m_spatial_w, m_spatial_b, m_gnorm_a, m_gnorm_b, m_w_mix_out, m_cross_norm, m_mem_norm, m_w_cq, m_w_ckv, m_w_co, m_ffn2_norm, m_ffn2_w_in, m_ffn2_w_out, m_final_norm, v_ffn1_norm, v_ffn1_w_in, v_ffn1_w_out, v_mix_norm, v_w_mix_in, v_ln_v_gain, v_ln_v_bias, v_spatial_w, v_spatial_b, v_gnorm_a, v_gnorm_b, v_w_mix_out, v_cross_norm, v_mem_norm, v_w_cq, v_w_ckv, v_w_co, v_ffn2_norm, v_ffn2_w_in, v_ffn2_w_out, v_final_norm):
    given = dict(locals())
    wts = {k: given[k] for k in _ORDER}
    mom = {k: given["m_" + k] for k in _ORDER}
    var = {k: given["v_" + k] for k in _ORDER}

    D = D_MODEL
    xs = x.reshape(-1, D)
    mems = mem.reshape(-1, D)
    tgt = loss_target.reshape(-1, D)
    vec = lambda a: a.reshape(1, -1)
    g1, gmix, gcross, gmem, g2, gfin = (vec(wts[k]) for k in
                                        ("ffn1_norm", "mix_norm", "cross_norm", "mem_norm", "ffn2_norm", "final_norm"))
    ln_g, ln_b, ga, gb = (vec(wts[k]) for k in ("ln_v_gain", "ln_v_bias", "gnorm_a", "gnorm_b"))
    w_s = spatial_w.reshape(G_A, SGU_BLOCK, SGU_BLOCK)
    b_t = spatial_b.reshape(G_A, SGU_BLOCK).T

    shard2d = {k: wts[k].reshape(wts[k].shape[1:]) for k in _BIG}
    (first_b,) = cast_bf16([shard2d[_BIG[0]]], "cast_first")
    shard_b = {_BIG[0]: first_b}
    full = {}

    landing, rows_done = {}, {}

    def gathering(pieces, fn, *args, **kw):
        pieces = [p if isinstance(p, tuple) else (p, None) for p in pieces]
        comm = merge_comms([comm_all_gather([shard_b[k]], rows, [landing[k]] if k in landing else None)
                            for k, rows in pieces])
        out, got = fn(*args, comm=comm, **kw)
        for (k, rows), (g,) in zip(pieces, split_results(comm, got)):
            landing[k] = g
            rows_done[k] = rows_done.get(k, 0) + (rows[1] if rows is not None else shard_b[k].shape[0])
            if rows_done[k] == shard_b[k].shape[0]:
                full[k] = g if k in _COL_SHARDED else g.reshape(-1, g.shape[2])
        return out

    shard_b.update(zip(_BIG[1:], gathering((_BIG[0],), cast_bf16, [shard2d[k] for k in _BIG[1:]], "cast_rest")))
    n1 = rms_fwd(xs, g1, "f_n1")
    a1, hsw1 = gathering(("ffn1_w_out",), mm_swiglu_g, n1, full["ffn1_w_in"], "f_a1")
    h1 = gathering(("w_mix_in",), mm_nn, hsw1, full["ffn1_w_out"], F32, "f_h1", scale=0.5, res=xs)
    n2 = gathering(("w_cq",), rms_fwd, h1, gmix, "f_n2")
    z = gathering(("w_mix_out", "w_co"), mm_nn_g, n2, full["w_mix_in"], F32, "f_z")
    ya = gathering((("w_ckv", (0, 512)),), sgu_fwd, z, ln_g, ln_b, w_s, b_t, "f_sgu")
    qkv = split_heads(z, "f_qkv")
    yb = gathering((("w_ckv", (512, 1536)), ("ffn2_w_in", (0, 512))), sb_fwd, qkv, "f_sb")
    ycat = gathering((("ffn2_w_in", (512, 128)),), rmscat_fwd, ya, yb, ga, gb, "f_ycat")
    h2 = gathering((("ffn2_w_in", (640, 256)),), mm_nn, ycat, full["w_mix_out"], F32, "f_h2", res=h1)
    n3 = gathering((("ffn2_w_in", (896, 128)),), rms_fwd, h2, gcross, "f_n3")
    memn = rms_fwd(mems, gmem, "f_memn")
    qc = gathering((("ffn2_w_in", (1024, 256)),), mm_nn, n3, full["w_cq"], BF16, "f_qc", scale=X_DH ** -0.5)
    kv = gathering((("ffn2_w_in", (1280, 128)),), mm_nn_g, memn, full["w_ckv"], BF16, "f_kv")
    o = gathering((("ffn2_w_in", (1408, 128)),), xattn_fwd, qc, kv, "f_xattn")
    h3 = gathering((("ffn2_w_in", (1536, 256)),), mm_nn, o, full["w_co"], F32, "f_h3", res=h2)
    n4 = gathering((("ffn2_w_in", (1792, 256)),), rms_fwd, h3, g2, "f_n4")
    a2, hsw2 = gathering(("ffn2_w_out",), mm_swiglu_g, n4, full["ffn2_w_in"], "f_a2")
    h4 = mm_nn(hsw2, full["ffn2_w_out"], F32, "f_h4", scale=0.5, res=h3)

    grads, parts, sums, recv = {}, {}, {}, {}
    core = lax.axis_index("c").astype(jnp.int32).reshape(1)

    def partial_of(k, g):
        grads[k] = g
        parts[k] = g if g.ndim == 3 else g.reshape(N_DEV, -1, g.shape[1])

    def reducing(pairs, chips, fn, *args, also=None, **kw):
        def piece(p):
            if isinstance(p, dict):
                return p
            k, rows = p if isinstance(p, tuple) else (p, None)
            return dict(sums=k, rows=rows, to=k)

        chips = [piece(p) for p in chips]
        comms = [comm_pairs([parts[k] for k in pairs])] if pairs else []
        comms += [comm_chips([sums[p["sums"]]], p["rows"], [recv[p["to"]]] if p["to"] in recv else None,
                             p.get("from_row"), p.get("out_rows")) for p in chips]
        comm = merge_comms(comms + ([also] if also is not None else []))
        out, got = fn(*args, comm=comm, **kw)
        got = split_results(comm, got)
        if pairs:
            for k, r in zip(pairs, got.pop(0)):
                sums[k] = pair_sum(parts[k], r, core, f"pair_sum_{k}")
        for p, (r,) in zip(chips, got):
            recv[p["to"]] = r
        return out if also is None else (out, got[-1])

    loss_part, dh4, df2, grads["final_norm"] = loss_head(h4, tgt, gfin, "loss_head")
    partial_of("ffn2_w_out", mm_tn(hsw2, df2, "b_ffn2_dwout"))
    da2 = reducing(("ffn2_w_out",), (), mm_swiglu_bwd, df2, full["ffn2_w_out"], a2, "b_ffn2_da")
    partial_of("ffn2_w_in", reducing((), (("ffn2_w_out", (0, 352)),), mm_tn_g, n4, da2, N_DEV, "b_ffn2_dwin"))
    dn4 = reducing(("ffn2_w_in",), (("ffn2_w_out", (352, 352)),), mm_nt_g, da2, full["ffn2_w_in"], "b_ffn2_dn")
    dh3, dh3b, grads["ffn2_norm"] = rms_bwd(dn4, h3, g2, dh4, 1.0, "b_n4")

    partial_of("w_co", mm_tn(o, dh3b, "b_dwco"))
    do = reducing(("w_co",), (("ffn2_w_in", (0, 256)),), mm_nt, dh3b, full["w_co"], BF16, "b_do")
    dqp, dkv = reducing((), (("ffn2_w_in", (256, 256)),), xattn_bwd, qc, kv, do, "b_xattn")
    partial_of("w_cq", mm_tn(n3, dqp, "b_dwcq"))
    dn3 = reducing(("w_cq",), (("ffn2_w_in", (512, 256)),), mm_nt, dqp, full["w_cq"], F32, "b_dn3")
    partial_of("w_ckv", mm_tn_g(memn, dkv, N_DEV, "b_dwckv"))
    dmemn = reducing(("w_ckv",), (("ffn2_w_in", (768, 128)),), mm_nt_g, dkv, full["w_ckv"], "b_dmemn")
    _, _, grads["mem_norm"] = rms_bwd(dmemn, mems, gmem, None, 1.0, "b_memn")
    dh2, dh2b, grads["cross_norm"] = rms_bwd(dn3, h2, gcross, dh3, 1.0, "b_n3")

    partial_of("w_mix_out", mm_tn(ycat, dh2b, "b_dwmixout"))
    dycat = reducing(("w_mix_out",), (("ffn2_w_in", (896, 256)),), mm_nt, dh2b, full["w_mix_out"], F32, "b_dycat")
    dya, dyb, grads["gnorm_a"], grads["gnorm_b"] = rmscat_bwd(dycat, ya, yb, ga, gb, "b_ycat")
    dza, grads["ln_v_gain"], grads["ln_v_bias"], grads["spatial_w"], grads["spatial_b"] = reducing(
        (), (("ffn2_w_in", (1152, 384)),), sgu_bwd, z, dya, ln_g, ln_b, w_s, b_t, "b_sgu")
    dq, dk, dv = reducing((), (("ffn2_w_in", (1536, 512)), "w_co", "w_cq", ("w_ckv", (0, 512))), sb_bwd, qkv, yb,
                          dyb, "b_sb")
    dz = join_dz(dza, dq, dk, dv, "b_dz")
    partial_of("w_mix_in", reducing((), (("w_ckv", (512, 1536)),), mm_tn_g, n2, dz, N_DEV, "b_dwmixin"))
    dn2 = reducing(("w_mix_in",), ("w_mix_out",), mm_nt_g, dz, full["w_mix_in"], "b_dn2")
    dh1, dh1b, grads["mix_norm"] = reducing((), (("w_mix_in", (0, 512)),), rms_bwd, dn2, h1, gmix, dh2, 0.5, "b_n2")

    pad = jnp.zeros((_SMALL_PAD, 128), F32)
    small_early = jnp.concatenate([_rows128(grads[k]) for k in _SMALL[1:]] + [pad], axis=0)
    g_w1out, (early_sibling,) = reducing((), (("w_mix_in", (512, 1024)),), mm_tn, hsw1, dh1b, "b_ffn1_dwout",
                                         also=comm_pairs([small_early]))
    partial_of("ffn1_w_out", g_w1out)
    early_pair = add2(small_early, early_sibling, "pair_sum_small_early")
    da1, (early_all,) = reducing(("ffn1_w_out",), (("w_mix_in", (1536, 512)),), mm_swiglu_bwd, dh1b,
                                 full["ffn1_w_out"], a1, "b_ffn1_da", also=comm_chips([early_pair]))
    half = D // 2
    partial_of("ffn1_w_in_a", reducing((), (("ffn1_w_out", (0, 352)),), mm_tn_g, n1, da1, N_DEV, "b_ffn1_dwin_a",
                                       cols=(0, half)))
    partial_of("ffn1_w_in_b", reducing(("ffn1_w_in_a",), (("ffn1_w_out", (352, 352)),), mm_tn_g, n1, da1, N_DEV,
                                       "b_ffn1_dwin_b", cols=(half, half)))
    dn1 = reducing(("ffn1_w_in_b",), (dict(sums="ffn1_w_in_a", rows=(0, half), to="ffn1_w_in", out_rows=D),),
                   mm_nt_g, da1, full["ffn1_w_in"], "b_ffn1_dn")
    (dx, _, grads["ffn1_norm"]) = reducing((), (dict(sums="ffn1_w_in_b", rows=(half, half), from_row=0, to="ffn1_w_in"),),
                                           rms_bwd, dn1, xs, g1, dh1, 1.0, "b_n1")

    out_g, out_d, out_m, out_v = {}, {}, {}, {}
    for k in _BIG:
        res = adamw(recv[k], shard2d[k], mom[k].reshape(shard2d[k].shape), var[k].reshape(shard2d[k].shape), f"adamw_{k}")
        out_g[k], out_d[k], out_m[k], out_v[k] = (t.reshape(wts[k].shape) for t in res)

    small_late = _rows128(grads[_SMALL[0]])
    (late_sibling,) = run_comm(comm_pairs([small_late]), "comm_pairs_small_late")
    late_pair = add2(small_late, late_sibling, "pair_sum_small_late")
    (late_all,) = run_comm(comm_chips([late_pair]), "comm_chips_small_late")
    small_all = jnp.concatenate([late_all, early_all], axis=1)
    pack = lambda d: jnp.concatenate([_rows128(d[k]) for k in _SMALL] + [pad], axis=0)
    res = adamw(small_all, pack(wts), pack(mom), pack(var), "adamw_small")
    row = 0
    for k in _SMALL:
        nrow = wts[k].size // 128
        for dst, t in zip((out_g, out_d, out_m, out_v), res):
            dst[k] = t[row:row + nrow].reshape(wts[k].shape)
        row += nrow

    loss = lax.psum(loss_part[0, 0], ("x", "y", "c"))
    grad_x = dx.reshape(x.shape)
    return (loss, grad_x, *[out_g[k] for k in _ORDER], *[out_d[k] for k in _ORDER],
            *[out_m[k] for k in _ORDER], *[out_v[k] for k in _ORDER])
```
